```python
import jax, jax.numpy as jnp
from jax import lax
import numpy as np

D_MODEL = 2048
BATCH = 8
SEQ = 2048
DEPTH = 1

MLA_HEADS = 8
MLA_Q_RANK = 512
MLA_KV_RANK = 512
MLA_NOPE_DIM = 128
MLA_ROPE_DIM = 64
MLA_V_DIM = 128
MLA_WIDTH = MLA_HEADS * MLA_V_DIM
MLA_IN = MLA_Q_RANK + MLA_KV_RANK + MLA_ROPE_DIM
ROPE_THETA = 10000.0
Q_BLOCK = 128

RWKV_HEAD = 64
RWKV_HEADS = 16
RWKV_WIDTH = RWKV_HEADS * RWKV_HEAD
DECAY_RANK = 96
ICLR_RANK = 96
RWKV_SIZES = [RWKV_WIDTH, RWKV_WIDTH, RWKV_WIDTH, DECAY_RANK, DECAY_RANK, ICLR_RANK, ICLR_RANK]
RWKV_IN = sum(RWKV_SIZES)
GN_EPS = 64e-5
NORM_EPS = 1e-6

IN_SIZES = [MLA_IN, RWKV_IN, MLA_WIDTH, RWKV_WIDTH, D_MODEL, D_MODEL]
D_IN = sum(IN_SIZES)

kernel_name = "hybrid_mla_rwkv7_gated_encoder_block"


def _offsets(sizes):
    return [int(o) for o in np.cumsum(sizes)[:-1]]


def rms_norm(x, g, eps=NORM_EPS):
    xf = x.astype(jnp.float32)
    y = xf * lax.rsqrt(jnp.mean(xf * xf, axis=-1, keepdims=True) + eps)
    return (y * g.astype(jnp.float32)).astype(x.dtype)


def apply_rotary(t, cos, sin):
    tf = t.astype(jnp.float32)
    t1, t2 = jnp.split(tf, 2, axis=-1)
    return jnp.concatenate([t1 * cos - t2 * sin, t1 * sin + t2 * cos], axis=-1).astype(t.dtype)


def centred_shift(p):
    pad = jnp.pad(p, ((0, 0), (1, 1), (0, 0)))
    return 0.5 * (pad[:, :-2] + pad[:, 2:])


def mla_attention(q_nope, q_rope, k_nope, k_rope, v):
    B, S, H, _ = q_nope.shape
    nb = S // Q_BLOCK
    scale = (MLA_NOPE_DIM + MLA_ROPE_DIM) ** -0.5
    qn = q_nope.reshape(B, nb, Q_BLOCK, H, MLA_NOPE_DIM).transpose(1, 0, 2, 3, 4)
    qr = q_rope.reshape(B, nb, Q_BLOCK, H, MLA_ROPE_DIM).transpose(1, 0, 2, 3, 4)

    def block(args):
        qn_b, qr_b = args
        s = (jnp.einsum('bqhd,bkhd->bhqk', qn_b, k_nope, preferred_element_type=jnp.float32)
             + jnp.einsum('bqhr,bkr->bhqk', qr_b, k_rope, preferred_element_type=jnp.float32))
        p = jax.nn.softmax(s * scale, axis=-1)
        return jnp.einsum('bhqk,bkhd->bqhd', p.astype(v.dtype), v)

    o = lax.map(block, (qn, qr))
    return o.transpose(1, 0, 2, 3, 4).reshape(B, S, H * MLA_V_DIM)


def rwkv7_bidir_scan(r, w_f, w_b, k_f, k_b, v, kk, a_f, a_b):
    B, S, H, N = r.shape

    def tm(fwd, bwd):
        return jnp.stack([fwd, bwd[:, ::-1]], axis=0).transpose(2, 0, 1, 3, 4)

    xs = (tm(r, r), tm(w_f, w_b), tm(k_f, k_b), tm(v, v), tm(-kk, -kk), tm(kk * a_f, kk * a_b))

    def step(st, inp):
        r_t, w_t, k_t, v_t, a_t, b_t = inp
        sa = jnp.einsum('dbhij,dbhj->dbhi', st, a_t)
        st = st * w_t[..., None, :] + sa[..., None] * b_t[..., None, :] + v_t[..., None] * k_t[..., None, :]
        y = jnp.einsum('dbhij,dbhj->dbhi', st, r_t)
        return st, y

    s0 = jnp.zeros((2, B, H, N, N), jnp.float32)
    _, ys = lax.scan(step, s0, xs)
    y = ys[:, 0] + ys[::-1, 1]
    return y.transpose(1, 0, 2, 3)


def _fwd_setup_inputs(seed: int = 0) -> dict:
    key = jax.random.key(seed)
    ks = jax.random.split(key, 32)
    f32 = jnp.float32
    nrm = lambda k, shape, s: jax.random.normal(k, shape, f32) * s
    gain = lambda k, n: 1.0 + nrm(k, (n,), 0.02)
    return {
        "x": nrm(ks[0], (BATCH, SEQ, D_MODEL), 1.0),
        "g_pre": gain(ks[1], D_MODEL),
        "w_in": nrm(ks[2], (D_MODEL, D_IN), D_MODEL ** -0.5),
        "mla_q_norm": gain(ks[3], MLA_Q_RANK),
        "mla_wq_b": nrm(ks[4], (MLA_Q_RANK, MLA_HEADS * (MLA_NOPE_DIM + MLA_ROPE_DIM)), MLA_Q_RANK ** -0.5),
        "mla_kv_norm": gain(ks[5], MLA_KV_RANK),
        "mla_wkv_b": nrm(ks[6], (MLA_KV_RANK, MLA_HEADS * (MLA_NOPE_DIM + MLA_V_DIM)), MLA_KV_RANK ** -0.5),
        "rwkv_mu": jax.random.uniform(ks[7], (RWKV_IN,), f32),
        "rwkv_w0_f": nrm(ks[8], (RWKV_WIDTH,), 0.5),
        "rwkv_w2_f": nrm(ks[9], (DECAY_RANK, RWKV_WIDTH), 0.5 * DECAY_RANK ** -0.5),
        "rwkv_w0_b": nrm(ks[10], (RWKV_WIDTH,), 0.5),
        "rwkv_w2_b": nrm(ks[11], (DECAY_RANK, RWKV_WIDTH), 0.5 * DECAY_RANK ** -0.5),
        "rwkv_a0_f": nrm(ks[12], (RWKV_WIDTH,), 0.1),
        "rwkv_a2_f": nrm(ks[13], (ICLR_RANK, RWKV_WIDTH), 0.5 * ICLR_RANK ** -0.5),
        "rwkv_a0_b": nrm(ks[14], (RWKV_WIDTH,), 0.1),
        "rwkv_a2_b": nrm(ks[15], (ICLR_RANK, RWKV_WIDTH), 0.5 * ICLR_RANK ** -0.5),
        "rwkv_k_k": 0.85 + nrm(ks[16], (RWKV_WIDTH,), 0.02),
        "rwkv_k_a": gain(ks[17], RWKV_WIDTH),
        "rwkv_r_k": nrm(ks[18], (RWKV_HEADS, RWKV_HEAD), 0.1),
        "rwkv_gn_g": gain(ks[19], RWKV_WIDTH),
        "rwkv_gn_b": nrm(ks[20], (RWKV_WIDTH,), 0.01),
        "w_br_mla": nrm(ks[21], (MLA_WIDTH, D_MODEL), MLA_WIDTH ** -0.5),
        "w_br_rwkv": nrm(ks[22], (RWKV_WIDTH, D_MODEL), RWKV_WIDTH ** -0.5),
        "w_out": nrm(ks[23], (D_MODEL, D_MODEL), D_MODEL ** -0.5),
        "g_post": gain(ks[24], D_MODEL),
    }


def _fwd_reference(x, g_pre, w_in, mla_q_norm, mla_wq_b, mla_kv_norm, mla_wkv_b, rwkv_mu,
              rwkv_w0_f, rwkv_w2_f, rwkv_w0_b, rwkv_w2_b, rwkv_a0_f, rwkv_a2_f, rwkv_a0_b,
              rwkv_a2_b, rwkv_k_k, rwkv_k_a, rwkv_r_k, rwkv_gn_g, rwkv_gn_b, w_br_mla,
              w_br_rwkv, w_out, g_post):
    f32 = jnp.float32
    B, S, _ = x.shape
    pos = jnp.arange(S, dtype=f32)
    inv_freq = jnp.power(ROPE_THETA, -jnp.arange(0, MLA_ROPE_DIM, 2, dtype=f32) / MLA_ROPE_DIM)
    ang = pos[:, None] * inv_freq[None, :]
    cos, sin = jnp.cos(ang), jnp.sin(ang)

    for _layer in range(DEPTH):
        h = rms_norm(x, g_pre)
        proj = h @ w_in
        mla_in, rwkv_in, z_mla, z_rwkv, gate_mla, gate_rwkv = jnp.split(proj, _offsets(IN_SIZES), axis=-1)

        q_a, kv_a, k_rope = jnp.split(mla_in, _offsets([MLA_Q_RANK, MLA_KV_RANK, MLA_ROPE_DIM]), axis=-1)
        q = (rms_norm(q_a, mla_q_norm) @ mla_wq_b).reshape(B, S, MLA_HEADS, MLA_NOPE_DIM + MLA_ROPE_DIM)
        kv = (rms_norm(kv_a, mla_kv_norm) @ mla_wkv_b).reshape(B, S, MLA_HEADS, MLA_NOPE_DIM + MLA_V_DIM)
        q_nope, q_rope = q[..., :MLA_NOPE_DIM], q[..., MLA_NOPE_DIM:]
        k_nope, v_mla = kv[..., :MLA_NOPE_DIM], kv[..., MLA_NOPE_DIM:]
        q_rope = apply_rotary(q_rope, cos[:, None, :], sin[:, None, :])
        k_rope = apply_rotary(k_rope, cos, sin)
        y_mla = mla_attention(q_nope, q_rope, k_nope, k_rope, v_mla)

        rin = rwkv_in.astype(f32)
        rin = rin + rwkv_mu * (centred_shift(rin) - rin)
        r, k, v, wd_f, wd_b, ad_f, ad_b = jnp.split(rin, _offsets(RWKV_SIZES), axis=-1)

        def decay(w0, wd, w2):
            z = w0 + jnp.tanh(wd) @ w2
            return jnp.exp(-jnp.exp(-jax.nn.softplus(-z) - 0.5))

        w_f = decay(rwkv_w0_f.astype(f32), wd_f, rwkv_w2_f.astype(f32))
        w_b = decay(rwkv_w0_b.astype(f32), wd_b, rwkv_w2_b.astype(f32))
        a_f = jax.nn.sigmoid(rwkv_a0_f.astype(f32) + ad_f @ rwkv_a2_f.astype(f32))
        a_b = jax.nn.sigmoid(rwkv_a0_b.astype(f32) + ad_b @ rwkv_a2_b.astype(f32))

        hd = lambda t: t.reshape(B, S, RWKV_HEADS, RWKV_HEAD)
        kk = hd(k * rwkv_k_k.astype(f32))
        kk = kk / jnp.maximum(jnp.linalg.norm(kk, axis=-1, keepdims=True), 1e-12)
        k_a = rwkv_k_a.astype(f32)
        k_f = k * (1.0 + (a_f - 1.0) * k_a)
        k_b = k * (1.0 + (a_b - 1.0) * k_a)
        r_h, v_h, k_fh, k_bh = hd(r), hd(v), hd(k_f), hd(k_b)
        y = rwkv7_bidir_scan(r_h, hd(w_f), hd(w_b), k_fh, k_bh, v_h, kk, hd(a_f), hd(a_b))
        mu = jnp.mean(y, axis=-1, keepdims=True)
        var = jnp.mean(jnp.square(y - mu), axis=-1, keepdims=True)
        yn = ((y - mu) * lax.rsqrt(var + GN_EPS)).reshape(B, S, RWKV_WIDTH)
        yn = yn * rwkv_gn_g.astype(f32) + rwkv_gn_b.astype(f32)
        bonus = jnp.sum(r_h * (k_fh + k_bh) * rwkv_r_k.astype(f32), axis=-1, keepdims=True) * v_h
        y_rwkv = (yn + bonus.reshape(B, S, RWKV_WIDTH)).astype(x.dtype)

        u_mla = (y_mla * jax.nn.silu(z_mla)) @ w_br_mla
        u_rwkv = (y_rwkv * jax.nn.silu(z_rwkv)) @ w_br_rwkv
        merged = jax.nn.sigmoid(gate_mla) * u_mla + jax.nn.sigmoid(gate_rwkv) * u_rwkv
        out = merged @ w_out
        x = (x + rms_norm(out, g_post)).astype(x.dtype)
    return x


import jax as _jax
import jax.numpy as _jnp

TWIN_FORMAT = 'train_step'
FWD_PARAMS = ['x', 'g_pre', 'w_in', 'mla_q_norm', 'mla_wq_b', 'mla_kv_norm', 'mla_wkv_b', 'rwkv_mu', 'rwkv_w0_f', 'rwkv_w2_f', 'rwkv_w0_b', 'rwkv_w2_b', 'rwkv_a0_f', 'rwkv_a2_f', 'rwkv_a0_b', 'rwkv_a2_b', 'rwkv_k_k', 'rwkv_k_a', 'rwkv_r_k', 'rwkv_gn_g', 'rwkv_gn_b', 'w_br_mla', 'w_br_rwkv', 'w_out', 'g_post']
TWIN_WEIGHTS = ['g_pre', 'w_in', 'mla_q_norm', 'mla_wq_b', 'mla_kv_norm', 'mla_wkv_b', 'rwkv_mu', 'rwkv_w0_f', 'rwkv_w2_f', 'rwkv_w0_b', 'rwkv_w2_b', 'rwkv_a0_f', 'rwkv_a2_f', 'rwkv_a0_b', 'rwkv_a2_b', 'rwkv_k_k', 'rwkv_k_a', 'rwkv_r_k', 'rwkv_gn_g', 'rwkv_gn_b', 'w_br_mla', 'w_br_rwkv', 'w_out', 'g_post']
TWIN_DIFF_INPUT = 'x'
TWIN_INPUTS = ['x', 'g_pre', 'w_in', 'mla_q_norm', 'mla_wq_b', 'mla_kv_norm', 'mla_wkv_b', 'rwkv_mu', 'rwkv_w0_f', 'rwkv_w2_f', 'rwkv_w0_b', 'rwkv_w2_b', 'rwkv_a0_f', 'rwkv_a2_f', 'rwkv_a0_b', 'rwkv_a2_b', 'rwkv_k_k', 'rwkv_k_a', 'rwkv_r_k', 'rwkv_gn_g', 'rwkv_gn_b', 'w_br_mla', 'w_br_rwkv', 'w_out', 'g_post', 'loss_target', 'm_g_pre', 'm_w_in', 'm_mla_q_norm', 'm_mla_wq_b', 'm_mla_kv_norm', 'm_mla_wkv_b', 'm_rwkv_mu', 'm_rwkv_w0_f', 'm_rwkv_w2_f', 'm_rwkv_w0_b', 'm_rwkv_w2_b', 'm_rwkv_a0_f', 'm_rwkv_a2_f', 'm_rwkv_a0_b', 'm_rwkv_a2_b', 'm_rwkv_k_k', 'm_rwkv_k_a', 'm_rwkv_r_k', 'm_rwkv_gn_g', 'm_rwkv_gn_b', 'm_w_br_mla', 'm_w_br_rwkv', 'm_w_out', 'm_g_post', 'v_g_pre', 'v_w_in', 'v_mla_q_norm', 'v_mla_wq_b', 'v_mla_kv_norm', 'v_mla_wkv_b', 'v_rwkv_mu', 'v_rwkv_w0_f', 'v_rwkv_w2_f', 'v_rwkv_w0_b', 'v_rwkv_w2_b', 'v_rwkv_a0_f', 'v_rwkv_a2_f', 'v_rwkv_a0_b', 'v_rwkv_a2_b', 'v_rwkv_k_k', 'v_rwkv_k_a', 'v_rwkv_r_k', 'v_rwkv_gn_g', 'v_rwkv_gn_b', 'v_w_br_mla', 'v_w_br_rwkv', 'v_w_out', 'v_g_post']
TWIN_OUTPUTS = ['loss', 'grad_x', 'grad_g_pre', 'grad_w_in', 'grad_mla_q_norm', 'grad_mla_wq_b', 'grad_mla_kv_norm', 'grad_mla_wkv_b', 'grad_rwkv_mu', 'grad_rwkv_w0_f', 'grad_rwkv_w2_f', 'grad_rwkv_w0_b', 'grad_rwkv_w2_b', 'grad_rwkv_a0_f', 'grad_rwkv_a2_f', 'grad_rwkv_a0_b', 'grad_rwkv_a2_b', 'grad_rwkv_k_k', 'grad_rwkv_k_a', 'grad_rwkv_r_k', 'grad_rwkv_gn_g', 'grad_rwkv_gn_b', 'grad_w_br_mla', 'grad_w_br_rwkv', 'grad_w_out', 'grad_g_post', 'delta_g_pre', 'delta_w_in', 'delta_mla_q_norm', 'delta_mla_wq_b', 'delta_mla_kv_norm', 'delta_mla_wkv_b', 'delta_rwkv_mu', 'delta_rwkv_w0_f', 'delta_rwkv_w2_f', 'delta_rwkv_w0_b', 'delta_rwkv_w2_b', 'delta_rwkv_a0_f', 'delta_rwkv_a2_f', 'delta_rwkv_a0_b', 'delta_rwkv_a2_b', 'delta_rwkv_k_k', 'delta_rwkv_k_a', 'delta_rwkv_r_k', 'delta_rwkv_gn_g', 'delta_rwkv_gn_b', 'delta_w_br_mla', 'delta_w_br_rwkv', 'delta_w_out', 'delta_g_post', 'new_m_g_pre', 'new_m_w_in', 'new_m_mla_q_norm', 'new_m_mla_wq_b', 'new_m_mla_kv_norm', 'new_m_mla_wkv_b', 'new_m_rwkv_mu', 'new_m_rwkv_w0_f', 'new_m_rwkv_w2_f', 'new_m_rwkv_w0_b', 'new_m_rwkv_w2_b', 'new_m_rwkv_a0_f', 'new_m_rwkv_a2_f', 'new_m_rwkv_a0_b', 'new_m_rwkv_a2_b', 'new_m_rwkv_k_k', 'new_m_rwkv_k_a', 'new_m_rwkv_r_k', 'new_m_rwkv_gn_g', 'new_m_rwkv_gn_b', 'new_m_w_br_mla', 'new_m_w_br_rwkv', 'new_m_w_out', 'new_m_g_post', 'new_v_g_pre', 'new_v_w_in', 'new_v_mla_q_norm', 'new_v_mla_wq_b', 'new_v_mla_kv_norm', 'new_v_mla_wkv_b', 'new_v_rwkv_mu', 'new_v_rwkv_w0_f', 'new_v_rwkv_w2_f', 'new_v_rwkv_w0_b', 'new_v_rwkv_w2_b', 'new_v_rwkv_a0_f', 'new_v_rwkv_a2_f', 'new_v_rwkv_a0_b', 'new_v_rwkv_a2_b', 'new_v_rwkv_k_k', 'new_v_rwkv_k_a', 'new_v_rwkv_r_k', 'new_v_rwkv_gn_g', 'new_v_rwkv_gn_b', 'new_v_w_br_mla', 'new_v_w_br_rwkv', 'new_v_w_out', 'new_v_g_post']
TWIN_LEAF_KINDS = {'loss': 'loss', 'grad_x': 'grad_x', 'grad_g_pre': 'grad_w', 'grad_w_in': 'grad_w', 'grad_mla_q_norm': 'grad_w', 'grad_mla_wq_b': 'grad_w', 'grad_mla_kv_norm': 'grad_w', 'grad_mla_wkv_b': 'grad_w', 'grad_rwkv_mu': 'grad_w', 'grad_rwkv_w0_f': 'grad_w', 'grad_rwkv_w2_f': 'grad_w', 'grad_rwkv_w0_b': 'grad_w', 'grad_rwkv_w2_b': 'grad_w', 'grad_rwkv_a0_f': 'grad_w', 'grad_rwkv_a2_f': 'grad_w', 'grad_rwkv_a0_b': 'grad_w', 'grad_rwkv_a2_b': 'grad_w', 'grad_rwkv_k_k': 'grad_w', 'grad_rwkv_k_a': 'grad_w', 'grad_rwkv_r_k': 'grad_w', 'grad_rwkv_gn_g': 'grad_w', 'grad_rwkv_gn_b': 'grad_w', 'grad_w_br_mla': 'grad_w', 'grad_w_br_rwkv': 'grad_w', 'grad_w_out': 'grad_w', 'grad_g_post': 'grad_w', 'delta_g_pre': 'delta_w', 'delta_w_in': 'delta_w', 'delta_mla_q_norm': 'delta_w', 'delta_mla_wq_b': 'delta_w', 'delta_mla_kv_norm': 'delta_w', 'delta_mla_wkv_b': 'delta_w', 'delta_rwkv_mu': 'delta_w', 'delta_rwkv_w0_f': 'delta_w', 'delta_rwkv_w2_f': 'delta_w', 'delta_rwkv_w0_b': 'delta_w', 'delta_rwkv_w2_b': 'delta_w', 'delta_rwkv_a0_f': 'delta_w', 'delta_rwkv_a2_f': 'delta_w', 'delta_rwkv_a0_b': 'delta_w', 'delta_rwkv_a2_b': 'delta_w', 'delta_rwkv_k_k': 'delta_w', 'delta_rwkv_k_a': 'delta_w', 'delta_rwkv_r_k': 'delta_w', 'delta_rwkv_gn_g': 'delta_w', 'delta_rwkv_gn_b': 'delta_w', 'delta_w_br_mla': 'delta_w', 'delta_w_br_rwkv': 'delta_w', 'delta_w_out': 'delta_w', 'delta_g_post': 'delta_w', 'new_m_g_pre': 'new_m', 'new_m_w_in': 'new_m', 'new_m_mla_q_norm': 'new_m', 'new_m_mla_wq_b': 'new_m', 'new_m_mla_kv_norm': 'new_m', 'new_m_mla_wkv_b': 'new_m', 'new_m_rwkv_mu': 'new_m', 'new_m_rwkv_w0_f': 'new_m', 'new_m_rwkv_w2_f': 'new_m', 'new_m_rwkv_w0_b': 'new_m', 'new_m_rwkv_w2_b': 'new_m', 'new_m_rwkv_a0_f': 'new_m', 'new_m_rwkv_a2_f': 'new_m', 'new_m_rwkv_a0_b': 'new_m', 'new_m_rwkv_a2_b': 'new_m', 'new_m_rwkv_k_k': 'new_m', 'new_m_rwkv_k_a': 'new_m', 'new_m_rwkv_r_k': 'new_m', 'new_m_rwkv_gn_g': 'new_m', 'new_m_rwkv_gn_b': 'new_m', 'new_m_w_br_mla': 'new_m', 'new_m_w_br_rwkv': 'new_m', 'new_m_w_out': 'new_m', 'new_m_g_post': 'new_m', 'new_v_g_pre': 'new_v', 'new_v_w_in': 'new_v', 'new_v_mla_q_norm': 'new_v', 'new_v_mla_wq_b': 'new_v', 'new_v_mla_kv_norm': 'new_v', 'new_v_mla_wkv_b': 'new_v', 'new_v_rwkv_mu': 'new_v', 'new_v_rwkv_w0_f': 'new_v', 'new_v_rwkv_w2_f': 'new_v', 'new_v_rwkv_w0_b': 'new_v', 'new_v_rwkv_w2_b': 'new_v', 'new_v_rwkv_a0_f': 'new_v', 'new_v_rwkv_a2_f': 'new_v', 'new_v_rwkv_a0_b': 'new_v', 'new_v_rwkv_a2_b': 'new_v', 'new_v_rwkv_k_k': 'new_v', 'new_v_rwkv_k_a': 'new_v', 'new_v_rwkv_r_k': 'new_v', 'new_v_rwkv_gn_g': 'new_v', 'new_v_rwkv_gn_b': 'new_v', 'new_v_w_br_mla': 'new_v', 'new_v_w_br_rwkv': 'new_v', 'new_v_w_out': 'new_v', 'new_v_g_post': 'new_v'}


def _forward(args):
    return _fwd_reference(*[args[k] for k in FWD_PARAMS])


def _output_shape():
    out = _jax.eval_shape(lambda: _forward(_fwd_setup_inputs(0)))
    return out.shape, out.dtype

N_MICROBATCH = 1
ADAM_LR = 0.001
ADAM_B1 = 0.9
ADAM_B2 = 0.999
ADAM_EPS = 1e-08
ADAM_WD = 0.01
ADAM_STEP = 10
PER_EXAMPLE_BATCH_AXIS = {'x': 0, 'loss_target': 0}
SHARED_INPUTS = []
_WEIGHT_DTYPES = {'g_pre': _jnp.float32, 'w_in': _jnp.float32, 'mla_q_norm': _jnp.float32, 'mla_wq_b': _jnp.float32, 'mla_kv_norm': _jnp.float32, 'mla_wkv_b': _jnp.float32, 'rwkv_mu': _jnp.float32, 'rwkv_w0_f': _jnp.float32, 'rwkv_w2_f': _jnp.float32, 'rwkv_w0_b': _jnp.float32, 'rwkv_w2_b': _jnp.float32, 'rwkv_a0_f': _jnp.float32, 'rwkv_a2_f': _jnp.float32, 'rwkv_a0_b': _jnp.float32, 'rwkv_a2_b': _jnp.float32, 'rwkv_k_k': _jnp.float32, 'rwkv_k_a': _jnp.float32, 'rwkv_r_k': _jnp.float32, 'rwkv_gn_g': _jnp.float32, 'rwkv_gn_b': _jnp.float32, 'w_br_mla': _jnp.float32, 'w_br_rwkv': _jnp.float32, 'w_out': _jnp.float32, 'g_post': _jnp.float32}
MOMENT_SCALE = {'g_pre': 2.025715e-01, 'w_in': 8.932559e-02, 'mla_q_norm': 1.812998e-02, 'mla_wq_b': 1.060369e-02, 'mla_kv_norm': 2.591658e-02, 'mla_wkv_b': 1.218370e-02, 'rwkv_mu': 2.219715e-01, 'rwkv_w0_f': 3.106761e-02, 'rwkv_w2_f': 9.991259e-03, 'rwkv_w0_b': 3.150625e-02, 'rwkv_w2_b': 1.014070e-02, 'rwkv_a0_f': 4.204360e-02, 'rwkv_a2_f': 2.888292e-02, 'rwkv_a0_b': 4.397472e-02, 'rwkv_a2_b': 3.008179e-02, 'rwkv_k_k': 3.720267e-02, 'rwkv_k_a': 1.617139e-01, 'rwkv_r_k': 3.430447e-01, 'rwkv_gn_g': 1.192430e-01, 'rwkv_gn_b': 2.014888e-01, 'w_br_mla': 9.384544e-03, 'w_br_rwkv': 8.886044e-02, 'w_out': 8.893461e-02, 'g_post': 8.013070e+00}


def _to_microbatches(a, axis):
    t = _jnp.moveaxis(a, axis, 0)
    t = t.reshape((N_MICROBATCH, t.shape[0] // N_MICROBATCH) + t.shape[1:])
    return _jnp.moveaxis(t, 1, axis + 1)


def setup_inputs(seed: int = 0) -> dict:
    inp = _fwd_setup_inputs(seed)
    key = _jax.random.fold_in(_jax.random.key(seed), 7919)
    shape, _ = _output_shape()
    out = dict(inp)
    out["loss_target"] = _jax.random.normal(_jax.random.fold_in(key, 0), shape, _jnp.float32)
    for i, name in enumerate(TWIN_WEIGHTS):
        w = inp[name].astype(_jnp.float32)
        if MOMENT_SCALE is None:
            s = _jnp.sqrt(_jnp.mean(_jnp.square(w)) + 1e-30)
        else:
            s = MOMENT_SCALE[name]
        km, kv = _jax.random.split(_jax.random.fold_in(key, i + 1))
        out[name] = w
        out["m_" + name] = s * _jax.random.normal(km, w.shape, _jnp.float32)
        out["v_" + name] = (s * s) * _jax.random.uniform(kv, w.shape, _jnp.float32, 0.5, 1.5)
    if N_MICROBATCH > 1:
        for name, axis in PER_EXAMPLE_BATCH_AXIS.items():
            out[name] = _to_microbatches(out[name], axis)
    return {'x': out['x'], 'g_pre': out['g_pre'], 'w_in': out['w_in'], 'mla_q_norm': out['mla_q_norm'], 'mla_wq_b': out['mla_wq_b'], 'mla_kv_norm': out['mla_kv_norm'], 'mla_wkv_b': out['mla_wkv_b'], 'rwkv_mu': out['rwkv_mu'], 'rwkv_w0_f': out['rwkv_w0_f'], 'rwkv_w2_f': out['rwkv_w2_f'], 'rwkv_w0_b': out['rwkv_w0_b'], 'rwkv_w2_b': out['rwkv_w2_b'], 'rwkv_a0_f': out['rwkv_a0_f'], 'rwkv_a2_f': out['rwkv_a2_f'], 'rwkv_a0_b': out['rwkv_a0_b'], 'rwkv_a2_b': out['rwkv_a2_b'], 'rwkv_k_k': out['rwkv_k_k'], 'rwkv_k_a': out['rwkv_k_a'], 'rwkv_r_k': out['rwkv_r_k'], 'rwkv_gn_g': out['rwkv_gn_g'], 'rwkv_gn_b': out['rwkv_gn_b'], 'w_br_mla': out['w_br_mla'], 'w_br_rwkv': out['w_br_rwkv'], 'w_out': out['w_out'], 'g_post': out['g_post'], 'loss_target': out['loss_target'], 'm_g_pre': out['m_g_pre'], 'm_w_in': out['m_w_in'], 'm_mla_q_norm': out['m_mla_q_norm'], 'm_mla_wq_b': out['m_mla_wq_b'], 'm_mla_kv_norm': out['m_mla_kv_norm'], 'm_mla_wkv_b': out['m_mla_wkv_b'], 'm_rwkv_mu': out['m_rwkv_mu'], 'm_rwkv_w0_f': out['m_rwkv_w0_f'], 'm_rwkv_w2_f': out['m_rwkv_w2_f'], 'm_rwkv_w0_b': out['m_rwkv_w0_b'], 'm_rwkv_w2_b': out['m_rwkv_w2_b'], 'm_rwkv_a0_f': out['m_rwkv_a0_f'], 'm_rwkv_a2_f': out['m_rwkv_a2_f'], 'm_rwkv_a0_b': out['m_rwkv_a0_b'], 'm_rwkv_a2_b': out['m_rwkv_a2_b'], 'm_rwkv_k_k': out['m_rwkv_k_k'], 'm_rwkv_k_a': out['m_rwkv_k_a'], 'm_rwkv_r_k': out['m_rwkv_r_k'], 'm_rwkv_gn_g': out['m_rwkv_gn_g'], 'm_rwkv_gn_b': out['m_rwkv_gn_b'], 'm_w_br_mla': out['m_w_br_mla'], 'm_w_br_rwkv': out['m_w_br_rwkv'], 'm_w_out': out['m_w_out'], 'm_g_post': out['m_g_post'], 'v_g_pre': out['v_g_pre'], 'v_w_in': out['v_w_in'], 'v_mla_q_norm': out['v_mla_q_norm'], 'v_mla_wq_b': out['v_mla_wq_b'], 'v_mla_kv_norm': out['v_mla_kv_norm'], 'v_mla_wkv_b': out['v_mla_wkv_b'], 'v_rwkv_mu': out['v_rwkv_mu'], 'v_rwkv_w0_f': out['v_rwkv_w0_f'], 'v_rwkv_w2_f': out['v_rwkv_w2_f'], 'v_rwkv_w0_b': out['v_rwkv_w0_b'], 'v_rwkv_w2_b': out['v_rwkv_w2_b'], 'v_rwkv_a0_f': out['v_rwkv_a0_f'], 'v_rwkv_a2_f': out['v_rwkv_a2_f'], 'v_rwkv_a0_b': out['v_rwkv_a0_b'], 'v_rwkv_a2_b': out['v_rwkv_a2_b'], 'v_rwkv_k_k': out['v_rwkv_k_k'], 'v_rwkv_k_a': out['v_rwkv_k_a'], 'v_rwkv_r_k': out['v_rwkv_r_k'], 'v_rwkv_gn_g': out['v_rwkv_gn_g'], 'v_rwkv_gn_b': out['v_rwkv_gn_b'], 'v_w_br_mla': out['v_w_br_mla'], 'v_w_br_rwkv': out['v_w_br_rwkv'], 'v_w_out': out['v_w_out'], 'v_g_post': out['v_g_post']}


def _loss(weights, diff, rest, loss_target):
    with _jax.named_scope("forward"):
        args = {**rest, TWIN_DIFF_INPUT: diff, **{k: w.astype(_WEIGHT_DTYPES[k]) for k, w in weights.items()}}
        y = _forward(args)
    with _jax.named_scope("loss_head"):
        err = _jnp.square(y.astype(_jnp.float32) - loss_target)
        return 0.5 * _jnp.sum(_jnp.mean(err, axis=-1)) if err.ndim else 0.5 * err


def _adamw(w, g, m, v):
    m = ADAM_B1 * m + (1.0 - ADAM_B1) * g
    v = ADAM_B2 * v + (1.0 - ADAM_B2) * _jnp.square(g)
    m_hat = m / (1.0 - ADAM_B1 ** ADAM_STEP)
    v_hat = v / (1.0 - ADAM_B2 ** ADAM_STEP)
    delta = -ADAM_LR * (m_hat / (_jnp.sqrt(v_hat) + ADAM_EPS) + ADAM_WD * w)
    return delta, m, v


def reference(x, g_pre, w_in, mla_q_norm, mla_wq_b, mla_kv_norm, mla_wkv_b, rwkv_mu, rwkv_w0_f, rwkv_w2_f, rwkv_w0_b, rwkv_w2_b, rwkv_a0_f, rwkv_a2_f, rwkv_a0_b, rwkv_a2_b, rwkv_k_k, rwkv_k_a, rwkv_r_k, rwkv_gn_g, rwkv_gn_b, w_br_mla, w_br_rwkv, w_out, g_post, loss_target, m_g_pre, m_w_in, m_mla_q_norm, m_mla_wq_b, m_mla_kv_norm, m_mla_wkv_b, m_rwkv_mu, m_rwkv_w0_f, m_rwkv_w2_f, m_rwkv_w0_b, m_rwkv_w2_b, m_rwkv_a0_f, m_rwkv_a2_f, m_rwkv_a0_b, m_rwkv_a2_b, m_rwkv_k_k, m_rwkv_k_a, m_rwkv_r_k, m_rwkv_gn_g, m_rwkv_gn_b, m_w_br_mla, m_w_br_rwkv, m_w_out, m_g_post, v_g_pre, v_w_in, v_mla_q_norm, v_mla_wq_b, v_mla_kv_norm, v_mla_wkv_b, v_rwkv_mu, v_rwkv_w0_f, v_rwkv_w2_f, v_rwkv_w0_b, v_rwkv_w2_b, v_rwkv_a0_f, v_rwkv_a2_f, v_rwkv_a0_b, v_rwkv_a2_b, v_rwkv_k_k, v_rwkv_k_a, v_rwkv_r_k, v_rwkv_gn_g, v_rwkv_gn_b, v_w_br_mla, v_w_br_rwkv, v_w_out, v_g_post):
    given = dict(x=x, g_pre=g_pre, w_in=w_in, mla_q_norm=mla_q_norm, mla_wq_b=mla_wq_b, mla_kv_norm=mla_kv_norm, mla_wkv_b=mla_wkv_b, rwkv_mu=rwkv_mu, rwkv_w0_f=rwkv_w0_f, rwkv_w2_f=rwkv_w2_f, rwkv_w0_b=rwkv_w0_b, rwkv_w2_b=rwkv_w2_b, rwkv_a0_f=rwkv_a0_f, rwkv_a2_f=rwkv_a2_f, rwkv_a0_b=rwkv_a0_b, rwkv_a2_b=rwkv_a2_b, rwkv_k_k=rwkv_k_k, rwkv_k_a=rwkv_k_a, rwkv_r_k=rwkv_r_k, rwkv_gn_g=rwkv_gn_g, rwkv_gn_b=rwkv_gn_b, w_br_mla=w_br_mla, w_br_rwkv=w_br_rwkv, w_out=w_out, g_post=g_post, loss_target=loss_target, m_g_pre=m_g_pre, m_w_in=m_w_in, m_mla_q_norm=m_mla_q_norm, m_mla_wq_b=m_mla_wq_b, m_mla_kv_norm=m_mla_kv_norm, m_mla_wkv_b=m_mla_wkv_b, m_rwkv_mu=m_rwkv_mu, m_rwkv_w0_f=m_rwkv_w0_f, m_rwkv_w2_f=m_rwkv_w2_f, m_rwkv_w0_b=m_rwkv_w0_b, m_rwkv_w2_b=m_rwkv_w2_b, m_rwkv_a0_f=m_rwkv_a0_f, m_rwkv_a2_f=m_rwkv_a2_f, m_rwkv_a0_b=m_rwkv_a0_b, m_rwkv_a2_b=m_rwkv_a2_b, m_rwkv_k_k=m_rwkv_k_k, m_rwkv_k_a=m_rwkv_k_a, m_rwkv_r_k=m_rwkv_r_k, m_rwkv_gn_g=m_rwkv_gn_g, m_rwkv_gn_b=m_rwkv_gn_b, m_w_br_mla=m_w_br_mla, m_w_br_rwkv=m_w_br_rwkv, m_w_out=m_w_out, m_g_post=m_g_post, v_g_pre=v_g_pre, v_w_in=v_w_in, v_mla_q_norm=v_mla_q_norm, v_mla_wq_b=v_mla_wq_b, v_mla_kv_norm=v_mla_kv_norm, v_mla_wkv_b=v_mla_wkv_b, v_rwkv_mu=v_rwkv_mu, v_rwkv_w0_f=v_rwkv_w0_f, v_rwkv_w2_f=v_rwkv_w2_f, v_rwkv_w0_b=v_rwkv_w0_b, v_rwkv_w2_b=v_rwkv_w2_b, v_rwkv_a0_f=v_rwkv_a0_f, v_rwkv_a2_f=v_rwkv_a2_f, v_rwkv_a0_b=v_rwkv_a0_b, v_rwkv_a2_b=v_rwkv_a2_b, v_rwkv_k_k=v_rwkv_k_k, v_rwkv_k_a=v_rwkv_k_a, v_rwkv_r_k=v_rwkv_r_k, v_rwkv_gn_g=v_rwkv_gn_g, v_rwkv_gn_b=v_rwkv_gn_b, v_w_br_mla=v_w_br_mla, v_w_br_rwkv=v_w_br_rwkv, v_w_out=v_w_out, v_g_post=v_g_post)
    weights = {n: given[n] for n in TWIN_WEIGHTS}
    shared = {n: given[n] for n in SHARED_INPUTS}
    per_example = {n: given[n] for n in ['x']}
    grad_fn = _jax.value_and_grad(_loss, argnums=(0, 1))

    def one_microbatch(ex, loss_target):
        ex = dict(ex)
        diff = ex.pop(TWIN_DIFF_INPUT)
        return grad_fn(weights, diff, {**shared, **ex}, loss_target)

    if N_MICROBATCH == 1:
        loss, (grad_w, grad_x) = one_microbatch(per_example, given["loss_target"])
    else:
        def body(carry, xs):
            loss_sum, grad_sum = carry
            l_k, (gw_k, gx_k) = one_microbatch(xs[0], xs[1])
            with _jax.named_scope("update"):
                return (loss_sum + l_k, _jax.tree.map(_jnp.add, grad_sum, gw_k)), gx_k

        init = (_jnp.zeros((), _jnp.float32), _jax.tree.map(_jnp.zeros_like, weights))
        (loss, grad_w), grad_x = _jax.lax.scan(body, init, (per_example, given["loss_target"]))
    with _jax.named_scope("update"):
        delta_w, new_m, new_v = {}, {}, {}
        for n in TWIN_WEIGHTS:
            delta_w[n], new_m[n], new_v[n] = _adamw(weights[n], grad_w[n], given["m_" + n], given["v_" + n])
    return (loss, grad_x, *[grad_w[n] for n in TWIN_WEIGHTS], *[delta_w[n] for n in TWIN_WEIGHTS],
            *[new_m[n] for n in TWIN_WEIGHTS], *[new_v[n] for n in TWIN_WEIGHTS])
```

```python
import functools
import math

import numpy as np
import jax
import jax.numpy as jnp
from jax import lax
from jax.experimental import pallas as pl
from jax.experimental.pallas import tpu as pltpu

F32, BF16 = jnp.float32, jnp.bfloat16
MESH_IDS = pl.DeviceIdType.MESH

D = 2048
T = 2048
HEADS = 8
Q_RANK = 512
KV_RANK = 512
NOPE = 128
ROPE = 64
VDIM = 128
RW = 1024
RH = 16
RN = 64
LORA = 96
D_IN = 10688
NORM_EPS = 1e-6
GN_EPS = 64e-5
ROPE_THETA = 10000.0
ADAM_LR, ADAM_B1, ADAM_B2, ADAM_EPS, ADAM_WD, ADAM_STEP = 0.001, 0.9, 0.999, 1e-08, 0.01, 10

LANE = 128
VMEM_BIG = 56 * 2**20

NP = 11008
OFF_QA, OFF_KVA, OFF_RKV, OFF_ZM, OFF_ZR, OFF_GM, OFF_GR, OFF_LORA, OFF_KR = 0, 512, 1024, 4096, 5120, 6144, 8192, 10240, 10752
NLERP = 3584

CHUNK = 64
NCH = T // CHUNK


def _dg(a, b, ca, cb, batch=False, prec=None):
    bd = ((0,), (0,)) if batch else ((), ())
    return lax.dot_general(a, b, (((ca,), (cb,)), bd), precision=prec, preferred_element_type=F32)


@jax.custom_vjp
def bdot(a, b):
    return _dg(a.astype(BF16), b.astype(BF16), 1, 0)


def _bdot_fwd(a, b):
    return bdot(a, b), (a, b)


def _bdot_bwd(res, g):
    a, b = res
    gb = g.astype(BF16)
    da = _dg(gb, b.astype(BF16), 1, 1)
    db = _dg(a.astype(BF16), gb, 0, 0)
    return da.astype(a.dtype), db.astype(b.dtype)


bdot.defvjp(_bdot_fwd, _bdot_bwd)


def _split(x):
    hi = x.astype(BF16)
    lo = (x - hi.astype(F32)).astype(BF16)
    return hi, lo


@jax.custom_vjp
def gsum(x, g2):
    hi, lo = _split(x)
    return _dg(hi, g2, 1, 0) + _dg(lo, g2, 1, 0)


def _gsum_fwd(x, g2):
    return gsum(x, g2), g2


def _gsum_bwd(g2, g):
    hi, lo = _split(g)
    return _dg(hi, g2, 1, 1) + _dg(lo, g2, 1, 1), jnp.zeros_like(g2)


gsum.defvjp(_gsum_fwd, _gsum_bwd)


def headsum(x, g2):
    return jnp.concatenate([gsum(x[:, i * LANE:(i + 1) * LANE], g2) for i in range(x.shape[1] // LANE)], axis=1)


_HI = lax.Precision.HIGHEST


@jax.custom_vjp
def nn(a, b):
    return _dg(a, b, 2, 1, True, _HI)


@jax.custom_vjp
def nt(a, b):
    return _dg(a, b, 2, 2, True, _HI)


@jax.custom_vjp
def tn(a, b):
    return _dg(a, b, 1, 1, True, _HI)


nn.defvjp(lambda a, b: (nn(a, b), (a, b)), lambda r, g: (nt(g, r[1]), tn(r[0], g)))
nt.defvjp(lambda a, b: (nt(a, b), (a, b)), lambda r, g: (nn(g, r[1]), tn(g, r[0])))
tn.defvjp(lambda a, b: (tn(a, b), (a, b)), lambda r, g: (nt(r[1], g), nn(r[0], g)))


def _rms(x, g):
    return x * lax.rsqrt(jnp.mean(x * x, axis=-1, keepdims=True) + NORM_EPS) * g


def _softplus(x):
    pos = x > 0
    return jnp.where(pos, x, 0.0) + jnp.log(1.0 + jnp.exp(-jnp.where(pos, x, -x)))


def _silu(z):
    return z * jax.nn.sigmoid(z)


def _tile(n, cands):
    for c in cands:
        if n % c == 0:
            return c
    raise ValueError(n)


def matmul(name, a, b, mode, out_dtype=F32):
    if mode == "nn":
        (m, k), n = a.shape, b.shape[1]
    elif mode == "nt":
        (m, k), n = a.shape, b.shape[0]
    else:
        (k, m), n = a.shape, b.shape[1]
    tm = _tile(m, (1024, 512, 256, 128))
    tn_ = _tile(n, (512, 256, 128))
    tk = _tile(k, (512, 256, 128))
    nk = k // tk
    if mode == "nn":
        a_spec = pl.BlockSpec((tm, tk), lambda i, j, kk: (i, kk))
        b_spec = pl.BlockSpec((tk, tn_), lambda i, j, kk: (kk, j))
        ca, cb = 1, 0
    elif mode == "nt":
        a_spec = pl.BlockSpec((tm, tk), lambda i, j, kk: (i, kk))
        b_spec = pl.BlockSpec((tn_, tk), lambda i, j, kk: (j, kk))
        ca, cb = 1, 1
    else:
        a_spec = pl.BlockSpec((tk, tm), lambda i, j, kk: (kk, i))
        b_spec = pl.BlockSpec((tk, tn_), lambda i, j, kk: (kk, j))
        ca, cb = 0, 0

    def body(a_ref, b_ref, o_ref, acc_ref):
        kk = pl.program_id(2)

        @pl.when(kk == 0)
        def _():
            acc_ref[...] = jnp.zeros_like(acc_ref)

        acc_ref[...] += _dg(a_ref[...].astype(BF16), b_ref[...].astype(BF16), ca, cb)

        @pl.when(kk == nk - 1)
        def _():
            o_ref[...] = acc_ref[...].astype(o_ref.dtype)

    return pl.pallas_call(
        body, name=name, grid=(m // tm, n // tn_, nk),
        in_specs=[a_spec, b_spec],
        out_specs=pl.BlockSpec((tm, tn_), lambda i, j, kk: (i, j)),
        out_shape=jax.ShapeDtypeStruct((m, n), out_dtype),
        scratch_shapes=[pltpu.VMEM((tm, tn_), F32)],
        compiler_params=pltpu.CompilerParams(dimension_semantics=("parallel", "parallel", "arbitrary"),
                                             vmem_limit_bytes=VMEM_BIG),
    )(a, b)


def _rspec(tr, width, blk):
    return pl.BlockSpec((tr, width), lambda i: (i, blk))


def _full_spec(arr):
    return pl.BlockSpec(arr.shape, lambda i: (0,) * arr.ndim)


class Stage:
    def __init__(self, name, f, outs, tr, diff_rows, diff_params, drow_dtypes):
        self.name, self.f, self.outs, self.tr = name, f, outs, tr
        self.diff_rows, self.diff_params, self.drow_dtypes = diff_rows, diff_params, drow_dtypes

    def fwd(self, rows, params):
        f, nr, npar = self.f, len(rows), len(params)
        stored = [(w, dt) for (w, dt) in self.outs if dt is not None]
        keep = [i for i, (w, dt) in enumerate(self.outs) if dt is not None]

        def body(*refs):
            vals = f(*[r[...].astype(F32) for r in refs[:nr]], *[p[...] for p in refs[nr:nr + npar]])
            for o_ref, i in zip(refs[nr + npar:], keep):
                o_ref[...] = vals[i].astype(o_ref.dtype)

        return pl.pallas_call(
            body, name=self.name + "_fwd", grid=(T // self.tr,),
            in_specs=[_rspec(self.tr, w, b) for (_, w, b) in rows] + [_full_spec(p) for p in params],
            out_specs=[_rspec(self.tr, w, 0) for (w, _) in stored],
            out_shape=[jax.ShapeDtypeStruct((T, w), dt) for (w, dt) in stored],
            compiler_params=pltpu.CompilerParams(dimension_semantics=("arbitrary",), vmem_limit_bytes=VMEM_BIG),
        )(*[r[0] for r in rows], *params)

    def bwd(self, rows, params, cts):
        f, nr, npar = self.f, len(rows), len(params)
        dr_idx, dp_idx = self.diff_rows, self.diff_params
        flat_cts = [c for lst in cts for c in lst]
        nct = len(flat_cts)

        def body(*refs):
            row_refs, par_refs = refs[:nr], refs[nr:nr + npar]
            ct_refs = refs[nr + npar:nr + npar + nct]
            drow_refs = refs[nr + npar + nct:nr + npar + nct + len(dr_idx)]
            dpar_refs = refs[nr + npar + nct + len(dr_idx):]
            row_vals = [r[...].astype(F32) for r in row_refs]
            par_vals = [p[...] for p in par_refs]

            def g(*dv):
                rv, pv = list(row_vals), list(par_vals)
                for j, i in enumerate(dr_idx):
                    rv[i] = dv[j]
                for j, i in enumerate(dp_idx):
                    pv[i] = dv[len(dr_idx) + j]
                return f(*rv, *pv)

            _, vjp = jax.vjp(g, *[row_vals[i] for i in dr_idx], *[par_vals[i] for i in dp_idx])
            ct_vals, pos = [], 0
            for lst in cts:
                acc = ct_refs[pos][...].astype(F32)
                for q in range(1, len(lst)):
                    acc = acc + ct_refs[pos + q][...].astype(F32)
                pos += len(lst)
                ct_vals.append(acc)
            grads = vjp(tuple(ct_vals))
            for j, r in enumerate(drow_refs):
                r[...] = grads[j].astype(r.dtype)

            @pl.when(pl.program_id(0) == 0)
            def _():
                for r in dpar_refs:
                    r[...] = jnp.zeros_like(r)

            for j, r in enumerate(dpar_refs):
                r[...] += grads[len(dr_idx) + j].astype(F32)

        drow_shapes = [jax.ShapeDtypeStruct((T, rows[i][1]), dt) for i, dt in zip(dr_idx, self.drow_dtypes)]
        dpar_shapes = [jax.ShapeDtypeStruct(params[i].shape, F32) for i in dp_idx]
        res = pl.pallas_call(
            body, name=self.name + "_bwd", grid=(T // self.tr,),
            in_specs=[_rspec(self.tr, w, b) for (_, w, b) in rows] + [_full_spec(p) for p in params]
            + [_rspec(self.tr, w, b) for (_, w, b) in flat_cts],
            out_specs=[_rspec(self.tr, rows[i][1], 0) for i in dr_idx] + [_full_spec(params[i]) for i in dp_idx],
            out_shape=drow_shapes + dpar_shapes,
            compiler_params=pltpu.CompilerParams(dimension_semantics=("arbitrary",), vmem_limit_bytes=VMEM_BIG),
        )(*[r[0] for r in rows], *params, *[c[0] for c in flat_cts])
        return res[:len(dr_idx)], res[len(dr_idx):]


def f_pre(x, g):
    return _rms(x, g), x


def f_mla(q_a, kv_a, kr, cosq, sinq, cosk, sink, gq, gkv, wq, wkv, e):
    q = bdot(_rms(q_a, gq), wq)
    kv = bdot(_rms(kv_a, gkv), wkv)
    t1, t2 = q[:, 1024:1280], q[:, 1280:1536]
    k1, k2 = kr[:, :LANE], kr[:, LANE:]
    kr1 = k1 * cosk - k2 * sink
    kr2 = k1 * sink + k2 * cosk
    return (q[:, :1024], t1 * cosq - t2 * sinq, t1 * sinq + t2 * cosq,
            kv[:, :1024], bdot(kr1, e), bdot(kr2, e), kv[:, 1024:])


def f_rwkv_pre(r, k, v, lora, w0f, w0b, a0f, a0b, kkw, kaw, w2f, w2b, a2f, a2b, g2):
    wdf, wdb, adf, adb = (lora[:, i * LANE:(i + 1) * LANE] for i in range(4))

    def logdecay(w0, wd, w2):
        z = w0 + bdot(jnp.tanh(wd), w2)
        return -jnp.exp(-_softplus(-z) - 0.5)

    a_f = jax.nn.sigmoid(a0f + bdot(adf, a2f))
    a_b = jax.nn.sigmoid(a0b + bdot(adb, a2b))
    kk = k * kkw
    kk = kk / jnp.maximum(jnp.sqrt(headsum(kk * kk, g2)), 1e-12)
    return (r, v, logdecay(w0f, wdf, w2f), logdecay(w0b, wdb, w2b),
            k * (1.0 + (a_f - 1.0) * kaw), k * (1.0 + (a_b - 1.0) * kaw), -kk, kk * a_f, kk * a_b)


def f_rwkv_post(yf, yb, r, kf, kb, v, z, gng, gnb, rk, g2):
    y = yf + yb
    mu = headsum(y, g2) * (1.0 / RN)
    d = y - mu
    var = headsum(d * d, g2) * (1.0 / RN)
    yn = d * lax.rsqrt(var + GN_EPS) * gng + gnb
    bonus = headsum(r * (kf + kb) * rk, g2) * v
    return ((yn + bonus) * _silu(z),)


def f_gate(y, z):
    return (y * _silu(z),)


def f_merge(um, ur, gm, gr):
    return (jax.nn.sigmoid(gm) * um + jax.nn.sigmoid(gr) * ur,)


_SHIFT_W = 256


def _lerp_colblock(j):
    return jnp.where(j < 3072 // _SHIFT_W, OFF_RKV // _SHIFT_W + j, OFF_LORA // _SHIFT_W + j - 3072 // _SHIFT_W)


def _nbr_mean(x):
    row = lax.broadcasted_iota(jnp.int32, x.shape, 0)
    up = jnp.where(row == 0, 0.0, pltpu.roll(x, 1, 0))
    dn = jnp.where(row == T - 1, 0.0, pltpu.roll(x, T - 1, 0))
    return 0.5 * (up + dn)


def shift_fwd(proj, mu):
    def body(x_ref, mu_ref, o_ref):
        x = x_ref[...]
        o_ref[...] = x + mu_ref[...] * (_nbr_mean(x) - x)

    return pl.pallas_call(
        body, name="shift_fwd", grid=(NLERP // _SHIFT_W,),
        in_specs=[pl.BlockSpec((T, _SHIFT_W), lambda j: (0, _lerp_colblock(j))),
                  pl.BlockSpec((1, _SHIFT_W), lambda j: (0, j))],
        out_specs=pl.BlockSpec((T, _SHIFT_W), lambda j: (0, j)),
        out_shape=jax.ShapeDtypeStruct((T, NLERP), F32),
        compiler_params=pltpu.CompilerParams(dimension_semantics=("parallel",), vmem_limit_bytes=VMEM_BIG),
    )(proj, mu)


def shift_bwd(proj, mu, g):
    def body(x_ref, mu_ref, g_ref, dx_ref, dmu_ref):
        x, gv = x_ref[...], g_ref[...]
        dmu_ref[...] = jnp.sum(gv * (_nbr_mean(x) - x), axis=0, keepdims=True)
        gm = gv * mu_ref[...]
        dx_ref[...] = (gv - gm + _nbr_mean(gm)).astype(dx_ref.dtype)

    col = pl.BlockSpec((T, _SHIFT_W), lambda j: (0, j))
    vec = pl.BlockSpec((1, _SHIFT_W), lambda j: (0, j))
    return pl.pallas_call(
        body, name="shift_bwd", grid=(NLERP // _SHIFT_W,),
        in_specs=[pl.BlockSpec((T, _SHIFT_W), lambda j: (0, _lerp_colblock(j))), vec, col],
        out_specs=[col, vec],
        out_shape=[jax.ShapeDtypeStruct((T, NLERP), BF16), jax.ShapeDtypeStruct((1, NLERP), F32)],
        compiler_params=pltpu.CompilerParams(dimension_semantics=("parallel",), vmem_limit_bytes=VMEM_BIG),
    )(proj, mu, g)


_TQ = 256
_ATT_SCALE = (NOPE + ROPE) ** -0.5


def _probs(q, k):
    s = _dg(q, k, 1, 1) * _ATT_SCALE
    e = jnp.exp(s - jnp.max(s, axis=-1, keepdims=True))
    return e * (1.0 / jnp.sum(e, axis=-1, keepdims=True))


def attn_fwd(q, k, v):
    def body(q_ref, k_ref, v_ref, o_ref):
        p = _probs(q_ref[0], k_ref[0])
        o_ref[0] = _dg(p.astype(BF16), v_ref[0], 1, 0)

    dq = NOPE + ROPE
    return pl.pallas_call(
        body, name="attn_fwd", grid=(HEADS, T // _TQ),
        in_specs=[pl.BlockSpec((1, _TQ, dq), lambda h, i: (h, i, 0)),
                  pl.BlockSpec((1, T, dq), lambda h, i: (h, 0, 0)),
                  pl.BlockSpec((1, T, VDIM), lambda h, i: (h, 0, 0))],
        out_specs=pl.BlockSpec((1, _TQ, VDIM), lambda h, i: (h, i, 0)),
        out_shape=jax.ShapeDtypeStruct((HEADS, T, VDIM), F32),
        compiler_params=pltpu.CompilerParams(dimension_semantics=("parallel", "arbitrary"), vmem_limit_bytes=VMEM_BIG),
    )(q, k, v)


def attn_bwd(q, k, v, do):
    def body(q_ref, k_ref, v_ref, do_ref, dq_ref, dk_ref, dv_ref):
        @pl.when(pl.program_id(1) == 0)
        def _():
            dk_ref[...] = jnp.zeros_like(dk_ref)
            dv_ref[...] = jnp.zeros_like(dv_ref)

        qv, kv_, vv = q_ref[0], k_ref[0], v_ref[0]
        dob = do_ref[0].astype(BF16)
        p = _probs(qv, kv_)
        dv_ref[0] += _dg(p.astype(BF16), dob, 0, 0)
        dp = _dg(dob, vv, 1, 1)
        ds = (p * (dp - jnp.sum(dp * p, axis=-1, keepdims=True)) * _ATT_SCALE).astype(BF16)
        dq_ref[0] = _dg(ds, kv_, 1, 0)
        dk_ref[0] += _dg(ds, qv, 0, 0)

    dq = NOPE + ROPE
    return pl.pallas_call(
        body, name="attn_bwd", grid=(HEADS, T // _TQ),
        in_specs=[pl.BlockSpec((1, _TQ, dq), lambda h, i: (h, i, 0)),
                  pl.BlockSpec((1, T, dq), lambda h, i: (h, 0, 0)),
                  pl.BlockSpec((1, T, VDIM), lambda h, i: (h, 0, 0)),
                  pl.BlockSpec((1, _TQ, VDIM), lambda h, i: (h, i, 0))],
        out_specs=[pl.BlockSpec((1, _TQ, dq), lambda h, i: (h, i, 0)),
                   pl.BlockSpec((1, T, dq), lambda h, i: (h, 0, 0)),
                   pl.BlockSpec((1, T, VDIM), lambda h, i: (h, 0, 0))],
        out_shape=[jax.ShapeDtypeStruct((HEADS, T, dq), F32), jax.ShapeDtypeStruct((HEADS, T, dq), F32),
                   jax.ShapeDtypeStruct((HEADS, T, VDIM), F32)],
        compiler_params=pltpu.CompilerParams(dimension_semantics=("parallel", "arbitrary"), vmem_limit_bytes=VMEM_BIG),
    )(q, k, v, do)


def _chunk(r, lw, k, v, a, b, ht, *, reverse):
    hb, c, _ = r.shape
    ti = lax.broadcasted_iota(jnp.int32, (c, c), 0)
    si = lax.broadcasted_iota(jnp.int32, (c, c), 1)
    incl = (si >= ti) if reverse else (si <= ti)
    strict = (si > ti) if reverse else (si < ti)
    ones = jnp.broadcast_to(incl.astype(F32)[None], (hb, c, c))
    cum = nn(ones, lw)
    cum_ex = cum - lw
    tot = jnp.sum(lw, axis=1, keepdims=True)
    mid = 0.5 * tot
    rt, at = r * jnp.exp(cum - mid), a * jnp.exp(cum_ex - mid)
    einv = jnp.exp(mid - cum)
    kt, bt = k * einv, b * einv
    m_ab = jnp.where(strict, nt(at, bt), 0.0)
    m_ak = jnp.where(strict, nt(at, kt), 0.0)
    m_rb = jnp.where(incl, nt(rt, bt), 0.0)
    m_rk = jnp.where(incl, nt(rt, kt), 0.0)
    u = nt(a * jnp.exp(cum_ex), ht) + nn(m_ak, v)
    lp = m_ab
    steps = int(math.log2(c))
    for i in range(steps):
        u = u + nn(lp, u)
        if i < steps - 1:
            lp = nn(lp, lp)
    y = nt(r * jnp.exp(cum), ht) + nn(m_rb, u) + nn(m_rk, v)
    eend = jnp.exp(tot - cum)
    ht_new = ht * jnp.exp(tot) + tn(u, b * eend) + tn(v, k * eend)
    return y, ht_new


_HB_F, _HB_B = 4, 2


def _chunk_map(reverse, backward):
    flip = reverse != backward
    return (lambda g, c: (g, NCH - 1 - c, 0)) if flip else (lambda g, c: (g, c, 0))


def scan_fwd(name, r, lw, k, v, a, b, reverse):
    hb = _HB_F
    cmap = _chunk_map(reverse, False)

    def body(r_ref, lw_ref, k_ref, v_ref, a_ref, b_ref, y_ref, h0_ref, ht_ref):
        @pl.when(pl.program_id(1) == 0)
        def _():
            ht_ref[...] = jnp.zeros_like(ht_ref)

        ht = ht_ref[...]
        h0_ref[0] = ht
        y, hn = _chunk(r_ref[...], lw_ref[...], k_ref[...], v_ref[...], a_ref[...], b_ref[...], ht, reverse=reverse)
        y_ref[...] = y
        ht_ref[...] = hn

    io = pl.BlockSpec((hb, CHUNK, RN), cmap)
    return pl.pallas_call(
        body, name=name, grid=(RH // hb, NCH),
        in_specs=[io] * 6,
        out_specs=[io, pl.BlockSpec((1, hb, RN, RN), lambda g, c: (cmap(g, c)[1], g, 0, 0))],
        out_shape=[jax.ShapeDtypeStruct((RH, T, RN), F32), jax.ShapeDtypeStruct((NCH, RH, RN, RN), F32)],
        scratch_shapes=[pltpu.VMEM((hb, RN, RN), F32)],
        compiler_params=pltpu.CompilerParams(dimension_semantics=("parallel", "arbitrary"), vmem_limit_bytes=VMEM_BIG),
    )(r, lw, k, v, a, b)


def scan_bwd(name, r, lw, k, v, a, b, h0, dy, reverse):
    hb = _HB_B
    cmap = _chunk_map(reverse, True)

    def body(r_ref, lw_ref, k_ref, v_ref, a_ref, b_ref, h0_ref, dy_ref, *rest):
        d_refs, dht_ref = rest[:6], rest[6]

        @pl.when(pl.program_id(1) == 0)
        def _():
            dht_ref[...] = jnp.zeros_like(dht_ref)

        _, vjp = jax.vjp(functools.partial(_chunk, reverse=reverse), r_ref[...], lw_ref[...], k_ref[...], v_ref[...],
                         a_ref[...], b_ref[...], h0_ref[0])
        grads = vjp((dy_ref[...], dht_ref[...]))
        for d_ref, gval in zip(d_refs, grads[:6]):
            d_ref[...] = gval
        dht_ref[...] = grads[6]

    io = pl.BlockSpec((hb, CHUNK, RN), cmap)
    return pl.pallas_call(
        body, name=name, grid=(RH // hb, NCH),
        in_specs=[io] * 6 + [pl.BlockSpec((1, hb, RN, RN), lambda g, c: (cmap(g, c)[1], g, 0, 0)), io],
        out_specs=[io] * 6,
        out_shape=[jax.ShapeDtypeStruct((RH, T, RN), F32)] * 6,
        scratch_shapes=[pltpu.VMEM((hb, RN, RN), F32)],
        compiler_params=pltpu.CompilerParams(dimension_semantics=("parallel", "arbitrary"), vmem_limit_bytes=VMEM_BIG),
    )(r, lw, k, v, a, b, h0, dy)


def loss_stage(out, x2, tgt, g_post):
    tr = 256

    def body(o_ref, x_ref, t_ref, g_ref, do_ref, dy_ref, dg_ref, loss_ref):
        @pl.when(pl.program_id(0) == 0)
        def _():
            dg_ref[...] = jnp.zeros_like(dg_ref)
            loss_ref[...] = jnp.zeros_like(loss_ref)

        nrm, vjp = jax.vjp(_rms, o_ref[...], g_ref[...])
        e = x_ref[...] + nrm - t_ref[...]
        s = jnp.sum(jnp.sum(e * e, axis=1, keepdims=True), axis=0, keepdims=True)
        loss_ref[...] += jnp.broadcast_to(s * (0.5 / D), loss_ref.shape)
        dy = e * (1.0 / D)
        do, dg = vjp(dy)
        do_ref[...] = do.astype(do_ref.dtype)
        dy_ref[...] = dy
        dg_ref[...] += dg

    row = pl.BlockSpec((tr, D), lambda i: (i, 0))
    return pl.pallas_call(
        body, name="loss_stage", grid=(T // tr,),
        in_specs=[row, row, row, pl.BlockSpec((1, D), lambda i: (0, 0))],
        out_specs=[row, row, pl.BlockSpec((1, D), lambda i: (0, 0)), pl.BlockSpec((8, LANE), lambda i: (0, 0))],
        out_shape=[jax.ShapeDtypeStruct((T, D), BF16), jax.ShapeDtypeStruct((T, D), F32),
                   jax.ShapeDtypeStruct((1, D), F32), jax.ShapeDtypeStruct((8, LANE), F32)],
        compiler_params=pltpu.CompilerParams(dimension_semantics=("arbitrary",), vmem_limit_bytes=VMEM_BIG),
    )(out, x2, tgt, g_post)


def adamw(name, w, m, v, parts):
    rows = w.shape[0]
    br = 1024 if rows % 1024 == 0 else rows
    npart = len(parts)

    def body(w_ref, m_ref, v_ref, *rest):
        g = rest[0][...].astype(F32)
        for p in rest[1:npart]:
            g = g + p[...].astype(F32)
        g_ref, d_ref, nm_ref, nv_ref = rest[npart:]
        mm = ADAM_B1 * m_ref[...] + (1.0 - ADAM_B1) * g
        vv = ADAM_B2 * v_ref[...] + (1.0 - ADAM_B2) * (g * g)
        m_hat = mm / (1.0 - ADAM_B1 ** ADAM_STEP)
        v_hat = vv / (1.0 - ADAM_B2 ** ADAM_STEP)
        g_ref[...] = g
        d_ref[...] = -ADAM_LR * (m_hat / (jnp.sqrt(v_hat) + ADAM_EPS) + ADAM_WD * w_ref[...])
        nm_ref[...] = mm
        nv_ref[...] = vv

    blk = pl.BlockSpec((br, LANE), lambda i: (i, 0))
    return pl.pallas_call(
        body, name=name, grid=(rows // br,),
        in_specs=[blk] * (3 + npart), out_specs=[blk] * 4,
        out_shape=[jax.ShapeDtypeStruct((rows, LANE), F32)] * 4,
        compiler_params=pltpu.CompilerParams(dimension_semantics=("parallel",)),
    )(w, m, v, *parts)


def sum4(recv):
    rows = recv.shape[1]
    br = 1024

    def body(r_ref, o_ref):
        o_ref[...] = ((r_ref[0].astype(F32) + r_ref[1].astype(F32)) + r_ref[2].astype(F32)) + r_ref[3].astype(F32)

    return pl.pallas_call(
        body, name="sum4", grid=(rows // br,),
        in_specs=[pl.BlockSpec((4, br, LANE), lambda i: (0, i, 0))],
        out_specs=pl.BlockSpec((br, LANE), lambda i: (i, 0)),
        out_shape=jax.ShapeDtypeStruct((rows, LANE), F32),
        compiler_params=pltpu.CompilerParams(dimension_semantics=("parallel",)),
    )(recv)


_ANY = pl.BlockSpec(memory_space=pl.ANY)


def xy_exchange(name, src, bcast):
    rows = src.shape[-2]

    def body(src_ref, dst_ref, send_sems, recv_sems, loc_sem):
        x, y, c = lax.axis_index("x"), lax.axis_index("y"), lax.axis_index("c")
        me = 2 * x + y
        loc = pltpu.make_async_copy(src_ref if bcast else src_ref.at[me], dst_ref.at[me], loc_sem)
        loc.start()
        sends, recvs = [], []
        for k in (1, 2, 3):
            px = 1 - x if k & 2 else x
            py = 1 - y if k & 1 else y
            peer = 2 * px + py
            s = src_ref if bcast else src_ref.at[peer]
            sends.append(pltpu.make_async_remote_copy(
                src_ref=s, dst_ref=dst_ref.at[me], send_sem=send_sems.at[k - 1], recv_sem=recv_sems.at[k - 1],
                device_id=(px, py, c), device_id_type=MESH_IDS))
            recvs.append(pltpu.make_async_remote_copy(
                src_ref=s, dst_ref=dst_ref.at[peer], send_sem=send_sems.at[k - 1], recv_sem=recv_sems.at[k - 1],
                device_id=(px, py, c), device_id_type=MESH_IDS))
        for cp in sends:
            cp.start()
        for cp in recvs:
            cp.wait_recv()
        for cp in sends:
            cp.wait_send()
        loc.wait()

    return pl.pallas_call(
        body, name=name, in_specs=[_ANY], out_specs=_ANY,
        out_shape=jax.ShapeDtypeStruct((4, rows, LANE), src.dtype),
        scratch_shapes=[pltpu.SemaphoreType.DMA((3,)), pltpu.SemaphoreType.DMA((3,)), pltpu.SemaphoreType.DMA],
    )(src)


def c_swap(name, src):
    def body(src_ref, dst_ref, send_sem, recv_sem):
        x, y, c = lax.axis_index("x"), lax.axis_index("y"), lax.axis_index("c")
        cp = pltpu.make_async_remote_copy(src_ref=src_ref, dst_ref=dst_ref, send_sem=send_sem, recv_sem=recv_sem,
                                          device_id=(x, y, 1 - c), device_id_type=MESH_IDS)
        cp.start()
        cp.wait()

    return pl.pallas_call(
        body, name=name, in_specs=[_ANY], out_specs=_ANY,
        out_shape=jax.ShapeDtypeStruct(src.shape, src.dtype),
        scratch_shapes=[pltpu.SemaphoreType.DMA, pltpu.SemaphoreType.DMA],
    )(src)


def allgather8(name, src):
    rows = src.shape[0]

    def body(src_ref, dst_ref, send_sems, recv_sems):
        x, y, c = lax.axis_index("x"), lax.axis_index("y"), lax.axis_index("c")
        me = 4 * x + 2 * y + c
        dst_ref[me] = src_ref[...]
        sends, recvs = [], []
        for k in range(1, 8):
            px = 1 - x if k & 4 else x
            py = 1 - y if k & 2 else y
            pc = 1 - c if k & 1 else c
            peer = 4 * px + 2 * py + pc
            for lst, slot in ((sends, me), (recvs, peer)):
                lst.append(pltpu.make_async_remote_copy(
                    src_ref=src_ref, dst_ref=dst_ref.at[slot], send_sem=send_sems.at[k - 1],
                    recv_sem=recv_sems.at[k - 1], device_id=(px, py, pc), device_id_type=MESH_IDS))
        for cp in sends:
            cp.start()
        for cp in recvs:
            cp.wait_recv()
        for cp in sends:
            cp.wait_send()

    vm = pl.BlockSpec(memory_space=pltpu.VMEM)
    return pl.pallas_call(
        body, name=name, in_specs=[vm], out_specs=vm,
        out_shape=jax.ShapeDtypeStruct((8, rows, LANE), src.dtype),
        scratch_shapes=[pltpu.SemaphoreType.DMA((7,)), pltpu.SemaphoreType.DMA((7,))],
    )(src)


WEIGHTS = ['g_pre', 'w_in', 'mla_q_norm', 'mla_wq_b', 'mla_kv_norm', 'mla_wkv_b', 'rwkv_mu', 'rwkv_w0_f', 'rwkv_w2_f',
           'rwkv_w0_b', 'rwkv_w2_b', 'rwkv_a0_f', 'rwkv_a2_f', 'rwkv_a0_b', 'rwkv_a2_b', 'rwkv_k_k', 'rwkv_k_a',
           'rwkv_r_k', 'rwkv_gn_g', 'rwkv_gn_b', 'w_br_mla', 'w_br_rwkv', 'w_out', 'g_post']
BIG_SHAPES = {'w_in': (D, D_IN // 4), 'mla_wq_b': (Q_RANK, 384), 'mla_wkv_b': (KV_RANK, 512),
              'rwkv_w2_f': (LORA, 256), 'rwkv_w2_b': (LORA, 256), 'rwkv_a2_f': (LORA, 256), 'rwkv_a2_b': (LORA, 256),
              'w_br_mla': (RW, 512), 'w_br_rwkv': (RW, 512), 'w_out': (512, D)}
BIG = list(BIG_SHAPES)
SMALL = [n for n in WEIGHTS if n not in BIG_SHAPES]
SMALL_SHAPES = {'g_pre': (D,), 'mla_q_norm': (Q_RANK,), 'mla_kv_norm': (KV_RANK,), 'rwkv_mu': (3456,),
                'rwkv_w0_f': (RW,), 'rwkv_w0_b': (RW,), 'rwkv_a0_f': (RW,), 'rwkv_a0_b': (RW,), 'rwkv_k_k': (RW,),
                'rwkv_k_a': (RW,), 'rwkv_r_k': (RH, RN), 'rwkv_gn_g': (RW,), 'rwkv_gn_b': (RW,), 'g_post': (D,)}
SMALL_LEN = sum(int(np.prod(s)) for s in SMALL_SHAPES.values())
SMALL_ROWS = 144


def _pack_big(d):
    return jnp.concatenate([d[n].reshape(-1, LANE) for n in BIG], axis=0)


def _unpack_big(packed):
    out, o = {}, 0
    for n in BIG:
        shp = BIG_SHAPES[n]
        sz = shp[0] * shp[1] // LANE
        out[n] = packed[o:o + sz].reshape(shp)
        o += sz
    return out


def _gathered(ag):
    out, o = {}, 0
    for n in BIG:
        shp = BIG_SHAPES[n]
        sz = shp[0] * shp[1] // LANE
        blk = ag[:, o:o + sz].reshape(4, *shp)
        out[n] = jnp.concatenate([blk[j] for j in range(4)], axis=0 if n == 'w_out' else 1)
        o += sz
    return out


def _shards(n, g):
    if n == 'w_out':
        return [g[j * 512:(j + 1) * 512] for j in range(4)]
    w = BIG_SHAPES[n][1]
    return [g[:, j * w:(j + 1) * w] for j in range(4)]


def _pack_small(d, extra=None):
    flat = jnp.concatenate([d[n].reshape(-1) for n in SMALL] + ([extra.reshape(-1)] if extra is not None else []))
    return jnp.pad(flat, (0, SMALL_ROWS * LANE - flat.shape[0])).reshape(SMALL_ROWS, LANE)


def _unpack_small(packed):
    flat, out, o = packed.reshape(-1), {}, 0
    for n in SMALL:
        sz = int(np.prod(SMALL_SHAPES[n]))
        out[n] = flat[o:o + sz].reshape(SMALL_SHAPES[n])
        o += sz
    return out


def _perm_w_in(w):
    z = lambda n: jnp.zeros((w.shape[0], n), w.dtype)
    lora = []
    for i in range(4):
        lora += [w[:, 4160 + LORA * i:4160 + LORA * (i + 1)], z(LANE - LORA)]
    return jnp.concatenate([w[:, 0:1024], w[:, 1088:4160], w[:, 4544:D_IN]] + lora
                           + [w[:, 1024:1056], z(96), w[:, 1056:1088], z(96)], axis=1)


def _unperm_w_in(g):
    lora = [g[:, OFF_LORA + LANE * i:OFF_LORA + LANE * i + LORA] for i in range(4)]
    return jnp.concatenate([g[:, 0:1024], g[:, OFF_KR:OFF_KR + 32], g[:, OFF_KR + LANE:OFF_KR + LANE + 32],
                            g[:, 1024:4096]] + lora + [g[:, 4096:OFF_LORA]], axis=1)


def _perm_wq(w):
    w3 = w.reshape(Q_RANK, HEADS, NOPE + ROPE)
    return jnp.concatenate([w3[:, :, :NOPE].reshape(Q_RANK, -1), w3[:, :, NOPE:NOPE + 32].reshape(Q_RANK, -1),
                            w3[:, :, NOPE + 32:].reshape(Q_RANK, -1)], axis=1)


def _unperm_wq(g):
    return jnp.concatenate([g[:, :1024].reshape(Q_RANK, HEADS, NOPE), g[:, 1024:1280].reshape(Q_RANK, HEADS, 32),
                            g[:, 1280:].reshape(Q_RANK, HEADS, 32)], axis=2).reshape(Q_RANK, -1)


def _perm_wkv(w):
    w3 = w.reshape(KV_RANK, HEADS, NOPE + VDIM)
    return jnp.concatenate([w3[:, :, :NOPE].reshape(KV_RANK, -1), w3[:, :, NOPE:].reshape(KV_RANK, -1)], axis=1)


def _unperm_wkv(g):
    return jnp.concatenate([g[:, :1024].reshape(KV_RANK, HEADS, NOPE), g[:, 1024:].reshape(KV_RANK, HEADS, VDIM)],
                           axis=2).reshape(KV_RANK, -1)


def _pad_rows(w):
    return jnp.pad(w, ((0, LANE - LORA), (0, 0)))


def _perm_mu(mu):
    parts = [mu[:3072]]
    for i in range(4):
        parts += [mu[3072 + LORA * i:3072 + LORA * (i + 1)], jnp.zeros((LANE - LORA,), mu.dtype)]
    return jnp.concatenate(parts).reshape(1, NLERP)


def _unperm_mu(g):
    g = g.reshape(-1)
    return jnp.concatenate([g[:3072]] + [g[3072 + LANE * i:3072 + LANE * i + LORA] for i in range(4)])


def _to_heads(t, n):
    return t.reshape(T, -1, n).transpose(1, 0, 2)


def _from_heads(t):
    return t.transpose(1, 0, 2).reshape(T, -1)


def _to_qk(n, r1, r2):
    return jnp.concatenate([n.reshape(T, HEADS, NOPE), r1.reshape(T, HEADS, 32), r2.reshape(T, HEADS, 32)],
                           axis=2).transpose(1, 0, 2)


def _from_qk(g):
    g = g.transpose(1, 0, 2)
    return g[:, :, :NOPE].reshape(T, -1), g[:, :, NOPE:NOPE + 32].reshape(T, -1), g[:, :, NOPE + 32:].reshape(T, -1)


def _constants():
    g2 = np.kron(np.eye(2, dtype=np.float32), np.ones((RN, RN), np.float32))
    e = np.zeros((LANE, 256), np.float32)
    for h in range(HEADS):
        e[np.arange(32), h * 32 + np.arange(32)] = 1.0
    pos = jnp.arange(T, dtype=F32)
    inv_freq = jnp.power(ROPE_THETA, -jnp.arange(0, ROPE, 2, dtype=F32) / ROPE)
    ang = pos[:, None] * inv_freq[None, :]
    cos, sin = jnp.cos(ang), jnp.sin(ang)
    padk = lambda t: jnp.pad(t, ((0, 0), (0, LANE - 32)))
    return (jnp.asarray(g2, BF16), jnp.asarray(e, BF16), jnp.tile(cos, (1, HEADS)), jnp.tile(sin, (1, HEADS)),
            padk(cos), padk(sin))


def _step(x, tgt, w, m, v):
    x2, tgt2 = x.reshape(T, D), tgt.reshape(T, D)
    g2, e_mat, cosq, sinq, cosk, sink = _constants()
    row = lambda n: w[n].reshape(1, -1)

    ag = xy_exchange("gather_weights", _pack_big({n: w[n].astype(BF16) for n in BIG}), bcast=True)
    full = _gathered(ag)
    wp = _perm_w_in(full['w_in'])
    wq = _perm_wq(full['mla_wq_b']).astype(F32)
    wkv = _perm_wkv(full['mla_wkv_b']).astype(F32)
    lora_w = [_pad_rows(full[n]).astype(F32) for n in ('rwkv_w2_f', 'rwkv_w2_b', 'rwkv_a2_f', 'rwkv_a2_b')]
    mu_p = _perm_mu(w['rwkv_mu'])

    st_pre = Stage("pre", f_pre, [(D, BF16), (D, None)], 256, [0], [0], [F32])
    st_mla = Stage("mla", f_mla, [(1024, BF16), (256, BF16), (256, BF16), (1024, BF16), (256, BF16), (256, BF16),
                                  (1024, BF16)], 256, [0, 1, 2], [0, 1, 2, 3], [BF16] * 3)
    st_rpre = Stage("rwkv_pre", f_rwkv_pre, [(RW, F32)] * 9, 128, [0, 1, 2, 3], list(range(10)), [F32] * 4)
    st_rpost = Stage("rwkv_post", f_rwkv_post, [(RW, BF16)], 256, [0, 2, 3, 4, 5, 6], [0, 1, 2],
                     [F32, F32, F32, F32, F32, BF16])
    st_gate = Stage("gate", f_gate, [(RW, BF16)], 256, [0, 1], [], [F32, BF16])
    st_merge = Stage("merge", f_merge, [(D, BF16)], 256, [0, 1, 2, 3], [], [BF16] * 4)

    pre_rows, pre_par = [(x2, D, 0)], [row('g_pre')]
    (h,) = st_pre.fwd(pre_rows, pre_par)
    proj = matmul("mm_in", h, wp, "nn")

    mla_rows = [(proj, 512, OFF_QA // 512), (proj, 512, OFF_KVA // 512), (proj, 256, OFF_KR // 256),
                (cosq, 256, 0), (sinq, 256, 0), (cosk, LANE, 0), (sink, LANE, 0)]
    mla_par = [row('mla_q_norm'), row('mla_kv_norm'), wq, wkv, e_mat]
    qn, qr1, qr2, kn, kr1, kr2, vv = st_mla.fwd(mla_rows, mla_par)
    qh, kh, vh = _to_qk(qn, qr1, qr2), _to_qk(kn, kr1, kr2), _to_heads(vv, VDIM)
    y_mla = _from_heads(attn_fwd(qh, kh, vh))

    lerp = shift_fwd(proj, mu_p)
    rpre_rows = [(lerp, RW, 0), (lerp, RW, 1), (lerp, RW, 2), (lerp, 512, 6)]
    rpre_par = [row('rwkv_w0_f'), row('rwkv_w0_b'), row('rwkv_a0_f'), row('rwkv_a0_b'), row('rwkv_k_k'),
                row('rwkv_k_a')] + lora_w + [g2]
    r_, v_, lwf, lwb, kf, kb, an, bf_, bb_ = st_rpre.fwd(rpre_rows, rpre_par)
    hd = lambda t: _to_heads(t, RN)
    fin = [hd(t) for t in (r_, lwf, kf, v_, an, bf_)]
    bin_ = [fin[0], hd(lwb), hd(kb), fin[3], fin[4], hd(bb_)]
    yf, h0f = scan_fwd("scan_f", *fin, reverse=False)
    yb, h0b = scan_fwd("scan_b", *bin_, reverse=True)
    rpost_rows = [(_from_heads(yf), RW, 0), (_from_heads(yb), RW, 0), (r_, RW, 0), (kf, RW, 0), (kb, RW, 0),
                  (v_, RW, 0), (proj, RW, OFF_ZR // RW)]
    rpost_par = [row('rwkv_gn_g'), row('rwkv_gn_b'), row('rwkv_r_k'), g2]
    (gr,) = st_rpost.fwd(rpost_rows, rpost_par)
    gate_rows = [(y_mla, RW, 0), (proj, RW, OFF_ZM // RW)]
    (gm,) = st_gate.fwd(gate_rows, [])
    um = matmul("mm_br_mla", gm, full['w_br_mla'], "nn")
    ur = matmul("mm_br_rwkv", gr, full['w_br_rwkv'], "nn")
    merge_rows = [(um, D, 0), (ur, D, 0), (proj, D, OFF_GM // D), (proj, D, OFF_GR // D)]
    (merged,) = st_merge.fwd(merge_rows, [])
    out = matmul("mm_out", merged, full['w_out'], "nn")
    d_out, dy, dg_post, loss_blk = loss_stage(out, x2, tgt2, row('g_post'))

    gw = {'g_post': dg_post}
    d_merged = matmul("mm_out_dx", d_out, full['w_out'], "nt")
    gw['w_out'] = matmul("mm_out_dw", merged, d_out, "tn")
    (d_um, d_ur, d_gm, d_gr), _ = st_merge.bwd(merge_rows, [], [[(d_merged, D, 0)]])
    d_gmla = matmul("mm_br_mla_dx", d_um, full['w_br_mla'], "nt")
    gw['w_br_mla'] = matmul("mm_br_mla_dw", gm, d_um, "tn")
    d_grw = matmul("mm_br_rwkv_dx", d_ur, full['w_br_rwkv'], "nt")
    gw['w_br_rwkv'] = matmul("mm_br_rwkv_dw", gr, d_ur, "tn")
    (d_ymla, d_zm), _ = st_gate.bwd(gate_rows, [], [[(d_gmla, RW, 0)]])
    (d_y, d_r3, d_kf2, d_kb2, d_v3, d_zr), (gw['rwkv_gn_g'], gw['rwkv_gn_b'], d_rk) = st_rpost.bwd(
        rpost_rows, rpost_par, [[(d_grw, RW, 0)]])
    gw['rwkv_r_k'] = d_rk
    dyh = hd(d_y)
    sf = [_from_heads(t) for t in scan_bwd("scan_f_bwd", *fin, h0f, dyh, reverse=False)]
    sb = [_from_heads(t) for t in scan_bwd("scan_b_bwd", *bin_, h0b, dyh, reverse=True)]
    c = lambda *ts: [(t, RW, 0) for t in ts]
    rpre_cts = [c(sf[0], sb[0], d_r3), c(sf[3], sb[3], d_v3), c(sf[1]), c(sb[1]), c(sf[2], d_kf2), c(sb[2], d_kb2),
                c(sf[4], sb[4]), c(sf[5]), c(sb[5])]
    d_lerp_rows, rpre_g = st_rpre.bwd(rpre_rows, rpre_par, rpre_cts)
    for n, gval in zip(('rwkv_w0_f', 'rwkv_w0_b', 'rwkv_a0_f', 'rwkv_a0_b', 'rwkv_k_k', 'rwkv_k_a'), rpre_g[:6]):
        gw[n] = gval
    for n, gval in zip(('rwkv_w2_f', 'rwkv_w2_b', 'rwkv_a2_f', 'rwkv_a2_b'), rpre_g[6:]):
        gw[n] = gval[:LORA]
    d_lerp, d_mu = shift_bwd(proj, mu_p, jnp.concatenate(d_lerp_rows, axis=1))
    gw['rwkv_mu'] = _unperm_mu(d_mu)

    dqh, dkh, dvh = attn_bwd(qh, kh, vh, _to_heads(d_ymla, VDIM))
    mla_cts = [[(t, t.shape[1], 0)] for t in (*_from_qk(dqh), *_from_qk(dkh), _from_heads(dvh))]
    (d_qa, d_kva, d_kr), (gw['mla_q_norm'], gw['mla_kv_norm'], d_wq, d_wkv) = st_mla.bwd(mla_rows, mla_par, mla_cts)
    gw['mla_wq_b'], gw['mla_wkv_b'] = _unperm_wq(d_wq), _unperm_wkv(d_wkv)

    dproj = jnp.concatenate([d_qa, d_kva, d_lerp[:, :3072], d_zm, d_zr, d_gm, d_gr, d_lerp[:, 3072:], d_kr], axis=1)
    dh = matmul("mm_in_dx", dproj, wp, "nt")
    gw['w_in'] = _unperm_w_in(matmul("mm_in_dw", h, dproj, "tn"))
    (grad_x,), (gw['g_pre'],) = st_pre.bwd(pre_rows, pre_par, [[(dh, D, 0)], [(dy, D, 0)]])

    send = jnp.stack([_pack_big({n: _shards(n, gw[n])[j].astype(BF16) for n in BIG}) for j in range(4)])
    mine = sum4(xy_exchange("scatter_grads", send, bcast=False))
    theirs = c_swap("swap_cores", mine)
    big = adamw("adamw_big", _pack_big(w), _pack_big(m), _pack_big(v), [mine, theirs])
    parts = allgather8("gather_small", _pack_small(gw, loss_blk[0, :1]))
    small = adamw("adamw_small", _pack_small(w), _pack_small(m), _pack_small(v), [parts[i] for i in range(8)])

    outs = []
    for b_arr, s_arr in zip(big, small):
        d = {**_unpack_big(b_arr), **_unpack_small(s_arr)}
        outs.append([d[n] for n in WEIGHTS])
    loss = small[0][SMALL_LEN // LANE, 0]
    return (loss, grad_x.reshape(1, T, D), *outs[0], *outs[1], *outs[2], *outs[3])


def kernel(x, g_pre, w_in, mla_q_norm, mla_wq_b, mla_kv_norm, mla_wkv_b, rwkv_mu, rwkv_w0_f, rwkv_w2_f, rwkv_w0_b, rwkv_w2_b, rwkv_a0_f, rwkv_a2_f, rwkv_a0_b, rwkv_a2_b, rwkv_k_k, rwkv_k_a, rwkv_r_k, rwkv_gn_g, rwkv_gn_b, w_br_mla, w_br_rwkv, w_out, g_post, loss_target, m_g_pre, m_w_in, m_mla_q_norm, m_mla_wq_b, m_mla_kv_norm, m_mla_wkv_b, m_rwkv_mu, m_rwkv_w0_f, m_rwkv_w2_f, m_rwkv_w0_b, m_rwkv_w2_b, m_rwkv_a0_f, m_rwkv_a2_f, m_rwkv_a0_b, m_rwkv_a2_b, m_rwkv_k_k, m_rwkv_k_a, m_rwkv_r_k, m_rwkv_gn_g, m_rwkv_gn_b, m_w_br_mla, m_w_br_rwkv, m_w_out, m_g_post, v_g_pre, v_w_in, v_mla_q_norm, v_mla_wq_b, v_mla_kv_norm, v_mla_wkv_b, v_rwkv_mu, v_rwkv_w0_f, v_rwkv_w2_f, v_rwkv_w0_b, v_rwkv_w2_b, v_rwkv_a0_f, v_rwkv_a2_f, v_rwkv_a0_b, v_rwkv_a2_b, v_rwkv_k_k, v_rwkv_k_a, v_rwkv_r_k, v_rwkv_gn_g, v_rwkv_gn_b, v_w_br_mla, v_w_br_rwkv, v_w_out, v_g_post):
    given = dict(locals())
    w = {n: given[n] for n in WEIGHTS}
    m = {n: given['m_' + n] for n in WEIGHTS}
    v = {n: given['v_' + n] for n in WEIGHTS}
    return _step(x, loss_target, w, m, v)
```

```python
import functools
import math

import numpy as np
import jax
import jax.numpy as jnp
from jax import lax
from jax.experimental import pallas as pl
from jax.experimental.pallas import tpu as pltpu

F32, BF16 = jnp.float32, jnp.bfloat16
MESH_IDS = pl.DeviceIdType.MESH

D = 2048
T = 2048
HEADS = 8
Q_RANK = 512
KV_RANK = 512
NOPE = 128
ROPE = 64
VDIM = 128
RW = 1024
RH = 16
RN = 64
LORA = 96
D_IN = 10688
NORM_EPS = 1e-6
GN_EPS = 64e-5
ROPE_THETA = 10000.0
ADAM_LR, ADAM_B1, ADAM_B2, ADAM_EPS, ADAM_WD, ADAM_STEP = 0.001, 0.9, 0.999, 1e-08, 0.01, 10

LANE = 128
VMEM_BIG = 56 * 2**20

NP = 11008
OFF_QA, OFF_KVA, OFF_RKV, OFF_ZM, OFF_ZR, OFF_GM, OFF_GR, OFF_LORA, OFF_KR = 0, 512, 1024, 4096, 5120, 6144, 8192, 10240, 10752
NLERP = 3584

CHUNK = 64
NCH = T // CHUNK


def _dg(a, b, ca, cb, batch=False, prec=None):
    bd = ((0,), (0,)) if batch else ((), ())
    return lax.dot_general(a, b, (((ca,), (cb,)), bd), precision=prec, preferred_element_type=F32)


@jax.custom_vjp
def bdot(a, b):
    return _dg(a.astype(BF16), b.astype(BF16), 1, 0)


def _bdot_fwd(a, b):
    return bdot(a, b), (a, b)


def _bdot_bwd(res, g):
    a, b = res
    gb = g.astype(BF16)
    da = _dg(gb, b.astype(BF16), 1, 1)
    db = _dg(a.astype(BF16), gb, 0, 0)
    return da.astype(a.dtype), db.astype(b.dtype)


bdot.defvjp(_bdot_fwd, _bdot_bwd)


def _split(x):
    hi = x.astype(BF16)
    lo = (x - hi.astype(F32)).astype(BF16)
    return hi, lo


@jax.custom_vjp
def gsum(x, g2):
    hi, lo = _split(x)
    return _dg(hi, g2, 1, 0) + _dg(lo, g2, 1, 0)


def _gsum_fwd(x, g2):
    return gsum(x, g2), g2


def _gsum_bwd(g2, g):
    hi, lo = _split(g)
    return _dg(hi, g2, 1, 1) + _dg(lo, g2, 1, 1), jnp.zeros_like(g2)


gsum.defvjp(_gsum_fwd, _gsum_bwd)


def headsum(x, g2):
    return jnp.concatenate([gsum(x[:, i * LANE:(i + 1) * LANE], g2) for i in range(x.shape[1] // LANE)], axis=1)


def _terms(x, n):
    out = []
    for i in range(n):
        t = x.astype(BF16)
        out.append(t)
        if i < n - 1:
            x = x - t.astype(F32)
    return out


def _bmm(a, b, ca, cb, na, nb):
    acc = None
    for i, ai in enumerate(_terms(a, na)):
        for j, bj in enumerate(_terms(b, nb)):
            if i + j < max(na, nb):
                p = _dg(ai, bj, ca, cb, True)
                acc = p if acc is None else acc + p
    return acc


_NN, _NT, _TN = (2, 1), (2, 2), (1, 1)


def _make_dots(nf, nb_nn, nb_nt, nb_tn):
    @jax.custom_vjp
    def nn(a, b):
        return _bmm(a, b, *_NN, nf, nf)

    @jax.custom_vjp
    def nt(a, b):
        return _bmm(a, b, *_NT, nf, nf)

    @jax.custom_vjp
    def tn(a, b):
        return _bmm(a, b, *_TN, nf, nf)

    nn.defvjp(lambda a, b: (nn(a, b), (a, b)),
              lambda r, g: (_bmm(g, r[1], *_NT, nb_nn, nb_nn), _bmm(r[0], g, *_TN, nb_nn, nb_nn)))
    nt.defvjp(lambda a, b: (nt(a, b), (a, b)),
              lambda r, g: (_bmm(g, r[1], *_NN, nb_nt, nb_nt), _bmm(g, r[0], *_TN, nb_nt, nb_nt)))
    tn.defvjp(lambda a, b: (tn(a, b), (a, b)),
              lambda r, g: (_bmm(r[1], g, *_NT, nb_tn, nb_tn), _bmm(r[0], g, *_NN, nb_tn, nb_tn)))
    return nn, nt, tn


_SCAN_NF, _SCAN_NB = 1, 1
nn, nt, tn = _make_dots(_SCAN_NF, 1, 2, 1)


@jax.custom_vjp
def cumdot(ones, x):
    return _bmm(ones, x, *_NN, 1, 3)


cumdot.defvjp(lambda o, x: (cumdot(o, x), o), lambda o, g: (jnp.zeros_like(o), _bmm(o, g, *_TN, 1, 3)))


def _solve_powers(l):
    pw = [l]
    for _ in range(int(math.log2(l.shape[-1])) - 1):
        pw.append(_bmm(pw[-1], pw[-1], *_NN, _SCAN_NF, _SCAN_NF))
    return pw


@jax.custom_vjp
def tri_solve(l, rhs):
    x = rhs
    for p in _solve_powers(l):
        x = x + _bmm(p, x, *_NN, _SCAN_NF, _SCAN_NF)
    return x


def _tri_solve_fwd(l, rhs):
    pw = _solve_powers(l)
    x = rhs
    for p in pw:
        x = x + _bmm(p, x, *_NN, _SCAN_NF, _SCAN_NF)
    return x, (pw, x)


def _tri_solve_bwd(res, g):
    pw, x = res
    y = g
    for p in pw:
        y = y + _bmm(p, y, *_TN, _SCAN_NB, _SCAN_NB)
    return _bmm(y, x, *_NT, _SCAN_NB, _SCAN_NB), y


tri_solve.defvjp(_tri_solve_fwd, _tri_solve_bwd)


def _rms(x, g):
    return x * lax.rsqrt(jnp.mean(x * x, axis=-1, keepdims=True) + NORM_EPS) * g


def _softplus(x):
    pos = x > 0
    return jnp.where(pos, x, 0.0) + jnp.log(1.0 + jnp.exp(-jnp.where(pos, x, -x)))


def _silu(z):
    return z * jax.nn.sigmoid(z)


def _tile(n, cands):
    for c in cands:
        if n % c == 0:
            return c
    raise ValueError(n)


_MM_OPERAND_BYTES = 24 * 2**20


def _mm_tiles(m, n, k):
    tn_ = _tile(n, (512, 256, 128))
    best = None
    for tm in (2048, 1024, 512, 256, 128):
        if m % tm:
            continue
        for d in range(k // LANE, 0, -1):
            tk = LANE * d
            if k % tk == 0 and 4 * tk * (tm + tn_) <= _MM_OPERAND_BYTES:
                if best is None or tm * tk > best[0] * best[2]:
                    best = (tm, tn_, tk)
                break
    return best


def matmul(name, a, b, mode, out_dtype=F32):
    if mode == "nn":
        (m, k), n = a.shape, b.shape[1]
    elif mode == "nt":
        (m, k), n = a.shape, b.shape[0]
    else:
        (k, m), n = a.shape, b.shape[1]
    tm, tn_, tk = _mm_tiles(m, n, k)
    nk = k // tk
    if mode == "nn":
        a_spec = pl.BlockSpec((tm, tk), lambda i, j, kk: (i, kk))
        b_spec = pl.BlockSpec((tk, tn_), lambda i, j, kk: (kk, j))
        ca, cb = 1, 0
    elif mode == "nt":
        a_spec = pl.BlockSpec((tm, tk), lambda i, j, kk: (i, kk))
        b_spec = pl.BlockSpec((tn_, tk), lambda i, j, kk: (j, kk))
        ca, cb = 1, 1
    else:
        a_spec = pl.BlockSpec((tk, tm), lambda i, j, kk: (kk, i))
        b_spec = pl.BlockSpec((tk, tn_), lambda i, j, kk: (kk, j))
        ca, cb = 0, 0

    def body(a_ref, b_ref, o_ref, *acc):
        part = _dg(a_ref[...].astype(BF16), b_ref[...].astype(BF16), ca, cb)
        if nk == 1:
            o_ref[...] = part.astype(o_ref.dtype)
            return
        acc_ref, kk = acc[0], pl.program_id(2)

        @pl.when(kk == 0)
        def _():
            acc_ref[...] = part

        @pl.when(kk > 0)
        def _():
            acc_ref[...] += part

        @pl.when(kk == nk - 1)
        def _():
            o_ref[...] = acc_ref[...].astype(o_ref.dtype)

    return pl.pallas_call(
        body, name=name, grid=(m // tm, n // tn_, nk),
        in_specs=[a_spec, b_spec],
        out_specs=pl.BlockSpec((tm, tn_), lambda i, j, kk: (i, j)),
        out_shape=jax.ShapeDtypeStruct((m, n), out_dtype),
        scratch_shapes=[pltpu.VMEM((tm, tn_), F32)] if nk > 1 else [],
        compiler_params=pltpu.CompilerParams(dimension_semantics=("parallel", "parallel", "arbitrary"),
                                             vmem_limit_bytes=VMEM_BIG),
    )(a, b)


def _rspec(tr, width, blk):
    return pl.BlockSpec((tr, width), lambda i: (i, blk))


def _full_spec(arr):
    return pl.BlockSpec(arr.shape, lambda i: (0,) * arr.ndim)


class Stage:
    def __init__(self, name, f, outs, tr, diff_rows, diff_params, drow_dtypes):
        self.name, self.f, self.outs, self.tr = name, f, outs, tr
        self.diff_rows, self.diff_params, self.drow_dtypes = diff_rows, diff_params, drow_dtypes

    def fwd(self, rows, params):
        f, nr, npar = self.f, len(rows), len(params)
        stored = [(w, dt) for (w, dt) in self.outs if dt is not None]
        keep = [i for i, (w, dt) in enumerate(self.outs) if dt is not None]

        def body(*refs):
            vals = f(*[r[...].astype(F32) for r in refs[:nr]], *[p[...] for p in refs[nr:nr + npar]])
            for o_ref, i in zip(refs[nr + npar:], keep):
                o_ref[...] = vals[i].astype(o_ref.dtype)

        return pl.pallas_call(
            body, name=self.name + "_fwd", grid=(T // self.tr,),
            in_specs=[_rspec(self.tr, w, b) for (_, w, b) in rows] + [_full_spec(p) for p in params],
            out_specs=[_rspec(self.tr, w, 0) for (w, _) in stored],
            out_shape=[jax.ShapeDtypeStruct((T, w), dt) for (w, dt) in stored],
            compiler_params=pltpu.CompilerParams(dimension_semantics=("arbitrary",), vmem_limit_bytes=VMEM_BIG),
        )(*[r[0] for r in rows], *params)

    def bwd(self, rows, params, cts):
        f, nr, npar = self.f, len(rows), len(params)
        dr_idx, dp_idx = self.diff_rows, self.diff_params
        flat_cts = [c for lst in cts for c in lst]
        nct = len(flat_cts)

        def body(*refs):
            row_refs, par_refs = refs[:nr], refs[nr:nr + npar]
            ct_refs = refs[nr + npar:nr + npar + nct]
            drow_refs = refs[nr + npar + nct:nr + npar + nct + len(dr_idx)]
            dpar_refs = refs[nr + npar + nct + len(dr_idx):]
            row_vals = [r[...].astype(F32) for r in row_refs]
            par_vals = [p[...] for p in par_refs]

            def g(*dv):
                rv, pv = list(row_vals), list(par_vals)
                for j, i in enumerate(dr_idx):
                    rv[i] = dv[j]
                for j, i in enumerate(dp_idx):
                    pv[i] = dv[len(dr_idx) + j]
                return f(*rv, *pv)

            _, vjp = jax.vjp(g, *[row_vals[i] for i in dr_idx], *[par_vals[i] for i in dp_idx])
            ct_vals, pos = [], 0
            for lst in cts:
                acc = ct_refs[pos][...].astype(F32)
                for q in range(1, len(lst)):
                    acc = acc + ct_refs[pos + q][...].astype(F32)
                pos += len(lst)
                ct_vals.append(acc)
            grads = vjp(tuple(ct_vals))
            for j, r in enumerate(drow_refs):
                r[...] = grads[j].astype(r.dtype)

            @pl.when(pl.program_id(0) == 0)
            def _():
                for r in dpar_refs:
                    r[...] = jnp.zeros_like(r)

            for j, r in enumerate(dpar_refs):
                r[...] += grads[len(dr_idx) + j].astype(F32)

        drow_shapes = [jax.ShapeDtypeStruct((T, rows[i][1]), dt) for i, dt in zip(dr_idx, self.drow_dtypes)]
        dpar_shapes = [jax.ShapeDtypeStruct(params[i].shape, F32) for i in dp_idx]
        res = pl.pallas_call(
            body, name=self.name + "_bwd", grid=(T // self.tr,),
            in_specs=[_rspec(self.tr, w, b) for (_, w, b) in rows] + [_full_spec(p) for p in params]
            + [_rspec(self.tr, w, b) for (_, w, b) in flat_cts],
            out_specs=[_rspec(self.tr, rows[i][1], 0) for i in dr_idx] + [_full_spec(params[i]) for i in dp_idx],
            out_shape=drow_shapes + dpar_shapes,
            compiler_params=pltpu.CompilerParams(dimension_semantics=("arbitrary",), vmem_limit_bytes=VMEM_BIG),
        )(*[r[0] for r in rows], *params, *[c[0] for c in flat_cts])
        return res[:len(dr_idx)], res[len(dr_idx):]


def f_pre(x, g):
    return _rms(x, g), x


def f_mla(q_a, kv_a, kr, cosq, sinq, cosk, sink, gq, gkv, wq, wkv, e):
    q = bdot(_rms(q_a, gq), wq)
    kv = bdot(_rms(kv_a, gkv), wkv)
    t1, t2 = q[:, 1024:1280], q[:, 1280:1536]
    k1, k2 = kr[:, :LANE], kr[:, LANE:]
    kr1 = k1 * cosk - k2 * sink
    kr2 = k1 * sink + k2 * cosk
    return (q[:, :1024], t1 * cosq - t2 * sinq, t1 * sinq + t2 * cosq,
            kv[:, :1024], bdot(kr1, e), bdot(kr2, e), kv[:, 1024:])


def f_rwkv_pre(r, k, v, lora, w0f, w0b, a0f, a0b, kkw, kaw, w2f, w2b, a2f, a2b, g2):
    wdf, wdb, adf, adb = (lora[:, i * LANE:(i + 1) * LANE] for i in range(4))

    def logdecay(w0, wd, w2):
        z = w0 + bdot(jnp.tanh(wd), w2)
        return -jnp.exp(-_softplus(-z) - 0.5)

    a_f = jax.nn.sigmoid(a0f + bdot(adf, a2f))
    a_b = jax.nn.sigmoid(a0b + bdot(adb, a2b))
    kk = k * kkw
    kk = kk / jnp.maximum(jnp.sqrt(headsum(kk * kk, g2)), 1e-12)
    return (r, v, logdecay(w0f, wdf, w2f), logdecay(w0b, wdb, w2b),
            k * (1.0 + (a_f - 1.0) * kaw), k * (1.0 + (a_b - 1.0) * kaw), -kk, kk * a_f, kk * a_b)


def f_rwkv_post(yf, yb, r, kf, kb, v, z, gng, gnb, rk, g2):
    y = yf + yb
    mu = headsum(y, g2) * (1.0 / RN)
    d = y - mu
    var = headsum(d * d, g2) * (1.0 / RN)
    yn = d * lax.rsqrt(var + GN_EPS) * gng + gnb
    bonus = headsum(r * (kf + kb) * rk, g2) * v
    return ((yn + bonus) * _silu(z),)


def f_gate(y, z):
    return (y * _silu(z),)


def f_merge(um, ur, gm, gr):
    return (jax.nn.sigmoid(gm) * um + jax.nn.sigmoid(gr) * ur,)


_SHIFT_W = 256


def _lerp_colblock(j):
    return jnp.where(j < 3072 // _SHIFT_W, OFF_RKV // _SHIFT_W + j, OFF_LORA // _SHIFT_W + j - 3072 // _SHIFT_W)


def _nbr_mean(x):
    row = lax.broadcasted_iota(jnp.int32, x.shape, 0)
    up = jnp.where(row == 0, 0.0, pltpu.roll(x, 1, 0))
    dn = jnp.where(row == T - 1, 0.0, pltpu.roll(x, T - 1, 0))
    return 0.5 * (up + dn)


def shift_fwd(proj, mu):
    def body(x_ref, mu_ref, o_ref):
        x = x_ref[...]
        o_ref[...] = x + mu_ref[...] * (_nbr_mean(x) - x)

    return pl.pallas_call(
        body, name="shift_fwd", grid=(NLERP // _SHIFT_W,),
        in_specs=[pl.BlockSpec((T, _SHIFT_W), lambda j: (0, _lerp_colblock(j))),
                  pl.BlockSpec((1, _SHIFT_W), lambda j: (0, j))],
        out_specs=pl.BlockSpec((T, _SHIFT_W), lambda j: (0, j)),
        out_shape=jax.ShapeDtypeStruct((T, NLERP), F32),
        compiler_params=pltpu.CompilerParams(dimension_semantics=("parallel",), vmem_limit_bytes=VMEM_BIG),
    )(proj, mu)


def shift_bwd(proj, mu, g):
    def body(x_ref, mu_ref, g_ref, dx_ref, dmu_ref):
        x, gv = x_ref[...], g_ref[...]
        dmu_ref[...] = jnp.sum(gv * (_nbr_mean(x) - x), axis=0, keepdims=True)
        gm = gv * mu_ref[...]
        dx_ref[...] = (gv - gm + _nbr_mean(gm)).astype(dx_ref.dtype)

    col = pl.BlockSpec((T, _SHIFT_W), lambda j: (0, j))
    vec = pl.BlockSpec((1, _SHIFT_W), lambda j: (0, j))
    return pl.pallas_call(
        body, name="shift_bwd", grid=(NLERP // _SHIFT_W,),
        in_specs=[pl.BlockSpec((T, _SHIFT_W), lambda j: (0, _lerp_colblock(j))), vec, col],
        out_specs=[col, vec],
        out_shape=[jax.ShapeDtypeStruct((T, NLERP), BF16), jax.ShapeDtypeStruct((1, NLERP), F32)],
        compiler_params=pltpu.CompilerParams(dimension_semantics=("parallel",), vmem_limit_bytes=VMEM_BIG),
    )(proj, mu, g)


_TQ = 256
_ATT_SCALE = (NOPE + ROPE) ** -0.5


def _probs(q, k):
    s = _dg(q, k, 1, 1) * _ATT_SCALE
    e = jnp.exp(s - jnp.max(s, axis=-1, keepdims=True))
    return e * (1.0 / jnp.sum(e, axis=-1, keepdims=True))


def attn_fwd(q, k, v):
    def body(q_ref, k_ref, v_ref, o_ref):
        p = _probs(q_ref[0], k_ref[0])
        o_ref[0] = _dg(p.astype(BF16), v_ref[0], 1, 0)

    dq = NOPE + ROPE
    return pl.pallas_call(
        body, name="attn_fwd", grid=(HEADS, T // _TQ),
        in_specs=[pl.BlockSpec((1, _TQ, dq), lambda h, i: (h, i, 0)),
                  pl.BlockSpec((1, T, dq), lambda h, i: (h, 0, 0)),
                  pl.BlockSpec((1, T, VDIM), lambda h, i: (h, 0, 0))],
        out_specs=pl.BlockSpec((1, _TQ, VDIM), lambda h, i: (h, i, 0)),
        out_shape=jax.ShapeDtypeStruct((HEADS, T, VDIM), F32),
        compiler_params=pltpu.CompilerParams(dimension_semantics=("parallel", "arbitrary"), vmem_limit_bytes=VMEM_BIG),
    )(q, k, v)


def attn_bwd(q, k, v, do):
    def body(q_ref, k_ref, v_ref, do_ref, dq_ref, dk_ref, dv_ref):
        @pl.when(pl.program_id(1) == 0)
        def _():
            dk_ref[...] = jnp.zeros_like(dk_ref)
            dv_ref[...] = jnp.zeros_like(dv_ref)

        qv, kv_, vv = q_ref[0], k_ref[0], v_ref[0]
        dob = do_ref[0].astype(BF16)
        p = _probs(qv, kv_)
        dv_ref[0] += _dg(p.astype(BF16), dob, 0, 0)
        dp = _dg(dob, vv, 1, 1)
        ds = (p * (dp - jnp.sum(dp * p, axis=-1, keepdims=True)) * _ATT_SCALE).astype(BF16)
        dq_ref[0] = _dg(ds, kv_, 1, 0)
        dk_ref[0] += _dg(ds, qv, 0, 0)

    dq = NOPE + ROPE
    return pl.pallas_call(
        body, name="attn_bwd", grid=(HEADS, T // _TQ),
        in_specs=[pl.BlockSpec((1, _TQ, dq), lambda h, i: (h, i, 0)),
                  pl.BlockSpec((1, T, dq), lambda h, i: (h, 0, 0)),
                  pl.BlockSpec((1, T, VDIM), lambda h, i: (h, 0, 0)),
                  pl.BlockSpec((1, _TQ, VDIM), lambda h, i: (h, i, 0))],
        out_specs=[pl.BlockSpec((1, _TQ, dq), lambda h, i: (h, i, 0)),
                   pl.BlockSpec((1, T, dq), lambda h, i: (h, 0, 0)),
                   pl.BlockSpec((1, T, VDIM), lambda h, i: (h, 0, 0))],
        out_shape=[jax.ShapeDtypeStruct((HEADS, T, dq), F32), jax.ShapeDtypeStruct((HEADS, T, dq), F32),
                   jax.ShapeDtypeStruct((HEADS, T, VDIM), F32)],
        compiler_params=pltpu.CompilerParams(dimension_semantics=("parallel", "arbitrary"), vmem_limit_bytes=VMEM_BIG),
    )(q, k, v, do)


def _chunk(r, lw, k, v, a, b, ht, *, reverse):
    hb, c, _ = r.shape
    ti = lax.broadcasted_iota(jnp.int32, (c, c), 0)
    si = lax.broadcasted_iota(jnp.int32, (c, c), 1)
    incl = (si >= ti) if reverse else (si <= ti)
    strict = (si > ti) if reverse else (si < ti)
    ones = jnp.broadcast_to(incl.astype(F32)[None], (hb, c, c))
    cum = cumdot(ones, lw)
    cum_ex = cum - lw
    tot = jnp.sum(lw, axis=1, keepdims=True)
    mid = 0.5 * tot
    rt, at = r * jnp.exp(cum - mid), a * jnp.exp(cum_ex - mid)
    einv = jnp.exp(mid - cum)
    kt, bt = k * einv, b * einv
    m_ab = jnp.where(strict, nt(at, bt), 0.0)
    m_ak = jnp.where(strict, nt(at, kt), 0.0)
    m_rb = jnp.where(incl, nt(rt, bt), 0.0)
    m_rk = jnp.where(incl, nt(rt, kt), 0.0)
    u = tri_solve(m_ab, nt(a * jnp.exp(cum_ex), ht) + nn(m_ak, v))
    y = nt(r * jnp.exp(cum), ht) + nn(m_rb, u) + nn(m_rk, v)
    eend = jnp.exp(tot - cum)
    ht_new = ht * jnp.exp(tot) + tn(u, b * eend) + tn(v, k * eend)
    return y, ht_new


_HB_F, _HB_B = 4, 2


def _split_heads(x):
    return jnp.stack([x[:, i * RN:(i + 1) * RN] for i in range(x.shape[1] // RN)])


def _merge_heads(y):
    return jnp.concatenate([y[i] for i in range(y.shape[0])], axis=1)


def _chunk_map(reverse, backward):
    flip = reverse != backward
    return (lambda g, c: (NCH - 1 - c, g)) if flip else (lambda g, c: (c, g))


def scan_fwd(name, r, lw, k, v, a, b, reverse):
    hb = _HB_F
    cmap = _chunk_map(reverse, False)

    def body(r_ref, lw_ref, k_ref, v_ref, a_ref, b_ref, y_ref, h0_ref, ht_ref):
        @pl.when(pl.program_id(1) == 0)
        def _():
            ht_ref[...] = jnp.zeros_like(ht_ref)

        ht = ht_ref[...]
        h0_ref[0] = ht
        ins = [_split_heads(x[...]) for x in (r_ref, lw_ref, k_ref, v_ref, a_ref, b_ref)]
        y, hn = _chunk(*ins, ht, reverse=reverse)
        y_ref[...] = _merge_heads(y)
        ht_ref[...] = hn

    io = pl.BlockSpec((CHUNK, hb * RN), cmap)
    return pl.pallas_call(
        body, name=name, grid=(RH // hb, NCH),
        in_specs=[io] * 6,
        out_specs=[io, pl.BlockSpec((1, hb, RN, RN), lambda g, c: (cmap(g, c)[0], g, 0, 0))],
        out_shape=[jax.ShapeDtypeStruct((T, RW), F32), jax.ShapeDtypeStruct((NCH, RH, RN, RN), F32)],
        scratch_shapes=[pltpu.VMEM((hb, RN, RN), F32)],
        compiler_params=pltpu.CompilerParams(dimension_semantics=("parallel", "arbitrary"), vmem_limit_bytes=VMEM_BIG),
    )(r, lw, k, v, a, b)


def scan_bwd(name, r, lw, k, v, a, b, h0, dy, reverse):
    hb = _HB_B
    cmap = _chunk_map(reverse, True)

    def body(r_ref, lw_ref, k_ref, v_ref, a_ref, b_ref, h0_ref, dy_ref, *rest):
        d_refs, dht_ref = rest[:6], rest[6]

        @pl.when(pl.program_id(1) == 0)
        def _():
            dht_ref[...] = jnp.zeros_like(dht_ref)

        ins = [_split_heads(x[...]) for x in (r_ref, lw_ref, k_ref, v_ref, a_ref, b_ref)]
        _, vjp = jax.vjp(functools.partial(_chunk, reverse=reverse), *ins, h0_ref[0])
        grads = vjp((_split_heads(dy_ref[...]), dht_ref[...]))
        for d_ref, gval in zip(d_refs, grads[:6]):
            d_ref[...] = _merge_heads(gval)
        dht_ref[...] = grads[6]

    io = pl.BlockSpec((CHUNK, hb * RN), cmap)
    return pl.pallas_call(
        body, name=name, grid=(RH // hb, NCH),
        in_specs=[io] * 6 + [pl.BlockSpec((1, hb, RN, RN), lambda g, c: (cmap(g, c)[0], g, 0, 0)), io],
        out_specs=[io] * 6,
        out_shape=[jax.ShapeDtypeStruct((T, RW), F32)] * 6,
        scratch_shapes=[pltpu.VMEM((hb, RN, RN), F32)],
        compiler_params=pltpu.CompilerParams(dimension_semantics=("parallel", "arbitrary"), vmem_limit_bytes=VMEM_BIG),
    )(r, lw, k, v, a, b, h0, dy)


def loss_stage(out, x2, tgt, g_post):
    tr = 256

    def body(o_ref, x_ref, t_ref, g_ref, do_ref, dy_ref, dg_ref, loss_ref):
        @pl.when(pl.program_id(0) == 0)
        def _():
            dg_ref[...] = jnp.zeros_like(dg_ref)
            loss_ref[...] = jnp.zeros_like(loss_ref)

        nrm, vjp = jax.vjp(_rms, o_ref[...], g_ref[...])
        e = x_ref[...] + nrm - t_ref[...]
        s = jnp.sum(jnp.sum(e * e, axis=1, keepdims=True), axis=0, keepdims=True)
        loss_ref[...] += jnp.broadcast_to(s * (0.5 / D), loss_ref.shape)
        dy = e * (1.0 / D)
        do, dg = vjp(dy)
        do_ref[...] = do.astype(do_ref.dtype)
        dy_ref[...] = dy
        dg_ref[...] += dg

    row = pl.BlockSpec((tr, D), lambda i: (i, 0))
    return pl.pallas_call(
        body, name="loss_stage", grid=(T // tr,),
        in_specs=[row, row, row, pl.BlockSpec((1, D), lambda i: (0, 0))],
        out_specs=[row, row, pl.BlockSpec((1, D), lambda i: (0, 0)), pl.BlockSpec((8, LANE), lambda i: (0, 0))],
        out_shape=[jax.ShapeDtypeStruct((T, D), BF16), jax.ShapeDtypeStruct((T, D), F32),
                   jax.ShapeDtypeStruct((1, D), F32), jax.ShapeDtypeStruct((8, LANE), F32)],
        compiler_params=pltpu.CompilerParams(dimension_semantics=("arbitrary",), vmem_limit_bytes=VMEM_BIG),
    )(out, x2, tgt, g_post)


def adamw(name, w, m, v, parts):
    rows = w.shape[0]
    br = 1024 if rows % 1024 == 0 else rows
    npart = len(parts)

    def body(w_ref, m_ref, v_ref, *rest):
        g = rest[0][...].astype(F32)
        for p in rest[1:npart]:
            g = g + p[...].astype(F32)
        g_ref, d_ref, nm_ref, nv_ref = rest[npart:]
        mm = ADAM_B1 * m_ref[...] + (1.0 - ADAM_B1) * g
        vv = ADAM_B2 * v_ref[...] + (1.0 - ADAM_B2) * (g * g)
        m_hat = mm / (1.0 - ADAM_B1 ** ADAM_STEP)
        v_hat = vv / (1.0 - ADAM_B2 ** ADAM_STEP)
        g_ref[...] = g
        d_ref[...] = -ADAM_LR * (m_hat / (jnp.sqrt(v_hat) + ADAM_EPS) + ADAM_WD * w_ref[...])
        nm_ref[...] = mm
        nv_ref[...] = vv

    blk = pl.BlockSpec((br, LANE), lambda i: (i, 0))
    return pl.pallas_call(
        body, name=name, grid=(rows // br,),
        in_specs=[blk] * (3 + npart), out_specs=[blk] * 4,
        out_shape=[jax.ShapeDtypeStruct((rows, LANE), F32)] * 4,
        compiler_params=pltpu.CompilerParams(dimension_semantics=("parallel",)),
    )(w, m, v, *parts)


def sum4(recv):
    rows = recv.shape[1]
    br = 1024

    def body(r_ref, o_ref):
        o_ref[...] = ((r_ref[0].astype(F32) + r_ref[1].astype(F32)) + r_ref[2].astype(F32)) + r_ref[3].astype(F32)

    return pl.pallas_call(
        body, name="sum4", grid=(rows // br,),
        in_specs=[pl.BlockSpec((4, br, LANE), lambda i: (0, i, 0))],
        out_specs=pl.BlockSpec((br, LANE), lambda i: (i, 0)),
        out_shape=jax.ShapeDtypeStruct((rows, LANE), F32),
        compiler_params=pltpu.CompilerParams(dimension_semantics=("parallel",)),
    )(recv)


_ANY = pl.BlockSpec(memory_space=pl.ANY)


def xy_exchange(name, src, bcast):
    rows = src.shape[-2]

    def body(src_ref, dst_ref, send_sems, recv_sems, loc_sem):
        x, y, c = lax.axis_index("x"), lax.axis_index("y"), lax.axis_index("c")
        me = 2 * x + y
        loc = pltpu.make_async_copy(src_ref if bcast else src_ref.at[me], dst_ref.at[me], loc_sem)
        loc.start()
        sends, recvs = [], []
        for k in (1, 2, 3):
            px = 1 - x if k & 2 else x
            py = 1 - y if k & 1 else y
            peer = 2 * px + py
            s = src_ref if bcast else src_ref.at[peer]
            sends.append(pltpu.make_async_remote_copy(
                src_ref=s, dst_ref=dst_ref.at[me], send_sem=send_sems.at[k - 1], recv_sem=recv_sems.at[k - 1],
                device_id=(px, py, c), device_id_type=MESH_IDS))
            recvs.append(pltpu.make_async_remote_copy(
                src_ref=s, dst_ref=dst_ref.at[peer], send_sem=send_sems.at[k - 1], recv_sem=recv_sems.at[k - 1],
                device_id=(px, py, c), device_id_type=MESH_IDS))
        for cp in sends:
            cp.start()
        for cp in recvs:
            cp.wait_recv()
        for cp in sends:
            cp.wait_send()
        loc.wait()

    return pl.pallas_call(
        body, name=name, in_specs=[_ANY], out_specs=_ANY,
        out_shape=jax.ShapeDtypeStruct((4, rows, LANE), src.dtype),
        scratch_shapes=[pltpu.SemaphoreType.DMA((3,)), pltpu.SemaphoreType.DMA((3,)), pltpu.SemaphoreType.DMA],
    )(src)


def c_swap(name, src):
    def body(src_ref, dst_ref, send_sem, recv_sem):
        x, y, c = lax.axis_index("x"), lax.axis_index("y"), lax.axis_index("c")
        cp = pltpu.make_async_remote_copy(src_ref=src_ref, dst_ref=dst_ref, send_sem=send_sem, recv_sem=recv_sem,
                                          device_id=(x, y, 1 - c), device_id_type=MESH_IDS)
        cp.start()
        cp.wait()

    return pl.pallas_call(
        body, name=name, in_specs=[_ANY], out_specs=_ANY,
        out_shape=jax.ShapeDtypeStruct(src.shape, src.dtype),
        scratch_shapes=[pltpu.SemaphoreType.DMA, pltpu.SemaphoreType.DMA],
    )(src)


def allgather8(name, src):
    rows = src.shape[0]

    def body(src_ref, dst_ref, send_sems, recv_sems):
        x, y, c = lax.axis_index("x"), lax.axis_index("y"), lax.axis_index("c")
        me = 4 * x + 2 * y + c
        dst_ref[me] = src_ref[...]
        sends, recvs = [], []
        for k in range(1, 8):
            px = 1 - x if k & 4 else x
            py = 1 - y if k & 2 else y
            pc = 1 - c if k & 1 else c
            peer = 4 * px + 2 * py + pc
            for lst, slot in ((sends, me), (recvs, peer)):
                lst.append(pltpu.make_async_remote_copy(
                    src_ref=src_ref, dst_ref=dst_ref.at[slot], send_sem=send_sems.at[k - 1],
                    recv_sem=recv_sems.at[k - 1], device_id=(px, py, pc), device_id_type=MESH_IDS))
        for cp in sends:
            cp.start()
        for cp in recvs:
            cp.wait_recv()
        for cp in sends:
            cp.wait_send()

    vm = pl.BlockSpec(memory_space=pltpu.VMEM)
    return pl.pallas_call(
        body, name=name, in_specs=[vm], out_specs=vm,
        out_shape=jax.ShapeDtypeStruct((8, rows, LANE), src.dtype),
        scratch_shapes=[pltpu.SemaphoreType.DMA((7,)), pltpu.SemaphoreType.DMA((7,))],
    )(src)


WEIGHTS = ['g_pre', 'w_in', 'mla_q_norm', 'mla_wq_b', 'mla_kv_norm', 'mla_wkv_b', 'rwkv_mu', 'rwkv_w0_f', 'rwkv_w2_f',
           'rwkv_w0_b', 'rwkv_w2_b', 'rwkv_a0_f', 'rwkv_a2_f', 'rwkv_a0_b', 'rwkv_a2_b', 'rwkv_k_k', 'rwkv_k_a',
           'rwkv_r_k', 'rwkv_gn_g', 'rwkv_gn_b', 'w_br_mla', 'w_br_rwkv', 'w_out', 'g_post']
BIG_SHAPES = {'w_in': (D, D_IN // 4), 'mla_wq_b': (Q_RANK, 384), 'mla_wkv_b': (KV_RANK, 512),
              'rwkv_w2_f': (LORA, 256), 'rwkv_w2_b': (LORA, 256), 'rwkv_a2_f': (LORA, 256), 'rwkv_a2_b': (LORA, 256),
              'w_br_mla': (RW, 512), 'w_br_rwkv': (RW, 512), 'w_out': (512, D)}
BIG = list(BIG_SHAPES)
SMALL = [n for n in WEIGHTS if n not in BIG_SHAPES]
SMALL_SHAPES = {'g_pre': (D,), 'mla_q_norm': (Q_RANK,), 'mla_kv_norm': (KV_RANK,), 'rwkv_mu': (3456,),
                'rwkv_w0_f': (RW,), 'rwkv_w0_b': (RW,), 'rwkv_a0_f': (RW,), 'rwkv_a0_b': (RW,), 'rwkv_k_k': (RW,),
                'rwkv_k_a': (RW,), 'rwkv_r_k': (RH, RN), 'rwkv_gn_g': (RW,), 'rwkv_gn_b': (RW,), 'g_post': (D,)}
SMALL_LEN = sum(int(np.prod(s)) for s in SMALL_SHAPES.values())
SMALL_ROWS = 144


def _pack_big(d):
    return jnp.concatenate([d[n].reshape(-1, LANE) for n in BIG], axis=0)


def _unpack_big(packed):
    out, o = {}, 0
    for n in BIG:
        shp = BIG_SHAPES[n]
        sz = shp[0] * shp[1] // LANE
        out[n] = packed[o:o + sz].reshape(shp)
        o += sz
    return out


def _gathered(ag):
    out, o = {}, 0
    for n in BIG:
        shp = BIG_SHAPES[n]
        sz = shp[0] * shp[1] // LANE
        blk = ag[:, o:o + sz].reshape(4, *shp)
        out[n] = jnp.concatenate([blk[j] for j in range(4)], axis=0 if n == 'w_out' else 1)
        o += sz
    return out


def _shards(n, g):
    if n == 'w_out':
        return [g[j * 512:(j + 1) * 512] for j in range(4)]
    w = BIG_SHAPES[n][1]
    return [g[:, j * w:(j + 1) * w] for j in range(4)]


def _pack_small(d, extra=None):
    flat = jnp.concatenate([d[n].reshape(-1) for n in SMALL] + ([extra.reshape(-1)] if extra is not None else []))
    return jnp.pad(flat, (0, SMALL_ROWS * LANE - flat.shape[0])).reshape(SMALL_ROWS, LANE)


def _unpack_small(packed):
    flat, out, o = packed.reshape(-1), {}, 0
    for n in SMALL:
        sz = int(np.prod(SMALL_SHAPES[n]))
        out[n] = flat[o:o + sz].reshape(SMALL_SHAPES[n])
        o += sz
    return out


def _perm_w_in(w):
    z = lambda n: jnp.zeros((w.shape[0], n), w.dtype)
    lora = []
    for i in range(4):
        lora += [w[:, 4160 + LORA * i:4160 + LORA * (i + 1)], z(LANE - LORA)]
    return jnp.concatenate([w[:, 0:1024], w[:, 1088:4160], w[:, 4544:D_IN]] + lora
                           + [w[:, 1024:1056], z(96), w[:, 1056:1088], z(96)], axis=1)


def _unperm_w_in(g):
    lora = [g[:, OFF_LORA + LANE * i:OFF_LORA + LANE * i + LORA] for i in range(4)]
    return jnp.concatenate([g[:, 0:1024], g[:, OFF_KR:OFF_KR + 32], g[:, OFF_KR + LANE:OFF_KR + LANE + 32],
                            g[:, 1024:4096]] + lora + [g[:, 4096:OFF_LORA]], axis=1)


def _perm_wq(w):
    w3 = w.reshape(Q_RANK, HEADS, NOPE + ROPE)
    return jnp.concatenate([w3[:, :, :NOPE].reshape(Q_RANK, -1), w3[:, :, NOPE:NOPE + 32].reshape(Q_RANK, -1),
                            w3[:, :, NOPE + 32:].reshape(Q_RANK, -1)], axis=1)


def _unperm_wq(g):
    return jnp.concatenate([g[:, :1024].reshape(Q_RANK, HEADS, NOPE), g[:, 1024:1280].reshape(Q_RANK, HEADS, 32),
                            g[:, 1280:].reshape(Q_RANK, HEADS, 32)], axis=2).reshape(Q_RANK, -1)


def _perm_wkv(w):
    w3 = w.reshape(KV_RANK, HEADS, NOPE + VDIM)
    return jnp.concatenate([w3[:, :, :NOPE].reshape(KV_RANK, -1), w3[:, :, NOPE:].reshape(KV_RANK, -1)], axis=1)


def _unperm_wkv(g):
    return jnp.concatenate([g[:, :1024].reshape(KV_RANK, HEADS, NOPE), g[:, 1024:].reshape(KV_RANK, HEADS, VDIM)],
                           axis=2).reshape(KV_RANK, -1)


def _pad_rows(w):
    return jnp.pad(w, ((0, LANE - LORA), (0, 0)))


def _perm_mu(mu):
    parts = [mu[:3072]]
    for i in range(4):
        parts += [mu[3072 + LORA * i:3072 + LORA * (i + 1)], jnp.zeros((LANE - LORA,), mu.dtype)]
    return jnp.concatenate(parts).reshape(1, NLERP)


def _unperm_mu(g):
    g = g.reshape(-1)
    return jnp.concatenate([g[:3072]] + [g[3072 + LANE * i:3072 + LANE * i + LORA] for i in range(4)])


def _to_heads(t, n):
    return t.reshape(T, -1, n).transpose(1, 0, 2)


def _from_heads(t):
    return t.transpose(1, 0, 2).reshape(T, -1)


def _to_qk(n, r1, r2):
    return jnp.concatenate([n.reshape(T, HEADS, NOPE), r1.reshape(T, HEADS, 32), r2.reshape(T, HEADS, 32)],
                           axis=2).transpose(1, 0, 2)


def _from_qk(g):
    g = g.transpose(1, 0, 2)
    return g[:, :, :NOPE].reshape(T, -1), g[:, :, NOPE:NOPE + 32].reshape(T, -1), g[:, :, NOPE + 32:].reshape(T, -1)


def _constants():
    g2 = np.kron(np.eye(2, dtype=np.float32), np.ones((RN, RN), np.float32))
    e = np.zeros((LANE, 256), np.float32)
    for h in range(HEADS):
        e[np.arange(32), h * 32 + np.arange(32)] = 1.0
    pos = jnp.arange(T, dtype=F32)
    inv_freq = jnp.power(ROPE_THETA, -jnp.arange(0, ROPE, 2, dtype=F32) / ROPE)
    ang = pos[:, None] * inv_freq[None, :]
    cos, sin = jnp.cos(ang), jnp.sin(ang)
    padk = lambda t: jnp.pad(t, ((0, 0), (0, LANE - 32)))
    return (jnp.asarray(g2, BF16), jnp.asarray(e, BF16), jnp.tile(cos, (1, HEADS)), jnp.tile(sin, (1, HEADS)),
            padk(cos), padk(sin))


def _step(x, tgt, w, m, v):
    x2, tgt2 = x.reshape(T, D), tgt.reshape(T, D)
    g2, e_mat, cosq, sinq, cosk, sink = _constants()
    row = lambda n: w[n].reshape(1, -1)

    ag = xy_exchange("gather_weights", _pack_big({n: w[n].astype(BF16) for n in BIG}), bcast=True)
    full = _gathered(ag)
    wp = _perm_w_in(full['w_in'])
    wq = _perm_wq(full['mla_wq_b']).astype(F32)
    wkv = _perm_wkv(full['mla_wkv_b']).astype(F32)
    lora_w = [_pad_rows(full[n]).astype(F32) for n in ('rwkv_w2_f', 'rwkv_w2_b', 'rwkv_a2_f', 'rwkv_a2_b')]
    mu_p = _perm_mu(w['rwkv_mu'])

    st_pre = Stage("pre", f_pre, [(D, BF16), (D, None)], 256, [0], [0], [F32])
    st_mla = Stage("mla", f_mla, [(1024, BF16), (256, BF16), (256, BF16), (1024, BF16), (256, BF16), (256, BF16),
                                  (1024, BF16)], 256, [0, 1, 2], [0, 1, 2, 3], [BF16] * 3)
    st_rpre = Stage("rwkv_pre", f_rwkv_pre, [(RW, F32)] * 9, 128, [0, 1, 2, 3], list(range(10)), [F32] * 4)
    st_rpost = Stage("rwkv_post", f_rwkv_post, [(RW, BF16)], 256, [0, 2, 3, 4, 5, 6], [0, 1, 2],
                     [F32, F32, F32, F32, F32, BF16])
    st_gate = Stage("gate", f_gate, [(RW, BF16)], 256, [0, 1], [], [F32, BF16])
    st_merge = Stage("merge", f_merge, [(D, BF16)], 256, [0, 1, 2, 3], [], [BF16] * 4)

    pre_rows, pre_par = [(x2, D, 0)], [row('g_pre')]
    (h,) = st_pre.fwd(pre_rows, pre_par)
    proj = matmul("mm_in", h, wp, "nn")

    mla_rows = [(proj, 512, OFF_QA // 512), (proj, 512, OFF_KVA // 512), (proj, 256, OFF_KR // 256),
                (cosq, 256, 0), (sinq, 256, 0), (cosk, LANE, 0), (sink, LANE, 0)]
    mla_par = [row('mla_q_norm'), row('mla_kv_norm'), wq, wkv, e_mat]
    qn, qr1, qr2, kn, kr1, kr2, vv = st_mla.fwd(mla_rows, mla_par)
    qh, kh, vh = _to_qk(qn, qr1, qr2), _to_qk(kn, kr1, kr2), _to_heads(vv, VDIM)
    y_mla = _from_heads(attn_fwd(qh, kh, vh))

    lerp = shift_fwd(proj, mu_p)
    rpre_rows = [(lerp, RW, 0), (lerp, RW, 1), (lerp, RW, 2), (lerp, 512, 6)]
    rpre_par = [row('rwkv_w0_f'), row('rwkv_w0_b'), row('rwkv_a0_f'), row('rwkv_a0_b'), row('rwkv_k_k'),
                row('rwkv_k_a')] + lora_w + [g2]
    r_, v_, lwf, lwb, kf, kb, an, bf_, bb_ = st_rpre.fwd(rpre_rows, rpre_par)
    fin = [r_, lwf, kf, v_, an, bf_]
    bin_ = [r_, lwb, kb, v_, an, bb_]
    yf, h0f = scan_fwd("scan_f", *fin, reverse=False)
    yb, h0b = scan_fwd("scan_b", *bin_, reverse=True)
    rpost_rows = [(yf, RW, 0), (yb, RW, 0), (r_, RW, 0), (kf, RW, 0), (kb, RW, 0),
                  (v_, RW, 0), (proj, RW, OFF_ZR // RW)]
    rpost_par = [row('rwkv_gn_g'), row('rwkv_gn_b'), row('rwkv_r_k'), g2]
    (gr,) = st_rpost.fwd(rpost_rows, rpost_par)
    gate_rows = [(y_mla, RW, 0), (proj, RW, OFF_ZM // RW)]
    (gm,) = st_gate.fwd(gate_rows, [])
    um = matmul("mm_br_mla", gm, full['w_br_mla'], "nn")
    ur = matmul("mm_br_rwkv", gr, full['w_br_rwkv'], "nn")
    merge_rows = [(um, D, 0), (ur, D, 0), (proj, D, OFF_GM // D), (proj, D, OFF_GR // D)]
    (merged,) = st_merge.fwd(merge_rows, [])
    out = matmul("mm_out", merged, full['w_out'], "nn")
    d_out, dy, dg_post, loss_blk = loss_stage(out, x2, tgt2, row('g_post'))

    gw = {'g_post': dg_post}
    d_merged = matmul("mm_out_dx", d_out, full['w_out'], "nt")
    gw['w_out'] = matmul("mm_out_dw", merged, d_out, "tn")
    (d_um, d_ur, d_gm, d_gr), _ = st_merge.bwd(merge_rows, [], [[(d_merged, D, 0)]])
    d_gmla = matmul("mm_br_mla_dx", d_um, full['w_br_mla'], "nt")
    gw['w_br_mla'] = matmul("mm_br_mla_dw", gm, d_um, "tn")
    d_grw = matmul("mm_br_rwkv_dx", d_ur, full['w_br_rwkv'], "nt")
    gw['w_br_rwkv'] = matmul("mm_br_rwkv_dw", gr, d_ur, "tn")
    (d_ymla, d_zm), _ = st_gate.bwd(gate_rows, [], [[(d_gmla, RW, 0)]])
    (d_y, d_r3, d_kf2, d_kb2, d_v3, d_zr), (gw['rwkv_gn_g'], gw['rwkv_gn_b'], d_rk) = st_rpost.bwd(
        rpost_rows, rpost_par, [[(d_grw, RW, 0)]])
    gw['rwkv_r_k'] = d_rk
    sf = scan_bwd("scan_f_bwd", *fin, h0f, d_y, reverse=False)
    sb = scan_bwd("scan_b_bwd", *bin_, h0b, d_y, reverse=True)
    c = lambda *ts: [(t, RW, 0) for t in ts]
    rpre_cts = [c(sf[0], sb[0], d_r3), c(sf[3], sb[3], d_v3), c(sf[1]), c(sb[1]), c(sf[2], d_kf2), c(sb[2], d_kb2),
                c(sf[4], sb[4]), c(sf[5]), c(sb[5])]
    d_lerp_rows, rpre_g = st_rpre.bwd(rpre_rows, rpre_par, rpre_cts)
    for n, gval in zip(('rwkv_w0_f', 'rwkv_w0_b', 'rwkv_a0_f', 'rwkv_a0_b', 'rwkv_k_k', 'rwkv_k_a'), rpre_g[:6]):
        gw[n] = gval
    for n, gval in zip(('rwkv_w2_f', 'rwkv_w2_b', 'rwkv_a2_f', 'rwkv_a2_b'), rpre_g[6:]):
        gw[n] = gval[:LORA]
    d_lerp, d_mu = shift_bwd(proj, mu_p, jnp.concatenate(d_lerp_rows, axis=1))
    gw['rwkv_mu'] = _unperm_mu(d_mu)

    dqh, dkh, dvh = attn_bwd(qh, kh, vh, _to_heads(d_ymla, VDIM))
    mla_cts = [[(t, t.shape[1], 0)] for t in (*_from_qk(dqh), *_from_qk(dkh), _from_heads(dvh))]
    (d_qa, d_kva, d_kr), (gw['mla_q_norm'], gw['mla_kv_norm'], d_wq, d_wkv) = st_mla.bwd(mla_rows, mla_par, mla_cts)
    gw['mla_wq_b'], gw['mla_wkv_b'] = _unperm_wq(d_wq), _unperm_wkv(d_wkv)

    dproj = jnp.concatenate([d_qa, d_kva, d_lerp[:, :3072], d_zm, d_zr, d_gm, d_gr, d_lerp[:, 3072:], d_kr], axis=1)
    dh = matmul("mm_in_dx", dproj, wp, "nt")
    gw['w_in'] = _unperm_w_in(matmul("mm_in_dw", h, dproj, "tn"))
    (grad_x,), (gw['g_pre'],) = st_pre.bwd(pre_rows, pre_par, [[(dh, D, 0)], [(dy, D, 0)]])

    send = jnp.stack([_pack_big({n: _shards(n, gw[n])[j].astype(BF16) for n in BIG}) for j in range(4)])
    mine = sum4(xy_exchange("scatter_grads", send, bcast=False))
    theirs = c_swap("swap_cores", mine)
    big = adamw("adamw_big", _pack_big(w), _pack_big(m), _pack_big(v), [mine, theirs])
    parts = allgather8("gather_small", _pack_small(gw, loss_blk[0, :1]))
    small = adamw("adamw_small", _pack_small(w), _pack_small(m), _pack_small(v), [parts[i] for i in range(8)])

    outs = []
    for b_arr, s_arr in zip(big, small):
        d = {**_unpack_big(b_arr), **_unpack_small(s_arr)}
        outs.append([d[n] for n in WEIGHTS])
    loss = small[0][SMALL_LEN // LANE, 0]
    return (loss, grad_x.reshape(1, T, D), *outs[0], *outs[1], *outs[2], *outs[3])


def kernel(x, g_pre, w_in, mla_q_norm, mla_wq_b, mla_kv_norm, mla_wkv_b, rwkv_mu, rwkv_w0_f, rwkv_w2_f, rwkv_w0_b, rwkv_w2_b, rwkv_a0_f, rwkv_a2_f, rwkv_a0_b, rwkv_a2_b, rwkv_k_k, rwkv_k_a, rwkv_r_k, rwkv_gn_g, rwkv_gn_b, w_br_mla, w_br_rwkv, w_out, g_post, loss_target, m_g_pre, m_w_in, m_mla_q_norm, m_mla_wq_b, m_mla_kv_norm, m_mla_wkv_b, m_rwkv_mu, m_rwkv_w0_f, m_rwkv_w2_f, m_rwkv_w0_b, m_rwkv_w2_b, m_rwkv_a0_f, m_rwkv_a2_f, m_rwkv_a0_b, m_rwkv_a2_b, m_rwkv_k_k, m_rwkv_k_a, m_rwkv_r_k, m_rwkv_gn_g, m_rwkv_gn_b, m_w_br_mla, m_w_br_rwkv, m_w_out, m_g_post, v_g_pre, v_w_in, v_mla_q_norm, v_mla_wq_b, v_mla_kv_norm, v_mla_wkv_b, v_rwkv_mu, v_rwkv_w0_f, v_rwkv_w2_f, v_rwkv_w0_b, v_rwkv_w2_b, v_rwkv_a0_f, v_rwkv_a2_f, v_rwkv_a0_b, v_rwkv_a2_b, v_rwkv_k_k, v_rwkv_k_a, v_rwkv_r_k, v_rwkv_gn_g, v_rwkv_gn_b, v_w_br_mla, v_w_br_rwkv, v_w_out, v_g_post):
    given = dict(locals())
    w = {n: given[n] for n in WEIGHTS}
    m = {n: given['m_' + n] for n in WEIGHTS}
    v = {n: given['v_' + n] for n in WEIGHTS}
    return _step(x, loss_target, w, m, v)
```

```python
import functools
import math

import numpy as np
import jax
import jax.numpy as jnp
from jax import lax
from jax.experimental import pallas as pl
from jax.experimental.pallas import tpu as pltpu

F32, BF16 = jnp.float32, jnp.bfloat16
MESH_IDS = pl.DeviceIdType.MESH

D = 2048
T = 2048
HEADS = 8
Q_RANK = 512
KV_RANK = 512
NOPE = 128
ROPE = 64
VDIM = 128
RW = 1024
RH = 16
RN = 64
LORA = 96
D_IN = 10688
NORM_EPS = 1e-6
GN_EPS = 64e-5
ROPE_THETA = 10000.0
ADAM_LR, ADAM_B1, ADAM_B2, ADAM_EPS, ADAM_WD, ADAM_STEP = 0.001, 0.9, 0.999, 1e-08, 0.01, 10

LANE = 128
VMEM_BIG = 56 * 2**20

NP = 11008
OFF_QA, OFF_KVA, OFF_RKV, OFF_ZM, OFF_ZR, OFF_GM, OFF_GR, OFF_LORA, OFF_KR = 0, 512, 1024, 4096, 5120, 6144, 8192, 10240, 10752
NLERP = 3584

CHUNK = 64
NCH = T // CHUNK


def _dg(a, b, ca, cb, batch=False, prec=None):
    bd = ((0,), (0,)) if batch else ((), ())
    return lax.dot_general(a, b, (((ca,), (cb,)), bd), precision=prec, preferred_element_type=F32)


@jax.custom_vjp
def bdot(a, b):
    return _dg(a.astype(BF16), b.astype(BF16), 1, 0)


def _bdot_fwd(a, b):
    return bdot(a, b), (a, b)


def _bdot_bwd(res, g):
    a, b = res
    gb = g.astype(BF16)
    da = _dg(gb, b.astype(BF16), 1, 1)
    db = _dg(a.astype(BF16), gb, 0, 0)
    return da.astype(a.dtype), db.astype(b.dtype)


bdot.defvjp(_bdot_fwd, _bdot_bwd)


def _split(x):
    hi = x.astype(BF16)
    lo = (x - hi.astype(F32)).astype(BF16)
    return hi, lo


@jax.custom_vjp
def gsum(x, g2):
    hi, lo = _split(x)
    return _dg(hi, g2, 1, 0) + _dg(lo, g2, 1, 0)


def _gsum_fwd(x, g2):
    return gsum(x, g2), g2


def _gsum_bwd(g2, g):
    hi, lo = _split(g)
    return _dg(hi, g2, 1, 1) + _dg(lo, g2, 1, 1), jnp.zeros_like(g2)


gsum.defvjp(_gsum_fwd, _gsum_bwd)


def headsum(x, g2):
    return jnp.concatenate([gsum(x[:, i * LANE:(i + 1) * LANE], g2) for i in range(x.shape[1] // LANE)], axis=1)


def _terms(x, n):
    out = []
    for i in range(n):
        t = x.astype(BF16)
        out.append(t)
        if i < n - 1:
            x = x - t.astype(F32)
    return out


def _bmm(a, b, ca, cb, na, nb):
    acc = None
    for i, ai in enumerate(_terms(a, na)):
        for j, bj in enumerate(_terms(b, nb)):
            if i + j < max(na, nb):
                p = _dg(ai, bj, ca, cb, True)
                acc = p if acc is None else acc + p
    return acc


_NN, _NT, _TN = (2, 1), (2, 2), (1, 1)


def _make_dots(nf, nb_nn, nb_nt, nb_tn):
    @jax.custom_vjp
    def nn(a, b):
        return _bmm(a, b, *_NN, nf, nf)

    @jax.custom_vjp
    def nt(a, b):
        return _bmm(a, b, *_NT, nf, nf)

    @jax.custom_vjp
    def tn(a, b):
        return _bmm(a, b, *_TN, nf, nf)

    nn.defvjp(lambda a, b: (nn(a, b), (a, b)),
              lambda r, g: (_bmm(g, r[1], *_NT, nb_nn, nb_nn), _bmm(r[0], g, *_TN, nb_nn, nb_nn)))
    nt.defvjp(lambda a, b: (nt(a, b), (a, b)),
              lambda r, g: (_bmm(g, r[1], *_NN, nb_nt, nb_nt), _bmm(g, r[0], *_TN, nb_nt, nb_nt)))
    tn.defvjp(lambda a, b: (tn(a, b), (a, b)),
              lambda r, g: (_bmm(r[1], g, *_NT, nb_tn, nb_tn), _bmm(r[0], g, *_NN, nb_tn, nb_tn)))
    return nn, nt, tn


_SCAN_NF, _SCAN_NB = 1, 1
nn, nt, tn = _make_dots(_SCAN_NF, 1, 2, 1)


@jax.custom_vjp
def cumdot(ones, x):
    return _bmm(ones, x, *_NN, 1, 3)


cumdot.defvjp(lambda o, x: (cumdot(o, x), o), lambda o, g: (jnp.zeros_like(o), _bmm(o, g, *_TN, 1, 3)))


def _solve_powers(l):
    pw = [l]
    for _ in range(int(math.log2(l.shape[-1])) - 1):
        pw.append(_bmm(pw[-1], pw[-1], *_NN, _SCAN_NF, _SCAN_NF))
    return pw


@jax.custom_vjp
def tri_solve(l, rhs):
    x = rhs
    for p in _solve_powers(l):
        x = x + _bmm(p, x, *_NN, _SCAN_NF, _SCAN_NF)
    return x


def _tri_solve_fwd(l, rhs):
    pw = _solve_powers(l)
    x = rhs
    for p in pw:
        x = x + _bmm(p, x, *_NN, _SCAN_NF, _SCAN_NF)
    return x, (pw, x)


def _tri_solve_bwd(res, g):
    pw, x = res
    y = g
    for p in pw:
        y = y + _bmm(p, y, *_TN, _SCAN_NB, _SCAN_NB)
    return _bmm(y, x, *_NT, _SCAN_NB, _SCAN_NB), y


tri_solve.defvjp(_tri_solve_fwd, _tri_solve_bwd)


def _rms(x, g):
    return x * lax.rsqrt(jnp.mean(x * x, axis=-1, keepdims=True) + NORM_EPS) * g


def _softplus(x):
    pos = x > 0
    return jnp.where(pos, x, 0.0) + jnp.log(1.0 + jnp.exp(-jnp.where(pos, x, -x)))


def _silu(z):
    return z * jax.nn.sigmoid(z)


def _tile(n, cands):
    for c in cands:
        if n % c == 0:
            return c
    raise ValueError(n)


_MM_OPERAND_BYTES = 24 * 2**20


def _mm_tiles(m, n, k):
    tn_ = _tile(n, (512, 256, 128))
    best = None
    for tm in (2048, 1024, 512, 256, 128):
        if m % tm:
            continue
        for d in range(k // LANE, 0, -1):
            tk = LANE * d
            if k % tk == 0 and 4 * tk * (tm + tn_) <= _MM_OPERAND_BYTES:
                if best is None or tm * tk > best[0] * best[2]:
                    best = (tm, tn_, tk)
                break
    return best


def matmul(name, a, b, mode, out_dtype=F32):
    if mode == "nn":
        (m, k), n = a.shape, b.shape[1]
    elif mode == "nt":
        (m, k), n = a.shape, b.shape[0]
    else:
        (k, m), n = a.shape, b.shape[1]
    tm, tn_, tk = _mm_tiles(m, n, k)
    nk = k // tk
    if mode == "nn":
        a_spec = pl.BlockSpec((tm, tk), lambda i, j, kk: (i, kk))
        b_spec = pl.BlockSpec((tk, tn_), lambda i, j, kk: (kk, j))
        ca, cb = 1, 0
    elif mode == "nt":
        a_spec = pl.BlockSpec((tm, tk), lambda i, j, kk: (i, kk))
        b_spec = pl.BlockSpec((tn_, tk), lambda i, j, kk: (j, kk))
        ca, cb = 1, 1
    else:
        a_spec = pl.BlockSpec((tk, tm), lambda i, j, kk: (kk, i))
        b_spec = pl.BlockSpec((tk, tn_), lambda i, j, kk: (kk, j))
        ca, cb = 0, 0

    def body(a_ref, b_ref, o_ref, *acc):
        part = _dg(a_ref[...].astype(BF16), b_ref[...].astype(BF16), ca, cb)
        if nk == 1:
            o_ref[...] = part.astype(o_ref.dtype)
            return
        acc_ref, kk = acc[0], pl.program_id(2)

        @pl.when(kk == 0)
        def _():
            acc_ref[...] = part

        @pl.when(kk > 0)
        def _():
            acc_ref[...] += part

        @pl.when(kk == nk - 1)
        def _():
            o_ref[...] = acc_ref[...].astype(o_ref.dtype)

    return pl.pallas_call(
        body, name=name, grid=(m // tm, n // tn_, nk),
        in_specs=[a_spec, b_spec],
        out_specs=pl.BlockSpec((tm, tn_), lambda i, j, kk: (i, j)),
        out_shape=jax.ShapeDtypeStruct((m, n), out_dtype),
        scratch_shapes=[pltpu.VMEM((tm, tn_), F32)] if nk > 1 else [],
        compiler_params=pltpu.CompilerParams(dimension_semantics=("parallel", "parallel", "arbitrary"),
                                             vmem_limit_bytes=VMEM_BIG),
    )(a, b)


def _rspec(tr, width, blk):
    return pl.BlockSpec((tr, width), lambda i: (i, blk))


def _full_spec(arr):
    return pl.BlockSpec(arr.shape, lambda i: (0,) * arr.ndim)


class Stage:
    def __init__(self, name, f, outs, tr, diff_rows, diff_params, drow_dtypes):
        self.name, self.f, self.outs, self.tr = name, f, outs, tr
        self.diff_rows, self.diff_params, self.drow_dtypes = diff_rows, diff_params, drow_dtypes

    def fwd(self, rows, params):
        f, nr, npar = self.f, len(rows), len(params)
        stored = [(w, dt) for (w, dt) in self.outs if dt is not None]
        keep = [i for i, (w, dt) in enumerate(self.outs) if dt is not None]

        def body(*refs):
            vals = f(*[r[...].astype(F32) for r in refs[:nr]], *[p[...] for p in refs[nr:nr + npar]])
            for o_ref, i in zip(refs[nr + npar:], keep):
                o_ref[...] = vals[i].astype(o_ref.dtype)

        return pl.pallas_call(
            body, name=self.name + "_fwd", grid=(T // self.tr,),
            in_specs=[_rspec(self.tr, w, b) for (_, w, b) in rows] + [_full_spec(p) for p in params],
            out_specs=[_rspec(self.tr, w, 0) for (w, _) in stored],
            out_shape=[jax.ShapeDtypeStruct((T, w), dt) for (w, dt) in stored],
            compiler_params=pltpu.CompilerParams(dimension_semantics=("arbitrary",), vmem_limit_bytes=VMEM_BIG),
        )(*[r[0] for r in rows], *params)

    def bwd(self, rows, params, cts):
        f, nr, npar = self.f, len(rows), len(params)
        dr_idx, dp_idx = self.diff_rows, self.diff_params
        flat_cts = [c for lst in cts for c in lst]
        nct = len(flat_cts)

        def body(*refs):
            row_refs, par_refs = refs[:nr], refs[nr:nr + npar]
            ct_refs = refs[nr + npar:nr + npar + nct]
            drow_refs = refs[nr + npar + nct:nr + npar + nct + len(dr_idx)]
            dpar_refs = refs[nr + npar + nct + len(dr_idx):]
            row_vals = [r[...].astype(F32) for r in row_refs]
            par_vals = [p[...] for p in par_refs]

            def g(*dv):
                rv, pv = list(row_vals), list(par_vals)
                for j, i in enumerate(dr_idx):
                    rv[i] = dv[j]
                for j, i in enumerate(dp_idx):
                    pv[i] = dv[len(dr_idx) + j]
                return f(*rv, *pv)

            _, vjp = jax.vjp(g, *[row_vals[i] for i in dr_idx], *[par_vals[i] for i in dp_idx])
            ct_vals, pos = [], 0
            for lst in cts:
                acc = ct_refs[pos][...].astype(F32)
                for q in range(1, len(lst)):
                    acc = acc + ct_refs[pos + q][...].astype(F32)
                pos += len(lst)
                ct_vals.append(acc)
            grads = vjp(tuple(ct_vals))
            for j, r in enumerate(drow_refs):
                r[...] = grads[j].astype(r.dtype)

            @pl.when(pl.program_id(0) == 0)
            def _():
                for r in dpar_refs:
                    r[...] = jnp.zeros_like(r)

            for j, r in enumerate(dpar_refs):
                r[...] += grads[len(dr_idx) + j].astype(F32)

        drow_shapes = [jax.ShapeDtypeStruct((T, rows[i][1]), dt) for i, dt in zip(dr_idx, self.drow_dtypes)]
        dpar_shapes = [jax.ShapeDtypeStruct(params[i].shape, F32) for i in dp_idx]
        res = pl.pallas_call(
            body, name=self.name + "_bwd", grid=(T // self.tr,),
            in_specs=[_rspec(self.tr, w, b) for (_, w, b) in rows] + [_full_spec(p) for p in params]
            + [_rspec(self.tr, w, b) for (_, w, b) in flat_cts],
            out_specs=[_rspec(self.tr, rows[i][1], 0) for i in dr_idx] + [_full_spec(params[i]) for i in dp_idx],
            out_shape=drow_shapes + dpar_shapes,
            compiler_params=pltpu.CompilerParams(dimension_semantics=("arbitrary",), vmem_limit_bytes=VMEM_BIG),
        )(*[r[0] for r in rows], *params, *[c[0] for c in flat_cts])
        return res[:len(dr_idx)], res[len(dr_idx):]


def f_pre(x, g):
    return _rms(x, g), x


def f_mla(q_a, kv_a, kr, cosq, sinq, cosk, sink, gq, gkv, wq, wkv, e):
    q = bdot(_rms(q_a, gq), wq)
    kv = bdot(_rms(kv_a, gkv), wkv)
    t1, t2 = q[:, 1024:1280], q[:, 1280:1536]
    k1, k2 = kr[:, :LANE], kr[:, LANE:]
    kr1 = k1 * cosk - k2 * sink
    kr2 = k1 * sink + k2 * cosk
    return (q[:, :1024], t1 * cosq - t2 * sinq, t1 * sinq + t2 * cosq,
            kv[:, :1024], bdot(kr1, e), bdot(kr2, e), kv[:, 1024:])


def f_rwkv_pre(r, k, v, lora, w0f, w0b, a0f, a0b, kkw, kaw, w2f, w2b, a2f, a2b, g2):
    wdf, wdb, adf, adb = (lora[:, i * LANE:(i + 1) * LANE] for i in range(4))

    def logdecay(w0, wd, w2):
        z = w0 + bdot(jnp.tanh(wd), w2)
        return -jnp.exp(-_softplus(-z) - 0.5)

    a_f = jax.nn.sigmoid(a0f + bdot(adf, a2f))
    a_b = jax.nn.sigmoid(a0b + bdot(adb, a2b))
    kk = k * kkw
    kk = kk / jnp.maximum(jnp.sqrt(headsum(kk * kk, g2)), 1e-12)
    return (r, v, logdecay(w0f, wdf, w2f), logdecay(w0b, wdb, w2b),
            k * (1.0 + (a_f - 1.0) * kaw), k * (1.0 + (a_b - 1.0) * kaw), -kk, kk * a_f, kk * a_b)


def f_rwkv_post(yf, yb, r, kf, kb, v, z, gng, gnb, rk, g2):
    y = yf + yb
    mu = headsum(y, g2) * (1.0 / RN)
    d = y - mu
    var = headsum(d * d, g2) * (1.0 / RN)
    yn = d * lax.rsqrt(var + GN_EPS) * gng + gnb
    bonus = headsum(r * (kf + kb) * rk, g2) * v
    return ((yn + bonus) * _silu(z),)


def f_gate(y, z):
    return (y * _silu(z),)


def f_merge(um, ur, gm, gr):
    return (jax.nn.sigmoid(gm) * um + jax.nn.sigmoid(gr) * ur,)


_SHIFT_W = 256


def _lerp_colblock(j):
    return jnp.where(j < 3072 // _SHIFT_W, OFF_RKV // _SHIFT_W + j, OFF_LORA // _SHIFT_W + j - 3072 // _SHIFT_W)


def _nbr_mean(x):
    row = lax.broadcasted_iota(jnp.int32, x.shape, 0)
    up = jnp.where(row == 0, 0.0, pltpu.roll(x, 1, 0))
    dn = jnp.where(row == T - 1, 0.0, pltpu.roll(x, T - 1, 0))
    return 0.5 * (up + dn)


def shift_fwd(proj, mu):
    def body(x_ref, mu_ref, o_ref):
        x = x_ref[...]
        o_ref[...] = x + mu_ref[...] * (_nbr_mean(x) - x)

    return pl.pallas_call(
        body, name="shift_fwd", grid=(NLERP // _SHIFT_W,),
        in_specs=[pl.BlockSpec((T, _SHIFT_W), lambda j: (0, _lerp_colblock(j))),
                  pl.BlockSpec((1, _SHIFT_W), lambda j: (0, j))],
        out_specs=pl.BlockSpec((T, _SHIFT_W), lambda j: (0, j)),
        out_shape=jax.ShapeDtypeStruct((T, NLERP), F32),
        compiler_params=pltpu.CompilerParams(dimension_semantics=("parallel",), vmem_limit_bytes=VMEM_BIG),
    )(proj, mu)


def shift_bwd(proj, mu, g):
    def body(x_ref, mu_ref, g_ref, dx_ref, dmu_ref):
        x, gv = x_ref[...], g_ref[...]
        dmu_ref[...] = jnp.sum(gv * (_nbr_mean(x) - x), axis=0, keepdims=True)
        gm = gv * mu_ref[...]
        dx_ref[...] = (gv - gm + _nbr_mean(gm)).astype(dx_ref.dtype)

    col = pl.BlockSpec((T, _SHIFT_W), lambda j: (0, j))
    vec = pl.BlockSpec((1, _SHIFT_W), lambda j: (0, j))
    return pl.pallas_call(
        body, name="shift_bwd", grid=(NLERP // _SHIFT_W,),
        in_specs=[pl.BlockSpec((T, _SHIFT_W), lambda j: (0, _lerp_colblock(j))), vec, col],
        out_specs=[col, vec],
        out_shape=[jax.ShapeDtypeStruct((T, NLERP), BF16), jax.ShapeDtypeStruct((1, NLERP), F32)],
        compiler_params=pltpu.CompilerParams(dimension_semantics=("parallel",), vmem_limit_bytes=VMEM_BIG),
    )(proj, mu, g)


_TQ = 256
_ATT_SCALE = (NOPE + ROPE) ** -0.5


def _probs(q, k):
    s = _dg(q, k, 1, 1) * _ATT_SCALE
    e = jnp.exp(s - jnp.max(s, axis=-1, keepdims=True))
    return e * (1.0 / jnp.sum(e, axis=-1, keepdims=True))


def attn_fwd(q, k, v):
    def body(q_ref, k_ref, v_ref, o_ref):
        p = _probs(q_ref[0], k_ref[0])
        o_ref[0] = _dg(p.astype(BF16), v_ref[0], 1, 0)

    dq = NOPE + ROPE
    return pl.pallas_call(
        body, name="attn_fwd", grid=(HEADS, T // _TQ),
        in_specs=[pl.BlockSpec((1, _TQ, dq), lambda h, i: (h, i, 0)),
                  pl.BlockSpec((1, T, dq), lambda h, i: (h, 0, 0)),
                  pl.BlockSpec((1, T, VDIM), lambda h, i: (h, 0, 0))],
        out_specs=pl.BlockSpec((1, _TQ, VDIM), lambda h, i: (h, i, 0)),
        out_shape=jax.ShapeDtypeStruct((HEADS, T, VDIM), F32),
        compiler_params=pltpu.CompilerParams(dimension_semantics=("parallel", "arbitrary"), vmem_limit_bytes=VMEM_BIG),
    )(q, k, v)


def attn_bwd(q, k, v, do):
    def body(q_ref, k_ref, v_ref, do_ref, dq_ref, dk_ref, dv_ref):
        @pl.when(pl.program_id(1) == 0)
        def _():
            dk_ref[...] = jnp.zeros_like(dk_ref)
            dv_ref[...] = jnp.zeros_like(dv_ref)

        qv, kv_, vv = q_ref[0], k_ref[0], v_ref[0]
        dob = do_ref[0].astype(BF16)
        p = _probs(qv, kv_)
        dv_ref[0] += _dg(p.astype(BF16), dob, 0, 0)
        dp = _dg(dob, vv, 1, 1)
        ds = (p * (dp - jnp.sum(dp * p, axis=-1, keepdims=True)) * _ATT_SCALE).astype(BF16)
        dq_ref[0] = _dg(ds, kv_, 1, 0)
        dk_ref[0] += _dg(ds, qv, 0, 0)

    dq = NOPE + ROPE
    return pl.pallas_call(
        body, name="attn_bwd", grid=(HEADS, T // _TQ),
        in_specs=[pl.BlockSpec((1, _TQ, dq), lambda h, i: (h, i, 0)),
                  pl.BlockSpec((1, T, dq), lambda h, i: (h, 0, 0)),
                  pl.BlockSpec((1, T, VDIM), lambda h, i: (h, 0, 0)),
                  pl.BlockSpec((1, _TQ, VDIM), lambda h, i: (h, i, 0))],
        out_specs=[pl.BlockSpec((1, _TQ, dq), lambda h, i: (h, i, 0)),
                   pl.BlockSpec((1, T, dq), lambda h, i: (h, 0, 0)),
                   pl.BlockSpec((1, T, VDIM), lambda h, i: (h, 0, 0))],
        out_shape=[jax.ShapeDtypeStruct((HEADS, T, dq), F32), jax.ShapeDtypeStruct((HEADS, T, dq), F32),
                   jax.ShapeDtypeStruct((HEADS, T, VDIM), F32)],
        compiler_params=pltpu.CompilerParams(dimension_semantics=("parallel", "arbitrary"), vmem_limit_bytes=VMEM_BIG),
    )(q, k, v, do)


def _chunk(r, lw, k, v, a, b, ht, *, reverse):
    hb, c, _ = r.shape
    ti = lax.broadcasted_iota(jnp.int32, (c, c), 0)
    si = lax.broadcasted_iota(jnp.int32, (c, c), 1)
    incl = (si >= ti) if reverse else (si <= ti)
    strict = (si > ti) if reverse else (si < ti)
    ones = jnp.broadcast_to(incl.astype(F32)[None], (hb, c, c))
    cum = cumdot(ones, lw)
    cum_ex = cum - lw
    tot = jnp.sum(lw, axis=1, keepdims=True)
    mid = 0.5 * tot
    rt, at = r * jnp.exp(cum - mid), a * jnp.exp(cum_ex - mid)
    einv = jnp.exp(mid - cum)
    kt, bt = k * einv, b * einv
    m_ab = jnp.where(strict, nt(at, bt), 0.0)
    m_ak = jnp.where(strict, nt(at, kt), 0.0)
    m_rb = jnp.where(incl, nt(rt, bt), 0.0)
    m_rk = jnp.where(incl, nt(rt, kt), 0.0)
    u = tri_solve(m_ab, nt(a * jnp.exp(cum_ex), ht) + nn(m_ak, v))
    y = nt(r * jnp.exp(cum), ht) + nn(m_rb, u) + nn(m_rk, v)
    eend = jnp.exp(tot - cum)
    ht_new = ht * jnp.exp(tot) + tn(u, b * eend) + tn(v, k * eend)
    return y, ht_new


_HB_F, _HB_B = 16, 8


def _split_heads(x):
    return jnp.stack([x[:, i * RN:(i + 1) * RN] for i in range(x.shape[1] // RN)])


def _merge_heads(y):
    return jnp.concatenate([y[i] for i in range(y.shape[0])], axis=1)


def _chunk_map(reverse, backward):
    flip = reverse != backward
    return (lambda g, c: (NCH - 1 - c, g)) if flip else (lambda g, c: (c, g))


def scan_fwd(name, r, lw, k, v, a, b, reverse):
    hb = _HB_F
    cmap = _chunk_map(reverse, False)

    def body(r_ref, lw_ref, k_ref, v_ref, a_ref, b_ref, y_ref, h0_ref, ht_ref):
        @pl.when(pl.program_id(1) == 0)
        def _():
            ht_ref[...] = jnp.zeros_like(ht_ref)

        ht = ht_ref[...]
        h0_ref[0] = ht
        ins = [_split_heads(x[...]) for x in (r_ref, lw_ref, k_ref, v_ref, a_ref, b_ref)]
        y, hn = _chunk(*ins, ht, reverse=reverse)
        y_ref[...] = _merge_heads(y)
        ht_ref[...] = hn

    io = pl.BlockSpec((CHUNK, hb * RN), cmap)
    return pl.pallas_call(
        body, name=name, grid=(RH // hb, NCH),
        in_specs=[io] * 6,
        out_specs=[io, pl.BlockSpec((1, hb, RN, RN), lambda g, c: (cmap(g, c)[0], g, 0, 0))],
        out_shape=[jax.ShapeDtypeStruct((T, RW), F32), jax.ShapeDtypeStruct((NCH, RH, RN, RN), F32)],
        scratch_shapes=[pltpu.VMEM((hb, RN, RN), F32)],
        compiler_params=pltpu.CompilerParams(dimension_semantics=("parallel", "arbitrary"), vmem_limit_bytes=VMEM_BIG),
    )(r, lw, k, v, a, b)


def scan_bwd(name, r, lw, k, v, a, b, h0, dy, reverse):
    hb = _HB_B
    cmap = _chunk_map(reverse, True)

    def body(r_ref, lw_ref, k_ref, v_ref, a_ref, b_ref, h0_ref, dy_ref, *rest):
        d_refs, dht_ref = rest[:6], rest[6]

        @pl.when(pl.program_id(1) == 0)
        def _():
            dht_ref[...] = jnp.zeros_like(dht_ref)

        ins = [_split_heads(x[...]) for x in (r_ref, lw_ref, k_ref, v_ref, a_ref, b_ref)]
        _, vjp = jax.vjp(functools.partial(_chunk, reverse=reverse), *ins, h0_ref[0])
        grads = vjp((_split_heads(dy_ref[...]), dht_ref[...]))
        for d_ref, gval in zip(d_refs, grads[:6]):
            d_ref[...] = _merge_heads(gval)
        dht_ref[...] = grads[6]

    io = pl.BlockSpec((CHUNK, hb * RN), cmap)
    return pl.pallas_call(
        body, name=name, grid=(RH // hb, NCH),
        in_specs=[io] * 6 + [pl.BlockSpec((1, hb, RN, RN), lambda g, c: (cmap(g, c)[0], g, 0, 0)), io],
        out_specs=[io] * 6,
        out_shape=[jax.ShapeDtypeStruct((T, RW), F32)] * 6,
        scratch_shapes=[pltpu.VMEM((hb, RN, RN), F32)],
        compiler_params=pltpu.CompilerParams(dimension_semantics=("parallel", "arbitrary"), vmem_limit_bytes=VMEM_BIG),
    )(r, lw, k, v, a, b, h0, dy)


def loss_stage(out, x2, tgt, g_post):
    tr = 256

    def body(o_ref, x_ref, t_ref, g_ref, do_ref, dy_ref, dg_ref, loss_ref):
        @pl.when(pl.program_id(0) == 0)
        def _():
            dg_ref[...] = jnp.zeros_like(dg_ref)
            loss_ref[...] = jnp.zeros_like(loss_ref)

        nrm, vjp = jax.vjp(_rms, o_ref[...], g_ref[...])
        e = x_ref[...] + nrm - t_ref[...]
        s = jnp.sum(jnp.sum(e * e, axis=1, keepdims=True), axis=0, keepdims=True)
        loss_ref[...] += jnp.broadcast_to(s * (0.5 / D), loss_ref.shape)
        dy = e * (1.0 / D)
        do, dg = vjp(dy)
        do_ref[...] = do.astype(do_ref.dtype)
        dy_ref[...] = dy
        dg_ref[...] += dg

    row = pl.BlockSpec((tr, D), lambda i: (i, 0))
    return pl.pallas_call(
        body, name="loss_stage", grid=(T // tr,),
        in_specs=[row, row, row, pl.BlockSpec((1, D), lambda i: (0, 0))],
        out_specs=[row, row, pl.BlockSpec((1, D), lambda i: (0, 0)), pl.BlockSpec((8, LANE), lambda i: (0, 0))],
        out_shape=[jax.ShapeDtypeStruct((T, D), BF16), jax.ShapeDtypeStruct((T, D), F32),
                   jax.ShapeDtypeStruct((1, D), F32), jax.ShapeDtypeStruct((8, LANE), F32)],
        compiler_params=pltpu.CompilerParams(dimension_semantics=("arbitrary",), vmem_limit_bytes=VMEM_BIG),
    )(out, x2, tgt, g_post)


_EW_BLOCK_BYTES = 1 << 20


def _row_tile(rows, cols):
    best = None
    for tr in range(16, rows + 1, 16):
        if rows % tr == 0 and tr * cols * 4 <= _EW_BLOCK_BYTES:
            best = tr
    return best or rows


def adamw(name, w, m, v, parts):
    rows, cols = w.shape
    br = _row_tile(rows, cols)
    npart = len(parts)

    def body(w_ref, m_ref, v_ref, *rest):
        g = rest[0][...].astype(F32)
        for p in rest[1:npart]:
            g = g + p[...].astype(F32)
        g_ref, d_ref, nm_ref, nv_ref = rest[npart:]
        mm = ADAM_B1 * m_ref[...] + (1.0 - ADAM_B1) * g
        vv = ADAM_B2 * v_ref[...] + (1.0 - ADAM_B2) * (g * g)
        m_hat = mm / (1.0 - ADAM_B1 ** ADAM_STEP)
        v_hat = vv / (1.0 - ADAM_B2 ** ADAM_STEP)
        g_ref[...] = g
        d_ref[...] = -ADAM_LR * (m_hat / (jnp.sqrt(v_hat) + ADAM_EPS) + ADAM_WD * w_ref[...])
        nm_ref[...] = mm
        nv_ref[...] = vv

    blk = pl.BlockSpec((br, cols), lambda i: (i, 0))
    return pl.pallas_call(
        body, name=name, grid=(rows // br,),
        in_specs=[blk] * (3 + npart), out_specs=[blk] * 4,
        out_shape=[jax.ShapeDtypeStruct((rows, cols), F32)] * 4,
        compiler_params=pltpu.CompilerParams(dimension_semantics=("parallel",), vmem_limit_bytes=VMEM_BIG),
    )(w, m, v, *parts)


def pair_sum(name, a, b):
    _, rows, cols = a.shape
    br = _row_tile(rows, 4 * cols)

    def body(a_ref, b_ref, o_ref):
        o_ref[...] = (a_ref[...].astype(F32) + b_ref[...].astype(F32)).astype(o_ref.dtype)

    blk = pl.BlockSpec((4, br, cols), lambda i: (0, i, 0))
    return pl.pallas_call(
        body, name=name, grid=(rows // br,), in_specs=[blk, blk], out_specs=blk,
        out_shape=jax.ShapeDtypeStruct(a.shape, BF16),
        compiler_params=pltpu.CompilerParams(dimension_semantics=("parallel",)),
    )(a, b)


def sum4(name, recv):
    _, rows, cols = recv.shape
    br = _row_tile(rows, 4 * cols)

    def body(r_ref, o_ref):
        o_ref[...] = ((r_ref[0].astype(F32) + r_ref[1].astype(F32)) + r_ref[2].astype(F32)) + r_ref[3].astype(F32)

    return pl.pallas_call(
        body, name=name, grid=(rows // br,),
        in_specs=[pl.BlockSpec((4, br, cols), lambda i: (0, i, 0))],
        out_specs=pl.BlockSpec((br, cols), lambda i: (i, 0)),
        out_shape=jax.ShapeDtypeStruct((rows, cols), F32),
        compiler_params=pltpu.CompilerParams(dimension_semantics=("parallel",)),
    )(recv)


_ANY = pl.BlockSpec(memory_space=pl.ANY)


def _place():
    x, y, c = lax.axis_index("x"), lax.axis_index("y"), lax.axis_index("c")
    return x, y, c, 2 * x + y


def _chip_peers(x, y):
    out = []
    for k in (1, 2, 3):
        px = 1 - x if k & 2 else x
        py = 1 - y if k & 1 else y
        out.append((k, px, py, 2 * px + py))
    return out


def _half(c, rows):
    hr = rows // 2
    return pl.ds(pl.multiple_of(c * hr, 16), hr)


def gather_weights(srcs):
    n = len(srcs)

    def body(*refs):
        src, dst = refs[:n], refs[n:2 * n]
        ssem, rsem, fssem, frsem, lsem = refs[2 * n:]
        x, y, c, me = _place()
        sib = (x, y, 1 - c)
        waits = []
        for i in range(n):
            loc = pltpu.make_async_copy(src[i], dst[i].at[me], lsem.at[i])
            loc.start()
            waits.append(loc.wait)
        chain = []
        for i in range(n):
            rows = srcs[i].shape[0]
            mine, other = _half(c, rows), _half(1 - c, rows)
            for k, px, py, peer in _chip_peers(x, y):
                sems = dict(send_sem=ssem.at[i, k - 1], recv_sem=rsem.at[i, k - 1], device_id=(px, py, c),
                            device_id_type=MESH_IDS)
                fsems = dict(send_sem=fssem.at[i, k - 1], recv_sem=frsem.at[i, k - 1], device_id=sib,
                             device_id_type=MESH_IDS)
                snd = pltpu.make_async_remote_copy(src_ref=src[i].at[mine], dst_ref=dst[i].at[me, mine], **sems)
                rcv = pltpu.make_async_remote_copy(src_ref=src[i].at[mine], dst_ref=dst[i].at[peer, mine], **sems)
                fwd = pltpu.make_async_remote_copy(src_ref=dst[i].at[peer, mine], dst_ref=dst[i].at[peer, mine], **fsems)
                frcv = pltpu.make_async_remote_copy(src_ref=dst[i].at[peer, mine], dst_ref=dst[i].at[peer, other], **fsems)
                snd.start()
                chain.append((rcv, fwd))
                waits += [frcv.wait_recv, snd.wait_send, fwd.wait_send]
        for rcv, fwd in chain:
            rcv.wait_recv()
            fwd.start()
        for w in waits:
            w()

    return pl.pallas_call(
        body, name="gather_weights", in_specs=[_ANY] * n, out_specs=[_ANY] * n,
        out_shape=[jax.ShapeDtypeStruct((4,) + s.shape, s.dtype) for s in srcs],
        scratch_shapes=[pltpu.SemaphoreType.DMA((n, 3))] * 4 + [pltpu.SemaphoreType.DMA((n,))],
    )(*srcs)


def pair_exchange(srcs):
    n = len(srcs)

    def body(*refs):
        src, keep, other = refs[:n], refs[n:2 * n], refs[2 * n:3 * n]
        ssem, rsem, lsem = refs[3 * n:]
        x, y, c, _ = _place()
        waits = []
        for i in range(n):
            rows = srcs[i].shape[1]
            loc = pltpu.make_async_copy(src[i].at[:, _half(c, rows)], keep[i], lsem.at[i])
            cp = pltpu.make_async_remote_copy(src_ref=src[i].at[:, _half(1 - c, rows)], dst_ref=other[i],
                                              send_sem=ssem.at[i], recv_sem=rsem.at[i], device_id=(x, y, 1 - c),
                                              device_id_type=MESH_IDS)
            cp.start()
            loc.start()
            waits += [cp.wait, loc.wait]
        for w in waits:
            w()

    halves = [jax.ShapeDtypeStruct((4, s.shape[1] // 2, s.shape[2]), s.dtype) for s in srcs]
    res = pl.pallas_call(
        body, name="pair_exchange", in_specs=[_ANY] * n, out_specs=[_ANY] * (2 * n), out_shape=halves + halves,
        scratch_shapes=[pltpu.SemaphoreType.DMA((n,))] * 3,
    )(*srcs)
    return res[:n], res[n:]


def scatter_grads(srcs):
    n = len(srcs)

    def body(*refs):
        src, dst = refs[:n], refs[n:2 * n]
        ssem, rsem, lsem = refs[2 * n:]
        x, y, c, me = _place()
        waits = []
        for i in range(n):
            loc = pltpu.make_async_copy(src[i].at[me], dst[i].at[me], lsem.at[i])
            loc.start()
            waits.append(loc.wait)
            for k, px, py, peer in _chip_peers(x, y):
                sems = dict(send_sem=ssem.at[i, k - 1], recv_sem=rsem.at[i, k - 1], device_id=(px, py, c),
                            device_id_type=MESH_IDS)
                snd = pltpu.make_async_remote_copy(src_ref=src[i].at[peer], dst_ref=dst[i].at[me], **sems)
                rcv = pltpu.make_async_remote_copy(src_ref=src[i].at[peer], dst_ref=dst[i].at[peer], **sems)
                snd.start()
                waits += [rcv.wait_recv, snd.wait_send]
        for w in waits:
            w()

    return pl.pallas_call(
        body, name="scatter_grads", in_specs=[_ANY] * n, out_specs=[_ANY] * n,
        out_shape=[jax.ShapeDtypeStruct(s.shape, s.dtype) for s in srcs],
        scratch_shapes=[pltpu.SemaphoreType.DMA((n, 3))] * 2 + [pltpu.SemaphoreType.DMA((n,))],
    )(*srcs)


def join_halves(srcs):
    n = len(srcs)

    def body(*refs):
        src, dst = refs[:n], refs[n:2 * n]
        ssem, rsem, lsem = refs[2 * n:]
        x, y, c, _ = _place()
        waits = []
        for i in range(n):
            rows = 2 * srcs[i].shape[0]
            loc = pltpu.make_async_copy(src[i], dst[i].at[_half(c, rows)], lsem.at[i])
            sems = dict(send_sem=ssem.at[i], recv_sem=rsem.at[i], device_id=(x, y, 1 - c), device_id_type=MESH_IDS)
            snd = pltpu.make_async_remote_copy(src_ref=src[i], dst_ref=dst[i].at[_half(c, rows)], **sems)
            rcv = pltpu.make_async_remote_copy(src_ref=src[i], dst_ref=dst[i].at[_half(1 - c, rows)], **sems)
            snd.start()
            loc.start()
            waits += [rcv.wait_recv, snd.wait_send, loc.wait]
        for w in waits:
            w()

    return pl.pallas_call(
        body, name="join_halves", in_specs=[_ANY] * n, out_specs=[_ANY] * n,
        out_shape=[jax.ShapeDtypeStruct((2 * s.shape[0], s.shape[1]), s.dtype) for s in srcs],
        scratch_shapes=[pltpu.SemaphoreType.DMA((n,))] * 3,
    )(*srcs)


def allgather8(name, src):
    rows = src.shape[0]

    def body(src_ref, dst_ref, send_sems, recv_sems):
        x, y, c = lax.axis_index("x"), lax.axis_index("y"), lax.axis_index("c")
        me = 4 * x + 2 * y + c
        dst_ref[me] = src_ref[...]
        sends, recvs = [], []
        for k in range(1, 8):
            px = 1 - x if k & 4 else x
            py = 1 - y if k & 2 else y
            pc = 1 - c if k & 1 else c
            peer = 4 * px + 2 * py + pc
            for lst, slot in ((sends, me), (recvs, peer)):
                lst.append(pltpu.make_async_remote_copy(
                    src_ref=src_ref, dst_ref=dst_ref.at[slot], send_sem=send_sems.at[k - 1],
                    recv_sem=recv_sems.at[k - 1], device_id=(px, py, pc), device_id_type=MESH_IDS))
        for cp in sends:
            cp.start()
        for cp in recvs:
            cp.wait_recv()
        for cp in sends:
            cp.wait_send()

    vm = pl.BlockSpec(memory_space=pltpu.VMEM)
    return pl.pallas_call(
        body, name=name, in_specs=[vm], out_specs=vm,
        out_shape=jax.ShapeDtypeStruct((8, rows, LANE), src.dtype),
        scratch_shapes=[pltpu.SemaphoreType.DMA((7,)), pltpu.SemaphoreType.DMA((7,))],
    )(src)


WEIGHTS = ['g_pre', 'w_in', 'mla_q_norm', 'mla_wq_b', 'mla_kv_norm', 'mla_wkv_b', 'rwkv_mu', 'rwkv_w0_f', 'rwkv_w2_f',
           'rwkv_w0_b', 'rwkv_w2_b', 'rwkv_a0_f', 'rwkv_a2_f', 'rwkv_a0_b', 'rwkv_a2_b', 'rwkv_k_k', 'rwkv_k_a',
           'rwkv_r_k', 'rwkv_gn_g', 'rwkv_gn_b', 'w_br_mla', 'w_br_rwkv', 'w_out', 'g_post']
BIG_SHAPES = {'w_in': (D, D_IN // 4), 'mla_wq_b': (Q_RANK, 384), 'mla_wkv_b': (KV_RANK, 512),
              'rwkv_w2_f': (LORA, 256), 'rwkv_w2_b': (LORA, 256), 'rwkv_a2_f': (LORA, 256), 'rwkv_a2_b': (LORA, 256),
              'w_br_mla': (RW, 512), 'w_br_rwkv': (RW, 512), 'w_out': (512, D)}
BIG = list(BIG_SHAPES)
SMALL = [n for n in WEIGHTS if n not in BIG_SHAPES]
SMALL_SHAPES = {'g_pre': (D,), 'mla_q_norm': (Q_RANK,), 'mla_kv_norm': (KV_RANK,), 'rwkv_mu': (3456,),
                'rwkv_w0_f': (RW,), 'rwkv_w0_b': (RW,), 'rwkv_a0_f': (RW,), 'rwkv_a0_b': (RW,), 'rwkv_k_k': (RW,),
                'rwkv_k_a': (RW,), 'rwkv_r_k': (RH, RN), 'rwkv_gn_g': (RW,), 'rwkv_gn_b': (RW,), 'g_post': (D,)}
SMALL_LEN = sum(int(np.prod(s)) for s in SMALL_SHAPES.values())
SMALL_ROWS = 144


UNITS = [('w_in',), ('mla_wq_b',), ('mla_wkv_b',), ('rwkv_w2_f', 'rwkv_w2_b', 'rwkv_a2_f', 'rwkv_a2_b'),
         ('w_br_mla', 'w_br_rwkv'), ('w_out',)]


def _unit_cat(parts):
    return parts[0] if len(parts) == 1 else jnp.concatenate(parts, axis=0)


def _unit_split(arr, names, axis):
    out, o = {}, 0
    for n in names:
        rows = BIG_SHAPES[n][0]
        out[n] = lax.slice_in_dim(arr, o, o + rows, axis=axis)
        o += rows
    return out


def _gathered(ag):
    out = {}
    for names, arr in zip(UNITS, ag):
        for n, blk in _unit_split(arr, names, 1).items():
            out[n] = jnp.concatenate([blk[j] for j in range(4)], axis=0 if n == 'w_out' else 1)
    return out


def _shards(n, g):
    if n == 'w_out':
        return [g[j * 512:(j + 1) * 512] for j in range(4)]
    w = BIG_SHAPES[n][1]
    return [g[:, j * w:(j + 1) * w] for j in range(4)]


def _pack_small(d, extra=None):
    flat = jnp.concatenate([d[n].reshape(-1) for n in SMALL] + ([extra.reshape(-1)] if extra is not None else []))
    return jnp.pad(flat, (0, SMALL_ROWS * LANE - flat.shape[0])).reshape(SMALL_ROWS, LANE)


def _unpack_small(packed):
    flat, out, o = packed.reshape(-1), {}, 0
    for n in SMALL:
        sz = int(np.prod(SMALL_SHAPES[n]))
        out[n] = flat[o:o + sz].reshape(SMALL_SHAPES[n])
        o += sz
    return out


def _perm_w_in(w):
    z = lambda n: jnp.zeros((w.shape[0], n), w.dtype)
    lora = []
    for i in range(4):
        lora += [w[:, 4160 + LORA * i:4160 + LORA * (i + 1)], z(LANE - LORA)]
    return jnp.concatenate([w[:, 0:1024], w[:, 1088:4160], w[:, 4544:D_IN]] + lora
                           + [w[:, 1024:1056], z(96), w[:, 1056:1088], z(96)], axis=1)


def _unperm_w_in(g):
    lora = [g[:, OFF_LORA + LANE * i:OFF_LORA + LANE * i + LORA] for i in range(4)]
    return jnp.concatenate([g[:, 0:1024], g[:, OFF_KR:OFF_KR + 32], g[:, OFF_KR + LANE:OFF_KR + LANE + 32],
                            g[:, 1024:4096]] + lora + [g[:, 4096:OFF_LORA]], axis=1)


def _perm_wq(w):
    w3 = w.reshape(Q_RANK, HEADS, NOPE + ROPE)
    return jnp.concatenate([w3[:, :, :NOPE].reshape(Q_RANK, -1), w3[:, :, NOPE:NOPE + 32].reshape(Q_RANK, -1),
                            w3[:, :, NOPE + 32:].reshape(Q_RANK, -1)], axis=1)


def _unperm_wq(g):
    return jnp.concatenate([g[:, :1024].reshape(Q_RANK, HEADS, NOPE), g[:, 1024:1280].reshape(Q_RANK, HEADS, 32),
                            g[:, 1280:].reshape(Q_RANK, HEADS, 32)], axis=2).reshape(Q_RANK, -1)


def _perm_wkv(w):
    w3 = w.reshape(KV_RANK, HEADS, NOPE + VDIM)
    return jnp.concatenate([w3[:, :, :NOPE].reshape(KV_RANK, -1), w3[:, :, NOPE:].reshape(KV_RANK, -1)], axis=1)


def _unperm_wkv(g):
    return jnp.concatenate([g[:, :1024].reshape(KV_RANK, HEADS, NOPE), g[:, 1024:].reshape(KV_RANK, HEADS, VDIM)],
                           axis=2).reshape(KV_RANK, -1)


def _pad_rows(w):
    return jnp.pad(w, ((0, LANE - LORA), (0, 0)))


def _perm_mu(mu):
    parts = [mu[:3072]]
    for i in range(4):
        parts += [mu[3072 + LORA * i:3072 + LORA * (i + 1)], jnp.zeros((LANE - LORA,), mu.dtype)]
    return jnp.concatenate(parts).reshape(1, NLERP)


def _unperm_mu(g):
    g = g.reshape(-1)
    return jnp.concatenate([g[:3072]] + [g[3072 + LANE * i:3072 + LANE * i + LORA] for i in range(4)])


def _to_heads(t, n):
    return t.reshape(T, -1, n).transpose(1, 0, 2)


def _from_heads(t):
    return t.transpose(1, 0, 2).reshape(T, -1)


def _to_qk(n, r1, r2):
    return jnp.concatenate([n.reshape(T, HEADS, NOPE), r1.reshape(T, HEADS, 32), r2.reshape(T, HEADS, 32)],
                           axis=2).transpose(1, 0, 2)


def _from_qk(g):
    g = g.transpose(1, 0, 2)
    return g[:, :, :NOPE].reshape(T, -1), g[:, :, NOPE:NOPE + 32].reshape(T, -1), g[:, :, NOPE + 32:].reshape(T, -1)


def _constants():
    g2 = np.kron(np.eye(2, dtype=np.float32), np.ones((RN, RN), np.float32))
    e = np.zeros((LANE, 256), np.float32)
    for h in range(HEADS):
        e[np.arange(32), h * 32 + np.arange(32)] = 1.0
    pos = jnp.arange(T, dtype=F32)
    inv_freq = jnp.power(ROPE_THETA, -jnp.arange(0, ROPE, 2, dtype=F32) / ROPE)
    ang = pos[:, None] * inv_freq[None, :]
    cos, sin = jnp.cos(ang), jnp.sin(ang)
    padk = lambda t: jnp.pad(t, ((0, 0), (0, LANE - 32)))
    return (jnp.asarray(g2, BF16), jnp.asarray(e, BF16), jnp.tile(cos, (1, HEADS)), jnp.tile(sin, (1, HEADS)),
            padk(cos), padk(sin))


def _step(x, tgt, w, m, v):
    x2, tgt2 = x.reshape(T, D), tgt.reshape(T, D)
    g2, e_mat, cosq, sinq, cosk, sink = _constants()
    row = lambda n: w[n].reshape(1, -1)

    full = _gathered(gather_weights([_unit_cat([w[n].astype(BF16) for n in u]) for u in UNITS]))
    wp = _perm_w_in(full['w_in'])
    wq = _perm_wq(full['mla_wq_b']).astype(F32)
    wkv = _perm_wkv(full['mla_wkv_b']).astype(F32)
    lora_w = [_pad_rows(full[n]).astype(F32) for n in ('rwkv_w2_f', 'rwkv_w2_b', 'rwkv_a2_f', 'rwkv_a2_b')]
    mu_p = _perm_mu(w['rwkv_mu'])

    st_pre = Stage("pre", f_pre, [(D, BF16), (D, None)], 256, [0], [0], [F32])
    st_mla = Stage("mla", f_mla, [(1024, BF16), (256, BF16), (256, BF16), (1024, BF16), (256, BF16), (256, BF16),
                                  (1024, BF16)], 256, [0, 1, 2], [0, 1, 2, 3], [BF16] * 3)
    st_rpre = Stage("rwkv_pre", f_rwkv_pre, [(RW, F32)] * 9, 128, [0, 1, 2, 3], list(range(10)), [F32] * 4)
    st_rpost = Stage("rwkv_post", f_rwkv_post, [(RW, BF16)], 256, [0, 2, 3, 4, 5, 6], [0, 1, 2],
                     [F32, F32, F32, F32, F32, BF16])
    st_gate = Stage("gate", f_gate, [(RW, BF16)], 256, [0, 1], [], [F32, BF16])
    st_merge = Stage("merge", f_merge, [(D, BF16)], 256, [0, 1, 2, 3], [], [BF16] * 4)

    pre_rows, pre_par = [(x2, D, 0)], [row('g_pre')]
    (h,) = st_pre.fwd(pre_rows, pre_par)
    proj = matmul("mm_in", h, wp, "nn")

    mla_rows = [(proj, 512, OFF_QA // 512), (proj, 512, OFF_KVA // 512), (proj, 256, OFF_KR // 256),
                (cosq, 256, 0), (sinq, 256, 0), (cosk, LANE, 0), (sink, LANE, 0)]
    mla_par = [row('mla_q_norm'), row('mla_kv_norm'), wq, wkv, e_mat]
    qn, qr1, qr2, kn, kr1, kr2, vv = st_mla.fwd(mla_rows, mla_par)
    qh, kh, vh = _to_qk(qn, qr1, qr2), _to_qk(kn, kr1, kr2), _to_heads(vv, VDIM)
    y_mla = _from_heads(attn_fwd(qh, kh, vh))

    lerp = shift_fwd(proj, mu_p)
    rpre_rows = [(lerp, RW, 0), (lerp, RW, 1), (lerp, RW, 2), (lerp, 512, 6)]
    rpre_par = [row('rwkv_w0_f'), row('rwkv_w0_b'), row('rwkv_a0_f'), row('rwkv_a0_b'), row('rwkv_k_k'),
                row('rwkv_k_a')] + lora_w + [g2]
    r_, v_, lwf, lwb, kf, kb, an, bf_, bb_ = st_rpre.fwd(rpre_rows, rpre_par)
    fin = [r_, lwf, kf, v_, an, bf_]
    bin_ = [r_, lwb, kb, v_, an, bb_]
    yf, h0f = scan_fwd("scan_f", *fin, reverse=False)
    yb, h0b = scan_fwd("scan_b", *bin_, reverse=True)
    rpost_rows = [(yf, RW, 0), (yb, RW, 0), (r_, RW, 0), (kf, RW, 0), (kb, RW, 0),
                  (v_, RW, 0), (proj, RW, OFF_ZR // RW)]
    rpost_par = [row('rwkv_gn_g'), row('rwkv_gn_b'), row('rwkv_r_k'), g2]
    (gr,) = st_rpost.fwd(rpost_rows, rpost_par)
    gate_rows = [(y_mla, RW, 0), (proj, RW, OFF_ZM // RW)]
    (gm,) = st_gate.fwd(gate_rows, [])
    um = matmul("mm_br_mla", gm, full['w_br_mla'], "nn")
    ur = matmul("mm_br_rwkv", gr, full['w_br_rwkv'], "nn")
    merge_rows = [(um, D, 0), (ur, D, 0), (proj, D, OFF_GM // D), (proj, D, OFF_GR // D)]
    (merged,) = st_merge.fwd(merge_rows, [])
    out = matmul("mm_out", merged, full['w_out'], "nn")
    d_out, dy, dg_post, loss_blk = loss_stage(out, x2, tgt2, row('g_post'))

    gw = {'g_post': dg_post}
    d_merged = matmul("mm_out_dx", d_out, full['w_out'], "nt")
    gw['w_out'] = matmul("mm_out_dw", merged, d_out, "tn")
    (d_um, d_ur, d_gm, d_gr), _ = st_merge.bwd(merge_rows, [], [[(d_merged, D, 0)]])
    d_gmla = matmul("mm_br_mla_dx", d_um, full['w_br_mla'], "nt")
    gw['w_br_mla'] = matmul("mm_br_mla_dw", gm, d_um, "tn")
    d_grw = matmul("mm_br_rwkv_dx", d_ur, full['w_br_rwkv'], "nt")
    gw['w_br_rwkv'] = matmul("mm_br_rwkv_dw", gr, d_ur, "tn")
    (d_ymla, d_zm), _ = st_gate.bwd(gate_rows, [], [[(d_gmla, RW, 0)]])
    (d_y, d_r3, d_kf2, d_kb2, d_v3, d_zr), (gw['rwkv_gn_g'], gw['rwkv_gn_b'], d_rk) = st_rpost.bwd(
        rpost_rows, rpost_par, [[(d_grw, RW, 0)]])
    gw['rwkv_r_k'] = d_rk
    sf = scan_bwd("scan_f_bwd", *fin, h0f, d_y, reverse=False)
    sb = scan_bwd("scan_b_bwd", *bin_, h0b, d_y, reverse=True)
    c = lambda *ts: [(t, RW, 0) for t in ts]
    rpre_cts = [c(sf[0], sb[0], d_r3), c(sf[3], sb[3], d_v3), c(sf[1]), c(sb[1]), c(sf[2], d_kf2), c(sb[2], d_kb2),
                c(sf[4], sb[4]), c(sf[5]), c(sb[5])]
    d_lerp_rows, rpre_g = st_rpre.bwd(rpre_rows, rpre_par, rpre_cts)
    for n, gval in zip(('rwkv_w0_f', 'rwkv_w0_b', 'rwkv_a0_f', 'rwkv_a0_b', 'rwkv_k_k', 'rwkv_k_a'), rpre_g[:6]):
        gw[n] = gval
    for n, gval in zip(('rwkv_w2_f', 'rwkv_w2_b', 'rwkv_a2_f', 'rwkv_a2_b'), rpre_g[6:]):
        gw[n] = gval[:LORA]
    d_lerp, d_mu = shift_bwd(proj, mu_p, jnp.concatenate(d_lerp_rows, axis=1))
    gw['rwkv_mu'] = _unperm_mu(d_mu)

    dqh, dkh, dvh = attn_bwd(qh, kh, vh, _to_heads(d_ymla, VDIM))
    mla_cts = [[(t, t.shape[1], 0)] for t in (*_from_qk(dqh), *_from_qk(dkh), _from_heads(dvh))]
    (d_qa, d_kva, d_kr), (gw['mla_q_norm'], gw['mla_kv_norm'], d_wq, d_wkv) = st_mla.bwd(mla_rows, mla_par, mla_cts)
    gw['mla_wq_b'], gw['mla_wkv_b'] = _unperm_wq(d_wq), _unperm_wkv(d_wkv)

    dproj = jnp.concatenate([d_qa, d_kva, d_lerp[:, :3072], d_zm, d_zr, d_gm, d_gr, d_lerp[:, 3072:], d_kr], axis=1)
    dh = matmul("mm_in_dx", dproj, wp, "nt")
    gw['w_in'] = _unperm_w_in(matmul("mm_in_dw", h, dproj, "tn"))
    (grad_x,), (gw['g_pre'],) = st_pre.bwd(pre_rows, pre_par, [[(dh, D, 0)], [(dy, D, 0)]])

    shards = {n: _shards(n, gw[n]) for n in BIG}
    send = [jnp.stack([_unit_cat([shards[n][j].astype(BF16) for n in u]) for j in range(4)]) for u in UNITS]
    keep, other = pair_exchange(send)
    pairs = [pair_sum(f"pair_sum_{i}", a, b) for i, (a, b) in enumerate(zip(keep, other))]
    recv = scatter_grads(pairs)
    grads = join_halves([sum4(f"sum4_{i}", r) for i, r in enumerate(recv)])
    big = [dict() for _ in range(4)]
    for i, (u, g_u) in enumerate(zip(UNITS, grads)):
        res = adamw(f"adamw_{i}", *[_unit_cat([t[n] for n in u]) for t in (w, m, v)], [g_u])
        for q in range(4):
            big[q].update(_unit_split(res[q], u, 0))
    parts = allgather8("gather_small", _pack_small(gw, loss_blk[0, :1]))
    small = adamw("adamw_small", _pack_small(w), _pack_small(m), _pack_small(v), [parts[i] for i in range(8)])

    outs = []
    for b_d, s_arr in zip(big, small):
        d = {**b_d, **_unpack_small(s_arr)}
        outs.append([d[n] for n in WEIGHTS])
    loss = small[0][SMALL_LEN // LANE, 0]
    return (loss, grad_x.reshape(1, T, D), *outs[0], *outs[1], *outs[2], *outs[3])


def kernel(x, g_pre, w_in, mla_q_norm, mla_wq_b, mla_kv_norm, mla_wkv_b, rwkv_mu, rwkv_w0_f, rwkv_w2_f, rwkv_w0_b, rwkv_w2_b, rwkv_a0_f, rwkv_a2_f, rwkv_a0_b, rwkv_a2_b, rwkv_k_k, rwkv_k_a, rwkv_r_k, rwkv_gn_g, rwkv_gn_b, w_br_mla, w_br_rwkv, w_out, g_post, loss_target, m_g_pre, m_w_in, m_mla_q_norm, m_mla_wq_b, m_mla_kv_norm, m_mla_wkv_b, m_rwkv_mu, m_rwkv_w0_f, m_rwkv_w2_f, m_rwkv_w0_b, m_rwkv_w2_b, m_rwkv_a0_f, m_rwkv_a2_f, m_rwkv_a0_b, m_rwkv_a2_b, m_rwkv_k_k, m_rwkv_k_a, m_rwkv_r_k, m_rwkv_gn_g, m_rwkv_gn_b, m_w_br_mla, m_w_br_rwkv, m_w_out, m_g_post, v_g_pre, v_w_in, v_mla_q_norm, v_mla_wq_b, v_mla_kv_norm, v_mla_wkv_b, v_rwkv_mu, v_rwkv_w0_f, v_rwkv_w2_f, v_rwkv_w0_b, v_rwkv_w2_b, v_rwkv_a0_f, v_rwkv_a2_f, v_rwkv_a0_b, v_rwkv_a2_b, v_rwkv_k_k, v_rwkv_k_a, v_rwkv_r_k, v_rwkv_gn_g, v_rwkv_gn_b, v_w_br_mla, v_w_br_rwkv, v_w_out, v_g_post):
    given = dict(locals())
    w = {n: given[n] for n in WEIGHTS}
    m = {n: given['m_' + n] for n in WEIGHTS}
    v = {n: given['v_' + n] for n in WEIGHTS}
    return _step(x, loss_target, w, m, v)
```

```python
import functools
import math

import numpy as np
import jax
import jax.numpy as jnp
from jax import lax
from jax.experimental import pallas as pl
from jax.experimental.pallas import tpu as pltpu

F32, BF16 = jnp.float32, jnp.bfloat16
MESH_IDS = pl.DeviceIdType.MESH

D = 2048
T = 2048
HEADS = 8
Q_RANK = 512
KV_RANK = 512
NOPE = 128
ROPE = 64
VDIM = 128
RW = 1024
RH = 16
RN = 64
LORA = 96
D_IN = 10688
NORM_EPS = 1e-6
GN_EPS = 64e-5
ROPE_THETA = 10000.0
ADAM_LR, ADAM_B1, ADAM_B2, ADAM_EPS, ADAM_WD, ADAM_STEP = 0.001, 0.9, 0.999, 1e-08, 0.01, 10

LANE = 128
VMEM_BIG = 56 * 2**20

NP = 11008
OFF_QA, OFF_KVA, OFF_RKV, OFF_ZM, OFF_ZR, OFF_GM, OFF_GR, OFF_LORA, OFF_KR = 0, 512, 1024, 4096, 5120, 6144, 8192, 10240, 10752
NLERP = 3584

CHUNK = 64
NCH = T // CHUNK


def _dg(a, b, ca, cb, batch=False, prec=None):
    bd = ((0,), (0,)) if batch else ((), ())
    return lax.dot_general(a, b, (((ca,), (cb,)), bd), precision=prec, preferred_element_type=F32)


@jax.custom_vjp
def bdot(a, b):
    return _dg(a.astype(BF16), b.astype(BF16), 1, 0)


def _bdot_fwd(a, b):
    return bdot(a, b), (a, b)


def _bdot_bwd(res, g):
    a, b = res
    gb = g.astype(BF16)
    da = _dg(gb, b.astype(BF16), 1, 1)
    db = _dg(a.astype(BF16), gb, 0, 0)
    return da.astype(a.dtype), db.astype(b.dtype)


bdot.defvjp(_bdot_fwd, _bdot_bwd)


def _split(x):
    hi = x.astype(BF16)
    lo = (x - hi.astype(F32)).astype(BF16)
    return hi, lo


@jax.custom_vjp
def gsum(x, g2):
    hi, lo = _split(x)
    return _dg(hi, g2, 1, 0) + _dg(lo, g2, 1, 0)


def _gsum_fwd(x, g2):
    return gsum(x, g2), g2


def _gsum_bwd(g2, g):
    hi, lo = _split(g)
    return _dg(hi, g2, 1, 1) + _dg(lo, g2, 1, 1), jnp.zeros_like(g2)


gsum.defvjp(_gsum_fwd, _gsum_bwd)


def headsum(x, g2):
    return jnp.concatenate([gsum(x[:, i * LANE:(i + 1) * LANE], g2) for i in range(x.shape[1] // LANE)], axis=1)


def _terms(x, n):
    out = []
    for i in range(n):
        t = x.astype(BF16)
        out.append(t)
        if i < n - 1:
            x = x - t.astype(F32)
    return out


def _bmm(a, b, ca, cb, na, nb):
    acc = None
    for i, ai in enumerate(_terms(a, na)):
        for j, bj in enumerate(_terms(b, nb)):
            if i + j < max(na, nb):
                p = _dg(ai, bj, ca, cb, True)
                acc = p if acc is None else acc + p
    return acc


_NN, _NT, _TN = (2, 1), (2, 2), (1, 1)


def _make_dots(nf, nb_nn, nb_nt, nb_tn):
    @jax.custom_vjp
    def nn(a, b):
        return _bmm(a, b, *_NN, nf, nf)

    @jax.custom_vjp
    def nt(a, b):
        return _bmm(a, b, *_NT, nf, nf)

    @jax.custom_vjp
    def tn(a, b):
        return _bmm(a, b, *_TN, nf, nf)

    nn.defvjp(lambda a, b: (nn(a, b), (a, b)),
              lambda r, g: (_bmm(g, r[1], *_NT, nb_nn, nb_nn), _bmm(r[0], g, *_TN, nb_nn, nb_nn)))
    nt.defvjp(lambda a, b: (nt(a, b), (a, b)),
              lambda r, g: (_bmm(g, r[1], *_NN, nb_nt, nb_nt), _bmm(g, r[0], *_TN, nb_nt, nb_nt)))
    tn.defvjp(lambda a, b: (tn(a, b), (a, b)),
              lambda r, g: (_bmm(r[1], g, *_NT, nb_tn, nb_tn), _bmm(r[0], g, *_NN, nb_tn, nb_tn)))
    return nn, nt, tn


_SCAN_NF, _SCAN_NB = 1, 1
nn, nt, tn = _make_dots(_SCAN_NF, 1, 2, 1)


@jax.custom_vjp
def cumdot(ones, x):
    return _bmm(ones, x, *_NN, 1, 3)


cumdot.defvjp(lambda o, x: (cumdot(o, x), o), lambda o, g: (jnp.zeros_like(o), _bmm(o, g, *_TN, 1, 3)))


def _solve_powers(l):
    pw = [l]
    for _ in range(int(math.log2(l.shape[-1])) - 1):
        pw.append(_bmm(pw[-1], pw[-1], *_NN, _SCAN_NF, _SCAN_NF))
    return pw


@jax.custom_vjp
def tri_solve(l, rhs):
    x = rhs
    for p in _solve_powers(l):
        x = x + _bmm(p, x, *_NN, _SCAN_NF, _SCAN_NF)
    return x


def _tri_solve_fwd(l, rhs):
    pw = _solve_powers(l)
    x = rhs
    for p in pw:
        x = x + _bmm(p, x, *_NN, _SCAN_NF, _SCAN_NF)
    return x, (pw, x)


def _tri_solve_bwd(res, g):
    pw, x = res
    y = g
    for p in pw:
        y = y + _bmm(p, y, *_TN, _SCAN_NB, _SCAN_NB)
    return _bmm(y, x, *_NT, _SCAN_NB, _SCAN_NB), y


tri_solve.defvjp(_tri_solve_fwd, _tri_solve_bwd)


def _rms(x, g):
    return x * lax.rsqrt(jnp.mean(x * x, axis=-1, keepdims=True) + NORM_EPS) * g


def _softplus(x):
    pos = x > 0
    return jnp.where(pos, x, 0.0) + jnp.log(1.0 + jnp.exp(-jnp.where(pos, x, -x)))


def _silu(z):
    return z * jax.nn.sigmoid(z)


def _tile(n, cands):
    for c in cands:
        if n % c == 0:
            return c
    raise ValueError(n)


_MM_OPERAND_BYTES = 24 * 2**20


def _mm_tiles(m, n, k):
    tn_ = _tile(n, (512, 256, 128))
    best = None
    for tm in (2048, 1024, 512, 256, 128):
        if m % tm:
            continue
        for d in range(k // LANE, 0, -1):
            tk = LANE * d
            if k % tk == 0 and 4 * tk * (tm + tn_) <= _MM_OPERAND_BYTES:
                if best is None or tm * tk > best[0] * best[2]:
                    best = (tm, tn_, tk)
                break
    return best


def matmul(name, a, b, mode, out_dtype=F32):
    if mode == "nn":
        (m, k), n = a.shape, b.shape[1]
    elif mode == "nt":
        (m, k), n = a.shape, b.shape[0]
    else:
        (k, m), n = a.shape, b.shape[1]
    tm, tn_, tk = _mm_tiles(m, n, k)
    nk = k // tk
    if mode == "nn":
        a_spec = pl.BlockSpec((tm, tk), lambda i, j, kk: (i, kk))
        b_spec = pl.BlockSpec((tk, tn_), lambda i, j, kk: (kk, j))
        ca, cb = 1, 0
    elif mode == "nt":
        a_spec = pl.BlockSpec((tm, tk), lambda i, j, kk: (i, kk))
        b_spec = pl.BlockSpec((tn_, tk), lambda i, j, kk: (j, kk))
        ca, cb = 1, 1
    else:
        a_spec = pl.BlockSpec((tk, tm), lambda i, j, kk: (kk, i))
        b_spec = pl.BlockSpec((tk, tn_), lambda i, j, kk: (kk, j))
        ca, cb = 0, 0

    def body(a_ref, b_ref, o_ref, *acc):
        part = _dg(a_ref[...].astype(BF16), b_ref[...].astype(BF16), ca, cb)
        if nk == 1:
            o_ref[...] = part.astype(o_ref.dtype)
            return
        acc_ref, kk = acc[0], pl.program_id(2)

        @pl.when(kk == 0)
        def _():
            acc_ref[...] = part

        @pl.when(kk > 0)
        def _():
            acc_ref[...] += part

        @pl.when(kk == nk - 1)
        def _():
            o_ref[...] = acc_ref[...].astype(o_ref.dtype)

    return pl.pallas_call(
        body, name=name, grid=(m // tm, n // tn_, nk),
        in_specs=[a_spec, b_spec],
        out_specs=pl.BlockSpec((tm, tn_), lambda i, j, kk: (i, j)),
        out_shape=jax.ShapeDtypeStruct((m, n), out_dtype),
        scratch_shapes=[pltpu.VMEM((tm, tn_), F32)] if nk > 1 else [],
        compiler_params=pltpu.CompilerParams(dimension_semantics=("parallel", "parallel", "arbitrary"),
                                             vmem_limit_bytes=VMEM_BIG),
    )(a, b)


def _rspec(tr, width, blk):
    return pl.BlockSpec((tr, width), lambda i: (i, blk))


def _full_spec(arr):
    return pl.BlockSpec(arr.shape, lambda i: (0,) * arr.ndim)


class Stage:
    def __init__(self, name, f, outs, tr, diff_rows, diff_params, drow_dtypes):
        self.name, self.f, self.outs, self.tr = name, f, outs, tr
        self.diff_rows, self.diff_params, self.drow_dtypes = diff_rows, diff_params, drow_dtypes

    def fwd(self, rows, params):
        f, nr, npar = self.f, len(rows), len(params)
        stored = [(w, dt) for (w, dt) in self.outs if dt is not None]
        keep = [i for i, (w, dt) in enumerate(self.outs) if dt is not None]

        def body(*refs):
            vals = f(*[r[...].astype(F32) for r in refs[:nr]], *[p[...] for p in refs[nr:nr + npar]])
            for o_ref, i in zip(refs[nr + npar:], keep):
                o_ref[...] = vals[i].astype(o_ref.dtype)

        return pl.pallas_call(
            body, name=self.name + "_fwd", grid=(T // self.tr,),
            in_specs=[_rspec(self.tr, w, b) for (_, w, b) in rows] + [_full_spec(p) for p in params],
            out_specs=[_rspec(self.tr, w, 0) for (w, _) in stored],
            out_shape=[jax.ShapeDtypeStruct((T, w), dt) for (w, dt) in stored],
            compiler_params=pltpu.CompilerParams(dimension_semantics=("arbitrary",), vmem_limit_bytes=VMEM_BIG),
        )(*[r[0] for r in rows], *params)

    def bwd(self, rows, params, cts):
        f, nr, npar = self.f, len(rows), len(params)
        dr_idx, dp_idx = self.diff_rows, self.diff_params
        flat_cts = [c for lst in cts for c in lst]
        nct = len(flat_cts)

        def body(*refs):
            row_refs, par_refs = refs[:nr], refs[nr:nr + npar]
            ct_refs = refs[nr + npar:nr + npar + nct]
            drow_refs = refs[nr + npar + nct:nr + npar + nct + len(dr_idx)]
            dpar_refs = refs[nr + npar + nct + len(dr_idx):]
            row_vals = [r[...].astype(F32) for r in row_refs]
            par_vals = [p[...] for p in par_refs]

            def g(*dv):
                rv, pv = list(row_vals), list(par_vals)
                for j, i in enumerate(dr_idx):
                    rv[i] = dv[j]
                for j, i in enumerate(dp_idx):
                    pv[i] = dv[len(dr_idx) + j]
                return f(*rv, *pv)

            _, vjp = jax.vjp(g, *[row_vals[i] for i in dr_idx], *[par_vals[i] for i in dp_idx])
            ct_vals, pos = [], 0
            for lst in cts:
                acc = ct_refs[pos][...].astype(F32)
                for q in range(1, len(lst)):
                    acc = acc + ct_refs[pos + q][...].astype(F32)
                pos += len(lst)
                ct_vals.append(acc)
            grads = vjp(tuple(ct_vals))
            for j, r in enumerate(drow_refs):
                r[...] = grads[j].astype(r.dtype)

            @pl.when(pl.program_id(0) == 0)
            def _():
                for r in dpar_refs:
                    r[...] = jnp.zeros_like(r)

            for j, r in enumerate(dpar_refs):
                r[...] += grads[len(dr_idx) + j].astype(F32)

        drow_shapes = [jax.ShapeDtypeStruct((T, rows[i][1]), dt) for i, dt in zip(dr_idx, self.drow_dtypes)]
        dpar_shapes = [jax.ShapeDtypeStruct(params[i].shape, F32) for i in dp_idx]
        res = pl.pallas_call(
            body, name=self.name + "_bwd", grid=(T // self.tr,),
            in_specs=[_rspec(self.tr, w, b) for (_, w, b) in rows] + [_full_spec(p) for p in params]
            + [_rspec(self.tr, w, b) for (_, w, b) in flat_cts],
            out_specs=[_rspec(self.tr, rows[i][1], 0) for i in dr_idx] + [_full_spec(params[i]) for i in dp_idx],
            out_shape=drow_shapes + dpar_shapes,
            compiler_params=pltpu.CompilerParams(dimension_semantics=("arbitrary",), vmem_limit_bytes=VMEM_BIG),
        )(*[r[0] for r in rows], *params, *[c[0] for c in flat_cts])
        return res[:len(dr_idx)], res[len(dr_idx):]


def f_pre(x, g):
    return _rms(x, g), x


def f_mla(q_a, kv_a, kr, cosq, sinq, cosk, sink, gq, gkv, wq, wkv, e):
    q = bdot(_rms(q_a, gq), wq)
    kv = bdot(_rms(kv_a, gkv), wkv)
    t1, t2 = q[:, 1024:1280], q[:, 1280:1536]
    k1, k2 = kr[:, :LANE], kr[:, LANE:]
    kr1 = k1 * cosk - k2 * sink
    kr2 = k1 * sink + k2 * cosk
    return (q[:, :1024], t1 * cosq - t2 * sinq, t1 * sinq + t2 * cosq,
            kv[:, :1024], bdot(kr1, e), bdot(kr2, e), kv[:, 1024:])


def f_rwkv_pre(r, k, v, lora, w0f, w0b, a0f, a0b, kkw, kaw, w2f, w2b, a2f, a2b, g2):
    wdf, wdb, adf, adb = (lora[:, i * LANE:(i + 1) * LANE] for i in range(4))

    def logdecay(w0, wd, w2):
        z = w0 + bdot(jnp.tanh(wd), w2)
        return -jnp.exp(-_softplus(-z) - 0.5)

    a_f = jax.nn.sigmoid(a0f + bdot(adf, a2f))
    a_b = jax.nn.sigmoid(a0b + bdot(adb, a2b))
    kk = k * kkw
    kk = kk / jnp.maximum(jnp.sqrt(headsum(kk * kk, g2)), 1e-12)
    return (r, v, logdecay(w0f, wdf, w2f), logdecay(w0b, wdb, w2b),
            k * (1.0 + (a_f - 1.0) * kaw), k * (1.0 + (a_b - 1.0) * kaw), -kk, kk * a_f, kk * a_b)


def f_rwkv_post(yf, yb, r, kf, kb, v, z, gng, gnb, rk, g2):
    y = yf + yb
    mu = headsum(y, g2) * (1.0 / RN)
    d = y - mu
    var = headsum(d * d, g2) * (1.0 / RN)
    yn = d * lax.rsqrt(var + GN_EPS) * gng + gnb
    bonus = headsum(r * (kf + kb) * rk, g2) * v
    return ((yn + bonus) * _silu(z),)


def f_gate(y, z):
    return (y * _silu(z),)


def f_merge(um, ur, gm, gr):
    return (jax.nn.sigmoid(gm) * um + jax.nn.sigmoid(gr) * ur,)


_SHIFT_W = 256


def _lerp_colblock(j):
    return jnp.where(j < 3072 // _SHIFT_W, OFF_RKV // _SHIFT_W + j, OFF_LORA // _SHIFT_W + j - 3072 // _SHIFT_W)


def _nbr_mean(x):
    row = lax.broadcasted_iota(jnp.int32, x.shape, 0)
    up = jnp.where(row == 0, 0.0, pltpu.roll(x, 1, 0))
    dn = jnp.where(row == T - 1, 0.0, pltpu.roll(x, T - 1, 0))
    return 0.5 * (up + dn)


def shift_fwd(proj, mu):
    def body(x_ref, mu_ref, o_ref):
        x = x_ref[...]
        o_ref[...] = x + mu_ref[...] * (_nbr_mean(x) - x)

    return pl.pallas_call(
        body, name="shift_fwd", grid=(NLERP // _SHIFT_W,),
        in_specs=[pl.BlockSpec((T, _SHIFT_W), lambda j: (0, _lerp_colblock(j))),
                  pl.BlockSpec((1, _SHIFT_W), lambda j: (0, j))],
        out_specs=pl.BlockSpec((T, _SHIFT_W), lambda j: (0, j)),
        out_shape=jax.ShapeDtypeStruct((T, NLERP), F32),
        compiler_params=pltpu.CompilerParams(dimension_semantics=("parallel",), vmem_limit_bytes=VMEM_BIG),
    )(proj, mu)


def shift_bwd(proj, mu, g):
    def body(x_ref, mu_ref, g_ref, dx_ref, dmu_ref):
        x, gv = x_ref[...], g_ref[...]
        dmu_ref[...] = jnp.sum(gv * (_nbr_mean(x) - x), axis=0, keepdims=True)
        gm = gv * mu_ref[...]
        dx_ref[...] = (gv - gm + _nbr_mean(gm)).astype(dx_ref.dtype)

    col = pl.BlockSpec((T, _SHIFT_W), lambda j: (0, j))
    vec = pl.BlockSpec((1, _SHIFT_W), lambda j: (0, j))
    return pl.pallas_call(
        body, name="shift_bwd", grid=(NLERP // _SHIFT_W,),
        in_specs=[pl.BlockSpec((T, _SHIFT_W), lambda j: (0, _lerp_colblock(j))), vec, col],
        out_specs=[col, vec],
        out_shape=[jax.ShapeDtypeStruct((T, NLERP), BF16), jax.ShapeDtypeStruct((1, NLERP), F32)],
        compiler_params=pltpu.CompilerParams(dimension_semantics=("parallel",), vmem_limit_bytes=VMEM_BIG),
    )(proj, mu, g)


_TQ = 256
_ATT_SCALE = (NOPE + ROPE) ** -0.5


def _probs(q, k):
    s = _dg(q, k, 1, 1) * _ATT_SCALE
    e = jnp.exp(s - jnp.max(s, axis=-1, keepdims=True))
    return e * (1.0 / jnp.sum(e, axis=-1, keepdims=True))


def attn_fwd(q, k, v):
    def body(q_ref, k_ref, v_ref, o_ref):
        p = _probs(q_ref[0], k_ref[0])
        o_ref[0] = _dg(p.astype(BF16), v_ref[0], 1, 0)

    dq = NOPE + ROPE
    return pl.pallas_call(
        body, name="attn_fwd", grid=(HEADS, T // _TQ),
        in_specs=[pl.BlockSpec((1, _TQ, dq), lambda h, i: (h, i, 0)),
                  pl.BlockSpec((1, T, dq), lambda h, i: (h, 0, 0)),
                  pl.BlockSpec((1, T, VDIM), lambda h, i: (h, 0, 0))],
        out_specs=pl.BlockSpec((1, _TQ, VDIM), lambda h, i: (h, i, 0)),
        out_shape=jax.ShapeDtypeStruct((HEADS, T, VDIM), F32),
        compiler_params=pltpu.CompilerParams(dimension_semantics=("parallel", "arbitrary"), vmem_limit_bytes=VMEM_BIG),
    )(q, k, v)


def attn_bwd(q, k, v, do):
    def body(q_ref, k_ref, v_ref, do_ref, dq_ref, dk_ref, dv_ref):
        @pl.when(pl.program_id(1) == 0)
        def _():
            dk_ref[...] = jnp.zeros_like(dk_ref)
            dv_ref[...] = jnp.zeros_like(dv_ref)

        qv, kv_, vv = q_ref[0], k_ref[0], v_ref[0]
        dob = do_ref[0].astype(BF16)
        p = _probs(qv, kv_)
        dv_ref[0] += _dg(p.astype(BF16), dob, 0, 0)
        dp = _dg(dob, vv, 1, 1)
        ds = (p * (dp - jnp.sum(dp * p, axis=-1, keepdims=True)) * _ATT_SCALE).astype(BF16)
        dq_ref[0] = _dg(ds, kv_, 1, 0)
        dk_ref[0] += _dg(ds, qv, 0, 0)

    dq = NOPE + ROPE
    return pl.pallas_call(
        body, name="attn_bwd", grid=(HEADS, T // _TQ),
        in_specs=[pl.BlockSpec((1, _TQ, dq), lambda h, i: (h, i, 0)),
                  pl.BlockSpec((1, T, dq), lambda h, i: (h, 0, 0)),
                  pl.BlockSpec((1, T, VDIM), lambda h, i: (h, 0, 0)),
                  pl.BlockSpec((1, _TQ, VDIM), lambda h, i: (h, i, 0))],
        out_specs=[pl.BlockSpec((1, _TQ, dq), lambda h, i: (h, i, 0)),
                   pl.BlockSpec((1, T, dq), lambda h, i: (h, 0, 0)),
                   pl.BlockSpec((1, T, VDIM), lambda h, i: (h, 0, 0))],
        out_shape=[jax.ShapeDtypeStruct((HEADS, T, dq), F32), jax.ShapeDtypeStruct((HEADS, T, dq), F32),
                   jax.ShapeDtypeStruct((HEADS, T, VDIM), F32)],
        compiler_params=pltpu.CompilerParams(dimension_semantics=("parallel", "arbitrary"), vmem_limit_bytes=VMEM_BIG),
    )(q, k, v, do)


def _chunk(r, lw, k, v, a, b, ht, *, reverse):
    hb, c, _ = r.shape
    ti = lax.broadcasted_iota(jnp.int32, (c, c), 0)
    si = lax.broadcasted_iota(jnp.int32, (c, c), 1)
    incl = (si >= ti) if reverse else (si <= ti)
    strict = (si > ti) if reverse else (si < ti)
    ones = jnp.broadcast_to(incl.astype(F32)[None], (hb, c, c))
    cum = cumdot(ones, lw)
    cum_ex = cum - lw
    tot = jnp.sum(lw, axis=1, keepdims=True)
    mid = 0.5 * tot
    rt, at = r * jnp.exp(cum - mid), a * jnp.exp(cum_ex - mid)
    einv = jnp.exp(mid - cum)
    kt, bt = k * einv, b * einv
    m_ab = jnp.where(strict, nt(at, bt), 0.0)
    m_ak = jnp.where(strict, nt(at, kt), 0.0)
    m_rb = jnp.where(incl, nt(rt, bt), 0.0)
    m_rk = jnp.where(incl, nt(rt, kt), 0.0)
    u = tri_solve(m_ab, nt(a * jnp.exp(cum_ex), ht) + nn(m_ak, v))
    y = nt(r * jnp.exp(cum), ht) + nn(m_rb, u) + nn(m_rk, v)
    eend = jnp.exp(tot - cum)
    ht_new = ht * jnp.exp(tot) + tn(u, b * eend) + tn(v, k * eend)
    return y, ht_new


_HB_F, _HB_B = 16, 8


def _split_heads(x):
    return jnp.stack([x[:, i * RN:(i + 1) * RN] for i in range(x.shape[1] // RN)])


def _merge_heads(y):
    return jnp.concatenate([y[i] for i in range(y.shape[0])], axis=1)


def _chunk_map(reverse, backward):
    flip = reverse != backward
    return (lambda g, c: (NCH - 1 - c, g)) if flip else (lambda g, c: (c, g))


def scan_fwd(name, r, lw, k, v, a, b, reverse):
    hb = _HB_F
    cmap = _chunk_map(reverse, False)

    def body(r_ref, lw_ref, k_ref, v_ref, a_ref, b_ref, y_ref, h0_ref, ht_ref):
        @pl.when(pl.program_id(1) == 0)
        def _():
            ht_ref[...] = jnp.zeros_like(ht_ref)

        ht = ht_ref[...]
        h0_ref[0] = ht
        ins = [_split_heads(x[...]) for x in (r_ref, lw_ref, k_ref, v_ref, a_ref, b_ref)]
        y, hn = _chunk(*ins, ht, reverse=reverse)
        y_ref[...] = _merge_heads(y)
        ht_ref[...] = hn

    io = pl.BlockSpec((CHUNK, hb * RN), cmap)
    return pl.pallas_call(
        body, name=name, grid=(RH // hb, NCH),
        in_specs=[io] * 6,
        out_specs=[io, pl.BlockSpec((1, hb, RN, RN), lambda g, c: (cmap(g, c)[0], g, 0, 0))],
        out_shape=[jax.ShapeDtypeStruct((T, RW), F32), jax.ShapeDtypeStruct((NCH, RH, RN, RN), F32)],
        scratch_shapes=[pltpu.VMEM((hb, RN, RN), F32)],
        compiler_params=pltpu.CompilerParams(dimension_semantics=("parallel", "arbitrary"), vmem_limit_bytes=VMEM_BIG),
    )(r, lw, k, v, a, b)


def scan_bwd(name, r, lw, k, v, a, b, h0, dy, reverse):
    hb = _HB_B
    cmap = _chunk_map(reverse, True)

    def body(r_ref, lw_ref, k_ref, v_ref, a_ref, b_ref, h0_ref, dy_ref, *rest):
        d_refs, dht_ref = rest[:6], rest[6]

        @pl.when(pl.program_id(1) == 0)
        def _():
            dht_ref[...] = jnp.zeros_like(dht_ref)

        ins = [_split_heads(x[...]) for x in (r_ref, lw_ref, k_ref, v_ref, a_ref, b_ref)]
        _, vjp = jax.vjp(functools.partial(_chunk, reverse=reverse), *ins, h0_ref[0])
        grads = vjp((_split_heads(dy_ref[...]), dht_ref[...]))
        for d_ref, gval in zip(d_refs, grads[:6]):
            d_ref[...] = _merge_heads(gval)
        dht_ref[...] = grads[6]

    io = pl.BlockSpec((CHUNK, hb * RN), cmap)
    return pl.pallas_call(
        body, name=name, grid=(RH // hb, NCH),
        in_specs=[io] * 6 + [pl.BlockSpec((1, hb, RN, RN), lambda g, c: (cmap(g, c)[0], g, 0, 0)), io],
        out_specs=[io] * 6,
        out_shape=[jax.ShapeDtypeStruct((T, RW), F32)] * 6,
        scratch_shapes=[pltpu.VMEM((hb, RN, RN), F32)],
        compiler_params=pltpu.CompilerParams(dimension_semantics=("parallel", "arbitrary"), vmem_limit_bytes=VMEM_BIG),
    )(r, lw, k, v, a, b, h0, dy)


def loss_stage(out, x2, tgt, g_post):
    tr = 256

    def body(o_ref, x_ref, t_ref, g_ref, do_ref, dy_ref, dg_ref, loss_ref):
        @pl.when(pl.program_id(0) == 0)
        def _():
            dg_ref[...] = jnp.zeros_like(dg_ref)
            loss_ref[...] = jnp.zeros_like(loss_ref)

        nrm, vjp = jax.vjp(_rms, o_ref[...], g_ref[...])
        e = x_ref[...] + nrm - t_ref[...]
        s = jnp.sum(jnp.sum(e * e, axis=1, keepdims=True), axis=0, keepdims=True)
        loss_ref[...] += jnp.broadcast_to(s * (0.5 / D), loss_ref.shape)
        dy = e * (1.0 / D)
        do, dg = vjp(dy)
        do_ref[...] = do.astype(do_ref.dtype)
        dy_ref[...] = dy
        dg_ref[...] += dg

    row = pl.BlockSpec((tr, D), lambda i: (i, 0))
    return pl.pallas_call(
        body, name="loss_stage", grid=(T // tr,),
        in_specs=[row, row, row, pl.BlockSpec((1, D), lambda i: (0, 0))],
        out_specs=[row, row, pl.BlockSpec((1, D), lambda i: (0, 0)), pl.BlockSpec((8, LANE), lambda i: (0, 0))],
        out_shape=[jax.ShapeDtypeStruct((T, D), BF16), jax.ShapeDtypeStruct((T, D), F32),
                   jax.ShapeDtypeStruct((1, D), F32), jax.ShapeDtypeStruct((8, LANE), F32)],
        compiler_params=pltpu.CompilerParams(dimension_semantics=("arbitrary",), vmem_limit_bytes=VMEM_BIG),
    )(out, x2, tgt, g_post)


_EW_BLOCK_BYTES = 1 << 20


def _row_tile(rows, cols):
    best = None
    for tr in range(16, rows + 1, 16):
        if rows % tr == 0 and tr * cols * 4 <= _EW_BLOCK_BYTES:
            best = tr
    return best or rows


def _adamw_update(g, w_ref, m_ref, v_ref, g_ref, d_ref, nm_ref, nv_ref):
    mm = ADAM_B1 * m_ref[...] + (1.0 - ADAM_B1) * g
    vv = ADAM_B2 * v_ref[...] + (1.0 - ADAM_B2) * (g * g)
    m_hat = mm / (1.0 - ADAM_B1 ** ADAM_STEP)
    v_hat = vv / (1.0 - ADAM_B2 ** ADAM_STEP)
    g_ref[...] = g
    d_ref[...] = -ADAM_LR * (m_hat / (jnp.sqrt(v_hat) + ADAM_EPS) + ADAM_WD * w_ref[...])
    nm_ref[...] = mm
    nv_ref[...] = vv


def adamw(name, w, m, v, parts):
    rows, cols = w.shape
    br = _row_tile(rows, cols)
    npart = len(parts)

    def body(w_ref, m_ref, v_ref, *rest):
        g = rest[0][...].astype(F32)
        for p in rest[1:npart]:
            g = g + p[...].astype(F32)
        _adamw_update(g, w_ref, m_ref, v_ref, *rest[npart:])

    blk = pl.BlockSpec((br, cols), lambda i: (i, 0))
    return pl.pallas_call(
        body, name=name, grid=(rows // br,),
        in_specs=[blk] * (3 + npart), out_specs=[blk] * 4,
        out_shape=[jax.ShapeDtypeStruct((rows, cols), F32)] * 4,
        compiler_params=pltpu.CompilerParams(dimension_semantics=("parallel",), vmem_limit_bytes=VMEM_BIG),
    )(w, m, v, *parts)


def adamw_halves(name, place, w, m, v, mine, theirs):
    rows, cols = w.shape
    br = _row_tile(rows // 2, cols)
    nb = rows // 2 // br

    def body(p_ref, w_ref, m_ref, v_ref, a_ref, b_ref, *outs):
        own = (pl.program_id(0) // nb) == p_ref[0]
        _adamw_update(jnp.where(own, a_ref[...], b_ref[...]), w_ref, m_ref, v_ref, *outs)

    blk = pl.BlockSpec((br, cols), lambda i, p: (i, 0))
    half = pl.BlockSpec((br, cols), lambda i, p: (i % nb, 0))
    return pl.pallas_call(
        body, name=name,
        grid_spec=pltpu.PrefetchScalarGridSpec(num_scalar_prefetch=1, grid=(rows // br,),
                                               in_specs=[blk] * 3 + [half] * 2, out_specs=[blk] * 4),
        out_shape=[jax.ShapeDtypeStruct((rows, cols), F32)] * 4,
        compiler_params=pltpu.CompilerParams(dimension_semantics=("arbitrary",), vmem_limit_bytes=VMEM_BIG),
    )(place, w, m, v, mine, theirs)


def pair_sum(name, place, send, other):
    _, hr, cols = other.shape
    br = _row_tile(hr, 4 * cols)
    nb = hr // br

    def body(p_ref, a_ref, b_ref, o_ref):
        o_ref[...] = (a_ref[...].astype(F32) + b_ref[...].astype(F32)).astype(o_ref.dtype)

    blk = pl.BlockSpec((4, br, cols), lambda i, p: (0, i, 0))
    return pl.pallas_call(
        body, name=name,
        grid_spec=pltpu.PrefetchScalarGridSpec(
            num_scalar_prefetch=1, grid=(nb,),
            in_specs=[pl.BlockSpec((4, br, cols), lambda i, p: (0, p[0] * nb + i, 0)), blk], out_specs=blk),
        out_shape=jax.ShapeDtypeStruct(other.shape, BF16),
        compiler_params=pltpu.CompilerParams(dimension_semantics=("arbitrary",)),
    )(place, send, other)


def sum4(name, place, recv, own):
    _, rows, cols = recv.shape
    br = _row_tile(rows, 4 * cols)

    def body(p_ref, r_ref, s_ref, o_ref):
        me = p_ref[1]
        t = [jnp.where(me == j, s_ref[j], r_ref[j]).astype(F32) for j in range(4)]
        o_ref[...] = ((t[0] + t[1]) + t[2]) + t[3]

    blk = pl.BlockSpec((4, br, cols), lambda i, p: (0, i, 0))
    return pl.pallas_call(
        body, name=name,
        grid_spec=pltpu.PrefetchScalarGridSpec(num_scalar_prefetch=1, grid=(rows // br,), in_specs=[blk, blk],
                                               out_specs=pl.BlockSpec((br, cols), lambda i, p: (i, 0))),
        out_shape=jax.ShapeDtypeStruct((rows, cols), F32),
        compiler_params=pltpu.CompilerParams(dimension_semantics=("arbitrary",)),
    )(place, recv, own)


_ANY = pl.BlockSpec(memory_space=pl.ANY)


def _place():
    x, y, c = lax.axis_index("x"), lax.axis_index("y"), lax.axis_index("c")
    return x, y, c, 2 * x + y


def _chip_peers(x, y):
    out = []
    for k in (1, 2, 3):
        px = 1 - x if k & 2 else x
        py = 1 - y if k & 1 else y
        out.append((k, px, py, 2 * px + py))
    return out


def _half(c, rows):
    hr = rows // 2
    return pl.ds(pl.multiple_of(c * hr, 16), hr)


def gather_weights(srcs):
    n = len(srcs)

    def body(*refs):
        src, dst = refs[:n], refs[n:2 * n]
        ssem, rsem, fssem, frsem = refs[2 * n:]
        x, y, c, me = _place()
        sib = (x, y, 1 - c)
        waits, chain = [], []
        for i in range(n):
            rows = srcs[i].shape[0]
            mine, other = _half(c, rows), _half(1 - c, rows)
            for k, px, py, peer in _chip_peers(x, y):
                sems = dict(send_sem=ssem.at[i, k - 1], recv_sem=rsem.at[i, k - 1], device_id=(px, py, c),
                            device_id_type=MESH_IDS)
                fsems = dict(send_sem=fssem.at[i, k - 1], recv_sem=frsem.at[i, k - 1], device_id=sib,
                             device_id_type=MESH_IDS)
                snd = pltpu.make_async_remote_copy(src_ref=src[i].at[mine], dst_ref=dst[i].at[me, mine], **sems)
                rcv = pltpu.make_async_remote_copy(src_ref=src[i].at[mine], dst_ref=dst[i].at[peer, mine], **sems)
                fwd = pltpu.make_async_remote_copy(src_ref=dst[i].at[peer, mine], dst_ref=dst[i].at[peer, mine], **fsems)
                frcv = pltpu.make_async_remote_copy(src_ref=dst[i].at[peer, mine], dst_ref=dst[i].at[peer, other], **fsems)
                snd.start()
                chain.append((rcv, fwd))
                waits += [frcv.wait_recv, snd.wait_send, fwd.wait_send]
        for rcv, fwd in chain:
            rcv.wait_recv()
            fwd.start()
        for w in waits:
            w()

    return pl.pallas_call(
        body, name="gather_weights", in_specs=[_ANY] * n, out_specs=[_ANY] * n,
        out_shape=[jax.ShapeDtypeStruct((4,) + s.shape, s.dtype) for s in srcs],
        scratch_shapes=[pltpu.SemaphoreType.DMA((n, 3))] * 4,
    )(*srcs)


def pair_exchange(srcs):
    n = len(srcs)

    def body(*refs):
        src, other = refs[:n], refs[n:2 * n]
        ssem, rsem = refs[2 * n:]
        x, y, c, _ = _place()
        cps = []
        for i in range(n):
            rows = srcs[i].shape[1]
            cps.append(pltpu.make_async_remote_copy(
                src_ref=src[i].at[:, _half(1 - c, rows)], dst_ref=other[i], send_sem=ssem.at[i], recv_sem=rsem.at[i],
                device_id=(x, y, 1 - c), device_id_type=MESH_IDS))
            cps[-1].start()
        for cp in cps:
            cp.wait()

    return pl.pallas_call(
        body, name="pair_exchange", in_specs=[_ANY] * n, out_specs=[_ANY] * n,
        out_shape=[jax.ShapeDtypeStruct((4, s.shape[1] // 2, s.shape[2]), s.dtype) for s in srcs],
        scratch_shapes=[pltpu.SemaphoreType.DMA((n,))] * 2,
    )(*srcs)


def scatter_grads(srcs):
    n = len(srcs)

    def body(*refs):
        src, dst = refs[:n], refs[n:2 * n]
        ssem, rsem = refs[2 * n:]
        x, y, c, me = _place()
        waits = []
        for i in range(n):
            for k, px, py, peer in _chip_peers(x, y):
                sems = dict(send_sem=ssem.at[i, k - 1], recv_sem=rsem.at[i, k - 1], device_id=(px, py, c),
                            device_id_type=MESH_IDS)
                snd = pltpu.make_async_remote_copy(src_ref=src[i].at[peer], dst_ref=dst[i].at[me], **sems)
                rcv = pltpu.make_async_remote_copy(src_ref=src[i].at[peer], dst_ref=dst[i].at[peer], **sems)
                snd.start()
                waits += [rcv.wait_recv, snd.wait_send]
        for w in waits:
            w()

    return pl.pallas_call(
        body, name="scatter_grads", in_specs=[_ANY] * n, out_specs=[_ANY] * n,
        out_shape=[jax.ShapeDtypeStruct(s.shape, s.dtype) for s in srcs],
        scratch_shapes=[pltpu.SemaphoreType.DMA((n, 3))] * 2,
    )(*srcs)


def swap_halves(srcs):
    n = len(srcs)

    def body(*refs):
        src, dst = refs[:n], refs[n:2 * n]
        ssem, rsem = refs[2 * n:]
        x, y, c, _ = _place()
        cps = []
        for i in range(n):
            cps.append(pltpu.make_async_remote_copy(src_ref=src[i], dst_ref=dst[i], send_sem=ssem.at[i],
                                                    recv_sem=rsem.at[i], device_id=(x, y, 1 - c),
                                                    device_id_type=MESH_IDS))
            cps[-1].start()
        for cp in cps:
            cp.wait()

    return pl.pallas_call(
        body, name="swap_halves", in_specs=[_ANY] * n, out_specs=[_ANY] * n,
        out_shape=[jax.ShapeDtypeStruct(s.shape, s.dtype) for s in srcs],
        scratch_shapes=[pltpu.SemaphoreType.DMA((n,))] * 2,
    )(*srcs)


def allgather8(name, src):
    rows = src.shape[0]

    def body(src_ref, dst_ref, send_sems, recv_sems):
        x, y, c = lax.axis_index("x"), lax.axis_index("y"), lax.axis_index("c")
        me = 4 * x + 2 * y + c
        dst_ref[me] = src_ref[...]
        sends, recvs = [], []
        for k in range(1, 8):
            px = 1 - x if k & 4 else x
            py = 1 - y if k & 2 else y
            pc = 1 - c if k & 1 else c
            peer = 4 * px + 2 * py + pc
            for lst, slot in ((sends, me), (recvs, peer)):
                lst.append(pltpu.make_async_remote_copy(
                    src_ref=src_ref, dst_ref=dst_ref.at[slot], send_sem=send_sems.at[k - 1],
                    recv_sem=recv_sems.at[k - 1], device_id=(px, py, pc), device_id_type=MESH_IDS))
        for cp in sends:
            cp.start()
        for cp in recvs:
            cp.wait_recv()
        for cp in sends:
            cp.wait_send()

    vm = pl.BlockSpec(memory_space=pltpu.VMEM)
    return pl.pallas_call(
        body, name=name, in_specs=[vm], out_specs=vm,
        out_shape=jax.ShapeDtypeStruct((8, rows, LANE), src.dtype),
        scratch_shapes=[pltpu.SemaphoreType.DMA((7,)), pltpu.SemaphoreType.DMA((7,))],
    )(src)


WEIGHTS = ['g_pre', 'w_in', 'mla_q_norm', 'mla_wq_b', 'mla_kv_norm', 'mla_wkv_b', 'rwkv_mu', 'rwkv_w0_f', 'rwkv_w2_f',
           'rwkv_w0_b', 'rwkv_w2_b', 'rwkv_a0_f', 'rwkv_a2_f', 'rwkv_a0_b', 'rwkv_a2_b', 'rwkv_k_k', 'rwkv_k_a',
           'rwkv_r_k', 'rwkv_gn_g', 'rwkv_gn_b', 'w_br_mla', 'w_br_rwkv', 'w_out', 'g_post']
BIG_SHAPES = {'w_in': (D, D_IN // 4), 'mla_wq_b': (Q_RANK, 384), 'mla_wkv_b': (KV_RANK, 512),
              'rwkv_w2_f': (LORA, 256), 'rwkv_w2_b': (LORA, 256), 'rwkv_a2_f': (LORA, 256), 'rwkv_a2_b': (LORA, 256),
              'w_br_mla': (RW, 512), 'w_br_rwkv': (RW, 512), 'w_out': (512, D)}
BIG = list(BIG_SHAPES)
SMALL = [n for n in WEIGHTS if n not in BIG_SHAPES]
SMALL_SHAPES = {'g_pre': (D,), 'mla_q_norm': (Q_RANK,), 'mla_kv_norm': (KV_RANK,), 'rwkv_mu': (3456,),
                'rwkv_w0_f': (RW,), 'rwkv_w0_b': (RW,), 'rwkv_a0_f': (RW,), 'rwkv_a0_b': (RW,), 'rwkv_k_k': (RW,),
                'rwkv_k_a': (RW,), 'rwkv_r_k': (RH, RN), 'rwkv_gn_g': (RW,), 'rwkv_gn_b': (RW,), 'g_post': (D,)}
SMALL_LEN = sum(int(np.prod(s)) for s in SMALL_SHAPES.values())
SMALL_ROWS = 144


UNITS = [('w_in',), ('mla_wq_b',), ('mla_wkv_b',), ('rwkv_w2_f', 'rwkv_w2_b', 'rwkv_a2_f', 'rwkv_a2_b'),
         ('w_br_mla', 'w_br_rwkv'), ('w_out',)]


def _unit_cat(parts):
    return parts[0] if len(parts) == 1 else jnp.concatenate(parts, axis=0)


def _unit_split(arr, names, axis):
    out, o = {}, 0
    for n in names:
        rows = BIG_SHAPES[n][0]
        out[n] = lax.slice_in_dim(arr, o, o + rows, axis=axis)
        o += rows
    return out


def _gathered(ag, own, me):
    out = {}
    for names, arr, mine in zip(UNITS, ag, own):
        slots = [jnp.where(me == j, mine, arr[j]) for j in range(4)]
        for n in names:
            parts = [_unit_split(s, names, 0)[n] for s in slots]
            out[n] = jnp.concatenate(parts, axis=0 if n == 'w_out' else 1)
    return out


def _shards(n, g):
    if n == 'w_out':
        return [g[j * 512:(j + 1) * 512] for j in range(4)]
    w = BIG_SHAPES[n][1]
    return [g[:, j * w:(j + 1) * w] for j in range(4)]


def _pack_small(d, extra=None):
    flat = jnp.concatenate([d[n].reshape(-1) for n in SMALL] + ([extra.reshape(-1)] if extra is not None else []))
    return jnp.pad(flat, (0, SMALL_ROWS * LANE - flat.shape[0])).reshape(SMALL_ROWS, LANE)


def _unpack_small(packed):
    flat, out, o = packed.reshape(-1), {}, 0
    for n in SMALL:
        sz = int(np.prod(SMALL_SHAPES[n]))
        out[n] = flat[o:o + sz].reshape(SMALL_SHAPES[n])
        o += sz
    return out


def _perm_w_in(w):
    z = lambda n: jnp.zeros((w.shape[0], n), w.dtype)
    lora = []
    for i in range(4):
        lora += [w[:, 4160 + LORA * i:4160 + LORA * (i + 1)], z(LANE - LORA)]
    return jnp.concatenate([w[:, 0:1024], w[:, 1088:4160], w[:, 4544:D_IN]] + lora
                           + [w[:, 1024:1056], z(96), w[:, 1056:1088], z(96)], axis=1)


def _unperm_w_in(g):
    lora = [g[:, OFF_LORA + LANE * i:OFF_LORA + LANE * i + LORA] for i in range(4)]
    return jnp.concatenate([g[:, 0:1024], g[:, OFF_KR:OFF_KR + 32], g[:, OFF_KR + LANE:OFF_KR + LANE + 32],
                            g[:, 1024:4096]] + lora + [g[:, 4096:OFF_LORA]], axis=1)


def _perm_wq(w):
    w3 = w.reshape(Q_RANK, HEADS, NOPE + ROPE)
    return jnp.concatenate([w3[:, :, :NOPE].reshape(Q_RANK, -1), w3[:, :, NOPE:NOPE + 32].reshape(Q_RANK, -1),
                            w3[:, :, NOPE + 32:].reshape(Q_RANK, -1)], axis=1)


def _unperm_wq(g):
    return jnp.concatenate([g[:, :1024].reshape(Q_RANK, HEADS, NOPE), g[:, 1024:1280].reshape(Q_RANK, HEADS, 32),
                            g[:, 1280:].reshape(Q_RANK, HEADS, 32)], axis=2).reshape(Q_RANK, -1)


def _perm_wkv(w):
    w3 = w.reshape(KV_RANK, HEADS, NOPE + VDIM)
    return jnp.concatenate([w3[:, :, :NOPE].reshape(KV_RANK, -1), w3[:, :, NOPE:].reshape(KV_RANK, -1)], axis=1)


def _unperm_wkv(g):
    return jnp.concatenate([g[:, :1024].reshape(KV_RANK, HEADS, NOPE), g[:, 1024:].reshape(KV_RANK, HEADS, VDIM)],
                           axis=2).reshape(KV_RANK, -1)


def _pad_rows(w):
    return jnp.pad(w, ((0, LANE - LORA), (0, 0)))


def _perm_mu(mu):
    parts = [mu[:3072]]
    for i in range(4):
        parts += [mu[3072 + LORA * i:3072 + LORA * (i + 1)], jnp.zeros((LANE - LORA,), mu.dtype)]
    return jnp.concatenate(parts).reshape(1, NLERP)


def _unperm_mu(g):
    g = g.reshape(-1)
    return jnp.concatenate([g[:3072]] + [g[3072 + LANE * i:3072 + LANE * i + LORA] for i in range(4)])


def _to_heads(t, n):
    return t.reshape(T, -1, n).transpose(1, 0, 2)


def _from_heads(t):
    return t.transpose(1, 0, 2).reshape(T, -1)


def _to_qk(n, r1, r2):
    return jnp.concatenate([n.reshape(T, HEADS, NOPE), r1.reshape(T, HEADS, 32), r2.reshape(T, HEADS, 32)],
                           axis=2).transpose(1, 0, 2)


def _from_qk(g):
    g = g.transpose(1, 0, 2)
    return g[:, :, :NOPE].reshape(T, -1), g[:, :, NOPE:NOPE + 32].reshape(T, -1), g[:, :, NOPE + 32:].reshape(T, -1)


def _constants():
    g2 = np.kron(np.eye(2, dtype=np.float32), np.ones((RN, RN), np.float32))
    e = np.zeros((LANE, 256), np.float32)
    for h in range(HEADS):
        e[np.arange(32), h * 32 + np.arange(32)] = 1.0
    pos = jnp.arange(T, dtype=F32)
    inv_freq = jnp.power(ROPE_THETA, -jnp.arange(0, ROPE, 2, dtype=F32) / ROPE)
    ang = pos[:, None] * inv_freq[None, :]
    cos, sin = jnp.cos(ang), jnp.sin(ang)
    padk = lambda t: jnp.pad(t, ((0, 0), (0, LANE - 32)))
    return (jnp.asarray(g2, BF16), jnp.asarray(e, BF16), jnp.tile(cos, (1, HEADS)), jnp.tile(sin, (1, HEADS)),
            padk(cos), padk(sin))


def _step(x, tgt, w, m, v):
    x2, tgt2 = x.reshape(T, D), tgt.reshape(T, D)
    g2, e_mat, cosq, sinq, cosk, sink = _constants()
    row = lambda n: w[n].reshape(1, -1)

    core, chip = lax.axis_index("c"), 2 * lax.axis_index("x") + lax.axis_index("y")
    place = jnp.stack([core, chip]).astype(jnp.int32)
    own_bf = [_unit_cat([w[n].astype(BF16) for n in u]) for u in UNITS]
    full = _gathered(gather_weights(own_bf), own_bf, chip)
    wp = _perm_w_in(full['w_in'])
    wq = _perm_wq(full['mla_wq_b']).astype(F32)
    wkv = _perm_wkv(full['mla_wkv_b']).astype(F32)
    lora_w = [_pad_rows(full[n]).astype(F32) for n in ('rwkv_w2_f', 'rwkv_w2_b', 'rwkv_a2_f', 'rwkv_a2_b')]
    mu_p = _perm_mu(w['rwkv_mu'])

    st_pre = Stage("pre", f_pre, [(D, BF16), (D, None)], 256, [0], [0], [F32])
    st_mla = Stage("mla", f_mla, [(1024, BF16), (256, BF16), (256, BF16), (1024, BF16), (256, BF16), (256, BF16),
                                  (1024, BF16)], 256, [0, 1, 2], [0, 1, 2, 3], [BF16] * 3)
    st_rpre = Stage("rwkv_pre", f_rwkv_pre, [(RW, F32)] * 9, 128, [0, 1, 2, 3], list(range(10)), [F32] * 4)
    st_rpost = Stage("rwkv_post", f_rwkv_post, [(RW, BF16)], 256, [0, 2, 3, 4, 5, 6], [0, 1, 2],
                     [F32, F32, F32, F32, F32, BF16])
    st_gate = Stage("gate", f_gate, [(RW, BF16)], 256, [0, 1], [], [F32, BF16])
    st_merge = Stage("merge", f_merge, [(D, BF16)], 256, [0, 1, 2, 3], [], [BF16] * 4)

    pre_rows, pre_par = [(x2, D, 0)], [row('g_pre')]
    (h,) = st_pre.fwd(pre_rows, pre_par)
    proj = matmul("mm_in", h, wp, "nn")

    mla_rows = [(proj, 512, OFF_QA // 512), (proj, 512, OFF_KVA // 512), (proj, 256, OFF_KR // 256),
                (cosq, 256, 0), (sinq, 256, 0), (cosk, LANE, 0), (sink, LANE, 0)]
    mla_par = [row('mla_q_norm'), row('mla_kv_norm'), wq, wkv, e_mat]
    qn, qr1, qr2, kn, kr1, kr2, vv = st_mla.fwd(mla_rows, mla_par)
    qh, kh, vh = _to_qk(qn, qr1, qr2), _to_qk(kn, kr1, kr2), _to_heads(vv, VDIM)
    y_mla = _from_heads(attn_fwd(qh, kh, vh))

    lerp = shift_fwd(proj, mu_p)
    rpre_rows = [(lerp, RW, 0), (lerp, RW, 1), (lerp, RW, 2), (lerp, 512, 6)]
    rpre_par = [row('rwkv_w0_f'), row('rwkv_w0_b'), row('rwkv_a0_f'), row('rwkv_a0_b'), row('rwkv_k_k'),
                row('rwkv_k_a')] + lora_w + [g2]
    r_, v_, lwf, lwb, kf, kb, an, bf_, bb_ = st_rpre.fwd(rpre_rows, rpre_par)
    fin = [r_, lwf, kf, v_, an, bf_]
    bin_ = [r_, lwb, kb, v_, an, bb_]
    yf, h0f = scan_fwd("scan_f", *fin, reverse=False)
    yb, h0b = scan_fwd("scan_b", *bin_, reverse=True)
    rpost_rows = [(yf, RW, 0), (yb, RW, 0), (r_, RW, 0), (kf, RW, 0), (kb, RW, 0),
                  (v_, RW, 0), (proj, RW, OFF_ZR // RW)]
    rpost_par = [row('rwkv_gn_g'), row('rwkv_gn_b'), row('rwkv_r_k'), g2]
    (gr,) = st_rpost.fwd(rpost_rows, rpost_par)
    gate_rows = [(y_mla, RW, 0), (proj, RW, OFF_ZM // RW)]
    (gm,) = st_gate.fwd(gate_rows, [])
    um = matmul("mm_br_mla", gm, full['w_br_mla'], "nn")
    ur = matmul("mm_br_rwkv", gr, full['w_br_rwkv'], "nn")
    merge_rows = [(um, D, 0), (ur, D, 0), (proj, D, OFF_GM // D), (proj, D, OFF_GR // D)]
    (merged,) = st_merge.fwd(merge_rows, [])
    out = matmul("mm_out", merged, full['w_out'], "nn")
    d_out, dy, dg_post, loss_blk = loss_stage(out, x2, tgt2, row('g_post'))

    gw = {'g_post': dg_post}
    d_merged = matmul("mm_out_dx", d_out, full['w_out'], "nt")
    gw['w_out'] = matmul("mm_out_dw", merged, d_out, "tn")
    (d_um, d_ur, d_gm, d_gr), _ = st_merge.bwd(merge_rows, [], [[(d_merged, D, 0)]])
    d_gmla = matmul("mm_br_mla_dx", d_um, full['w_br_mla'], "nt")
    gw['w_br_mla'] = matmul("mm_br_mla_dw", gm, d_um, "tn")
    d_grw = matmul("mm_br_rwkv_dx", d_ur, full['w_br_rwkv'], "nt")
    gw['w_br_rwkv'] = matmul("mm_br_rwkv_dw", gr, d_ur, "tn")
    (d_ymla, d_zm), _ = st_gate.bwd(gate_rows, [], [[(d_gmla, RW, 0)]])
    (d_y, d_r3, d_kf2, d_kb2, d_v3, d_zr), (gw['rwkv_gn_g'], gw['rwkv_gn_b'], d_rk) = st_rpost.bwd(
        rpost_rows, rpost_par, [[(d_grw, RW, 0)]])
    gw['rwkv_r_k'] = d_rk
    sf = scan_bwd("scan_f_bwd", *fin, h0f, d_y, reverse=False)
    sb = scan_bwd("scan_b_bwd", *bin_, h0b, d_y, reverse=True)
    c = lambda *ts: [(t, RW, 0) for t in ts]
    rpre_cts = [c(sf[0], sb[0], d_r3), c(sf[3], sb[3], d_v3), c(sf[1]), c(sb[1]), c(sf[2], d_kf2), c(sb[2], d_kb2),
                c(sf[4], sb[4]), c(sf[5]), c(sb[5])]
    d_lerp_rows, rpre_g = st_rpre.bwd(rpre_rows, rpre_par, rpre_cts)
    for n, gval in zip(('rwkv_w0_f', 'rwkv_w0_b', 'rwkv_a0_f', 'rwkv_a0_b', 'rwkv_k_k', 'rwkv_k_a'), rpre_g[:6]):
        gw[n] = gval
    for n, gval in zip(('rwkv_w2_f', 'rwkv_w2_b', 'rwkv_a2_f', 'rwkv_a2_b'), rpre_g[6:]):
        gw[n] = gval[:LORA]
    d_lerp, d_mu = shift_bwd(proj, mu_p, jnp.concatenate(d_lerp_rows, axis=1))
    gw['rwkv_mu'] = _unperm_mu(d_mu)

    dqh, dkh, dvh = attn_bwd(qh, kh, vh, _to_heads(d_ymla, VDIM))
    mla_cts = [[(t, t.shape[1], 0)] for t in (*_from_qk(dqh), *_from_qk(dkh), _from_heads(dvh))]
    (d_qa, d_kva, d_kr), (gw['mla_q_norm'], gw['mla_kv_norm'], d_wq, d_wkv) = st_mla.bwd(mla_rows, mla_par, mla_cts)
    gw['mla_wq_b'], gw['mla_wkv_b'] = _unperm_wq(d_wq), _unperm_wkv(d_wkv)

    dproj = jnp.concatenate([d_qa, d_kva, d_lerp[:, :3072], d_zm, d_zr, d_gm, d_gr, d_lerp[:, 3072:], d_kr], axis=1)
    dh = matmul("mm_in_dx", dproj, wp, "nt")
    gw['w_in'] = _unperm_w_in(matmul("mm_in_dw", h, dproj, "tn"))
    (grad_x,), (gw['g_pre'],) = st_pre.bwd(pre_rows, pre_par, [[(dh, D, 0)], [(dy, D, 0)]])

    shards = {n: _shards(n, gw[n]) for n in BIG}
    send = [jnp.stack([_unit_cat([shards[n][j].astype(BF16) for n in u]) for j in range(4)]) for u in UNITS]
    other = pair_exchange(send)
    pairs = [pair_sum(f"pair_sum_{i}", place, s, o) for i, (s, o) in enumerate(zip(send, other))]
    recv = scatter_grads(pairs)
    mine = [sum4(f"sum4_{i}", place, r, p) for i, (r, p) in enumerate(zip(recv, pairs))]
    theirs = swap_halves(mine)
    big = [dict() for _ in range(4)]
    for i, u in enumerate(UNITS):
        res = adamw_halves(f"adamw_{i}", place, *[_unit_cat([t[n] for n in u]) for t in (w, m, v)], mine[i], theirs[i])
        for q in range(4):
            big[q].update(_unit_split(res[q], u, 0))
    parts = allgather8("gather_small", _pack_small(gw, loss_blk[0, :1]))
    small = adamw("adamw_small", _pack_small(w), _pack_small(m), _pack_small(v), [parts[i] for i in range(8)])

    outs = []
    for b_d, s_arr in zip(big, small):
        d = {**b_d, **_unpack_small(s_arr)}
        outs.append([d[n] for n in WEIGHTS])
    loss = small[0][SMALL_LEN // LANE, 0]
    return (loss, grad_x.reshape(1, T, D), *outs[0], *outs[1], *outs[2], *outs[3])


def kernel(x, g_pre, w_in, mla_q_norm, mla_wq_b, mla_kv_norm, mla_wkv_b, rwkv_mu, rwkv_w0_f, rwkv_w2_f, rwkv_w0_b, rwkv_w2_b, rwkv_a0_f, rwkv_a2_f, rwkv_a0_b, rwkv_a2_b, rwkv_k_k, rwkv_k_a, rwkv_r_k, rwkv_gn_g, rwkv_gn_b, w_br_mla, w_br_rwkv, w_out, g_post, loss_target, m_g_pre, m_w_in, m_mla_q_norm, m_mla_wq_b, m_mla_kv_norm, m_mla_wkv_b, m_rwkv_mu, m_rwkv_w0_f, m_rwkv_w2_f, m_rwkv_w0_b, m_rwkv_w2_b, m_rwkv_a0_f, m_rwkv_a2_f, m_rwkv_a0_b, m_rwkv_a2_b, m_rwkv_k_k, m_rwkv_k_a, m_rwkv_r_k, m_rwkv_gn_g, m_rwkv_gn_b, m_w_br_mla, m_w_br_rwkv, m_w_out, m_g_post, v_g_pre, v_w_in, v_mla_q_norm, v_mla_wq_b, v_mla_kv_norm, v_mla_wkv_b, v_rwkv_mu, v_rwkv_w0_f, v_rwkv_w2_f, v_rwkv_w0_b, v_rwkv_w2_b, v_rwkv_a0_f, v_rwkv_a2_f, v_rwkv_a0_b, v_rwkv_a2_b, v_rwkv_k_k, v_rwkv_k_a, v_rwkv_r_k, v_rwkv_gn_g, v_rwkv_gn_b, v_w_br_mla, v_w_br_rwkv, v_w_out, v_g_post):
    given = dict(locals())
    w = {n: given[n] for n in WEIGHTS}
    m = {n: given['m_' + n] for n in WEIGHTS}
    v = {n: given['v_' + n] for n in WEIGHTS}
    return _step(x, loss_target, w, m, v)
```

```python
import functools
import math

import numpy as np
import jax
import jax.numpy as jnp
from jax import lax
from jax.experimental import pallas as pl
from jax.experimental.pallas import tpu as pltpu

F32, BF16 = jnp.float32, jnp.bfloat16
MESH_IDS = pl.DeviceIdType.MESH

D = 2048
T = 2048
HEADS = 8
Q_RANK = 512
KV_RANK = 512
NOPE = 128
ROPE = 64
VDIM = 128
RW = 1024
RH = 16
RN = 64
LORA = 96
D_IN = 10688
NORM_EPS = 1e-6
GN_EPS = 64e-5
ROPE_THETA = 10000.0
ADAM_LR, ADAM_B1, ADAM_B2, ADAM_EPS, ADAM_WD, ADAM_STEP = 0.001, 0.9, 0.999, 1e-08, 0.01, 10

LANE = 128
VMEM_BIG = 56 * 2**20

NP = 11008
OFF_QA, OFF_KVA, OFF_RKV, OFF_ZM, OFF_ZR, OFF_GM, OFF_GR, OFF_LORA, OFF_KR = 0, 512, 1024, 4096, 5120, 6144, 8192, 10240, 10752
NLERP = 3584

CHUNK = 64
NCH = T // CHUNK


def _dg(a, b, ca, cb, batch=False, prec=None):
    bd = ((0,), (0,)) if batch else ((), ())
    return lax.dot_general(a, b, (((ca,), (cb,)), bd), precision=prec, preferred_element_type=F32)


@jax.custom_vjp
def bdot(a, b):
    return _dg(a.astype(BF16), b.astype(BF16), 1, 0)


def _bdot_fwd(a, b):
    return bdot(a, b), (a, b)


def _bdot_bwd(res, g):
    a, b = res
    gb = g.astype(BF16)
    da = _dg(gb, b.astype(BF16), 1, 1)
    db = _dg(a.astype(BF16), gb, 0, 0)
    return da.astype(a.dtype), db.astype(b.dtype)


bdot.defvjp(_bdot_fwd, _bdot_bwd)


def _split(x):
    hi = x.astype(BF16)
    lo = (x - hi.astype(F32)).astype(BF16)
    return hi, lo


@jax.custom_vjp
def gsum(x, g2):
    hi, lo = _split(x)
    return _dg(hi, g2, 1, 0) + _dg(lo, g2, 1, 0)


def _gsum_fwd(x, g2):
    return gsum(x, g2), g2


def _gsum_bwd(g2, g):
    hi, lo = _split(g)
    return _dg(hi, g2, 1, 1) + _dg(lo, g2, 1, 1), jnp.zeros_like(g2)


gsum.defvjp(_gsum_fwd, _gsum_bwd)


def headsum(x, g2):
    return jnp.concatenate([gsum(x[:, i * LANE:(i + 1) * LANE], g2) for i in range(x.shape[1] // LANE)], axis=1)


def _terms(x, n):
    out = []
    for i in range(n):
        t = x.astype(BF16)
        out.append(t)
        if i < n - 1:
            x = x - t.astype(F32)
    return out


def _bmm(a, b, ca, cb, na, nb):
    acc = None
    for i, ai in enumerate(_terms(a, na)):
        for j, bj in enumerate(_terms(b, nb)):
            if i + j < max(na, nb):
                p = _dg(ai, bj, ca, cb, True)
                acc = p if acc is None else acc + p
    return acc


_NN, _NT, _TN = (2, 1), (2, 2), (1, 1)


def _make_dots(nf, nb_nn, nb_nt, nb_tn):
    @jax.custom_vjp
    def nn(a, b):
        return _bmm(a, b, *_NN, nf, nf)

    @jax.custom_vjp
    def nt(a, b):
        return _bmm(a, b, *_NT, nf, nf)

    @jax.custom_vjp
    def tn(a, b):
        return _bmm(a, b, *_TN, nf, nf)

    nn.defvjp(lambda a, b: (nn(a, b), (a, b)),
              lambda r, g: (_bmm(g, r[1], *_NT, nb_nn, nb_nn), _bmm(r[0], g, *_TN, nb_nn, nb_nn)))
    nt.defvjp(lambda a, b: (nt(a, b), (a, b)),
              lambda r, g: (_bmm(g, r[1], *_NN, nb_nt, nb_nt), _bmm(g, r[0], *_TN, nb_nt, nb_nt)))
    tn.defvjp(lambda a, b: (tn(a, b), (a, b)),
              lambda r, g: (_bmm(r[1], g, *_NT, nb_tn, nb_tn), _bmm(r[0], g, *_NN, nb_tn, nb_tn)))
    return nn, nt, tn


_SCAN_NF, _SCAN_NB = 1, 1
nn, nt, tn = _make_dots(_SCAN_NF, 1, 2, 1)


@jax.custom_vjp
def cumdot(ones, x):
    return _bmm(ones, x, *_NN, 1, 3)


cumdot.defvjp(lambda o, x: (cumdot(o, x), o), lambda o, g: (jnp.zeros_like(o), _bmm(o, g, *_TN, 1, 3)))


def _solve_powers(l):
    pw = [l]
    for _ in range(int(math.log2(l.shape[-1])) - 1):
        pw.append(_bmm(pw[-1], pw[-1], *_NN, _SCAN_NF, _SCAN_NF))
    return pw


@jax.custom_vjp
def tri_solve(l, rhs):
    x = rhs
    for p in _solve_powers(l):
        x = x + _bmm(p, x, *_NN, _SCAN_NF, _SCAN_NF)
    return x


def _tri_solve_fwd(l, rhs):
    pw = _solve_powers(l)
    x = rhs
    for p in pw:
        x = x + _bmm(p, x, *_NN, _SCAN_NF, _SCAN_NF)
    return x, (pw, x)


def _tri_solve_bwd(res, g):
    pw, x = res
    y = g
    for p in pw:
        y = y + _bmm(p, y, *_TN, _SCAN_NB, _SCAN_NB)
    return _bmm(y, x, *_NT, _SCAN_NB, _SCAN_NB), y


tri_solve.defvjp(_tri_solve_fwd, _tri_solve_bwd)


def _rms(x, g):
    return x * lax.rsqrt(jnp.mean(x * x, axis=-1, keepdims=True) + NORM_EPS) * g


def _softplus(x):
    pos = x > 0
    return jnp.where(pos, x, 0.0) + jnp.log(1.0 + jnp.exp(-jnp.where(pos, x, -x)))


def _silu(z):
    return z * jax.nn.sigmoid(z)


def _tile(n, cands):
    for c in cands:
        if n % c == 0:
            return c
    raise ValueError(n)


_MM_VMEM_BYTES = 32 * 2**20


def _mm_tiles(m, n, k):
    best = None
    for tm in (2048, 1024, 512, 256, 128):
        for tn_ in (2048, 1024, 512, 256, 128):
            for d in range(k // LANE, 0, -1):
                tk = LANE * d
                if m % tm or n % tn_ or k % tk:
                    continue
                nk = k // tk
                vmem = 4 * tk * (tm + tn_) + 8 * tm * tn_ + (4 * tm * tn_ if nk > 1 else 0)
                if vmem > _MM_VMEM_BYTES:
                    continue
                a_reads = n // tn_ if nk > 1 else 1
                b_reads = 1 if (nk == 1 and n == tn_) else m // tm
                cost = (a_reads * m * k + b_reads * k * n, -tm * tn_ * tk)
                if best is None or cost < best[0]:
                    best = (cost, (tm, tn_, tk))
    return best[1]


def matmul(name, a, b, mode, out_dtype=F32):
    if mode == "nn":
        (m, k), n = a.shape, b.shape[1]
    elif mode == "nt":
        (m, k), n = a.shape, b.shape[0]
    else:
        (k, m), n = a.shape, b.shape[1]
    tm, tn_, tk = _mm_tiles(m, n, k)
    nk = k // tk
    if mode == "nn":
        a_spec = pl.BlockSpec((tm, tk), lambda i, j, kk: (i, kk))
        b_spec = pl.BlockSpec((tk, tn_), lambda i, j, kk: (kk, j))
        ca, cb = 1, 0
    elif mode == "nt":
        a_spec = pl.BlockSpec((tm, tk), lambda i, j, kk: (i, kk))
        b_spec = pl.BlockSpec((tn_, tk), lambda i, j, kk: (j, kk))
        ca, cb = 1, 1
    else:
        a_spec = pl.BlockSpec((tk, tm), lambda i, j, kk: (kk, i))
        b_spec = pl.BlockSpec((tk, tn_), lambda i, j, kk: (kk, j))
        ca, cb = 0, 0

    def body(a_ref, b_ref, o_ref, *acc):
        part = _dg(a_ref[...].astype(BF16), b_ref[...].astype(BF16), ca, cb)
        if nk == 1:
            o_ref[...] = part.astype(o_ref.dtype)
            return
        acc_ref, kk = acc[0], pl.program_id(2)

        @pl.when(kk == 0)
        def _():
            acc_ref[...] = part

        @pl.when(kk > 0)
        def _():
            acc_ref[...] += part

        @pl.when(kk == nk - 1)
        def _():
            o_ref[...] = acc_ref[...].astype(o_ref.dtype)

    return pl.pallas_call(
        body, name=name, grid=(m // tm, n // tn_, nk),
        in_specs=[a_spec, b_spec],
        out_specs=pl.BlockSpec((tm, tn_), lambda i, j, kk: (i, j)),
        out_shape=jax.ShapeDtypeStruct((m, n), out_dtype),
        scratch_shapes=[pltpu.VMEM((tm, tn_), F32)] if nk > 1 else [],
        compiler_params=pltpu.CompilerParams(dimension_semantics=("parallel", "parallel", "arbitrary"),
                                             vmem_limit_bytes=VMEM_BIG),
    )(a, b)


def _rspec(tr, width, blk):
    return pl.BlockSpec((tr, width), lambda i: (i, blk))


def _full_spec(arr):
    return pl.BlockSpec(arr.shape, lambda i: (0,) * arr.ndim)


class Stage:
    def __init__(self, name, f, outs, tr, diff_rows, diff_params, drow_dtypes):
        self.name, self.f, self.outs, self.tr = name, f, outs, tr
        self.diff_rows, self.diff_params, self.drow_dtypes = diff_rows, diff_params, drow_dtypes

    def fwd(self, rows, params):
        f, nr, npar = self.f, len(rows), len(params)
        stored = [(w, dt) for (w, dt) in self.outs if dt is not None]
        keep = [i for i, (w, dt) in enumerate(self.outs) if dt is not None]

        def body(*refs):
            vals = f(*[r[...].astype(F32) for r in refs[:nr]], *[p[...] for p in refs[nr:nr + npar]])
            for o_ref, i in zip(refs[nr + npar:], keep):
                o_ref[...] = vals[i].astype(o_ref.dtype)

        return pl.pallas_call(
            body, name=self.name + "_fwd", grid=(T // self.tr,),
            in_specs=[_rspec(self.tr, w, b) for (_, w, b) in rows] + [_full_spec(p) for p in params],
            out_specs=[_rspec(self.tr, w, 0) for (w, _) in stored],
            out_shape=[jax.ShapeDtypeStruct((T, w), dt) for (w, dt) in stored],
            compiler_params=pltpu.CompilerParams(dimension_semantics=("arbitrary",), vmem_limit_bytes=VMEM_BIG),
        )(*[r[0] for r in rows], *params)

    def bwd(self, rows, params, cts):
        f, nr, npar = self.f, len(rows), len(params)
        dr_idx, dp_idx = self.diff_rows, self.diff_params
        flat_cts = [c for lst in cts for c in lst]
        nct = len(flat_cts)

        def body(*refs):
            row_refs, par_refs = refs[:nr], refs[nr:nr + npar]
            ct_refs = refs[nr + npar:nr + npar + nct]
            drow_refs = refs[nr + npar + nct:nr + npar + nct + len(dr_idx)]
            dpar_refs = refs[nr + npar + nct + len(dr_idx):]
            row_vals = [r[...].astype(F32) for r in row_refs]
            par_vals = [p[...] for p in par_refs]

            def g(*dv):
                rv, pv = list(row_vals), list(par_vals)
                for j, i in enumerate(dr_idx):
                    rv[i] = dv[j]
                for j, i in enumerate(dp_idx):
                    pv[i] = dv[len(dr_idx) + j]
                return f(*rv, *pv)

            _, vjp = jax.vjp(g, *[row_vals[i] for i in dr_idx], *[par_vals[i] for i in dp_idx])
            ct_vals, pos = [], 0
            for lst in cts:
                acc = ct_refs[pos][...].astype(F32)
                for q in range(1, len(lst)):
                    acc = acc + ct_refs[pos + q][...].astype(F32)
                pos += len(lst)
                ct_vals.append(acc)
            grads = vjp(tuple(ct_vals))
            for j, r in enumerate(drow_refs):
                r[...] = grads[j].astype(r.dtype)

            @pl.when(pl.program_id(0) == 0)
            def _():
                for r in dpar_refs:
                    r[...] = jnp.zeros_like(r)

            for j, r in enumerate(dpar_refs):
                r[...] += grads[len(dr_idx) + j].astype(F32)

        drow_shapes = [jax.ShapeDtypeStruct((T, rows[i][1]), dt) for i, dt in zip(dr_idx, self.drow_dtypes)]
        dpar_shapes = [jax.ShapeDtypeStruct(params[i].shape, F32) for i in dp_idx]
        res = pl.pallas_call(
            body, name=self.name + "_bwd", grid=(T // self.tr,),
            in_specs=[_rspec(self.tr, w, b) for (_, w, b) in rows] + [_full_spec(p) for p in params]
            + [_rspec(self.tr, w, b) for (_, w, b) in flat_cts],
            out_specs=[_rspec(self.tr, rows[i][1], 0) for i in dr_idx] + [_full_spec(params[i]) for i in dp_idx],
            out_shape=drow_shapes + dpar_shapes,
            compiler_params=pltpu.CompilerParams(dimension_semantics=("arbitrary",), vmem_limit_bytes=VMEM_BIG),
        )(*[r[0] for r in rows], *params, *[c[0] for c in flat_cts])
        return res[:len(dr_idx)], res[len(dr_idx):]


def f_pre(x, g):
    return _rms(x, g), x


def f_mla(q_a, kv_a, kr, cosq, sinq, cosk, sink, gq, gkv, wq, wkv, e):
    q = bdot(_rms(q_a, gq), wq)
    kv = bdot(_rms(kv_a, gkv), wkv)
    t1, t2 = q[:, 1024:1280], q[:, 1280:1536]
    k1, k2 = kr[:, :LANE], kr[:, LANE:]
    kr1 = k1 * cosk - k2 * sink
    kr2 = k1 * sink + k2 * cosk
    return (q[:, :1024], t1 * cosq - t2 * sinq, t1 * sinq + t2 * cosq,
            kv[:, :1024], bdot(kr1, e), bdot(kr2, e), kv[:, 1024:])


def f_rwkv_pre(r, k, v, lora, w0f, w0b, a0f, a0b, kkw, kaw, w2f, w2b, a2f, a2b, g2):
    wdf, wdb, adf, adb = (lora[:, i * LANE:(i + 1) * LANE] for i in range(4))

    def logdecay(w0, wd, w2):
        z = w0 + bdot(jnp.tanh(wd), w2)
        return -jnp.exp(-_softplus(-z) - 0.5)

    a_f = jax.nn.sigmoid(a0f + bdot(adf, a2f))
    a_b = jax.nn.sigmoid(a0b + bdot(adb, a2b))
    kk = k * kkw
    kk = kk / jnp.maximum(jnp.sqrt(headsum(kk * kk, g2)), 1e-12)
    return (r, v, logdecay(w0f, wdf, w2f), logdecay(w0b, wdb, w2b),
            k * (1.0 + (a_f - 1.0) * kaw), k * (1.0 + (a_b - 1.0) * kaw), -kk, kk * a_f, kk * a_b)


def f_rwkv_post(yf, yb, r, kf, kb, v, z, gng, gnb, rk, g2):
    y = yf + yb
    mu = headsum(y, g2) * (1.0 / RN)
    d = y - mu
    var = headsum(d * d, g2) * (1.0 / RN)
    yn = d * lax.rsqrt(var + GN_EPS) * gng + gnb
    bonus = headsum(r * (kf + kb) * rk, g2) * v
    return ((yn + bonus) * _silu(z),)


def f_gate(y, z):
    return (y * _silu(z),)


def f_merge(um, ur, gm, gr):
    return (jax.nn.sigmoid(gm) * um + jax.nn.sigmoid(gr) * ur,)


_SHIFT_W = 256


def _lerp_colblock(j):
    return jnp.where(j < 3072 // _SHIFT_W, OFF_RKV // _SHIFT_W + j, OFF_LORA // _SHIFT_W + j - 3072 // _SHIFT_W)


def _nbr_mean(x):
    row = lax.broadcasted_iota(jnp.int32, x.shape, 0)
    up = jnp.where(row == 0, 0.0, pltpu.roll(x, 1, 0))
    dn = jnp.where(row == T - 1, 0.0, pltpu.roll(x, T - 1, 0))
    return 0.5 * (up + dn)


def shift_fwd(proj, mu):
    def body(x_ref, mu_ref, o_ref):
        x = x_ref[...]
        o_ref[...] = x + mu_ref[...] * (_nbr_mean(x) - x)

    return pl.pallas_call(
        body, name="shift_fwd", grid=(NLERP // _SHIFT_W,),
        in_specs=[pl.BlockSpec((T, _SHIFT_W), lambda j: (0, _lerp_colblock(j))),
                  pl.BlockSpec((1, _SHIFT_W), lambda j: (0, j))],
        out_specs=pl.BlockSpec((T, _SHIFT_W), lambda j: (0, j)),
        out_shape=jax.ShapeDtypeStruct((T, NLERP), F32),
        compiler_params=pltpu.CompilerParams(dimension_semantics=("parallel",), vmem_limit_bytes=VMEM_BIG),
    )(proj, mu)


def shift_bwd(proj, mu, g):
    def body(x_ref, mu_ref, g_ref, dx_ref, dmu_ref):
        x, gv = x_ref[...], g_ref[...]
        dmu_ref[...] = jnp.sum(gv * (_nbr_mean(x) - x), axis=0, keepdims=True)
        gm = gv * mu_ref[...]
        dx_ref[...] = (gv - gm + _nbr_mean(gm)).astype(dx_ref.dtype)

    col = pl.BlockSpec((T, _SHIFT_W), lambda j: (0, j))
    vec = pl.BlockSpec((1, _SHIFT_W), lambda j: (0, j))
    return pl.pallas_call(
        body, name="shift_bwd", grid=(NLERP // _SHIFT_W,),
        in_specs=[pl.BlockSpec((T, _SHIFT_W), lambda j: (0, _lerp_colblock(j))), vec, col],
        out_specs=[col, vec],
        out_shape=[jax.ShapeDtypeStruct((T, NLERP), BF16), jax.ShapeDtypeStruct((1, NLERP), F32)],
        compiler_params=pltpu.CompilerParams(dimension_semantics=("parallel",), vmem_limit_bytes=VMEM_BIG),
    )(proj, mu, g)


_TQ = 256
_ATT_SCALE = (NOPE + ROPE) ** -0.5


def _probs(q, k):
    s = _dg(q, k, 1, 1) * _ATT_SCALE
    e = jnp.exp(s - jnp.max(s, axis=-1, keepdims=True))
    return e * (1.0 / jnp.sum(e, axis=-1, keepdims=True))


def attn_fwd(q, k, v):
    def body(q_ref, k_ref, v_ref, o_ref):
        p = _probs(q_ref[0], k_ref[0])
        o_ref[0] = _dg(p.astype(BF16), v_ref[0], 1, 0)

    dq = NOPE + ROPE
    return pl.pallas_call(
        body, name="attn_fwd", grid=(HEADS, T // _TQ),
        in_specs=[pl.BlockSpec((1, _TQ, dq), lambda h, i: (h, i, 0)),
                  pl.BlockSpec((1, T, dq), lambda h, i: (h, 0, 0)),
                  pl.BlockSpec((1, T, VDIM), lambda h, i: (h, 0, 0))],
        out_specs=pl.BlockSpec((1, _TQ, VDIM), lambda h, i: (h, i, 0)),
        out_shape=jax.ShapeDtypeStruct((HEADS, T, VDIM), F32),
        compiler_params=pltpu.CompilerParams(dimension_semantics=("parallel", "arbitrary"), vmem_limit_bytes=VMEM_BIG),
    )(q, k, v)


def attn_bwd(q, k, v, do):
    def body(q_ref, k_ref, v_ref, do_ref, dq_ref, dk_ref, dv_ref):
        @pl.when(pl.program_id(1) == 0)
        def _():
            dk_ref[...] = jnp.zeros_like(dk_ref)
            dv_ref[...] = jnp.zeros_like(dv_ref)

        qv, kv_, vv = q_ref[0], k_ref[0], v_ref[0]
        dob = do_ref[0].astype(BF16)
        p = _probs(qv, kv_)
        dv_ref[0] += _dg(p.astype(BF16), dob, 0, 0)
        dp = _dg(dob, vv, 1, 1)
        ds = (p * (dp - jnp.sum(dp * p, axis=-1, keepdims=True)) * _ATT_SCALE).astype(BF16)
        dq_ref[0] = _dg(ds, kv_, 1, 0)
        dk_ref[0] += _dg(ds, qv, 0, 0)

    dq = NOPE + ROPE
    return pl.pallas_call(
        body, name="attn_bwd", grid=(HEADS, T // _TQ),
        in_specs=[pl.BlockSpec((1, _TQ, dq), lambda h, i: (h, i, 0)),
                  pl.BlockSpec((1, T, dq), lambda h, i: (h, 0, 0)),
                  pl.BlockSpec((1, T, VDIM), lambda h, i: (h, 0, 0)),
                  pl.BlockSpec((1, _TQ, VDIM), lambda h, i: (h, i, 0))],
        out_specs=[pl.BlockSpec((1, _TQ, dq), lambda h, i: (h, i, 0)),
                   pl.BlockSpec((1, T, dq), lambda h, i: (h, 0, 0)),
                   pl.BlockSpec((1, T, VDIM), lambda h, i: (h, 0, 0))],
        out_shape=[jax.ShapeDtypeStruct((HEADS, T, dq), F32), jax.ShapeDtypeStruct((HEADS, T, dq), F32),
                   jax.ShapeDtypeStruct((HEADS, T, VDIM), F32)],
        compiler_params=pltpu.CompilerParams(dimension_semantics=("parallel", "arbitrary"), vmem_limit_bytes=VMEM_BIG),
    )(q, k, v, do)


def _chunk(r, lw, k, v, a, b, ht, *, reverse):
    hb, c, _ = r.shape
    ti = lax.broadcasted_iota(jnp.int32, (c, c), 0)
    si = lax.broadcasted_iota(jnp.int32, (c, c), 1)
    incl = (si >= ti) if reverse else (si <= ti)
    strict = (si > ti) if reverse else (si < ti)
    ones = jnp.broadcast_to(incl.astype(F32)[None], (hb, c, c))
    cum = cumdot(ones, lw)
    cum_ex = cum - lw
    tot = jnp.sum(lw, axis=1, keepdims=True)
    mid = 0.5 * tot
    rt, at = r * jnp.exp(cum - mid), a * jnp.exp(cum_ex - mid)
    einv = jnp.exp(mid - cum)
    kt, bt = k * einv, b * einv
    m_ab = jnp.where(strict, nt(at, bt), 0.0)
    m_ak = jnp.where(strict, nt(at, kt), 0.0)
    m_rb = jnp.where(incl, nt(rt, bt), 0.0)
    m_rk = jnp.where(incl, nt(rt, kt), 0.0)
    u = tri_solve(m_ab, nt(a * jnp.exp(cum_ex), ht) + nn(m_ak, v))
    y = nt(r * jnp.exp(cum), ht) + nn(m_rb, u) + nn(m_rk, v)
    eend = jnp.exp(tot - cum)
    ht_new = ht * jnp.exp(tot) + tn(u, b * eend) + tn(v, k * eend)
    return y, ht_new


_HB_F, _HB_B = 16, 8


def _split_heads(x):
    return jnp.stack([x[:, i * RN:(i + 1) * RN] for i in range(x.shape[1] // RN)])


def _merge_heads(y):
    return jnp.concatenate([y[i] for i in range(y.shape[0])], axis=1)


def _chunk_map(reverse, backward):
    flip = reverse != backward
    return (lambda g, c: (NCH - 1 - c, g)) if flip else (lambda g, c: (c, g))


def scan_fwd(name, r, lw, k, v, a, b, reverse):
    hb = _HB_F
    cmap = _chunk_map(reverse, False)

    def body(r_ref, lw_ref, k_ref, v_ref, a_ref, b_ref, y_ref, h0_ref, ht_ref):
        @pl.when(pl.program_id(1) == 0)
        def _():
            ht_ref[...] = jnp.zeros_like(ht_ref)

        ht = ht_ref[...]
        h0_ref[0] = ht
        ins = [_split_heads(x[...]) for x in (r_ref, lw_ref, k_ref, v_ref, a_ref, b_ref)]
        y, hn = _chunk(*ins, ht, reverse=reverse)
        y_ref[...] = _merge_heads(y)
        ht_ref[...] = hn

    io = pl.BlockSpec((CHUNK, hb * RN), cmap)
    return pl.pallas_call(
        body, name=name, grid=(RH // hb, NCH),
        in_specs=[io] * 6,
        out_specs=[io, pl.BlockSpec((1, hb, RN, RN), lambda g, c: (cmap(g, c)[0], g, 0, 0))],
        out_shape=[jax.ShapeDtypeStruct((T, RW), F32), jax.ShapeDtypeStruct((NCH, RH, RN, RN), F32)],
        scratch_shapes=[pltpu.VMEM((hb, RN, RN), F32)],
        compiler_params=pltpu.CompilerParams(dimension_semantics=("parallel", "arbitrary"), vmem_limit_bytes=VMEM_BIG),
    )(r, lw, k, v, a, b)


def scan_bwd(name, r, lw, k, v, a, b, h0, dy, reverse):
    hb = _HB_B
    cmap = _chunk_map(reverse, True)

    def body(r_ref, lw_ref, k_ref, v_ref, a_ref, b_ref, h0_ref, dy_ref, *rest):
        d_refs, dht_ref = rest[:6], rest[6]

        @pl.when(pl.program_id(1) == 0)
        def _():
            dht_ref[...] = jnp.zeros_like(dht_ref)

        ins = [_split_heads(x[...]) for x in (r_ref, lw_ref, k_ref, v_ref, a_ref, b_ref)]
        _, vjp = jax.vjp(functools.partial(_chunk, reverse=reverse), *ins, h0_ref[0])
        grads = vjp((_split_heads(dy_ref[...]), dht_ref[...]))
        for d_ref, gval in zip(d_refs, grads[:6]):
            d_ref[...] = _merge_heads(gval)
        dht_ref[...] = grads[6]

    io = pl.BlockSpec((CHUNK, hb * RN), cmap)
    return pl.pallas_call(
        body, name=name, grid=(RH // hb, NCH),
        in_specs=[io] * 6 + [pl.BlockSpec((1, hb, RN, RN), lambda g, c: (cmap(g, c)[0], g, 0, 0)), io],
        out_specs=[io] * 6,
        out_shape=[jax.ShapeDtypeStruct((T, RW), F32)] * 6,
        scratch_shapes=[pltpu.VMEM((hb, RN, RN), F32)],
        compiler_params=pltpu.CompilerParams(dimension_semantics=("parallel", "arbitrary"), vmem_limit_bytes=VMEM_BIG),
    )(r, lw, k, v, a, b, h0, dy)


def loss_stage(out, x2, tgt, g_post):
    tr = 256

    def body(o_ref, x_ref, t_ref, g_ref, do_ref, dy_ref, dg_ref, loss_ref):
        @pl.when(pl.program_id(0) == 0)
        def _():
            dg_ref[...] = jnp.zeros_like(dg_ref)
            loss_ref[...] = jnp.zeros_like(loss_ref)

        nrm, vjp = jax.vjp(_rms, o_ref[...], g_ref[...])
        e = x_ref[...] + nrm - t_ref[...]
        s = jnp.sum(jnp.sum(e * e, axis=1, keepdims=True), axis=0, keepdims=True)
        loss_ref[...] += jnp.broadcast_to(s * (0.5 / D), loss_ref.shape)
        dy = e * (1.0 / D)
        do, dg = vjp(dy)
        do_ref[...] = do.astype(do_ref.dtype)
        dy_ref[...] = dy
        dg_ref[...] += dg

    row = pl.BlockSpec((tr, D), lambda i: (i, 0))
    return pl.pallas_call(
        body, name="loss_stage", grid=(T // tr,),
        in_specs=[row, row, row, pl.BlockSpec((1, D), lambda i: (0, 0))],
        out_specs=[row, row, pl.BlockSpec((1, D), lambda i: (0, 0)), pl.BlockSpec((8, LANE), lambda i: (0, 0))],
        out_shape=[jax.ShapeDtypeStruct((T, D), BF16), jax.ShapeDtypeStruct((T, D), F32),
                   jax.ShapeDtypeStruct((1, D), F32), jax.ShapeDtypeStruct((8, LANE), F32)],
        compiler_params=pltpu.CompilerParams(dimension_semantics=("arbitrary",), vmem_limit_bytes=VMEM_BIG),
    )(out, x2, tgt, g_post)


_EW_BLOCK_BYTES = 1 << 20


def _row_tile(rows, cols):
    best = None
    for tr in range(16, rows + 1, 16):
        if rows % tr == 0 and tr * cols * 4 <= _EW_BLOCK_BYTES:
            best = tr
    return best or rows


def _adamw_update(g, w_ref, m_ref, v_ref, g_ref, d_ref, nm_ref, nv_ref):
    mm = ADAM_B1 * m_ref[...] + (1.0 - ADAM_B1) * g
    vv = ADAM_B2 * v_ref[...] + (1.0 - ADAM_B2) * (g * g)
    m_hat = mm / (1.0 - ADAM_B1 ** ADAM_STEP)
    v_hat = vv / (1.0 - ADAM_B2 ** ADAM_STEP)
    g_ref[...] = g
    d_ref[...] = -ADAM_LR * (m_hat / (jnp.sqrt(v_hat) + ADAM_EPS) + ADAM_WD * w_ref[...])
    nm_ref[...] = mm
    nv_ref[...] = vv


def adamw(name, w, m, v, parts):
    rows, cols = w.shape
    br = _row_tile(rows, cols)
    npart = len(parts)

    def body(w_ref, m_ref, v_ref, *rest):
        g = rest[0][...].astype(F32)
        for p in rest[1:npart]:
            g = g + p[...].astype(F32)
        _adamw_update(g, w_ref, m_ref, v_ref, *rest[npart:])

    blk = pl.BlockSpec((br, cols), lambda i: (i, 0))
    return pl.pallas_call(
        body, name=name, grid=(rows // br,),
        in_specs=[blk] * (3 + npart), out_specs=[blk] * 4,
        out_shape=[jax.ShapeDtypeStruct((rows, cols), F32)] * 4,
        compiler_params=pltpu.CompilerParams(dimension_semantics=("parallel",), vmem_limit_bytes=VMEM_BIG),
    )(w, m, v, *parts)


def adamw_halves(name, place, w, m, v, mine, theirs, axis):
    rows, cols = w.shape
    if axis == 0:
        br = _row_tile(rows // 2, cols)
        nb = rows // 2 // br
        half = pl.BlockSpec((br, cols), lambda i, p: (i % nb, 0))
    else:
        br = _row_tile(rows, cols)
        half = pl.BlockSpec((br, cols // 2), lambda i, p: (i, 0))

    def body(p_ref, w_ref, m_ref, v_ref, a_ref, b_ref, *outs):
        a, b = a_ref[...], b_ref[...]
        if axis == 0:
            g = jnp.where((pl.program_id(0) // nb) == p_ref[0], a, b)
        else:
            first = p_ref[0] == 0
            g = jnp.concatenate([jnp.where(first, a, b), jnp.where(first, b, a)], axis=1)
        _adamw_update(g, w_ref, m_ref, v_ref, *outs)

    blk = pl.BlockSpec((br, cols), lambda i, p: (i, 0))
    return pl.pallas_call(
        body, name=name,
        grid_spec=pltpu.PrefetchScalarGridSpec(num_scalar_prefetch=1, grid=(rows // br,),
                                               in_specs=[blk] * 3 + [half] * 2, out_specs=[blk] * 4),
        out_shape=[jax.ShapeDtypeStruct((rows, cols), F32)] * 4,
        compiler_params=pltpu.CompilerParams(dimension_semantics=("arbitrary",), vmem_limit_bytes=VMEM_BIG),
    )(place, w, m, v, mine, theirs)


def pair_sum(name, place, send, other, axis):
    _, hr, hc = other.shape
    br = _row_tile(hr, 4 * hc)
    nb = hr // br
    a_map = (lambda i, p: (0, p[0] * nb + i, 0)) if axis == 0 else (lambda i, p: (0, i, p[0]))

    def body(p_ref, a_ref, b_ref, o_ref):
        o_ref[...] = (a_ref[...].astype(F32) + b_ref[...].astype(F32)).astype(o_ref.dtype)

    blk = pl.BlockSpec((4, br, hc), lambda i, p: (0, i, 0))
    return pl.pallas_call(
        body, name=name,
        grid_spec=pltpu.PrefetchScalarGridSpec(num_scalar_prefetch=1, grid=(nb,),
                                               in_specs=[pl.BlockSpec((4, br, hc), a_map), blk], out_specs=blk),
        out_shape=jax.ShapeDtypeStruct(other.shape, BF16),
        compiler_params=pltpu.CompilerParams(dimension_semantics=("arbitrary",)),
    )(place, send, other)


def sum4(name, place, recv, own):
    _, rows, cols = recv.shape
    br = _row_tile(rows, 4 * cols)

    def body(p_ref, r_ref, s_ref, o_ref):
        me = p_ref[1]
        t = [jnp.where(me == j, s_ref[j], r_ref[j]).astype(F32) for j in range(4)]
        o_ref[...] = ((t[0] + t[1]) + t[2]) + t[3]

    blk = pl.BlockSpec((4, br, cols), lambda i, p: (0, i, 0))
    return pl.pallas_call(
        body, name=name,
        grid_spec=pltpu.PrefetchScalarGridSpec(num_scalar_prefetch=1, grid=(rows // br,), in_specs=[blk, blk],
                                               out_specs=pl.BlockSpec((br, cols), lambda i, p: (i, 0))),
        out_shape=jax.ShapeDtypeStruct((rows, cols), F32),
        compiler_params=pltpu.CompilerParams(dimension_semantics=("arbitrary",)),
    )(place, recv, own)


_ANY = pl.BlockSpec(memory_space=pl.ANY)


def _place():
    x, y, c = lax.axis_index("x"), lax.axis_index("y"), lax.axis_index("c")
    return x, y, c, 2 * x + y


def _chip_peers(x, y):
    out = []
    for k in (1, 2, 3):
        px = 1 - x if k & 2 else x
        py = 1 - y if k & 1 else y
        out.append((k, px, py, 2 * px + py))
    return out


def _half(c, shape, axis):
    n = shape[axis] // 2
    sl = pl.ds(pl.multiple_of(c * n, 16 if axis == 0 else LANE), n)
    return (sl,) if axis == 0 else (pl.ds(0, shape[0]), sl)


def gather_weights(srcs, axes):
    n = len(srcs)

    def body(*refs):
        src, dst = refs[:n], refs[n:2 * n]
        ssem, rsem, fssem, frsem = refs[2 * n:]
        x, y, c, me = _place()
        sib = (x, y, 1 - c)
        waits, chain = [], []
        for i in range(n):
            mine, other = _half(c, srcs[i].shape, axes[i]), _half(1 - c, srcs[i].shape, axes[i])
            for k, px, py, peer in _chip_peers(x, y):
                sems = dict(send_sem=ssem.at[i, k - 1], recv_sem=rsem.at[i, k - 1], device_id=(px, py, c),
                            device_id_type=MESH_IDS)
                fsems = dict(send_sem=fssem.at[i, k - 1], recv_sem=frsem.at[i, k - 1], device_id=sib,
                             device_id_type=MESH_IDS)
                got = dst[i].at[(peer,) + mine]
                snd = pltpu.make_async_remote_copy(src_ref=src[i].at[mine], dst_ref=dst[i].at[(me,) + mine], **sems)
                rcv = pltpu.make_async_remote_copy(src_ref=src[i].at[mine], dst_ref=got, **sems)
                fwd = pltpu.make_async_remote_copy(src_ref=got, dst_ref=got, **fsems)
                frcv = pltpu.make_async_remote_copy(src_ref=got, dst_ref=dst[i].at[(peer,) + other], **fsems)
                snd.start()
                chain.append((rcv, fwd))
                waits += [frcv.wait_recv, snd.wait_send, fwd.wait_send]
        for rcv, fwd in chain:
            rcv.wait_recv()
            fwd.start()
        for w in waits:
            w()

    return pl.pallas_call(
        body, name="gather_weights", in_specs=[_ANY] * n, out_specs=[_ANY] * n,
        out_shape=[jax.ShapeDtypeStruct((4,) + s.shape, s.dtype) for s in srcs],
        scratch_shapes=[pltpu.SemaphoreType.DMA((n, 3))] * 4,
    )(*srcs)


def pair_exchange(srcs, axes):
    n = len(srcs)

    def half_shape(s, axis):
        return (4, s.shape[1] // 2, s.shape[2]) if axis == 0 else (4, s.shape[1], s.shape[2] // 2)

    def body(*refs):
        src, other = refs[:n], refs[n:2 * n]
        ssem, rsem = refs[2 * n:]
        x, y, c, _ = _place()
        cps = []
        for i in range(n):
            idx = (pl.ds(0, 4),) + _half(1 - c, srcs[i].shape[1:], axes[i])
            cps.append(pltpu.make_async_remote_copy(
                src_ref=src[i].at[idx], dst_ref=other[i], send_sem=ssem.at[i], recv_sem=rsem.at[i],
                device_id=(x, y, 1 - c), device_id_type=MESH_IDS))
            cps[-1].start()
        for cp in cps:
            cp.wait()

    return pl.pallas_call(
        body, name="pair_exchange", in_specs=[_ANY] * n, out_specs=[_ANY] * n,
        out_shape=[jax.ShapeDtypeStruct(half_shape(s, a), s.dtype) for s, a in zip(srcs, axes)],
        scratch_shapes=[pltpu.SemaphoreType.DMA((n,))] * 2,
    )(*srcs)


def scatter_grads(srcs):
    n = len(srcs)

    def body(*refs):
        src, dst = refs[:n], refs[n:2 * n]
        ssem, rsem = refs[2 * n:]
        x, y, c, me = _place()
        waits = []
        for i in range(n):
            for k, px, py, peer in _chip_peers(x, y):
                sems = dict(send_sem=ssem.at[i, k - 1], recv_sem=rsem.at[i, k - 1], device_id=(px, py, c),
                            device_id_type=MESH_IDS)
                snd = pltpu.make_async_remote_copy(src_ref=src[i].at[peer], dst_ref=dst[i].at[me], **sems)
                rcv = pltpu.make_async_remote_copy(src_ref=src[i].at[peer], dst_ref=dst[i].at[peer], **sems)
                snd.start()
                waits += [rcv.wait_recv, snd.wait_send]
        for w in waits:
            w()

    return pl.pallas_call(
        body, name="scatter_grads", in_specs=[_ANY] * n, out_specs=[_ANY] * n,
        out_shape=[jax.ShapeDtypeStruct(s.shape, s.dtype) for s in srcs],
        scratch_shapes=[pltpu.SemaphoreType.DMA((n, 3))] * 2,
    )(*srcs)


def swap_halves(srcs):
    n = len(srcs)

    def body(*refs):
        src, dst = refs[:n], refs[n:2 * n]
        ssem, rsem = refs[2 * n:]
        x, y, c, _ = _place()
        cps = []
        for i in range(n):
            cps.append(pltpu.make_async_remote_copy(src_ref=src[i], dst_ref=dst[i], send_sem=ssem.at[i],
                                                    recv_sem=rsem.at[i], device_id=(x, y, 1 - c),
                                                    device_id_type=MESH_IDS))
            cps[-1].start()
        for cp in cps:
            cp.wait()

    return pl.pallas_call(
        body, name="swap_halves", in_specs=[_ANY] * n, out_specs=[_ANY] * n,
        out_shape=[jax.ShapeDtypeStruct(s.shape, s.dtype) for s in srcs],
        scratch_shapes=[pltpu.SemaphoreType.DMA((n,))] * 2,
    )(*srcs)


def allgather8(name, src):
    rows = src.shape[0]

    def body(src_ref, dst_ref, send_sems, recv_sems):
        x, y, c = lax.axis_index("x"), lax.axis_index("y"), lax.axis_index("c")
        me = 4 * x + 2 * y + c
        dst_ref[me] = src_ref[...]
        sends, recvs = [], []
        for k in range(1, 8):
            px = 1 - x if k & 4 else x
            py = 1 - y if k & 2 else y
            pc = 1 - c if k & 1 else c
            peer = 4 * px + 2 * py + pc
            for lst, slot in ((sends, me), (recvs, peer)):
                lst.append(pltpu.make_async_remote_copy(
                    src_ref=src_ref, dst_ref=dst_ref.at[slot], send_sem=send_sems.at[k - 1],
                    recv_sem=recv_sems.at[k - 1], device_id=(px, py, pc), device_id_type=MESH_IDS))
        for cp in sends:
            cp.start()
        for cp in recvs:
            cp.wait_recv()
        for cp in sends:
            cp.wait_send()

    vm = pl.BlockSpec(memory_space=pltpu.VMEM)
    return pl.pallas_call(
        body, name=name, in_specs=[vm], out_specs=vm,
        out_shape=jax.ShapeDtypeStruct((8, rows, LANE), src.dtype),
        scratch_shapes=[pltpu.SemaphoreType.DMA((7,)), pltpu.SemaphoreType.DMA((7,))],
    )(src)


WEIGHTS = ['g_pre', 'w_in', 'mla_q_norm', 'mla_wq_b', 'mla_kv_norm', 'mla_wkv_b', 'rwkv_mu', 'rwkv_w0_f', 'rwkv_w2_f',
           'rwkv_w0_b', 'rwkv_w2_b', 'rwkv_a0_f', 'rwkv_a2_f', 'rwkv_a0_b', 'rwkv_a2_b', 'rwkv_k_k', 'rwkv_k_a',
           'rwkv_r_k', 'rwkv_gn_g', 'rwkv_gn_b', 'w_br_mla', 'w_br_rwkv', 'w_out', 'g_post']
BIG_SHAPES = {'w_in': (D_IN // 4, D), 'mla_wq_b': (Q_RANK, 384), 'mla_wkv_b': (KV_RANK, 512),
              'rwkv_w2_f': (LORA, 256), 'rwkv_w2_b': (LORA, 256), 'rwkv_a2_f': (LORA, 256), 'rwkv_a2_b': (LORA, 256),
              'w_br_mla': (RW, 512), 'w_br_rwkv': (RW, 512), 'w_out': (512, D)}
BIG = list(BIG_SHAPES)
SMALL = [n for n in WEIGHTS if n not in BIG_SHAPES]
SMALL_SHAPES = {'g_pre': (D,), 'mla_q_norm': (Q_RANK,), 'mla_kv_norm': (KV_RANK,), 'rwkv_mu': (3456,),
                'rwkv_w0_f': (RW,), 'rwkv_w0_b': (RW,), 'rwkv_a0_f': (RW,), 'rwkv_a0_b': (RW,), 'rwkv_k_k': (RW,),
                'rwkv_k_a': (RW,), 'rwkv_r_k': (RH, RN), 'rwkv_gn_g': (RW,), 'rwkv_gn_b': (RW,), 'g_post': (D,)}
SMALL_LEN = sum(int(np.prod(s)) for s in SMALL_SHAPES.values())
SMALL_ROWS = 144


UNITS = [('w_in',), ('mla_wq_b',), ('mla_wkv_b',), ('rwkv_w2_f', 'rwkv_w2_b', 'rwkv_a2_f', 'rwkv_a2_b'),
         ('w_br_mla', 'w_br_rwkv'), ('w_out',)]
UNIT_AXIS = [1, 0, 0, 0, 0, 0]
ROW_SHARDED = ('w_in', 'w_out')


def _unit_cat(parts):
    return parts[0] if len(parts) == 1 else jnp.concatenate(parts, axis=0)


def _unit_split(arr, names, axis):
    out, o = {}, 0
    for n in names:
        rows = BIG_SHAPES[n][0]
        out[n] = lax.slice_in_dim(arr, o, o + rows, axis=axis)
        o += rows
    return out


def _gathered(ag, own, me):
    out = {}
    for names, arr, mine in zip(UNITS, ag, own):
        slots = [jnp.where(me == j, mine, arr[j]) for j in range(4)]
        for n in names:
            parts = [_unit_split(s, names, 0)[n] for s in slots]
            out[n] = jnp.concatenate(parts, axis=0 if n in ROW_SHARDED else 1)
    return out


def _shards(n, g):
    r, w = BIG_SHAPES[n]
    if n in ROW_SHARDED:
        return [g[j * r:(j + 1) * r] for j in range(4)]
    return [g[:, j * w:(j + 1) * w] for j in range(4)]


def _pack_small(d, extra=None):
    flat = jnp.concatenate([d[n].reshape(-1) for n in SMALL] + ([extra.reshape(-1)] if extra is not None else []))
    return jnp.pad(flat, (0, SMALL_ROWS * LANE - flat.shape[0])).reshape(SMALL_ROWS, LANE)


def _unpack_small(packed):
    flat, out, o = packed.reshape(-1), {}, 0
    for n in SMALL:
        sz = int(np.prod(SMALL_SHAPES[n]))
        out[n] = flat[o:o + sz].reshape(SMALL_SHAPES[n])
        o += sz
    return out


def _perm_w_in(w):
    z = lambda n: jnp.zeros((n, w.shape[1]), w.dtype)
    lora = []
    for i in range(4):
        lora += [w[4160 + LORA * i:4160 + LORA * (i + 1)], z(LANE - LORA)]
    return jnp.concatenate([w[0:1024], w[1088:4160], w[4544:D_IN]] + lora
                           + [w[1024:1056], z(96), w[1056:1088], z(96)], axis=0)


def _unperm_w_in(g):
    lora = [g[OFF_LORA + LANE * i:OFF_LORA + LANE * i + LORA] for i in range(4)]
    return jnp.concatenate([g[0:1024], g[OFF_KR:OFF_KR + 32], g[OFF_KR + LANE:OFF_KR + LANE + 32],
                            g[1024:4096]] + lora + [g[4096:OFF_LORA]], axis=0)


def _perm_wq(w):
    w3 = w.reshape(Q_RANK, HEADS, NOPE + ROPE)
    return jnp.concatenate([w3[:, :, :NOPE].reshape(Q_RANK, -1), w3[:, :, NOPE:NOPE + 32].reshape(Q_RANK, -1),
                            w3[:, :, NOPE + 32:].reshape(Q_RANK, -1)], axis=1)


def _unperm_wq(g):
    return jnp.concatenate([g[:, :1024].reshape(Q_RANK, HEADS, NOPE), g[:, 1024:1280].reshape(Q_RANK, HEADS, 32),
                            g[:, 1280:].reshape(Q_RANK, HEADS, 32)], axis=2).reshape(Q_RANK, -1)


def _perm_wkv(w):
    w3 = w.reshape(KV_RANK, HEADS, NOPE + VDIM)
    return jnp.concatenate([w3[:, :, :NOPE].reshape(KV_RANK, -1), w3[:, :, NOPE:].reshape(KV_RANK, -1)], axis=1)


def _unperm_wkv(g):
    return jnp.concatenate([g[:, :1024].reshape(KV_RANK, HEADS, NOPE), g[:, 1024:].reshape(KV_RANK, HEADS, VDIM)],
                           axis=2).reshape(KV_RANK, -1)


def _pad_rows(w):
    return jnp.pad(w, ((0, LANE - LORA), (0, 0)))


def _perm_mu(mu):
    parts = [mu[:3072]]
    for i in range(4):
        parts += [mu[3072 + LORA * i:3072 + LORA * (i + 1)], jnp.zeros((LANE - LORA,), mu.dtype)]
    return jnp.concatenate(parts).reshape(1, NLERP)


def _unperm_mu(g):
    g = g.reshape(-1)
    return jnp.concatenate([g[:3072]] + [g[3072 + LANE * i:3072 + LANE * i + LORA] for i in range(4)])


def _to_heads(t, n):
    return t.reshape(T, -1, n).transpose(1, 0, 2)


def _from_heads(t):
    return t.transpose(1, 0, 2).reshape(T, -1)


def _to_qk(n, r1, r2):
    return jnp.concatenate([n.reshape(T, HEADS, NOPE), r1.reshape(T, HEADS, 32), r2.reshape(T, HEADS, 32)],
                           axis=2).transpose(1, 0, 2)


def _from_qk(g):
    g = g.transpose(1, 0, 2)
    return g[:, :, :NOPE].reshape(T, -1), g[:, :, NOPE:NOPE + 32].reshape(T, -1), g[:, :, NOPE + 32:].reshape(T, -1)


def _constants():
    g2 = np.kron(np.eye(2, dtype=np.float32), np.ones((RN, RN), np.float32))
    e = np.zeros((LANE, 256), np.float32)
    for h in range(HEADS):
        e[np.arange(32), h * 32 + np.arange(32)] = 1.0
    pos = jnp.arange(T, dtype=F32)
    inv_freq = jnp.power(ROPE_THETA, -jnp.arange(0, ROPE, 2, dtype=F32) / ROPE)
    ang = pos[:, None] * inv_freq[None, :]
    cos, sin = jnp.cos(ang), jnp.sin(ang)
    padk = lambda t: jnp.pad(t, ((0, 0), (0, LANE - 32)))
    return (jnp.asarray(g2, BF16), jnp.asarray(e, BF16), jnp.tile(cos, (1, HEADS)), jnp.tile(sin, (1, HEADS)),
            padk(cos), padk(sin))


def _step(x, tgt, w, m, v):
    x2, tgt2 = x.reshape(T, D), tgt.reshape(T, D)
    g2, e_mat, cosq, sinq, cosk, sink = _constants()
    row = lambda n: w[n].reshape(1, -1)
    w, m, v = ({**t, 'w_in': t['w_in'].T} for t in (w, m, v))

    core, chip = lax.axis_index("c"), 2 * lax.axis_index("x") + lax.axis_index("y")
    place = jnp.stack([core, chip]).astype(jnp.int32)
    own_bf = [_unit_cat([w[n].astype(BF16) for n in u]) for u in UNITS]
    full = _gathered(gather_weights(own_bf, UNIT_AXIS), own_bf, chip)
    wp = _perm_w_in(full['w_in'])
    wq = _perm_wq(full['mla_wq_b']).astype(F32)
    wkv = _perm_wkv(full['mla_wkv_b']).astype(F32)
    lora_w = [_pad_rows(full[n]).astype(F32) for n in ('rwkv_w2_f', 'rwkv_w2_b', 'rwkv_a2_f', 'rwkv_a2_b')]
    mu_p = _perm_mu(w['rwkv_mu'])

    st_pre = Stage("pre", f_pre, [(D, BF16), (D, None)], 256, [0], [0], [F32])
    st_mla = Stage("mla", f_mla, [(1024, BF16), (256, BF16), (256, BF16), (1024, BF16), (256, BF16), (256, BF16),
                                  (1024, BF16)], 256, [0, 1, 2], [0, 1, 2, 3], [BF16] * 3)
    st_rpre = Stage("rwkv_pre", f_rwkv_pre, [(RW, F32)] * 9, 128, [0, 1, 2, 3], list(range(10)), [F32] * 4)
    st_rpost = Stage("rwkv_post", f_rwkv_post, [(RW, BF16)], 256, [0, 2, 3, 4, 5, 6], [0, 1, 2],
                     [F32, F32, F32, F32, F32, BF16])
    st_gate = Stage("gate", f_gate, [(RW, BF16)], 256, [0, 1], [], [F32, BF16])
    st_merge = Stage("merge", f_merge, [(D, BF16)], 256, [0, 1, 2, 3], [], [BF16] * 4)

    pre_rows, pre_par = [(x2, D, 0)], [row('g_pre')]
    (h,) = st_pre.fwd(pre_rows, pre_par)
    proj = matmul("mm_in", h, wp, "nt")

    mla_rows = [(proj, 512, OFF_QA // 512), (proj, 512, OFF_KVA // 512), (proj, 256, OFF_KR // 256),
                (cosq, 256, 0), (sinq, 256, 0), (cosk, LANE, 0), (sink, LANE, 0)]
    mla_par = [row('mla_q_norm'), row('mla_kv_norm'), wq, wkv, e_mat]
    qn, qr1, qr2, kn, kr1, kr2, vv = st_mla.fwd(mla_rows, mla_par)
    qh, kh, vh = _to_qk(qn, qr1, qr2), _to_qk(kn, kr1, kr2), _to_heads(vv, VDIM)
    y_mla = _from_heads(attn_fwd(qh, kh, vh))

    lerp = shift_fwd(proj, mu_p)
    rpre_rows = [(lerp, RW, 0), (lerp, RW, 1), (lerp, RW, 2), (lerp, 512, 6)]
    rpre_par = [row('rwkv_w0_f'), row('rwkv_w0_b'), row('rwkv_a0_f'), row('rwkv_a0_b'), row('rwkv_k_k'),
                row('rwkv_k_a')] + lora_w + [g2]
    r_, v_, lwf, lwb, kf, kb, an, bf_, bb_ = st_rpre.fwd(rpre_rows, rpre_par)
    fin = [r_, lwf, kf, v_, an, bf_]
    bin_ = [r_, lwb, kb, v_, an, bb_]
    yf, h0f = scan_fwd("scan_f", *fin, reverse=False)
    yb, h0b = scan_fwd("scan_b", *bin_, reverse=True)
    rpost_rows = [(yf, RW, 0), (yb, RW, 0), (r_, RW, 0), (kf, RW, 0), (kb, RW, 0),
                  (v_, RW, 0), (proj, RW, OFF_ZR // RW)]
    rpost_par = [row('rwkv_gn_g'), row('rwkv_gn_b'), row('rwkv_r_k'), g2]
    (gr,) = st_rpost.fwd(rpost_rows, rpost_par)
    gate_rows = [(y_mla, RW, 0), (proj, RW, OFF_ZM // RW)]
    (gm,) = st_gate.fwd(gate_rows, [])
    um = matmul("mm_br_mla", gm, full['w_br_mla'], "nn")
    ur = matmul("mm_br_rwkv", gr, full['w_br_rwkv'], "nn")
    merge_rows = [(um, D, 0), (ur, D, 0), (proj, D, OFF_GM // D), (proj, D, OFF_GR // D)]
    (merged,) = st_merge.fwd(merge_rows, [])
    out = matmul("mm_out", merged, full['w_out'], "nn")
    d_out, dy, dg_post, loss_blk = loss_stage(out, x2, tgt2, row('g_post'))

    gw = {'g_post': dg_post}
    d_merged = matmul("mm_out_dx", d_out, full['w_out'], "nt")
    gw['w_out'] = matmul("mm_out_dw", merged, d_out, "tn")
    (d_um, d_ur, d_gm, d_gr), _ = st_merge.bwd(merge_rows, [], [[(d_merged, D, 0)]])
    d_gmla = matmul("mm_br_mla_dx", d_um, full['w_br_mla'], "nt")
    gw['w_br_mla'] = matmul("mm_br_mla_dw", gm, d_um, "tn")
    d_grw = matmul("mm_br_rwkv_dx", d_ur, full['w_br_rwkv'], "nt")
    gw['w_br_rwkv'] = matmul("mm_br_rwkv_dw", gr, d_ur, "tn")
    (d_ymla, d_zm), _ = st_gate.bwd(gate_rows, [], [[(d_gmla, RW, 0)]])
    (d_y, d_r3, d_kf2, d_kb2, d_v3, d_zr), (gw['rwkv_gn_g'], gw['rwkv_gn_b'], d_rk) = st_rpost.bwd(
        rpost_rows, rpost_par, [[(d_grw, RW, 0)]])
    gw['rwkv_r_k'] = d_rk
    sf = scan_bwd("scan_f_bwd", *fin, h0f, d_y, reverse=False)
    sb = scan_bwd("scan_b_bwd", *bin_, h0b, d_y, reverse=True)
    c = lambda *ts: [(t, RW, 0) for t in ts]
    rpre_cts = [c(sf[0], sb[0], d_r3), c(sf[3], sb[3], d_v3), c(sf[1]), c(sb[1]), c(sf[2], d_kf2), c(sb[2], d_kb2),
                c(sf[4], sb[4]), c(sf[5]), c(sb[5])]
    d_lerp_rows, rpre_g = st_rpre.bwd(rpre_rows, rpre_par, rpre_cts)
    for n, gval in zip(('rwkv_w0_f', 'rwkv_w0_b', 'rwkv_a0_f', 'rwkv_a0_b', 'rwkv_k_k', 'rwkv_k_a'), rpre_g[:6]):
        gw[n] = gval
    for n, gval in zip(('rwkv_w2_f', 'rwkv_w2_b', 'rwkv_a2_f', 'rwkv_a2_b'), rpre_g[6:]):
        gw[n] = gval[:LORA]
    d_lerp, d_mu = shift_bwd(proj, mu_p, jnp.concatenate(d_lerp_rows, axis=1))
    gw['rwkv_mu'] = _unperm_mu(d_mu)

    dqh, dkh, dvh = attn_bwd(qh, kh, vh, _to_heads(d_ymla, VDIM))
    mla_cts = [[(t, t.shape[1], 0)] for t in (*_from_qk(dqh), *_from_qk(dkh), _from_heads(dvh))]
    (d_qa, d_kva, d_kr), (gw['mla_q_norm'], gw['mla_kv_norm'], d_wq, d_wkv) = st_mla.bwd(mla_rows, mla_par, mla_cts)
    gw['mla_wq_b'], gw['mla_wkv_b'] = _unperm_wq(d_wq), _unperm_wkv(d_wkv)

    dproj = jnp.concatenate([d_qa, d_kva, d_lerp[:, :3072], d_zm, d_zr, d_gm, d_gr, d_lerp[:, 3072:], d_kr], axis=1)
    dh = matmul("mm_in_dx", dproj, wp, "nn")
    gw['w_in'] = _unperm_w_in(matmul("mm_in_dw", dproj, h, "tn"))
    (grad_x,), (gw['g_pre'],) = st_pre.bwd(pre_rows, pre_par, [[(dh, D, 0)], [(dy, D, 0)]])

    shards = {n: _shards(n, gw[n]) for n in BIG}
    send = [jnp.stack([_unit_cat([shards[n][j].astype(BF16) for n in u]) for j in range(4)]) for u in UNITS]
    other = pair_exchange(send, UNIT_AXIS)
    pairs = [pair_sum(f"pair_sum_{i}", place, s, o, ax) for i, (s, o, ax) in enumerate(zip(send, other, UNIT_AXIS))]
    recv = scatter_grads(pairs)
    mine = [sum4(f"sum4_{i}", place, r, p) for i, (r, p) in enumerate(zip(recv, pairs))]
    theirs = swap_halves(mine)
    big = [dict() for _ in range(4)]
    for i, u in enumerate(UNITS):
        res = adamw_halves(f"adamw_{i}", place, *[_unit_cat([t[n] for n in u]) for t in (w, m, v)], mine[i], theirs[i],
                           UNIT_AXIS[i])
        for q in range(4):
            big[q].update(_unit_split(res[q], u, 0))
    parts = allgather8("gather_small", _pack_small(gw, loss_blk[0, :1]))
    small = adamw("adamw_small", _pack_small(w), _pack_small(m), _pack_small(v), [parts[i] for i in range(8)])

    outs = []
    for b_d, s_arr in zip(big, small):
        d = {**b_d, **_unpack_small(s_arr)}
        d['w_in'] = d['w_in'].T
        outs.append([d[n] for n in WEIGHTS])
    loss = small[0][SMALL_LEN // LANE, 0]
    return (loss, grad_x.reshape(1, T, D), *outs[0], *outs[1], *outs[2], *outs[3])


def kernel(x, g_pre, w_in, mla_q_norm, mla_wq_b, mla_kv_norm, mla_wkv_b, rwkv_mu, rwkv_w0_f, rwkv_w2_f, rwkv_w0_b, rwkv_w2_b, rwkv_a0_f, rwkv_a2_f, rwkv_a0_b, rwkv_a2_b, rwkv_k_k, rwkv_k_a, rwkv_r_k, rwkv_gn_g, rwkv_gn_b, w_br_mla, w_br_rwkv, w_out, g_post, loss_target, m_g_pre, m_w_in, m_mla_q_norm, m_mla_wq_b, m_mla_kv_norm, m_mla_wkv_b, m_rwkv_mu, m_rwkv_w0_f, m_rwkv_w2_f, m_rwkv_w0_b, m_rwkv_w2_b, m_rwkv_a0_f, m_rwkv_a2_f, m_rwkv_a0_b, m_rwkv_a2_b, m_rwkv_k_k, m_rwkv_k_a, m_rwkv_r_k, m_rwkv_gn_g, m_rwkv_gn_b, m_w_br_mla, m_w_br_rwkv, m_w_out, m_g_post, v_g_pre, v_w_in, v_mla_q_norm, v_mla_wq_b, v_mla_kv_norm, v_mla_wkv_b, v_rwkv_mu, v_rwkv_w0_f, v_rwkv_w2_f, v_rwkv_w0_b, v_rwkv_w2_b, v_rwkv_a0_f, v_rwkv_a2_f, v_rwkv_a0_b, v_rwkv_a2_b, v_rwkv_k_k, v_rwkv_k_a, v_rwkv_r_k, v_rwkv_gn_g, v_rwkv_gn_b, v_w_br_mla, v_w_br_rwkv, v_w_out, v_g_post):
    given = dict(locals())
    w = {n: given[n] for n in WEIGHTS}
    m = {n: given['m_' + n] for n in WEIGHTS}
    v = {n: given['v_' + n] for n in WEIGHTS}
    return _step(x, loss_target, w, m, v)
```

```python
import functools
import math

import numpy as np
import jax
import jax.numpy as jnp
from jax import lax
from jax.experimental import pallas as pl
from jax.experimental.pallas import tpu as pltpu

F32, BF16 = jnp.float32, jnp.bfloat16
MESH_IDS = pl.DeviceIdType.MESH

D = 2048
T = 2048
HEADS = 8
Q_RANK = 512
KV_RANK = 512
NOPE = 128
ROPE = 64
VDIM = 128
RW = 1024
RH = 16
RN = 64
LORA = 96
D_IN = 10688
NORM_EPS = 1e-6
GN_EPS = 64e-5
ROPE_THETA = 10000.0
ADAM_LR, ADAM_B1, ADAM_B2, ADAM_EPS, ADAM_WD, ADAM_STEP = 0.001, 0.9, 0.999, 1e-08, 0.01, 10

LANE = 128
VMEM_BIG = 56 * 2**20

NP = 11008
OFF_QA, OFF_KVA, OFF_RKV, OFF_ZM, OFF_ZR, OFF_GM, OFF_GR, OFF_LORA, OFF_KR = 0, 512, 1024, 4096, 5120, 6144, 8192, 10240, 10752
NLERP = 3584

CHUNK = 64
NCH = T // CHUNK


def _dg(a, b, ca, cb, batch=False, prec=None):
    bd = ((0,), (0,)) if batch else ((), ())
    return lax.dot_general(a, b, (((ca,), (cb,)), bd), precision=prec, preferred_element_type=F32)


@jax.custom_vjp
def bdot(a, b):
    return _dg(a.astype(BF16), b.astype(BF16), 1, 0)


def _bdot_fwd(a, b):
    return bdot(a, b), (a, b)


def _bdot_bwd(res, g):
    a, b = res
    gb = g.astype(BF16)
    da = _dg(gb, b.astype(BF16), 1, 1)
    db = _dg(a.astype(BF16), gb, 0, 0)
    return da.astype(a.dtype), db.astype(b.dtype)


bdot.defvjp(_bdot_fwd, _bdot_bwd)


def _split(x):
    hi = x.astype(BF16)
    lo = (x - hi.astype(F32)).astype(BF16)
    return hi, lo


@jax.custom_vjp
def gsum(x, g2):
    hi, lo = _split(x)
    return _dg(hi, g2, 1, 0) + _dg(lo, g2, 1, 0)


def _gsum_fwd(x, g2):
    return gsum(x, g2), g2


def _gsum_bwd(g2, g):
    hi, lo = _split(g)
    return _dg(hi, g2, 1, 1) + _dg(lo, g2, 1, 1), jnp.zeros_like(g2)


gsum.defvjp(_gsum_fwd, _gsum_bwd)


def headsum(x, g2):
    return jnp.concatenate([gsum(x[:, i * LANE:(i + 1) * LANE], g2) for i in range(x.shape[1] // LANE)], axis=1)


def _terms(x, n):
    out = []
    for i in range(n):
        t = x.astype(BF16)
        out.append(t)
        if i < n - 1:
            x = x - t.astype(F32)
    return out


def _bmm(a, b, ca, cb, na, nb):
    acc = None
    for i, ai in enumerate(_terms(a, na)):
        for j, bj in enumerate(_terms(b, nb)):
            if i + j < max(na, nb):
                p = _dg(ai, bj, ca, cb, True)
                acc = p if acc is None else acc + p
    return acc


_NN, _NT, _TN = (2, 1), (2, 2), (1, 1)


def _make_dots(nf, nb_nn, nb_nt, nb_tn):
    @jax.custom_vjp
    def nn(a, b):
        return _bmm(a, b, *_NN, nf, nf)

    @jax.custom_vjp
    def nt(a, b):
        return _bmm(a, b, *_NT, nf, nf)

    @jax.custom_vjp
    def tn(a, b):
        return _bmm(a, b, *_TN, nf, nf)

    nn.defvjp(lambda a, b: (nn(a, b), (a, b)),
              lambda r, g: (_bmm(g, r[1], *_NT, nb_nn, nb_nn), _bmm(r[0], g, *_TN, nb_nn, nb_nn)))
    nt.defvjp(lambda a, b: (nt(a, b), (a, b)),
              lambda r, g: (_bmm(g, r[1], *_NN, nb_nt, nb_nt), _bmm(g, r[0], *_TN, nb_nt, nb_nt)))
    tn.defvjp(lambda a, b: (tn(a, b), (a, b)),
              lambda r, g: (_bmm(r[1], g, *_NT, nb_tn, nb_tn), _bmm(r[0], g, *_NN, nb_tn, nb_tn)))
    return nn, nt, tn


_SCAN_NF, _SCAN_NB = 1, 1
nn, nt, tn = _make_dots(_SCAN_NF, 1, 2, 1)


@jax.custom_vjp
def cumdot(ones, x):
    return _bmm(ones, x, *_NN, 1, 3)


cumdot.defvjp(lambda o, x: (cumdot(o, x), o), lambda o, g: (jnp.zeros_like(o), _bmm(o, g, *_TN, 1, 3)))


def _solve_powers(l):
    pw = [l]
    for _ in range(int(math.log2(l.shape[-1])) - 1):
        pw.append(_bmm(pw[-1], pw[-1], *_NN, _SCAN_NF, _SCAN_NF))
    return pw


@jax.custom_vjp
def tri_solve(l, rhs):
    x = rhs
    for p in _solve_powers(l):
        x = x + _bmm(p, x, *_NN, _SCAN_NF, _SCAN_NF)
    return x


def _tri_solve_fwd(l, rhs):
    pw = _solve_powers(l)
    x = rhs
    for p in pw:
        x = x + _bmm(p, x, *_NN, _SCAN_NF, _SCAN_NF)
    return x, (pw, x)


def _tri_solve_bwd(res, g):
    pw, x = res
    y = g
    for p in pw:
        y = y + _bmm(p, y, *_TN, _SCAN_NB, _SCAN_NB)
    return _bmm(y, x, *_NT, _SCAN_NB, _SCAN_NB), y


tri_solve.defvjp(_tri_solve_fwd, _tri_solve_bwd)


def _rms(x, g):
    return x * lax.rsqrt(jnp.mean(x * x, axis=-1, keepdims=True) + NORM_EPS) * g


def _softplus(x):
    pos = x > 0
    return jnp.where(pos, x, 0.0) + jnp.log(1.0 + jnp.exp(-jnp.where(pos, x, -x)))


def _silu(z):
    return z * jax.nn.sigmoid(z)


def _tile(n, cands):
    for c in cands:
        if n % c == 0:
            return c
    raise ValueError(n)


_MM_VMEM_BYTES = 32 * 2**20


def _mm_tiles(m, n, k):
    best = None
    for tm in (2048, 1024, 512, 256):
        for tn_ in (2048, 1024, 512, 256):
            for d in range(k // LANE, 0, -1):
                tk = LANE * d
                if m % tm or n % tn_ or k % tk:
                    continue
                nk = k // tk
                vmem = 4 * tk * (tm + tn_) + 8 * tm * tn_ + (4 * tm * tn_ if nk > 1 else 0)
                if vmem > _MM_VMEM_BYTES:
                    continue
                a_reads = n // tn_ if nk > 1 else 1
                b_reads = 1 if (nk == 1 and n == tn_) else m // tm
                acc_rmw = nk * m * n if nk > 1 else 0
                cost = (a_reads * m * k + b_reads * k * n + acc_rmw, -tm * tn_ * tk)
                if best is None or cost < best[0]:
                    best = (cost, (tm, tn_, tk))
    return best[1]


def matmul(name, a, b, mode, out_dtype=F32):
    if mode == "nn":
        (m, k), n = a.shape, b.shape[1]
    elif mode == "nt":
        (m, k), n = a.shape, b.shape[0]
    else:
        (k, m), n = a.shape, b.shape[1]
    tm, tn_, tk = _mm_tiles(m, n, k)
    nk = k // tk
    if mode == "nn":
        a_spec = pl.BlockSpec((tm, tk), lambda i, j, kk: (i, kk))
        b_spec = pl.BlockSpec((tk, tn_), lambda i, j, kk: (kk, j))
        ca, cb = 1, 0
    elif mode == "nt":
        a_spec = pl.BlockSpec((tm, tk), lambda i, j, kk: (i, kk))
        b_spec = pl.BlockSpec((tn_, tk), lambda i, j, kk: (j, kk))
        ca, cb = 1, 1
    else:
        a_spec = pl.BlockSpec((tk, tm), lambda i, j, kk: (kk, i))
        b_spec = pl.BlockSpec((tk, tn_), lambda i, j, kk: (kk, j))
        ca, cb = 0, 0

    def body(a_ref, b_ref, o_ref, *acc):
        part = _dg(a_ref[...].astype(BF16), b_ref[...].astype(BF16), ca, cb)
        if nk == 1:
            o_ref[...] = part.astype(o_ref.dtype)
            return
        acc_ref, kk = acc[0], pl.program_id(2)

        @pl.when(kk == 0)
        def _():
            acc_ref[...] = part

        @pl.when(kk > 0)
        def _():
            acc_ref[...] += part

        @pl.when(kk == nk - 1)
        def _():
            o_ref[...] = acc_ref[...].astype(o_ref.dtype)

    return pl.pallas_call(
        body, name=name, grid=(m // tm, n // tn_, nk),
        in_specs=[a_spec, b_spec],
        out_specs=pl.BlockSpec((tm, tn_), lambda i, j, kk: (i, j)),
        out_shape=jax.ShapeDtypeStruct((m, n), out_dtype),
        scratch_shapes=[pltpu.VMEM((tm, tn_), F32)] if nk > 1 else [],
        compiler_params=pltpu.CompilerParams(dimension_semantics=("parallel", "parallel", "arbitrary"),
                                             vmem_limit_bytes=VMEM_BIG),
    )(a, b)


def _rspec(tr, width, blk):
    return pl.BlockSpec((tr, width), lambda i: (i, blk))


def _full_spec(arr):
    return pl.BlockSpec(arr.shape, lambda i: (0,) * arr.ndim)


class Stage:
    def __init__(self, name, f, outs, tr, diff_rows, diff_params, drow_dtypes):
        self.name, self.f, self.outs, self.tr = name, f, outs, tr
        self.diff_rows, self.diff_params, self.drow_dtypes = diff_rows, diff_params, drow_dtypes

    def fwd(self, rows, params):
        f, nr, npar = self.f, len(rows), len(params)
        stored = [(w, dt) for (w, dt) in self.outs if dt is not None]
        keep = [i for i, (w, dt) in enumerate(self.outs) if dt is not None]

        def body(*refs):
            vals = f(*[r[...].astype(F32) for r in refs[:nr]], *[p[...] for p in refs[nr:nr + npar]])
            for o_ref, i in zip(refs[nr + npar:], keep):
                o_ref[...] = vals[i].astype(o_ref.dtype)

        return pl.pallas_call(
            body, name=self.name + "_fwd", grid=(T // self.tr,),
            in_specs=[_rspec(self.tr, w, b) for (_, w, b) in rows] + [_full_spec(p) for p in params],
            out_specs=[_rspec(self.tr, w, 0) for (w, _) in stored],
            out_shape=[jax.ShapeDtypeStruct((T, w), dt) for (w, dt) in stored],
            compiler_params=pltpu.CompilerParams(dimension_semantics=("arbitrary",), vmem_limit_bytes=VMEM_BIG),
        )(*[r[0] for r in rows], *params)

    def bwd(self, rows, params, cts):
        f, nr, npar = self.f, len(rows), len(params)
        dr_idx, dp_idx = self.diff_rows, self.diff_params
        flat_cts = [c for lst in cts for c in lst]
        nct = len(flat_cts)

        def body(*refs):
            row_refs, par_refs = refs[:nr], refs[nr:nr + npar]
            ct_refs = refs[nr + npar:nr + npar + nct]
            drow_refs = refs[nr + npar + nct:nr + npar + nct + len(dr_idx)]
            dpar_refs = refs[nr + npar + nct + len(dr_idx):]
            row_vals = [r[...].astype(F32) for r in row_refs]
            par_vals = [p[...] for p in par_refs]

            def g(*dv):
                rv, pv = list(row_vals), list(par_vals)
                for j, i in enumerate(dr_idx):
                    rv[i] = dv[j]
                for j, i in enumerate(dp_idx):
                    pv[i] = dv[len(dr_idx) + j]
                return f(*rv, *pv)

            _, vjp = jax.vjp(g, *[row_vals[i] for i in dr_idx], *[par_vals[i] for i in dp_idx])
            ct_vals, pos = [], 0
            for lst in cts:
                acc = ct_refs[pos][...].astype(F32)
                for q in range(1, len(lst)):
                    acc = acc + ct_refs[pos + q][...].astype(F32)
                pos += len(lst)
                ct_vals.append(acc)
            grads = vjp(tuple(ct_vals))
            for j, r in enumerate(drow_refs):
                r[...] = grads[j].astype(r.dtype)

            @pl.when(pl.program_id(0) == 0)
            def _():
                for r in dpar_refs:
                    r[...] = jnp.zeros_like(r)

            for j, r in enumerate(dpar_refs):
                r[...] += grads[len(dr_idx) + j].astype(F32)

        drow_shapes = [jax.ShapeDtypeStruct((T, rows[i][1]), dt) for i, dt in zip(dr_idx, self.drow_dtypes)]
        dpar_shapes = [jax.ShapeDtypeStruct(params[i].shape, F32) for i in dp_idx]
        res = pl.pallas_call(
            body, name=self.name + "_bwd", grid=(T // self.tr,),
            in_specs=[_rspec(self.tr, w, b) for (_, w, b) in rows] + [_full_spec(p) for p in params]
            + [_rspec(self.tr, w, b) for (_, w, b) in flat_cts],
            out_specs=[_rspec(self.tr, rows[i][1], 0) for i in dr_idx] + [_full_spec(params[i]) for i in dp_idx],
            out_shape=drow_shapes + dpar_shapes,
            compiler_params=pltpu.CompilerParams(dimension_semantics=("arbitrary",), vmem_limit_bytes=VMEM_BIG),
        )(*[r[0] for r in rows], *params, *[c[0] for c in flat_cts])
        return res[:len(dr_idx)], res[len(dr_idx):]


def f_pre(x, g):
    return _rms(x, g), x


def f_mla(q_a, kv_a, kr, cosq, sinq, cosk, sink, gq, gkv, wq, wkv, e):
    q = bdot(_rms(q_a, gq), wq)
    kv = bdot(_rms(kv_a, gkv), wkv)
    t1, t2 = q[:, 1024:1280], q[:, 1280:1536]
    k1, k2 = kr[:, :LANE], kr[:, LANE:]
    kr1 = k1 * cosk - k2 * sink
    kr2 = k1 * sink + k2 * cosk
    return (q[:, :1024], t1 * cosq - t2 * sinq, t1 * sinq + t2 * cosq,
            kv[:, :1024], bdot(kr1, e), bdot(kr2, e), kv[:, 1024:])


def f_rwkv_pre(r, k, v, lora, w0f, w0b, a0f, a0b, kkw, kaw, w2f, w2b, a2f, a2b, g2):
    wdf, wdb, adf, adb = (lora[:, i * LANE:(i + 1) * LANE] for i in range(4))

    def logdecay(w0, wd, w2):
        z = w0 + bdot(jnp.tanh(wd), w2)
        return -jnp.exp(-_softplus(-z) - 0.5)

    a_f = jax.nn.sigmoid(a0f + bdot(adf, a2f))
    a_b = jax.nn.sigmoid(a0b + bdot(adb, a2b))
    kk = k * kkw
    kk = kk / jnp.maximum(jnp.sqrt(headsum(kk * kk, g2)), 1e-12)
    return (r, v, logdecay(w0f, wdf, w2f), logdecay(w0b, wdb, w2b),
            k * (1.0 + (a_f - 1.0) * kaw), k * (1.0 + (a_b - 1.0) * kaw), -kk, kk * a_f, kk * a_b)


def f_rwkv_post(yf, yb, r, kf, kb, v, z, gng, gnb, rk, g2):
    y = yf + yb
    mu = headsum(y, g2) * (1.0 / RN)
    d = y - mu
    var = headsum(d * d, g2) * (1.0 / RN)
    yn = d * lax.rsqrt(var + GN_EPS) * gng + gnb
    bonus = headsum(r * (kf + kb) * rk, g2) * v
    return ((yn + bonus) * _silu(z),)


def f_gate(y, z):
    return (y * _silu(z),)


def f_merge(um, ur, gm, gr):
    return (jax.nn.sigmoid(gm) * um + jax.nn.sigmoid(gr) * ur,)


_SHIFT_W = 256


def _lerp_colblock(j):
    return jnp.where(j < 3072 // _SHIFT_W, OFF_RKV // _SHIFT_W + j, OFF_LORA // _SHIFT_W + j - 3072 // _SHIFT_W)


def _nbr_mean(x):
    row = lax.broadcasted_iota(jnp.int32, x.shape, 0)
    up = jnp.where(row == 0, 0.0, pltpu.roll(x, 1, 0))
    dn = jnp.where(row == T - 1, 0.0, pltpu.roll(x, T - 1, 0))
    return 0.5 * (up + dn)


def shift_fwd(proj, mu):
    def body(x_ref, mu_ref, o_ref):
        x = x_ref[...]
        o_ref[...] = x + mu_ref[...] * (_nbr_mean(x) - x)

    return pl.pallas_call(
        body, name="shift_fwd", grid=(NLERP // _SHIFT_W,),
        in_specs=[pl.BlockSpec((T, _SHIFT_W), lambda j: (0, _lerp_colblock(j))),
                  pl.BlockSpec((1, _SHIFT_W), lambda j: (0, j))],
        out_specs=pl.BlockSpec((T, _SHIFT_W), lambda j: (0, j)),
        out_shape=jax.ShapeDtypeStruct((T, NLERP), F32),
        compiler_params=pltpu.CompilerParams(dimension_semantics=("parallel",), vmem_limit_bytes=VMEM_BIG),
    )(proj, mu)


def shift_bwd(proj, mu, g):
    def body(x_ref, mu_ref, g_ref, dx_ref, dmu_ref):
        x, gv = x_ref[...], g_ref[...]
        dmu_ref[...] = jnp.sum(gv * (_nbr_mean(x) - x), axis=0, keepdims=True)
        gm = gv * mu_ref[...]
        dx_ref[...] = (gv - gm + _nbr_mean(gm)).astype(dx_ref.dtype)

    col = pl.BlockSpec((T, _SHIFT_W), lambda j: (0, j))
    vec = pl.BlockSpec((1, _SHIFT_W), lambda j: (0, j))
    return pl.pallas_call(
        body, name="shift_bwd", grid=(NLERP // _SHIFT_W,),
        in_specs=[pl.BlockSpec((T, _SHIFT_W), lambda j: (0, _lerp_colblock(j))), vec, col],
        out_specs=[col, vec],
        out_shape=[jax.ShapeDtypeStruct((T, NLERP), BF16), jax.ShapeDtypeStruct((1, NLERP), F32)],
        compiler_params=pltpu.CompilerParams(dimension_semantics=("parallel",), vmem_limit_bytes=VMEM_BIG),
    )(proj, mu, g)


_TQ = 256
_ATT_SCALE = (NOPE + ROPE) ** -0.5


def _probs(q, k):
    s = _dg(q, k, 1, 1) * _ATT_SCALE
    e = jnp.exp(s - jnp.max(s, axis=-1, keepdims=True))
    return e * (1.0 / jnp.sum(e, axis=-1, keepdims=True))


def attn_fwd(q, k, v):
    def body(q_ref, k_ref, v_ref, o_ref):
        p = _probs(q_ref[0], k_ref[0])
        o_ref[0] = _dg(p.astype(BF16), v_ref[0], 1, 0)

    dq = NOPE + ROPE
    return pl.pallas_call(
        body, name="attn_fwd", grid=(HEADS, T // _TQ),
        in_specs=[pl.BlockSpec((1, _TQ, dq), lambda h, i: (h, i, 0)),
                  pl.BlockSpec((1, T, dq), lambda h, i: (h, 0, 0)),
                  pl.BlockSpec((1, T, VDIM), lambda h, i: (h, 0, 0))],
        out_specs=pl.BlockSpec((1, _TQ, VDIM), lambda h, i: (h, i, 0)),
        out_shape=jax.ShapeDtypeStruct((HEADS, T, VDIM), F32),
        compiler_params=pltpu.CompilerParams(dimension_semantics=("parallel", "arbitrary"), vmem_limit_bytes=VMEM_BIG),
    )(q, k, v)


def attn_bwd(q, k, v, do):
    def body(q_ref, k_ref, v_ref, do_ref, dq_ref, dk_ref, dv_ref):
        @pl.when(pl.program_id(1) == 0)
        def _():
            dk_ref[...] = jnp.zeros_like(dk_ref)
            dv_ref[...] = jnp.zeros_like(dv_ref)

        qv, kv_, vv = q_ref[0], k_ref[0], v_ref[0]
        dob = do_ref[0].astype(BF16)
        p = _probs(qv, kv_)
        dv_ref[0] += _dg(p.astype(BF16), dob, 0, 0)
        dp = _dg(dob, vv, 1, 1)
        ds = (p * (dp - jnp.sum(dp * p, axis=-1, keepdims=True)) * _ATT_SCALE).astype(BF16)
        dq_ref[0] = _dg(ds, kv_, 1, 0)
        dk_ref[0] += _dg(ds, qv, 0, 0)

    dq = NOPE + ROPE
    return pl.pallas_call(
        body, name="attn_bwd", grid=(HEADS, T // _TQ),
        in_specs=[pl.BlockSpec((1, _TQ, dq), lambda h, i: (h, i, 0)),
                  pl.BlockSpec((1, T, dq), lambda h, i: (h, 0, 0)),
                  pl.BlockSpec((1, T, VDIM), lambda h, i: (h, 0, 0)),
                  pl.BlockSpec((1, _TQ, VDIM), lambda h, i: (h, i, 0))],
        out_specs=[pl.BlockSpec((1, _TQ, dq), lambda h, i: (h, i, 0)),
                   pl.BlockSpec((1, T, dq), lambda h, i: (h, 0, 0)),
                   pl.BlockSpec((1, T, VDIM), lambda h, i: (h, 0, 0))],
        out_shape=[jax.ShapeDtypeStruct((HEADS, T, dq), F32), jax.ShapeDtypeStruct((HEADS, T, dq), F32),
                   jax.ShapeDtypeStruct((HEADS, T, VDIM), F32)],
        compiler_params=pltpu.CompilerParams(dimension_semantics=("parallel", "arbitrary"), vmem_limit_bytes=VMEM_BIG),
    )(q, k, v, do)


def _chunk(r, lw, k, v, a, b, ht, *, reverse):
    hb, c, _ = r.shape
    ti = lax.broadcasted_iota(jnp.int32, (c, c), 0)
    si = lax.broadcasted_iota(jnp.int32, (c, c), 1)
    incl = (si >= ti) if reverse else (si <= ti)
    strict = (si > ti) if reverse else (si < ti)
    ones = jnp.broadcast_to(incl.astype(F32)[None], (hb, c, c))
    cum = cumdot(ones, lw)
    cum_ex = cum - lw
    tot = jnp.sum(lw, axis=1, keepdims=True)
    mid = 0.5 * tot
    rt, at = r * jnp.exp(cum - mid), a * jnp.exp(cum_ex - mid)
    einv = jnp.exp(mid - cum)
    kt, bt = k * einv, b * einv
    m_ab = jnp.where(strict, nt(at, bt), 0.0)
    m_ak = jnp.where(strict, nt(at, kt), 0.0)
    m_rb = jnp.where(incl, nt(rt, bt), 0.0)
    m_rk = jnp.where(incl, nt(rt, kt), 0.0)
    u = tri_solve(m_ab, nt(a * jnp.exp(cum_ex), ht) + nn(m_ak, v))
    y = nt(r * jnp.exp(cum), ht) + nn(m_rb, u) + nn(m_rk, v)
    eend = jnp.exp(tot - cum)
    ht_new = ht * jnp.exp(tot) + tn(u, b * eend) + tn(v, k * eend)
    return y, ht_new


_HB_F, _HB_B = 16, 16


def _split_heads(x):
    return jnp.stack([x[:, i * RN:(i + 1) * RN] for i in range(x.shape[1] // RN)])


def _merge_heads(y):
    return jnp.concatenate([y[i] for i in range(y.shape[0])], axis=1)


def _chunk_map(reverse, backward):
    flip = reverse != backward
    return (lambda g, c: (NCH - 1 - c, g)) if flip else (lambda g, c: (c, g))


def scan_fwd(name, r, lw, k, v, a, b, reverse):
    hb = _HB_F
    cmap = _chunk_map(reverse, False)

    def body(r_ref, lw_ref, k_ref, v_ref, a_ref, b_ref, y_ref, h0_ref, ht_ref):
        @pl.when(pl.program_id(1) == 0)
        def _():
            ht_ref[...] = jnp.zeros_like(ht_ref)

        ht = ht_ref[...]
        h0_ref[0] = ht
        ins = [_split_heads(x[...]) for x in (r_ref, lw_ref, k_ref, v_ref, a_ref, b_ref)]
        y, hn = _chunk(*ins, ht, reverse=reverse)
        y_ref[...] = _merge_heads(y)
        ht_ref[...] = hn

    io = pl.BlockSpec((CHUNK, hb * RN), cmap)
    return pl.pallas_call(
        body, name=name, grid=(RH // hb, NCH),
        in_specs=[io] * 6,
        out_specs=[io, pl.BlockSpec((1, hb, RN, RN), lambda g, c: (cmap(g, c)[0], g, 0, 0))],
        out_shape=[jax.ShapeDtypeStruct((T, RW), F32), jax.ShapeDtypeStruct((NCH, RH, RN, RN), F32)],
        scratch_shapes=[pltpu.VMEM((hb, RN, RN), F32)],
        compiler_params=pltpu.CompilerParams(dimension_semantics=("parallel", "arbitrary"), vmem_limit_bytes=VMEM_BIG),
    )(r, lw, k, v, a, b)


def scan_bwd(name, r, lw, k, v, a, b, h0, dy, reverse):
    hb = _HB_B
    cmap = _chunk_map(reverse, True)

    def body(r_ref, lw_ref, k_ref, v_ref, a_ref, b_ref, h0_ref, dy_ref, *rest):
        d_refs, dht_ref = rest[:6], rest[6]

        @pl.when(pl.program_id(1) == 0)
        def _():
            dht_ref[...] = jnp.zeros_like(dht_ref)

        ins = [_split_heads(x[...]) for x in (r_ref, lw_ref, k_ref, v_ref, a_ref, b_ref)]
        _, vjp = jax.vjp(functools.partial(_chunk, reverse=reverse), *ins, h0_ref[0])
        grads = vjp((_split_heads(dy_ref[...]), dht_ref[...]))
        for d_ref, gval in zip(d_refs, grads[:6]):
            d_ref[...] = _merge_heads(gval)
        dht_ref[...] = grads[6]

    io = pl.BlockSpec((CHUNK, hb * RN), cmap)
    return pl.pallas_call(
        body, name=name, grid=(RH // hb, NCH),
        in_specs=[io] * 6 + [pl.BlockSpec((1, hb, RN, RN), lambda g, c: (cmap(g, c)[0], g, 0, 0)), io],
        out_specs=[io] * 6,
        out_shape=[jax.ShapeDtypeStruct((T, RW), F32)] * 6,
        scratch_shapes=[pltpu.VMEM((hb, RN, RN), F32)],
        compiler_params=pltpu.CompilerParams(dimension_semantics=("parallel", "arbitrary"), vmem_limit_bytes=VMEM_BIG),
    )(r, lw, k, v, a, b, h0, dy)


def loss_stage(out, x2, tgt, g_post):
    tr = 256

    def body(o_ref, x_ref, t_ref, g_ref, do_ref, dy_ref, dg_ref, loss_ref):
        @pl.when(pl.program_id(0) == 0)
        def _():
            dg_ref[...] = jnp.zeros_like(dg_ref)
            loss_ref[...] = jnp.zeros_like(loss_ref)

        nrm, vjp = jax.vjp(_rms, o_ref[...], g_ref[...])
        e = x_ref[...] + nrm - t_ref[...]
        s = jnp.sum(jnp.sum(e * e, axis=1, keepdims=True), axis=0, keepdims=True)
        loss_ref[...] += jnp.broadcast_to(s * (0.5 / D), loss_ref.shape)
        dy = e * (1.0 / D)
        do, dg = vjp(dy)
        do_ref[...] = do.astype(do_ref.dtype)
        dy_ref[...] = dy
        dg_ref[...] += dg

    row = pl.BlockSpec((tr, D), lambda i: (i, 0))
    return pl.pallas_call(
        body, name="loss_stage", grid=(T // tr,),
        in_specs=[row, row, row, pl.BlockSpec((1, D), lambda i: (0, 0))],
        out_specs=[row, row, pl.BlockSpec((1, D), lambda i: (0, 0)), pl.BlockSpec((8, LANE), lambda i: (0, 0))],
        out_shape=[jax.ShapeDtypeStruct((T, D), BF16), jax.ShapeDtypeStruct((T, D), F32),
                   jax.ShapeDtypeStruct((1, D), F32), jax.ShapeDtypeStruct((8, LANE), F32)],
        compiler_params=pltpu.CompilerParams(dimension_semantics=("arbitrary",), vmem_limit_bytes=VMEM_BIG),
    )(out, x2, tgt, g_post)


_EW_BLOCK_BYTES = 1 << 20


def _row_tile(rows, cols):
    best = None
    for tr in range(16, rows + 1, 16):
        if rows % tr == 0 and tr * cols * 4 <= _EW_BLOCK_BYTES:
            best = tr
    return best or rows


def _axis_tile(shape, axis, words):
    rows, cols = shape
    n, other, unit = (rows, cols, 16) if axis == 0 else (cols, rows, LANE)
    best = unit if n % unit == 0 else n
    for t in range(unit, n + 1, unit):
        if n % t == 0 and t * other * words * 4 <= _EW_BLOCK_BYTES:
            best = t
    blk = (best, cols) if axis == 0 else (rows, best)
    at = (lambda s: (s, 0)) if axis == 0 else (lambda s: (0, s))
    return blk, n // best, at


def _adamw_update(g, w_ref, m_ref, v_ref, g_ref, d_ref, nm_ref, nv_ref):
    mm = ADAM_B1 * m_ref[...] + (1.0 - ADAM_B1) * g
    vv = ADAM_B2 * v_ref[...] + (1.0 - ADAM_B2) * (g * g)
    m_hat = mm / (1.0 - ADAM_B1 ** ADAM_STEP)
    v_hat = vv / (1.0 - ADAM_B2 ** ADAM_STEP)
    g_ref[...] = g
    d_ref[...] = -ADAM_LR * (m_hat / (jnp.sqrt(v_hat) + ADAM_EPS) + ADAM_WD * w_ref[...])
    nm_ref[...] = mm
    nv_ref[...] = vv


def adamw(name, w, m, v, parts):
    rows, cols = w.shape
    br = _row_tile(rows, cols)
    npart = len(parts)

    def body(w_ref, m_ref, v_ref, *rest):
        g = rest[0][...].astype(F32)
        for p in rest[1:npart]:
            g = g + p[...].astype(F32)
        _adamw_update(g, w_ref, m_ref, v_ref, *rest[npart:])

    blk = pl.BlockSpec((br, cols), lambda i: (i, 0))
    return pl.pallas_call(
        body, name=name, grid=(rows // br,),
        in_specs=[blk] * (3 + npart), out_specs=[blk] * 4,
        out_shape=[jax.ShapeDtypeStruct((rows, cols), F32)] * 4,
        compiler_params=pltpu.CompilerParams(dimension_semantics=("parallel",), vmem_limit_bytes=VMEM_BIG),
    )(w, m, v, *parts)


def adamw_halves(name, place, w, m, v, mine, theirs, axis):
    half_shape = mine.shape
    blk_shape, nb, at = _axis_tile(half_shape, axis, 1)

    def body(p_ref, w_ref, m_ref, v_ref, a_ref, b_ref, *outs):
        own = (pl.program_id(0) // nb) == p_ref[0]
        _adamw_update(jnp.where(own, a_ref[...], b_ref[...]), w_ref, m_ref, v_ref, *outs)

    blk = pl.BlockSpec(blk_shape, lambda i, p: at(i))
    half = pl.BlockSpec(blk_shape, lambda i, p: at(i % nb))
    return pl.pallas_call(
        body, name=name,
        grid_spec=pltpu.PrefetchScalarGridSpec(num_scalar_prefetch=1, grid=(2 * nb,),
                                               in_specs=[blk] * 3 + [half] * 2, out_specs=[blk] * 4),
        out_shape=[jax.ShapeDtypeStruct(w.shape, F32)] * 4,
        compiler_params=pltpu.CompilerParams(dimension_semantics=("arbitrary",), vmem_limit_bytes=VMEM_BIG),
    )(place, w, m, v, mine, theirs)


def pair_sum(name, place, send, other, axis):
    blk_shape, nb, at = _axis_tile(other.shape[1:], axis, 4)

    def body(p_ref, a_ref, b_ref, o_ref):
        o_ref[...] = (a_ref[...].astype(F32) + b_ref[...].astype(F32)).astype(o_ref.dtype)

    blk = pl.BlockSpec((4,) + blk_shape, lambda i, p: (0,) + at(i))
    mine = pl.BlockSpec((4,) + blk_shape, lambda i, p: (0,) + at(p[0] * nb + i))
    return pl.pallas_call(
        body, name=name,
        grid_spec=pltpu.PrefetchScalarGridSpec(num_scalar_prefetch=1, grid=(nb,), in_specs=[mine, blk], out_specs=blk),
        out_shape=jax.ShapeDtypeStruct(other.shape, BF16),
        compiler_params=pltpu.CompilerParams(dimension_semantics=("arbitrary",), vmem_limit_bytes=VMEM_BIG),
    )(place, send, other)


def sum4(name, place, recv, own, axis):
    blk_shape, nb, at = _axis_tile(recv.shape[1:], axis, 4)

    def body(p_ref, r_ref, s_ref, o_ref):
        me = p_ref[1]
        t = [jnp.where(me == j, s_ref[j], r_ref[j]).astype(F32) for j in range(4)]
        o_ref[...] = ((t[0] + t[1]) + t[2]) + t[3]

    blk = pl.BlockSpec((4,) + blk_shape, lambda i, p: (0,) + at(i))
    return pl.pallas_call(
        body, name=name,
        grid_spec=pltpu.PrefetchScalarGridSpec(num_scalar_prefetch=1, grid=(nb,), in_specs=[blk, blk],
                                               out_specs=pl.BlockSpec(blk_shape, lambda i, p: at(i))),
        out_shape=jax.ShapeDtypeStruct(recv.shape[1:], F32),
        compiler_params=pltpu.CompilerParams(dimension_semantics=("arbitrary",), vmem_limit_bytes=VMEM_BIG),
    )(place, recv, own)


_ANY = pl.BlockSpec(memory_space=pl.ANY)


def _place():
    x, y, c = lax.axis_index("x"), lax.axis_index("y"), lax.axis_index("c")
    return x, y, c, 2 * x + y


def _chip_peers(x, y):
    out = []
    for k in (1, 2, 3):
        px = 1 - x if k & 2 else x
        py = 1 - y if k & 1 else y
        out.append((k, px, py, 2 * px + py))
    return out


def _half(c, shape, axis):
    n = shape[axis] // 2
    sl = pl.ds(pl.multiple_of(c * n, 16 if axis == 0 else LANE), n)
    return (sl,) if axis == 0 else (pl.ds(0, shape[0]), sl)


def gather_weights(srcs, axes):
    n = len(srcs)

    def body(*refs):
        src, dst = refs[:n], refs[n:2 * n]
        ssem, rsem, fssem, frsem = refs[2 * n:]
        x, y, c, me = _place()
        sib = (x, y, 1 - c)
        waits, chain = [], []
        for i in range(n):
            mine, other = _half(c, srcs[i].shape, axes[i]), _half(1 - c, srcs[i].shape, axes[i])
            for k, px, py, peer in _chip_peers(x, y):
                sems = dict(send_sem=ssem.at[i, k - 1], recv_sem=rsem.at[i, k - 1], device_id=(px, py, c),
                            device_id_type=MESH_IDS)
                fsems = dict(send_sem=fssem.at[i, k - 1], recv_sem=frsem.at[i, k - 1], device_id=sib,
                             device_id_type=MESH_IDS)
                got = dst[i].at[(peer,) + mine]
                snd = pltpu.make_async_remote_copy(src_ref=src[i].at[mine], dst_ref=dst[i].at[(me,) + mine], **sems)
                rcv = pltpu.make_async_remote_copy(src_ref=src[i].at[mine], dst_ref=got, **sems)
                fwd = pltpu.make_async_remote_copy(src_ref=got, dst_ref=got, **fsems)
                frcv = pltpu.make_async_remote_copy(src_ref=got, dst_ref=dst[i].at[(peer,) + other], **fsems)
                snd.start()
                chain.append((rcv, fwd))
                waits += [frcv.wait_recv, snd.wait_send, fwd.wait_send]
        for rcv, fwd in chain:
            rcv.wait_recv()
            fwd.start()
        for w in waits:
            w()

    return pl.pallas_call(
        body, name="gather_weights", in_specs=[_ANY] * n, out_specs=[_ANY] * n,
        out_shape=[jax.ShapeDtypeStruct((4,) + s.shape, s.dtype) for s in srcs],
        scratch_shapes=[pltpu.SemaphoreType.DMA((n, 3))] * 4,
    )(*srcs)


def pair_exchange(srcs, axes):
    n = len(srcs)

    def half_shape(s, axis):
        return (4, s.shape[1] // 2, s.shape[2]) if axis == 0 else (4, s.shape[1], s.shape[2] // 2)

    def body(*refs):
        src, other = refs[:n], refs[n:2 * n]
        ssem, rsem = refs[2 * n:]
        x, y, c, _ = _place()
        cps = []
        for i in range(n):
            idx = (pl.ds(0, 4),) + _half(1 - c, srcs[i].shape[1:], axes[i])
            cps.append(pltpu.make_async_remote_copy(
                src_ref=src[i].at[idx], dst_ref=other[i], send_sem=ssem.at[i], recv_sem=rsem.at[i],
                device_id=(x, y, 1 - c), device_id_type=MESH_IDS))
            cps[-1].start()
        for cp in cps:
            cp.wait()

    return pl.pallas_call(
        body, name="pair_exchange", in_specs=[_ANY] * n, out_specs=[_ANY] * n,
        out_shape=[jax.ShapeDtypeStruct(half_shape(s, a), s.dtype) for s, a in zip(srcs, axes)],
        scratch_shapes=[pltpu.SemaphoreType.DMA((n,))] * 2,
    )(*srcs)


def scatter_grads(srcs):
    n = len(srcs)

    def body(*refs):
        src, dst = refs[:n], refs[n:2 * n]
        ssem, rsem = refs[2 * n:]
        x, y, c, me = _place()
        waits = []
        for i in range(n):
            for k, px, py, peer in _chip_peers(x, y):
                sems = dict(send_sem=ssem.at[i, k - 1], recv_sem=rsem.at[i, k - 1], device_id=(px, py, c),
                            device_id_type=MESH_IDS)
                snd = pltpu.make_async_remote_copy(src_ref=src[i].at[peer], dst_ref=dst[i].at[me], **sems)
                rcv = pltpu.make_async_remote_copy(src_ref=src[i].at[peer], dst_ref=dst[i].at[peer], **sems)
                snd.start()
                waits += [rcv.wait_recv, snd.wait_send]
        for w in waits:
            w()

    return pl.pallas_call(
        body, name="scatter_grads", in_specs=[_ANY] * n, out_specs=[_ANY] * n,
        out_shape=[jax.ShapeDtypeStruct(s.shape, s.dtype) for s in srcs],
        scratch_shapes=[pltpu.SemaphoreType.DMA((n, 3))] * 2,
    )(*srcs)


def swap_halves(srcs):
    n = len(srcs)

    def body(*refs):
        src, dst = refs[:n], refs[n:2 * n]
        ssem, rsem = refs[2 * n:]
        x, y, c, _ = _place()
        cps = []
        for i in range(n):
            cps.append(pltpu.make_async_remote_copy(src_ref=src[i], dst_ref=dst[i], send_sem=ssem.at[i],
                                                    recv_sem=rsem.at[i], device_id=(x, y, 1 - c),
                                                    device_id_type=MESH_IDS))
            cps[-1].start()
        for cp in cps:
            cp.wait()

    return pl.pallas_call(
        body, name="swap_halves", in_specs=[_ANY] * n, out_specs=[_ANY] * n,
        out_shape=[jax.ShapeDtypeStruct(s.shape, s.dtype) for s in srcs],
        scratch_shapes=[pltpu.SemaphoreType.DMA((n,))] * 2,
    )(*srcs)


def allgather8(name, src):
    rows = src.shape[0]

    def body(src_ref, dst_ref, send_sems, recv_sems):
        x, y, c = lax.axis_index("x"), lax.axis_index("y"), lax.axis_index("c")
        me = 4 * x + 2 * y + c
        dst_ref[me] = src_ref[...]
        sends, recvs = [], []
        for k in range(1, 8):
            px = 1 - x if k & 4 else x
            py = 1 - y if k & 2 else y
            pc = 1 - c if k & 1 else c
            peer = 4 * px + 2 * py + pc
            for lst, slot in ((sends, me), (recvs, peer)):
                lst.append(pltpu.make_async_remote_copy(
                    src_ref=src_ref, dst_ref=dst_ref.at[slot], send_sem=send_sems.at[k - 1],
                    recv_sem=recv_sems.at[k - 1], device_id=(px, py, pc), device_id_type=MESH_IDS))
        for cp in sends:
            cp.start()
        for cp in recvs:
            cp.wait_recv()
        for cp in sends:
            cp.wait_send()

    vm = pl.BlockSpec(memory_space=pltpu.VMEM)
    return pl.pallas_call(
        body, name=name, in_specs=[vm], out_specs=vm,
        out_shape=jax.ShapeDtypeStruct((8, rows, LANE), src.dtype),
        scratch_shapes=[pltpu.SemaphoreType.DMA((7,)), pltpu.SemaphoreType.DMA((7,))],
    )(src)


WEIGHTS = ['g_pre', 'w_in', 'mla_q_norm', 'mla_wq_b', 'mla_kv_norm', 'mla_wkv_b', 'rwkv_mu', 'rwkv_w0_f', 'rwkv_w2_f',
           'rwkv_w0_b', 'rwkv_w2_b', 'rwkv_a0_f', 'rwkv_a2_f', 'rwkv_a0_b', 'rwkv_a2_b', 'rwkv_k_k', 'rwkv_k_a',
           'rwkv_r_k', 'rwkv_gn_g', 'rwkv_gn_b', 'w_br_mla', 'w_br_rwkv', 'w_out', 'g_post']
BIG_SHAPES = {'w_in': (D_IN // 4, D), 'mla_wq_b': (Q_RANK, 384), 'mla_wkv_b': (KV_RANK, 512),
              'rwkv_w2_f': (LORA, 256), 'rwkv_w2_b': (LORA, 256), 'rwkv_a2_f': (LORA, 256), 'rwkv_a2_b': (LORA, 256),
              'w_br_mla': (RW, 512), 'w_br_rwkv': (RW, 512), 'w_out': (512, D)}
BIG = list(BIG_SHAPES)
SMALL = [n for n in WEIGHTS if n not in BIG_SHAPES]
SMALL_SHAPES = {'g_pre': (D,), 'mla_q_norm': (Q_RANK,), 'mla_kv_norm': (KV_RANK,), 'rwkv_mu': (3456,),
                'rwkv_w0_f': (RW,), 'rwkv_w0_b': (RW,), 'rwkv_a0_f': (RW,), 'rwkv_a0_b': (RW,), 'rwkv_k_k': (RW,),
                'rwkv_k_a': (RW,), 'rwkv_r_k': (RH, RN), 'rwkv_gn_g': (RW,), 'rwkv_gn_b': (RW,), 'g_post': (D,)}
SMALL_LEN = sum(int(np.prod(s)) for s in SMALL_SHAPES.values())
SMALL_ROWS = 144


UNITS = [('w_in',), ('mla_wq_b',), ('mla_wkv_b',), ('rwkv_w2_f', 'rwkv_w2_b', 'rwkv_a2_f', 'rwkv_a2_b'),
         ('w_br_mla', 'w_br_rwkv'), ('w_out',)]
UNIT_AXIS = [1, 0, 0, 0, 0, 0]
ROW_SHARDED = ('w_in', 'w_out')


def _unit_cat(parts):
    return parts[0] if len(parts) == 1 else jnp.concatenate(parts, axis=0)


def _unit_split(arr, names, axis):
    out, o = {}, 0
    for n in names:
        rows = BIG_SHAPES[n][0]
        out[n] = lax.slice_in_dim(arr, o, o + rows, axis=axis)
        o += rows
    return out


def _gathered(ag, own, me):
    out = {}
    for names, arr, mine in zip(UNITS, ag, own):
        slots = [jnp.where(me == j, mine, arr[j]) for j in range(4)]
        for n in names:
            parts = [_unit_split(s, names, 0)[n] for s in slots]
            out[n] = jnp.concatenate(parts, axis=0 if n in ROW_SHARDED else 1)
    return out


def _shards(n, g):
    r, w = BIG_SHAPES[n]
    if n in ROW_SHARDED:
        return [g[j * r:(j + 1) * r] for j in range(4)]
    return [g[:, j * w:(j + 1) * w] for j in range(4)]


def _pack_small(d, extra=None):
    flat = jnp.concatenate([d[n].reshape(-1) for n in SMALL] + ([extra.reshape(-1)] if extra is not None else []))
    return jnp.pad(flat, (0, SMALL_ROWS * LANE - flat.shape[0])).reshape(SMALL_ROWS, LANE)


def _unpack_small(packed):
    flat, out, o = packed.reshape(-1), {}, 0
    for n in SMALL:
        sz = int(np.prod(SMALL_SHAPES[n]))
        out[n] = flat[o:o + sz].reshape(SMALL_SHAPES[n])
        o += sz
    return out


def _perm_w_in(w):
    z = lambda n: jnp.zeros((n, w.shape[1]), w.dtype)
    lora = []
    for i in range(4):
        lora += [w[4160 + LORA * i:4160 + LORA * (i + 1)], z(LANE - LORA)]
    return jnp.concatenate([w[0:1024], w[1088:4160], w[4544:D_IN]] + lora
                           + [w[1024:1056], z(96), w[1056:1088], z(96)], axis=0)


def _unperm_w_in(g):
    lora = [g[OFF_LORA + LANE * i:OFF_LORA + LANE * i + LORA] for i in range(4)]
    return jnp.concatenate([g[0:1024], g[OFF_KR:OFF_KR + 32], g[OFF_KR + LANE:OFF_KR + LANE + 32],
                            g[1024:4096]] + lora + [g[4096:OFF_LORA]], axis=0)


def _perm_wq(w):
    w3 = w.reshape(Q_RANK, HEADS, NOPE + ROPE)
    return jnp.concatenate([w3[:, :, :NOPE].reshape(Q_RANK, -1), w3[:, :, NOPE:NOPE + 32].reshape(Q_RANK, -1),
                            w3[:, :, NOPE + 32:].reshape(Q_RANK, -1)], axis=1)


def _unperm_wq(g):
    return jnp.concatenate([g[:, :1024].reshape(Q_RANK, HEADS, NOPE), g[:, 1024:1280].reshape(Q_RANK, HEADS, 32),
                            g[:, 1280:].reshape(Q_RANK, HEADS, 32)], axis=2).reshape(Q_RANK, -1)


def _perm_wkv(w):
    w3 = w.reshape(KV_RANK, HEADS, NOPE + VDIM)
    return jnp.concatenate([w3[:, :, :NOPE].reshape(KV_RANK, -1), w3[:, :, NOPE:].reshape(KV_RANK, -1)], axis=1)


def _unperm_wkv(g):
    return jnp.concatenate([g[:, :1024].reshape(KV_RANK, HEADS, NOPE), g[:, 1024:].reshape(KV_RANK, HEADS, VDIM)],
                           axis=2).reshape(KV_RANK, -1)


def _pad_rows(w):
    return jnp.pad(w, ((0, LANE - LORA), (0, 0)))


def _perm_mu(mu):
    parts = [mu[:3072]]
    for i in range(4):
        parts += [mu[3072 + LORA * i:3072 + LORA * (i + 1)], jnp.zeros((LANE - LORA,), mu.dtype)]
    return jnp.concatenate(parts).reshape(1, NLERP)


def _unperm_mu(g):
    g = g.reshape(-1)
    return jnp.concatenate([g[:3072]] + [g[3072 + LANE * i:3072 + LANE * i + LORA] for i in range(4)])


def _to_heads(t, n):
    return t.reshape(T, -1, n).transpose(1, 0, 2)


def _from_heads(t):
    return t.transpose(1, 0, 2).reshape(T, -1)


def _to_qk(n, r1, r2):
    return jnp.concatenate([n.reshape(T, HEADS, NOPE), r1.reshape(T, HEADS, 32), r2.reshape(T, HEADS, 32)],
                           axis=2).transpose(1, 0, 2)


def _from_qk(g):
    g = g.transpose(1, 0, 2)
    return g[:, :, :NOPE].reshape(T, -1), g[:, :, NOPE:NOPE + 32].reshape(T, -1), g[:, :, NOPE + 32:].reshape(T, -1)


def _constants():
    g2 = np.kron(np.eye(2, dtype=np.float32), np.ones((RN, RN), np.float32))
    e = np.zeros((LANE, 256), np.float32)
    for h in range(HEADS):
        e[np.arange(32), h * 32 + np.arange(32)] = 1.0
    pos = jnp.arange(T, dtype=F32)
    inv_freq = jnp.power(ROPE_THETA, -jnp.arange(0, ROPE, 2, dtype=F32) / ROPE)
    ang = pos[:, None] * inv_freq[None, :]
    cos, sin = jnp.cos(ang), jnp.sin(ang)
    padk = lambda t: jnp.pad(t, ((0, 0), (0, LANE - 32)))
    return (jnp.asarray(g2, BF16), jnp.asarray(e, BF16), jnp.tile(cos, (1, HEADS)), jnp.tile(sin, (1, HEADS)),
            padk(cos), padk(sin))


def _step(x, tgt, w, m, v):
    x2, tgt2 = x.reshape(T, D), tgt.reshape(T, D)
    g2, e_mat, cosq, sinq, cosk, sink = _constants()
    row = lambda n: w[n].reshape(1, -1)
    w, m, v = ({**t, 'w_in': t['w_in'].T} for t in (w, m, v))

    core, chip = lax.axis_index("c"), 2 * lax.axis_index("x") + lax.axis_index("y")
    place = jnp.stack([core, chip]).astype(jnp.int32)
    own_bf = [_unit_cat([w[n].astype(BF16) for n in u]) for u in UNITS]
    full = _gathered(gather_weights(own_bf, UNIT_AXIS), own_bf, chip)
    wp = _perm_w_in(full['w_in'])
    wq = _perm_wq(full['mla_wq_b']).astype(F32)
    wkv = _perm_wkv(full['mla_wkv_b']).astype(F32)
    lora_w = [_pad_rows(full[n]).astype(F32) for n in ('rwkv_w2_f', 'rwkv_w2_b', 'rwkv_a2_f', 'rwkv_a2_b')]
    mu_p = _perm_mu(w['rwkv_mu'])

    st_pre = Stage("pre", f_pre, [(D, BF16), (D, None)], 256, [0], [0], [F32])
    st_mla = Stage("mla", f_mla, [(1024, BF16), (256, BF16), (256, BF16), (1024, BF16), (256, BF16), (256, BF16),
                                  (1024, BF16)], 256, [0, 1, 2], [0, 1, 2, 3], [BF16] * 3)
    st_rpre = Stage("rwkv_pre", f_rwkv_pre, [(RW, F32)] * 9, 128, [0, 1, 2, 3], list(range(10)), [F32] * 4)
    st_rpost = Stage("rwkv_post", f_rwkv_post, [(RW, BF16)], 256, [0, 2, 3, 4, 5, 6], [0, 1, 2],
                     [F32, F32, F32, F32, F32, BF16])
    st_gate = Stage("gate", f_gate, [(RW, BF16)], 256, [0, 1], [], [F32, BF16])
    st_merge = Stage("merge", f_merge, [(D, BF16)], 256, [0, 1, 2, 3], [], [BF16] * 4)

    pre_rows, pre_par = [(x2, D, 0)], [row('g_pre')]
    (h,) = st_pre.fwd(pre_rows, pre_par)
    proj = matmul("mm_in", h, wp, "nt")

    mla_rows = [(proj, 512, OFF_QA // 512), (proj, 512, OFF_KVA // 512), (proj, 256, OFF_KR // 256),
                (cosq, 256, 0), (sinq, 256, 0), (cosk, LANE, 0), (sink, LANE, 0)]
    mla_par = [row('mla_q_norm'), row('mla_kv_norm'), wq, wkv, e_mat]
    qn, qr1, qr2, kn, kr1, kr2, vv = st_mla.fwd(mla_rows, mla_par)
    qh, kh, vh = _to_qk(qn, qr1, qr2), _to_qk(kn, kr1, kr2), _to_heads(vv, VDIM)
    y_mla = _from_heads(attn_fwd(qh, kh, vh))

    lerp = shift_fwd(proj, mu_p)
    rpre_rows = [(lerp, RW, 0), (lerp, RW, 1), (lerp, RW, 2), (lerp, 512, 6)]
    rpre_par = [row('rwkv_w0_f'), row('rwkv_w0_b'), row('rwkv_a0_f'), row('rwkv_a0_b'), row('rwkv_k_k'),
                row('rwkv_k_a')] + lora_w + [g2]
    r_, v_, lwf, lwb, kf, kb, an, bf_, bb_ = st_rpre.fwd(rpre_rows, rpre_par)
    fin = [r_, lwf, kf, v_, an, bf_]
    bin_ = [r_, lwb, kb, v_, an, bb_]
    yf, h0f = scan_fwd("scan_f", *fin, reverse=False)
    yb, h0b = scan_fwd("scan_b", *bin_, reverse=True)
    rpost_rows = [(yf, RW, 0), (yb, RW, 0), (r_, RW, 0), (kf, RW, 0), (kb, RW, 0),
                  (v_, RW, 0), (proj, RW, OFF_ZR // RW)]
    rpost_par = [row('rwkv_gn_g'), row('rwkv_gn_b'), row('rwkv_r_k'), g2]
    (gr,) = st_rpost.fwd(rpost_rows, rpost_par)
    gate_rows = [(y_mla, RW, 0), (proj, RW, OFF_ZM // RW)]
    (gm,) = st_gate.fwd(gate_rows, [])
    um = matmul("mm_br_mla", gm, full['w_br_mla'], "nn")
    ur = matmul("mm_br_rwkv", gr, full['w_br_rwkv'], "nn")
    merge_rows = [(um, D, 0), (ur, D, 0), (proj, D, OFF_GM // D), (proj, D, OFF_GR // D)]
    (merged,) = st_merge.fwd(merge_rows, [])
    out = matmul("mm_out", merged, full['w_out'], "nn")
    d_out, dy, dg_post, loss_blk = loss_stage(out, x2, tgt2, row('g_post'))

    gw = {'g_post': dg_post}
    d_merged = matmul("mm_out_dx", d_out, full['w_out'], "nt")
    gw['w_out'] = matmul("mm_out_dw", merged, d_out, "tn")
    (d_um, d_ur, d_gm, d_gr), _ = st_merge.bwd(merge_rows, [], [[(d_merged, D, 0)]])
    d_gmla = matmul("mm_br_mla_dx", d_um, full['w_br_mla'], "nt")
    gw['w_br_mla'] = matmul("mm_br_mla_dw", gm, d_um, "tn")
    d_grw = matmul("mm_br_rwkv_dx", d_ur, full['w_br_rwkv'], "nt")
    gw['w_br_rwkv'] = matmul("mm_br_rwkv_dw", gr, d_ur, "tn")
    (d_ymla, d_zm), _ = st_gate.bwd(gate_rows, [], [[(d_gmla, RW, 0)]])
    (d_y, d_r3, d_kf2, d_kb2, d_v3, d_zr), (gw['rwkv_gn_g'], gw['rwkv_gn_b'], d_rk) = st_rpost.bwd(
        rpost_rows, rpost_par, [[(d_grw, RW, 0)]])
    gw['rwkv_r_k'] = d_rk
    sf = scan_bwd("scan_f_bwd", *fin, h0f, d_y, reverse=False)
    sb = scan_bwd("scan_b_bwd", *bin_, h0b, d_y, reverse=True)
    c = lambda *ts: [(t, RW, 0) for t in ts]
    rpre_cts = [c(sf[0], sb[0], d_r3), c(sf[3], sb[3], d_v3), c(sf[1]), c(sb[1]), c(sf[2], d_kf2), c(sb[2], d_kb2),
                c(sf[4], sb[4]), c(sf[5]), c(sb[5])]
    d_lerp_rows, rpre_g = st_rpre.bwd(rpre_rows, rpre_par, rpre_cts)
    for n, gval in zip(('rwkv_w0_f', 'rwkv_w0_b', 'rwkv_a0_f', 'rwkv_a0_b', 'rwkv_k_k', 'rwkv_k_a'), rpre_g[:6]):
        gw[n] = gval
    for n, gval in zip(('rwkv_w2_f', 'rwkv_w2_b', 'rwkv_a2_f', 'rwkv_a2_b'), rpre_g[6:]):
        gw[n] = gval[:LORA]
    d_lerp, d_mu = shift_bwd(proj, mu_p, jnp.concatenate(d_lerp_rows, axis=1))
    gw['rwkv_mu'] = _unperm_mu(d_mu)

    dqh, dkh, dvh = attn_bwd(qh, kh, vh, _to_heads(d_ymla, VDIM))
    mla_cts = [[(t, t.shape[1], 0)] for t in (*_from_qk(dqh), *_from_qk(dkh), _from_heads(dvh))]
    (d_qa, d_kva, d_kr), (gw['mla_q_norm'], gw['mla_kv_norm'], d_wq, d_wkv) = st_mla.bwd(mla_rows, mla_par, mla_cts)
    gw['mla_wq_b'], gw['mla_wkv_b'] = _unperm_wq(d_wq), _unperm_wkv(d_wkv)

    dproj = jnp.concatenate([d_qa, d_kva, d_lerp[:, :3072], d_zm, d_zr, d_gm, d_gr, d_lerp[:, 3072:], d_kr], axis=1)
    dh = matmul("mm_in_dx", dproj, wp, "nn")
    gw['w_in'] = _unperm_w_in(matmul("mm_in_dw", dproj, h, "tn", BF16))
    (grad_x,), (gw['g_pre'],) = st_pre.bwd(pre_rows, pre_par, [[(dh, D, 0)], [(dy, D, 0)]])

    shards = {n: _shards(n, gw[n]) for n in BIG}
    send = [jnp.stack([_unit_cat([shards[n][j].astype(BF16) for n in u]) for j in range(4)]) for u in UNITS]
    other = pair_exchange(send, UNIT_AXIS)
    pairs = [pair_sum(f"pair_sum_{i}", place, s, o, ax) for i, (s, o, ax) in enumerate(zip(send, other, UNIT_AXIS))]
    recv = scatter_grads(pairs)
    mine = [sum4(f"sum4_{i}", place, r, p, ax) for i, (r, p, ax) in enumerate(zip(recv, pairs, UNIT_AXIS))]
    theirs = swap_halves(mine)
    big = [dict() for _ in range(4)]
    for i, u in enumerate(UNITS):
        res = adamw_halves(f"adamw_{i}", place, *[_unit_cat([t[n] for n in u]) for t in (w, m, v)], mine[i], theirs[i],
                           UNIT_AXIS[i])
        for q in range(4):
            big[q].update(_unit_split(res[q], u, 0))
    parts = allgather8("gather_small", _pack_small(gw, loss_blk[0, :1]))
    small = adamw("adamw_small", _pack_small(w), _pack_small(m), _pack_small(v), [parts[i] for i in range(8)])

    outs = []
    for b_d, s_arr in zip(big, small):
        d = {**b_d, **_unpack_small(s_arr)}
        d['w_in'] = d['w_in'].T
        outs.append([d[n] for n in WEIGHTS])
    loss = small[0][SMALL_LEN // LANE, 0]
    return (loss, grad_x.reshape(1, T, D), *outs[0], *outs[1], *outs[2], *outs[3])


def kernel(x, g_pre, w_in, mla_q_norm, mla_wq_b, mla_kv_norm, mla_wkv_b, rwkv_mu, rwkv_w0_f, rwkv_w2_f, rwkv_w0_b, rwkv_w2_b, rwkv_a0_f, rwkv_a2_f, rwkv_a0_b, rwkv_a2_b, rwkv_k_k, rwkv_k_a, rwkv_r_k, rwkv_gn_g, rwkv_gn_b, w_br_mla, w_br_rwkv, w_out, g_post, loss_target, m_g_pre, m_w_in, m_mla_q_norm, m_mla_wq_b, m_mla_kv_norm, m_mla_wkv_b, m_rwkv_mu, m_rwkv_w0_f, m_rwkv_w2_f, m_rwkv_w0_b, m_rwkv_w2_b, m_rwkv_a0_f, m_rwkv_a2_f, m_rwkv_a0_b, m_rwkv_a2_b, m_rwkv_k_k, m_rwkv_k_a, m_rwkv_r_k, m_rwkv_gn_g, m_rwkv_gn_b, m_w_br_mla, m_w_br_rwkv, m_w_out, m_g_post, v_g_pre, v_w_in, v_mla_q_norm, v_mla_wq_b, v_mla_kv_norm, v_mla_wkv_b, v_rwkv_mu, v_rwkv_w0_f, v_rwkv_w2_f, v_rwkv_w0_b, v_rwkv_w2_b, v_rwkv_a0_f, v_rwkv_a2_f, v_rwkv_a0_b, v_rwkv_a2_b, v_rwkv_k_k, v_rwkv_k_a, v_rwkv_r_k, v_rwkv_gn_g, v_rwkv_gn_b, v_w_br_mla, v_w_br_rwkv, v_w_out, v_g_post):
    given = dict(locals())
    w = {n: given[n] for n in WEIGHTS}
    m = {n: given['m_' + n] for n in WEIGHTS}
    v = {n: given['v_' + n] for n in WEIGHTS}
    return _step(x, loss_target, w, m, v)
```

```python
import functools
import math

import numpy as np
import jax
import jax.numpy as jnp
from jax import lax
from jax.experimental import pallas as pl
from jax.experimental.pallas import tpu as pltpu

F32, BF16 = jnp.float32, jnp.bfloat16
MESH_IDS = pl.DeviceIdType.MESH

D = 2048
T = 2048
HEADS = 8
Q_RANK = 512
KV_RANK = 512
NOPE = 128
ROPE = 64
VDIM = 128
RW = 1024
RH = 16
RN = 64
LORA = 96
D_IN = 10688
NORM_EPS = 1e-6
GN_EPS = 64e-5
ROPE_THETA = 10000.0
ADAM_LR, ADAM_B1, ADAM_B2, ADAM_EPS, ADAM_WD, ADAM_STEP = 0.001, 0.9, 0.999, 1e-08, 0.01, 10

LANE = 128
VMEM_BIG = 56 * 2**20

NP = 11008
OFF_QA, OFF_KVA, OFF_RKV, OFF_ZM, OFF_ZR, OFF_GM, OFF_GR, OFF_LORA, OFF_KR = 0, 512, 1024, 4096, 5120, 6144, 8192, 10240, 10752
NLERP = 3584

CHUNK = 64
NCH = T // CHUNK


def _dg(a, b, ca, cb, batch=False, prec=None):
    bd = ((0,), (0,)) if batch else ((), ())
    return lax.dot_general(a, b, (((ca,), (cb,)), bd), precision=prec, preferred_element_type=F32)


@jax.custom_vjp
def bdot(a, b):
    return _dg(a.astype(BF16), b.astype(BF16), 1, 0)


def _bdot_fwd(a, b):
    return bdot(a, b), (a, b)


def _bdot_bwd(res, g):
    a, b = res
    gb = g.astype(BF16)
    da = _dg(gb, b.astype(BF16), 1, 1)
    db = _dg(a.astype(BF16), gb, 0, 0)
    return da.astype(a.dtype), db.astype(b.dtype)


bdot.defvjp(_bdot_fwd, _bdot_bwd)


def _split(x):
    hi = x.astype(BF16)
    lo = (x - hi.astype(F32)).astype(BF16)
    return hi, lo


@jax.custom_vjp
def gsum(x, g2):
    hi, lo = _split(x)
    return _dg(hi, g2, 1, 0) + _dg(lo, g2, 1, 0)


def _gsum_fwd(x, g2):
    return gsum(x, g2), g2


def _gsum_bwd(g2, g):
    hi, lo = _split(g)
    return _dg(hi, g2, 1, 1) + _dg(lo, g2, 1, 1), jnp.zeros_like(g2)


gsum.defvjp(_gsum_fwd, _gsum_bwd)


def headsum(x, g2):
    return jnp.concatenate([gsum(x[:, i * LANE:(i + 1) * LANE], g2) for i in range(x.shape[1] // LANE)], axis=1)


def _terms(x, n):
    out = []
    for i in range(n):
        t = x.astype(BF16)
        out.append(t)
        if i < n - 1:
            x = x - t.astype(F32)
    return out


def _bmm(a, b, ca, cb, na, nb):
    acc = None
    for i, ai in enumerate(_terms(a, na)):
        for j, bj in enumerate(_terms(b, nb)):
            if i + j < max(na, nb):
                p = _dg(ai, bj, ca, cb, True)
                acc = p if acc is None else acc + p
    return acc


_NN, _NT, _TN = (2, 1), (2, 2), (1, 1)


def _make_dots(nf, nb_nn, nb_nt, nb_tn):
    @jax.custom_vjp
    def nn(a, b):
        return _bmm(a, b, *_NN, nf, nf)

    @jax.custom_vjp
    def nt(a, b):
        return _bmm(a, b, *_NT, nf, nf)

    @jax.custom_vjp
    def tn(a, b):
        return _bmm(a, b, *_TN, nf, nf)

    nn.defvjp(lambda a, b: (nn(a, b), (a, b)),
              lambda r, g: (_bmm(g, r[1], *_NT, nb_nn, nb_nn), _bmm(r[0], g, *_TN, nb_nn, nb_nn)))
    nt.defvjp(lambda a, b: (nt(a, b), (a, b)),
              lambda r, g: (_bmm(g, r[1], *_NN, nb_nt, nb_nt), _bmm(g, r[0], *_TN, nb_nt, nb_nt)))
    tn.defvjp(lambda a, b: (tn(a, b), (a, b)),
              lambda r, g: (_bmm(r[1], g, *_NT, nb_tn, nb_tn), _bmm(r[0], g, *_NN, nb_tn, nb_tn)))
    return nn, nt, tn


_SCAN_NF, _SCAN_NB = 1, 1
nn, nt, tn = _make_dots(_SCAN_NF, 1, 2, 1)


@jax.custom_vjp
def cumdot(ones, x):
    return _bmm(ones, x, *_NN, 1, 3)


cumdot.defvjp(lambda o, x: (cumdot(o, x), o), lambda o, g: (jnp.zeros_like(o), _bmm(o, g, *_TN, 1, 3)))


def _solve_powers(l):
    pw = [l]
    for _ in range(int(math.log2(l.shape[-1])) - 1):
        pw.append(_bmm(pw[-1], pw[-1], *_NN, _SCAN_NF, _SCAN_NF))
    return pw


@jax.custom_vjp
def tri_solve(l, rhs):
    x = rhs
    for p in _solve_powers(l):
        x = x + _bmm(p, x, *_NN, _SCAN_NF, _SCAN_NF)
    return x


def _tri_solve_fwd(l, rhs):
    pw = _solve_powers(l)
    x = rhs
    for p in pw:
        x = x + _bmm(p, x, *_NN, _SCAN_NF, _SCAN_NF)
    return x, (pw, x)


def _tri_solve_bwd(res, g):
    pw, x = res
    y = g
    for p in pw:
        y = y + _bmm(p, y, *_TN, _SCAN_NB, _SCAN_NB)
    return _bmm(y, x, *_NT, _SCAN_NB, _SCAN_NB), y


tri_solve.defvjp(_tri_solve_fwd, _tri_solve_bwd)


def _rms(x, g):
    return x * lax.rsqrt(jnp.mean(x * x, axis=-1, keepdims=True) + NORM_EPS) * g


def _softplus(x):
    pos = x > 0
    return jnp.where(pos, x, 0.0) + jnp.log(1.0 + jnp.exp(-jnp.where(pos, x, -x)))


def _silu(z):
    return z * jax.nn.sigmoid(z)


def _tile(n, cands):
    for c in cands:
        if n % c == 0:
            return c
    raise ValueError(n)


_MM_VMEM_BYTES = 32 * 2**20


def _mm_tiles(m, n, k):
    best = None
    for tm in (2048, 1024, 512, 256):
        for tn_ in (2048, 1024, 512, 256):
            for d in range(k // LANE, 0, -1):
                tk = LANE * d
                if m % tm or n % tn_ or k % tk:
                    continue
                nk = k // tk
                vmem = 4 * tk * (tm + tn_) + 8 * tm * tn_ + (4 * tm * tn_ if nk > 1 else 0)
                if vmem > _MM_VMEM_BYTES:
                    continue
                a_reads = n // tn_ if nk > 1 else 1
                b_reads = 1 if (nk == 1 and n == tn_) else m // tm
                acc_rmw = nk * m * n if nk > 1 else 0
                cost = (a_reads * m * k + b_reads * k * n + acc_rmw, -tm * tn_ * tk)
                if best is None or cost < best[0]:
                    best = (cost, (tm, tn_, tk))
    return best[1]


class Side:
    def __init__(self, ins, outs, sems, start, finish):
        self.ins, self.outs, self.sems, self.start, self.finish = ins, outs, sems, start, finish


def matmul(name, a, b, mode, out_dtype=F32, side=None):
    if mode == "nn":
        (m, k), n = a.shape, b.shape[1]
    elif mode == "nt":
        (m, k), n = a.shape, b.shape[0]
    else:
        (k, m), n = a.shape, b.shape[1]
    tm, tn_, tk = _mm_tiles(m, n, k)
    nk = k // tk
    if mode == "nn":
        a_spec = pl.BlockSpec((tm, tk), lambda i, j, kk: (i, kk))
        b_spec = pl.BlockSpec((tk, tn_), lambda i, j, kk: (kk, j))
        ca, cb = 1, 0
    elif mode == "nt":
        a_spec = pl.BlockSpec((tm, tk), lambda i, j, kk: (i, kk))
        b_spec = pl.BlockSpec((tn_, tk), lambda i, j, kk: (j, kk))
        ca, cb = 1, 1
    else:
        a_spec = pl.BlockSpec((tk, tm), lambda i, j, kk: (kk, i))
        b_spec = pl.BlockSpec((tk, tn_), lambda i, j, kk: (kk, j))
        ca, cb = 0, 0

    grid = (m // tm, n // tn_, nk)
    n_in = len(side.ins) if side else 0
    n_out = len(side.outs) if side else 0

    def body(a_ref, b_ref, *rest):
        s_ins, o_ref, s_outs = rest[:n_in], rest[n_in], rest[n_in + 1:n_in + 1 + n_out]
        scratch = rest[n_in + 1 + n_out:]
        acc, s_sems = (scratch[:1], scratch[1:]) if nk > 1 else ((), scratch)
        if side:
            step = (pl.program_id(0) * grid[1] + pl.program_id(1)) * grid[2] + pl.program_id(2)

            @pl.when(step == 0)
            def _():
                side.start(s_ins, s_outs, s_sems)

        part = _dg(a_ref[...].astype(BF16), b_ref[...].astype(BF16), ca, cb)
        if nk == 1:
            o_ref[...] = part.astype(o_ref.dtype)
        else:
            acc_ref, kk = acc[0], pl.program_id(2)

            @pl.when(kk == 0)
            def _():
                acc_ref[...] = part

            @pl.when(kk > 0)
            def _():
                acc_ref[...] += part

            @pl.when(kk == nk - 1)
            def _():
                o_ref[...] = acc_ref[...].astype(o_ref.dtype)

        if side:
            @pl.when(step == grid[0] * grid[1] * grid[2] - 1)
            def _():
                side.finish(s_ins, s_outs, s_sems)

    res = pl.pallas_call(
        body, name=name, grid=grid,
        in_specs=[a_spec, b_spec] + [_ANY] * n_in,
        out_specs=[pl.BlockSpec((tm, tn_), lambda i, j, kk: (i, j))] + [_ANY] * n_out,
        out_shape=[jax.ShapeDtypeStruct((m, n), out_dtype)] + (list(side.outs) if side else []),
        scratch_shapes=([pltpu.VMEM((tm, tn_), F32)] if nk > 1 else []) + (list(side.sems) if side else []),
        compiler_params=pltpu.CompilerParams(
            dimension_semantics=("arbitrary",) * 3 if side else ("parallel", "parallel", "arbitrary"),
            vmem_limit_bytes=VMEM_BIG),
    )(a, b, *(side.ins if side else []))
    return (res[0], res[1:]) if side else res[0]


def _rspec(tr, width, blk):
    return pl.BlockSpec((tr, width), lambda i: (i, blk))


def _full_spec(arr):
    return pl.BlockSpec(arr.shape, lambda i: (0,) * arr.ndim)


class Stage:
    def __init__(self, name, f, outs, tr, diff_rows, diff_params, drow_dtypes):
        self.name, self.f, self.outs, self.tr = name, f, outs, tr
        self.diff_rows, self.diff_params, self.drow_dtypes = diff_rows, diff_params, drow_dtypes

    def fwd(self, rows, params):
        f, nr, npar = self.f, len(rows), len(params)
        stored = [(w, dt) for (w, dt) in self.outs if dt is not None]
        keep = [i for i, (w, dt) in enumerate(self.outs) if dt is not None]

        def body(*refs):
            vals = f(*[r[...].astype(F32) for r in refs[:nr]], *[p[...] for p in refs[nr:nr + npar]])
            for o_ref, i in zip(refs[nr + npar:], keep):
                o_ref[...] = vals[i].astype(o_ref.dtype)

        return pl.pallas_call(
            body, name=self.name + "_fwd", grid=(T // self.tr,),
            in_specs=[_rspec(self.tr, w, b) for (_, w, b) in rows] + [_full_spec(p) for p in params],
            out_specs=[_rspec(self.tr, w, 0) for (w, _) in stored],
            out_shape=[jax.ShapeDtypeStruct((T, w), dt) for (w, dt) in stored],
            compiler_params=pltpu.CompilerParams(dimension_semantics=("arbitrary",), vmem_limit_bytes=VMEM_BIG),
        )(*[r[0] for r in rows], *params)

    def bwd(self, rows, params, cts):
        f, nr, npar = self.f, len(rows), len(params)
        dr_idx, dp_idx = self.diff_rows, self.diff_params
        flat_cts = [c for lst in cts for c in lst]
        nct = len(flat_cts)

        def body(*refs):
            row_refs, par_refs = refs[:nr], refs[nr:nr + npar]
            ct_refs = refs[nr + npar:nr + npar + nct]
            drow_refs = refs[nr + npar + nct:nr + npar + nct + len(dr_idx)]
            dpar_refs = refs[nr + npar + nct + len(dr_idx):]
            row_vals = [r[...].astype(F32) for r in row_refs]
            par_vals = [p[...] for p in par_refs]

            def g(*dv):
                rv, pv = list(row_vals), list(par_vals)
                for j, i in enumerate(dr_idx):
                    rv[i] = dv[j]
                for j, i in enumerate(dp_idx):
                    pv[i] = dv[len(dr_idx) + j]
                return f(*rv, *pv)

            _, vjp = jax.vjp(g, *[row_vals[i] for i in dr_idx], *[par_vals[i] for i in dp_idx])
            ct_vals, pos = [], 0
            for lst in cts:
                acc = ct_refs[pos][...].astype(F32)
                for q in range(1, len(lst)):
                    acc = acc + ct_refs[pos + q][...].astype(F32)
                pos += len(lst)
                ct_vals.append(acc)
            grads = vjp(tuple(ct_vals))
            for j, r in enumerate(drow_refs):
                r[...] = grads[j].astype(r.dtype)

            @pl.when(pl.program_id(0) == 0)
            def _():
                for r in dpar_refs:
                    r[...] = jnp.zeros_like(r)

            for j, r in enumerate(dpar_refs):
                r[...] += grads[len(dr_idx) + j].astype(F32)

        drow_shapes = [jax.ShapeDtypeStruct((T, rows[i][1]), dt) for i, dt in zip(dr_idx, self.drow_dtypes)]
        dpar_shapes = [jax.ShapeDtypeStruct(params[i].shape, F32) for i in dp_idx]
        res = pl.pallas_call(
            body, name=self.name + "_bwd", grid=(T // self.tr,),
            in_specs=[_rspec(self.tr, w, b) for (_, w, b) in rows] + [_full_spec(p) for p in params]
            + [_rspec(self.tr, w, b) for (_, w, b) in flat_cts],
            out_specs=[_rspec(self.tr, rows[i][1], 0) for i in dr_idx] + [_full_spec(params[i]) for i in dp_idx],
            out_shape=drow_shapes + dpar_shapes,
            compiler_params=pltpu.CompilerParams(dimension_semantics=("arbitrary",), vmem_limit_bytes=VMEM_BIG),
        )(*[r[0] for r in rows], *params, *[c[0] for c in flat_cts])
        return res[:len(dr_idx)], res[len(dr_idx):]


def f_pre(x, g):
    return _rms(x, g), x


def f_mla(q_a, kv_a, kr, cosq, sinq, cosk, sink, gq, gkv, wq, wkv, e):
    q = bdot(_rms(q_a, gq), wq)
    kv = bdot(_rms(kv_a, gkv), wkv)
    t1, t2 = q[:, 1024:1280], q[:, 1280:1536]
    k1, k2 = kr[:, :LANE], kr[:, LANE:]
    kr1 = k1 * cosk - k2 * sink
    kr2 = k1 * sink + k2 * cosk
    return (q[:, :1024], t1 * cosq - t2 * sinq, t1 * sinq + t2 * cosq,
            kv[:, :1024], bdot(kr1, e), bdot(kr2, e), kv[:, 1024:])


def f_rwkv_pre(r, k, v, lora, w0f, w0b, a0f, a0b, kkw, kaw, w2f, w2b, a2f, a2b, g2):
    wdf, wdb, adf, adb = (lora[:, i * LANE:(i + 1) * LANE] for i in range(4))

    def logdecay(w0, wd, w2):
        z = w0 + bdot(jnp.tanh(wd), w2)
        return -jnp.exp(-_softplus(-z) - 0.5)

    a_f = jax.nn.sigmoid(a0f + bdot(adf, a2f))
    a_b = jax.nn.sigmoid(a0b + bdot(adb, a2b))
    kk = k * kkw
    kk = kk / jnp.maximum(jnp.sqrt(headsum(kk * kk, g2)), 1e-12)
    return (r, v, logdecay(w0f, wdf, w2f), logdecay(w0b, wdb, w2b),
            k * (1.0 + (a_f - 1.0) * kaw), k * (1.0 + (a_b - 1.0) * kaw), -kk, kk * a_f, kk * a_b)


def f_rwkv_post(yf, yb, r, kf, kb, v, z, gng, gnb, rk, g2):
    y = yf + yb
    mu = headsum(y, g2) * (1.0 / RN)
    d = y - mu
    var = headsum(d * d, g2) * (1.0 / RN)
    yn = d * lax.rsqrt(var + GN_EPS) * gng + gnb
    bonus = headsum(r * (kf + kb) * rk, g2) * v
    return ((yn + bonus) * _silu(z),)


def f_gate(y, z):
    return (y * _silu(z),)


def f_merge(um, ur, gm, gr):
    return (jax.nn.sigmoid(gm) * um + jax.nn.sigmoid(gr) * ur,)


_SHIFT_W = 256


def _lerp_colblock(j):
    return jnp.where(j < 3072 // _SHIFT_W, OFF_RKV // _SHIFT_W + j, OFF_LORA // _SHIFT_W + j - 3072 // _SHIFT_W)


def _nbr_mean(x):
    row = lax.broadcasted_iota(jnp.int32, x.shape, 0)
    up = jnp.where(row == 0, 0.0, pltpu.roll(x, 1, 0))
    dn = jnp.where(row == T - 1, 0.0, pltpu.roll(x, T - 1, 0))
    return 0.5 * (up + dn)


def shift_fwd(proj, mu):
    def body(x_ref, mu_ref, o_ref):
        x = x_ref[...]
        o_ref[...] = x + mu_ref[...] * (_nbr_mean(x) - x)

    return pl.pallas_call(
        body, name="shift_fwd", grid=(NLERP // _SHIFT_W,),
        in_specs=[pl.BlockSpec((T, _SHIFT_W), lambda j: (0, _lerp_colblock(j))),
                  pl.BlockSpec((1, _SHIFT_W), lambda j: (0, j))],
        out_specs=pl.BlockSpec((T, _SHIFT_W), lambda j: (0, j)),
        out_shape=jax.ShapeDtypeStruct((T, NLERP), F32),
        compiler_params=pltpu.CompilerParams(dimension_semantics=("parallel",), vmem_limit_bytes=VMEM_BIG),
    )(proj, mu)


def shift_bwd(proj, mu, g):
    def body(x_ref, mu_ref, g_ref, dx_ref, dmu_ref):
        x, gv = x_ref[...], g_ref[...]
        dmu_ref[...] = jnp.sum(gv * (_nbr_mean(x) - x), axis=0, keepdims=True)
        gm = gv * mu_ref[...]
        dx_ref[...] = (gv - gm + _nbr_mean(gm)).astype(dx_ref.dtype)

    col = pl.BlockSpec((T, _SHIFT_W), lambda j: (0, j))
    vec = pl.BlockSpec((1, _SHIFT_W), lambda j: (0, j))
    return pl.pallas_call(
        body, name="shift_bwd", grid=(NLERP // _SHIFT_W,),
        in_specs=[pl.BlockSpec((T, _SHIFT_W), lambda j: (0, _lerp_colblock(j))), vec, col],
        out_specs=[col, vec],
        out_shape=[jax.ShapeDtypeStruct((T, NLERP), BF16), jax.ShapeDtypeStruct((1, NLERP), F32)],
        compiler_params=pltpu.CompilerParams(dimension_semantics=("parallel",), vmem_limit_bytes=VMEM_BIG),
    )(proj, mu, g)


_TQ = 256
_ATT_SCALE = (NOPE + ROPE) ** -0.5


def _probs(q, k):
    s = _dg(q, k, 1, 1) * _ATT_SCALE
    e = jnp.exp(s - jnp.max(s, axis=-1, keepdims=True))
    return e * (1.0 / jnp.sum(e, axis=-1, keepdims=True))


def attn_fwd(q, k, v):
    def body(q_ref, k_ref, v_ref, o_ref):
        p = _probs(q_ref[0], k_ref[0])
        o_ref[0] = _dg(p.astype(BF16), v_ref[0], 1, 0)

    dq = NOPE + ROPE
    return pl.pallas_call(
        body, name="attn_fwd", grid=(HEADS, T // _TQ),
        in_specs=[pl.BlockSpec((1, _TQ, dq), lambda h, i: (h, i, 0)),
                  pl.BlockSpec((1, T, dq), lambda h, i: (h, 0, 0)),
                  pl.BlockSpec((1, T, VDIM), lambda h, i: (h, 0, 0))],
        out_specs=pl.BlockSpec((1, _TQ, VDIM), lambda h, i: (h, i, 0)),
        out_shape=jax.ShapeDtypeStruct((HEADS, T, VDIM), F32),
        compiler_params=pltpu.CompilerParams(dimension_semantics=("parallel", "arbitrary"), vmem_limit_bytes=VMEM_BIG),
    )(q, k, v)


def attn_bwd(q, k, v, do):
    def body(q_ref, k_ref, v_ref, do_ref, dq_ref, dk_ref, dv_ref):
        @pl.when(pl.program_id(1) == 0)
        def _():
            dk_ref[...] = jnp.zeros_like(dk_ref)
            dv_ref[...] = jnp.zeros_like(dv_ref)

        qv, kv_, vv = q_ref[0], k_ref[0], v_ref[0]
        dob = do_ref[0].astype(BF16)
        p = _probs(qv, kv_)
        dv_ref[0] += _dg(p.astype(BF16), dob, 0, 0)
        dp = _dg(dob, vv, 1, 1)
        ds = (p * (dp - jnp.sum(dp * p, axis=-1, keepdims=True)) * _ATT_SCALE).astype(BF16)
        dq_ref[0] = _dg(ds, kv_, 1, 0)
        dk_ref[0] += _dg(ds, qv, 0, 0)

    dq = NOPE + ROPE
    return pl.pallas_call(
        body, name="attn_bwd", grid=(HEADS, T // _TQ),
        in_specs=[pl.BlockSpec((1, _TQ, dq), lambda h, i: (h, i, 0)),
                  pl.BlockSpec((1, T, dq), lambda h, i: (h, 0, 0)),
                  pl.BlockSpec((1, T, VDIM), lambda h, i: (h, 0, 0)),
                  pl.BlockSpec((1, _TQ, VDIM), lambda h, i: (h, i, 0))],
        out_specs=[pl.BlockSpec((1, _TQ, dq), lambda h, i: (h, i, 0)),
                   pl.BlockSpec((1, T, dq), lambda h, i: (h, 0, 0)),
                   pl.BlockSpec((1, T, VDIM), lambda h, i: (h, 0, 0))],
        out_shape=[jax.ShapeDtypeStruct((HEADS, T, dq), F32), jax.ShapeDtypeStruct((HEADS, T, dq), F32),
                   jax.ShapeDtypeStruct((HEADS, T, VDIM), F32)],
        compiler_params=pltpu.CompilerParams(dimension_semantics=("parallel", "arbitrary"), vmem_limit_bytes=VMEM_BIG),
    )(q, k, v, do)


def _chunk(r, lw, k, v, a, b, ht, *, reverse):
    hb, c, _ = r.shape
    ti = lax.broadcasted_iota(jnp.int32, (c, c), 0)
    si = lax.broadcasted_iota(jnp.int32, (c, c), 1)
    incl = (si >= ti) if reverse else (si <= ti)
    strict = (si > ti) if reverse else (si < ti)
    ones = jnp.broadcast_to(incl.astype(F32)[None], (hb, c, c))
    cum = cumdot(ones, lw)
    cum_ex = cum - lw
    tot = jnp.sum(lw, axis=1, keepdims=True)
    mid = 0.5 * tot
    rt, at = r * jnp.exp(cum - mid), a * jnp.exp(cum_ex - mid)
    einv = jnp.exp(mid - cum)
    kt, bt = k * einv, b * einv
    m_ab = jnp.where(strict, nt(at, bt), 0.0)
    m_ak = jnp.where(strict, nt(at, kt), 0.0)
    m_rb = jnp.where(incl, nt(rt, bt), 0.0)
    m_rk = jnp.where(incl, nt(rt, kt), 0.0)
    u = tri_solve(m_ab, nt(a * jnp.exp(cum_ex), ht) + nn(m_ak, v))
    y = nt(r * jnp.exp(cum), ht) + nn(m_rb, u) + nn(m_rk, v)
    eend = jnp.exp(tot - cum)
    ht_new = ht * jnp.exp(tot) + tn(u, b * eend) + tn(v, k * eend)
    return y, ht_new


_HB_F, _HB_B = 16, 16


def _split_heads(x):
    return jnp.stack([x[:, i * RN:(i + 1) * RN] for i in range(x.shape[1] // RN)])


def _merge_heads(y):
    return jnp.concatenate([y[i] for i in range(y.shape[0])], axis=1)


def _chunk_map(reverse, backward):
    flip = reverse != backward
    return (lambda g, c: (NCH - 1 - c, g)) if flip else (lambda g, c: (c, g))


def scan_fwd(name, r, lw, k, v, a, b, reverse):
    hb = _HB_F
    cmap = _chunk_map(reverse, False)

    def body(r_ref, lw_ref, k_ref, v_ref, a_ref, b_ref, y_ref, h0_ref, ht_ref):
        @pl.when(pl.program_id(1) == 0)
        def _():
            ht_ref[...] = jnp.zeros_like(ht_ref)

        ht = ht_ref[...]
        h0_ref[0] = ht
        ins = [_split_heads(x[...]) for x in (r_ref, lw_ref, k_ref, v_ref, a_ref, b_ref)]
        y, hn = _chunk(*ins, ht, reverse=reverse)
        y_ref[...] = _merge_heads(y)
        ht_ref[...] = hn

    io = pl.BlockSpec((CHUNK, hb * RN), cmap)
    return pl.pallas_call(
        body, name=name, grid=(RH // hb, NCH),
        in_specs=[io] * 6,
        out_specs=[io, pl.BlockSpec((1, hb, RN, RN), lambda g, c: (cmap(g, c)[0], g, 0, 0))],
        out_shape=[jax.ShapeDtypeStruct((T, RW), F32), jax.ShapeDtypeStruct((NCH, RH, RN, RN), F32)],
        scratch_shapes=[pltpu.VMEM((hb, RN, RN), F32)],
        compiler_params=pltpu.CompilerParams(dimension_semantics=("parallel", "arbitrary"), vmem_limit_bytes=VMEM_BIG),
    )(r, lw, k, v, a, b)


def scan_bwd(name, r, lw, k, v, a, b, h0, dy, reverse):
    hb = _HB_B
    cmap = _chunk_map(reverse, True)

    def body(r_ref, lw_ref, k_ref, v_ref, a_ref, b_ref, h0_ref, dy_ref, *rest):
        d_refs, dht_ref = rest[:6], rest[6]

        @pl.when(pl.program_id(1) == 0)
        def _():
            dht_ref[...] = jnp.zeros_like(dht_ref)

        ins = [_split_heads(x[...]) for x in (r_ref, lw_ref, k_ref, v_ref, a_ref, b_ref)]
        _, vjp = jax.vjp(functools.partial(_chunk, reverse=reverse), *ins, h0_ref[0])
        grads = vjp((_split_heads(dy_ref[...]), dht_ref[...]))
        for d_ref, gval in zip(d_refs, grads[:6]):
            d_ref[...] = _merge_heads(gval)
        dht_ref[...] = grads[6]

    io = pl.BlockSpec((CHUNK, hb * RN), cmap)
    return pl.pallas_call(
        body, name=name, grid=(RH // hb, NCH),
        in_specs=[io] * 6 + [pl.BlockSpec((1, hb, RN, RN), lambda g, c: (cmap(g, c)[0], g, 0, 0)), io],
        out_specs=[io] * 6,
        out_shape=[jax.ShapeDtypeStruct((T, RW), F32)] * 6,
        scratch_shapes=[pltpu.VMEM((hb, RN, RN), F32)],
        compiler_params=pltpu.CompilerParams(dimension_semantics=("parallel", "arbitrary"), vmem_limit_bytes=VMEM_BIG),
    )(r, lw, k, v, a, b, h0, dy)


def loss_stage(out, x2, tgt, g_post):
    tr = 256

    def body(o_ref, x_ref, t_ref, g_ref, do_ref, dy_ref, dg_ref, loss_ref):
        @pl.when(pl.program_id(0) == 0)
        def _():
            dg_ref[...] = jnp.zeros_like(dg_ref)
            loss_ref[...] = jnp.zeros_like(loss_ref)

        nrm, vjp = jax.vjp(_rms, o_ref[...], g_ref[...])
        e = x_ref[...] + nrm - t_ref[...]
        s = jnp.sum(jnp.sum(e * e, axis=1, keepdims=True), axis=0, keepdims=True)
        loss_ref[...] += jnp.broadcast_to(s * (0.5 / D), loss_ref.shape)
        dy = e * (1.0 / D)
        do, dg = vjp(dy)
        do_ref[...] = do.astype(do_ref.dtype)
        dy_ref[...] = dy
        dg_ref[...] += dg

    row = pl.BlockSpec((tr, D), lambda i: (i, 0))
    return pl.pallas_call(
        body, name="loss_stage", grid=(T // tr,),
        in_specs=[row, row, row, pl.BlockSpec((1, D), lambda i: (0, 0))],
        out_specs=[row, row, pl.BlockSpec((1, D), lambda i: (0, 0)), pl.BlockSpec((8, LANE), lambda i: (0, 0))],
        out_shape=[jax.ShapeDtypeStruct((T, D), BF16), jax.ShapeDtypeStruct((T, D), F32),
                   jax.ShapeDtypeStruct((1, D), F32), jax.ShapeDtypeStruct((8, LANE), F32)],
        compiler_params=pltpu.CompilerParams(dimension_semantics=("arbitrary",), vmem_limit_bytes=VMEM_BIG),
    )(out, x2, tgt, g_post)


_EW_BLOCK_BYTES = 1 << 20


def _row_tile(rows, cols):
    best = None
    for tr in range(16, rows + 1, 16):
        if rows % tr == 0 and tr * cols * 4 <= _EW_BLOCK_BYTES:
            best = tr
    return best or rows


def _axis_tile(shape, axis, words):
    rows, cols = shape
    n, other, unit = (rows, cols, 16) if axis == 0 else (cols, rows, LANE)
    best = unit if n % unit == 0 else n
    for t in range(unit, n + 1, unit):
        if n % t == 0 and t * other * words * 4 <= _EW_BLOCK_BYTES:
            best = t
    blk = (best, cols) if axis == 0 else (rows, best)
    at = (lambda s: (s, 0)) if axis == 0 else (lambda s: (0, s))
    return blk, n // best, at


def _adamw_update(g, w_ref, m_ref, v_ref, g_ref, d_ref, nm_ref, nv_ref):
    mm = ADAM_B1 * m_ref[...] + (1.0 - ADAM_B1) * g
    vv = ADAM_B2 * v_ref[...] + (1.0 - ADAM_B2) * (g * g)
    m_hat = mm / (1.0 - ADAM_B1 ** ADAM_STEP)
    v_hat = vv / (1.0 - ADAM_B2 ** ADAM_STEP)
    g_ref[...] = g
    d_ref[...] = -ADAM_LR * (m_hat / (jnp.sqrt(v_hat) + ADAM_EPS) + ADAM_WD * w_ref[...])
    nm_ref[...] = mm
    nv_ref[...] = vv


def adamw(name, w, m, v, parts):
    rows, cols = w.shape
    br = _row_tile(rows, cols)
    npart = len(parts)

    def body(w_ref, m_ref, v_ref, *rest):
        g = rest[0][...].astype(F32)
        for p in rest[1:npart]:
            g = g + p[...].astype(F32)
        _adamw_update(g, w_ref, m_ref, v_ref, *rest[npart:])

    blk = pl.BlockSpec((br, cols), lambda i: (i, 0))
    return pl.pallas_call(
        body, name=name, grid=(rows // br,),
        in_specs=[blk] * (3 + npart), out_specs=[blk] * 4,
        out_shape=[jax.ShapeDtypeStruct((rows, cols), F32)] * 4,
        compiler_params=pltpu.CompilerParams(dimension_semantics=("parallel",), vmem_limit_bytes=VMEM_BIG),
    )(w, m, v, *parts)


def adamw_halves(name, place, w, m, v, mine, theirs, axis):
    half_shape = mine.shape
    blk_shape, nb, at = _axis_tile(half_shape, axis, 1)

    def body(p_ref, w_ref, m_ref, v_ref, a_ref, b_ref, *outs):
        own = (pl.program_id(0) // nb) == p_ref[0]
        _adamw_update(jnp.where(own, a_ref[...], b_ref[...]), w_ref, m_ref, v_ref, *outs)

    blk = pl.BlockSpec(blk_shape, lambda i, p: at(i))
    half = pl.BlockSpec(blk_shape, lambda i, p: at(i % nb))
    return pl.pallas_call(
        body, name=name,
        grid_spec=pltpu.PrefetchScalarGridSpec(num_scalar_prefetch=1, grid=(2 * nb,),
                                               in_specs=[blk] * 3 + [half] * 2, out_specs=[blk] * 4),
        out_shape=[jax.ShapeDtypeStruct(w.shape, F32)] * 4,
        compiler_params=pltpu.CompilerParams(dimension_semantics=("arbitrary",), vmem_limit_bytes=VMEM_BIG),
    )(place, w, m, v, mine, theirs)


def pair_sum(name, place, send, other, axis):
    blk_shape, nb, at = _axis_tile(other.shape[1:], axis, 4)

    def body(p_ref, a_ref, b_ref, o_ref):
        o_ref[...] = (a_ref[...].astype(F32) + b_ref[...].astype(F32)).astype(o_ref.dtype)

    blk = pl.BlockSpec((4,) + blk_shape, lambda i, p: (0,) + at(i))
    mine = pl.BlockSpec((4,) + blk_shape, lambda i, p: (0,) + at(p[0] * nb + i))
    return pl.pallas_call(
        body, name=name,
        grid_spec=pltpu.PrefetchScalarGridSpec(num_scalar_prefetch=1, grid=(nb,), in_specs=[mine, blk], out_specs=blk),
        out_shape=jax.ShapeDtypeStruct(other.shape, BF16),
        compiler_params=pltpu.CompilerParams(dimension_semantics=("arbitrary",), vmem_limit_bytes=VMEM_BIG),
    )(place, send, other)


def sum4(name, place, recv, own, axis):
    blk_shape, nb, at = _axis_tile(recv.shape[1:], axis, 4)

    def body(p_ref, r_ref, s_ref, o_ref):
        me = p_ref[0]
        t = [jnp.where(me == j, s_ref[j], r_ref[j]).astype(F32) for j in range(4)]
        o_ref[...] = ((t[0] + t[1]) + t[2]) + t[3]

    blk = pl.BlockSpec((4,) + blk_shape, lambda i, p: (0,) + at(i))
    return pl.pallas_call(
        body, name=name,
        grid_spec=pltpu.PrefetchScalarGridSpec(num_scalar_prefetch=1, grid=(nb,), in_specs=[blk, blk],
                                               out_specs=pl.BlockSpec(blk_shape, lambda i, p: at(i))),
        out_shape=jax.ShapeDtypeStruct(recv.shape[1:], F32),
        compiler_params=pltpu.CompilerParams(dimension_semantics=("arbitrary",), vmem_limit_bytes=VMEM_BIG),
    )(place, recv, own)


_ANY = pl.BlockSpec(memory_space=pl.ANY)


def _place():
    x, y, c = lax.axis_index("x"), lax.axis_index("y"), lax.axis_index("c")
    return x, y, c, 2 * x + y


def _chip_peers(x, y):
    out = []
    for k in (1, 2, 3):
        px = 1 - x if k & 2 else x
        py = 1 - y if k & 1 else y
        out.append((k, px, py, 2 * px + py))
    return out


def _half(c, shape, axis):
    n = shape[axis] // 2
    sl = pl.ds(pl.multiple_of(c * n, 16 if axis == 0 else LANE), n)
    return (sl,) if axis == 0 else (pl.ds(0, shape[0]), sl)


def gather_weights(srcs, axes):
    side = gather_side(srcs, axes)
    n = len(srcs)

    def body(*refs):
        ins, outs, sems = refs[:n], refs[n:2 * n], refs[2 * n:]
        side.start(ins, outs, sems)
        side.finish(ins, outs, sems)

    return pl.pallas_call(
        body, name="gather_weights", in_specs=[_ANY] * n, out_specs=[_ANY] * n,
        out_shape=side.outs, scratch_shapes=side.sems,
    )(*srcs)


def gather_side(srcs, axes):
    n = len(srcs)

    def copies(src, dst, sems, sends_only=False):
        ssem, rsem, fssem, frsem = sems
        x, y, c, me = _place()
        sib = (x, y, 1 - c)
        out = []
        for i in range(n):
            mine, other = _half(c, srcs[i].shape, axes[i]), _half(1 - c, srcs[i].shape, axes[i])
            for k, px, py, peer in _chip_peers(x, y):
                sems_k = dict(send_sem=ssem.at[i, k - 1], recv_sem=rsem.at[i, k - 1], device_id=(px, py, c),
                              device_id_type=MESH_IDS)
                fsems = dict(send_sem=fssem.at[i, k - 1], recv_sem=frsem.at[i, k - 1], device_id=sib,
                             device_id_type=MESH_IDS)
                got = dst[i].at[(peer,) + mine]
                snd = pltpu.make_async_remote_copy(src_ref=src[i].at[mine], dst_ref=dst[i].at[(me,) + mine], **sems_k)
                if sends_only:
                    out.append(snd)
                    continue
                out.append((
                    snd,
                    pltpu.make_async_remote_copy(src_ref=src[i].at[mine], dst_ref=got, **sems_k),
                    pltpu.make_async_remote_copy(src_ref=got, dst_ref=got, **fsems),
                    pltpu.make_async_remote_copy(src_ref=got, dst_ref=dst[i].at[(peer,) + other], **fsems)))
        return out

    def start(src, dst, sems):
        for snd in copies(src, dst, sems, sends_only=True):
            snd.start()

    def finish(src, dst, sems):
        cps = copies(src, dst, sems)
        for _, rcv, fwd, _ in cps:
            rcv.wait_recv()
            fwd.start()
        for snd, _, fwd, frcv in cps:
            frcv.wait_recv()
            snd.wait_send()
            fwd.wait_send()

    return Side(list(srcs), [jax.ShapeDtypeStruct((4,) + s.shape, s.dtype) for s in srcs],
                [pltpu.SemaphoreType.DMA((n, 3))] * 4, start, finish)


def pair_exchange(name, srcs, axes):
    n = len(srcs)

    def half_shape(s, axis):
        return (4, s.shape[1] // 2, s.shape[2]) if axis == 0 else (4, s.shape[1], s.shape[2] // 2)

    def body(*refs):
        src, other = refs[:n], refs[n:2 * n]
        ssem, rsem = refs[2 * n:]
        x, y, c, _ = _place()
        cps = []
        for i in range(n):
            idx = (pl.ds(0, 4),) + _half(1 - c, srcs[i].shape[1:], axes[i])
            cps.append(pltpu.make_async_remote_copy(
                src_ref=src[i].at[idx], dst_ref=other[i], send_sem=ssem.at[i], recv_sem=rsem.at[i],
                device_id=(x, y, 1 - c), device_id_type=MESH_IDS))
            cps[-1].start()
        for cp in cps:
            cp.wait()

    return pl.pallas_call(
        body, name=name, in_specs=[_ANY] * n, out_specs=[_ANY] * n,
        out_shape=[jax.ShapeDtypeStruct(half_shape(s, a), s.dtype) for s, a in zip(srcs, axes)],
        scratch_shapes=[pltpu.SemaphoreType.DMA((n,))] * 2,
    )(*srcs)


def scatter_grads(srcs):
    side = scatter_side(srcs)
    n = len(srcs)

    def body(*refs):
        ins, outs, sems = refs[:n], refs[n:2 * n], refs[2 * n:]
        side.start(ins, outs, sems)
        side.finish(ins, outs, sems)

    return pl.pallas_call(
        body, name="scatter_grads", in_specs=[_ANY] * n, out_specs=[_ANY] * n,
        out_shape=side.outs, scratch_shapes=side.sems,
    )(*srcs)


def scatter_side(srcs):
    n = len(srcs)

    def copies(src, dst, sems, sends_only=False):
        ssem, rsem = sems
        x, y, c, me = _place()
        out = []
        for i in range(n):
            for k, px, py, peer in _chip_peers(x, y):
                sems_k = dict(send_sem=ssem.at[i, k - 1], recv_sem=rsem.at[i, k - 1], device_id=(px, py, c),
                              device_id_type=MESH_IDS)
                snd = pltpu.make_async_remote_copy(src_ref=src[i].at[peer], dst_ref=dst[i].at[me], **sems_k)
                if sends_only:
                    out.append(snd)
                    continue
                out.append((snd, pltpu.make_async_remote_copy(src_ref=src[i].at[peer], dst_ref=dst[i].at[peer],
                                                              **sems_k)))
        return out

    def start(src, dst, sems):
        for snd in copies(src, dst, sems, sends_only=True):
            snd.start()

    def finish(src, dst, sems):
        for snd, rcv in copies(src, dst, sems):
            rcv.wait_recv()
            snd.wait_send()

    return Side(list(srcs), [jax.ShapeDtypeStruct(s.shape, s.dtype) for s in srcs],
                [pltpu.SemaphoreType.DMA((n, 3))] * 2, start, finish)


def swap_halves(srcs):
    n = len(srcs)

    def body(*refs):
        src, dst = refs[:n], refs[n:2 * n]
        ssem, rsem = refs[2 * n:]
        x, y, c, _ = _place()
        cps = []
        for i in range(n):
            cps.append(pltpu.make_async_remote_copy(src_ref=src[i], dst_ref=dst[i], send_sem=ssem.at[i],
                                                    recv_sem=rsem.at[i], device_id=(x, y, 1 - c),
                                                    device_id_type=MESH_IDS))
            cps[-1].start()
        for cp in cps:
            cp.wait()

    return pl.pallas_call(
        body, name="swap_halves", in_specs=[_ANY] * n, out_specs=[_ANY] * n,
        out_shape=[jax.ShapeDtypeStruct(s.shape, s.dtype) for s in srcs],
        scratch_shapes=[pltpu.SemaphoreType.DMA((n,))] * 2,
    )(*srcs)


def allgather8(name, src):
    rows = src.shape[0]

    def body(src_ref, dst_ref, send_sems, recv_sems):
        x, y, c = lax.axis_index("x"), lax.axis_index("y"), lax.axis_index("c")
        me = 4 * x + 2 * y + c
        dst_ref[me] = src_ref[...]
        sends, recvs = [], []
        for k in range(1, 8):
            px = 1 - x if k & 4 else x
            py = 1 - y if k & 2 else y
            pc = 1 - c if k & 1 else c
            peer = 4 * px + 2 * py + pc
            for lst, slot in ((sends, me), (recvs, peer)):
                lst.append(pltpu.make_async_remote_copy(
                    src_ref=src_ref, dst_ref=dst_ref.at[slot], send_sem=send_sems.at[k - 1],
                    recv_sem=recv_sems.at[k - 1], device_id=(px, py, pc), device_id_type=MESH_IDS))
        for cp in sends:
            cp.start()
        for cp in recvs:
            cp.wait_recv()
        for cp in sends:
            cp.wait_send()

    vm = pl.BlockSpec(memory_space=pltpu.VMEM)
    return pl.pallas_call(
        body, name=name, in_specs=[vm], out_specs=vm,
        out_shape=jax.ShapeDtypeStruct((8, rows, LANE), src.dtype),
        scratch_shapes=[pltpu.SemaphoreType.DMA((7,)), pltpu.SemaphoreType.DMA((7,))],
    )(src)


WEIGHTS = ['g_pre', 'w_in', 'mla_q_norm', 'mla_wq_b', 'mla_kv_norm', 'mla_wkv_b', 'rwkv_mu', 'rwkv_w0_f', 'rwkv_w2_f',
           'rwkv_w0_b', 'rwkv_w2_b', 'rwkv_a0_f', 'rwkv_a2_f', 'rwkv_a0_b', 'rwkv_a2_b', 'rwkv_k_k', 'rwkv_k_a',
           'rwkv_r_k', 'rwkv_gn_g', 'rwkv_gn_b', 'w_br_mla', 'w_br_rwkv', 'w_out', 'g_post']
BIG_SHAPES = {'w_in': (D_IN // 4, D), 'mla_wq_b': (Q_RANK, 384), 'mla_wkv_b': (KV_RANK, 512),
              'rwkv_w2_f': (LORA, 256), 'rwkv_w2_b': (LORA, 256), 'rwkv_a2_f': (LORA, 256), 'rwkv_a2_b': (LORA, 256),
              'w_br_mla': (RW, 512), 'w_br_rwkv': (RW, 512), 'w_out': (512, D)}
BIG = list(BIG_SHAPES)
SMALL = [n for n in WEIGHTS if n not in BIG_SHAPES]
SMALL_SHAPES = {'g_pre': (D,), 'mla_q_norm': (Q_RANK,), 'mla_kv_norm': (KV_RANK,), 'rwkv_mu': (3456,),
                'rwkv_w0_f': (RW,), 'rwkv_w0_b': (RW,), 'rwkv_a0_f': (RW,), 'rwkv_a0_b': (RW,), 'rwkv_k_k': (RW,),
                'rwkv_k_a': (RW,), 'rwkv_r_k': (RH, RN), 'rwkv_gn_g': (RW,), 'rwkv_gn_b': (RW,), 'g_post': (D,)}
SMALL_LEN = sum(int(np.prod(s)) for s in SMALL_SHAPES.values())
SMALL_ROWS = 144


UNITS = [('w_in',), ('mla_wq_b',), ('mla_wkv_b',), ('rwkv_w2_f', 'rwkv_w2_b', 'rwkv_a2_f', 'rwkv_a2_b'),
         ('w_br_mla', 'w_br_rwkv'), ('w_out',)]
UNIT_AXIS = [1, 0, 0, 0, 0, 0]
ROW_SHARDED = ('w_in', 'w_out')


def _unit_cat(parts):
    return parts[0] if len(parts) == 1 else jnp.concatenate(parts, axis=0)


def _unit_split(arr, names, axis):
    out, o = {}, 0
    for n in names:
        rows = BIG_SHAPES[n][0]
        out[n] = lax.slice_in_dim(arr, o, o + rows, axis=axis)
        o += rows
    return out


def _gathered(units, ag, own, me):
    out = {}
    for names, arr, mine in zip(units, ag, own):
        slots = [jnp.where(me == j, mine, arr[j]) for j in range(4)]
        for n in names:
            parts = [_unit_split(s, names, 0)[n] for s in slots]
            out[n] = jnp.concatenate(parts, axis=0 if n in ROW_SHARDED else 1)
    return out


def _shards(n, g):
    r, w = BIG_SHAPES[n]
    if n in ROW_SHARDED:
        return [g[j * r:(j + 1) * r] for j in range(4)]
    return [g[:, j * w:(j + 1) * w] for j in range(4)]


def _pack_small(d, extra=None):
    flat = jnp.concatenate([d[n].reshape(-1) for n in SMALL] + ([extra.reshape(-1)] if extra is not None else []))
    return jnp.pad(flat, (0, SMALL_ROWS * LANE - flat.shape[0])).reshape(SMALL_ROWS, LANE)


def _unpack_small(packed):
    flat, out, o = packed.reshape(-1), {}, 0
    for n in SMALL:
        sz = int(np.prod(SMALL_SHAPES[n]))
        out[n] = flat[o:o + sz].reshape(SMALL_SHAPES[n])
        o += sz
    return out


def _perm_w_in(w):
    z = lambda n: jnp.zeros((n, w.shape[1]), w.dtype)
    lora = []
    for i in range(4):
        lora += [w[4160 + LORA * i:4160 + LORA * (i + 1)], z(LANE - LORA)]
    return jnp.concatenate([w[0:1024], w[1088:4160], w[4544:D_IN]] + lora
                           + [w[1024:1056], z(96), w[1056:1088], z(96)], axis=0)


def _unperm_w_in(g):
    lora = [g[OFF_LORA + LANE * i:OFF_LORA + LANE * i + LORA] for i in range(4)]
    return jnp.concatenate([g[0:1024], g[OFF_KR:OFF_KR + 32], g[OFF_KR + LANE:OFF_KR + LANE + 32],
                            g[1024:4096]] + lora + [g[4096:OFF_LORA]], axis=0)


def _perm_wq(w):
    w3 = w.reshape(Q_RANK, HEADS, NOPE + ROPE)
    return jnp.concatenate([w3[:, :, :NOPE].reshape(Q_RANK, -1), w3[:, :, NOPE:NOPE + 32].reshape(Q_RANK, -1),
                            w3[:, :, NOPE + 32:].reshape(Q_RANK, -1)], axis=1)


def _unperm_wq(g):
    return jnp.concatenate([g[:, :1024].reshape(Q_RANK, HEADS, NOPE), g[:, 1024:1280].reshape(Q_RANK, HEADS, 32),
                            g[:, 1280:].reshape(Q_RANK, HEADS, 32)], axis=2).reshape(Q_RANK, -1)


def _perm_wkv(w):
    w3 = w.reshape(KV_RANK, HEADS, NOPE + VDIM)
    return jnp.concatenate([w3[:, :, :NOPE].reshape(KV_RANK, -1), w3[:, :, NOPE:].reshape(KV_RANK, -1)], axis=1)


def _unperm_wkv(g):
    return jnp.concatenate([g[:, :1024].reshape(KV_RANK, HEADS, NOPE), g[:, 1024:].reshape(KV_RANK, HEADS, VDIM)],
                           axis=2).reshape(KV_RANK, -1)


def _pad_rows(w):
    return jnp.pad(w, ((0, LANE - LORA), (0, 0)))


def _perm_mu(mu):
    parts = [mu[:3072]]
    for i in range(4):
        parts += [mu[3072 + LORA * i:3072 + LORA * (i + 1)], jnp.zeros((LANE - LORA,), mu.dtype)]
    return jnp.concatenate(parts).reshape(1, NLERP)


def _unperm_mu(g):
    g = g.reshape(-1)
    return jnp.concatenate([g[:3072]] + [g[3072 + LANE * i:3072 + LANE * i + LORA] for i in range(4)])


def _to_heads(t, n):
    return t.reshape(T, -1, n).transpose(1, 0, 2)


def _from_heads(t):
    return t.transpose(1, 0, 2).reshape(T, -1)


def _to_qk(n, r1, r2):
    return jnp.concatenate([n.reshape(T, HEADS, NOPE), r1.reshape(T, HEADS, 32), r2.reshape(T, HEADS, 32)],
                           axis=2).transpose(1, 0, 2)


def _from_qk(g):
    g = g.transpose(1, 0, 2)
    return g[:, :, :NOPE].reshape(T, -1), g[:, :, NOPE:NOPE + 32].reshape(T, -1), g[:, :, NOPE + 32:].reshape(T, -1)


def _constants():
    g2 = np.kron(np.eye(2, dtype=np.float32), np.ones((RN, RN), np.float32))
    e = np.zeros((LANE, 256), np.float32)
    for h in range(HEADS):
        e[np.arange(32), h * 32 + np.arange(32)] = 1.0
    pos = jnp.arange(T, dtype=F32)
    inv_freq = jnp.power(ROPE_THETA, -jnp.arange(0, ROPE, 2, dtype=F32) / ROPE)
    ang = pos[:, None] * inv_freq[None, :]
    cos, sin = jnp.cos(ang), jnp.sin(ang)
    padk = lambda t: jnp.pad(t, ((0, 0), (0, LANE - 32)))
    return (jnp.asarray(g2, BF16), jnp.asarray(e, BF16), jnp.tile(cos, (1, HEADS)), jnp.tile(sin, (1, HEADS)),
            padk(cos), padk(sin))


def _step(x, tgt, w, m, v):
    x2, tgt2 = x.reshape(T, D), tgt.reshape(T, D)
    g2, e_mat, cosq, sinq, cosk, sink = _constants()
    row = lambda n: w[n].reshape(1, -1)
    w, m, v = ({**t, 'w_in': t['w_in'].T} for t in (w, m, v))

    core, chip = lax.axis_index("c"), 2 * lax.axis_index("x") + lax.axis_index("y")
    core1, chip1 = core.astype(jnp.int32).reshape(1), chip.astype(jnp.int32).reshape(1)
    own_bf = [_unit_cat([w[n].astype(BF16) for n in u]) for u in UNITS]
    full = _gathered(UNITS[:1], gather_weights(own_bf[:1], UNIT_AXIS[:1]), own_bf[:1], chip)
    wp = _perm_w_in(full['w_in'])
    mu_p = _perm_mu(w['rwkv_mu'])

    st_pre = Stage("pre", f_pre, [(D, BF16), (D, None)], 256, [0], [0], [F32])
    st_mla = Stage("mla", f_mla, [(1024, BF16), (256, BF16), (256, BF16), (1024, BF16), (256, BF16), (256, BF16),
                                  (1024, BF16)], 256, [0, 1, 2], [0, 1, 2, 3], [BF16] * 3)
    st_rpre = Stage("rwkv_pre", f_rwkv_pre, [(RW, F32)] * 9, 128, [0, 1, 2, 3], list(range(10)), [F32] * 4)
    st_rpost = Stage("rwkv_post", f_rwkv_post, [(RW, BF16)], 256, [0, 2, 3, 4, 5, 6], [0, 1, 2],
                     [F32, F32, F32, F32, F32, BF16])
    st_gate = Stage("gate", f_gate, [(RW, BF16)], 256, [0, 1], [], [F32, BF16])
    st_merge = Stage("merge", f_merge, [(D, BF16)], 256, [0, 1, 2, 3], [], [BF16] * 4)

    pre_rows, pre_par = [(x2, D, 0)], [row('g_pre')]
    (h,) = st_pre.fwd(pre_rows, pre_par)
    proj, rest = matmul("mm_in", h, wp, "nt", side=gather_side(own_bf[1:], UNIT_AXIS[1:]))
    full.update(_gathered(UNITS[1:], rest, own_bf[1:], chip))
    wq = _perm_wq(full['mla_wq_b']).astype(F32)
    wkv = _perm_wkv(full['mla_wkv_b']).astype(F32)
    lora_w = [_pad_rows(full[n]).astype(F32) for n in ('rwkv_w2_f', 'rwkv_w2_b', 'rwkv_a2_f', 'rwkv_a2_b')]

    mla_rows = [(proj, 512, OFF_QA // 512), (proj, 512, OFF_KVA // 512), (proj, 256, OFF_KR // 256),
                (cosq, 256, 0), (sinq, 256, 0), (cosk, LANE, 0), (sink, LANE, 0)]
    mla_par = [row('mla_q_norm'), row('mla_kv_norm'), wq, wkv, e_mat]
    qn, qr1, qr2, kn, kr1, kr2, vv = st_mla.fwd(mla_rows, mla_par)
    qh, kh, vh = _to_qk(qn, qr1, qr2), _to_qk(kn, kr1, kr2), _to_heads(vv, VDIM)
    y_mla = _from_heads(attn_fwd(qh, kh, vh))

    lerp = shift_fwd(proj, mu_p)
    rpre_rows = [(lerp, RW, 0), (lerp, RW, 1), (lerp, RW, 2), (lerp, 512, 6)]
    rpre_par = [row('rwkv_w0_f'), row('rwkv_w0_b'), row('rwkv_a0_f'), row('rwkv_a0_b'), row('rwkv_k_k'),
                row('rwkv_k_a')] + lora_w + [g2]
    r_, v_, lwf, lwb, kf, kb, an, bf_, bb_ = st_rpre.fwd(rpre_rows, rpre_par)
    fin = [r_, lwf, kf, v_, an, bf_]
    bin_ = [r_, lwb, kb, v_, an, bb_]
    yf, h0f = scan_fwd("scan_f", *fin, reverse=False)
    yb, h0b = scan_fwd("scan_b", *bin_, reverse=True)
    rpost_rows = [(yf, RW, 0), (yb, RW, 0), (r_, RW, 0), (kf, RW, 0), (kb, RW, 0),
                  (v_, RW, 0), (proj, RW, OFF_ZR // RW)]
    rpost_par = [row('rwkv_gn_g'), row('rwkv_gn_b'), row('rwkv_r_k'), g2]
    (gr,) = st_rpost.fwd(rpost_rows, rpost_par)
    gate_rows = [(y_mla, RW, 0), (proj, RW, OFF_ZM // RW)]
    (gm,) = st_gate.fwd(gate_rows, [])
    um = matmul("mm_br_mla", gm, full['w_br_mla'], "nn")
    ur = matmul("mm_br_rwkv", gr, full['w_br_rwkv'], "nn")
    merge_rows = [(um, D, 0), (ur, D, 0), (proj, D, OFF_GM // D), (proj, D, OFF_GR // D)]
    (merged,) = st_merge.fwd(merge_rows, [])
    out = matmul("mm_out", merged, full['w_out'], "nn")
    d_out, dy, dg_post, loss_blk = loss_stage(out, x2, tgt2, row('g_post'))

    gw = {'g_post': dg_post}
    d_merged = matmul("mm_out_dx", d_out, full['w_out'], "nt")
    gw['w_out'] = matmul("mm_out_dw", merged, d_out, "tn")
    (d_um, d_ur, d_gm, d_gr), _ = st_merge.bwd(merge_rows, [], [[(d_merged, D, 0)]])
    d_gmla = matmul("mm_br_mla_dx", d_um, full['w_br_mla'], "nt")
    gw['w_br_mla'] = matmul("mm_br_mla_dw", gm, d_um, "tn")
    d_grw = matmul("mm_br_rwkv_dx", d_ur, full['w_br_rwkv'], "nt")
    gw['w_br_rwkv'] = matmul("mm_br_rwkv_dw", gr, d_ur, "tn")
    (d_ymla, d_zm), _ = st_gate.bwd(gate_rows, [], [[(d_gmla, RW, 0)]])
    (d_y, d_r3, d_kf2, d_kb2, d_v3, d_zr), (gw['rwkv_gn_g'], gw['rwkv_gn_b'], d_rk) = st_rpost.bwd(
        rpost_rows, rpost_par, [[(d_grw, RW, 0)]])
    gw['rwkv_r_k'] = d_rk
    sf = scan_bwd("scan_f_bwd", *fin, h0f, d_y, reverse=False)
    sb = scan_bwd("scan_b_bwd", *bin_, h0b, d_y, reverse=True)
    c = lambda *ts: [(t, RW, 0) for t in ts]
    rpre_cts = [c(sf[0], sb[0], d_r3), c(sf[3], sb[3], d_v3), c(sf[1]), c(sb[1]), c(sf[2], d_kf2), c(sb[2], d_kb2),
                c(sf[4], sb[4]), c(sf[5]), c(sb[5])]
    d_lerp_rows, rpre_g = st_rpre.bwd(rpre_rows, rpre_par, rpre_cts)
    for n, gval in zip(('rwkv_w0_f', 'rwkv_w0_b', 'rwkv_a0_f', 'rwkv_a0_b', 'rwkv_k_k', 'rwkv_k_a'), rpre_g[:6]):
        gw[n] = gval
    for n, gval in zip(('rwkv_w2_f', 'rwkv_w2_b', 'rwkv_a2_f', 'rwkv_a2_b'), rpre_g[6:]):
        gw[n] = gval[:LORA]
    d_lerp, d_mu = shift_bwd(proj, mu_p, jnp.concatenate(d_lerp_rows, axis=1))
    gw['rwkv_mu'] = _unperm_mu(d_mu)

    dqh, dkh, dvh = attn_bwd(qh, kh, vh, _to_heads(d_ymla, VDIM))
    mla_cts = [[(t, t.shape[1], 0)] for t in (*_from_qk(dqh), *_from_qk(dkh), _from_heads(dvh))]
    (d_qa, d_kva, d_kr), (gw['mla_q_norm'], gw['mla_kv_norm'], d_wq, d_wkv) = st_mla.bwd(mla_rows, mla_par, mla_cts)
    gw['mla_wq_b'], gw['mla_wkv_b'] = _unperm_wq(d_wq), _unperm_wkv(d_wkv)

    dproj = jnp.concatenate([d_qa, d_kva, d_lerp[:, :3072], d_zm, d_zr, d_gm, d_gr, d_lerp[:, 3072:], d_kr], axis=1)

    def pair_sums(name, ids):
        send = [jnp.stack([_unit_cat([_shards(n, gw[n])[j].astype(BF16) for n in UNITS[i]]) for j in range(4)])
                for i in ids]
        axes = [UNIT_AXIS[i] for i in ids]
        other = pair_exchange(name, send, axes)
        return [pair_sum(f"pair_sum_{i}", core1, s, o, ax) for i, s, o, ax in zip(ids, send, other, axes)]

    late, early = [0], list(range(1, len(UNITS)))
    pairs_e = pair_sums("pair_exchange_rest", early)
    gw_in, recv_e = matmul("mm_in_dw", dproj, h, "tn", BF16, side=scatter_side(pairs_e))
    gw['w_in'] = _unperm_w_in(gw_in)
    pairs_l = pair_sums("pair_exchange_w_in", late)
    dh, recv_l = matmul("mm_in_dx", dproj, wp, "nn", side=scatter_side(pairs_l))
    (grad_x,), (gw['g_pre'],) = st_pre.bwd(pre_rows, pre_par, [[(dh, D, 0)], [(dy, D, 0)]])
    recv, pairs = list(recv_l) + list(recv_e), pairs_l + pairs_e
    mine = [sum4(f"sum4_{i}", chip1, r, p, ax) for i, (r, p, ax) in enumerate(zip(recv, pairs, UNIT_AXIS))]
    theirs = swap_halves(mine)
    big = [dict() for _ in range(4)]
    for i, u in enumerate(UNITS):
        res = adamw_halves(f"adamw_{i}", core1, *[_unit_cat([t[n] for n in u]) for t in (w, m, v)], mine[i], theirs[i],
                           UNIT_AXIS[i])
        for q in range(4):
            big[q].update(_unit_split(res[q], u, 0))
    parts = allgather8("gather_small", _pack_small(gw, loss_blk[0, :1]))
    small = adamw("adamw_small", _pack_small(w), _pack_small(m), _pack_small(v), [parts[i] for i in range(8)])

    outs = []
    for b_d, s_arr in zip(big, small):
        d = {**b_d, **_unpack_small(s_arr)}
        d['w_in'] = d['w_in'].T
        outs.append([d[n] for n in WEIGHTS])
    loss = small[0][SMALL_LEN // LANE, 0]
    return (loss, grad_x.reshape(1, T, D), *outs[0], *outs[1], *outs[2], *outs[3])


def kernel(x, g_pre, w_in, mla_q_norm, mla_wq_b, mla_kv_norm, mla_wkv_b, rwkv_mu, rwkv_w0_f, rwkv_w2_f, rwkv_w0_b, rwkv_w2_b, rwkv_a0_f, rwkv_a2_f, rwkv_a0_b, rwkv_a2_b, rwkv_k_k, rwkv_k_a, rwkv_r_k, rwkv_gn_g, rwkv_gn_b, w_br_mla, w_br_rwkv, w_out, g_post, loss_target, m_g_pre, m_w_in, m_mla_q_norm, m_mla_wq_b, m_mla_kv_norm, m_mla_wkv_b, m_rwkv_mu, m_rwkv_w0_f, m_rwkv_w2_f, m_rwkv_w0_b, m_rwkv_w2_b, m_rwkv_a0_f, m_rwkv_a2_f, m_rwkv_a0_b, m_rwkv_a2_b, m_rwkv_k_k, m_rwkv_k_a, m_rwkv_r_k, m_rwkv_gn_g, m_rwkv_gn_b, m_w_br_mla, m_w_br_rwkv, m_w_out, m_g_post, v_g_pre, v_w_in, v_mla_q_norm, v_mla_wq_b, v_mla_kv_norm, v_mla_wkv_b, v_rwkv_mu, v_rwkv_w0_f, v_rwkv_w2_f, v_rwkv_w0_b, v_rwkv_w2_b, v_rwkv_a0_f, v_rwkv_a2_f, v_rwkv_a0_b, v_rwkv_a2_b, v_rwkv_k_k, v_rwkv_k_a, v_rwkv_r_k, v_rwkv_gn_g, v_rwkv_gn_b, v_w_br_mla, v_w_br_rwkv, v_w_out, v_g_post):
    given = dict(locals())
    w = {n: given[n] for n in WEIGHTS}
    m = {n: given['m_' + n] for n in WEIGHTS}
    v = {n: given['v_' + n] for n in WEIGHTS}
    return _step(x, loss_target, w, m, v)
```

```python
import functools
import math

import numpy as np
import jax
import jax.numpy as jnp
from jax import lax
from jax.experimental import pallas as pl
from jax.experimental.pallas import tpu as pltpu

F32, BF16 = jnp.float32, jnp.bfloat16
MESH_IDS = pl.DeviceIdType.MESH

D = 2048
T = 2048
HEADS = 8
Q_RANK = 512
KV_RANK = 512
NOPE = 128
ROPE = 64
VDIM = 128
RW = 1024
RH = 16
RN = 64
LORA = 96
D_IN = 10688
NORM_EPS = 1e-6
GN_EPS = 64e-5
ROPE_THETA = 10000.0
ADAM_LR, ADAM_B1, ADAM_B2, ADAM_EPS, ADAM_WD, ADAM_STEP = 0.001, 0.9, 0.999, 1e-08, 0.01, 10

LANE = 128
VMEM_BIG = 56 * 2**20

NP = 11008
OFF_QA, OFF_KVA, OFF_RKV, OFF_ZM, OFF_ZR, OFF_GM, OFF_GR, OFF_LORA, OFF_KR = 0, 512, 1024, 4096, 5120, 6144, 8192, 10240, 10752
NLERP = 3584

CHUNK = 64
NCH = T // CHUNK


def _dg(a, b, ca, cb, batch=False, prec=None):
    bd = ((0,), (0,)) if batch else ((), ())
    return lax.dot_general(a, b, (((ca,), (cb,)), bd), precision=prec, preferred_element_type=F32)


@jax.custom_vjp
def bdot(a, b):
    return _dg(a.astype(BF16), b.astype(BF16), 1, 0)


def _bdot_fwd(a, b):
    return bdot(a, b), (a, b)


def _bdot_bwd(res, g):
    a, b = res
    gb = g.astype(BF16)
    da = _dg(gb, b.astype(BF16), 1, 1)
    db = _dg(a.astype(BF16), gb, 0, 0)
    return da.astype(a.dtype), db.astype(b.dtype)


bdot.defvjp(_bdot_fwd, _bdot_bwd)


def _split(x):
    hi = x.astype(BF16)
    lo = (x - hi.astype(F32)).astype(BF16)
    return hi, lo


@jax.custom_vjp
def gsum(x, g2):
    hi, lo = _split(x)
    return _dg(hi, g2, 1, 0) + _dg(lo, g2, 1, 0)


def _gsum_fwd(x, g2):
    return gsum(x, g2), g2


def _gsum_bwd(g2, g):
    hi, lo = _split(g)
    return _dg(hi, g2, 1, 1) + _dg(lo, g2, 1, 1), jnp.zeros_like(g2)


gsum.defvjp(_gsum_fwd, _gsum_bwd)


def headsum(x, g2):
    return jnp.concatenate([gsum(x[:, i * LANE:(i + 1) * LANE], g2) for i in range(x.shape[1] // LANE)], axis=1)


def _terms(x, n):
    out = []
    for i in range(n):
        t = x.astype(BF16)
        out.append(t)
        if i < n - 1:
            x = x - t.astype(F32)
    return out


def _bmm(a, b, ca, cb, na, nb):
    acc = None
    for i, ai in enumerate(_terms(a, na)):
        for j, bj in enumerate(_terms(b, nb)):
            if i + j < max(na, nb):
                p = _dg(ai, bj, ca, cb, True)
                acc = p if acc is None else acc + p
    return acc


_NN, _NT, _TN = (2, 1), (2, 2), (1, 1)


def _make_dots(nf, nb_nn, nb_nt, nb_tn):
    @jax.custom_vjp
    def nn(a, b):
        return _bmm(a, b, *_NN, nf, nf)

    @jax.custom_vjp
    def nt(a, b):
        return _bmm(a, b, *_NT, nf, nf)

    @jax.custom_vjp
    def tn(a, b):
        return _bmm(a, b, *_TN, nf, nf)

    nn.defvjp(lambda a, b: (nn(a, b), (a, b)),
              lambda r, g: (_bmm(g, r[1], *_NT, nb_nn, nb_nn), _bmm(r[0], g, *_TN, nb_nn, nb_nn)))
    nt.defvjp(lambda a, b: (nt(a, b), (a, b)),
              lambda r, g: (_bmm(g, r[1], *_NN, 1, nb_nt), _bmm(g, r[0], *_TN, 1, nb_nt)))
    tn.defvjp(lambda a, b: (tn(a, b), (a, b)),
              lambda r, g: (_bmm(r[1], g, *_NT, nb_tn, nb_tn), _bmm(r[0], g, *_NN, nb_tn, nb_tn)))
    return nn, nt, tn


_SCAN_NF, _SCAN_NB = 1, 1
nn, nt, tn = _make_dots(_SCAN_NF, 1, 2, 1)


@jax.custom_vjp
def cumdot(ones, x):
    return _bmm(ones, x, *_NN, 1, 3)


cumdot.defvjp(lambda o, x: (cumdot(o, x), o), lambda o, g: (jnp.zeros_like(o), _bmm(o, g, *_TN, 1, 3)))


def _solve_powers(l):
    pw = [l]
    for _ in range(int(math.log2(l.shape[-1])) - 1):
        pw.append(_bmm(pw[-1], pw[-1], *_NN, _SCAN_NF, _SCAN_NF))
    return pw


@jax.custom_vjp
def tri_solve(l, rhs):
    x = rhs
    for p in _solve_powers(l):
        x = x + _bmm(p, x, *_NN, _SCAN_NF, _SCAN_NF)
    return x


def _tri_solve_fwd(l, rhs):
    pw = _solve_powers(l)
    x = rhs
    for p in pw:
        x = x + _bmm(p, x, *_NN, _SCAN_NF, _SCAN_NF)
    return x, (pw, x)


def _tri_solve_bwd(res, g):
    pw, x = res
    y = g
    for p in pw:
        y = y + _bmm(p, y, *_TN, _SCAN_NB, _SCAN_NB)
    return _bmm(y, x, *_NT, _SCAN_NB, _SCAN_NB), y


tri_solve.defvjp(_tri_solve_fwd, _tri_solve_bwd)


def _rms(x, g):
    return x * lax.rsqrt(jnp.mean(x * x, axis=-1, keepdims=True) + NORM_EPS) * g


def _softplus(x):
    pos = x > 0
    return jnp.where(pos, x, 0.0) + jnp.log(1.0 + jnp.exp(-jnp.where(pos, x, -x)))


def _silu(z):
    return z * jax.nn.sigmoid(z)


def _tile(n, cands):
    for c in cands:
        if n % c == 0:
            return c
    raise ValueError(n)


_MM_VMEM_BYTES = 32 * 2**20


def _mm_tiles(m, n, k):
    best = None
    for tm in (2048, 1024, 512, 256):
        for tn_ in (2048, 1024, 512, 256):
            for d in range(k // LANE, 0, -1):
                tk = LANE * d
                if m % tm or n % tn_ or k % tk:
                    continue
                nk = k // tk
                vmem = 4 * tk * (tm + tn_) + 8 * tm * tn_ + (4 * tm * tn_ if nk > 1 else 0)
                if vmem > _MM_VMEM_BYTES:
                    continue
                a_reads = n // tn_ if nk > 1 else 1
                b_reads = 1 if (nk == 1 and n == tn_) else m // tm
                acc_rmw = nk * m * n if nk > 1 else 0
                cost = (a_reads * m * k + b_reads * k * n + acc_rmw, -tm * tn_ * tk)
                if best is None or cost < best[0]:
                    best = (cost, (tm, tn_, tk))
    return best[1]


class Side:
    def __init__(self, ins, outs, sems, start, finish):
        self.ins, self.outs, self.sems, self.start, self.finish = ins, outs, sems, start, finish


def matmul(name, a, b, mode, out_dtype=F32, side=None):
    if mode == "nn":
        (m, k), n = a.shape, b.shape[1]
    elif mode == "nt":
        (m, k), n = a.shape, b.shape[0]
    else:
        (k, m), n = a.shape, b.shape[1]
    tm, tn_, tk = _mm_tiles(m, n, k)
    nk = k // tk
    if mode == "nn":
        a_spec = pl.BlockSpec((tm, tk), lambda i, j, kk: (i, kk))
        b_spec = pl.BlockSpec((tk, tn_), lambda i, j, kk: (kk, j))
        ca, cb = 1, 0
    elif mode == "nt":
        a_spec = pl.BlockSpec((tm, tk), lambda i, j, kk: (i, kk))
        b_spec = pl.BlockSpec((tn_, tk), lambda i, j, kk: (j, kk))
        ca, cb = 1, 1
    else:
        a_spec = pl.BlockSpec((tk, tm), lambda i, j, kk: (kk, i))
        b_spec = pl.BlockSpec((tk, tn_), lambda i, j, kk: (kk, j))
        ca, cb = 0, 0

    grid = (m // tm, n // tn_, nk)
    n_in = len(side.ins) if side else 0
    n_out = len(side.outs) if side else 0

    def body(a_ref, b_ref, *rest):
        s_ins, o_ref, s_outs = rest[:n_in], rest[n_in], rest[n_in + 1:n_in + 1 + n_out]
        scratch = rest[n_in + 1 + n_out:]
        acc, s_sems = (scratch[:1], scratch[1:]) if nk > 1 else ((), scratch)
        if side:
            step = (pl.program_id(0) * grid[1] + pl.program_id(1)) * grid[2] + pl.program_id(2)

            @pl.when(step == 0)
            def _():
                side.start(s_ins, s_outs, s_sems)

        part = _dg(a_ref[...].astype(BF16), b_ref[...].astype(BF16), ca, cb)
        if nk == 1:
            o_ref[...] = part.astype(o_ref.dtype)
        else:
            acc_ref, kk = acc[0], pl.program_id(2)

            @pl.when(kk == 0)
            def _():
                acc_ref[...] = part

            @pl.when(kk > 0)
            def _():
                acc_ref[...] += part

            @pl.when(kk == nk - 1)
            def _():
                o_ref[...] = acc_ref[...].astype(o_ref.dtype)

        if side:
            @pl.when(step == grid[0] * grid[1] * grid[2] - 1)
            def _():
                side.finish(s_ins, s_outs, s_sems)

    res = pl.pallas_call(
        body, name=name, grid=grid,
        in_specs=[a_spec, b_spec] + [_ANY] * n_in,
        out_specs=[pl.BlockSpec((tm, tn_), lambda i, j, kk: (i, j))] + [_ANY] * n_out,
        out_shape=[jax.ShapeDtypeStruct((m, n), out_dtype)] + (list(side.outs) if side else []),
        scratch_shapes=([pltpu.VMEM((tm, tn_), F32)] if nk > 1 else []) + (list(side.sems) if side else []),
        compiler_params=pltpu.CompilerParams(
            dimension_semantics=("arbitrary",) * 3 if side else ("parallel", "parallel", "arbitrary"),
            vmem_limit_bytes=VMEM_BIG),
    )(a, b, *(side.ins if side else []))
    return (res[0], res[1:]) if side else res[0]


def _rspec(tr, width, blk):
    return pl.BlockSpec((tr, width), lambda i: (i, blk))


def _full_spec(arr):
    return pl.BlockSpec(arr.shape, lambda i: (0,) * arr.ndim)


class Stage:
    def __init__(self, name, f, outs, tr, diff_rows, diff_params, drow_dtypes):
        self.name, self.f, self.outs, self.tr = name, f, outs, tr
        self.diff_rows, self.diff_params, self.drow_dtypes = diff_rows, diff_params, drow_dtypes

    def fwd(self, rows, params):
        f, nr, npar = self.f, len(rows), len(params)
        stored = [(w, dt) for (w, dt) in self.outs if dt is not None]
        keep = [i for i, (w, dt) in enumerate(self.outs) if dt is not None]

        def body(*refs):
            vals = f(*[r[...].astype(F32) for r in refs[:nr]], *[p[...] for p in refs[nr:nr + npar]])
            for o_ref, i in zip(refs[nr + npar:], keep):
                o_ref[...] = vals[i].astype(o_ref.dtype)

        return pl.pallas_call(
            body, name=self.name + "_fwd", grid=(T // self.tr,),
            in_specs=[_rspec(self.tr, w, b) for (_, w, b) in rows] + [_full_spec(p) for p in params],
            out_specs=[_rspec(self.tr, w, 0) for (w, _) in stored],
            out_shape=[jax.ShapeDtypeStruct((T, w), dt) for (w, dt) in stored],
            compiler_params=pltpu.CompilerParams(dimension_semantics=("arbitrary",), vmem_limit_bytes=VMEM_BIG),
        )(*[r[0] for r in rows], *params)

    def bwd(self, rows, params, cts):
        f, nr, npar = self.f, len(rows), len(params)
        dr_idx, dp_idx = self.diff_rows, self.diff_params
        flat_cts = [c for lst in cts for c in lst]
        nct = len(flat_cts)

        def body(*refs):
            row_refs, par_refs = refs[:nr], refs[nr:nr + npar]
            ct_refs = refs[nr + npar:nr + npar + nct]
            drow_refs = refs[nr + npar + nct:nr + npar + nct + len(dr_idx)]
            dpar_refs = refs[nr + npar + nct + len(dr_idx):]
            row_vals = [r[...].astype(F32) for r in row_refs]
            par_vals = [p[...] for p in par_refs]

            def g(*dv):
                rv, pv = list(row_vals), list(par_vals)
                for j, i in enumerate(dr_idx):
                    rv[i] = dv[j]
                for j, i in enumerate(dp_idx):
                    pv[i] = dv[len(dr_idx) + j]
                return f(*rv, *pv)

            _, vjp = jax.vjp(g, *[row_vals[i] for i in dr_idx], *[par_vals[i] for i in dp_idx])
            ct_vals, pos = [], 0
            for lst in cts:
                acc = ct_refs[pos][...].astype(F32)
                for q in range(1, len(lst)):
                    acc = acc + ct_refs[pos + q][...].astype(F32)
                pos += len(lst)
                ct_vals.append(acc)
            grads = vjp(tuple(ct_vals))
            for j, r in enumerate(drow_refs):
                r[...] = grads[j].astype(r.dtype)

            @pl.when(pl.program_id(0) == 0)
            def _():
                for r in dpar_refs:
                    r[...] = jnp.zeros_like(r)

            for j, r in enumerate(dpar_refs):
                r[...] += grads[len(dr_idx) + j].astype(F32)

        drow_shapes = [jax.ShapeDtypeStruct((T, rows[i][1]), dt) for i, dt in zip(dr_idx, self.drow_dtypes)]
        dpar_shapes = [jax.ShapeDtypeStruct(params[i].shape, F32) for i in dp_idx]
        res = pl.pallas_call(
            body, name=self.name + "_bwd", grid=(T // self.tr,),
            in_specs=[_rspec(self.tr, w, b) for (_, w, b) in rows] + [_full_spec(p) for p in params]
            + [_rspec(self.tr, w, b) for (_, w, b) in flat_cts],
            out_specs=[_rspec(self.tr, rows[i][1], 0) for i in dr_idx] + [_full_spec(params[i]) for i in dp_idx],
            out_shape=drow_shapes + dpar_shapes,
            compiler_params=pltpu.CompilerParams(dimension_semantics=("arbitrary",), vmem_limit_bytes=VMEM_BIG),
        )(*[r[0] for r in rows], *params, *[c[0] for c in flat_cts])
        return res[:len(dr_idx)], res[len(dr_idx):]


def f_pre(x, g):
    return _rms(x, g), x


@jax.custom_vjp
def swap32(t):
    width = t.shape[1]
    lane = lax.broadcasted_iota(jnp.int32, t.shape, 1) % LANE
    return jnp.where(lane < 32, pltpu.roll(t, width - 32, 1), jnp.where(lane < 64, pltpu.roll(t, 32, 1), 0.0))


swap32.defvjp(lambda t: (swap32(t), None), lambda _, g: (swap32(g),))


def f_mla(q_a, kv_a, kr, cq, sq, ck, sk, gq, gkv, wq, wkv):
    q = bdot(_rms(q_a, gq), wq)
    kv = bdot(_rms(kv_a, gkv), wkv)
    t, k = q[:, 1024:], kr[:, :LANE]
    return (q[:, :1024], t * cq + swap32(t) * sq, kv[:, :1024], k * ck + swap32(k) * sk, kv[:, 1024:])


def f_rwkv_pre(lerp, w0f, w0b, a0f, a0b, kkw, kaw, w2f, w2b, a2f, a2b, g2):
    r, k, v = lerp[:, :RW], lerp[:, RW:2 * RW], lerp[:, 2 * RW:3 * RW]
    wdf, wdb, adf, adb = (lerp[:, 3 * RW + i * LANE:3 * RW + (i + 1) * LANE] for i in range(4))

    def logdecay(w0, wd, w2):
        z = w0 + bdot(jnp.tanh(wd), w2)
        return -jnp.exp(-_softplus(-z) - 0.5)

    a_f = jax.nn.sigmoid(a0f + bdot(adf, a2f))
    a_b = jax.nn.sigmoid(a0b + bdot(adb, a2b))
    kk = k * kkw
    kk = kk / jnp.maximum(jnp.sqrt(headsum(kk * kk, g2)), 1e-12)
    return (r, v, logdecay(w0f, wdf, w2f), logdecay(w0b, wdb, w2b),
            k * (1.0 + (a_f - 1.0) * kaw), k * (1.0 + (a_b - 1.0) * kaw), -kk, kk * a_f, kk * a_b)


def f_rwkv_post(yf, yb, r, kf, kb, v, z, gng, gnb, rk, g2):
    y = yf + yb
    mu = headsum(y, g2) * (1.0 / RN)
    d = y - mu
    var = headsum(d * d, g2) * (1.0 / RN)
    yn = d * lax.rsqrt(var + GN_EPS) * gng + gnb
    bonus = headsum(r * (kf + kb) * rk, g2) * v
    return ((yn + bonus) * _silu(z),)


def f_gate(y, z):
    return (y * _silu(z),)


def f_merge(um, ur, gm, gr):
    return (jax.nn.sigmoid(gm) * um + jax.nn.sigmoid(gr) * ur,)


_SHIFT_W = 256


def _lerp_colblock(j):
    return jnp.where(j < 3072 // _SHIFT_W, OFF_RKV // _SHIFT_W + j, OFF_LORA // _SHIFT_W + j - 3072 // _SHIFT_W)


def _nbr_mean(x):
    row = lax.broadcasted_iota(jnp.int32, x.shape, 0)
    up = jnp.where(row == 0, 0.0, pltpu.roll(x, 1, 0))
    dn = jnp.where(row == T - 1, 0.0, pltpu.roll(x, T - 1, 0))
    return 0.5 * (up + dn)


def shift_fwd(proj, mu):
    def body(x_ref, mu_ref, o_ref):
        x = x_ref[...]
        o_ref[...] = x + mu_ref[...] * (_nbr_mean(x) - x)

    return pl.pallas_call(
        body, name="shift_fwd", grid=(NLERP // _SHIFT_W,),
        in_specs=[pl.BlockSpec((T, _SHIFT_W), lambda j: (0, _lerp_colblock(j))),
                  pl.BlockSpec((1, _SHIFT_W), lambda j: (0, j))],
        out_specs=pl.BlockSpec((T, _SHIFT_W), lambda j: (0, j)),
        out_shape=jax.ShapeDtypeStruct((T, NLERP), F32),
        compiler_params=pltpu.CompilerParams(dimension_semantics=("parallel",), vmem_limit_bytes=VMEM_BIG),
    )(proj, mu)


def shift_bwd(proj, mu, g):
    def body(x_ref, mu_ref, g_ref, dx_ref, dmu_ref):
        x, gv = x_ref[...], g_ref[...]
        dmu_ref[...] = jnp.sum(gv * (_nbr_mean(x) - x), axis=0, keepdims=True)
        gm = gv * mu_ref[...]
        dx_ref[...] = (gv - gm + _nbr_mean(gm)).astype(dx_ref.dtype)

    col = pl.BlockSpec((T, _SHIFT_W), lambda j: (0, j))
    vec = pl.BlockSpec((1, _SHIFT_W), lambda j: (0, j))
    return pl.pallas_call(
        body, name="shift_bwd", grid=(NLERP // _SHIFT_W,),
        in_specs=[pl.BlockSpec((T, _SHIFT_W), lambda j: (0, _lerp_colblock(j))), vec, col],
        out_specs=[col, vec],
        out_shape=[jax.ShapeDtypeStruct((T, NLERP), BF16), jax.ShapeDtypeStruct((1, NLERP), F32)],
        compiler_params=pltpu.CompilerParams(dimension_semantics=("parallel",), vmem_limit_bytes=VMEM_BIG),
    )(proj, mu, g)


_TQ = 256
_ATT_SCALE = (NOPE + ROPE) ** -0.5


def _probs(q, k):
    s = _dg(q, k, 1, 1) * _ATT_SCALE
    e = jnp.exp(s - jnp.max(s, axis=-1, keepdims=True))
    return e * (1.0 / jnp.sum(e, axis=-1, keepdims=True))


_Q_BLK = pl.BlockSpec((_TQ, LANE), lambda h, i: (i, h))
_K_BLK = pl.BlockSpec((T, LANE), lambda h, i: (0, h))
_KR_BLK = pl.BlockSpec((T, LANE), lambda h, i: (0, 0))


def _load_qk(qn_ref, qr_ref, kn_ref, kr_ref, kcat_ref):
    @pl.when(pl.program_id(1) == 0)
    def _():
        kcat_ref[:, :LANE] = kn_ref[...]
        kcat_ref[:, LANE:] = kr_ref[...]

    return jnp.concatenate([qn_ref[...], qr_ref[...]], axis=1), kcat_ref[...]


def attn_fwd(qn, qr, kn, kr, v):
    def body(qn_ref, qr_ref, kn_ref, kr_ref, v_ref, o_ref, kcat_ref):
        q, k = _load_qk(qn_ref, qr_ref, kn_ref, kr_ref, kcat_ref)
        o_ref[...] = _dg(_probs(q, k).astype(BF16), v_ref[...], 1, 0)

    return pl.pallas_call(
        body, name="attn_fwd", grid=(HEADS, T // _TQ),
        in_specs=[_Q_BLK, _Q_BLK, _K_BLK, _KR_BLK, _K_BLK], out_specs=_Q_BLK,
        out_shape=jax.ShapeDtypeStruct((T, HEADS * VDIM), F32),
        scratch_shapes=[pltpu.VMEM((T, 2 * LANE), BF16)],
        compiler_params=pltpu.CompilerParams(dimension_semantics=("arbitrary", "arbitrary"), vmem_limit_bytes=VMEM_BIG),
    )(qn, qr, kn, kr, v)


def attn_bwd(qn, qr, kn, kr, v, do):
    def body(qn_ref, qr_ref, kn_ref, kr_ref, v_ref, do_ref, dqn_ref, dqr_ref, dkn_ref, dkr_ref, dv_ref, kcat_ref):
        h, i = pl.program_id(0), pl.program_id(1)

        @pl.when(i == 0)
        def _():
            dkn_ref[...] = jnp.zeros_like(dkn_ref)
            dv_ref[...] = jnp.zeros_like(dv_ref)

        @pl.when((i == 0) & (h == 0))
        def _():
            dkr_ref[...] = jnp.zeros_like(dkr_ref)

        q, k = _load_qk(qn_ref, qr_ref, kn_ref, kr_ref, kcat_ref)
        dob = do_ref[...].astype(BF16)
        p = _probs(q, k)
        dv_ref[...] += _dg(p.astype(BF16), dob, 0, 0)
        dp = _dg(dob, v_ref[...], 1, 1)
        ds = (p * (dp - jnp.sum(dp * p, axis=-1, keepdims=True)) * _ATT_SCALE).astype(BF16)
        dq = _dg(ds, k, 1, 0)
        dqn_ref[...] = dq[:, :LANE]
        dqr_ref[...] = dq[:, LANE:]
        dk = _dg(ds, q, 0, 0)
        dkn_ref[...] += dk[:, :LANE]
        dkr_ref[...] += dk[:, LANE:]

    wide = jax.ShapeDtypeStruct((T, HEADS * LANE), F32)
    return pl.pallas_call(
        body, name="attn_bwd", grid=(HEADS, T // _TQ),
        in_specs=[_Q_BLK, _Q_BLK, _K_BLK, _KR_BLK, _K_BLK, _Q_BLK],
        out_specs=[_Q_BLK, _Q_BLK, _K_BLK, _KR_BLK, _K_BLK],
        out_shape=[wide, wide, wide, jax.ShapeDtypeStruct((T, LANE), F32), wide],
        scratch_shapes=[pltpu.VMEM((T, 2 * LANE), BF16)],
        compiler_params=pltpu.CompilerParams(dimension_semantics=("arbitrary", "arbitrary"), vmem_limit_bytes=VMEM_BIG),
    )(qn, qr, kn, kr, v, do)


def _chunk(r, lw, k, v, a, b, ht, *, reverse):
    hb, c, _ = r.shape
    ti = lax.broadcasted_iota(jnp.int32, (c, c), 0)
    si = lax.broadcasted_iota(jnp.int32, (c, c), 1)
    incl = (si >= ti) if reverse else (si <= ti)
    strict = (si > ti) if reverse else (si < ti)
    ones = jnp.broadcast_to(incl.astype(F32)[None], (hb, c, c))
    cum = cumdot(ones, lw)
    cum_ex = cum - lw
    tot = jnp.sum(lw, axis=1, keepdims=True)
    mid = 0.5 * tot
    rt, at = r * jnp.exp(cum - mid), a * jnp.exp(cum_ex - mid)
    einv = jnp.exp(mid - cum)
    kt, bt = k * einv, b * einv
    m_ab = jnp.where(strict, nt(at, bt), 0.0)
    m_ak = jnp.where(strict, nt(at, kt), 0.0)
    m_rb = jnp.where(incl, nt(rt, bt), 0.0)
    m_rk = jnp.where(incl, nt(rt, kt), 0.0)
    u = tri_solve(m_ab, nt(a * jnp.exp(cum_ex), ht) + nn(m_ak, v))
    y = nt(r * jnp.exp(cum), ht) + nn(m_rb, u) + nn(m_rk, v)
    eend = jnp.exp(tot - cum)
    ht_new = ht * jnp.exp(tot) + tn(u, b * eend) + tn(v, k * eend)
    return y, ht_new


_HB_F, _HB_B = 16, 16


def _split_heads(x):
    return jnp.stack([x[:, i * RN:(i + 1) * RN] for i in range(x.shape[1] // RN)])


def _merge_heads(y):
    return jnp.concatenate([y[i] for i in range(y.shape[0])], axis=1)


def _chunk_map(reverse, backward):
    flip = reverse != backward
    return (lambda g, c: (NCH - 1 - c, g)) if flip else (lambda g, c: (c, g))


def scan_fwd(name, r, lw, k, v, a, b, reverse):
    hb = _HB_F
    cmap = _chunk_map(reverse, False)

    def body(r_ref, lw_ref, k_ref, v_ref, a_ref, b_ref, y_ref, h0_ref, ht_ref):
        @pl.when(pl.program_id(1) == 0)
        def _():
            ht_ref[...] = jnp.zeros_like(ht_ref)

        ht = ht_ref[...]
        h0_ref[0] = ht
        ins = [_split_heads(x[...]) for x in (r_ref, lw_ref, k_ref, v_ref, a_ref, b_ref)]
        y, hn = _chunk(*ins, ht, reverse=reverse)
        y_ref[...] = _merge_heads(y)
        ht_ref[...] = hn

    io = pl.BlockSpec((CHUNK, hb * RN), cmap)
    return pl.pallas_call(
        body, name=name, grid=(RH // hb, NCH),
        in_specs=[io] * 6,
        out_specs=[io, pl.BlockSpec((1, hb, RN, RN), lambda g, c: (cmap(g, c)[0], g, 0, 0))],
        out_shape=[jax.ShapeDtypeStruct((T, RW), F32), jax.ShapeDtypeStruct((NCH, RH, RN, RN), F32)],
        scratch_shapes=[pltpu.VMEM((hb, RN, RN), F32)],
        compiler_params=pltpu.CompilerParams(dimension_semantics=("parallel", "arbitrary"), vmem_limit_bytes=VMEM_BIG),
    )(r, lw, k, v, a, b)


def scan_bwd(name, r, lw, k, v, a, b, h0, dy, reverse):
    hb = _HB_B
    cmap = _chunk_map(reverse, True)

    def body(r_ref, lw_ref, k_ref, v_ref, a_ref, b_ref, h0_ref, dy_ref, *rest):
        d_refs, dht_ref = rest[:6], rest[6]

        @pl.when(pl.program_id(1) == 0)
        def _():
            dht_ref[...] = jnp.zeros_like(dht_ref)

        ins = [_split_heads(x[...]) for x in (r_ref, lw_ref, k_ref, v_ref, a_ref, b_ref)]
        _, vjp = jax.vjp(functools.partial(_chunk, reverse=reverse), *ins, h0_ref[0])
        grads = vjp((_split_heads(dy_ref[...]), dht_ref[...]))
        for d_ref, gval in zip(d_refs, grads[:6]):
            d_ref[...] = _merge_heads(gval)
        dht_ref[...] = grads[6]

    io = pl.BlockSpec((CHUNK, hb * RN), cmap)
    return pl.pallas_call(
        body, name=name, grid=(RH // hb, NCH),
        in_specs=[io] * 6 + [pl.BlockSpec((1, hb, RN, RN), lambda g, c: (cmap(g, c)[0], g, 0, 0)), io],
        out_specs=[io] * 6,
        out_shape=[jax.ShapeDtypeStruct((T, RW), F32)] * 6,
        scratch_shapes=[pltpu.VMEM((hb, RN, RN), F32)],
        compiler_params=pltpu.CompilerParams(dimension_semantics=("parallel", "arbitrary"), vmem_limit_bytes=VMEM_BIG),
    )(r, lw, k, v, a, b, h0, dy)


def loss_stage(out, x2, tgt, g_post):
    tr = 256

    def body(o_ref, x_ref, t_ref, g_ref, do_ref, dy_ref, dg_ref, loss_ref):
        @pl.when(pl.program_id(0) == 0)
        def _():
            dg_ref[...] = jnp.zeros_like(dg_ref)
            loss_ref[...] = jnp.zeros_like(loss_ref)

        nrm, vjp = jax.vjp(_rms, o_ref[...], g_ref[...])
        e = x_ref[...] + nrm - t_ref[...]
        s = jnp.sum(jnp.sum(e * e, axis=1, keepdims=True), axis=0, keepdims=True)
        loss_ref[...] += jnp.broadcast_to(s * (0.5 / D), loss_ref.shape)
        dy = e * (1.0 / D)
        do, dg = vjp(dy)
        do_ref[...] = do.astype(do_ref.dtype)
        dy_ref[...] = dy
        dg_ref[...] += dg

    row = pl.BlockSpec((tr, D), lambda i: (i, 0))
    return pl.pallas_call(
        body, name="loss_stage", grid=(T // tr,),
        in_specs=[row, row, row, pl.BlockSpec((1, D), lambda i: (0, 0))],
        out_specs=[row, row, pl.BlockSpec((1, D), lambda i: (0, 0)), pl.BlockSpec((8, LANE), lambda i: (0, 0))],
        out_shape=[jax.ShapeDtypeStruct((T, D), BF16), jax.ShapeDtypeStruct((T, D), F32),
                   jax.ShapeDtypeStruct((1, D), F32), jax.ShapeDtypeStruct((8, LANE), F32)],
        compiler_params=pltpu.CompilerParams(dimension_semantics=("arbitrary",), vmem_limit_bytes=VMEM_BIG),
    )(out, x2, tgt, g_post)


_EW_BLOCK_BYTES = 1 << 20


def _row_tile(rows, cols):
    best = None
    for tr in range(16, rows + 1, 16):
        if rows % tr == 0 and tr * cols * 4 <= _EW_BLOCK_BYTES:
            best = tr
    return best or rows


def _axis_tile(shape, axis, words):
    rows, cols = shape
    n, other, unit = (rows, cols, 16) if axis == 0 else (cols, rows, LANE)
    best = unit if n % unit == 0 else n
    for t in range(unit, n + 1, unit):
        if n % t == 0 and t * other * words * 4 <= _EW_BLOCK_BYTES:
            best = t
    blk = (best, cols) if axis == 0 else (rows, best)
    at = (lambda s: (s, 0)) if axis == 0 else (lambda s: (0, s))
    return blk, n // best, at


def _adamw_update(g, w_ref, m_ref, v_ref, g_ref, d_ref, nm_ref, nv_ref):
    mm = ADAM_B1 * m_ref[...] + (1.0 - ADAM_B1) * g
    vv = ADAM_B2 * v_ref[...] + (1.0 - ADAM_B2) * (g * g)
    m_hat = mm / (1.0 - ADAM_B1 ** ADAM_STEP)
    v_hat = vv / (1.0 - ADAM_B2 ** ADAM_STEP)
    g_ref[...] = g
    d_ref[...] = -ADAM_LR * (m_hat / (jnp.sqrt(v_hat) + ADAM_EPS) + ADAM_WD * w_ref[...])
    nm_ref[...] = mm
    nv_ref[...] = vv


def adamw(name, w, m, v, parts):
    rows, cols = w.shape
    br = _row_tile(rows, cols)
    npart = len(parts)

    def body(w_ref, m_ref, v_ref, *rest):
        g = rest[0][...].astype(F32)
        for p in rest[1:npart]:
            g = g + p[...].astype(F32)
        _adamw_update(g, w_ref, m_ref, v_ref, *rest[npart:])

    blk = pl.BlockSpec((br, cols), lambda i: (i, 0))
    return pl.pallas_call(
        body, name=name, grid=(rows // br,),
        in_specs=[blk] * (3 + npart), out_specs=[blk] * 4,
        out_shape=[jax.ShapeDtypeStruct((rows, cols), F32)] * 4,
        compiler_params=pltpu.CompilerParams(dimension_semantics=("parallel",), vmem_limit_bytes=VMEM_BIG),
    )(w, m, v, *parts)


def adamw_halves(name, place, w, m, v, mine, theirs, axis):
    half_shape = mine.shape
    blk_shape, nb, at = _axis_tile(half_shape, axis, 1)

    def body(p_ref, w_ref, m_ref, v_ref, a_ref, b_ref, *outs):
        own = (pl.program_id(0) // nb) == p_ref[0]
        _adamw_update(jnp.where(own, a_ref[...], b_ref[...]), w_ref, m_ref, v_ref, *outs)

    blk = pl.BlockSpec(blk_shape, lambda i, p: at(i))
    half = pl.BlockSpec(blk_shape, lambda i, p: at(i % nb))
    return pl.pallas_call(
        body, name=name,
        grid_spec=pltpu.PrefetchScalarGridSpec(num_scalar_prefetch=1, grid=(2 * nb,),
                                               in_specs=[blk] * 3 + [half] * 2, out_specs=[blk] * 4),
        out_shape=[jax.ShapeDtypeStruct(w.shape, F32)] * 4,
        compiler_params=pltpu.CompilerParams(dimension_semantics=("arbitrary",), vmem_limit_bytes=VMEM_BIG),
    )(place, w, m, v, mine, theirs)


def pair_sum(name, place, send, other, axis):
    blk_shape, nb, at = _axis_tile(other.shape[1:], axis, 4)

    def body(p_ref, a_ref, b_ref, o_ref):
        o_ref[...] = (a_ref[...].astype(F32) + b_ref[...].astype(F32)).astype(o_ref.dtype)

    blk = pl.BlockSpec((4,) + blk_shape, lambda i, p: (0,) + at(i))
    mine = pl.BlockSpec((4,) + blk_shape, lambda i, p: (0,) + at(p[0] * nb + i))
    return pl.pallas_call(
        body, name=name,
        grid_spec=pltpu.PrefetchScalarGridSpec(num_scalar_prefetch=1, grid=(nb,), in_specs=[mine, blk], out_specs=blk),
        out_shape=jax.ShapeDtypeStruct(other.shape, BF16),
        compiler_params=pltpu.CompilerParams(dimension_semantics=("arbitrary",), vmem_limit_bytes=VMEM_BIG),
    )(place, send, other)


def sum4(name, place, recv, own, axis):
    blk_shape, nb, at = _axis_tile(recv.shape[1:], axis, 4)

    def body(p_ref, r_ref, s_ref, o_ref):
        me = p_ref[0]
        t = [jnp.where(me == j, s_ref[j], r_ref[j]).astype(F32) for j in range(4)]
        o_ref[...] = ((t[0] + t[1]) + t[2]) + t[3]

    blk = pl.BlockSpec((4,) + blk_shape, lambda i, p: (0,) + at(i))
    return pl.pallas_call(
        body, name=name,
        grid_spec=pltpu.PrefetchScalarGridSpec(num_scalar_prefetch=1, grid=(nb,), in_specs=[blk, blk],
                                               out_specs=pl.BlockSpec(blk_shape, lambda i, p: at(i))),
        out_shape=jax.ShapeDtypeStruct(recv.shape[1:], F32),
        compiler_params=pltpu.CompilerParams(dimension_semantics=("arbitrary",), vmem_limit_bytes=VMEM_BIG),
    )(place, recv, own)


_ANY = pl.BlockSpec(memory_space=pl.ANY)


def _place():
    x, y, c = lax.axis_index("x"), lax.axis_index("y"), lax.axis_index("c")
    return x, y, c, 2 * x + y


def _chip_peers(x, y):
    out = []
    for k in (1, 2, 3):
        px = 1 - x if k & 2 else x
        py = 1 - y if k & 1 else y
        out.append((k, px, py, 2 * px + py))
    return out


def _half(c, shape, axis):
    n = shape[axis] // 2
    sl = pl.ds(pl.multiple_of(c * n, 16 if axis == 0 else LANE), n)
    return (sl,) if axis == 0 else (pl.ds(0, shape[0]), sl)


def gather_weights(srcs, axes):
    side = gather_side(srcs, axes)
    n = len(srcs)

    def body(*refs):
        ins, outs, sems = refs[:n], refs[n:2 * n], refs[2 * n:]
        side.start(ins, outs, sems)
        side.finish(ins, outs, sems)

    return pl.pallas_call(
        body, name="gather_weights", in_specs=[_ANY] * n, out_specs=[_ANY] * n,
        out_shape=side.outs, scratch_shapes=side.sems,
    )(*srcs)


def gather_side(srcs, axes):
    n = len(srcs)

    def copies(src, dst, sems, sends_only=False):
        ssem, rsem, fssem, frsem = sems
        x, y, c, me = _place()
        sib = (x, y, 1 - c)
        out = []
        for i in range(n):
            mine, other = _half(c, srcs[i].shape, axes[i]), _half(1 - c, srcs[i].shape, axes[i])
            for k, px, py, peer in _chip_peers(x, y):
                sems_k = dict(send_sem=ssem.at[i, k - 1], recv_sem=rsem.at[i, k - 1], device_id=(px, py, c),
                              device_id_type=MESH_IDS)
                fsems = dict(send_sem=fssem.at[i, k - 1], recv_sem=frsem.at[i, k - 1], device_id=sib,
                             device_id_type=MESH_IDS)
                got = dst[i].at[(peer,) + mine]
                snd = pltpu.make_async_remote_copy(src_ref=src[i].at[mine], dst_ref=dst[i].at[(me,) + mine], **sems_k)
                if sends_only:
                    out.append(snd)
                    continue
                out.append((
                    snd,
                    pltpu.make_async_remote_copy(src_ref=src[i].at[mine], dst_ref=got, **sems_k),
                    pltpu.make_async_remote_copy(src_ref=got, dst_ref=got, **fsems),
                    pltpu.make_async_remote_copy(src_ref=got, dst_ref=dst[i].at[(peer,) + other], **fsems)))
        return out

    def start(src, dst, sems):
        for snd in copies(src, dst, sems, sends_only=True):
            snd.start()

    def finish(src, dst, sems):
        cps = copies(src, dst, sems)
        for _, rcv, fwd, _ in cps:
            rcv.wait_recv()
            fwd.start()
        for snd, _, fwd, frcv in cps:
            frcv.wait_recv()
            snd.wait_send()
            fwd.wait_send()

    return Side(list(srcs), [jax.ShapeDtypeStruct((4,) + s.shape, s.dtype) for s in srcs],
                [pltpu.SemaphoreType.DMA((n, 3))] * 4, start, finish)


def pair_exchange(name, srcs, axes):
    n = len(srcs)

    def half_shape(s, axis):
        return (4, s.shape[1] // 2, s.shape[2]) if axis == 0 else (4, s.shape[1], s.shape[2] // 2)

    def body(*refs):
        src, other = refs[:n], refs[n:2 * n]
        ssem, rsem = refs[2 * n:]
        x, y, c, _ = _place()
        cps = []
        for i in range(n):
            idx = (pl.ds(0, 4),) + _half(1 - c, srcs[i].shape[1:], axes[i])
            cps.append(pltpu.make_async_remote_copy(
                src_ref=src[i].at[idx], dst_ref=other[i], send_sem=ssem.at[i], recv_sem=rsem.at[i],
                device_id=(x, y, 1 - c), device_id_type=MESH_IDS))
            cps[-1].start()
        for cp in cps:
            cp.wait()

    return pl.pallas_call(
        body, name=name, in_specs=[_ANY] * n, out_specs=[_ANY] * n,
        out_shape=[jax.ShapeDtypeStruct(half_shape(s, a), s.dtype) for s, a in zip(srcs, axes)],
        scratch_shapes=[pltpu.SemaphoreType.DMA((n,))] * 2,
    )(*srcs)


def scatter_grads(srcs):
    side = scatter_side(srcs)
    n = len(srcs)

    def body(*refs):
        ins, outs, sems = refs[:n], refs[n:2 * n], refs[2 * n:]
        side.start(ins, outs, sems)
        side.finish(ins, outs, sems)

    return pl.pallas_call(
        body, name="scatter_grads", in_specs=[_ANY] * n, out_specs=[_ANY] * n,
        out_shape=side.outs, scratch_shapes=side.sems,
    )(*srcs)


def scatter_side(srcs):
    n = len(srcs)

    def copies(src, dst, sems, sends_only=False):
        ssem, rsem = sems
        x, y, c, me = _place()
        out = []
        for i in range(n):
            for k, px, py, peer in _chip_peers(x, y):
                sems_k = dict(send_sem=ssem.at[i, k - 1], recv_sem=rsem.at[i, k - 1], device_id=(px, py, c),
                              device_id_type=MESH_IDS)
                snd = pltpu.make_async_remote_copy(src_ref=src[i].at[peer], dst_ref=dst[i].at[me], **sems_k)
                if sends_only:
                    out.append(snd)
                    continue
                out.append((snd, pltpu.make_async_remote_copy(src_ref=src[i].at[peer], dst_ref=dst[i].at[peer],
                                                              **sems_k)))
        return out

    def start(src, dst, sems):
        for snd in copies(src, dst, sems, sends_only=True):
            snd.start()

    def finish(src, dst, sems):
        for snd, rcv in copies(src, dst, sems):
            rcv.wait_recv()
            snd.wait_send()

    return Side(list(srcs), [jax.ShapeDtypeStruct(s.shape, s.dtype) for s in srcs],
                [pltpu.SemaphoreType.DMA((n, 3))] * 2, start, finish)


def swap_halves(srcs):
    n = len(srcs)

    def body(*refs):
        src, dst = refs[:n], refs[n:2 * n]
        ssem, rsem = refs[2 * n:]
        x, y, c, _ = _place()
        cps = []
        for i in range(n):
            cps.append(pltpu.make_async_remote_copy(src_ref=src[i], dst_ref=dst[i], send_sem=ssem.at[i],
                                                    recv_sem=rsem.at[i], device_id=(x, y, 1 - c),
                                                    device_id_type=MESH_IDS))
            cps[-1].start()
        for cp in cps:
            cp.wait()

    return pl.pallas_call(
        body, name="swap_halves", in_specs=[_ANY] * n, out_specs=[_ANY] * n,
        out_shape=[jax.ShapeDtypeStruct(s.shape, s.dtype) for s in srcs],
        scratch_shapes=[pltpu.SemaphoreType.DMA((n,))] * 2,
    )(*srcs)


def allgather8(name, src):
    rows = src.shape[0]

    def body(src_ref, dst_ref, send_sems, recv_sems):
        x, y, c = lax.axis_index("x"), lax.axis_index("y"), lax.axis_index("c")
        me = 4 * x + 2 * y + c
        dst_ref[me] = src_ref[...]
        sends, recvs = [], []
        for k in range(1, 8):
            px = 1 - x if k & 4 else x
            py = 1 - y if k & 2 else y
            pc = 1 - c if k & 1 else c
            peer = 4 * px + 2 * py + pc
            for lst, slot in ((sends, me), (recvs, peer)):
                lst.append(pltpu.make_async_remote_copy(
                    src_ref=src_ref, dst_ref=dst_ref.at[slot], send_sem=send_sems.at[k - 1],
                    recv_sem=recv_sems.at[k - 1], device_id=(px, py, pc), device_id_type=MESH_IDS))
        for cp in sends:
            cp.start()
        for cp in recvs:
            cp.wait_recv()
        for cp in sends:
            cp.wait_send()

    vm = pl.BlockSpec(memory_space=pltpu.VMEM)
    return pl.pallas_call(
        body, name=name, in_specs=[vm], out_specs=vm,
        out_shape=jax.ShapeDtypeStruct((8, rows, LANE), src.dtype),
        scratch_shapes=[pltpu.SemaphoreType.DMA((7,)), pltpu.SemaphoreType.DMA((7,))],
    )(src)


WEIGHTS = ['g_pre', 'w_in', 'mla_q_norm', 'mla_wq_b', 'mla_kv_norm', 'mla_wkv_b', 'rwkv_mu', 'rwkv_w0_f', 'rwkv_w2_f',
           'rwkv_w0_b', 'rwkv_w2_b', 'rwkv_a0_f', 'rwkv_a2_f', 'rwkv_a0_b', 'rwkv_a2_b', 'rwkv_k_k', 'rwkv_k_a',
           'rwkv_r_k', 'rwkv_gn_g', 'rwkv_gn_b', 'w_br_mla', 'w_br_rwkv', 'w_out', 'g_post']
BIG_SHAPES = {'w_in': (D_IN // 4, D), 'mla_wq_b': (Q_RANK, 384), 'mla_wkv_b': (KV_RANK, 512),
              'rwkv_w2_f': (LORA, 256), 'rwkv_w2_b': (LORA, 256), 'rwkv_a2_f': (LORA, 256), 'rwkv_a2_b': (LORA, 256),
              'w_br_mla': (RW, 512), 'w_br_rwkv': (RW, 512), 'w_out': (512, D)}
BIG = list(BIG_SHAPES)
SMALL = [n for n in WEIGHTS if n not in BIG_SHAPES]
SMALL_SHAPES = {'g_pre': (D,), 'mla_q_norm': (Q_RANK,), 'mla_kv_norm': (KV_RANK,), 'rwkv_mu': (3456,),
                'rwkv_w0_f': (RW,), 'rwkv_w0_b': (RW,), 'rwkv_a0_f': (RW,), 'rwkv_a0_b': (RW,), 'rwkv_k_k': (RW,),
                'rwkv_k_a': (RW,), 'rwkv_r_k': (RH, RN), 'rwkv_gn_g': (RW,), 'rwkv_gn_b': (RW,), 'g_post': (D,)}
SMALL_LEN = sum(int(np.prod(s)) for s in SMALL_SHAPES.values())
SMALL_ROWS = 144


UNITS = [('w_in',), ('mla_wq_b',), ('mla_wkv_b',), ('rwkv_w2_f', 'rwkv_w2_b', 'rwkv_a2_f', 'rwkv_a2_b'),
         ('w_br_mla', 'w_br_rwkv'), ('w_out',)]
UNIT_AXIS = [1, 0, 0, 0, 0, 0]
ROW_SHARDED = ('w_in', 'w_out')


def _unit_cat(parts):
    return parts[0] if len(parts) == 1 else jnp.concatenate(parts, axis=0)


def _unit_split(arr, names, axis):
    out, o = {}, 0
    for n in names:
        rows = BIG_SHAPES[n][0]
        out[n] = lax.slice_in_dim(arr, o, o + rows, axis=axis)
        o += rows
    return out


def _gathered(units, ag, own, me):
    out = {}
    for names, arr, mine in zip(units, ag, own):
        slots = [jnp.where(me == j, mine, arr[j]) for j in range(4)]
        for n in names:
            parts = [_unit_split(s, names, 0)[n] for s in slots]
            out[n] = jnp.concatenate(parts, axis=0 if n in ROW_SHARDED else 1)
    return out


def _shards(n, g):
    r, w = BIG_SHAPES[n]
    if n in ROW_SHARDED:
        return [g[j * r:(j + 1) * r] for j in range(4)]
    return [g[:, j * w:(j + 1) * w] for j in range(4)]


def _pack_small(d, extra=None):
    flat = jnp.concatenate([d[n].reshape(-1) for n in SMALL] + ([extra.reshape(-1)] if extra is not None else []))
    return jnp.pad(flat, (0, SMALL_ROWS * LANE - flat.shape[0])).reshape(SMALL_ROWS, LANE)


def _unpack_small(packed):
    flat, out, o = packed.reshape(-1), {}, 0
    for n in SMALL:
        sz = int(np.prod(SMALL_SHAPES[n]))
        out[n] = flat[o:o + sz].reshape(SMALL_SHAPES[n])
        o += sz
    return out


def _perm_w_in(w):
    z = lambda n: jnp.zeros((n, w.shape[1]), w.dtype)
    lora = []
    for i in range(4):
        lora += [w[4160 + LORA * i:4160 + LORA * (i + 1)], z(LANE - LORA)]
    return jnp.concatenate([w[0:1024], w[1088:4160], w[4544:D_IN]] + lora + [w[1024:1088], z(256 - ROPE)], axis=0)


def _unperm_w_in(g):
    lora = [g[OFF_LORA + LANE * i:OFF_LORA + LANE * i + LORA] for i in range(4)]
    return jnp.concatenate([g[0:1024], g[OFF_KR:OFF_KR + ROPE], g[1024:4096]] + lora + [g[4096:OFF_LORA]], axis=0)


def _perm_wq(w):
    w3 = w.reshape(Q_RANK, HEADS, NOPE + ROPE)
    rope = jnp.pad(w3[:, :, NOPE:], ((0, 0), (0, 0), (0, LANE - ROPE)))
    return jnp.concatenate([w3[:, :, :NOPE].reshape(Q_RANK, -1), rope.reshape(Q_RANK, -1)], axis=1)


def _unperm_wq(g):
    return jnp.concatenate([g[:, :1024].reshape(Q_RANK, HEADS, NOPE),
                            g[:, 1024:].reshape(Q_RANK, HEADS, LANE)[:, :, :ROPE]], axis=2).reshape(Q_RANK, -1)


def _perm_wkv(w):
    w3 = w.reshape(KV_RANK, HEADS, NOPE + VDIM)
    return jnp.concatenate([w3[:, :, :NOPE].reshape(KV_RANK, -1), w3[:, :, NOPE:].reshape(KV_RANK, -1)], axis=1)


def _unperm_wkv(g):
    return jnp.concatenate([g[:, :1024].reshape(KV_RANK, HEADS, NOPE), g[:, 1024:].reshape(KV_RANK, HEADS, VDIM)],
                           axis=2).reshape(KV_RANK, -1)


def _pad_rows(w):
    return jnp.pad(w, ((0, LANE - LORA), (0, 0)))


def _perm_mu(mu):
    parts = [mu[:3072]]
    for i in range(4):
        parts += [mu[3072 + LORA * i:3072 + LORA * (i + 1)], jnp.zeros((LANE - LORA,), mu.dtype)]
    return jnp.concatenate(parts).reshape(1, NLERP)


def _unperm_mu(g):
    g = g.reshape(-1)
    return jnp.concatenate([g[:3072]] + [g[3072 + LANE * i:3072 + LANE * i + LORA] for i in range(4)])


def _constants():
    g2 = np.kron(np.eye(2, dtype=np.float32), np.ones((RN, RN), np.float32))
    pos = jnp.arange(T, dtype=F32)
    inv_freq = jnp.power(ROPE_THETA, -jnp.arange(0, ROPE, 2, dtype=F32) / ROPE)
    ang = pos[:, None] * inv_freq[None, :]
    cos, sin, zero = jnp.cos(ang), jnp.sin(ang), jnp.zeros((T, LANE - ROPE), F32)
    cq = jnp.tile(jnp.concatenate([cos, cos, zero], axis=1), (1, HEADS))
    sq = jnp.tile(jnp.concatenate([-sin, sin, zero], axis=1), (1, HEADS))
    return jnp.asarray(g2, BF16), cq, sq


def _step(x, tgt, w, m, v):
    x2, tgt2 = x.reshape(T, D), tgt.reshape(T, D)
    g2, cq, sq = _constants()
    row = lambda n: w[n].reshape(1, -1)
    w, m, v = ({**t, 'w_in': t['w_in'].T} for t in (w, m, v))

    core, chip = lax.axis_index("c"), 2 * lax.axis_index("x") + lax.axis_index("y")
    core1, chip1 = core.astype(jnp.int32).reshape(1), chip.astype(jnp.int32).reshape(1)
    own_bf = [_unit_cat([w[n].astype(BF16) for n in u]) for u in UNITS]
    full = _gathered(UNITS[:1], gather_weights(own_bf[:1], UNIT_AXIS[:1]), own_bf[:1], chip)
    wp = _perm_w_in(full['w_in'])
    mu_p = _perm_mu(w['rwkv_mu'])

    st_pre = Stage("pre", f_pre, [(D, BF16), (D, None)], 256, [0], [0], [F32])
    st_mla = Stage("mla", f_mla, [(1024, BF16), (1024, BF16), (1024, BF16), (LANE, BF16), (1024, BF16)], 128,
                   [0, 1, 2], [0, 1, 2, 3], [BF16] * 3)
    st_rpre = Stage("rwkv_pre", f_rwkv_pre, [(RW, F32)] * 9, 128, [0], list(range(10)), [F32])
    st_rpost = Stage("rwkv_post", f_rwkv_post, [(RW, BF16)], 256, [0, 2, 3, 4, 5, 6], [0, 1, 2],
                     [F32, F32, F32, F32, F32, BF16])
    st_gate = Stage("gate", f_gate, [(RW, BF16)], 256, [0, 1], [], [F32, BF16])
    st_merge = Stage("merge", f_merge, [(D, BF16)], 256, [0, 1, 2, 3], [], [BF16] * 4)

    pre_rows, pre_par = [(x2, D, 0)], [row('g_pre')]
    (h,) = st_pre.fwd(pre_rows, pre_par)
    proj, rest = matmul("mm_in", h, wp, "nt", side=gather_side(own_bf[1:], UNIT_AXIS[1:]))
    full.update(_gathered(UNITS[1:], rest, own_bf[1:], chip))
    wq = _perm_wq(full['mla_wq_b']).astype(F32)
    wkv = _perm_wkv(full['mla_wkv_b']).astype(F32)
    lora_w = [_pad_rows(full[n]).astype(F32) for n in ('rwkv_w2_f', 'rwkv_w2_b', 'rwkv_a2_f', 'rwkv_a2_b')]

    mla_rows = [(proj, 512, OFF_QA // 512), (proj, 512, OFF_KVA // 512), (proj, 256, OFF_KR // 256),
                (cq, 1024, 0), (sq, 1024, 0), (cq, LANE, 0), (sq, LANE, 0)]
    mla_par = [row('mla_q_norm'), row('mla_kv_norm'), wq, wkv]
    att = st_mla.fwd(mla_rows, mla_par)
    y_mla = attn_fwd(*att)

    lerp = shift_fwd(proj, mu_p)
    rpre_rows = [(lerp, NLERP, 0)]
    rpre_par = [row('rwkv_w0_f'), row('rwkv_w0_b'), row('rwkv_a0_f'), row('rwkv_a0_b'), row('rwkv_k_k'),
                row('rwkv_k_a')] + lora_w + [g2]
    r_, v_, lwf, lwb, kf, kb, an, bf_, bb_ = st_rpre.fwd(rpre_rows, rpre_par)
    fin = [r_, lwf, kf, v_, an, bf_]
    bin_ = [r_, lwb, kb, v_, an, bb_]
    yf, h0f = scan_fwd("scan_f", *fin, reverse=False)
    yb, h0b = scan_fwd("scan_b", *bin_, reverse=True)
    rpost_rows = [(yf, RW, 0), (yb, RW, 0), (r_, RW, 0), (kf, RW, 0), (kb, RW, 0),
                  (v_, RW, 0), (proj, RW, OFF_ZR // RW)]
    rpost_par = [row('rwkv_gn_g'), row('rwkv_gn_b'), row('rwkv_r_k'), g2]
    (gr,) = st_rpost.fwd(rpost_rows, rpost_par)
    gate_rows = [(y_mla, RW, 0), (proj, RW, OFF_ZM // RW)]
    (gm,) = st_gate.fwd(gate_rows, [])
    um = matmul("mm_br_mla", gm, full['w_br_mla'], "nn")
    ur = matmul("mm_br_rwkv", gr, full['w_br_rwkv'], "nn")
    merge_rows = [(um, D, 0), (ur, D, 0), (proj, D, OFF_GM // D), (proj, D, OFF_GR // D)]
    (merged,) = st_merge.fwd(merge_rows, [])
    out = matmul("mm_out", merged, full['w_out'], "nn")
    d_out, dy, dg_post, loss_blk = loss_stage(out, x2, tgt2, row('g_post'))

    gw = {'g_post': dg_post}
    d_merged = matmul("mm_out_dx", d_out, full['w_out'], "nt")
    gw['w_out'] = matmul("mm_out_dw", merged, d_out, "tn")
    (d_um, d_ur, d_gm, d_gr), _ = st_merge.bwd(merge_rows, [], [[(d_merged, D, 0)]])
    d_gmla = matmul("mm_br_mla_dx", d_um, full['w_br_mla'], "nt")
    gw['w_br_mla'] = matmul("mm_br_mla_dw", gm, d_um, "tn")
    d_grw = matmul("mm_br_rwkv_dx", d_ur, full['w_br_rwkv'], "nt")
    gw['w_br_rwkv'] = matmul("mm_br_rwkv_dw", gr, d_ur, "tn")
    (d_ymla, d_zm), _ = st_gate.bwd(gate_rows, [], [[(d_gmla, RW, 0)]])
    (d_y, d_r3, d_kf2, d_kb2, d_v3, d_zr), (gw['rwkv_gn_g'], gw['rwkv_gn_b'], d_rk) = st_rpost.bwd(
        rpost_rows, rpost_par, [[(d_grw, RW, 0)]])
    gw['rwkv_r_k'] = d_rk
    sf = scan_bwd("scan_f_bwd", *fin, h0f, d_y, reverse=False)
    sb = scan_bwd("scan_b_bwd", *bin_, h0b, d_y, reverse=True)
    c = lambda *ts: [(t, RW, 0) for t in ts]
    rpre_cts = [c(sf[0], sb[0], d_r3), c(sf[3], sb[3], d_v3), c(sf[1]), c(sb[1]), c(sf[2], d_kf2), c(sb[2], d_kb2),
                c(sf[4], sb[4]), c(sf[5]), c(sb[5])]
    (d_rin,), rpre_g = st_rpre.bwd(rpre_rows, rpre_par, rpre_cts)
    for n, gval in zip(('rwkv_w0_f', 'rwkv_w0_b', 'rwkv_a0_f', 'rwkv_a0_b', 'rwkv_k_k', 'rwkv_k_a'), rpre_g[:6]):
        gw[n] = gval
    for n, gval in zip(('rwkv_w2_f', 'rwkv_w2_b', 'rwkv_a2_f', 'rwkv_a2_b'), rpre_g[6:]):
        gw[n] = gval[:LORA]
    d_lerp, d_mu = shift_bwd(proj, mu_p, d_rin)
    gw['rwkv_mu'] = _unperm_mu(d_mu)

    mla_cts = [[(t, t.shape[1], 0)] for t in attn_bwd(*att, d_ymla)]
    (d_qa, d_kva, d_kr), (gw['mla_q_norm'], gw['mla_kv_norm'], d_wq, d_wkv) = st_mla.bwd(mla_rows, mla_par, mla_cts)
    gw['mla_wq_b'], gw['mla_wkv_b'] = _unperm_wq(d_wq), _unperm_wkv(d_wkv)

    dproj = jnp.concatenate([d_qa, d_kva, d_lerp[:, :3072], d_zm, d_zr, d_gm, d_gr, d_lerp[:, 3072:], d_kr], axis=1)

    def pair_sums(name, ids):
        send = [jnp.stack([_unit_cat([_shards(n, gw[n])[j].astype(BF16) for n in UNITS[i]]) for j in range(4)])
                for i in ids]
        axes = [UNIT_AXIS[i] for i in ids]
        other = pair_exchange(name, send, axes)
        return [pair_sum(f"pair_sum_{i}", core1, s, o, ax) for i, s, o, ax in zip(ids, send, other, axes)]

    late, early = [0], list(range(1, len(UNITS)))
    pairs_e = pair_sums("pair_exchange_rest", early)
    gw_in, recv_e = matmul("mm_in_dw", dproj, h, "tn", BF16, side=scatter_side(pairs_e))
    gw['w_in'] = _unperm_w_in(gw_in)
    pairs_l = pair_sums("pair_exchange_w_in", late)
    dh, recv_l = matmul("mm_in_dx", dproj, wp, "nn", side=scatter_side(pairs_l))
    (grad_x,), (gw['g_pre'],) = st_pre.bwd(pre_rows, pre_par, [[(dh, D, 0)], [(dy, D, 0)]])
    recv, pairs = list(recv_l) + list(recv_e), pairs_l + pairs_e
    mine = [sum4(f"sum4_{i}", chip1, r, p, ax) for i, (r, p, ax) in enumerate(zip(recv, pairs, UNIT_AXIS))]
    theirs = swap_halves(mine)
    big = [dict() for _ in range(4)]
    for i, u in enumerate(UNITS):
        res = adamw_halves(f"adamw_{i}", core1, *[_unit_cat([t[n] for n in u]) for t in (w, m, v)], mine[i], theirs[i],
                           UNIT_AXIS[i])
        for q in range(4):
            big[q].update(_unit_split(res[q], u, 0))
    parts = allgather8("gather_small", _pack_small(gw, loss_blk[0, :1]))
    small = adamw("adamw_small", _pack_small(w), _pack_small(m), _pack_small(v), [parts[i] for i in range(8)])

    outs = []
    for b_d, s_arr in zip(big, small):
        d = {**b_d, **_unpack_small(s_arr)}
        d['w_in'] = d['w_in'].T
        outs.append([d[n] for n in WEIGHTS])
    loss = small[0][SMALL_LEN // LANE, 0]
    return (loss, grad_x.reshape(1, T, D), *outs[0], *outs[1], *outs[2], *outs[3])


def kernel(x, g_pre, w_in, mla_q_norm, mla_wq_b, mla_kv_norm, mla_wkv_b, rwkv_mu, rwkv_w0_f, rwkv_w2_f, rwkv_w0_b, rwkv_w2_b, rwkv_a0_f, rwkv_a2_f, rwkv_a0_b, rwkv_a2_b, rwkv_k_k, rwkv_k_a, rwkv_r_k, rwkv_gn_g, rwkv_gn_b, w_br_mla, w_br_rwkv, w_out, g_post, loss_target, m_g_pre, m_w_in, m_mla_q_norm, m_mla_wq_b, m_mla_kv_norm, m_mla_wkv_b, m_rwkv_mu, m_rwkv_w0_f, m_rwkv_w2_f, m_rwkv_w0_b, m_rwkv_w2_b, m_rwkv_a0_f, m_rwkv_a2_f, m_rwkv_a0_b, m_rwkv_a2_b, m_rwkv_k_k, m_rwkv_k_a, m_rwkv_r_k, m_rwkv_gn_g, m_rwkv_gn_b, m_w_br_mla, m_w_br_rwkv, m_w_out, m_g_post, v_g_pre, v_w_in, v_mla_q_norm, v_mla_wq_b, v_mla_kv_norm, v_mla_wkv_b, v_rwkv_mu, v_rwkv_w0_f, v_rwkv_w2_f, v_rwkv_w0_b, v_rwkv_w2_b, v_rwkv_a0_f, v_rwkv_a2_f, v_rwkv_a0_b, v_rwkv_a2_b, v_rwkv_k_k, v_rwkv_k_a, v_rwkv_r_k, v_rwkv_gn_g, v_rwkv_gn_b, v_w_br_mla, v_w_br_rwkv, v_w_out, v_g_post):
    given = dict(locals())
    w = {n: given[n] for n in WEIGHTS}
    m = {n: given['m_' + n] for n in WEIGHTS}
    v = {n: given['v_' + n] for n in WEIGHTS}
    return _step(x, loss_target, w, m, v)
```

```python
import functools
import math

import numpy as np
import jax
import jax.numpy as jnp
from jax import lax
from jax.experimental import pallas as pl
from jax.experimental.pallas import tpu as pltpu

F32, BF16 = jnp.float32, jnp.bfloat16
MESH_IDS = pl.DeviceIdType.MESH

D = 2048
T = 2048
HEADS = 8
Q_RANK = 512
KV_RANK = 512
NOPE = 128
ROPE = 64
VDIM = 128
RW = 1024
RH = 16
RN = 64
LORA = 96
D_IN = 10688
NORM_EPS = 1e-6
GN_EPS = 64e-5
ROPE_THETA = 10000.0
ADAM_LR, ADAM_B1, ADAM_B2, ADAM_EPS, ADAM_WD, ADAM_STEP = 0.001, 0.9, 0.999, 1e-08, 0.01, 10

LANE = 128
VMEM_BIG = 56 * 2**20

NP = 11008
OFF_QA, OFF_KVA, OFF_RKV, OFF_ZM, OFF_ZR, OFF_GM, OFF_GR, OFF_LORA, OFF_KR = 0, 512, 1024, 4096, 5120, 6144, 8192, 10240, 10752
NLERP = 3584

CHUNK = 64
NCH = T // CHUNK


def _dg(a, b, ca, cb, batch=False, prec=None):
    bd = ((0,), (0,)) if batch else ((), ())
    return lax.dot_general(a, b, (((ca,), (cb,)), bd), precision=prec, preferred_element_type=F32)


@jax.custom_vjp
def bdot(a, b):
    return _dg(a.astype(BF16), b.astype(BF16), 1, 0)


def _bdot_fwd(a, b):
    return bdot(a, b), (a, b)


def _bdot_bwd(res, g):
    a, b = res
    gb = g.astype(BF16)
    da = _dg(gb, b.astype(BF16), 1, 1)
    db = _dg(a.astype(BF16), gb, 0, 0)
    return da.astype(a.dtype), db.astype(b.dtype)


bdot.defvjp(_bdot_fwd, _bdot_bwd)


def _split(x):
    hi = x.astype(BF16)
    lo = (x - hi.astype(F32)).astype(BF16)
    return hi, lo


@jax.custom_vjp
def gsum(x, g2):
    hi, lo = _split(x)
    return _dg(hi, g2, 1, 0) + _dg(lo, g2, 1, 0)


def _gsum_fwd(x, g2):
    return gsum(x, g2), g2


def _gsum_bwd(g2, g):
    hi, lo = _split(g)
    return _dg(hi, g2, 1, 1) + _dg(lo, g2, 1, 1), jnp.zeros_like(g2)


gsum.defvjp(_gsum_fwd, _gsum_bwd)


def headsum(x, g2):
    return jnp.concatenate([gsum(x[:, i * LANE:(i + 1) * LANE], g2) for i in range(x.shape[1] // LANE)], axis=1)


def _terms(x, n):
    out = []
    for i in range(n):
        t = x.astype(BF16)
        out.append(t)
        if i < n - 1:
            x = x - t.astype(F32)
    return out


def _bmm(a, b, ca, cb, na, nb):
    acc = None
    for i, ai in enumerate(_terms(a, na)):
        for j, bj in enumerate(_terms(b, nb)):
            if i + j < max(na, nb):
                p = _dg(ai, bj, ca, cb, True)
                acc = p if acc is None else acc + p
    return acc


_NN, _NT, _TN = (2, 1), (2, 2), (1, 1)


def _make_dots(nf, nb_nn, nb_nt, nb_tn):
    @jax.custom_vjp
    def nn(a, b):
        return _bmm(a, b, *_NN, nf, nf)

    @jax.custom_vjp
    def nt(a, b):
        return _bmm(a, b, *_NT, nf, nf)

    @jax.custom_vjp
    def tn(a, b):
        return _bmm(a, b, *_TN, nf, nf)

    nn.defvjp(lambda a, b: (nn(a, b), (a, b)),
              lambda r, g: (_bmm(g, r[1], *_NT, nb_nn, nb_nn), _bmm(r[0], g, *_TN, nb_nn, nb_nn)))
    nt.defvjp(lambda a, b: (nt(a, b), (a, b)),
              lambda r, g: (_bmm(g, r[1], *_NN, 1, nb_nt), _bmm(g, r[0], *_TN, 1, nb_nt)))
    tn.defvjp(lambda a, b: (tn(a, b), (a, b)),
              lambda r, g: (_bmm(r[1], g, *_NT, nb_tn, nb_tn), _bmm(r[0], g, *_NN, nb_tn, nb_tn)))
    return nn, nt, tn


_SCAN_NF, _SCAN_NB = 1, 1
nn, nt, tn = _make_dots(_SCAN_NF, 1, 2, 1)


@jax.custom_vjp
def cumdot(ones, x):
    return _bmm(ones, x, *_NN, 1, 3)


cumdot.defvjp(lambda o, x: (cumdot(o, x), o), lambda o, g: (jnp.zeros_like(o), _bmm(o, g, *_TN, 1, 3)))


def _solve_powers(l):
    pw = [l]
    for _ in range(int(math.log2(l.shape[-1])) - 1):
        pw.append(_bmm(pw[-1], pw[-1], *_NN, _SCAN_NF, _SCAN_NF))
    return pw


@jax.custom_vjp
def tri_solve(l, rhs):
    x = rhs
    for p in _solve_powers(l):
        x = x + _bmm(p, x, *_NN, _SCAN_NF, _SCAN_NF)
    return x


def _tri_solve_fwd(l, rhs):
    pw = _solve_powers(l)
    x = rhs
    for p in pw:
        x = x + _bmm(p, x, *_NN, _SCAN_NF, _SCAN_NF)
    return x, (pw, x)


def _tri_solve_bwd(res, g):
    pw, x = res
    y = g
    for p in pw:
        y = y + _bmm(p, y, *_TN, _SCAN_NB, _SCAN_NB)
    return _bmm(y, x, *_NT, _SCAN_NB, _SCAN_NB), y


tri_solve.defvjp(_tri_solve_fwd, _tri_solve_bwd)


def _rms(x, g):
    return x * lax.rsqrt(jnp.mean(x * x, axis=-1, keepdims=True) + NORM_EPS) * g


def _softplus(x):
    pos = x > 0
    return jnp.where(pos, x, 0.0) + jnp.log(1.0 + jnp.exp(-jnp.where(pos, x, -x)))


def _silu(z):
    return z * jax.nn.sigmoid(z)


def _tile(n, cands):
    for c in cands:
        if n % c == 0:
            return c
    raise ValueError(n)


_MM_VMEM_BYTES = 32 * 2**20


def _mm_tiles(m, n, k):
    best = None
    for tm in (2048, 1024, 512, 256):
        for tn_ in (2048, 1024, 512, 256):
            for d in range(k // LANE, 0, -1):
                tk = LANE * d
                if m % tm or n % tn_ or k % tk:
                    continue
                nk = k // tk
                vmem = 4 * tk * (tm + tn_) + 8 * tm * tn_ + (4 * tm * tn_ if nk > 1 else 0)
                if vmem > _MM_VMEM_BYTES:
                    continue
                a_reads = n // tn_ if nk > 1 else 1
                b_reads = 1 if (nk == 1 and n == tn_) else m // tm
                acc_rmw = nk * m * n if nk > 1 else 0
                cost = (a_reads * m * k + b_reads * k * n + acc_rmw, -tm * tn_ * tk)
                if best is None or cost < best[0]:
                    best = (cost, (tm, tn_, tk))
    return best[1]


class Side:
    def __init__(self, ins, outs, sems, start, finish):
        self.ins, self.outs, self.sems, self.start, self.finish = ins, outs, sems, start, finish


def matmul(name, a, b, mode, out_dtype=F32, side=None, after=()):
    if mode == "nn":
        (m, k), n = a.shape, b.shape[1]
    elif mode == "nt":
        (m, k), n = a.shape, b.shape[0]
    else:
        (k, m), n = a.shape, b.shape[1]
    tm, tn_, tk = _mm_tiles(m, n, k)
    nk = k // tk
    if mode == "nn":
        a_spec = pl.BlockSpec((tm, tk), lambda i, j, kk: (i, kk))
        b_spec = pl.BlockSpec((tk, tn_), lambda i, j, kk: (kk, j))
        ca, cb = 1, 0
    elif mode == "nt":
        a_spec = pl.BlockSpec((tm, tk), lambda i, j, kk: (i, kk))
        b_spec = pl.BlockSpec((tn_, tk), lambda i, j, kk: (j, kk))
        ca, cb = 1, 1
    else:
        a_spec = pl.BlockSpec((tk, tm), lambda i, j, kk: (kk, i))
        b_spec = pl.BlockSpec((tk, tn_), lambda i, j, kk: (kk, j))
        ca, cb = 0, 0

    grid = (m // tm, n // tn_, nk)
    n_in = len(side.ins) if side else 0
    n_out = len(side.outs) if side else 0
    n_dep = len(after)

    def body(a_ref, b_ref, *rest):
        rest = rest[n_dep:]
        s_ins, o_ref, s_outs = rest[:n_in], rest[n_in], rest[n_in + 1:n_in + 1 + n_out]
        scratch = rest[n_in + 1 + n_out:]
        acc, s_sems = (scratch[:1], scratch[1:]) if nk > 1 else ((), scratch)
        if side:
            step = (pl.program_id(0) * grid[1] + pl.program_id(1)) * grid[2] + pl.program_id(2)

            @pl.when(step == 0)
            def _():
                side.start(s_ins, s_outs, s_sems)

        part = _dg(a_ref[...].astype(BF16), b_ref[...].astype(BF16), ca, cb)
        if nk == 1:
            o_ref[...] = part.astype(o_ref.dtype)
        else:
            acc_ref, kk = acc[0], pl.program_id(2)

            @pl.when(kk == 0)
            def _():
                acc_ref[...] = part

            @pl.when(kk > 0)
            def _():
                acc_ref[...] += part

            @pl.when(kk == nk - 1)
            def _():
                o_ref[...] = acc_ref[...].astype(o_ref.dtype)

        if side:
            @pl.when(step == grid[0] * grid[1] * grid[2] - 1)
            def _():
                side.finish(s_ins, s_outs, s_sems)

    res = pl.pallas_call(
        body, name=name, grid=grid,
        in_specs=[a_spec, b_spec] + [_ANY] * (n_dep + n_in),
        out_specs=[pl.BlockSpec((tm, tn_), lambda i, j, kk: (i, j))] + [_ANY] * n_out,
        out_shape=[jax.ShapeDtypeStruct((m, n), out_dtype)] + (list(side.outs) if side else []),
        scratch_shapes=([pltpu.VMEM((tm, tn_), F32)] if nk > 1 else []) + (list(side.sems) if side else []),
        compiler_params=pltpu.CompilerParams(
            dimension_semantics=("arbitrary",) * 3 if side else ("parallel", "parallel", "arbitrary"),
            vmem_limit_bytes=VMEM_BIG),
    )(a, b, *after, *(side.ins if side else []))
    return (res[0], res[1:]) if side else res[0]


def _rspec(tr, width, blk):
    return pl.BlockSpec((tr, width), lambda i: (i, blk))


def _full_spec(arr):
    return pl.BlockSpec(arr.shape, lambda i: (0,) * arr.ndim)


class Stage:
    def __init__(self, name, f, outs, tr, diff_rows, diff_params, drow_dtypes):
        self.name, self.f, self.outs, self.tr = name, f, outs, tr
        self.diff_rows, self.diff_params, self.drow_dtypes = diff_rows, diff_params, drow_dtypes

    def fwd(self, rows, params):
        f, nr, npar = self.f, len(rows), len(params)
        stored = [(w, dt) for (w, dt) in self.outs if dt is not None]
        keep = [i for i, (w, dt) in enumerate(self.outs) if dt is not None]

        def body(*refs):
            vals = f(*[r[...].astype(F32) for r in refs[:nr]], *[p[...] for p in refs[nr:nr + npar]])
            for o_ref, i in zip(refs[nr + npar:], keep):
                o_ref[...] = vals[i].astype(o_ref.dtype)

        return pl.pallas_call(
            body, name=self.name + "_fwd", grid=(T // self.tr,),
            in_specs=[_rspec(self.tr, w, b) for (_, w, b) in rows] + [_full_spec(p) for p in params],
            out_specs=[_rspec(self.tr, w, 0) for (w, _) in stored],
            out_shape=[jax.ShapeDtypeStruct((T, w), dt) for (w, dt) in stored],
            compiler_params=pltpu.CompilerParams(dimension_semantics=("arbitrary",), vmem_limit_bytes=VMEM_BIG),
        )(*[r[0] for r in rows], *params)

    def bwd(self, rows, params, cts):
        f, nr, npar = self.f, len(rows), len(params)
        dr_idx, dp_idx = self.diff_rows, self.diff_params
        flat_cts = [c for lst in cts for c in lst]
        nct = len(flat_cts)

        def body(*refs):
            row_refs, par_refs = refs[:nr], refs[nr:nr + npar]
            ct_refs = refs[nr + npar:nr + npar + nct]
            drow_refs = refs[nr + npar + nct:nr + npar + nct + len(dr_idx)]
            dpar_refs = refs[nr + npar + nct + len(dr_idx):]
            row_vals = [r[...].astype(F32) for r in row_refs]
            par_vals = [p[...] for p in par_refs]

            def g(*dv):
                rv, pv = list(row_vals), list(par_vals)
                for j, i in enumerate(dr_idx):
                    rv[i] = dv[j]
                for j, i in enumerate(dp_idx):
                    pv[i] = dv[len(dr_idx) + j]
                return f(*rv, *pv)

            _, vjp = jax.vjp(g, *[row_vals[i] for i in dr_idx], *[par_vals[i] for i in dp_idx])
            ct_vals, pos = [], 0
            for lst in cts:
                acc = ct_refs[pos][...].astype(F32)
                for q in range(1, len(lst)):
                    acc = acc + ct_refs[pos + q][...].astype(F32)
                pos += len(lst)
                ct_vals.append(acc)
            grads = vjp(tuple(ct_vals))
            for j, r in enumerate(drow_refs):
                r[...] = grads[j].astype(r.dtype)

            @pl.when(pl.program_id(0) == 0)
            def _():
                for r in dpar_refs:
                    r[...] = jnp.zeros_like(r)

            for j, r in enumerate(dpar_refs):
                r[...] += grads[len(dr_idx) + j].astype(F32)

        drow_shapes = [jax.ShapeDtypeStruct((T, rows[i][1]), dt) for i, dt in zip(dr_idx, self.drow_dtypes)]
        dpar_shapes = [jax.ShapeDtypeStruct(params[i].shape, F32) for i in dp_idx]
        res = pl.pallas_call(
            body, name=self.name + "_bwd", grid=(T // self.tr,),
            in_specs=[_rspec(self.tr, w, b) for (_, w, b) in rows] + [_full_spec(p) for p in params]
            + [_rspec(self.tr, w, b) for (_, w, b) in flat_cts],
            out_specs=[_rspec(self.tr, rows[i][1], 0) for i in dr_idx] + [_full_spec(params[i]) for i in dp_idx],
            out_shape=drow_shapes + dpar_shapes,
            compiler_params=pltpu.CompilerParams(dimension_semantics=("arbitrary",), vmem_limit_bytes=VMEM_BIG),
        )(*[r[0] for r in rows], *params, *[c[0] for c in flat_cts])
        return res[:len(dr_idx)], res[len(dr_idx):]


def f_pre(x, g):
    return _rms(x, g), x


@jax.custom_vjp
def swap32(t):
    width = t.shape[1]
    lane = lax.broadcasted_iota(jnp.int32, t.shape, 1) % LANE
    return jnp.where(lane < 32, pltpu.roll(t, width - 32, 1), jnp.where(lane < 64, pltpu.roll(t, 32, 1), 0.0))


swap32.defvjp(lambda t: (swap32(t), None), lambda _, g: (swap32(g),))


def f_mla(q_a, kv_a, kr, cq, sq, ck, sk, gq, gkv, wq, wkv):
    q = bdot(_rms(q_a, gq), wq)
    kv = bdot(_rms(kv_a, gkv), wkv)
    t, k = q[:, 1024:], kr[:, :LANE]
    return (q[:, :1024], t * cq + swap32(t) * sq, kv[:, :1024], k * ck + swap32(k) * sk, kv[:, 1024:])


def f_rwkv_pre(lerp, w0f, w0b, a0f, a0b, kkw, kaw, w2f, w2b, a2f, a2b, g2):
    r, k, v = lerp[:, :RW], lerp[:, RW:2 * RW], lerp[:, 2 * RW:3 * RW]
    wdf, wdb, adf, adb = (lerp[:, 3 * RW + i * LANE:3 * RW + (i + 1) * LANE] for i in range(4))

    def logdecay(w0, wd, w2):
        z = w0 + bdot(jnp.tanh(wd), w2)
        return -jnp.exp(-_softplus(-z) - 0.5)

    a_f = jax.nn.sigmoid(a0f + bdot(adf, a2f))
    a_b = jax.nn.sigmoid(a0b + bdot(adb, a2b))
    kk = k * kkw
    kk = kk / jnp.maximum(jnp.sqrt(headsum(kk * kk, g2)), 1e-12)
    return (r, v, logdecay(w0f, wdf, w2f), logdecay(w0b, wdb, w2b),
            k * (1.0 + (a_f - 1.0) * kaw), k * (1.0 + (a_b - 1.0) * kaw), -kk, kk * a_f, kk * a_b)


def f_rwkv_post(yf, yb, r, kf, kb, v, z, gng, gnb, rk, g2):
    y = yf + yb
    mu = headsum(y, g2) * (1.0 / RN)
    d = y - mu
    var = headsum(d * d, g2) * (1.0 / RN)
    yn = d * lax.rsqrt(var + GN_EPS) * gng + gnb
    bonus = headsum(r * (kf + kb) * rk, g2) * v
    return ((yn + bonus) * _silu(z),)


def f_gate(y, z):
    return (y * _silu(z),)


def f_merge(um, ur, gm, gr):
    return (jax.nn.sigmoid(gm) * um + jax.nn.sigmoid(gr) * ur,)


_SHIFT_W = 256


def _lerp_colblock(j):
    return jnp.where(j < 3072 // _SHIFT_W, OFF_RKV // _SHIFT_W + j, OFF_LORA // _SHIFT_W + j - 3072 // _SHIFT_W)


def _nbr_mean(x):
    row = lax.broadcasted_iota(jnp.int32, x.shape, 0)
    up = jnp.where(row == 0, 0.0, pltpu.roll(x, 1, 0))
    dn = jnp.where(row == T - 1, 0.0, pltpu.roll(x, T - 1, 0))
    return 0.5 * (up + dn)


def shift_fwd(proj, mu):
    def body(x_ref, mu_ref, o_ref):
        x = x_ref[...]
        o_ref[...] = x + mu_ref[...] * (_nbr_mean(x) - x)

    return pl.pallas_call(
        body, name="shift_fwd", grid=(NLERP // _SHIFT_W,),
        in_specs=[pl.BlockSpec((T, _SHIFT_W), lambda j: (0, _lerp_colblock(j))),
                  pl.BlockSpec((1, _SHIFT_W), lambda j: (0, j))],
        out_specs=pl.BlockSpec((T, _SHIFT_W), lambda j: (0, j)),
        out_shape=jax.ShapeDtypeStruct((T, NLERP), F32),
        compiler_params=pltpu.CompilerParams(dimension_semantics=("parallel",), vmem_limit_bytes=VMEM_BIG),
    )(proj, mu)


def shift_bwd(proj, mu, g):
    def body(x_ref, mu_ref, g_ref, dx_ref, dmu_ref):
        x, gv = x_ref[...], g_ref[...]
        dmu_ref[...] = jnp.sum(gv * (_nbr_mean(x) - x), axis=0, keepdims=True)
        gm = gv * mu_ref[...]
        dx_ref[...] = (gv - gm + _nbr_mean(gm)).astype(dx_ref.dtype)

    col = pl.BlockSpec((T, _SHIFT_W), lambda j: (0, j))
    vec = pl.BlockSpec((1, _SHIFT_W), lambda j: (0, j))
    return pl.pallas_call(
        body, name="shift_bwd", grid=(NLERP // _SHIFT_W,),
        in_specs=[pl.BlockSpec((T, _SHIFT_W), lambda j: (0, _lerp_colblock(j))), vec, col],
        out_specs=[col, vec],
        out_shape=[jax.ShapeDtypeStruct((T, NLERP), BF16), jax.ShapeDtypeStruct((1, NLERP), F32)],
        compiler_params=pltpu.CompilerParams(dimension_semantics=("parallel",), vmem_limit_bytes=VMEM_BIG),
    )(proj, mu, g)


_TQ = 256
_ATT_SCALE = (NOPE + ROPE) ** -0.5


def _probs(q, k):
    s = _dg(q, k, 1, 1) * _ATT_SCALE
    e = jnp.exp(s - jnp.max(s, axis=-1, keepdims=True))
    return e * (1.0 / jnp.sum(e, axis=-1, keepdims=True))


_Q_BLK = pl.BlockSpec((_TQ, LANE), lambda h, i: (i, h))
_K_BLK = pl.BlockSpec((T, LANE), lambda h, i: (0, h))
_KR_BLK = pl.BlockSpec((T, LANE), lambda h, i: (0, 0))


def _load_qk(qn_ref, qr_ref, kn_ref, kr_ref, kcat_ref):
    @pl.when(pl.program_id(1) == 0)
    def _():
        kcat_ref[:, :LANE] = kn_ref[...]
        kcat_ref[:, LANE:] = kr_ref[...]

    return jnp.concatenate([qn_ref[...], qr_ref[...]], axis=1), kcat_ref[...]


def attn_fwd(qn, qr, kn, kr, v):
    def body(qn_ref, qr_ref, kn_ref, kr_ref, v_ref, o_ref, kcat_ref):
        q, k = _load_qk(qn_ref, qr_ref, kn_ref, kr_ref, kcat_ref)
        o_ref[...] = _dg(_probs(q, k).astype(BF16), v_ref[...], 1, 0)

    return pl.pallas_call(
        body, name="attn_fwd", grid=(HEADS, T // _TQ),
        in_specs=[_Q_BLK, _Q_BLK, _K_BLK, _KR_BLK, _K_BLK], out_specs=_Q_BLK,
        out_shape=jax.ShapeDtypeStruct((T, HEADS * VDIM), F32),
        scratch_shapes=[pltpu.VMEM((T, 2 * LANE), BF16)],
        compiler_params=pltpu.CompilerParams(dimension_semantics=("arbitrary", "arbitrary"), vmem_limit_bytes=VMEM_BIG),
    )(qn, qr, kn, kr, v)


def attn_bwd(qn, qr, kn, kr, v, do):
    def body(qn_ref, qr_ref, kn_ref, kr_ref, v_ref, do_ref, dqn_ref, dqr_ref, dkn_ref, dkr_ref, dv_ref, kcat_ref):
        h, i = pl.program_id(0), pl.program_id(1)

        @pl.when(i == 0)
        def _():
            dkn_ref[...] = jnp.zeros_like(dkn_ref)
            dv_ref[...] = jnp.zeros_like(dv_ref)

        @pl.when((i == 0) & (h == 0))
        def _():
            dkr_ref[...] = jnp.zeros_like(dkr_ref)

        q, k = _load_qk(qn_ref, qr_ref, kn_ref, kr_ref, kcat_ref)
        dob = do_ref[...].astype(BF16)
        p = _probs(q, k)
        dv_ref[...] += _dg(p.astype(BF16), dob, 0, 0)
        dp = _dg(dob, v_ref[...], 1, 1)
        ds = (p * (dp - jnp.sum(dp * p, axis=-1, keepdims=True)) * _ATT_SCALE).astype(BF16)
        dq = _dg(ds, k, 1, 0)
        dqn_ref[...] = dq[:, :LANE]
        dqr_ref[...] = dq[:, LANE:]
        dk = _dg(ds, q, 0, 0)
        dkn_ref[...] += dk[:, :LANE]
        dkr_ref[...] += dk[:, LANE:]

    wide = jax.ShapeDtypeStruct((T, HEADS * LANE), F32)
    return pl.pallas_call(
        body, name="attn_bwd", grid=(HEADS, T // _TQ),
        in_specs=[_Q_BLK, _Q_BLK, _K_BLK, _KR_BLK, _K_BLK, _Q_BLK],
        out_specs=[_Q_BLK, _Q_BLK, _K_BLK, _KR_BLK, _K_BLK],
        out_shape=[wide, wide, wide, jax.ShapeDtypeStruct((T, LANE), F32), wide],
        scratch_shapes=[pltpu.VMEM((T, 2 * LANE), BF16)],
        compiler_params=pltpu.CompilerParams(dimension_semantics=("arbitrary", "arbitrary"), vmem_limit_bytes=VMEM_BIG),
    )(qn, qr, kn, kr, v, do)


def _chunk(r, lw, k, v, a, b, ht, *, reverse):
    hb, c, _ = r.shape
    ti = lax.broadcasted_iota(jnp.int32, (c, c), 0)
    si = lax.broadcasted_iota(jnp.int32, (c, c), 1)
    incl = (si >= ti) if reverse else (si <= ti)
    strict = (si > ti) if reverse else (si < ti)
    ones = jnp.broadcast_to(incl.astype(F32)[None], (hb, c, c))
    cum = cumdot(ones, lw)
    cum_ex = cum - lw
    tot = jnp.sum(lw, axis=1, keepdims=True)
    mid = 0.5 * tot
    rt, at = r * jnp.exp(cum - mid), a * jnp.exp(cum_ex - mid)
    einv = jnp.exp(mid - cum)
    kt, bt = k * einv, b * einv
    m_ab = jnp.where(strict, nt(at, bt), 0.0)
    m_ak = jnp.where(strict, nt(at, kt), 0.0)
    m_rb = jnp.where(incl, nt(rt, bt), 0.0)
    m_rk = jnp.where(incl, nt(rt, kt), 0.0)
    u = tri_solve(m_ab, nt(a * jnp.exp(cum_ex), ht) + nn(m_ak, v))
    y = nt(r * jnp.exp(cum), ht) + nn(m_rb, u) + nn(m_rk, v)
    eend = jnp.exp(tot - cum)
    ht_new = ht * jnp.exp(tot) + tn(u, b * eend) + tn(v, k * eend)
    return y, ht_new


_HB_F, _HB_B = 16, 16


def _split_heads(x):
    return jnp.stack([x[:, i * RN:(i + 1) * RN] for i in range(x.shape[1] // RN)])


def _merge_heads(y):
    return jnp.concatenate([y[i] for i in range(y.shape[0])], axis=1)


def _chunk_map(reverse, backward):
    flip = reverse != backward
    return (lambda g, c: (NCH - 1 - c, g)) if flip else (lambda g, c: (c, g))


def scan_fwd(name, r, lw, k, v, a, b, reverse):
    hb = _HB_F
    cmap = _chunk_map(reverse, False)

    def body(r_ref, lw_ref, k_ref, v_ref, a_ref, b_ref, y_ref, h0_ref, ht_ref):
        @pl.when(pl.program_id(1) == 0)
        def _():
            ht_ref[...] = jnp.zeros_like(ht_ref)

        ht = ht_ref[...]
        h0_ref[0] = ht
        ins = [_split_heads(x[...]) for x in (r_ref, lw_ref, k_ref, v_ref, a_ref, b_ref)]
        y, hn = _chunk(*ins, ht, reverse=reverse)
        y_ref[...] = _merge_heads(y)
        ht_ref[...] = hn

    io = pl.BlockSpec((CHUNK, hb * RN), cmap)
    return pl.pallas_call(
        body, name=name, grid=(RH // hb, NCH),
        in_specs=[io] * 6,
        out_specs=[io, pl.BlockSpec((1, hb, RN, RN), lambda g, c: (cmap(g, c)[0], g, 0, 0))],
        out_shape=[jax.ShapeDtypeStruct((T, RW), F32), jax.ShapeDtypeStruct((NCH, RH, RN, RN), F32)],
        scratch_shapes=[pltpu.VMEM((hb, RN, RN), F32)],
        compiler_params=pltpu.CompilerParams(dimension_semantics=("parallel", "arbitrary"), vmem_limit_bytes=VMEM_BIG),
    )(r, lw, k, v, a, b)


def scan_bwd(name, r, lw, k, v, a, b, h0, dy, reverse):
    hb = _HB_B
    cmap = _chunk_map(reverse, True)

    def body(r_ref, lw_ref, k_ref, v_ref, a_ref, b_ref, h0_ref, dy_ref, *rest):
        d_refs, dht_ref = rest[:6], rest[6]

        @pl.when(pl.program_id(1) == 0)
        def _():
            dht_ref[...] = jnp.zeros_like(dht_ref)

        ins = [_split_heads(x[...]) for x in (r_ref, lw_ref, k_ref, v_ref, a_ref, b_ref)]
        _, vjp = jax.vjp(functools.partial(_chunk, reverse=reverse), *ins, h0_ref[0])
        grads = vjp((_split_heads(dy_ref[...]), dht_ref[...]))
        for d_ref, gval in zip(d_refs, grads[:6]):
            d_ref[...] = _merge_heads(gval)
        dht_ref[...] = grads[6]

    io = pl.BlockSpec((CHUNK, hb * RN), cmap)
    return pl.pallas_call(
        body, name=name, grid=(RH // hb, NCH),
        in_specs=[io] * 6 + [pl.BlockSpec((1, hb, RN, RN), lambda g, c: (cmap(g, c)[0], g, 0, 0)), io],
        out_specs=[io] * 6,
        out_shape=[jax.ShapeDtypeStruct((T, RW), F32)] * 6,
        scratch_shapes=[pltpu.VMEM((hb, RN, RN), F32)],
        compiler_params=pltpu.CompilerParams(dimension_semantics=("parallel", "arbitrary"), vmem_limit_bytes=VMEM_BIG),
    )(r, lw, k, v, a, b, h0, dy)


def loss_stage(out, x2, tgt, g_post):
    tr = 256

    def body(o_ref, x_ref, t_ref, g_ref, do_ref, dy_ref, dg_ref, loss_ref):
        @pl.when(pl.program_id(0) == 0)
        def _():
            dg_ref[...] = jnp.zeros_like(dg_ref)
            loss_ref[...] = jnp.zeros_like(loss_ref)

        nrm, vjp = jax.vjp(_rms, o_ref[...], g_ref[...])
        e = x_ref[...] + nrm - t_ref[...]
        s = jnp.sum(jnp.sum(e * e, axis=1, keepdims=True), axis=0, keepdims=True)
        loss_ref[...] += jnp.broadcast_to(s * (0.5 / D), loss_ref.shape)
        dy = e * (1.0 / D)
        do, dg = vjp(dy)
        do_ref[...] = do.astype(do_ref.dtype)
        dy_ref[...] = dy
        dg_ref[...] += dg

    row = pl.BlockSpec((tr, D), lambda i: (i, 0))
    return pl.pallas_call(
        body, name="loss_stage", grid=(T // tr,),
        in_specs=[row, row, row, pl.BlockSpec((1, D), lambda i: (0, 0))],
        out_specs=[row, row, pl.BlockSpec((1, D), lambda i: (0, 0)), pl.BlockSpec((8, LANE), lambda i: (0, 0))],
        out_shape=[jax.ShapeDtypeStruct((T, D), BF16), jax.ShapeDtypeStruct((T, D), F32),
                   jax.ShapeDtypeStruct((1, D), F32), jax.ShapeDtypeStruct((8, LANE), F32)],
        compiler_params=pltpu.CompilerParams(dimension_semantics=("arbitrary",), vmem_limit_bytes=VMEM_BIG),
    )(out, x2, tgt, g_post)


_EW_BLOCK_BYTES = 1 << 20


def _row_tile(rows, cols):
    best = None
    for tr in range(16, rows + 1, 16):
        if rows % tr == 0 and tr * cols * 4 <= _EW_BLOCK_BYTES:
            best = tr
    return best or rows


def _axis_tile(shape, axis, words):
    rows, cols = shape
    n, other, unit = (rows, cols, 16) if axis == 0 else (cols, rows, LANE)
    best = unit if n % unit == 0 else n
    for t in range(unit, n + 1, unit):
        if n % t == 0 and t * other * words * 4 <= _EW_BLOCK_BYTES:
            best = t
    blk = (best, cols) if axis == 0 else (rows, best)
    at = (lambda s: (s, 0)) if axis == 0 else (lambda s: (0, s))
    return blk, n // best, at


def _adamw_update(g, w_ref, m_ref, v_ref, g_ref, d_ref, nm_ref, nv_ref):
    mm = ADAM_B1 * m_ref[...] + (1.0 - ADAM_B1) * g
    vv = ADAM_B2 * v_ref[...] + (1.0 - ADAM_B2) * (g * g)
    m_hat = mm / (1.0 - ADAM_B1 ** ADAM_STEP)
    v_hat = vv / (1.0 - ADAM_B2 ** ADAM_STEP)
    g_ref[...] = g
    d_ref[...] = -ADAM_LR * (m_hat / (jnp.sqrt(v_hat) + ADAM_EPS) + ADAM_WD * w_ref[...])
    nm_ref[...] = mm
    nv_ref[...] = vv


def adamw(name, w, m, v, parts):
    rows, cols = w.shape
    br = _row_tile(rows, cols)
    npart = len(parts)

    def body(w_ref, m_ref, v_ref, *rest):
        g = rest[0][...].astype(F32)
        for p in rest[1:npart]:
            g = g + p[...].astype(F32)
        _adamw_update(g, w_ref, m_ref, v_ref, *rest[npart:])

    blk = pl.BlockSpec((br, cols), lambda i: (i, 0))
    return pl.pallas_call(
        body, name=name, grid=(rows // br,),
        in_specs=[blk] * (3 + npart), out_specs=[blk] * 4,
        out_shape=[jax.ShapeDtypeStruct((rows, cols), F32)] * 4,
        compiler_params=pltpu.CompilerParams(dimension_semantics=("parallel",), vmem_limit_bytes=VMEM_BIG),
    )(w, m, v, *parts)


def adamw_halves(name, place, w, m, v, mine, theirs, axis):
    half_shape = mine.shape
    blk_shape, nb, at = _axis_tile(half_shape, axis, 1)

    def body(p_ref, w_ref, m_ref, v_ref, a_ref, b_ref, *outs):
        own = (pl.program_id(0) // nb) == p_ref[0]
        _adamw_update(jnp.where(own, a_ref[...], b_ref[...]), w_ref, m_ref, v_ref, *outs)

    blk = pl.BlockSpec(blk_shape, lambda i, p: at(i))
    half = pl.BlockSpec(blk_shape, lambda i, p: at(i % nb))
    return pl.pallas_call(
        body, name=name,
        grid_spec=pltpu.PrefetchScalarGridSpec(num_scalar_prefetch=1, grid=(2 * nb,),
                                               in_specs=[blk] * 3 + [half] * 2, out_specs=[blk] * 4),
        out_shape=[jax.ShapeDtypeStruct(w.shape, F32)] * 4,
        compiler_params=pltpu.CompilerParams(dimension_semantics=("arbitrary",), vmem_limit_bytes=VMEM_BIG),
    )(place, w, m, v, mine, theirs)


def pair_sum(name, place, send, other, axis):
    blk_shape, nb, at = _axis_tile(other.shape[1:], axis, 4)

    def body(p_ref, a_ref, b_ref, o_ref):
        o_ref[...] = (a_ref[...].astype(F32) + b_ref[...].astype(F32)).astype(o_ref.dtype)

    blk = pl.BlockSpec((4,) + blk_shape, lambda i, p: (0,) + at(i))
    mine = pl.BlockSpec((4,) + blk_shape, lambda i, p: (0,) + at(p[0] * nb + i))
    return pl.pallas_call(
        body, name=name,
        grid_spec=pltpu.PrefetchScalarGridSpec(num_scalar_prefetch=1, grid=(nb,), in_specs=[mine, blk], out_specs=blk),
        out_shape=jax.ShapeDtypeStruct(other.shape, BF16),
        compiler_params=pltpu.CompilerParams(dimension_semantics=("arbitrary",), vmem_limit_bytes=VMEM_BIG),
    )(place, send, other)


def sum4(name, place, recv, own, axis):
    blk_shape, nb, at = _axis_tile(recv.shape[1:], axis, 4)

    def body(p_ref, r_ref, s_ref, o_ref):
        me = p_ref[0]
        t = [jnp.where(me == j, s_ref[j], r_ref[j]).astype(F32) for j in range(4)]
        o_ref[...] = ((t[0] + t[1]) + t[2]) + t[3]

    blk = pl.BlockSpec((4,) + blk_shape, lambda i, p: (0,) + at(i))
    return pl.pallas_call(
        body, name=name,
        grid_spec=pltpu.PrefetchScalarGridSpec(num_scalar_prefetch=1, grid=(nb,), in_specs=[blk, blk],
                                               out_specs=pl.BlockSpec(blk_shape, lambda i, p: at(i))),
        out_shape=jax.ShapeDtypeStruct(recv.shape[1:], F32),
        compiler_params=pltpu.CompilerParams(dimension_semantics=("arbitrary",), vmem_limit_bytes=VMEM_BIG),
    )(place, recv, own)


_ANY = pl.BlockSpec(memory_space=pl.ANY)


def _place():
    x, y, c = lax.axis_index("x"), lax.axis_index("y"), lax.axis_index("c")
    return x, y, c, 2 * x + y


def _chip_peers(x, y):
    out = []
    for k in (1, 2, 3):
        px = 1 - x if k & 2 else x
        py = 1 - y if k & 1 else y
        out.append((k, px, py, 2 * px + py))
    return out


def _half(c, shape, axis):
    n = shape[axis] // 2
    sl = pl.ds(pl.multiple_of(c * n, 16 if axis == 0 else LANE), n)
    return (sl,) if axis == 0 else (pl.ds(0, shape[0]), sl)


def gather_weights(srcs, axes):
    side = gather_side(srcs, axes)
    n = len(srcs)

    def body(*refs):
        ins, outs, sems = refs[:n], refs[n:2 * n], refs[2 * n:]
        side.start(ins, outs, sems)
        side.finish(ins, outs, sems)

    return pl.pallas_call(
        body, name="gather_weights", in_specs=[_ANY] * n, out_specs=[_ANY] * n,
        out_shape=side.outs, scratch_shapes=side.sems,
    )(*srcs)


def gather_side(srcs, axes):
    n = len(srcs)

    def copies(src, dst, sems, sends_only=False):
        ssem, rsem, fssem, frsem = sems
        x, y, c, me = _place()
        sib = (x, y, 1 - c)
        out = []
        for i in range(n):
            mine, other = _half(c, srcs[i].shape, axes[i]), _half(1 - c, srcs[i].shape, axes[i])
            for k, px, py, peer in _chip_peers(x, y):
                sems_k = dict(send_sem=ssem.at[i, k - 1], recv_sem=rsem.at[i, k - 1], device_id=(px, py, c),
                              device_id_type=MESH_IDS)
                fsems = dict(send_sem=fssem.at[i, k - 1], recv_sem=frsem.at[i, k - 1], device_id=sib,
                             device_id_type=MESH_IDS)
                got = dst[i].at[(peer,) + mine]
                snd = pltpu.make_async_remote_copy(src_ref=src[i].at[mine], dst_ref=dst[i].at[(me,) + mine], **sems_k)
                if sends_only:
                    out.append(snd)
                    continue
                out.append((
                    snd,
                    pltpu.make_async_remote_copy(src_ref=src[i].at[mine], dst_ref=got, **sems_k),
                    pltpu.make_async_remote_copy(src_ref=got, dst_ref=got, **fsems),
                    pltpu.make_async_remote_copy(src_ref=got, dst_ref=dst[i].at[(peer,) + other], **fsems)))
        return out

    def start(src, dst, sems):
        for snd in copies(src, dst, sems, sends_only=True):
            snd.start()

    def finish(src, dst, sems):
        cps = copies(src, dst, sems)
        for _, rcv, fwd, _ in cps:
            rcv.wait_recv()
            fwd.start()
        for snd, _, fwd, frcv in cps:
            frcv.wait_recv()
            snd.wait_send()
            fwd.wait_send()

    return Side(list(srcs), [jax.ShapeDtypeStruct((4,) + s.shape, s.dtype) for s in srcs],
                [pltpu.SemaphoreType.DMA((n, 3))] * 4, start, finish)


def pair_exchange(name, srcs, axes):
    n = len(srcs)

    def half_shape(s, axis):
        return (4, s.shape[1] // 2, s.shape[2]) if axis == 0 else (4, s.shape[1], s.shape[2] // 2)

    def body(*refs):
        src, other = refs[:n], refs[n:2 * n]
        ssem, rsem = refs[2 * n:]
        x, y, c, _ = _place()
        cps = []
        for i in range(n):
            idx = (pl.ds(0, 4),) + _half(1 - c, srcs[i].shape[1:], axes[i])
            cps.append(pltpu.make_async_remote_copy(
                src_ref=src[i].at[idx], dst_ref=other[i], send_sem=ssem.at[i], recv_sem=rsem.at[i],
                device_id=(x, y, 1 - c), device_id_type=MESH_IDS))
            cps[-1].start()
        for cp in cps:
            cp.wait()

    return pl.pallas_call(
        body, name=name, in_specs=[_ANY] * n, out_specs=[_ANY] * n,
        out_shape=[jax.ShapeDtypeStruct(half_shape(s, a), s.dtype) for s, a in zip(srcs, axes)],
        scratch_shapes=[pltpu.SemaphoreType.DMA((n,))] * 2,
    )(*srcs)


def scatter_grads(srcs):
    side = scatter_side(srcs)
    n = len(srcs)

    def body(*refs):
        ins, outs, sems = refs[:n], refs[n:2 * n], refs[2 * n:]
        side.start(ins, outs, sems)
        side.finish(ins, outs, sems)

    return pl.pallas_call(
        body, name="scatter_grads", in_specs=[_ANY] * n, out_specs=[_ANY] * n,
        out_shape=side.outs, scratch_shapes=side.sems,
    )(*srcs)


def scatter_side(srcs):
    n = len(srcs)

    def copies(src, dst, sems, sends_only=False):
        ssem, rsem = sems
        x, y, c, me = _place()
        out = []
        for i in range(n):
            for k, px, py, peer in _chip_peers(x, y):
                sems_k = dict(send_sem=ssem.at[i, k - 1], recv_sem=rsem.at[i, k - 1], device_id=(px, py, c),
                              device_id_type=MESH_IDS)
                snd = pltpu.make_async_remote_copy(src_ref=src[i].at[peer], dst_ref=dst[i].at[me], **sems_k)
                if sends_only:
                    out.append(snd)
                    continue
                out.append((snd, pltpu.make_async_remote_copy(src_ref=src[i].at[peer], dst_ref=dst[i].at[peer],
                                                              **sems_k)))
        return out

    def start(src, dst, sems):
        for snd in copies(src, dst, sems, sends_only=True):
            snd.start()

    def finish(src, dst, sems):
        for snd, rcv in copies(src, dst, sems):
            rcv.wait_recv()
            snd.wait_send()

    return Side(list(srcs), [jax.ShapeDtypeStruct(s.shape, s.dtype) for s in srcs],
                [pltpu.SemaphoreType.DMA((n, 3))] * 2, start, finish)


_HBM = pl.BlockSpec(memory_space=pltpu.HBM)
_SEM = pl.BlockSpec(memory_space=pltpu.SEMAPHORE)
_DATAFLOW = pltpu.SideEffectType.DATAFLOW_SIDE_EFFECTING


def scatter_start(name, srcs):
    n = len(srcs)
    side = scatter_side(srcs)
    ns = 3 * n

    def body(*refs):
        src, land = refs[:n], refs[n:2 * n]
        sems = refs[2 * n:2 * n + 2 * ns]
        side.start(src, land, (_SemGrid(sems[:ns]), _SemGrid(sems[ns:])))
        refs[-1][...] = jnp.zeros_like(refs[-1])

    hbm = [pltpu.HBM(s.shape, s.dtype) for s in srcs]
    res = pl.pallas_call(
        body, name=name,
        out_shape=[pltpu.SemaphoreType.DMA(())] * (2 * ns) + hbm + hbm + [jax.ShapeDtypeStruct((8, LANE), F32)],
        in_specs=[_HBM] * (2 * n),
        out_specs=[_SEM] * (2 * ns) + [_HBM] * (2 * n) + [pl.BlockSpec(memory_space=pltpu.VMEM)],
        input_output_aliases={i: 2 * ns + i for i in range(2 * n)},
        compiler_params=pltpu.CompilerParams(has_side_effects=_DATAFLOW),
    )(*[pltpu.with_memory_space_constraint(s, pltpu.HBM) for s in srcs],
      *[pltpu.with_memory_space_constraint(lax.empty(s.shape, s.dtype), pltpu.HBM) for s in srcs])
    return res[:2 * ns], res[2 * ns:2 * ns + n], res[2 * ns + n:2 * ns + 2 * n], res[-1]


def scatter_wait(name, sems, srcs, lands, after):
    n = len(srcs)
    side = scatter_side(srcs)
    ns = 3 * n

    def body(*refs):
        src, land = refs[:n], refs[n:2 * n]
        s = refs[2 * n:2 * n + 2 * ns]
        side.finish(src, land, (_SemGrid(s[:ns]), _SemGrid(s[ns:])))

    hbm = [pltpu.HBM(s.shape, s.dtype) for s in srcs]
    res = pl.pallas_call(
        body, name=name, out_shape=hbm + hbm,
        in_specs=[_HBM] * (2 * n) + [_SEM] * (2 * ns) + [_ANY], out_specs=[_HBM] * (2 * n),
        input_output_aliases={i: i for i in range(2 * n)},
        compiler_params=pltpu.CompilerParams(has_side_effects=_DATAFLOW),
    )(*srcs, *lands, *sems, after)
    return res[:n], res[n:]


class _SemGrid:
    def __init__(self, sems):
        self.sems = sems

    @property
    def at(self):
        return self

    def __getitem__(self, ik):
        return self.sems[3 * ik[0] + ik[1]]


def swap_halves(name, srcs):
    n = len(srcs)

    def body(*refs):
        src, dst = refs[:n], refs[n:2 * n]
        ssem, rsem = refs[2 * n:]
        x, y, c, _ = _place()
        cps = []
        for i in range(n):
            cps.append(pltpu.make_async_remote_copy(src_ref=src[i], dst_ref=dst[i], send_sem=ssem.at[i],
                                                    recv_sem=rsem.at[i], device_id=(x, y, 1 - c),
                                                    device_id_type=MESH_IDS))
            cps[-1].start()
        for cp in cps:
            cp.wait()

    return pl.pallas_call(
        body, name=name, in_specs=[_ANY] * n, out_specs=[_ANY] * n,
        out_shape=[jax.ShapeDtypeStruct(s.shape, s.dtype) for s in srcs],
        scratch_shapes=[pltpu.SemaphoreType.DMA((n,))] * 2,
    )(*srcs)


def allgather8(name, src):
    rows = src.shape[0]

    def body(src_ref, dst_ref, send_sems, recv_sems):
        x, y, c = lax.axis_index("x"), lax.axis_index("y"), lax.axis_index("c")
        me = 4 * x + 2 * y + c
        dst_ref[me] = src_ref[...]
        sends, recvs = [], []
        for k in range(1, 8):
            px = 1 - x if k & 4 else x
            py = 1 - y if k & 2 else y
            pc = 1 - c if k & 1 else c
            peer = 4 * px + 2 * py + pc
            for lst, slot in ((sends, me), (recvs, peer)):
                lst.append(pltpu.make_async_remote_copy(
                    src_ref=src_ref, dst_ref=dst_ref.at[slot], send_sem=send_sems.at[k - 1],
                    recv_sem=recv_sems.at[k - 1], device_id=(px, py, pc), device_id_type=MESH_IDS))
        for cp in sends:
            cp.start()
        for cp in recvs:
            cp.wait_recv()
        for cp in sends:
            cp.wait_send()

    vm = pl.BlockSpec(memory_space=pltpu.VMEM)
    return pl.pallas_call(
        body, name=name, in_specs=[vm], out_specs=vm,
        out_shape=jax.ShapeDtypeStruct((8, rows, LANE), src.dtype),
        scratch_shapes=[pltpu.SemaphoreType.DMA((7,)), pltpu.SemaphoreType.DMA((7,))],
    )(src)


WEIGHTS = ['g_pre', 'w_in', 'mla_q_norm', 'mla_wq_b', 'mla_kv_norm', 'mla_wkv_b', 'rwkv_mu', 'rwkv_w0_f', 'rwkv_w2_f',
           'rwkv_w0_b', 'rwkv_w2_b', 'rwkv_a0_f', 'rwkv_a2_f', 'rwkv_a0_b', 'rwkv_a2_b', 'rwkv_k_k', 'rwkv_k_a',
           'rwkv_r_k', 'rwkv_gn_g', 'rwkv_gn_b', 'w_br_mla', 'w_br_rwkv', 'w_out', 'g_post']
BIG_SHAPES = {'w_in': (D_IN // 4, D), 'mla_wq_b': (Q_RANK, 384), 'mla_wkv_b': (KV_RANK, 512),
              'rwkv_w2_f': (LORA, 256), 'rwkv_w2_b': (LORA, 256), 'rwkv_a2_f': (LORA, 256), 'rwkv_a2_b': (LORA, 256),
              'w_br_mla': (RW, 512), 'w_br_rwkv': (RW, 512), 'w_out': (512, D)}
BIG = list(BIG_SHAPES)
SMALL = [n for n in WEIGHTS if n not in BIG_SHAPES]
SMALL_SHAPES = {'g_pre': (D,), 'mla_q_norm': (Q_RANK,), 'mla_kv_norm': (KV_RANK,), 'rwkv_mu': (3456,),
                'rwkv_w0_f': (RW,), 'rwkv_w0_b': (RW,), 'rwkv_a0_f': (RW,), 'rwkv_a0_b': (RW,), 'rwkv_k_k': (RW,),
                'rwkv_k_a': (RW,), 'rwkv_r_k': (RH, RN), 'rwkv_gn_g': (RW,), 'rwkv_gn_b': (RW,), 'g_post': (D,)}
SMALL_LEN = sum(int(np.prod(s)) for s in SMALL_SHAPES.values())
SMALL_ROWS = 144


UNITS = [('w_in',), ('mla_wq_b',), ('mla_wkv_b',), ('rwkv_w2_f', 'rwkv_w2_b', 'rwkv_a2_f', 'rwkv_a2_b'),
         ('w_br_mla', 'w_br_rwkv'), ('w_out',)]
UNIT_AXIS = [1, 0, 0, 0, 0, 0]
ROW_SHARDED = ('w_in', 'w_out')


def _unit_cat(parts):
    return parts[0] if len(parts) == 1 else jnp.concatenate(parts, axis=0)


def _unit_split(arr, names, axis):
    out, o = {}, 0
    for n in names:
        rows = BIG_SHAPES[n][0]
        out[n] = lax.slice_in_dim(arr, o, o + rows, axis=axis)
        o += rows
    return out


def _gathered(units, ag, own, me):
    out = {}
    for names, arr, mine in zip(units, ag, own):
        slots = [jnp.where(me == j, mine, arr[j]) for j in range(4)]
        for n in names:
            parts = [_unit_split(s, names, 0)[n] for s in slots]
            out[n] = jnp.concatenate(parts, axis=0 if n in ROW_SHARDED else 1)
    return out


def _shards(n, g):
    r, w = BIG_SHAPES[n]
    if n in ROW_SHARDED:
        return [g[j * r:(j + 1) * r] for j in range(4)]
    return [g[:, j * w:(j + 1) * w] for j in range(4)]


def _pack_small(d, extra=None):
    flat = jnp.concatenate([d[n].reshape(-1) for n in SMALL] + ([extra.reshape(-1)] if extra is not None else []))
    return jnp.pad(flat, (0, SMALL_ROWS * LANE - flat.shape[0])).reshape(SMALL_ROWS, LANE)


def _unpack_small(packed):
    flat, out, o = packed.reshape(-1), {}, 0
    for n in SMALL:
        sz = int(np.prod(SMALL_SHAPES[n]))
        out[n] = flat[o:o + sz].reshape(SMALL_SHAPES[n])
        o += sz
    return out


def _perm_w_in(w):
    z = lambda n: jnp.zeros((n, w.shape[1]), w.dtype)
    lora = []
    for i in range(4):
        lora += [w[4160 + LORA * i:4160 + LORA * (i + 1)], z(LANE - LORA)]
    return jnp.concatenate([w[0:1024], w[1088:4160], w[4544:D_IN]] + lora + [w[1024:1088], z(256 - ROPE)], axis=0)


def _unperm_w_in(g):
    lora = [g[OFF_LORA + LANE * i:OFF_LORA + LANE * i + LORA] for i in range(4)]
    return jnp.concatenate([g[0:1024], g[OFF_KR:OFF_KR + ROPE], g[1024:4096]] + lora + [g[4096:OFF_LORA]], axis=0)


def _perm_wq(w):
    w3 = w.reshape(Q_RANK, HEADS, NOPE + ROPE)
    rope = jnp.pad(w3[:, :, NOPE:], ((0, 0), (0, 0), (0, LANE - ROPE)))
    return jnp.concatenate([w3[:, :, :NOPE].reshape(Q_RANK, -1), rope.reshape(Q_RANK, -1)], axis=1)


def _unperm_wq(g):
    return jnp.concatenate([g[:, :1024].reshape(Q_RANK, HEADS, NOPE),
                            g[:, 1024:].reshape(Q_RANK, HEADS, LANE)[:, :, :ROPE]], axis=2).reshape(Q_RANK, -1)


def _perm_wkv(w):
    w3 = w.reshape(KV_RANK, HEADS, NOPE + VDIM)
    return jnp.concatenate([w3[:, :, :NOPE].reshape(KV_RANK, -1), w3[:, :, NOPE:].reshape(KV_RANK, -1)], axis=1)


def _unperm_wkv(g):
    return jnp.concatenate([g[:, :1024].reshape(KV_RANK, HEADS, NOPE), g[:, 1024:].reshape(KV_RANK, HEADS, VDIM)],
                           axis=2).reshape(KV_RANK, -1)


def _pad_rows(w):
    return jnp.pad(w, ((0, LANE - LORA), (0, 0)))


def _perm_mu(mu):
    parts = [mu[:3072]]
    for i in range(4):
        parts += [mu[3072 + LORA * i:3072 + LORA * (i + 1)], jnp.zeros((LANE - LORA,), mu.dtype)]
    return jnp.concatenate(parts).reshape(1, NLERP)


def _unperm_mu(g):
    g = g.reshape(-1)
    return jnp.concatenate([g[:3072]] + [g[3072 + LANE * i:3072 + LANE * i + LORA] for i in range(4)])


def _constants():
    g2 = np.kron(np.eye(2, dtype=np.float32), np.ones((RN, RN), np.float32))
    pos = jnp.arange(T, dtype=F32)
    inv_freq = jnp.power(ROPE_THETA, -jnp.arange(0, ROPE, 2, dtype=F32) / ROPE)
    ang = pos[:, None] * inv_freq[None, :]
    cos, sin, zero = jnp.cos(ang), jnp.sin(ang), jnp.zeros((T, LANE - ROPE), F32)
    cq = jnp.tile(jnp.concatenate([cos, cos, zero], axis=1), (1, HEADS))
    sq = jnp.tile(jnp.concatenate([-sin, sin, zero], axis=1), (1, HEADS))
    return jnp.asarray(g2, BF16), cq, sq


def _step(x, tgt, w, m, v):
    x2, tgt2 = x.reshape(T, D), tgt.reshape(T, D)
    g2, cq, sq = _constants()
    row = lambda n: w[n].reshape(1, -1)
    w, m, v = ({**t, 'w_in': t['w_in'].T} for t in (w, m, v))

    core, chip = lax.axis_index("c"), 2 * lax.axis_index("x") + lax.axis_index("y")
    core1, chip1 = core.astype(jnp.int32).reshape(1), chip.astype(jnp.int32).reshape(1)
    own_bf = [_unit_cat([w[n].astype(BF16) for n in u]) for u in UNITS]
    full = _gathered(UNITS[:1], gather_weights(own_bf[:1], UNIT_AXIS[:1]), own_bf[:1], chip)
    wp = _perm_w_in(full['w_in'])
    mu_p = _perm_mu(w['rwkv_mu'])

    st_pre = Stage("pre", f_pre, [(D, BF16), (D, None)], 256, [0], [0], [F32])
    st_mla = Stage("mla", f_mla, [(1024, BF16), (1024, BF16), (1024, BF16), (LANE, BF16), (1024, BF16)], 128,
                   [0, 1, 2], [0, 1, 2, 3], [BF16] * 3)
    st_rpre = Stage("rwkv_pre", f_rwkv_pre, [(RW, F32)] * 9, 128, [0], list(range(10)), [F32])
    st_rpost = Stage("rwkv_post", f_rwkv_post, [(RW, BF16)], 256, [0, 2, 3, 4, 5, 6], [0, 1, 2],
                     [F32, F32, F32, F32, F32, BF16])
    st_gate = Stage("gate", f_gate, [(RW, BF16)], 256, [0, 1], [], [F32, BF16])
    st_merge = Stage("merge", f_merge, [(D, BF16)], 256, [0, 1, 2, 3], [], [BF16] * 4)

    pre_rows, pre_par = [(x2, D, 0)], [row('g_pre')]
    (h,) = st_pre.fwd(pre_rows, pre_par)
    proj, rest = matmul("mm_in", h, wp, "nt", side=gather_side(own_bf[1:], UNIT_AXIS[1:]))
    full.update(_gathered(UNITS[1:], rest, own_bf[1:], chip))
    wq = _perm_wq(full['mla_wq_b']).astype(F32)
    wkv = _perm_wkv(full['mla_wkv_b']).astype(F32)
    lora_w = [_pad_rows(full[n]).astype(F32) for n in ('rwkv_w2_f', 'rwkv_w2_b', 'rwkv_a2_f', 'rwkv_a2_b')]

    mla_rows = [(proj, 512, OFF_QA // 512), (proj, 512, OFF_KVA // 512), (proj, 256, OFF_KR // 256),
                (cq, 1024, 0), (sq, 1024, 0), (cq, LANE, 0), (sq, LANE, 0)]
    mla_par = [row('mla_q_norm'), row('mla_kv_norm'), wq, wkv]
    att = st_mla.fwd(mla_rows, mla_par)
    y_mla = attn_fwd(*att)

    lerp = shift_fwd(proj, mu_p)
    rpre_rows = [(lerp, NLERP, 0)]
    rpre_par = [row('rwkv_w0_f'), row('rwkv_w0_b'), row('rwkv_a0_f'), row('rwkv_a0_b'), row('rwkv_k_k'),
                row('rwkv_k_a')] + lora_w + [g2]
    r_, v_, lwf, lwb, kf, kb, an, bf_, bb_ = st_rpre.fwd(rpre_rows, rpre_par)
    fin = [r_, lwf, kf, v_, an, bf_]
    bin_ = [r_, lwb, kb, v_, an, bb_]
    yf, h0f = scan_fwd("scan_f", *fin, reverse=False)
    yb, h0b = scan_fwd("scan_b", *bin_, reverse=True)
    rpost_rows = [(yf, RW, 0), (yb, RW, 0), (r_, RW, 0), (kf, RW, 0), (kb, RW, 0),
                  (v_, RW, 0), (proj, RW, OFF_ZR // RW)]
    rpost_par = [row('rwkv_gn_g'), row('rwkv_gn_b'), row('rwkv_r_k'), g2]
    (gr,) = st_rpost.fwd(rpost_rows, rpost_par)
    gate_rows = [(y_mla, RW, 0), (proj, RW, OFF_ZM // RW)]
    (gm,) = st_gate.fwd(gate_rows, [])
    um = matmul("mm_br_mla", gm, full['w_br_mla'], "nn")
    ur = matmul("mm_br_rwkv", gr, full['w_br_rwkv'], "nn")
    merge_rows = [(um, D, 0), (ur, D, 0), (proj, D, OFF_GM // D), (proj, D, OFF_GR // D)]
    (merged,) = st_merge.fwd(merge_rows, [])
    out = matmul("mm_out", merged, full['w_out'], "nn")
    d_out, dy, dg_post, loss_blk = loss_stage(out, x2, tgt2, row('g_post'))

    gw = {'g_post': dg_post}
    d_merged = matmul("mm_out_dx", d_out, full['w_out'], "nt")
    gw['w_out'] = matmul("mm_out_dw", merged, d_out, "tn")
    (d_um, d_ur, d_gm, d_gr), _ = st_merge.bwd(merge_rows, [], [[(d_merged, D, 0)]])
    d_gmla = matmul("mm_br_mla_dx", d_um, full['w_br_mla'], "nt")
    gw['w_br_mla'] = matmul("mm_br_mla_dw", gm, d_um, "tn")
    d_grw = matmul("mm_br_rwkv_dx", d_ur, full['w_br_rwkv'], "nt")
    gw['w_br_rwkv'] = matmul("mm_br_rwkv_dw", gr, d_ur, "tn")
    (d_ymla, d_zm), _ = st_gate.bwd(gate_rows, [], [[(d_gmla, RW, 0)]])
    (d_y, d_r3, d_kf2, d_kb2, d_v3, d_zr), (gw['rwkv_gn_g'], gw['rwkv_gn_b'], d_rk) = st_rpost.bwd(
        rpost_rows, rpost_par, [[(d_grw, RW, 0)]])
    gw['rwkv_r_k'] = d_rk
    sf = scan_bwd("scan_f_bwd", *fin, h0f, d_y, reverse=False)
    sb = scan_bwd("scan_b_bwd", *bin_, h0b, d_y, reverse=True)
    c = lambda *ts: [(t, RW, 0) for t in ts]
    rpre_cts = [c(sf[0], sb[0], d_r3), c(sf[3], sb[3], d_v3), c(sf[1]), c(sb[1]), c(sf[2], d_kf2), c(sb[2], d_kb2),
                c(sf[4], sb[4]), c(sf[5]), c(sb[5])]
    (d_rin,), rpre_g = st_rpre.bwd(rpre_rows, rpre_par, rpre_cts)
    for n, gval in zip(('rwkv_w0_f', 'rwkv_w0_b', 'rwkv_a0_f', 'rwkv_a0_b', 'rwkv_k_k', 'rwkv_k_a'), rpre_g[:6]):
        gw[n] = gval
    for n, gval in zip(('rwkv_w2_f', 'rwkv_w2_b', 'rwkv_a2_f', 'rwkv_a2_b'), rpre_g[6:]):
        gw[n] = gval[:LORA]
    d_lerp, d_mu = shift_bwd(proj, mu_p, d_rin)
    gw['rwkv_mu'] = _unperm_mu(d_mu)

    mla_cts = [[(t, t.shape[1], 0)] for t in attn_bwd(*att, d_ymla)]
    (d_qa, d_kva, d_kr), (gw['mla_q_norm'], gw['mla_kv_norm'], d_wq, d_wkv) = st_mla.bwd(mla_rows, mla_par, mla_cts)
    gw['mla_wq_b'], gw['mla_wkv_b'] = _unperm_wq(d_wq), _unperm_wkv(d_wkv)

    dproj = jnp.concatenate([d_qa, d_kva, d_lerp[:, :3072], d_zm, d_zr, d_gm, d_gr, d_lerp[:, 3072:], d_kr], axis=1)

    def pair_sums(name, ids):
        send = [jnp.stack([_unit_cat([_shards(n, gw[n])[j].astype(BF16) for n in UNITS[i]]) for j in range(4)])
                for i in ids]
        axes = [UNIT_AXIS[i] for i in ids]
        other = pair_exchange(name, send, axes)
        return [pair_sum(f"pair_sum_{i}", core1, s, o, ax) for i, s, o, ax in zip(ids, send, other, axes)]

    late, early = [0], list(range(1, len(UNITS)))
    pairs_e = pair_sums("pair_exchange_rest", early)
    gw_in, recv_e = matmul("mm_in_dw", dproj, h, "tn", BF16, side=scatter_side(pairs_e))
    gw['w_in'] = _unperm_w_in(gw_in)
    pairs_l = pair_sums("pair_exchange_w_in", late)
    sems, src_fly, land_fly, token = scatter_start("scatter_w_in_start", pairs_l)
    dh = matmul("mm_in_dx", dproj, wp, "nn", after=(token,))
    (grad_x,), (gw['g_pre'],) = st_pre.bwd(pre_rows, pre_par, [[(dh, D, 0)], [(dy, D, 0)]])

    big = [dict() for _ in range(4)]

    def update(name, ids, recv, pairs):
        mine = [sum4(f"sum4_{i}", chip1, r, p, UNIT_AXIS[i]) for i, r, p in zip(ids, recv, pairs)]
        theirs = swap_halves(name, mine)
        for i, mi, th in zip(ids, mine, theirs):
            res = adamw_halves(f"adamw_{i}", core1, *[_unit_cat([t[n] for n in UNITS[i]]) for t in (w, m, v)], mi, th,
                               UNIT_AXIS[i])
            for q in range(4):
                big[q].update(_unit_split(res[q], UNITS[i], 0))
        return res

    last = update("swap_halves_rest", early, recv_e, pairs_e)
    parts = allgather8("gather_small", _pack_small(gw, loss_blk[0, :1]))
    small = adamw("adamw_small", _pack_small(w), _pack_small(m), _pack_small(v), [parts[i] for i in range(8)])
    pairs_l, recv_l = scatter_wait("scatter_w_in_wait", sems, src_fly, land_fly, small[0] + last[0][:1, :1])
    update("swap_halves_w_in", late, recv_l, pairs_l)

    outs = []
    for b_d, s_arr in zip(big, small):
        d = {**b_d, **_unpack_small(s_arr)}
        d['w_in'] = d['w_in'].T
        outs.append([d[n] for n in WEIGHTS])
    loss = small[0][SMALL_LEN // LANE, 0]
    return (loss, grad_x.reshape(1, T, D), *outs[0], *outs[1], *outs[2], *outs[3])


def kernel(x, g_pre, w_in, mla_q_norm, mla_wq_b, mla_kv_norm, mla_wkv_b, rwkv_mu, rwkv_w0_f, rwkv_w2_f, rwkv_w0_b, rwkv_w2_b, rwkv_a0_f, rwkv_a2_f, rwkv_a0_b, rwkv_a2_b, rwkv_k_k, rwkv_k_a, rwkv_r_k, rwkv_gn_g, rwkv_gn_b, w_br_mla, w_br_rwkv, w_out, g_post, loss_target, m_g_pre, m_w_in, m_mla_q_norm, m_mla_wq_b, m_mla_kv_norm, m_mla_wkv_b, m_rwkv_mu, m_rwkv_w0_f, m_rwkv_w2_f, m_rwkv_w0_b, m_rwkv_w2_b, m_rwkv_a0_f, m_rwkv_a2_f, m_rwkv_a0_b, m_rwkv_a2_b, m_rwkv_k_k, m_rwkv_k_a, m_rwkv_r_k, m_rwkv_gn_g, m_rwkv_gn_b, m_w_br_mla, m_w_br_rwkv, m_w_out, m_g_post, v_g_pre, v_w_in, v_mla_q_norm, v_mla_wq_b, v_mla_kv_norm, v_mla_wkv_b, v_rwkv_mu, v_rwkv_w0_f, v_rwkv_w2_f, v_rwkv_w0_b, v_rwkv_w2_b, v_rwkv_a0_f, v_rwkv_a2_f, v_rwkv_a0_b, v_rwkv_a2_b, v_rwkv_k_k, v_rwkv_k_a, v_rwkv_r_k, v_rwkv_gn_g, v_rwkv_gn_b, v_w_br_mla, v_w_br_rwkv, v_w_out, v_g_post):
    given = dict(locals())
    w = {n: given[n] for n in WEIGHTS}
    m = {n: given['m_' + n] for n in WEIGHTS}
    v = {n: given['v_' + n] for n in WEIGHTS}
    return _step(x, loss_target, w, m, v)
```

```python
import functools
import math

import numpy as np
import jax
import jax.numpy as jnp
from jax import lax
from jax.experimental import pallas as pl
from jax.experimental.pallas import tpu as pltpu

F32, BF16 = jnp.float32, jnp.bfloat16
MESH_IDS = pl.DeviceIdType.MESH

D = 2048
T = 2048
HEADS = 8
Q_RANK = 512
KV_RANK = 512
NOPE = 128
ROPE = 64
VDIM = 128
RW = 1024
RH = 16
RN = 64
LORA = 96
D_IN = 10688
NORM_EPS = 1e-6
GN_EPS = 64e-5
ROPE_THETA = 10000.0
ADAM_LR, ADAM_B1, ADAM_B2, ADAM_EPS, ADAM_WD, ADAM_STEP = 0.001, 0.9, 0.999, 1e-08, 0.01, 10

LANE = 128
VMEM_BIG = 56 * 2**20

NP = 11008
OFF_QA, OFF_KVA, OFF_RKV, OFF_ZM, OFF_ZR, OFF_GM, OFF_GR, OFF_LORA, OFF_KR = 0, 512, 1024, 4096, 5120, 6144, 8192, 10240, 10752
NLERP = 3584

CHUNK = 64
NCH = T // CHUNK


def _dg(a, b, ca, cb, batch=False, prec=None):
    bd = ((0,), (0,)) if batch else ((), ())
    return lax.dot_general(a, b, (((ca,), (cb,)), bd), precision=prec, preferred_element_type=F32)


@jax.custom_vjp
def bdot(a, b):
    return _dg(a.astype(BF16), b.astype(BF16), 1, 0)


def _bdot_fwd(a, b):
    return bdot(a, b), (a, b)


def _bdot_bwd(res, g):
    a, b = res
    gb = g.astype(BF16)
    da = _dg(gb, b.astype(BF16), 1, 1)
    db = _dg(a.astype(BF16), gb, 0, 0)
    return da.astype(a.dtype), db.astype(b.dtype)


bdot.defvjp(_bdot_fwd, _bdot_bwd)


def _split(x):
    hi = x.astype(BF16)
    lo = (x - hi.astype(F32)).astype(BF16)
    return hi, lo


@jax.custom_vjp
def gsum(x, g2):
    hi, lo = _split(x)
    return _dg(hi, g2, 1, 0) + _dg(lo, g2, 1, 0)


def _gsum_fwd(x, g2):
    return gsum(x, g2), g2


def _gsum_bwd(g2, g):
    hi, lo = _split(g)
    return _dg(hi, g2, 1, 1) + _dg(lo, g2, 1, 1), jnp.zeros_like(g2)


gsum.defvjp(_gsum_fwd, _gsum_bwd)


def headsum(x, g2):
    return jnp.concatenate([gsum(x[:, i * LANE:(i + 1) * LANE], g2) for i in range(x.shape[1] // LANE)], axis=1)


def _terms(x, n):
    out = []
    for i in range(n):
        t = x.astype(BF16)
        out.append(t)
        if i < n - 1:
            x = x - t.astype(F32)
    return out


def _bmm(a, b, ca, cb, na, nb):
    acc = None
    for i, ai in enumerate(_terms(a, na)):
        for j, bj in enumerate(_terms(b, nb)):
            if i + j < max(na, nb):
                p = _dg(ai, bj, ca, cb, True)
                acc = p if acc is None else acc + p
    return acc


_NN, _NT, _TN = (2, 1), (2, 2), (1, 1)


def _make_dots(nf, nb_nn, nb_nt, nb_tn):
    @jax.custom_vjp
    def nn(a, b):
        return _bmm(a, b, *_NN, nf, nf)

    @jax.custom_vjp
    def nt(a, b):
        return _bmm(a, b, *_NT, nf, nf)

    @jax.custom_vjp
    def tn(a, b):
        return _bmm(a, b, *_TN, nf, nf)

    nn.defvjp(lambda a, b: (nn(a, b), (a, b)),
              lambda r, g: (_bmm(g, r[1], *_NT, nb_nn, nb_nn), _bmm(r[0], g, *_TN, nb_nn, nb_nn)))
    nt.defvjp(lambda a, b: (nt(a, b), (a, b)),
              lambda r, g: (_bmm(g, r[1], *_NN, 1, nb_nt), _bmm(g, r[0], *_TN, 1, nb_nt)))
    tn.defvjp(lambda a, b: (tn(a, b), (a, b)),
              lambda r, g: (_bmm(r[1], g, *_NT, nb_tn, nb_tn), _bmm(r[0], g, *_NN, nb_tn, nb_tn)))
    return nn, nt, tn


_SCAN_NF, _SCAN_NB = 1, 1
nn, nt, tn = _make_dots(_SCAN_NF, 1, 2, 1)


@jax.custom_vjp
def cumdot(ones, x):
    return _bmm(ones, x, *_NN, 1, 3)


cumdot.defvjp(lambda o, x: (cumdot(o, x), o), lambda o, g: (jnp.zeros_like(o), _bmm(o, g, *_TN, 1, 3)))


def _solve_powers(l):
    pw = [l]
    for _ in range(int(math.log2(l.shape[-1])) - 1):
        pw.append(_bmm(pw[-1], pw[-1], *_NN, _SCAN_NF, _SCAN_NF))
    return pw


@jax.custom_vjp
def tri_solve(l, rhs):
    x = rhs
    for p in _solve_powers(l):
        x = x + _bmm(p, x, *_NN, _SCAN_NF, _SCAN_NF)
    return x


def _tri_solve_fwd(l, rhs):
    pw = _solve_powers(l)
    x = rhs
    for p in pw:
        x = x + _bmm(p, x, *_NN, _SCAN_NF, _SCAN_NF)
    return x, (pw, x)


def _tri_solve_bwd(res, g):
    pw, x = res
    y = g
    for p in pw:
        y = y + _bmm(p, y, *_TN, _SCAN_NB, _SCAN_NB)
    return _bmm(y, x, *_NT, _SCAN_NB, _SCAN_NB), y


tri_solve.defvjp(_tri_solve_fwd, _tri_solve_bwd)


def _rms(x, g):
    return x * lax.rsqrt(jnp.mean(x * x, axis=-1, keepdims=True) + NORM_EPS) * g


def _softplus(x):
    pos = x > 0
    return jnp.where(pos, x, 0.0) + jnp.log(1.0 + jnp.exp(-jnp.where(pos, x, -x)))


def _silu(z):
    return z * jax.nn.sigmoid(z)


def _tile(n, cands):
    for c in cands:
        if n % c == 0:
            return c
    raise ValueError(n)


_MM_VMEM_BYTES = 32 * 2**20


def _mm_tiles(m, n, k):
    best = None
    for tm in (2048, 1024, 512, 256):
        for tn_ in (2048, 1024, 512, 256):
            for d in range(k // LANE, 0, -1):
                tk = LANE * d
                if m % tm or n % tn_ or k % tk:
                    continue
                nk = k // tk
                vmem = 4 * tk * (tm + tn_) + 8 * tm * tn_ + (4 * tm * tn_ if nk > 1 else 0)
                if vmem > _MM_VMEM_BYTES:
                    continue
                a_reads = n // tn_ if nk > 1 else 1
                b_reads = 1 if (nk == 1 and n == tn_) else m // tm
                acc_rmw = nk * m * n if nk > 1 else 0
                cost = (a_reads * m * k + b_reads * k * n + acc_rmw, -tm * tn_ * tk)
                if best is None or cost < best[0]:
                    best = (cost, (tm, tn_, tk))
    return best[1]


class Side:
    def __init__(self, ins, outs, sems, start, finish):
        self.ins, self.outs, self.sems, self.start, self.finish = ins, outs, sems, start, finish


def matmul(name, a, b, mode, out_dtype=F32, side=None, after=()):
    if mode == "nn":
        (m, k), n = a.shape, b.shape[1]
    elif mode == "nt":
        (m, k), n = a.shape, b.shape[0]
    else:
        (k, m), n = a.shape, b.shape[1]
    tm, tn_, tk = _mm_tiles(m, n, k)
    nk = k // tk
    if mode == "nn":
        a_spec = pl.BlockSpec((tm, tk), lambda i, j, kk: (i, kk))
        b_spec = pl.BlockSpec((tk, tn_), lambda i, j, kk: (kk, j))
        ca, cb = 1, 0
    elif mode == "nt":
        a_spec = pl.BlockSpec((tm, tk), lambda i, j, kk: (i, kk))
        b_spec = pl.BlockSpec((tn_, tk), lambda i, j, kk: (j, kk))
        ca, cb = 1, 1
    else:
        a_spec = pl.BlockSpec((tk, tm), lambda i, j, kk: (kk, i))
        b_spec = pl.BlockSpec((tk, tn_), lambda i, j, kk: (kk, j))
        ca, cb = 0, 0

    grid = (m // tm, n // tn_, nk)
    n_in = len(side.ins) if side else 0
    n_out = len(side.outs) if side else 0
    n_dep = len(after)

    def body(a_ref, b_ref, *rest):
        rest = rest[n_dep:]
        s_ins, o_ref, s_outs = rest[:n_in], rest[n_in], rest[n_in + 1:n_in + 1 + n_out]
        scratch = rest[n_in + 1 + n_out:]
        acc, s_sems = (scratch[:1], scratch[1:]) if nk > 1 else ((), scratch)
        if side:
            step = (pl.program_id(0) * grid[1] + pl.program_id(1)) * grid[2] + pl.program_id(2)

            @pl.when(step == 0)
            def _():
                side.start(s_ins, s_outs, s_sems)

        part = _dg(a_ref[...].astype(BF16), b_ref[...].astype(BF16), ca, cb)
        if nk == 1:
            o_ref[...] = part.astype(o_ref.dtype)
        else:
            acc_ref, kk = acc[0], pl.program_id(2)

            @pl.when(kk == 0)
            def _():
                acc_ref[...] = part

            @pl.when(kk > 0)
            def _():
                acc_ref[...] += part

            @pl.when(kk == nk - 1)
            def _():
                o_ref[...] = acc_ref[...].astype(o_ref.dtype)

        if side:
            @pl.when(step == grid[0] * grid[1] * grid[2] - 1)
            def _():
                side.finish(s_ins, s_outs, s_sems)

    res = pl.pallas_call(
        body, name=name, grid=grid,
        in_specs=[a_spec, b_spec] + [_ANY] * (n_dep + n_in),
        out_specs=[pl.BlockSpec((tm, tn_), lambda i, j, kk: (i, j))] + [_ANY] * n_out,
        out_shape=[jax.ShapeDtypeStruct((m, n), out_dtype)] + (list(side.outs) if side else []),
        scratch_shapes=([pltpu.VMEM((tm, tn_), F32)] if nk > 1 else []) + (list(side.sems) if side else []),
        compiler_params=pltpu.CompilerParams(
            dimension_semantics=("arbitrary",) * 3 if side else ("parallel", "parallel", "arbitrary"),
            vmem_limit_bytes=VMEM_BIG),
    )(a, b, *after, *(side.ins if side else []))
    return (res[0], res[1:]) if side else res[0]


def _rspec(tr, width, blk):
    return pl.BlockSpec((tr, width), lambda i: (i, blk))


def _full_spec(arr):
    return pl.BlockSpec(arr.shape, lambda i: (0,) * arr.ndim)


class Stage:
    def __init__(self, name, f, outs, tr, diff_rows, diff_params, drow_dtypes):
        self.name, self.f, self.outs, self.tr = name, f, outs, tr
        self.diff_rows, self.diff_params, self.drow_dtypes = diff_rows, diff_params, drow_dtypes

    def fwd(self, rows, params):
        f, nr, npar = self.f, len(rows), len(params)
        stored = [(w, dt) for (w, dt) in self.outs if dt is not None]
        keep = [i for i, (w, dt) in enumerate(self.outs) if dt is not None]

        def body(*refs):
            vals = f(*[r[...].astype(F32) for r in refs[:nr]], *[p[...] for p in refs[nr:nr + npar]])
            for o_ref, i in zip(refs[nr + npar:], keep):
                o_ref[...] = vals[i].astype(o_ref.dtype)

        return pl.pallas_call(
            body, name=self.name + "_fwd", grid=(T // self.tr,),
            in_specs=[_rspec(self.tr, w, b) for (_, w, b) in rows] + [_full_spec(p) for p in params],
            out_specs=[_rspec(self.tr, w, 0) for (w, _) in stored],
            out_shape=[jax.ShapeDtypeStruct((T, w), dt) for (w, dt) in stored],
            compiler_params=pltpu.CompilerParams(dimension_semantics=("arbitrary",), vmem_limit_bytes=VMEM_BIG),
        )(*[r[0] for r in rows], *params)

    def bwd(self, rows, params, cts):
        f, nr, npar = self.f, len(rows), len(params)
        dr_idx, dp_idx = self.diff_rows, self.diff_params
        flat_cts = [c for lst in cts for c in lst]
        nct = len(flat_cts)

        def body(*refs):
            row_refs, par_refs = refs[:nr], refs[nr:nr + npar]
            ct_refs = refs[nr + npar:nr + npar + nct]
            drow_refs = refs[nr + npar + nct:nr + npar + nct + len(dr_idx)]
            dpar_refs = refs[nr + npar + nct + len(dr_idx):]
            row_vals = [r[...].astype(F32) for r in row_refs]
            par_vals = [p[...] for p in par_refs]

            def g(*dv):
                rv, pv = list(row_vals), list(par_vals)
                for j, i in enumerate(dr_idx):
                    rv[i] = dv[j]
                for j, i in enumerate(dp_idx):
                    pv[i] = dv[len(dr_idx) + j]
                return f(*rv, *pv)

            _, vjp = jax.vjp(g, *[row_vals[i] for i in dr_idx], *[par_vals[i] for i in dp_idx])
            ct_vals, pos = [], 0
            for lst in cts:
                acc = ct_refs[pos][...].astype(F32)
                for q in range(1, len(lst)):
                    acc = acc + ct_refs[pos + q][...].astype(F32)
                pos += len(lst)
                ct_vals.append(acc)
            grads = vjp(tuple(ct_vals))
            for j, r in enumerate(drow_refs):
                r[...] = grads[j].astype(r.dtype)

            @pl.when(pl.program_id(0) == 0)
            def _():
                for r in dpar_refs:
                    r[...] = jnp.zeros_like(r)

            for j, r in enumerate(dpar_refs):
                r[...] += grads[len(dr_idx) + j].astype(F32)

        drow_shapes = [jax.ShapeDtypeStruct((T, rows[i][1]), dt) for i, dt in zip(dr_idx, self.drow_dtypes)]
        dpar_shapes = [jax.ShapeDtypeStruct(params[i].shape, F32) for i in dp_idx]
        res = pl.pallas_call(
            body, name=self.name + "_bwd", grid=(T // self.tr,),
            in_specs=[_rspec(self.tr, w, b) for (_, w, b) in rows] + [_full_spec(p) for p in params]
            + [_rspec(self.tr, w, b) for (_, w, b) in flat_cts],
            out_specs=[_rspec(self.tr, rows[i][1], 0) for i in dr_idx] + [_full_spec(params[i]) for i in dp_idx],
            out_shape=drow_shapes + dpar_shapes,
            compiler_params=pltpu.CompilerParams(dimension_semantics=("arbitrary",), vmem_limit_bytes=VMEM_BIG),
        )(*[r[0] for r in rows], *params, *[c[0] for c in flat_cts])
        return res[:len(dr_idx)], res[len(dr_idx):]


def f_pre(x, g):
    return _rms(x, g), x


@jax.custom_vjp
def swap32(t):
    width = t.shape[1]
    lane = lax.broadcasted_iota(jnp.int32, t.shape, 1) % LANE
    return jnp.where(lane < 32, pltpu.roll(t, width - 32, 1), jnp.where(lane < 64, pltpu.roll(t, 32, 1), 0.0))


swap32.defvjp(lambda t: (swap32(t), None), lambda _, g: (swap32(g),))


def f_mla(q_a, kv_a, kr, cq, sq, ck, sk, gq, gkv, wq, wkv):
    q = bdot(_rms(q_a, gq), wq)
    kv = bdot(_rms(kv_a, gkv), wkv)
    t, k = q[:, 1024:], kr[:, :LANE]
    return (q[:, :1024], t * cq + swap32(t) * sq, kv[:, :1024], k * ck + swap32(k) * sk, kv[:, 1024:])


def f_rwkv_pre(lerp, w0f, w0b, a0f, a0b, kkw, kaw, w2f, w2b, a2f, a2b, g2):
    r, k, v = lerp[:, :RW], lerp[:, RW:2 * RW], lerp[:, 2 * RW:3 * RW]
    wdf, wdb, adf, adb = (lerp[:, 3 * RW + i * LANE:3 * RW + (i + 1) * LANE] for i in range(4))

    def logdecay(w0, wd, w2):
        z = w0 + bdot(jnp.tanh(wd), w2)
        return -jnp.exp(-_softplus(-z) - 0.5)

    a_f = jax.nn.sigmoid(a0f + bdot(adf, a2f))
    a_b = jax.nn.sigmoid(a0b + bdot(adb, a2b))
    kk = k * kkw
    kk = kk / jnp.maximum(jnp.sqrt(headsum(kk * kk, g2)), 1e-12)
    return (r, v, logdecay(w0f, wdf, w2f), logdecay(w0b, wdb, w2b),
            k * (1.0 + (a_f - 1.0) * kaw), k * (1.0 + (a_b - 1.0) * kaw), -kk, kk * a_f, kk * a_b)


def f_rwkv_post(yf, yb, r, kf, kb, v, z, gng, gnb, rk, g2):
    y = yf + yb
    mu = headsum(y, g2) * (1.0 / RN)
    d = y - mu
    var = headsum(d * d, g2) * (1.0 / RN)
    yn = d * lax.rsqrt(var + GN_EPS) * gng + gnb
    bonus = headsum(r * (kf + kb) * rk, g2) * v
    return ((yn + bonus) * _silu(z),)


def f_gate(y, z):
    return (y * _silu(z),)


def f_merge(um, ur, gm, gr):
    return (jax.nn.sigmoid(gm) * um + jax.nn.sigmoid(gr) * ur,)


_SHIFT_W = 256


def _lerp_colblock(j):
    return jnp.where(j < 3072 // _SHIFT_W, OFF_RKV // _SHIFT_W + j, OFF_LORA // _SHIFT_W + j - 3072 // _SHIFT_W)


def _nbr_mean(x):
    row = lax.broadcasted_iota(jnp.int32, x.shape, 0)
    up = jnp.where(row == 0, 0.0, pltpu.roll(x, 1, 0))
    dn = jnp.where(row == T - 1, 0.0, pltpu.roll(x, T - 1, 0))
    return 0.5 * (up + dn)


def shift_fwd(proj, mu):
    def body(x_ref, mu_ref, o_ref):
        x = x_ref[...]
        o_ref[...] = x + mu_ref[...] * (_nbr_mean(x) - x)

    return pl.pallas_call(
        body, name="shift_fwd", grid=(NLERP // _SHIFT_W,),
        in_specs=[pl.BlockSpec((T, _SHIFT_W), lambda j: (0, _lerp_colblock(j))),
                  pl.BlockSpec((1, _SHIFT_W), lambda j: (0, j))],
        out_specs=pl.BlockSpec((T, _SHIFT_W), lambda j: (0, j)),
        out_shape=jax.ShapeDtypeStruct((T, NLERP), F32),
        compiler_params=pltpu.CompilerParams(dimension_semantics=("parallel",), vmem_limit_bytes=VMEM_BIG),
    )(proj, mu)


def shift_bwd(proj, mu, g):
    def body(x_ref, mu_ref, g_ref, dx_ref, dmu_ref):
        x, gv = x_ref[...], g_ref[...]
        dmu_ref[...] = jnp.sum(gv * (_nbr_mean(x) - x), axis=0, keepdims=True)
        gm = gv * mu_ref[...]
        dx_ref[...] = (gv - gm + _nbr_mean(gm)).astype(dx_ref.dtype)

    col = pl.BlockSpec((T, _SHIFT_W), lambda j: (0, j))
    vec = pl.BlockSpec((1, _SHIFT_W), lambda j: (0, j))
    return pl.pallas_call(
        body, name="shift_bwd", grid=(NLERP // _SHIFT_W,),
        in_specs=[pl.BlockSpec((T, _SHIFT_W), lambda j: (0, _lerp_colblock(j))), vec, col],
        out_specs=[col, vec],
        out_shape=[jax.ShapeDtypeStruct((T, NLERP), BF16), jax.ShapeDtypeStruct((1, NLERP), F32)],
        compiler_params=pltpu.CompilerParams(dimension_semantics=("parallel",), vmem_limit_bytes=VMEM_BIG),
    )(proj, mu, g)


_TQ = 512
_ATT_SCALE = (NOPE + ROPE) ** -0.5


def _probs(q, k):
    s = _dg(q, k, 1, 1) * _ATT_SCALE
    e = jnp.exp(s - jnp.max(s, axis=-1, keepdims=True))
    return e * (1.0 / jnp.sum(e, axis=-1, keepdims=True))


_Q_BLK = pl.BlockSpec((_TQ, LANE), lambda h, i: (i, h))
_K_BLK = pl.BlockSpec((T, LANE), lambda h, i: (0, h))
_KR_BLK = pl.BlockSpec((T, LANE), lambda h, i: (0, 0))


def _load_qk(qn_ref, qr_ref, kn_ref, kr_ref, kcat_ref):
    @pl.when(pl.program_id(1) == 0)
    def _():
        kcat_ref[:, :LANE] = kn_ref[...]
        kcat_ref[:, LANE:] = kr_ref[...]

    return jnp.concatenate([qn_ref[...], qr_ref[...]], axis=1), kcat_ref[...]


def attn_fwd(qn, qr, kn, kr, v):
    def body(qn_ref, qr_ref, kn_ref, kr_ref, v_ref, o_ref, kcat_ref):
        q, k = _load_qk(qn_ref, qr_ref, kn_ref, kr_ref, kcat_ref)
        o_ref[...] = _dg(_probs(q, k).astype(BF16), v_ref[...], 1, 0)

    return pl.pallas_call(
        body, name="attn_fwd", grid=(HEADS, T // _TQ),
        in_specs=[_Q_BLK, _Q_BLK, _K_BLK, _KR_BLK, _K_BLK], out_specs=_Q_BLK,
        out_shape=jax.ShapeDtypeStruct((T, HEADS * VDIM), F32),
        scratch_shapes=[pltpu.VMEM((T, 2 * LANE), BF16)],
        compiler_params=pltpu.CompilerParams(dimension_semantics=("arbitrary", "arbitrary"), vmem_limit_bytes=VMEM_BIG),
    )(qn, qr, kn, kr, v)


def attn_bwd(qn, qr, kn, kr, v, do):
    def body(qn_ref, qr_ref, kn_ref, kr_ref, v_ref, do_ref, dqn_ref, dqr_ref, dkn_ref, dkr_ref, dv_ref, kcat_ref):
        h, i = pl.program_id(0), pl.program_id(1)

        @pl.when(i == 0)
        def _():
            dkn_ref[...] = jnp.zeros_like(dkn_ref)
            dv_ref[...] = jnp.zeros_like(dv_ref)

        @pl.when((i == 0) & (h == 0))
        def _():
            dkr_ref[...] = jnp.zeros_like(dkr_ref)

        q, k = _load_qk(qn_ref, qr_ref, kn_ref, kr_ref, kcat_ref)
        dob = do_ref[...].astype(BF16)
        p = _probs(q, k)
        dv_ref[...] += _dg(p.astype(BF16), dob, 0, 0)
        dp = _dg(dob, v_ref[...], 1, 1)
        ds = (p * (dp - jnp.sum(dp * p, axis=-1, keepdims=True)) * _ATT_SCALE).astype(BF16)
        dq = _dg(ds, k, 1, 0)
        dqn_ref[...] = dq[:, :LANE]
        dqr_ref[...] = dq[:, LANE:]
        dk = _dg(ds, q, 0, 0)
        dkn_ref[...] += dk[:, :LANE]
        dkr_ref[...] += dk[:, LANE:]

    wide = jax.ShapeDtypeStruct((T, HEADS * LANE), F32)
    return pl.pallas_call(
        body, name="attn_bwd", grid=(HEADS, T // _TQ),
        in_specs=[_Q_BLK, _Q_BLK, _K_BLK, _KR_BLK, _K_BLK, _Q_BLK],
        out_specs=[_Q_BLK, _Q_BLK, _K_BLK, _KR_BLK, _K_BLK],
        out_shape=[wide, wide, wide, jax.ShapeDtypeStruct((T, LANE), F32), wide],
        scratch_shapes=[pltpu.VMEM((T, 2 * LANE), BF16)],
        compiler_params=pltpu.CompilerParams(dimension_semantics=("arbitrary", "arbitrary"), vmem_limit_bytes=VMEM_BIG),
    )(qn, qr, kn, kr, v, do)


def _chunk(r, lw, k, v, a, b, ht, *, reverse):
    hb, c, _ = r.shape
    ti = lax.broadcasted_iota(jnp.int32, (c, c), 0)
    si = lax.broadcasted_iota(jnp.int32, (c, c), 1)
    incl = (si >= ti) if reverse else (si <= ti)
    strict = (si > ti) if reverse else (si < ti)
    ones = jnp.broadcast_to(incl.astype(F32)[None], (hb, c, c))
    cum = cumdot(ones, lw)
    cum_ex = cum - lw
    tot = jnp.sum(lw, axis=1, keepdims=True)
    mid = 0.5 * tot
    rt, at = r * jnp.exp(cum - mid), a * jnp.exp(cum_ex - mid)
    einv = jnp.exp(mid - cum)
    kt, bt = k * einv, b * einv
    m_ab = jnp.where(strict, nt(at, bt), 0.0)
    m_ak = jnp.where(strict, nt(at, kt), 0.0)
    m_rb = jnp.where(incl, nt(rt, bt), 0.0)
    m_rk = jnp.where(incl, nt(rt, kt), 0.0)
    u = tri_solve(m_ab, nt(a * jnp.exp(cum_ex), ht) + nn(m_ak, v))
    y = nt(r * jnp.exp(cum), ht) + nn(m_rb, u) + nn(m_rk, v)
    eend = jnp.exp(tot - cum)
    ht_new = ht * jnp.exp(tot) + tn(u, b * eend) + tn(v, k * eend)
    return y, ht_new


_HB_F, _HB_B = 16, 16


def _split_heads(x):
    return jnp.stack([x[:, i * RN:(i + 1) * RN] for i in range(x.shape[1] // RN)])


def _merge_heads(y):
    return jnp.concatenate([y[i] for i in range(y.shape[0])], axis=1)


def _chunk_map(reverse, backward):
    flip = reverse != backward
    return (lambda g, c: (NCH - 1 - c, g)) if flip else (lambda g, c: (c, g))


def scan_fwd(name, r, lw, k, v, a, b, reverse):
    hb = _HB_F
    cmap = _chunk_map(reverse, False)

    def body(r_ref, lw_ref, k_ref, v_ref, a_ref, b_ref, y_ref, h0_ref, ht_ref):
        @pl.when(pl.program_id(1) == 0)
        def _():
            ht_ref[...] = jnp.zeros_like(ht_ref)

        ht = ht_ref[...]
        h0_ref[0] = ht
        ins = [_split_heads(x[...]) for x in (r_ref, lw_ref, k_ref, v_ref, a_ref, b_ref)]
        y, hn = _chunk(*ins, ht, reverse=reverse)
        y_ref[...] = _merge_heads(y)
        ht_ref[...] = hn

    io = pl.BlockSpec((CHUNK, hb * RN), cmap)
    return pl.pallas_call(
        body, name=name, grid=(RH // hb, NCH),
        in_specs=[io] * 6,
        out_specs=[io, pl.BlockSpec((1, hb, RN, RN), lambda g, c: (cmap(g, c)[0], g, 0, 0))],
        out_shape=[jax.ShapeDtypeStruct((T, RW), F32), jax.ShapeDtypeStruct((NCH, RH, RN, RN), F32)],
        scratch_shapes=[pltpu.VMEM((hb, RN, RN), F32)],
        compiler_params=pltpu.CompilerParams(dimension_semantics=("parallel", "arbitrary"), vmem_limit_bytes=VMEM_BIG),
    )(r, lw, k, v, a, b)


def scan_bwd(name, r, lw, k, v, a, b, h0, dy, reverse):
    hb = _HB_B
    cmap = _chunk_map(reverse, True)

    def body(r_ref, lw_ref, k_ref, v_ref, a_ref, b_ref, h0_ref, dy_ref, *rest):
        d_refs, dht_ref = rest[:6], rest[6]

        @pl.when(pl.program_id(1) == 0)
        def _():
            dht_ref[...] = jnp.zeros_like(dht_ref)

        ins = [_split_heads(x[...]) for x in (r_ref, lw_ref, k_ref, v_ref, a_ref, b_ref)]
        _, vjp = jax.vjp(functools.partial(_chunk, reverse=reverse), *ins, h0_ref[0])
        grads = vjp((_split_heads(dy_ref[...]), dht_ref[...]))
        for d_ref, gval in zip(d_refs, grads[:6]):
            d_ref[...] = _merge_heads(gval)
        dht_ref[...] = grads[6]

    io = pl.BlockSpec((CHUNK, hb * RN), cmap)
    return pl.pallas_call(
        body, name=name, grid=(RH // hb, NCH),
        in_specs=[io] * 6 + [pl.BlockSpec((1, hb, RN, RN), lambda g, c: (cmap(g, c)[0], g, 0, 0)), io],
        out_specs=[io] * 6,
        out_shape=[jax.ShapeDtypeStruct((T, RW), F32)] * 6,
        scratch_shapes=[pltpu.VMEM((hb, RN, RN), F32)],
        compiler_params=pltpu.CompilerParams(dimension_semantics=("parallel", "arbitrary"), vmem_limit_bytes=VMEM_BIG),
    )(r, lw, k, v, a, b, h0, dy)


def loss_stage(out, x2, tgt, g_post):
    tr = 256

    def body(o_ref, x_ref, t_ref, g_ref, do_ref, dy_ref, dg_ref, loss_ref):
        @pl.when(pl.program_id(0) == 0)
        def _():
            dg_ref[...] = jnp.zeros_like(dg_ref)
            loss_ref[...] = jnp.zeros_like(loss_ref)

        nrm, vjp = jax.vjp(_rms, o_ref[...], g_ref[...])
        e = x_ref[...] + nrm - t_ref[...]
        s = jnp.sum(jnp.sum(e * e, axis=1, keepdims=True), axis=0, keepdims=True)
        loss_ref[...] += jnp.broadcast_to(s * (0.5 / D), loss_ref.shape)
        dy = e * (1.0 / D)
        do, dg = vjp(dy)
        do_ref[...] = do.astype(do_ref.dtype)
        dy_ref[...] = dy
        dg_ref[...] += dg

    row = pl.BlockSpec((tr, D), lambda i: (i, 0))
    return pl.pallas_call(
        body, name="loss_stage", grid=(T // tr,),
        in_specs=[row, row, row, pl.BlockSpec((1, D), lambda i: (0, 0))],
        out_specs=[row, row, pl.BlockSpec((1, D), lambda i: (0, 0)), pl.BlockSpec((8, LANE), lambda i: (0, 0))],
        out_shape=[jax.ShapeDtypeStruct((T, D), BF16), jax.ShapeDtypeStruct((T, D), F32),
                   jax.ShapeDtypeStruct((1, D), F32), jax.ShapeDtypeStruct((8, LANE), F32)],
        compiler_params=pltpu.CompilerParams(dimension_semantics=("arbitrary",), vmem_limit_bytes=VMEM_BIG),
    )(out, x2, tgt, g_post)


_EW_BLOCK_BYTES = 1 << 20


def _row_tile(rows, cols):
    best = None
    for tr in range(16, rows + 1, 16):
        if rows % tr == 0 and tr * cols * 4 <= _EW_BLOCK_BYTES:
            best = tr
    return best or rows


def _axis_tile(shape, axis, words):
    rows, cols = shape
    n, other, unit = (rows, cols, 16) if axis == 0 else (cols, rows, LANE)
    best = unit if n % unit == 0 else n
    for t in range(unit, n + 1, unit):
        if n % t == 0 and t * other * words * 4 <= _EW_BLOCK_BYTES:
            best = t
    blk = (best, cols) if axis == 0 else (rows, best)
    at = (lambda s: (s, 0)) if axis == 0 else (lambda s: (0, s))
    return blk, n // best, at


def _adamw_update(g, w_ref, m_ref, v_ref, g_ref, d_ref, nm_ref, nv_ref):
    mm = ADAM_B1 * m_ref[...] + (1.0 - ADAM_B1) * g
    vv = ADAM_B2 * v_ref[...] + (1.0 - ADAM_B2) * (g * g)
    m_hat = mm / (1.0 - ADAM_B1 ** ADAM_STEP)
    v_hat = vv / (1.0 - ADAM_B2 ** ADAM_STEP)
    g_ref[...] = g
    d_ref[...] = -ADAM_LR * (m_hat / (jnp.sqrt(v_hat) + ADAM_EPS) + ADAM_WD * w_ref[...])
    nm_ref[...] = mm
    nv_ref[...] = vv


def adamw(name, w, m, v, parts):
    rows, cols = w.shape
    br = _row_tile(rows, cols)
    npart = len(parts)

    def body(w_ref, m_ref, v_ref, *rest):
        g = rest[0][...].astype(F32)
        for p in rest[1:npart]:
            g = g + p[...].astype(F32)
        _adamw_update(g, w_ref, m_ref, v_ref, *rest[npart:])

    blk = pl.BlockSpec((br, cols), lambda i: (i, 0))
    return pl.pallas_call(
        body, name=name, grid=(rows // br,),
        in_specs=[blk] * (3 + npart), out_specs=[blk] * 4,
        out_shape=[jax.ShapeDtypeStruct((rows, cols), F32)] * 4,
        compiler_params=pltpu.CompilerParams(dimension_semantics=("parallel",), vmem_limit_bytes=VMEM_BIG),
    )(w, m, v, *parts)


def adamw_halves(name, place, w, m, v, mine, theirs, axis):
    half_shape = mine.shape
    blk_shape, nb, at = _axis_tile(half_shape, axis, 1)

    def body(p_ref, w_ref, m_ref, v_ref, a_ref, b_ref, *outs):
        own = (pl.program_id(0) // nb) == p_ref[0]
        _adamw_update(jnp.where(own, a_ref[...], b_ref[...]), w_ref, m_ref, v_ref, *outs)

    blk = pl.BlockSpec(blk_shape, lambda i, p: at(i))
    half = pl.BlockSpec(blk_shape, lambda i, p: at(i % nb))
    return pl.pallas_call(
        body, name=name,
        grid_spec=pltpu.PrefetchScalarGridSpec(num_scalar_prefetch=1, grid=(2 * nb,),
                                               in_specs=[blk] * 3 + [half] * 2, out_specs=[blk] * 4),
        out_shape=[jax.ShapeDtypeStruct(w.shape, F32)] * 4,
        compiler_params=pltpu.CompilerParams(dimension_semantics=("arbitrary",), vmem_limit_bytes=VMEM_BIG),
    )(place, w, m, v, mine, theirs)


def pair_sum(name, place, send, other, axis):
    blk_shape, nb, at = _axis_tile(other.shape[1:], axis, 4)

    def body(p_ref, a_ref, b_ref, o_ref):
        o_ref[...] = (a_ref[...].astype(F32) + b_ref[...].astype(F32)).astype(o_ref.dtype)

    blk = pl.BlockSpec((4,) + blk_shape, lambda i, p: (0,) + at(i))
    mine = pl.BlockSpec((4,) + blk_shape, lambda i, p: (0,) + at(p[0] * nb + i))
    return pl.pallas_call(
        body, name=name,
        grid_spec=pltpu.PrefetchScalarGridSpec(num_scalar_prefetch=1, grid=(nb,), in_specs=[mine, blk], out_specs=blk),
        out_shape=jax.ShapeDtypeStruct(other.shape, BF16),
        compiler_params=pltpu.CompilerParams(dimension_semantics=("arbitrary",), vmem_limit_bytes=VMEM_BIG),
    )(place, send, other)


def sum4(name, place, recv, own, axis):
    blk_shape, nb, at = _axis_tile(recv.shape[1:], axis, 4)

    def body(p_ref, r_ref, s_ref, o_ref):
        me = p_ref[0]
        t = [jnp.where(me == j, s_ref[j], r_ref[j]).astype(F32) for j in range(4)]
        o_ref[...] = ((t[0] + t[1]) + t[2]) + t[3]

    blk = pl.BlockSpec((4,) + blk_shape, lambda i, p: (0,) + at(i))
    return pl.pallas_call(
        body, name=name,
        grid_spec=pltpu.PrefetchScalarGridSpec(num_scalar_prefetch=1, grid=(nb,), in_specs=[blk, blk],
                                               out_specs=pl.BlockSpec(blk_shape, lambda i, p: at(i))),
        out_shape=jax.ShapeDtypeStruct(recv.shape[1:], F32),
        compiler_params=pltpu.CompilerParams(dimension_semantics=("arbitrary",), vmem_limit_bytes=VMEM_BIG),
    )(place, recv, own)


_ANY = pl.BlockSpec(memory_space=pl.ANY)


def _place():
    x, y, c = lax.axis_index("x"), lax.axis_index("y"), lax.axis_index("c")
    return x, y, c, 2 * x + y


def _chip_peers(x, y):
    out = []
    for k in (1, 2, 3):
        px = 1 - x if k & 2 else x
        py = 1 - y if k & 1 else y
        out.append((k, px, py, 2 * px + py))
    return out


def _half(c, shape, axis):
    n = shape[axis] // 2
    sl = pl.ds(pl.multiple_of(c * n, 16 if axis == 0 else LANE), n)
    return (sl,) if axis == 0 else (pl.ds(0, shape[0]), sl)


def gather_weights(srcs, axes):
    side = gather_side(srcs, axes)
    n = len(srcs)

    def body(*refs):
        ins, outs, sems = refs[:n], refs[n:2 * n], refs[2 * n:]
        side.start(ins, outs, sems)
        side.finish(ins, outs, sems)

    return pl.pallas_call(
        body, name="gather_weights", in_specs=[_ANY] * n, out_specs=[_ANY] * n,
        out_shape=side.outs, scratch_shapes=side.sems,
    )(*srcs)


def gather_side(srcs, axes):
    n = len(srcs)

    def copies(src, dst, sems, sends_only=False):
        ssem, rsem, fssem, frsem = sems
        x, y, c, me = _place()
        sib = (x, y, 1 - c)
        out = []
        for i in range(n):
            mine, other = _half(c, srcs[i].shape, axes[i]), _half(1 - c, srcs[i].shape, axes[i])
            for k, px, py, peer in _chip_peers(x, y):
                sems_k = dict(send_sem=ssem.at[i, k - 1], recv_sem=rsem.at[i, k - 1], device_id=(px, py, c),
                              device_id_type=MESH_IDS)
                fsems = dict(send_sem=fssem.at[i, k - 1], recv_sem=frsem.at[i, k - 1], device_id=sib,
                             device_id_type=MESH_IDS)
                got = dst[i].at[(peer,) + mine]
                snd = pltpu.make_async_remote_copy(src_ref=src[i].at[mine], dst_ref=dst[i].at[(me,) + mine], **sems_k)
                if sends_only:
                    out.append(snd)
                    continue
                out.append((
                    snd,
                    pltpu.make_async_remote_copy(src_ref=src[i].at[mine], dst_ref=got, **sems_k),
                    pltpu.make_async_remote_copy(src_ref=got, dst_ref=got, **fsems),
                    pltpu.make_async_remote_copy(src_ref=got, dst_ref=dst[i].at[(peer,) + other], **fsems)))
        return out

    def start(src, dst, sems):
        for snd in copies(src, dst, sems, sends_only=True):
            snd.start()

    def finish(src, dst, sems):
        cps = copies(src, dst, sems)
        for _, rcv, fwd, _ in cps:
            rcv.wait_recv()
            fwd.start()
        for snd, _, fwd, frcv in cps:
            frcv.wait_recv()
            snd.wait_send()
            fwd.wait_send()

    return Side(list(srcs), [jax.ShapeDtypeStruct((4,) + s.shape, s.dtype) for s in srcs],
                [pltpu.SemaphoreType.DMA((n, 3))] * 4, start, finish)


def pair_exchange(name, srcs, axes):
    n = len(srcs)

    def half_shape(s, axis):
        return (4, s.shape[1] // 2, s.shape[2]) if axis == 0 else (4, s.shape[1], s.shape[2] // 2)

    def body(*refs):
        src, other = refs[:n], refs[n:2 * n]
        ssem, rsem = refs[2 * n:]
        x, y, c, _ = _place()
        cps = []
        for i in range(n):
            idx = (pl.ds(0, 4),) + _half(1 - c, srcs[i].shape[1:], axes[i])
            cps.append(pltpu.make_async_remote_copy(
                src_ref=src[i].at[idx], dst_ref=other[i], send_sem=ssem.at[i], recv_sem=rsem.at[i],
                device_id=(x, y, 1 - c), device_id_type=MESH_IDS))
            cps[-1].start()
        for cp in cps:
            cp.wait()

    return pl.pallas_call(
        body, name=name, in_specs=[_ANY] * n, out_specs=[_ANY] * n,
        out_shape=[jax.ShapeDtypeStruct(half_shape(s, a), s.dtype) for s, a in zip(srcs, axes)],
        scratch_shapes=[pltpu.SemaphoreType.DMA((n,))] * 2,
    )(*srcs)


def scatter_grads(srcs):
    side = scatter_side(srcs)
    n = len(srcs)

    def body(*refs):
        ins, outs, sems = refs[:n], refs[n:2 * n], refs[2 * n:]
        side.start(ins, outs, sems)
        side.finish(ins, outs, sems)

    return pl.pallas_call(
        body, name="scatter_grads", in_specs=[_ANY] * n, out_specs=[_ANY] * n,
        out_shape=side.outs, scratch_shapes=side.sems,
    )(*srcs)


def scatter_side(srcs):
    n = len(srcs)

    def copies(src, dst, sems, sends_only=False):
        ssem, rsem = sems
        x, y, c, me = _place()
        out = []
        for i in range(n):
            for k, px, py, peer in _chip_peers(x, y):
                sems_k = dict(send_sem=ssem.at[i, k - 1], recv_sem=rsem.at[i, k - 1], device_id=(px, py, c),
                              device_id_type=MESH_IDS)
                snd = pltpu.make_async_remote_copy(src_ref=src[i].at[peer], dst_ref=dst[i].at[me], **sems_k)
                if sends_only:
                    out.append(snd)
                    continue
                out.append((snd, pltpu.make_async_remote_copy(src_ref=src[i].at[peer], dst_ref=dst[i].at[peer],
                                                              **sems_k)))
        return out

    def start(src, dst, sems):
        for snd in copies(src, dst, sems, sends_only=True):
            snd.start()

    def finish(src, dst, sems):
        for snd, rcv in copies(src, dst, sems):
            rcv.wait_recv()
            snd.wait_send()

    return Side(list(srcs), [jax.ShapeDtypeStruct(s.shape, s.dtype) for s in srcs],
                [pltpu.SemaphoreType.DMA((n, 3))] * 2, start, finish)


_HBM = pl.BlockSpec(memory_space=pltpu.HBM)
_SEM = pl.BlockSpec(memory_space=pltpu.SEMAPHORE)
_DATAFLOW = pltpu.SideEffectType.DATAFLOW_SIDE_EFFECTING


def scatter_start(name, srcs):
    n = len(srcs)
    side = scatter_side(srcs)
    ns = 3 * n

    def body(*refs):
        src, land = refs[:n], refs[n:2 * n]
        sems = refs[2 * n:2 * n + 2 * ns]
        side.start(src, land, (_SemGrid(sems[:ns]), _SemGrid(sems[ns:])))
        refs[-1][...] = jnp.zeros_like(refs[-1])

    hbm = [pltpu.HBM(s.shape, s.dtype) for s in srcs]
    res = pl.pallas_call(
        body, name=name,
        out_shape=[pltpu.SemaphoreType.DMA(())] * (2 * ns) + hbm + hbm + [jax.ShapeDtypeStruct((8, LANE), F32)],
        in_specs=[_HBM] * (2 * n),
        out_specs=[_SEM] * (2 * ns) + [_HBM] * (2 * n) + [pl.BlockSpec(memory_space=pltpu.VMEM)],
        input_output_aliases={i: 2 * ns + i for i in range(2 * n)},
        compiler_params=pltpu.CompilerParams(has_side_effects=_DATAFLOW),
    )(*[pltpu.with_memory_space_constraint(s, pltpu.HBM) for s in srcs],
      *[pltpu.with_memory_space_constraint(lax.empty(s.shape, s.dtype), pltpu.HBM) for s in srcs])
    return res[:2 * ns], res[2 * ns:2 * ns + n], res[2 * ns + n:2 * ns + 2 * n], res[-1]


def scatter_wait(name, sems, srcs, lands, after):
    n = len(srcs)
    side = scatter_side(srcs)
    ns = 3 * n

    def body(*refs):
        src, land = refs[:n], refs[n:2 * n]
        s = refs[2 * n:2 * n + 2 * ns]
        side.finish(src, land, (_SemGrid(s[:ns]), _SemGrid(s[ns:])))

    hbm = [pltpu.HBM(s.shape, s.dtype) for s in srcs]
    res = pl.pallas_call(
        body, name=name, out_shape=hbm + hbm,
        in_specs=[_HBM] * (2 * n) + [_SEM] * (2 * ns) + [_ANY], out_specs=[_HBM] * (2 * n),
        input_output_aliases={i: i for i in range(2 * n)},
        compiler_params=pltpu.CompilerParams(has_side_effects=_DATAFLOW),
    )(*srcs, *lands, *sems, after)
    return res[:n], res[n:]


class _SemGrid:
    def __init__(self, sems):
        self.sems = sems

    @property
    def at(self):
        return self

    def __getitem__(self, ik):
        return self.sems[3 * ik[0] + ik[1]]


def swap_halves(name, srcs):
    n = len(srcs)

    def body(*refs):
        src, dst = refs[:n], refs[n:2 * n]
        ssem, rsem = refs[2 * n:]
        x, y, c, _ = _place()
        cps = []
        for i in range(n):
            cps.append(pltpu.make_async_remote_copy(src_ref=src[i], dst_ref=dst[i], send_sem=ssem.at[i],
                                                    recv_sem=rsem.at[i], device_id=(x, y, 1 - c),
                                                    device_id_type=MESH_IDS))
            cps[-1].start()
        for cp in cps:
            cp.wait()

    return pl.pallas_call(
        body, name=name, in_specs=[_ANY] * n, out_specs=[_ANY] * n,
        out_shape=[jax.ShapeDtypeStruct(s.shape, s.dtype) for s in srcs],
        scratch_shapes=[pltpu.SemaphoreType.DMA((n,))] * 2,
    )(*srcs)


def _ag8_copies(src, dst, sems, sends_only=False):
    x, y, c = lax.axis_index("x"), lax.axis_index("y"), lax.axis_index("c")
    me = 4 * x + 2 * y + c
    out = []
    for k in range(1, 8):
        px = 1 - x if k & 4 else x
        py = 1 - y if k & 2 else y
        pc = 1 - c if k & 1 else c
        peer = 4 * px + 2 * py + pc
        out.append(tuple(pltpu.make_async_remote_copy(
            src_ref=src, dst_ref=dst.at[slot], send_sem=sems[k - 1], recv_sem=sems[7 + k - 1],
            device_id=(px, py, pc), device_id_type=MESH_IDS) for slot in ((me,) if sends_only else (me, peer))))
    return out


def allgather8_start(name, src):
    def body(src_ref, land_ref, *rest):
        for (snd,) in _ag8_copies(src_ref, land_ref, rest[:14], sends_only=True):
            snd.start()

    land = jax.ShapeDtypeStruct((8,) + src.shape, src.dtype)
    res = pl.pallas_call(
        body, name=name,
        out_shape=[pltpu.SemaphoreType.DMA(())] * 14 + [pltpu.HBM(src.shape, src.dtype), pltpu.HBM(land.shape, land.dtype)],
        in_specs=[_HBM, _HBM], out_specs=[_SEM] * 14 + [_HBM, _HBM],
        input_output_aliases={0: 14, 1: 15},
        compiler_params=pltpu.CompilerParams(has_side_effects=_DATAFLOW),
    )(pltpu.with_memory_space_constraint(src, pltpu.HBM),
      pltpu.with_memory_space_constraint(lax.empty(land.shape, land.dtype), pltpu.HBM))
    return res[:14], res[14], res[15]


def allgather8_wait(name, sems, src, land, after):
    def body(src_ref, land_ref, *rest):
        for snd, rcv in _ag8_copies(src_ref, land_ref, rest[:14]):
            rcv.wait_recv()
            snd.wait_send()

    return pl.pallas_call(
        body, name=name, out_shape=[pltpu.HBM(src.shape, src.dtype), pltpu.HBM(land.shape, land.dtype)],
        in_specs=[_HBM, _HBM] + [_SEM] * 14 + [_ANY], out_specs=[_HBM, _HBM],
        input_output_aliases={0: 0, 1: 1},
        compiler_params=pltpu.CompilerParams(has_side_effects=_DATAFLOW),
    )(src, land, *sems, after)


WEIGHTS = ['g_pre', 'w_in', 'mla_q_norm', 'mla_wq_b', 'mla_kv_norm', 'mla_wkv_b', 'rwkv_mu', 'rwkv_w0_f', 'rwkv_w2_f',
           'rwkv_w0_b', 'rwkv_w2_b', 'rwkv_a0_f', 'rwkv_a2_f', 'rwkv_a0_b', 'rwkv_a2_b', 'rwkv_k_k', 'rwkv_k_a',
           'rwkv_r_k', 'rwkv_gn_g', 'rwkv_gn_b', 'w_br_mla', 'w_br_rwkv', 'w_out', 'g_post']
BIG_SHAPES = {'w_in': (D_IN // 4, D), 'mla_wq_b': (Q_RANK, 384), 'mla_wkv_b': (KV_RANK, 512),
              'rwkv_w2_f': (LORA, 256), 'rwkv_w2_b': (LORA, 256), 'rwkv_a2_f': (LORA, 256), 'rwkv_a2_b': (LORA, 256),
              'w_br_mla': (RW, 512), 'w_br_rwkv': (RW, 512), 'w_out': (512, D)}
BIG = list(BIG_SHAPES)
SMALL = [n for n in WEIGHTS if n not in BIG_SHAPES]
SMALL_SHAPES = {'g_pre': (D,), 'mla_q_norm': (Q_RANK,), 'mla_kv_norm': (KV_RANK,), 'rwkv_mu': (3456,),
                'rwkv_w0_f': (RW,), 'rwkv_w0_b': (RW,), 'rwkv_a0_f': (RW,), 'rwkv_a0_b': (RW,), 'rwkv_k_k': (RW,),
                'rwkv_k_a': (RW,), 'rwkv_r_k': (RH, RN), 'rwkv_gn_g': (RW,), 'rwkv_gn_b': (RW,), 'g_post': (D,)}
SMALL_LEN = sum(int(np.prod(s)) for s in SMALL_SHAPES.values())
SMALL_ROWS = 144


UNITS = [('w_in',), ('mla_wq_b',), ('mla_wkv_b',), ('rwkv_w2_f', 'rwkv_w2_b', 'rwkv_a2_f', 'rwkv_a2_b'),
         ('w_br_mla', 'w_br_rwkv'), ('w_out',)]
UNIT_AXIS = [1, 0, 0, 0, 0, 0]
ROW_SHARDED = ('w_in', 'w_out')


def _unit_cat(parts):
    return parts[0] if len(parts) == 1 else jnp.concatenate(parts, axis=0)


def _unit_split(arr, names, axis):
    out, o = {}, 0
    for n in names:
        rows = BIG_SHAPES[n][0]
        out[n] = lax.slice_in_dim(arr, o, o + rows, axis=axis)
        o += rows
    return out


def _gathered(units, ag, own, me):
    out = {}
    for names, arr, mine in zip(units, ag, own):
        slots = [jnp.where(me == j, mine, arr[j]) for j in range(4)]
        for n in names:
            parts = [_unit_split(s, names, 0)[n] for s in slots]
            out[n] = jnp.concatenate(parts, axis=0 if n in ROW_SHARDED else 1)
    return out


def _shards(n, g):
    r, w = BIG_SHAPES[n]
    if n in ROW_SHARDED:
        return [g[j * r:(j + 1) * r] for j in range(4)]
    return [g[:, j * w:(j + 1) * w] for j in range(4)]


def _pack_small(d, extra=None):
    flat = jnp.concatenate([d[n].reshape(-1) for n in SMALL] + ([extra.reshape(-1)] if extra is not None else []))
    return jnp.pad(flat, (0, SMALL_ROWS * LANE - flat.shape[0])).reshape(SMALL_ROWS, LANE)


def _unpack_small(packed):
    flat, out, o = packed.reshape(-1), {}, 0
    for n in SMALL:
        sz = int(np.prod(SMALL_SHAPES[n]))
        out[n] = flat[o:o + sz].reshape(SMALL_SHAPES[n])
        o += sz
    return out


def _perm_w_in(gathered, own, me):
    per = D_IN // 4

    def rows(a, b):
        out = []
        while a < b:
            j, lo = divmod(a, per)
            hi = min(b - j * per, per)
            out.append(jnp.where(me == j, own[lo:hi], gathered[j, lo:hi]))
            a = j * per + hi
        return out

    z = lambda n: [jnp.zeros((n, own.shape[1]), own.dtype)]
    lora = []
    for i in range(4):
        lora += rows(4160 + LORA * i, 4160 + LORA * (i + 1)) + z(LANE - LORA)
    return jnp.concatenate(rows(0, 1024) + rows(1088, 4160) + rows(4544, D_IN) + lora + rows(1024, 1088)
                           + z(256 - ROPE), axis=0)


def _unperm_w_in(g):
    lora = [g[OFF_LORA + LANE * i:OFF_LORA + LANE * i + LORA] for i in range(4)]
    return jnp.concatenate([g[0:1024], g[OFF_KR:OFF_KR + ROPE], g[1024:4096]] + lora + [g[4096:OFF_LORA]], axis=0)


def _perm_wq(w):
    w3 = w.reshape(Q_RANK, HEADS, NOPE + ROPE)
    rope = jnp.pad(w3[:, :, NOPE:], ((0, 0), (0, 0), (0, LANE - ROPE)))
    return jnp.concatenate([w3[:, :, :NOPE].reshape(Q_RANK, -1), rope.reshape(Q_RANK, -1)], axis=1)


def _unperm_wq(g):
    return jnp.concatenate([g[:, :1024].reshape(Q_RANK, HEADS, NOPE),
                            g[:, 1024:].reshape(Q_RANK, HEADS, LANE)[:, :, :ROPE]], axis=2).reshape(Q_RANK, -1)


def _perm_wkv(w):
    w3 = w.reshape(KV_RANK, HEADS, NOPE + VDIM)
    return jnp.concatenate([w3[:, :, :NOPE].reshape(KV_RANK, -1), w3[:, :, NOPE:].reshape(KV_RANK, -1)], axis=1)


def _unperm_wkv(g):
    return jnp.concatenate([g[:, :1024].reshape(KV_RANK, HEADS, NOPE), g[:, 1024:].reshape(KV_RANK, HEADS, VDIM)],
                           axis=2).reshape(KV_RANK, -1)


def _pad_rows(w):
    return jnp.pad(w, ((0, LANE - LORA), (0, 0)))


def _perm_mu(mu):
    parts = [mu[:3072]]
    for i in range(4):
        parts += [mu[3072 + LORA * i:3072 + LORA * (i + 1)], jnp.zeros((LANE - LORA,), mu.dtype)]
    return jnp.concatenate(parts).reshape(1, NLERP)


def _unperm_mu(g):
    g = g.reshape(-1)
    return jnp.concatenate([g[:3072]] + [g[3072 + LANE * i:3072 + LANE * i + LORA] for i in range(4)])


def _constants():
    g2 = np.kron(np.eye(2, dtype=np.float32), np.ones((RN, RN), np.float32))
    pos = jnp.arange(T, dtype=F32)
    inv_freq = jnp.power(ROPE_THETA, -jnp.arange(0, ROPE, 2, dtype=F32) / ROPE)
    ang = pos[:, None] * inv_freq[None, :]
    cos, sin, zero = jnp.cos(ang), jnp.sin(ang), jnp.zeros((T, LANE - ROPE), F32)
    cq = jnp.tile(jnp.concatenate([cos, cos, zero], axis=1), (1, HEADS))
    sq = jnp.tile(jnp.concatenate([-sin, sin, zero], axis=1), (1, HEADS))
    return jnp.asarray(g2, BF16), cq, sq


def _step(x, tgt, w, m, v):
    x2, tgt2 = x.reshape(T, D), tgt.reshape(T, D)
    g2, cq, sq = _constants()
    row = lambda n: w[n].reshape(1, -1)
    w, m, v = ({**t, 'w_in': t['w_in'].T} for t in (w, m, v))

    core, chip = lax.axis_index("c"), 2 * lax.axis_index("x") + lax.axis_index("y")
    core1, chip1 = core.astype(jnp.int32).reshape(1), chip.astype(jnp.int32).reshape(1)
    own_bf = [_unit_cat([w[n].astype(BF16) for n in u]) for u in UNITS]
    wp = _perm_w_in(gather_weights(own_bf[:1], UNIT_AXIS[:1])[0], own_bf[0], chip)
    full = {}
    mu_p = _perm_mu(w['rwkv_mu'])

    st_pre = Stage("pre", f_pre, [(D, BF16), (D, None)], 256, [0], [0], [F32])
    st_mla = Stage("mla", f_mla, [(1024, BF16), (1024, BF16), (1024, BF16), (LANE, BF16), (1024, BF16)], 256,
                   [0, 1, 2], [0, 1, 2, 3], [BF16] * 3)
    st_rpre = Stage("rwkv_pre", f_rwkv_pre, [(RW, F32)] * 9, 128, [0], list(range(10)), [F32])
    st_rpost = Stage("rwkv_post", f_rwkv_post, [(RW, BF16)], 256, [0, 2, 3, 4, 5, 6], [0, 1, 2],
                     [F32, F32, F32, F32, F32, BF16])
    st_gate = Stage("gate", f_gate, [(RW, BF16)], 256, [0, 1], [], [F32, BF16])
    st_merge = Stage("merge", f_merge, [(D, BF16)], 256, [0, 1, 2, 3], [], [BF16] * 4)

    pre_rows, pre_par = [(x2, D, 0)], [row('g_pre')]
    (h,) = st_pre.fwd(pre_rows, pre_par)
    proj, rest = matmul("mm_in", h, wp, "nt", side=gather_side(own_bf[1:], UNIT_AXIS[1:]))
    full.update(_gathered(UNITS[1:], rest, own_bf[1:], chip))
    wq, wkv = _perm_wq(full['mla_wq_b']), _perm_wkv(full['mla_wkv_b'])
    lora_w = [_pad_rows(full[n]).astype(F32) for n in ('rwkv_w2_f', 'rwkv_w2_b', 'rwkv_a2_f', 'rwkv_a2_b')]

    mla_rows = [(proj, 512, OFF_QA // 512), (proj, 512, OFF_KVA // 512), (proj, 256, OFF_KR // 256),
                (cq, 1024, 0), (sq, 1024, 0), (cq, LANE, 0), (sq, LANE, 0)]
    mla_par = [row('mla_q_norm'), row('mla_kv_norm'), wq, wkv]
    att = st_mla.fwd(mla_rows, mla_par)
    y_mla = attn_fwd(*att)

    lerp = shift_fwd(proj, mu_p)
    rpre_rows = [(lerp, NLERP, 0)]
    rpre_par = [row('rwkv_w0_f'), row('rwkv_w0_b'), row('rwkv_a0_f'), row('rwkv_a0_b'), row('rwkv_k_k'),
                row('rwkv_k_a')] + lora_w + [g2]
    r_, v_, lwf, lwb, kf, kb, an, bf_, bb_ = st_rpre.fwd(rpre_rows, rpre_par)
    fin = [r_, lwf, kf, v_, an, bf_]
    bin_ = [r_, lwb, kb, v_, an, bb_]
    yf, h0f = scan_fwd("scan_f", *fin, reverse=False)
    yb, h0b = scan_fwd("scan_b", *bin_, reverse=True)
    rpost_rows = [(yf, RW, 0), (yb, RW, 0), (r_, RW, 0), (kf, RW, 0), (kb, RW, 0),
                  (v_, RW, 0), (proj, RW, OFF_ZR // RW)]
    rpost_par = [row('rwkv_gn_g'), row('rwkv_gn_b'), row('rwkv_r_k'), g2]
    (gr,) = st_rpost.fwd(rpost_rows, rpost_par)
    gate_rows = [(y_mla, RW, 0), (proj, RW, OFF_ZM // RW)]
    (gm,) = st_gate.fwd(gate_rows, [])
    um = matmul("mm_br_mla", gm, full['w_br_mla'], "nn")
    ur = matmul("mm_br_rwkv", gr, full['w_br_rwkv'], "nn")
    merge_rows = [(um, D, 0), (ur, D, 0), (proj, D, OFF_GM // D), (proj, D, OFF_GR // D)]
    (merged,) = st_merge.fwd(merge_rows, [])
    out = matmul("mm_out", merged, full['w_out'], "nn")
    d_out, dy, dg_post, loss_blk = loss_stage(out, x2, tgt2, row('g_post'))

    gw = {'g_post': dg_post}
    d_merged = matmul("mm_out_dx", d_out, full['w_out'], "nt")
    gw['w_out'] = matmul("mm_out_dw", merged, d_out, "tn")
    (d_um, d_ur, d_gm, d_gr), _ = st_merge.bwd(merge_rows, [], [[(d_merged, D, 0)]])
    d_gmla = matmul("mm_br_mla_dx", d_um, full['w_br_mla'], "nt")
    gw['w_br_mla'] = matmul("mm_br_mla_dw", gm, d_um, "tn")
    d_grw = matmul("mm_br_rwkv_dx", d_ur, full['w_br_rwkv'], "nt")
    gw['w_br_rwkv'] = matmul("mm_br_rwkv_dw", gr, d_ur, "tn")
    (d_ymla, d_zm), _ = st_gate.bwd(gate_rows, [], [[(d_gmla, RW, 0)]])
    (d_y, d_r3, d_kf2, d_kb2, d_v3, d_zr), (gw['rwkv_gn_g'], gw['rwkv_gn_b'], d_rk) = st_rpost.bwd(
        rpost_rows, rpost_par, [[(d_grw, RW, 0)]])
    gw['rwkv_r_k'] = d_rk
    sf = scan_bwd("scan_f_bwd", *fin, h0f, d_y, reverse=False)
    sb = scan_bwd("scan_b_bwd", *bin_, h0b, d_y, reverse=True)
    c = lambda *ts: [(t, RW, 0) for t in ts]
    rpre_cts = [c(sf[0], sb[0], d_r3), c(sf[3], sb[3], d_v3), c(sf[1]), c(sb[1]), c(sf[2], d_kf2), c(sb[2], d_kb2),
                c(sf[4], sb[4]), c(sf[5]), c(sb[5])]
    (d_rin,), rpre_g = st_rpre.bwd(rpre_rows, rpre_par, rpre_cts)
    for n, gval in zip(('rwkv_w0_f', 'rwkv_w0_b', 'rwkv_a0_f', 'rwkv_a0_b', 'rwkv_k_k', 'rwkv_k_a'), rpre_g[:6]):
        gw[n] = gval
    for n, gval in zip(('rwkv_w2_f', 'rwkv_w2_b', 'rwkv_a2_f', 'rwkv_a2_b'), rpre_g[6:]):
        gw[n] = gval[:LORA]
    d_lerp, d_mu = shift_bwd(proj, mu_p, d_rin)
    gw['rwkv_mu'] = _unperm_mu(d_mu)

    mla_cts = [[(t, t.shape[1], 0)] for t in attn_bwd(*att, d_ymla)]
    (d_qa, d_kva, d_kr), (gw['mla_q_norm'], gw['mla_kv_norm'], d_wq, d_wkv) = st_mla.bwd(mla_rows, mla_par, mla_cts)
    gw['mla_wq_b'], gw['mla_wkv_b'] = _unperm_wq(d_wq), _unperm_wkv(d_wkv)

    dproj = jnp.concatenate([d_qa, d_kva, d_lerp[:, :3072], d_zm, d_zr, d_gm, d_gr, d_lerp[:, 3072:], d_kr], axis=1)

    def pair_sums(name, ids):
        send = [jnp.stack([_unit_cat([_shards(n, gw[n])[j].astype(BF16) for n in UNITS[i]]) for j in range(4)])
                for i in ids]
        axes = [UNIT_AXIS[i] for i in ids]
        other = pair_exchange(name, send, axes)
        return [pair_sum(f"pair_sum_{i}", core1, s, o, ax) for i, s, o, ax in zip(ids, send, other, axes)]

    late, early = [0], list(range(1, len(UNITS)))
    pairs_e = pair_sums("pair_exchange_rest", early)
    gw_in, recv_e = matmul("mm_in_dw", dproj, h, "tn", BF16, side=scatter_side(pairs_e))
    gw['w_in'] = _unperm_w_in(gw_in)
    pairs_l = pair_sums("pair_exchange_w_in", late)
    sems, src_fly, land_fly, token = scatter_start("scatter_w_in_start", pairs_l)
    dh = matmul("mm_in_dx", dproj, wp, "nn", after=(token,))
    (grad_x,), (gw['g_pre'],) = st_pre.bwd(pre_rows, pre_par, [[(dh, D, 0)], [(dy, D, 0)]])

    big = [dict() for _ in range(4)]

    def update(name, ids, recv, pairs):
        mine = [sum4(f"sum4_{i}", chip1, r, p, UNIT_AXIS[i]) for i, r, p in zip(ids, recv, pairs)]
        theirs = swap_halves(name, mine)
        for i, mi, th in zip(ids, mine, theirs):
            res = adamw_halves(f"adamw_{i}", core1, *[_unit_cat([t[n] for n in UNITS[i]]) for t in (w, m, v)], mi, th,
                               UNIT_AXIS[i])
            for q in range(4):
                big[q].update(_unit_split(res[q], UNITS[i], 0))
        return res

    small_fly = allgather8_start("gather_small_start", _pack_small(gw, loss_blk[0, :1]))
    last = update("swap_halves_rest", early, recv_e, pairs_e)
    own_small, landed = allgather8_wait("gather_small_wait", *small_fly, last[0])
    dev = 2 * chip + core
    parts = [jnp.where(dev == i, own_small, landed[i]) for i in range(8)]
    small = adamw("adamw_small", _pack_small(w), _pack_small(m), _pack_small(v), parts)
    pairs_l, recv_l = scatter_wait("scatter_w_in_wait", sems, src_fly, land_fly, small[0] + last[0][:1, :1])
    update("swap_halves_w_in", late, recv_l, pairs_l)

    outs = []
    for b_d, s_arr in zip(big, small):
        d = {**b_d, **_unpack_small(s_arr)}
        d['w_in'] = d['w_in'].T
        outs.append([d[n] for n in WEIGHTS])
    loss = small[0][SMALL_LEN // LANE, 0]
    return (loss, grad_x.reshape(1, T, D), *outs[0], *outs[1], *outs[2], *outs[3])


def kernel(x, g_pre, w_in, mla_q_norm, mla_wq_b, mla_kv_norm, mla_wkv_b, rwkv_mu, rwkv_w0_f, rwkv_w2_f, rwkv_w0_b, rwkv_w2_b, rwkv_a0_f, rwkv_a2_f, rwkv_a0_b, rwkv_a2_b, rwkv_k_k, rwkv_k_a, rwkv_r_k, rwkv_gn_g, rwkv_gn_b, w_br_mla, w_br_rwkv, w_out, g_post, loss_target, m_g_pre, m_w_in, m_mla_q_norm, m_mla_wq_b, m_mla_kv_norm, m_mla_wkv_b, m_rwkv_mu, m_rwkv_w0_f, m_rwkv_w2_f, m_rwkv_w0_b, m_rwkv_w2_b, m_rwkv_a0_f, m_rwkv_a2_f, m_rwkv_a0_b, m_rwkv_a2_b, m_rwkv_k_k, m_rwkv_k_a, m_rwkv_r_k, m_rwkv_gn_g, m_rwkv_gn_b, m_w_br_mla, m_w_br_rwkv, m_w_out, m_g_post, v_g_pre, v_w_in, v_mla_q_norm, v_mla_wq_b, v_mla_kv_norm, v_mla_wkv_b, v_rwkv_mu, v_rwkv_w0_f, v_rwkv_w2_f, v_rwkv_w0_b, v_rwkv_w2_b, v_rwkv_a0_f, v_rwkv_a2_f, v_rwkv_a0_b, v_rwkv_a2_b, v_rwkv_k_k, v_rwkv_k_a, v_rwkv_r_k, v_rwkv_gn_g, v_rwkv_gn_b, v_w_br_mla, v_w_br_rwkv, v_w_out, v_g_post):
    given = dict(locals())
    w = {n: given[n] for n in WEIGHTS}
    m = {n: given['m_' + n] for n in WEIGHTS}
    v = {n: given['v_' + n] for n in WEIGHTS}
    return _step(x, loss_target, w, m, v)
```

```python
import functools
import math

import numpy as np
import jax
import jax.numpy as jnp
from jax import lax
from jax.experimental import pallas as pl
from jax.experimental.pallas import tpu as pltpu

F32, BF16 = jnp.float32, jnp.bfloat16
MESH_IDS = pl.DeviceIdType.MESH

D = 2048
T = 2048
HEADS = 8
Q_RANK = 512
KV_RANK = 512
NOPE = 128
ROPE = 64
VDIM = 128
RW = 1024
RH = 16
RN = 64
LORA = 96
D_IN = 10688
NORM_EPS = 1e-6
GN_EPS = 64e-5
ROPE_THETA = 10000.0
ADAM_LR, ADAM_B1, ADAM_B2, ADAM_EPS, ADAM_WD, ADAM_STEP = 0.001, 0.9, 0.999, 1e-08, 0.01, 10

LANE = 128
VMEM_BIG = 56 * 2**20

NP = 11008
OFF_QA, OFF_KVA, OFF_RKV, OFF_ZM, OFF_ZR, OFF_GM, OFF_GR, OFF_LORA, OFF_KR = 0, 512, 1024, 4096, 5120, 6144, 8192, 10240, 10752
NLERP = 3584

CHUNK = 64
NCH = T // CHUNK


def _dg(a, b, ca, cb, batch=False, prec=None):
    bd = ((0,), (0,)) if batch else ((), ())
    return lax.dot_general(a, b, (((ca,), (cb,)), bd), precision=prec, preferred_element_type=F32)


@jax.custom_vjp
def bdot(a, b):
    return _dg(a.astype(BF16), b.astype(BF16), 1, 0)


def _bdot_fwd(a, b):
    return bdot(a, b), (a, b)


def _bdot_bwd(res, g):
    a, b = res
    gb = g.astype(BF16)
    da = _dg(gb, b.astype(BF16), 1, 1)
    db = _dg(a.astype(BF16), gb, 0, 0)
    return da.astype(a.dtype), db.astype(b.dtype)


bdot.defvjp(_bdot_fwd, _bdot_bwd)


def _split(x):
    hi = x.astype(BF16)
    lo = (x - hi.astype(F32)).astype(BF16)
    return hi, lo


@jax.custom_vjp
def gsum(x, g2):
    hi, lo = _split(x)
    return _dg(hi, g2, 1, 0) + _dg(lo, g2, 1, 0)


def _gsum_fwd(x, g2):
    return gsum(x, g2), g2


def _gsum_bwd(g2, g):
    hi, lo = _split(g)
    return _dg(hi, g2, 1, 1) + _dg(lo, g2, 1, 1), jnp.zeros_like(g2)


gsum.defvjp(_gsum_fwd, _gsum_bwd)


def headsum(x, g2):
    return jnp.concatenate([gsum(x[:, i * LANE:(i + 1) * LANE], g2) for i in range(x.shape[1] // LANE)], axis=1)


def _terms(x, n):
    out = []
    for i in range(n):
        t = x.astype(BF16)
        out.append(t)
        if i < n - 1:
            x = x - t.astype(F32)
    return out


def _bmm(a, b, ca, cb, na, nb):
    acc = None
    for i, ai in enumerate(_terms(a, na)):
        for j, bj in enumerate(_terms(b, nb)):
            if i + j < max(na, nb):
                p = _dg(ai, bj, ca, cb, True)
                acc = p if acc is None else acc + p
    return acc


_NN, _NT, _TN = (2, 1), (2, 2), (1, 1)


def _make_dots(nf, nb_nn, nb_nt, nb_tn):
    @jax.custom_vjp
    def nn(a, b):
        return _bmm(a, b, *_NN, nf, nf)

    @jax.custom_vjp
    def nt(a, b):
        return _bmm(a, b, *_NT, nf, nf)

    @jax.custom_vjp
    def tn(a, b):
        return _bmm(a, b, *_TN, nf, nf)

    nn.defvjp(lambda a, b: (nn(a, b), (a, b)),
              lambda r, g: (_bmm(g, r[1], *_NT, nb_nn, nb_nn), _bmm(r[0], g, *_TN, nb_nn, nb_nn)))
    nt.defvjp(lambda a, b: (nt(a, b), (a, b)),
              lambda r, g: (_bmm(g, r[1], *_NN, 1, nb_nt), _bmm(g, r[0], *_TN, 1, nb_nt)))
    tn.defvjp(lambda a, b: (tn(a, b), (a, b)),
              lambda r, g: (_bmm(r[1], g, *_NT, nb_tn, nb_tn), _bmm(r[0], g, *_NN, nb_tn, nb_tn)))
    return nn, nt, tn


_SCAN_NF, _SCAN_NB = 1, 1
nn, nt, tn = _make_dots(_SCAN_NF, 1, 2, 1)


@jax.custom_vjp
def cumdot(ones, x):
    return _bmm(ones, x, *_NN, 1, 3)


cumdot.defvjp(lambda o, x: (cumdot(o, x), o), lambda o, g: (jnp.zeros_like(o), _bmm(o, g, *_TN, 1, 3)))


def _solve_powers(l):
    pw = [l]
    for _ in range(int(math.log2(l.shape[-1])) - 1):
        pw.append(_bmm(pw[-1], pw[-1], *_NN, _SCAN_NF, _SCAN_NF))
    return pw


@jax.custom_vjp
def tri_solve(l, rhs):
    x = rhs
    for p in _solve_powers(l):
        x = x + _bmm(p, x, *_NN, _SCAN_NF, _SCAN_NF)
    return x


def _tri_solve_fwd(l, rhs):
    pw = _solve_powers(l)
    x = rhs
    for p in pw:
        x = x + _bmm(p, x, *_NN, _SCAN_NF, _SCAN_NF)
    return x, (pw, x)


def _tri_solve_bwd(res, g):
    pw, x = res
    y = g
    for p in pw:
        y = y + _bmm(p, y, *_TN, _SCAN_NB, _SCAN_NB)
    return _bmm(y, x, *_NT, _SCAN_NB, _SCAN_NB), y


tri_solve.defvjp(_tri_solve_fwd, _tri_solve_bwd)


def _rms(x, g):
    return x * lax.rsqrt(jnp.mean(x * x, axis=-1, keepdims=True) + NORM_EPS) * g


def _softplus(x):
    pos = x > 0
    return jnp.where(pos, x, 0.0) + jnp.log(1.0 + jnp.exp(-jnp.where(pos, x, -x)))


def _silu(z):
    return z * jax.nn.sigmoid(z)


def _tile(n, cands):
    for c in cands:
        if n % c == 0:
            return c
    raise ValueError(n)


_MM_VMEM_BYTES = 32 * 2**20


def _mm_tiles(m, n, k):
    best = None
    for tm in (2048, 1024, 512, 256):
        for tn_ in (2048, 1024, 512, 256):
            for d in range(k // LANE, 0, -1):
                tk = LANE * d
                if m % tm or n % tn_ or k % tk:
                    continue
                nk = k // tk
                vmem = 4 * tk * (tm + tn_) + 8 * tm * tn_ + (4 * tm * tn_ if nk > 1 else 0)
                if vmem > _MM_VMEM_BYTES:
                    continue
                a_reads = n // tn_ if nk > 1 else 1
                b_reads = 1 if (nk == 1 and n == tn_) else m // tm
                acc_rmw = nk * m * n if nk > 1 else 0
                cost = (a_reads * m * k + b_reads * k * n + acc_rmw, -tm * tn_ * tk)
                if best is None or cost < best[0]:
                    best = (cost, (tm, tn_, tk))
    return best[1]


class Side:
    def __init__(self, ins, outs, sems, start, finish):
        self.ins, self.outs, self.sems, self.start, self.finish = ins, outs, sems, start, finish


def matmul(name, a, b, mode, out_dtype=F32, side=None, after=()):
    if mode == "nn":
        (m, k), n = a.shape, b.shape[1]
    elif mode == "nt":
        (m, k), n = a.shape, b.shape[0]
    else:
        (k, m), n = a.shape, b.shape[1]
    tm, tn_, tk = _mm_tiles(m, n, k)
    nk = k // tk
    if mode == "nn":
        a_spec = pl.BlockSpec((tm, tk), lambda i, j, kk: (i, kk))
        b_spec = pl.BlockSpec((tk, tn_), lambda i, j, kk: (kk, j))
        ca, cb = 1, 0
    elif mode == "nt":
        a_spec = pl.BlockSpec((tm, tk), lambda i, j, kk: (i, kk))
        b_spec = pl.BlockSpec((tn_, tk), lambda i, j, kk: (j, kk))
        ca, cb = 1, 1
    else:
        a_spec = pl.BlockSpec((tk, tm), lambda i, j, kk: (kk, i))
        b_spec = pl.BlockSpec((tk, tn_), lambda i, j, kk: (kk, j))
        ca, cb = 0, 0

    grid = (m // tm, n // tn_, nk)
    n_in = len(side.ins) if side else 0
    n_out = len(side.outs) if side else 0
    n_dep = len(after)

    def body(a_ref, b_ref, *rest):
        rest = rest[n_dep:]
        s_ins, o_ref, s_outs = rest[:n_in], rest[n_in], rest[n_in + 1:n_in + 1 + n_out]
        scratch = rest[n_in + 1 + n_out:]
        acc, s_sems = (scratch[:1], scratch[1:]) if nk > 1 else ((), scratch)
        if side:
            step = (pl.program_id(0) * grid[1] + pl.program_id(1)) * grid[2] + pl.program_id(2)

            @pl.when(step == 0)
            def _():
                side.start(s_ins, s_outs, s_sems)

        part = _dg(a_ref[...].astype(BF16), b_ref[...].astype(BF16), ca, cb)
        if nk == 1:
            o_ref[...] = part.astype(o_ref.dtype)
        else:
            acc_ref, kk = acc[0], pl.program_id(2)

            @pl.when(kk == 0)
            def _():
                acc_ref[...] = part

            @pl.when(kk > 0)
            def _():
                acc_ref[...] += part

            @pl.when(kk == nk - 1)
            def _():
                o_ref[...] = acc_ref[...].astype(o_ref.dtype)

        if side:
            @pl.when(step == grid[0] * grid[1] * grid[2] - 1)
            def _():
                side.finish(s_ins, s_outs, s_sems)

    res = pl.pallas_call(
        body, name=name, grid=grid,
        in_specs=[a_spec, b_spec] + [_ANY] * (n_dep + n_in),
        out_specs=[pl.BlockSpec((tm, tn_), lambda i, j, kk: (i, j))] + [_ANY] * n_out,
        out_shape=[jax.ShapeDtypeStruct((m, n), out_dtype)] + (list(side.outs) if side else []),
        scratch_shapes=([pltpu.VMEM((tm, tn_), F32)] if nk > 1 else []) + (list(side.sems) if side else []),
        compiler_params=pltpu.CompilerParams(
            dimension_semantics=("arbitrary",) * 3 if side else ("parallel", "parallel", "arbitrary"),
            vmem_limit_bytes=VMEM_BIG),
    )(a, b, *after, *(side.ins if side else []))
    return (res[0], res[1:]) if side else res[0]


def _rspec(tr, width, blk):
    return pl.BlockSpec((tr, width), lambda i: (i, blk))


def _full_spec(arr):
    return pl.BlockSpec(arr.shape, lambda i: (0,) * arr.ndim)


class Stage:
    def __init__(self, name, f, outs, tr, diff_rows, diff_params, drow_dtypes):
        self.name, self.f, self.outs, self.tr = name, f, outs, tr
        self.diff_rows, self.diff_params, self.drow_dtypes = diff_rows, diff_params, drow_dtypes

    def fwd(self, rows, params):
        f, nr, npar = self.f, len(rows), len(params)
        stored = [(w, dt) for (w, dt) in self.outs if dt is not None]
        keep = [i for i, (w, dt) in enumerate(self.outs) if dt is not None]

        def body(*refs):
            vals = f(*[r[...].astype(F32) for r in refs[:nr]], *[p[...] for p in refs[nr:nr + npar]])
            for o_ref, i in zip(refs[nr + npar:], keep):
                o_ref[...] = vals[i].astype(o_ref.dtype)

        return pl.pallas_call(
            body, name=self.name + "_fwd", grid=(T // self.tr,),
            in_specs=[_rspec(self.tr, w, b) for (_, w, b) in rows] + [_full_spec(p) for p in params],
            out_specs=[_rspec(self.tr, w, 0) for (w, _) in stored],
            out_shape=[jax.ShapeDtypeStruct((T, w), dt) for (w, dt) in stored],
            compiler_params=pltpu.CompilerParams(dimension_semantics=("arbitrary",), vmem_limit_bytes=VMEM_BIG),
        )(*[r[0] for r in rows], *params)

    def bwd(self, rows, params, cts):
        f, nr, npar = self.f, len(rows), len(params)
        dr_idx, dp_idx = self.diff_rows, self.diff_params
        flat_cts = [c for lst in cts for c in lst]
        nct = len(flat_cts)

        def body(*refs):
            row_refs, par_refs = refs[:nr], refs[nr:nr + npar]
            ct_refs = refs[nr + npar:nr + npar + nct]
            drow_refs = refs[nr + npar + nct:nr + npar + nct + len(dr_idx)]
            dpar_refs = refs[nr + npar + nct + len(dr_idx):]
            row_vals = [r[...].astype(F32) for r in row_refs]
            par_vals = [p[...] for p in par_refs]

            def g(*dv):
                rv, pv = list(row_vals), list(par_vals)
                for j, i in enumerate(dr_idx):
                    rv[i] = dv[j]
                for j, i in enumerate(dp_idx):
                    pv[i] = dv[len(dr_idx) + j]
                return f(*rv, *pv)

            _, vjp = jax.vjp(g, *[row_vals[i] for i in dr_idx], *[par_vals[i] for i in dp_idx])
            ct_vals, pos = [], 0
            for lst in cts:
                acc = ct_refs[pos][...].astype(F32)
                for q in range(1, len(lst)):
                    acc = acc + ct_refs[pos + q][...].astype(F32)
                pos += len(lst)
                ct_vals.append(acc)
            grads = vjp(tuple(ct_vals))
            for j, r in enumerate(drow_refs):
                r[...] = grads[j].astype(r.dtype)

            @pl.when(pl.program_id(0) == 0)
            def _():
                for r in dpar_refs:
                    r[...] = jnp.zeros_like(r)

            for j, r in enumerate(dpar_refs):
                r[...] += grads[len(dr_idx) + j].astype(F32)

        drow_shapes = [jax.ShapeDtypeStruct((T, rows[i][1]), dt) for i, dt in zip(dr_idx, self.drow_dtypes)]
        dpar_shapes = [jax.ShapeDtypeStruct(params[i].shape, F32) for i in dp_idx]
        res = pl.pallas_call(
            body, name=self.name + "_bwd", grid=(T // self.tr,),
            in_specs=[_rspec(self.tr, w, b) for (_, w, b) in rows] + [_full_spec(p) for p in params]
            + [_rspec(self.tr, w, b) for (_, w, b) in flat_cts],
            out_specs=[_rspec(self.tr, rows[i][1], 0) for i in dr_idx] + [_full_spec(params[i]) for i in dp_idx],
            out_shape=drow_shapes + dpar_shapes,
            compiler_params=pltpu.CompilerParams(dimension_semantics=("arbitrary",), vmem_limit_bytes=VMEM_BIG),
        )(*[r[0] for r in rows], *params, *[c[0] for c in flat_cts])
        return res[:len(dr_idx)], res[len(dr_idx):]


def f_pre(x, g):
    return _rms(x, g), x


@jax.custom_vjp
def swap32(t):
    width = t.shape[1]
    lane = lax.broadcasted_iota(jnp.int32, t.shape, 1) % LANE
    return jnp.where(lane < 32, pltpu.roll(t, width - 32, 1), jnp.where(lane < 64, pltpu.roll(t, 32, 1), 0.0))


swap32.defvjp(lambda t: (swap32(t), None), lambda _, g: (swap32(g),))


def f_mla(q_a, kv_a, kr, cq, sq, ck, sk, gq, gkv, wq, wkv):
    q = bdot(_rms(q_a, gq), wq)
    kv = bdot(_rms(kv_a, gkv), wkv)
    t, k = q[:, 1024:], kr[:, :LANE]
    return (q[:, :1024], t * cq + swap32(t) * sq, kv[:, :1024], k * ck + swap32(k) * sk, kv[:, 1024:])


def f_rwkv_pre(lerp, w0f, w0b, a0f, a0b, kkw, kaw, w2f, w2b, a2f, a2b, g2):
    r, k, v = lerp[:, :RW], lerp[:, RW:2 * RW], lerp[:, 2 * RW:3 * RW]
    wdf, wdb, adf, adb = (lerp[:, 3 * RW + i * LANE:3 * RW + (i + 1) * LANE] for i in range(4))

    def logdecay(w0, wd, w2):
        z = w0 + bdot(jnp.tanh(wd), w2)
        return -jnp.exp(-_softplus(-z) - 0.5)

    a_f = jax.nn.sigmoid(a0f + bdot(adf, a2f))
    a_b = jax.nn.sigmoid(a0b + bdot(adb, a2b))
    kk = k * kkw
    kk = kk / jnp.maximum(jnp.sqrt(headsum(kk * kk, g2)), 1e-12)
    return (r, v, logdecay(w0f, wdf, w2f), logdecay(w0b, wdb, w2b),
            k * (1.0 + (a_f - 1.0) * kaw), k * (1.0 + (a_b - 1.0) * kaw), -kk, kk * a_f, kk * a_b)


def f_rwkv_post(yf, yb, r, kf, kb, v, z, gng, gnb, rk, g2):
    y = yf + yb
    mu = headsum(y, g2) * (1.0 / RN)
    d = y - mu
    var = headsum(d * d, g2) * (1.0 / RN)
    yn = d * lax.rsqrt(var + GN_EPS) * gng + gnb
    bonus = headsum(r * (kf + kb) * rk, g2) * v
    return ((yn + bonus) * _silu(z),)


def f_gate(y, z):
    return (y * _silu(z),)


def f_merge(um, ur, gm, gr):
    return (jax.nn.sigmoid(gm) * um + jax.nn.sigmoid(gr) * ur,)


_SHIFT_W = 256


def _lerp_colblock(j):
    return jnp.where(j < 3072 // _SHIFT_W, OFF_RKV // _SHIFT_W + j, OFF_LORA // _SHIFT_W + j - 3072 // _SHIFT_W)


def _nbr_mean(x):
    row = lax.broadcasted_iota(jnp.int32, x.shape, 0)
    up = jnp.where(row == 0, 0.0, pltpu.roll(x, 1, 0))
    dn = jnp.where(row == T - 1, 0.0, pltpu.roll(x, T - 1, 0))
    return 0.5 * (up + dn)


def shift_fwd(proj, mu):
    def body(x_ref, mu_ref, o_ref):
        x = x_ref[...]
        o_ref[...] = x + mu_ref[...] * (_nbr_mean(x) - x)

    return pl.pallas_call(
        body, name="shift_fwd", grid=(NLERP // _SHIFT_W,),
        in_specs=[pl.BlockSpec((T, _SHIFT_W), lambda j: (0, _lerp_colblock(j))),
                  pl.BlockSpec((1, _SHIFT_W), lambda j: (0, j))],
        out_specs=pl.BlockSpec((T, _SHIFT_W), lambda j: (0, j)),
        out_shape=jax.ShapeDtypeStruct((T, NLERP), F32),
        compiler_params=pltpu.CompilerParams(dimension_semantics=("parallel",), vmem_limit_bytes=VMEM_BIG),
    )(proj, mu)


def shift_bwd(proj, mu, g):
    def body(x_ref, mu_ref, g_ref, dx_ref, dmu_ref):
        x, gv = x_ref[...], g_ref[...]
        dmu_ref[...] = jnp.sum(gv * (_nbr_mean(x) - x), axis=0, keepdims=True)
        gm = gv * mu_ref[...]
        dx_ref[...] = (gv - gm + _nbr_mean(gm)).astype(dx_ref.dtype)

    col = pl.BlockSpec((T, _SHIFT_W), lambda j: (0, j))
    vec = pl.BlockSpec((1, _SHIFT_W), lambda j: (0, j))
    return pl.pallas_call(
        body, name="shift_bwd", grid=(NLERP // _SHIFT_W,),
        in_specs=[pl.BlockSpec((T, _SHIFT_W), lambda j: (0, _lerp_colblock(j))), vec, col],
        out_specs=[col, vec],
        out_shape=[jax.ShapeDtypeStruct((T, NLERP), BF16), jax.ShapeDtypeStruct((1, NLERP), F32)],
        compiler_params=pltpu.CompilerParams(dimension_semantics=("parallel",), vmem_limit_bytes=VMEM_BIG),
    )(proj, mu, g)


_TQ_F, _TQ_B = 256, 512
_ATT_SCALE = (NOPE + ROPE) ** -0.5


def _probs(q, k):
    s = _dg(q, k, 1, 1) * _ATT_SCALE
    e = jnp.exp(s - jnp.max(s, axis=-1, keepdims=True))
    return e * (1.0 / jnp.sum(e, axis=-1, keepdims=True))


def _q_blk(tq):
    return pl.BlockSpec((tq, LANE), lambda h, i: (i, h))


_K_BLK = pl.BlockSpec((T, LANE), lambda h, i: (0, h))
_KR_BLK = pl.BlockSpec((T, LANE), lambda h, i: (0, 0))


def _load_qk(qn_ref, qr_ref, kn_ref, kr_ref, kcat_ref):
    @pl.when(pl.program_id(1) == 0)
    def _():
        kcat_ref[:, :LANE] = kn_ref[...]
        kcat_ref[:, LANE:] = kr_ref[...]

    return jnp.concatenate([qn_ref[...], qr_ref[...]], axis=1), kcat_ref[...]


def attn_fwd(qn, qr, kn, kr, v):
    def body(qn_ref, qr_ref, kn_ref, kr_ref, v_ref, o_ref, kcat_ref):
        q, k = _load_qk(qn_ref, qr_ref, kn_ref, kr_ref, kcat_ref)
        o_ref[...] = _dg(_probs(q, k).astype(BF16), v_ref[...], 1, 0)

    return pl.pallas_call(
        body, name="attn_fwd", grid=(HEADS, T // _TQ_F),
        in_specs=[_q_blk(_TQ_F), _q_blk(_TQ_F), _K_BLK, _KR_BLK, _K_BLK], out_specs=_q_blk(_TQ_F),
        out_shape=jax.ShapeDtypeStruct((T, HEADS * VDIM), F32),
        scratch_shapes=[pltpu.VMEM((T, 2 * LANE), BF16)],
        compiler_params=pltpu.CompilerParams(dimension_semantics=("arbitrary", "arbitrary"), vmem_limit_bytes=VMEM_BIG),
    )(qn, qr, kn, kr, v)


def attn_bwd(qn, qr, kn, kr, v, do):
    def body(qn_ref, qr_ref, kn_ref, kr_ref, v_ref, do_ref, dqn_ref, dqr_ref, dkn_ref, dkr_ref, dv_ref, kcat_ref):
        h, i = pl.program_id(0), pl.program_id(1)

        @pl.when(i == 0)
        def _():
            dkn_ref[...] = jnp.zeros_like(dkn_ref)
            dv_ref[...] = jnp.zeros_like(dv_ref)

        @pl.when((i == 0) & (h == 0))
        def _():
            dkr_ref[...] = jnp.zeros_like(dkr_ref)

        q, k = _load_qk(qn_ref, qr_ref, kn_ref, kr_ref, kcat_ref)
        dob = do_ref[...].astype(BF16)
        p = _probs(q, k)
        dv_ref[...] += _dg(p.astype(BF16), dob, 0, 0)
        dp = _dg(dob, v_ref[...], 1, 1)
        ds = (p * (dp - jnp.sum(dp * p, axis=-1, keepdims=True)) * _ATT_SCALE).astype(BF16)
        dq = _dg(ds, k, 1, 0)
        dqn_ref[...] = dq[:, :LANE]
        dqr_ref[...] = dq[:, LANE:]
        dk = _dg(ds, q, 0, 0)
        dkn_ref[...] += dk[:, :LANE]
        dkr_ref[...] += dk[:, LANE:]

    wide = jax.ShapeDtypeStruct((T, HEADS * LANE), F32)
    return pl.pallas_call(
        body, name="attn_bwd", grid=(HEADS, T // _TQ_B),
        in_specs=[_q_blk(_TQ_B), _q_blk(_TQ_B), _K_BLK, _KR_BLK, _K_BLK, _q_blk(_TQ_B)],
        out_specs=[_q_blk(_TQ_B), _q_blk(_TQ_B), _K_BLK, _KR_BLK, _K_BLK],
        out_shape=[wide, wide, wide, jax.ShapeDtypeStruct((T, LANE), F32), wide],
        scratch_shapes=[pltpu.VMEM((T, 2 * LANE), BF16)],
        compiler_params=pltpu.CompilerParams(dimension_semantics=("arbitrary", "arbitrary"), vmem_limit_bytes=VMEM_BIG),
    )(qn, qr, kn, kr, v, do)


def _chunk(r, lw, k, v, a, b, ht, *, reverse):
    hb, c, _ = r.shape
    ti = lax.broadcasted_iota(jnp.int32, (c, c), 0)
    si = lax.broadcasted_iota(jnp.int32, (c, c), 1)
    incl = (si >= ti) if reverse else (si <= ti)
    strict = (si > ti) if reverse else (si < ti)
    ones = jnp.broadcast_to(incl.astype(F32)[None], (hb, c, c))
    cum = cumdot(ones, lw)
    cum_ex = cum - lw
    tot = jnp.sum(lw, axis=1, keepdims=True)
    mid = 0.5 * tot
    rt, at = r * jnp.exp(cum - mid), a * jnp.exp(cum_ex - mid)
    einv = jnp.exp(mid - cum)
    kt, bt = k * einv, b * einv
    m_ab = jnp.where(strict, nt(at, bt), 0.0)
    m_ak = jnp.where(strict, nt(at, kt), 0.0)
    m_rb = jnp.where(incl, nt(rt, bt), 0.0)
    m_rk = jnp.where(incl, nt(rt, kt), 0.0)
    u = tri_solve(m_ab, nt(a * jnp.exp(cum_ex), ht) + nn(m_ak, v))
    y = nt(r * jnp.exp(cum), ht) + nn(m_rb, u) + nn(m_rk, v)
    eend = jnp.exp(tot - cum)
    ht_new = ht * jnp.exp(tot) + tn(u, b * eend) + tn(v, k * eend)
    return y, ht_new


_HB_F, _HB_B = 16, 16


def _split_heads(x):
    return jnp.stack([x[:, i * RN:(i + 1) * RN] for i in range(x.shape[1] // RN)])


def _merge_heads(y):
    return jnp.concatenate([y[i] for i in range(y.shape[0])], axis=1)


def _chunk_map(reverse, backward):
    flip = reverse != backward
    return (lambda g, c: (NCH - 1 - c, g)) if flip else (lambda g, c: (c, g))


def scan_fwd(name, r, lw, k, v, a, b, reverse):
    hb = _HB_F
    cmap = _chunk_map(reverse, False)

    def body(r_ref, lw_ref, k_ref, v_ref, a_ref, b_ref, y_ref, h0_ref, ht_ref):
        @pl.when(pl.program_id(1) == 0)
        def _():
            ht_ref[...] = jnp.zeros_like(ht_ref)

        ht = ht_ref[...]
        h0_ref[0] = ht
        ins = [_split_heads(x[...]) for x in (r_ref, lw_ref, k_ref, v_ref, a_ref, b_ref)]
        y, hn = _chunk(*ins, ht, reverse=reverse)
        y_ref[...] = _merge_heads(y)
        ht_ref[...] = hn

    io = pl.BlockSpec((CHUNK, hb * RN), cmap)
    return pl.pallas_call(
        body, name=name, grid=(RH // hb, NCH),
        in_specs=[io] * 6,
        out_specs=[io, pl.BlockSpec((1, hb, RN, RN), lambda g, c: (cmap(g, c)[0], g, 0, 0))],
        out_shape=[jax.ShapeDtypeStruct((T, RW), F32), jax.ShapeDtypeStruct((NCH, RH, RN, RN), F32)],
        scratch_shapes=[pltpu.VMEM((hb, RN, RN), F32)],
        compiler_params=pltpu.CompilerParams(dimension_semantics=("parallel", "arbitrary"), vmem_limit_bytes=VMEM_BIG),
    )(r, lw, k, v, a, b)


def scan_bwd(name, r, lw, k, v, a, b, h0, dy, reverse):
    hb = _HB_B
    cmap = _chunk_map(reverse, True)

    def body(r_ref, lw_ref, k_ref, v_ref, a_ref, b_ref, h0_ref, dy_ref, *rest):
        d_refs, dht_ref = rest[:6], rest[6]

        @pl.when(pl.program_id(1) == 0)
        def _():
            dht_ref[...] = jnp.zeros_like(dht_ref)

        ins = [_split_heads(x[...]) for x in (r_ref, lw_ref, k_ref, v_ref, a_ref, b_ref)]
        _, vjp = jax.vjp(functools.partial(_chunk, reverse=reverse), *ins, h0_ref[0])
        grads = vjp((_split_heads(dy_ref[...]), dht_ref[...]))
        for d_ref, gval in zip(d_refs, grads[:6]):
            d_ref[...] = _merge_heads(gval).astype(d_ref.dtype)
        dht_ref[...] = grads[6]

    io = pl.BlockSpec((CHUNK, hb * RN), cmap)
    return pl.pallas_call(
        body, name=name, grid=(RH // hb, NCH),
        in_specs=[io] * 6 + [pl.BlockSpec((1, hb, RN, RN), lambda g, c: (cmap(g, c)[0], g, 0, 0)), io],
        out_specs=[io] * 6,
        out_shape=[jax.ShapeDtypeStruct((T, RW), F32 if i == 1 else BF16) for i in range(6)],
        scratch_shapes=[pltpu.VMEM((hb, RN, RN), F32)],
        compiler_params=pltpu.CompilerParams(dimension_semantics=("parallel", "arbitrary"), vmem_limit_bytes=VMEM_BIG),
    )(r, lw, k, v, a, b, h0, dy)


def loss_stage(out, x2, tgt, g_post):
    tr = 256

    def body(o_ref, x_ref, t_ref, g_ref, do_ref, dy_ref, dg_ref, loss_ref):
        @pl.when(pl.program_id(0) == 0)
        def _():
            dg_ref[...] = jnp.zeros_like(dg_ref)
            loss_ref[...] = jnp.zeros_like(loss_ref)

        nrm, vjp = jax.vjp(_rms, o_ref[...], g_ref[...])
        e = x_ref[...] + nrm - t_ref[...]
        s = jnp.sum(jnp.sum(e * e, axis=1, keepdims=True), axis=0, keepdims=True)
        loss_ref[...] += jnp.broadcast_to(s * (0.5 / D), loss_ref.shape)
        dy = e * (1.0 / D)
        do, dg = vjp(dy)
        do_ref[...] = do.astype(do_ref.dtype)
        dy_ref[...] = dy
        dg_ref[...] += dg

    row = pl.BlockSpec((tr, D), lambda i: (i, 0))
    return pl.pallas_call(
        body, name="loss_stage", grid=(T // tr,),
        in_specs=[row, row, row, pl.BlockSpec((1, D), lambda i: (0, 0))],
        out_specs=[row, row, pl.BlockSpec((1, D), lambda i: (0, 0)), pl.BlockSpec((8, LANE), lambda i: (0, 0))],
        out_shape=[jax.ShapeDtypeStruct((T, D), BF16), jax.ShapeDtypeStruct((T, D), F32),
                   jax.ShapeDtypeStruct((1, D), F32), jax.ShapeDtypeStruct((8, LANE), F32)],
        compiler_params=pltpu.CompilerParams(dimension_semantics=("arbitrary",), vmem_limit_bytes=VMEM_BIG),
    )(out, x2, tgt, g_post)


_EW_BLOCK_BYTES = 1 << 20


def _row_tile(rows, cols):
    best = None
    for tr in range(16, rows + 1, 16):
        if rows % tr == 0 and tr * cols * 4 <= _EW_BLOCK_BYTES:
            best = tr
    return best or rows


def _axis_tile(shape, axis, words):
    rows, cols = shape
    n, other, unit = (rows, cols, 16) if axis == 0 else (cols, rows, LANE)
    best = unit if n % unit == 0 else n
    for t in range(unit, n + 1, unit):
        if n % t == 0 and t * other * words * 4 <= _EW_BLOCK_BYTES:
            best = t
    blk = (best, cols) if axis == 0 else (rows, best)
    at = (lambda s: (s, 0)) if axis == 0 else (lambda s: (0, s))
    return blk, n // best, at


def _adamw_update(g, w_ref, m_ref, v_ref, g_ref, d_ref, nm_ref, nv_ref):
    mm = ADAM_B1 * m_ref[...] + (1.0 - ADAM_B1) * g
    vv = ADAM_B2 * v_ref[...] + (1.0 - ADAM_B2) * (g * g)
    m_hat = mm / (1.0 - ADAM_B1 ** ADAM_STEP)
    v_hat = vv / (1.0 - ADAM_B2 ** ADAM_STEP)
    g_ref[...] = g
    d_ref[...] = -ADAM_LR * (m_hat / (jnp.sqrt(v_hat) + ADAM_EPS) + ADAM_WD * w_ref[...])
    nm_ref[...] = mm
    nv_ref[...] = vv


def adamw(name, w, m, v, parts):
    rows, cols = w.shape
    br = _row_tile(rows, cols)
    npart = len(parts)

    def body(w_ref, m_ref, v_ref, *rest):
        g = rest[0][...].astype(F32)
        for p in rest[1:npart]:
            g = g + p[...].astype(F32)
        _adamw_update(g, w_ref, m_ref, v_ref, *rest[npart:])

    blk = pl.BlockSpec((br, cols), lambda i: (i, 0))
    return pl.pallas_call(
        body, name=name, grid=(rows // br,),
        in_specs=[blk] * (3 + npart), out_specs=[blk] * 4,
        out_shape=[jax.ShapeDtypeStruct((rows, cols), F32)] * 4,
        compiler_params=pltpu.CompilerParams(dimension_semantics=("parallel",), vmem_limit_bytes=VMEM_BIG),
    )(w, m, v, *parts)


def adamw_halves(name, place, w, m, v, mine, theirs, axis):
    half_shape = mine.shape
    blk_shape, nb, at = _axis_tile(half_shape, axis, 1)

    def body(p_ref, w_ref, m_ref, v_ref, a_ref, b_ref, *outs):
        own = (pl.program_id(0) // nb) == p_ref[0]
        _adamw_update(jnp.where(own, a_ref[...], b_ref[...]), w_ref, m_ref, v_ref, *outs)

    blk = pl.BlockSpec(blk_shape, lambda i, p: at(i))
    half = pl.BlockSpec(blk_shape, lambda i, p: at(i % nb))
    return pl.pallas_call(
        body, name=name,
        grid_spec=pltpu.PrefetchScalarGridSpec(num_scalar_prefetch=1, grid=(2 * nb,),
                                               in_specs=[blk] * 3 + [half] * 2, out_specs=[blk] * 4),
        out_shape=[jax.ShapeDtypeStruct(w.shape, F32)] * 4,
        compiler_params=pltpu.CompilerParams(dimension_semantics=("arbitrary",), vmem_limit_bytes=VMEM_BIG),
    )(place, w, m, v, mine, theirs)


def pair_sum(name, place, send, other, axis):
    blk_shape, nb, at = _axis_tile(other.shape[1:], axis, 4)

    def body(p_ref, a_ref, b_ref, o_ref):
        o_ref[...] = (a_ref[...].astype(F32) + b_ref[...].astype(F32)).astype(o_ref.dtype)

    blk = pl.BlockSpec((4,) + blk_shape, lambda i, p: (0,) + at(i))
    mine = pl.BlockSpec((4,) + blk_shape, lambda i, p: (0,) + at(p[0] * nb + i))
    return pl.pallas_call(
        body, name=name,
        grid_spec=pltpu.PrefetchScalarGridSpec(num_scalar_prefetch=1, grid=(nb,), in_specs=[mine, blk], out_specs=blk),
        out_shape=jax.ShapeDtypeStruct(other.shape, BF16),
        compiler_params=pltpu.CompilerParams(dimension_semantics=("arbitrary",), vmem_limit_bytes=VMEM_BIG),
    )(place, send, other)


def sum4(name, place, recv, own, axis):
    blk_shape, nb, at = _axis_tile(recv.shape[1:], axis, 4)

    def body(p_ref, r_ref, s_ref, o_ref):
        me = p_ref[0]
        t = [jnp.where(me == j, s_ref[j], r_ref[j]).astype(F32) for j in range(4)]
        o_ref[...] = ((t[0] + t[1]) + t[2]) + t[3]

    blk = pl.BlockSpec((4,) + blk_shape, lambda i, p: (0,) + at(i))
    return pl.pallas_call(
        body, name=name,
        grid_spec=pltpu.PrefetchScalarGridSpec(num_scalar_prefetch=1, grid=(nb,), in_specs=[blk, blk],
                                               out_specs=pl.BlockSpec(blk_shape, lambda i, p: at(i))),
        out_shape=jax.ShapeDtypeStruct(recv.shape[1:], F32),
        compiler_params=pltpu.CompilerParams(dimension_semantics=("arbitrary",), vmem_limit_bytes=VMEM_BIG),
    )(place, recv, own)


_ANY = pl.BlockSpec(memory_space=pl.ANY)


def _place():
    x, y, c = lax.axis_index("x"), lax.axis_index("y"), lax.axis_index("c")
    return x, y, c, 2 * x + y


def _chip_peers(x, y):
    out = []
    for k in (1, 2, 3):
        px = 1 - x if k & 2 else x
        py = 1 - y if k & 1 else y
        out.append((k, px, py, 2 * px + py))
    return out


def _half(c, shape, axis):
    n = shape[axis] // 2
    sl = pl.ds(pl.multiple_of(c * n, 16 if axis == 0 else LANE), n)
    return (sl,) if axis == 0 else (pl.ds(0, shape[0]), sl)


def gather_weights(srcs, axes):
    side = gather_side(srcs, axes)
    n = len(srcs)

    def body(*refs):
        ins, outs, sems = refs[:n], refs[n:2 * n], refs[2 * n:]
        side.start(ins, outs, sems)
        side.finish(ins, outs, sems)

    return pl.pallas_call(
        body, name="gather_weights", in_specs=[_ANY] * n, out_specs=[_ANY] * n,
        out_shape=side.outs, scratch_shapes=side.sems,
    )(*srcs)


def gather_side(srcs, axes):
    n = len(srcs)

    def copies(src, dst, sems, sends_only=False):
        ssem, rsem, fssem, frsem = sems
        x, y, c, me = _place()
        sib = (x, y, 1 - c)
        out = []
        for i in range(n):
            mine, other = _half(c, srcs[i].shape, axes[i]), _half(1 - c, srcs[i].shape, axes[i])
            for k, px, py, peer in _chip_peers(x, y):
                sems_k = dict(send_sem=ssem.at[i, k - 1], recv_sem=rsem.at[i, k - 1], device_id=(px, py, c),
                              device_id_type=MESH_IDS)
                fsems = dict(send_sem=fssem.at[i, k - 1], recv_sem=frsem.at[i, k - 1], device_id=sib,
                             device_id_type=MESH_IDS)
                got = dst[i].at[(peer,) + mine]
                snd = pltpu.make_async_remote_copy(src_ref=src[i].at[mine], dst_ref=dst[i].at[(me,) + mine], **sems_k)
                if sends_only:
                    out.append(snd)
                    continue
                out.append((
                    snd,
                    pltpu.make_async_remote_copy(src_ref=src[i].at[mine], dst_ref=got, **sems_k),
                    pltpu.make_async_remote_copy(src_ref=got, dst_ref=got, **fsems),
                    pltpu.make_async_remote_copy(src_ref=got, dst_ref=dst[i].at[(peer,) + other], **fsems)))
        return out

    def start(src, dst, sems):
        for snd in copies(src, dst, sems, sends_only=True):
            snd.start()

    def finish(src, dst, sems):
        cps = copies(src, dst, sems)
        for _, rcv, fwd, _ in cps:
            rcv.wait_recv()
            fwd.start()
        for snd, _, fwd, frcv in cps:
            frcv.wait_recv()
            snd.wait_send()
            fwd.wait_send()

    return Side(list(srcs), [jax.ShapeDtypeStruct((4,) + s.shape, s.dtype) for s in srcs],
                [pltpu.SemaphoreType.DMA((n, 3))] * 4, start, finish)


def pair_exchange(name, srcs, axes):
    n = len(srcs)

    def half_shape(s, axis):
        return (4, s.shape[1] // 2, s.shape[2]) if axis == 0 else (4, s.shape[1], s.shape[2] // 2)

    def body(*refs):
        src, other = refs[:n], refs[n:2 * n]
        ssem, rsem = refs[2 * n:]
        x, y, c, _ = _place()
        cps = []
        for i in range(n):
            idx = (pl.ds(0, 4),) + _half(1 - c, srcs[i].shape[1:], axes[i])
            cps.append(pltpu.make_async_remote_copy(
                src_ref=src[i].at[idx], dst_ref=other[i], send_sem=ssem.at[i], recv_sem=rsem.at[i],
                device_id=(x, y, 1 - c), device_id_type=MESH_IDS))
            cps[-1].start()
        for cp in cps:
            cp.wait()

    return pl.pallas_call(
        body, name=name, in_specs=[_ANY] * n, out_specs=[_ANY] * n,
        out_shape=[jax.ShapeDtypeStruct(half_shape(s, a), s.dtype) for s, a in zip(srcs, axes)],
        scratch_shapes=[pltpu.SemaphoreType.DMA((n,))] * 2,
    )(*srcs)


def scatter_grads(srcs):
    side = scatter_side(srcs)
    n = len(srcs)

    def body(*refs):
        ins, outs, sems = refs[:n], refs[n:2 * n], refs[2 * n:]
        side.start(ins, outs, sems)
        side.finish(ins, outs, sems)

    return pl.pallas_call(
        body, name="scatter_grads", in_specs=[_ANY] * n, out_specs=[_ANY] * n,
        out_shape=side.outs, scratch_shapes=side.sems,
    )(*srcs)


def scatter_side(srcs):
    n = len(srcs)

    def copies(src, dst, sems, sends_only=False):
        ssem, rsem = sems
        x, y, c, me = _place()
        out = []
        for i in range(n):
            for k, px, py, peer in _chip_peers(x, y):
                sems_k = dict(send_sem=ssem.at[i, k - 1], recv_sem=rsem.at[i, k - 1], device_id=(px, py, c),
                              device_id_type=MESH_IDS)
                snd = pltpu.make_async_remote_copy(src_ref=src[i].at[peer], dst_ref=dst[i].at[me], **sems_k)
                if sends_only:
                    out.append(snd)
                    continue
                out.append((snd, pltpu.make_async_remote_copy(src_ref=src[i].at[peer], dst_ref=dst[i].at[peer],
                                                              **sems_k)))
        return out

    def start(src, dst, sems):
        for snd in copies(src, dst, sems, sends_only=True):
            snd.start()

    def finish(src, dst, sems):
        for snd, rcv in copies(src, dst, sems):
            rcv.wait_recv()
            snd.wait_send()

    return Side(list(srcs), [jax.ShapeDtypeStruct(s.shape, s.dtype) for s in srcs],
                [pltpu.SemaphoreType.DMA((n, 3))] * 2, start, finish)


_HBM = pl.BlockSpec(memory_space=pltpu.HBM)
_SEM = pl.BlockSpec(memory_space=pltpu.SEMAPHORE)
_DATAFLOW = pltpu.SideEffectType.DATAFLOW_SIDE_EFFECTING


def scatter_start(name, srcs):
    n = len(srcs)
    side = scatter_side(srcs)
    ns = 3 * n

    def body(*refs):
        src, land = refs[:n], refs[n:2 * n]
        sems = refs[2 * n:2 * n + 2 * ns]
        side.start(src, land, (_SemGrid(sems[:ns]), _SemGrid(sems[ns:])))
        refs[-1][...] = jnp.zeros_like(refs[-1])

    hbm = [pltpu.HBM(s.shape, s.dtype) for s in srcs]
    res = pl.pallas_call(
        body, name=name,
        out_shape=[pltpu.SemaphoreType.DMA(())] * (2 * ns) + hbm + hbm + [jax.ShapeDtypeStruct((8, LANE), F32)],
        in_specs=[_HBM] * (2 * n),
        out_specs=[_SEM] * (2 * ns) + [_HBM] * (2 * n) + [pl.BlockSpec(memory_space=pltpu.VMEM)],
        input_output_aliases={i: 2 * ns + i for i in range(2 * n)},
        compiler_params=pltpu.CompilerParams(has_side_effects=_DATAFLOW),
    )(*[pltpu.with_memory_space_constraint(s, pltpu.HBM) for s in srcs],
      *[pltpu.with_memory_space_constraint(lax.empty(s.shape, s.dtype), pltpu.HBM) for s in srcs])
    return res[:2 * ns], res[2 * ns:2 * ns + n], res[2 * ns + n:2 * ns + 2 * n], res[-1]


def scatter_wait(name, sems, srcs, lands, after):
    n = len(srcs)
    side = scatter_side(srcs)
    ns = 3 * n

    def body(*refs):
        src, land = refs[:n], refs[n:2 * n]
        s = refs[2 * n:2 * n + 2 * ns]
        side.finish(src, land, (_SemGrid(s[:ns]), _SemGrid(s[ns:])))

    hbm = [pltpu.HBM(s.shape, s.dtype) for s in srcs]
    res = pl.pallas_call(
        body, name=name, out_shape=hbm + hbm,
        in_specs=[_HBM] * (2 * n) + [_SEM] * (2 * ns) + [_ANY], out_specs=[_HBM] * (2 * n),
        input_output_aliases={i: i for i in range(2 * n)},
        compiler_params=pltpu.CompilerParams(has_side_effects=_DATAFLOW),
    )(*srcs, *lands, *sems, after)
    return res[:n], res[n:]


class _SemGrid:
    def __init__(self, sems):
        self.sems = sems

    @property
    def at(self):
        return self

    def __getitem__(self, ik):
        return self.sems[3 * ik[0] + ik[1]]


def swap_halves(name, srcs):
    n = len(srcs)

    def body(*refs):
        src, dst = refs[:n], refs[n:2 * n]
        ssem, rsem = refs[2 * n:]
        x, y, c, _ = _place()
        cps = []
        for i in range(n):
            cps.append(pltpu.make_async_remote_copy(src_ref=src[i], dst_ref=dst[i], send_sem=ssem.at[i],
                                                    recv_sem=rsem.at[i], device_id=(x, y, 1 - c),
                                                    device_id_type=MESH_IDS))
            cps[-1].start()
        for cp in cps:
            cp.wait()

    return pl.pallas_call(
        body, name=name, in_specs=[_ANY] * n, out_specs=[_ANY] * n,
        out_shape=[jax.ShapeDtypeStruct(s.shape, s.dtype) for s in srcs],
        scratch_shapes=[pltpu.SemaphoreType.DMA((n,))] * 2,
    )(*srcs)


def _ag8_copies(src, dst, sems, sends_only=False):
    x, y, c = lax.axis_index("x"), lax.axis_index("y"), lax.axis_index("c")
    me = 4 * x + 2 * y + c
    out = []
    for k in range(1, 8):
        px = 1 - x if k & 4 else x
        py = 1 - y if k & 2 else y
        pc = 1 - c if k & 1 else c
        peer = 4 * px + 2 * py + pc
        out.append(tuple(pltpu.make_async_remote_copy(
            src_ref=src, dst_ref=dst.at[slot], send_sem=sems[k - 1], recv_sem=sems[7 + k - 1],
            device_id=(px, py, pc), device_id_type=MESH_IDS) for slot in ((me,) if sends_only else (me, peer))))
    return out


def allgather8_start(name, src):
    def body(src_ref, land_ref, *rest):
        for (snd,) in _ag8_copies(src_ref, land_ref, rest[:14], sends_only=True):
            snd.start()

    land = jax.ShapeDtypeStruct((8,) + src.shape, src.dtype)
    res = pl.pallas_call(
        body, name=name,
        out_shape=[pltpu.SemaphoreType.DMA(())] * 14 + [pltpu.HBM(src.shape, src.dtype), pltpu.HBM(land.shape, land.dtype)],
        in_specs=[_HBM, _HBM], out_specs=[_SEM] * 14 + [_HBM, _HBM],
        input_output_aliases={0: 14, 1: 15},
        compiler_params=pltpu.CompilerParams(has_side_effects=_DATAFLOW),
    )(pltpu.with_memory_space_constraint(src, pltpu.HBM),
      pltpu.with_memory_space_constraint(lax.empty(land.shape, land.dtype), pltpu.HBM))
    return res[:14], res[14], res[15]


def allgather8_wait(name, sems, src, land, after):
    def body(src_ref, land_ref, *rest):
        for snd, rcv in _ag8_copies(src_ref, land_ref, rest[:14]):
            rcv.wait_recv()
            snd.wait_send()

    return pl.pallas_call(
        body, name=name, out_shape=[pltpu.HBM(src.shape, src.dtype), pltpu.HBM(land.shape, land.dtype)],
        in_specs=[_HBM, _HBM] + [_SEM] * 14 + [_ANY], out_specs=[_HBM, _HBM],
        input_output_aliases={0: 0, 1: 1},
        compiler_params=pltpu.CompilerParams(has_side_effects=_DATAFLOW),
    )(src, land, *sems, after)


WEIGHTS = ['g_pre', 'w_in', 'mla_q_norm', 'mla_wq_b', 'mla_kv_norm', 'mla_wkv_b', 'rwkv_mu', 'rwkv_w0_f', 'rwkv_w2_f',
           'rwkv_w0_b', 'rwkv_w2_b', 'rwkv_a0_f', 'rwkv_a2_f', 'rwkv_a0_b', 'rwkv_a2_b', 'rwkv_k_k', 'rwkv_k_a',
           'rwkv_r_k', 'rwkv_gn_g', 'rwkv_gn_b', 'w_br_mla', 'w_br_rwkv', 'w_out', 'g_post']
BIG_SHAPES = {'w_in': (D_IN // 4, D), 'mla_wq_b': (Q_RANK, 384), 'mla_wkv_b': (KV_RANK, 512),
              'rwkv_w2_f': (LORA, 256), 'rwkv_w2_b': (LORA, 256), 'rwkv_a2_f': (LORA, 256), 'rwkv_a2_b': (LORA, 256),
              'w_br_mla': (RW, 512), 'w_br_rwkv': (RW, 512), 'w_out': (512, D)}
BIG = list(BIG_SHAPES)
SMALL = [n for n in WEIGHTS if n not in BIG_SHAPES]
SMALL_SHAPES = {'g_pre': (D,), 'mla_q_norm': (Q_RANK,), 'mla_kv_norm': (KV_RANK,), 'rwkv_mu': (3456,),
                'rwkv_w0_f': (RW,), 'rwkv_w0_b': (RW,), 'rwkv_a0_f': (RW,), 'rwkv_a0_b': (RW,), 'rwkv_k_k': (RW,),
                'rwkv_k_a': (RW,), 'rwkv_r_k': (RH, RN), 'rwkv_gn_g': (RW,), 'rwkv_gn_b': (RW,), 'g_post': (D,)}
SMALL_LEN = sum(int(np.prod(s)) for s in SMALL_SHAPES.values())
SMALL_ROWS = 144


UNITS = [('w_in',), ('mla_wq_b',), ('mla_wkv_b',), ('rwkv_w2_f', 'rwkv_w2_b', 'rwkv_a2_f', 'rwkv_a2_b'),
         ('w_br_mla',), ('w_br_rwkv',), ('w_out',)]
UNIT_AXIS = [1, 0, 0, 0, 0, 0, 0]
ROW_SHARDED = ('w_in', 'w_out')


def _unit_cat(parts):
    return parts[0] if len(parts) == 1 else jnp.concatenate(parts, axis=0)


def _unit_split(arr, names, axis):
    out, o = {}, 0
    for n in names:
        rows = BIG_SHAPES[n][0]
        out[n] = lax.slice_in_dim(arr, o, o + rows, axis=axis)
        o += rows
    return out


def _gathered(units, ag, own, me):
    out = {}
    for names, arr, mine in zip(units, ag, own):
        slots = [jnp.where(me == j, mine, arr[j]) for j in range(4)]
        for n in names:
            parts = [_unit_split(s, names, 0)[n] for s in slots]
            out[n] = jnp.concatenate(parts, axis=0 if n in ROW_SHARDED else 1)
    return out


def _shards(n, g):
    r, w = BIG_SHAPES[n]
    if n in ROW_SHARDED:
        return [g[j * r:(j + 1) * r] for j in range(4)]
    return [g[:, j * w:(j + 1) * w] for j in range(4)]


def _pack_small(d, extra=None):
    flat = jnp.concatenate([d[n].reshape(-1) for n in SMALL] + ([extra.reshape(-1)] if extra is not None else []))
    return jnp.pad(flat, (0, SMALL_ROWS * LANE - flat.shape[0])).reshape(SMALL_ROWS, LANE)


def _unpack_small(packed):
    flat, out, o = packed.reshape(-1), {}, 0
    for n in SMALL:
        sz = int(np.prod(SMALL_SHAPES[n]))
        out[n] = flat[o:o + sz].reshape(SMALL_SHAPES[n])
        o += sz
    return out


def _perm_w_in(gathered, own, me):
    per = D_IN // 4

    def rows(a, b):
        out = []
        while a < b:
            j, lo = divmod(a, per)
            hi = min(b - j * per, per)
            out.append(jnp.where(me == j, own[lo:hi], gathered[j, lo:hi]))
            a = j * per + hi
        return out

    z = lambda n: [jnp.zeros((n, own.shape[1]), own.dtype)]
    lora = []
    for i in range(4):
        lora += rows(4160 + LORA * i, 4160 + LORA * (i + 1)) + z(LANE - LORA)
    return jnp.concatenate(rows(0, 1024) + rows(1088, 4160) + rows(4544, D_IN) + lora + rows(1024, 1088)
                           + z(256 - ROPE), axis=0)


def _unperm_w_in(g):
    lora = [g[OFF_LORA + LANE * i:OFF_LORA + LANE * i + LORA] for i in range(4)]
    return jnp.concatenate([g[0:1024], g[OFF_KR:OFF_KR + ROPE], g[1024:4096]] + lora + [g[4096:OFF_LORA]], axis=0)


def _perm_wq(w):
    w3 = w.reshape(Q_RANK, HEADS, NOPE + ROPE)
    rope = jnp.pad(w3[:, :, NOPE:], ((0, 0), (0, 0), (0, LANE - ROPE)))
    return jnp.concatenate([w3[:, :, :NOPE].reshape(Q_RANK, -1), rope.reshape(Q_RANK, -1)], axis=1)


def _unperm_wq(g):
    return jnp.concatenate([g[:, :1024].reshape(Q_RANK, HEADS, NOPE),
                            g[:, 1024:].reshape(Q_RANK, HEADS, LANE)[:, :, :ROPE]], axis=2).reshape(Q_RANK, -1)


def _perm_wkv(w):
    w3 = w.reshape(KV_RANK, HEADS, NOPE + VDIM)
    return jnp.concatenate([w3[:, :, :NOPE].reshape(KV_RANK, -1), w3[:, :, NOPE:].reshape(KV_RANK, -1)], axis=1)


def _unperm_wkv(g):
    return jnp.concatenate([g[:, :1024].reshape(KV_RANK, HEADS, NOPE), g[:, 1024:].reshape(KV_RANK, HEADS, VDIM)],
                           axis=2).reshape(KV_RANK, -1)


def _pad_rows(w):
    return jnp.pad(w, ((0, LANE - LORA), (0, 0)))


def _perm_mu(mu):
    parts = [mu[:3072]]
    for i in range(4):
        parts += [mu[3072 + LORA * i:3072 + LORA * (i + 1)], jnp.zeros((LANE - LORA,), mu.dtype)]
    return jnp.concatenate(parts).reshape(1, NLERP)


def _unperm_mu(g):
    g = g.reshape(-1)
    return jnp.concatenate([g[:3072]] + [g[3072 + LANE * i:3072 + LANE * i + LORA] for i in range(4)])


def _constants():
    g2 = np.kron(np.eye(2, dtype=np.float32), np.ones((RN, RN), np.float32))
    pos = jnp.arange(T, dtype=F32)
    inv_freq = jnp.power(ROPE_THETA, -jnp.arange(0, ROPE, 2, dtype=F32) / ROPE)
    ang = pos[:, None] * inv_freq[None, :]
    cos, sin, zero = jnp.cos(ang), jnp.sin(ang), jnp.zeros((T, LANE - ROPE), F32)
    cq = jnp.tile(jnp.concatenate([cos, cos, zero], axis=1), (1, HEADS))
    sq = jnp.tile(jnp.concatenate([-sin, sin, zero], axis=1), (1, HEADS))
    return jnp.asarray(g2, BF16), cq, sq


def _step(x, tgt, w, m, v):
    x2, tgt2 = x.reshape(T, D), tgt.reshape(T, D)
    g2, cq, sq = _constants()
    row = lambda n: w[n].reshape(1, -1)
    w, m, v = ({**t, 'w_in': t['w_in'].T} for t in (w, m, v))

    core, chip = lax.axis_index("c"), 2 * lax.axis_index("x") + lax.axis_index("y")
    core1, chip1 = core.astype(jnp.int32).reshape(1), chip.astype(jnp.int32).reshape(1)
    own_bf = [_unit_cat([w[n].astype(BF16) for n in u]) for u in UNITS]
    wp = _perm_w_in(gather_weights(own_bf[:1], UNIT_AXIS[:1])[0], own_bf[0], chip)
    full = {}
    mu_p = _perm_mu(w['rwkv_mu'])

    st_pre = Stage("pre", f_pre, [(D, BF16), (D, None)], 256, [0], [0], [F32])
    st_mla = Stage("mla", f_mla, [(1024, BF16), (1024, BF16), (1024, BF16), (LANE, BF16), (1024, BF16)], 256,
                   [0, 1, 2], [0, 1, 2, 3], [BF16] * 3)
    st_rpre = Stage("rwkv_pre", f_rwkv_pre, [(RW, F32)] * 9, 128, [0], list(range(10)), [F32])
    st_rpost = Stage("rwkv_post", f_rwkv_post, [(RW, BF16)], 256, [0, 2, 3, 4, 5, 6], [0, 1, 2],
                     [F32, F32, F32, F32, F32, BF16])
    st_gate = Stage("gate", f_gate, [(RW, BF16)], 256, [0, 1], [], [F32, BF16])
    st_merge = Stage("merge", f_merge, [(D, BF16)], 256, [0, 1, 2, 3], [], [BF16] * 4)

    pre_rows, pre_par = [(x2, D, 0)], [row('g_pre')]
    (h,) = st_pre.fwd(pre_rows, pre_par)
    proj, rest = matmul("mm_in", h, wp, "nt", side=gather_side(own_bf[1:], UNIT_AXIS[1:]))
    full.update(_gathered(UNITS[1:], rest, own_bf[1:], chip))
    wq, wkv = _perm_wq(full['mla_wq_b']), _perm_wkv(full['mla_wkv_b'])
    lora_w = [_pad_rows(full[n]).astype(F32) for n in ('rwkv_w2_f', 'rwkv_w2_b', 'rwkv_a2_f', 'rwkv_a2_b')]

    mla_rows = [(proj, 512, OFF_QA // 512), (proj, 512, OFF_KVA // 512), (proj, 256, OFF_KR // 256),
                (cq, 1024, 0), (sq, 1024, 0), (cq, LANE, 0), (sq, LANE, 0)]
    mla_par = [row('mla_q_norm'), row('mla_kv_norm'), wq, wkv]
    att = st_mla.fwd(mla_rows, mla_par)
    y_mla = attn_fwd(*att)

    lerp = shift_fwd(proj, mu_p)
    rpre_rows = [(lerp, NLERP, 0)]
    rpre_par = [row('rwkv_w0_f'), row('rwkv_w0_b'), row('rwkv_a0_f'), row('rwkv_a0_b'), row('rwkv_k_k'),
                row('rwkv_k_a')] + lora_w + [g2]
    r_, v_, lwf, lwb, kf, kb, an, bf_, bb_ = st_rpre.fwd(rpre_rows, rpre_par)
    fin = [r_, lwf, kf, v_, an, bf_]
    bin_ = [r_, lwb, kb, v_, an, bb_]
    yf, h0f = scan_fwd("scan_f", *fin, reverse=False)
    yb, h0b = scan_fwd("scan_b", *bin_, reverse=True)
    rpost_rows = [(yf, RW, 0), (yb, RW, 0), (r_, RW, 0), (kf, RW, 0), (kb, RW, 0),
                  (v_, RW, 0), (proj, RW, OFF_ZR // RW)]
    rpost_par = [row('rwkv_gn_g'), row('rwkv_gn_b'), row('rwkv_r_k'), g2]
    (gr,) = st_rpost.fwd(rpost_rows, rpost_par)
    gate_rows = [(y_mla, RW, 0), (proj, RW, OFF_ZM // RW)]
    (gm,) = st_gate.fwd(gate_rows, [])
    um = matmul("mm_br_mla", gm, full['w_br_mla'], "nn")
    ur = matmul("mm_br_rwkv", gr, full['w_br_rwkv'], "nn")
    merge_rows = [(um, D, 0), (ur, D, 0), (proj, D, OFF_GM // D), (proj, D, OFF_GR // D)]
    (merged,) = st_merge.fwd(merge_rows, [])
    out = matmul("mm_out", merged, full['w_out'], "nn")
    d_out, dy, dg_post, loss_blk = loss_stage(out, x2, tgt2, row('g_post'))

    gw = {'g_post': dg_post}
    d_merged = matmul("mm_out_dx", d_out, full['w_out'], "nt")
    gw['w_out'] = matmul("mm_out_dw", merged, d_out, "tn")
    (d_um, d_ur, d_gm, d_gr), _ = st_merge.bwd(merge_rows, [], [[(d_merged, D, 0)]])
    d_gmla = matmul("mm_br_mla_dx", d_um, full['w_br_mla'], "nt")
    gw['w_br_mla'] = matmul("mm_br_mla_dw", gm, d_um, "tn")
    d_grw = matmul("mm_br_rwkv_dx", d_ur, full['w_br_rwkv'], "nt")
    gw['w_br_rwkv'] = matmul("mm_br_rwkv_dw", gr, d_ur, "tn")
    (d_ymla, d_zm), _ = st_gate.bwd(gate_rows, [], [[(d_gmla, RW, 0)]])
    (d_y, d_r3, d_kf2, d_kb2, d_v3, d_zr), (gw['rwkv_gn_g'], gw['rwkv_gn_b'], d_rk) = st_rpost.bwd(
        rpost_rows, rpost_par, [[(d_grw, RW, 0)]])
    gw['rwkv_r_k'] = d_rk
    sf = scan_bwd("scan_f_bwd", *fin, h0f, d_y, reverse=False)
    sb = scan_bwd("scan_b_bwd", *bin_, h0b, d_y, reverse=True)
    c = lambda *ts: [(t, RW, 0) for t in ts]
    rpre_cts = [c(sf[0], sb[0], d_r3), c(sf[3], sb[3], d_v3), c(sf[1]), c(sb[1]), c(sf[2], d_kf2), c(sb[2], d_kb2),
                c(sf[4], sb[4]), c(sf[5]), c(sb[5])]
    (d_rin,), rpre_g = st_rpre.bwd(rpre_rows, rpre_par, rpre_cts)
    for n, gval in zip(('rwkv_w0_f', 'rwkv_w0_b', 'rwkv_a0_f', 'rwkv_a0_b', 'rwkv_k_k', 'rwkv_k_a'), rpre_g[:6]):
        gw[n] = gval
    for n, gval in zip(('rwkv_w2_f', 'rwkv_w2_b', 'rwkv_a2_f', 'rwkv_a2_b'), rpre_g[6:]):
        gw[n] = gval[:LORA]
    d_lerp, d_mu = shift_bwd(proj, mu_p, d_rin)
    gw['rwkv_mu'] = _unperm_mu(d_mu)

    mla_cts = [[(t, t.shape[1], 0)] for t in attn_bwd(*att, d_ymla)]
    (d_qa, d_kva, d_kr), (gw['mla_q_norm'], gw['mla_kv_norm'], d_wq, d_wkv) = st_mla.bwd(mla_rows, mla_par, mla_cts)
    gw['mla_wq_b'], gw['mla_wkv_b'] = _unperm_wq(d_wq), _unperm_wkv(d_wkv)

    dproj = jnp.concatenate([d_qa, d_kva, d_lerp[:, :3072], d_zm, d_zr, d_gm, d_gr, d_lerp[:, 3072:], d_kr], axis=1)

    def pair_sums(name, ids):
        send = [jnp.stack([_unit_cat([_shards(n, gw[n])[j].astype(BF16) for n in UNITS[i]]) for j in range(4)])
                for i in ids]
        axes = [UNIT_AXIS[i] for i in ids]
        other = pair_exchange(name, send, axes)
        return [pair_sum(f"pair_sum_{i}", core1, s, o, ax) for i, s, o, ax in zip(ids, send, other, axes)]

    late, early = [0], list(range(1, len(UNITS)))
    pairs_e = pair_sums("pair_exchange_rest", early)
    gw_in, recv_e = matmul("mm_in_dw", dproj, h, "tn", BF16, side=scatter_side(pairs_e))
    gw['w_in'] = _unperm_w_in(gw_in)
    pairs_l = pair_sums("pair_exchange_w_in", late)
    sems, src_fly, land_fly, token = scatter_start("scatter_w_in_start", pairs_l)
    dh = matmul("mm_in_dx", dproj, wp, "nn", after=(token,))
    (grad_x,), (gw['g_pre'],) = st_pre.bwd(pre_rows, pre_par, [[(dh, D, 0)], [(dy, D, 0)]])

    big = [dict() for _ in range(4)]

    def update(name, ids, recv, pairs):
        mine = [sum4(f"sum4_{i}", chip1, r, p, UNIT_AXIS[i]) for i, r, p in zip(ids, recv, pairs)]
        theirs = swap_halves(name, mine)
        for i, mi, th in zip(ids, mine, theirs):
            res = adamw_halves(f"adamw_{i}", core1, *[_unit_cat([t[n] for n in UNITS[i]]) for t in (w, m, v)], mi, th,
                               UNIT_AXIS[i])
            for q in range(4):
                big[q].update(_unit_split(res[q], UNITS[i], 0))
        return res

    small_fly = allgather8_start("gather_small_start", _pack_small(gw, loss_blk[0, :1]))
    last = update("swap_halves_rest", early, recv_e, pairs_e)
    own_small, landed = allgather8_wait("gather_small_wait", *small_fly, last[0])
    dev = 2 * chip + core
    parts = [jnp.where(dev == i, own_small, landed[i]) for i in range(8)]
    small = adamw("adamw_small", _pack_small(w), _pack_small(m), _pack_small(v), parts)
    pairs_l, recv_l = scatter_wait("scatter_w_in_wait", sems, src_fly, land_fly, small[0] + last[0][:1, :1])
    update("swap_halves_w_in", late, recv_l, pairs_l)

    outs = []
    for b_d, s_arr in zip(big, small):
        d = {**b_d, **_unpack_small(s_arr)}
        d['w_in'] = d['w_in'].T
        outs.append([d[n] for n in WEIGHTS])
    loss = small[0][SMALL_LEN // LANE, 0]
    return (loss, grad_x.reshape(1, T, D), *outs[0], *outs[1], *outs[2], *outs[3])


def kernel(x, g_pre, w_in, mla_q_norm, mla_wq_b, mla_kv_norm, mla_wkv_b, rwkv_mu, rwkv_w0_f, rwkv_w2_f, rwkv_w0_b, rwkv_w2_b, rwkv_a0_f, rwkv_a2_f, rwkv_a0_b, rwkv_a2_b, rwkv_k_k, rwkv_k_a, rwkv_r_k, rwkv_gn_g, rwkv_gn_b, w_br_mla, w_br_rwkv, w_out, g_post, loss_target, m_g_pre, m_w_in, m_mla_q_norm, m_mla_wq_b, m_mla_kv_norm, m_mla_wkv_b, m_rwkv_mu, m_rwkv_w0_f, m_rwkv_w2_f, m_rwkv_w0_b, m_rwkv_w2_b, m_rwkv_a0_f, m_rwkv_a2_f, m_rwkv_a0_b, m_rwkv_a2_b, m_rwkv_k_k, m_rwkv_k_a, m_rwkv_r_k, m_rwkv_gn_g, m_rwkv_gn_b, m_w_br_mla, m_w_br_rwkv, m_w_out, m_g_post, v_g_pre, v_w_in, v_mla_q_norm, v_mla_wq_b, v_mla_kv_norm, v_mla_wkv_b, v_rwkv_mu, v_rwkv_w0_f, v_rwkv_w2_f, v_rwkv_w0_b, v_rwkv_w2_b, v_rwkv_a0_f, v_rwkv_a2_f, v_rwkv_a0_b, v_rwkv_a2_b, v_rwkv_k_k, v_rwkv_k_a, v_rwkv_r_k, v_rwkv_gn_g, v_rwkv_gn_b, v_w_br_mla, v_w_br_rwkv, v_w_out, v_g_post):
    given = dict(locals())
    w = {n: given[n] for n in WEIGHTS}
    m = {n: given['m_' + n] for n in WEIGHTS}
    v = {n: given['v_' + n] for n in WEIGHTS}
    return _step(x, loss_target, w, m, v)
```

```python
import functools
import math

import numpy as np
import jax
import jax.numpy as jnp
from jax import lax
from jax.experimental import pallas as pl
from jax.experimental.pallas import tpu as pltpu

F32, BF16 = jnp.float32, jnp.bfloat16
MESH_IDS = pl.DeviceIdType.MESH

D = 2048
T = 2048
HEADS = 8
Q_RANK = 512
KV_RANK = 512
NOPE = 128
ROPE = 64
VDIM = 128
RW = 1024
RH = 16
RN = 64
LORA = 96
D_IN = 10688
NORM_EPS = 1e-6
GN_EPS = 64e-5
ROPE_THETA = 10000.0
ADAM_LR, ADAM_B1, ADAM_B2, ADAM_EPS, ADAM_WD, ADAM_STEP = 0.001, 0.9, 0.999, 1e-08, 0.01, 10

LANE = 128
VMEM_BIG = 56 * 2**20

NP = 11008
OFF_QA, OFF_KVA, OFF_RKV, OFF_ZM, OFF_ZR, OFF_GM, OFF_GR, OFF_LORA, OFF_KR = 0, 512, 1024, 4096, 5120, 6144, 8192, 10240, 10752
NLERP = 3584

CHUNK = 64
NCH = T // CHUNK


def _dg(a, b, ca, cb, batch=False, prec=None):
    bd = ((0,), (0,)) if batch else ((), ())
    return lax.dot_general(a, b, (((ca,), (cb,)), bd), precision=prec, preferred_element_type=F32)


@jax.custom_vjp
def bdot(a, b):
    return _dg(a.astype(BF16), b.astype(BF16), 1, 0)


def _bdot_fwd(a, b):
    return bdot(a, b), (a, b)


def _bdot_bwd(res, g):
    a, b = res
    gb = g.astype(BF16)
    da = _dg(gb, b.astype(BF16), 1, 1)
    db = _dg(a.astype(BF16), gb, 0, 0)
    return da.astype(a.dtype), db.astype(b.dtype)


bdot.defvjp(_bdot_fwd, _bdot_bwd)


def _split(x):
    hi = x.astype(BF16)
    lo = (x - hi.astype(F32)).astype(BF16)
    return hi, lo


@jax.custom_vjp
def gsum(x, g2):
    hi, lo = _split(x)
    return _dg(hi, g2, 1, 0) + _dg(lo, g2, 1, 0)


def _gsum_fwd(x, g2):
    return gsum(x, g2), g2


def _gsum_bwd(g2, g):
    hi, lo = _split(g)
    return _dg(hi, g2, 1, 1) + _dg(lo, g2, 1, 1), jnp.zeros_like(g2)


gsum.defvjp(_gsum_fwd, _gsum_bwd)


def headsum(x, g2):
    return jnp.concatenate([gsum(x[:, i * LANE:(i + 1) * LANE], g2) for i in range(x.shape[1] // LANE)], axis=1)


def _terms(x, n):
    out = []
    for i in range(n):
        t = x.astype(BF16)
        out.append(t)
        if i < n - 1:
            x = x - t.astype(F32)
    return out


def _bmm(a, b, ca, cb, na, nb):
    acc = None
    for i, ai in enumerate(_terms(a, na)):
        for j, bj in enumerate(_terms(b, nb)):
            if i + j < max(na, nb):
                p = _dg(ai, bj, ca, cb, True)
                acc = p if acc is None else acc + p
    return acc


_NN, _NT, _TN = (2, 1), (2, 2), (1, 1)


def _make_dots(nf, nb_nn, nb_nt, nb_tn):
    @jax.custom_vjp
    def nn(a, b):
        return _bmm(a, b, *_NN, nf, nf)

    @jax.custom_vjp
    def nt(a, b):
        return _bmm(a, b, *_NT, nf, nf)

    @jax.custom_vjp
    def tn(a, b):
        return _bmm(a, b, *_TN, nf, nf)

    nn.defvjp(lambda a, b: (nn(a, b), (a, b)),
              lambda r, g: (_bmm(g, r[1], *_NT, nb_nn, nb_nn), _bmm(r[0], g, *_TN, nb_nn, nb_nn)))
    nt.defvjp(lambda a, b: (nt(a, b), (a, b)),
              lambda r, g: (_bmm(g, r[1], *_NN, 1, nb_nt), _bmm(g, r[0], *_TN, 1, nb_nt)))
    tn.defvjp(lambda a, b: (tn(a, b), (a, b)),
              lambda r, g: (_bmm(r[1], g, *_NT, nb_tn, nb_tn), _bmm(r[0], g, *_NN, nb_tn, nb_tn)))
    return nn, nt, tn


_SCAN_NF, _SCAN_NB = 1, 1
nn, nt, tn = _make_dots(_SCAN_NF, 1, 2, 1)


@jax.custom_vjp
def cumdot(ones, x):
    return _bmm(ones, x, *_NN, 1, 3)


cumdot.defvjp(lambda o, x: (cumdot(o, x), o), lambda o, g: (jnp.zeros_like(o), _bmm(o, g, *_TN, 1, 3)))


def _solve_powers(l):
    pw = [l]
    for _ in range(int(math.log2(l.shape[-1])) - 1):
        pw.append(_bmm(pw[-1], pw[-1], *_NN, _SCAN_NF, _SCAN_NF))
    return pw


@jax.custom_vjp
def tri_solve(l, rhs):
    x = rhs
    for p in _solve_powers(l):
        x = x + _bmm(p, x, *_NN, _SCAN_NF, _SCAN_NF)
    return x


def _tri_solve_fwd(l, rhs):
    pw = _solve_powers(l)
    x = rhs
    for p in pw:
        x = x + _bmm(p, x, *_NN, _SCAN_NF, _SCAN_NF)
    return x, (pw, x)


def _tri_solve_bwd(res, g):
    pw, x = res
    y = g
    for p in pw:
        y = y + _bmm(p, y, *_TN, _SCAN_NB, _SCAN_NB)
    return _bmm(y, x, *_NT, _SCAN_NB, _SCAN_NB), y


tri_solve.defvjp(_tri_solve_fwd, _tri_solve_bwd)


def _rms(x, g):
    return x * lax.rsqrt(jnp.mean(x * x, axis=-1, keepdims=True) + NORM_EPS) * g


def _softplus(x):
    pos = x > 0
    return jnp.where(pos, x, 0.0) + jnp.log(1.0 + jnp.exp(-jnp.where(pos, x, -x)))


def _silu(z):
    return z * jax.nn.sigmoid(z)


def _tile(n, cands):
    for c in cands:
        if n % c == 0:
            return c
    raise ValueError(n)


_MM_VMEM_BYTES = 32 * 2**20


def _mm_tiles(m, n, k):
    best = None
    for tm in (2048, 1024, 512, 256):
        for tn_ in (2048, 1024, 512, 256):
            for d in range(k // LANE, 0, -1):
                tk = LANE * d
                if m % tm or n % tn_ or k % tk:
                    continue
                nk = k // tk
                vmem = 4 * tk * (tm + tn_) + 8 * tm * tn_ + (4 * tm * tn_ if nk > 1 else 0)
                if vmem > _MM_VMEM_BYTES:
                    continue
                a_reads = n // tn_ if nk > 1 else 1
                b_reads = 1 if (nk == 1 and n == tn_) else m // tm
                acc_rmw = nk * m * n if nk > 1 else 0
                cost = (a_reads * m * k + b_reads * k * n + acc_rmw, -tm * tn_ * tk)
                if best is None or cost < best[0]:
                    best = (cost, (tm, tn_, tk))
    return best[1]


class Side:
    def __init__(self, ins, outs, sems, start, finish):
        self.ins, self.outs, self.sems, self.start, self.finish = ins, outs, sems, start, finish


def matmul(name, a, b, mode, out_dtype=F32, side=None, after=()):
    if mode == "nn":
        (m, k), n = a.shape, b.shape[1]
    elif mode == "nt":
        (m, k), n = a.shape, b.shape[0]
    else:
        (k, m), n = a.shape, b.shape[1]
    tm, tn_, tk = _mm_tiles(m, n, k)
    nk = k // tk
    if mode == "nn":
        a_spec = pl.BlockSpec((tm, tk), lambda i, j, kk: (i, kk))
        b_spec = pl.BlockSpec((tk, tn_), lambda i, j, kk: (kk, j))
        ca, cb = 1, 0
    elif mode == "nt":
        a_spec = pl.BlockSpec((tm, tk), lambda i, j, kk: (i, kk))
        b_spec = pl.BlockSpec((tn_, tk), lambda i, j, kk: (j, kk))
        ca, cb = 1, 1
    else:
        a_spec = pl.BlockSpec((tk, tm), lambda i, j, kk: (kk, i))
        b_spec = pl.BlockSpec((tk, tn_), lambda i, j, kk: (kk, j))
        ca, cb = 0, 0

    grid = (m // tm, n // tn_, nk)
    n_in = len(side.ins) if side else 0
    n_out = len(side.outs) if side else 0
    n_dep = len(after)

    def body(a_ref, b_ref, *rest):
        rest = rest[n_dep:]
        s_ins, o_ref, s_outs = rest[:n_in], rest[n_in], rest[n_in + 1:n_in + 1 + n_out]
        scratch = rest[n_in + 1 + n_out:]
        acc, s_sems = (scratch[:1], scratch[1:]) if nk > 1 else ((), scratch)
        if side:
            step = (pl.program_id(0) * grid[1] + pl.program_id(1)) * grid[2] + pl.program_id(2)

            @pl.when(step == 0)
            def _():
                side.start(s_ins, s_outs, s_sems)

        part = _dg(a_ref[...].astype(BF16), b_ref[...].astype(BF16), ca, cb)
        if nk == 1:
            o_ref[...] = part.astype(o_ref.dtype)
        else:
            acc_ref, kk = acc[0], pl.program_id(2)

            @pl.when(kk == 0)
            def _():
                acc_ref[...] = part

            @pl.when(kk > 0)
            def _():
                acc_ref[...] += part

            @pl.when(kk == nk - 1)
            def _():
                o_ref[...] = acc_ref[...].astype(o_ref.dtype)

        if side:
            @pl.when(step == grid[0] * grid[1] * grid[2] - 1)
            def _():
                side.finish(s_ins, s_outs, s_sems)

    res = pl.pallas_call(
        body, name=name, grid=grid,
        in_specs=[a_spec, b_spec] + [_ANY] * (n_dep + n_in),
        out_specs=[pl.BlockSpec((tm, tn_), lambda i, j, kk: (i, j))] + [_ANY] * n_out,
        out_shape=[jax.ShapeDtypeStruct((m, n), out_dtype)] + (list(side.outs) if side else []),
        scratch_shapes=([pltpu.VMEM((tm, tn_), F32)] if nk > 1 else []) + (list(side.sems) if side else []),
        compiler_params=pltpu.CompilerParams(
            dimension_semantics=("arbitrary",) * 3 if side else ("parallel", "parallel", "arbitrary"),
            vmem_limit_bytes=VMEM_BIG),
    )(a, b, *after, *(side.ins if side else []))
    return (res[0], res[1:]) if side else res[0]


def _rspec(tr, width, blk):
    return pl.BlockSpec((tr, width), lambda i: (i, blk))


def _full_spec(arr):
    return pl.BlockSpec(arr.shape, lambda i: (0,) * arr.ndim)


class Stage:
    def __init__(self, name, f, outs, tr, diff_rows, diff_params, drow_dtypes):
        self.name, self.f, self.outs, self.tr = name, f, outs, tr
        self.diff_rows, self.diff_params, self.drow_dtypes = diff_rows, diff_params, drow_dtypes

    def fwd(self, rows, params):
        f, nr, npar = self.f, len(rows), len(params)
        stored = [(w, dt) for (w, dt) in self.outs if dt is not None]
        keep = [i for i, (w, dt) in enumerate(self.outs) if dt is not None]

        def body(*refs):
            vals = f(*[r[...].astype(F32) for r in refs[:nr]], *[p[...] for p in refs[nr:nr + npar]])
            for o_ref, i in zip(refs[nr + npar:], keep):
                o_ref[...] = vals[i].astype(o_ref.dtype)

        return pl.pallas_call(
            body, name=self.name + "_fwd", grid=(T // self.tr,),
            in_specs=[_rspec(self.tr, w, b) for (_, w, b) in rows] + [_full_spec(p) for p in params],
            out_specs=[_rspec(self.tr, w, 0) for (w, _) in stored],
            out_shape=[jax.ShapeDtypeStruct((T, w), dt) for (w, dt) in stored],
            compiler_params=pltpu.CompilerParams(dimension_semantics=("arbitrary",), vmem_limit_bytes=VMEM_BIG),
        )(*[r[0] for r in rows], *params)

    def bwd(self, rows, params, cts):
        f, nr, npar = self.f, len(rows), len(params)
        dr_idx, dp_idx = self.diff_rows, self.diff_params
        flat_cts = [c for lst in cts for c in lst]
        nct = len(flat_cts)

        def body(*refs):
            row_refs, par_refs = refs[:nr], refs[nr:nr + npar]
            ct_refs = refs[nr + npar:nr + npar + nct]
            drow_refs = refs[nr + npar + nct:nr + npar + nct + len(dr_idx)]
            dpar_refs = refs[nr + npar + nct + len(dr_idx):]
            row_vals = [r[...].astype(F32) for r in row_refs]
            par_vals = [p[...] for p in par_refs]

            def g(*dv):
                rv, pv = list(row_vals), list(par_vals)
                for j, i in enumerate(dr_idx):
                    rv[i] = dv[j]
                for j, i in enumerate(dp_idx):
                    pv[i] = dv[len(dr_idx) + j]
                return f(*rv, *pv)

            _, vjp = jax.vjp(g, *[row_vals[i] for i in dr_idx], *[par_vals[i] for i in dp_idx])
            ct_vals, pos = [], 0
            for lst in cts:
                acc = ct_refs[pos][...].astype(F32)
                for q in range(1, len(lst)):
                    acc = acc + ct_refs[pos + q][...].astype(F32)
                pos += len(lst)
                ct_vals.append(acc)
            grads = vjp(tuple(ct_vals))
            for j, r in enumerate(drow_refs):
                r[...] = grads[j].astype(r.dtype)

            @pl.when(pl.program_id(0) == 0)
            def _():
                for r in dpar_refs:
                    r[...] = jnp.zeros_like(r)

            for j, r in enumerate(dpar_refs):
                r[...] += grads[len(dr_idx) + j].astype(F32)

        drow_shapes = [jax.ShapeDtypeStruct((T, rows[i][1]), dt) for i, dt in zip(dr_idx, self.drow_dtypes)]
        dpar_shapes = [jax.ShapeDtypeStruct(params[i].shape, F32) for i in dp_idx]
        res = pl.pallas_call(
            body, name=self.name + "_bwd", grid=(T // self.tr,),
            in_specs=[_rspec(self.tr, w, b) for (_, w, b) in rows] + [_full_spec(p) for p in params]
            + [_rspec(self.tr, w, b) for (_, w, b) in flat_cts],
            out_specs=[_rspec(self.tr, rows[i][1], 0) for i in dr_idx] + [_full_spec(params[i]) for i in dp_idx],
            out_shape=drow_shapes + dpar_shapes,
            compiler_params=pltpu.CompilerParams(dimension_semantics=("arbitrary",), vmem_limit_bytes=VMEM_BIG),
        )(*[r[0] for r in rows], *params, *[c[0] for c in flat_cts])
        return res[:len(dr_idx)], res[len(dr_idx):]


def f_pre(x, g):
    return _rms(x, g), x


@jax.custom_vjp
def swap32(t):
    width = t.shape[1]
    lane = lax.broadcasted_iota(jnp.int32, t.shape, 1) % LANE
    return jnp.where(lane < 32, pltpu.roll(t, width - 32, 1), jnp.where(lane < 64, pltpu.roll(t, 32, 1), 0.0))


swap32.defvjp(lambda t: (swap32(t), None), lambda _, g: (swap32(g),))


def f_mla(q_a, kv_a, kr, cq, sq, ck, sk, gq, gkv, wq, wkv):
    q = bdot(_rms(q_a, gq), wq)
    kv = bdot(_rms(kv_a, gkv), wkv)
    t, k = q[:, 1024:], kr[:, :LANE]
    return (q[:, :1024], t * cq + swap32(t) * sq, kv[:, :1024], k * ck + swap32(k) * sk, kv[:, 1024:])


def f_rwkv_pre(lerp, w0f, w0b, a0f, a0b, kkw, kaw, w2f, w2b, a2f, a2b, g2):
    r, k, v = lerp[:, :RW], lerp[:, RW:2 * RW], lerp[:, 2 * RW:3 * RW]
    wdf, wdb, adf, adb = (lerp[:, 3 * RW + i * LANE:3 * RW + (i + 1) * LANE] for i in range(4))

    def logdecay(w0, wd, w2):
        z = w0 + bdot(jnp.tanh(wd), w2)
        return -jnp.exp(-_softplus(-z) - 0.5)

    a_f = jax.nn.sigmoid(a0f + bdot(adf, a2f))
    a_b = jax.nn.sigmoid(a0b + bdot(adb, a2b))
    kk = k * kkw
    kk = kk / jnp.maximum(jnp.sqrt(headsum(kk * kk, g2)), 1e-12)
    return (r, v, logdecay(w0f, wdf, w2f), logdecay(w0b, wdb, w2b),
            k * (1.0 + (a_f - 1.0) * kaw), k * (1.0 + (a_b - 1.0) * kaw), -kk, kk * a_f, kk * a_b)


def f_rwkv_post(yf, yb, r, kf, kb, v, z, gng, gnb, rk, g2):
    y = yf + yb
    mu = headsum(y, g2) * (1.0 / RN)
    d = y - mu
    var = headsum(d * d, g2) * (1.0 / RN)
    yn = d * lax.rsqrt(var + GN_EPS) * gng + gnb
    bonus = headsum(r * (kf + kb) * rk, g2) * v
    return ((yn + bonus) * _silu(z),)


def f_gate(y, z):
    return (y * _silu(z),)


def f_merge(um, ur, gm, gr):
    return (jax.nn.sigmoid(gm) * um + jax.nn.sigmoid(gr) * ur,)


_SHIFT_W = 256


def _lerp_colblock(j):
    return jnp.where(j < 3072 // _SHIFT_W, OFF_RKV // _SHIFT_W + j, OFF_LORA // _SHIFT_W + j - 3072 // _SHIFT_W)


def _nbr_mean(x):
    row = lax.broadcasted_iota(jnp.int32, x.shape, 0)
    up = jnp.where(row == 0, 0.0, pltpu.roll(x, 1, 0))
    dn = jnp.where(row == T - 1, 0.0, pltpu.roll(x, T - 1, 0))
    return 0.5 * (up + dn)


def shift_fwd(proj, mu):
    def body(x_ref, mu_ref, o_ref):
        x = x_ref[...]
        o_ref[...] = x + mu_ref[...] * (_nbr_mean(x) - x)

    return pl.pallas_call(
        body, name="shift_fwd", grid=(NLERP // _SHIFT_W,),
        in_specs=[pl.BlockSpec((T, _SHIFT_W), lambda j: (0, _lerp_colblock(j))),
                  pl.BlockSpec((1, _SHIFT_W), lambda j: (0, j))],
        out_specs=pl.BlockSpec((T, _SHIFT_W), lambda j: (0, j)),
        out_shape=jax.ShapeDtypeStruct((T, NLERP), F32),
        compiler_params=pltpu.CompilerParams(dimension_semantics=("parallel",), vmem_limit_bytes=VMEM_BIG),
    )(proj, mu)


def shift_bwd(proj, mu, g):
    def body(x_ref, mu_ref, g_ref, dx_ref, dmu_ref):
        x, gv = x_ref[...], g_ref[...]
        dmu_ref[...] = jnp.sum(gv * (_nbr_mean(x) - x), axis=0, keepdims=True)
        gm = gv * mu_ref[...]
        dx_ref[...] = (gv - gm + _nbr_mean(gm)).astype(dx_ref.dtype)

    col = pl.BlockSpec((T, _SHIFT_W), lambda j: (0, j))
    vec = pl.BlockSpec((1, _SHIFT_W), lambda j: (0, j))
    return pl.pallas_call(
        body, name="shift_bwd", grid=(NLERP // _SHIFT_W,),
        in_specs=[pl.BlockSpec((T, _SHIFT_W), lambda j: (0, _lerp_colblock(j))), vec, col],
        out_specs=[col, vec],
        out_shape=[jax.ShapeDtypeStruct((T, NLERP), BF16), jax.ShapeDtypeStruct((1, NLERP), F32)],
        compiler_params=pltpu.CompilerParams(dimension_semantics=("parallel",), vmem_limit_bytes=VMEM_BIG),
    )(proj, mu, g)


_TQ_F, _TQ_B = 256, 512
_ATT_SCALE = (NOPE + ROPE) ** -0.5


def _probs(q, k):
    s = _dg(q, k, 1, 1) * _ATT_SCALE
    e = jnp.exp(s - jnp.max(s, axis=-1, keepdims=True))
    return e * (1.0 / jnp.sum(e, axis=-1, keepdims=True))


def _q_blk(tq):
    return pl.BlockSpec((tq, LANE), lambda h, i: (i, h))


_K_BLK = pl.BlockSpec((T, LANE), lambda h, i: (0, h))
_KR_BLK = pl.BlockSpec((T, LANE), lambda h, i: (0, 0))


def _load_qk(qn_ref, qr_ref, kn_ref, kr_ref, kcat_ref):
    @pl.when(pl.program_id(1) == 0)
    def _():
        kcat_ref[:, :LANE] = kn_ref[...]
        kcat_ref[:, LANE:] = kr_ref[...]

    return jnp.concatenate([qn_ref[...], qr_ref[...]], axis=1), kcat_ref[...]


def attn_fwd(qn, qr, kn, kr, v, side):
    n_in, n_out, steps = len(side.ins), len(side.outs), HEADS * (T // _TQ_F)

    def body(qn_ref, qr_ref, kn_ref, kr_ref, v_ref, *rest):
        s_ins, o_ref, s_outs = rest[:n_in], rest[n_in], rest[n_in + 1:n_in + 1 + n_out]
        kcat_ref, s_sems = rest[n_in + 1 + n_out], rest[n_in + 2 + n_out:]
        step = pl.program_id(0) * (T // _TQ_F) + pl.program_id(1)

        @pl.when(step == 0)
        def _():
            side.start(s_ins, s_outs, s_sems)

        q, k = _load_qk(qn_ref, qr_ref, kn_ref, kr_ref, kcat_ref)
        o_ref[...] = _dg(_probs(q, k).astype(BF16), v_ref[...], 1, 0)

        @pl.when(step == steps - 1)
        def _():
            side.finish(s_ins, s_outs, s_sems)

    res = pl.pallas_call(
        body, name="attn_fwd", grid=(HEADS, T // _TQ_F),
        in_specs=[_q_blk(_TQ_F), _q_blk(_TQ_F), _K_BLK, _KR_BLK, _K_BLK] + [_ANY] * n_in,
        out_specs=[_q_blk(_TQ_F)] + [_ANY] * n_out,
        out_shape=[jax.ShapeDtypeStruct((T, HEADS * VDIM), F32)] + list(side.outs),
        scratch_shapes=[pltpu.VMEM((T, 2 * LANE), BF16)] + list(side.sems),
        compiler_params=pltpu.CompilerParams(dimension_semantics=("arbitrary", "arbitrary"), vmem_limit_bytes=VMEM_BIG),
    )(qn, qr, kn, kr, v, *side.ins)
    return res[0], res[1:]


def attn_bwd(qn, qr, kn, kr, v, do):
    def body(qn_ref, qr_ref, kn_ref, kr_ref, v_ref, do_ref, dqn_ref, dqr_ref, dkn_ref, dkr_ref, dv_ref, kcat_ref):
        h, i = pl.program_id(0), pl.program_id(1)

        @pl.when(i == 0)
        def _():
            dkn_ref[...] = jnp.zeros_like(dkn_ref)
            dv_ref[...] = jnp.zeros_like(dv_ref)

        @pl.when((i == 0) & (h == 0))
        def _():
            dkr_ref[...] = jnp.zeros_like(dkr_ref)

        q, k = _load_qk(qn_ref, qr_ref, kn_ref, kr_ref, kcat_ref)
        dob = do_ref[...].astype(BF16)
        p = _probs(q, k)
        dv_ref[...] += _dg(p.astype(BF16), dob, 0, 0)
        dp = _dg(dob, v_ref[...], 1, 1)
        ds = (p * (dp - jnp.sum(dp * p, axis=-1, keepdims=True)) * _ATT_SCALE).astype(BF16)
        dq = _dg(ds, k, 1, 0)
        dqn_ref[...] = dq[:, :LANE]
        dqr_ref[...] = dq[:, LANE:]
        dk = _dg(ds, q, 0, 0)
        dkn_ref[...] += dk[:, :LANE]
        dkr_ref[...] += dk[:, LANE:]

    wide = jax.ShapeDtypeStruct((T, HEADS * LANE), F32)
    return pl.pallas_call(
        body, name="attn_bwd", grid=(HEADS, T // _TQ_B),
        in_specs=[_q_blk(_TQ_B), _q_blk(_TQ_B), _K_BLK, _KR_BLK, _K_BLK, _q_blk(_TQ_B)],
        out_specs=[_q_blk(_TQ_B), _q_blk(_TQ_B), _K_BLK, _KR_BLK, _K_BLK],
        out_shape=[wide, wide, wide, jax.ShapeDtypeStruct((T, LANE), F32), wide],
        scratch_shapes=[pltpu.VMEM((T, 2 * LANE), BF16)],
        compiler_params=pltpu.CompilerParams(dimension_semantics=("arbitrary", "arbitrary"), vmem_limit_bytes=VMEM_BIG),
    )(qn, qr, kn, kr, v, do)


def _chunk(r, lw, k, v, a, b, ht, *, reverse):
    hb, c, _ = r.shape
    ti = lax.broadcasted_iota(jnp.int32, (c, c), 0)
    si = lax.broadcasted_iota(jnp.int32, (c, c), 1)
    incl = (si >= ti) if reverse else (si <= ti)
    strict = (si > ti) if reverse else (si < ti)
    ones = jnp.broadcast_to(incl.astype(F32)[None], (hb, c, c))
    cum = cumdot(ones, lw)
    cum_ex = cum - lw
    tot = jnp.sum(lw, axis=1, keepdims=True)
    mid = 0.5 * tot
    rt, at = r * jnp.exp(cum - mid), a * jnp.exp(cum_ex - mid)
    einv = jnp.exp(mid - cum)
    kt, bt = k * einv, b * einv
    m_ab = jnp.where(strict, nt(at, bt), 0.0)
    m_ak = jnp.where(strict, nt(at, kt), 0.0)
    m_rb = jnp.where(incl, nt(rt, bt), 0.0)
    m_rk = jnp.where(incl, nt(rt, kt), 0.0)
    u = tri_solve(m_ab, nt(a * jnp.exp(cum_ex), ht) + nn(m_ak, v))
    y = nt(r * jnp.exp(cum), ht) + nn(m_rb, u) + nn(m_rk, v)
    eend = jnp.exp(tot - cum)
    ht_new = ht * jnp.exp(tot) + tn(u, b * eend) + tn(v, k * eend)
    return y, ht_new


_HB_F, _HB_B = 16, 16


def _split_heads(x):
    return jnp.stack([x[:, i * RN:(i + 1) * RN] for i in range(x.shape[1] // RN)])


def _merge_heads(y):
    return jnp.concatenate([y[i] for i in range(y.shape[0])], axis=1)


def _chunk_map(reverse, backward):
    flip = reverse != backward
    return (lambda g, c: (NCH - 1 - c, g)) if flip else (lambda g, c: (c, g))


def scan_fwd(name, r, lw, k, v, a, b, reverse):
    hb = _HB_F
    cmap = _chunk_map(reverse, False)

    def body(r_ref, lw_ref, k_ref, v_ref, a_ref, b_ref, y_ref, h0_ref, ht_ref):
        @pl.when(pl.program_id(1) == 0)
        def _():
            ht_ref[...] = jnp.zeros_like(ht_ref)

        ht = ht_ref[...]
        h0_ref[0] = ht
        ins = [_split_heads(x[...]) for x in (r_ref, lw_ref, k_ref, v_ref, a_ref, b_ref)]
        y, hn = _chunk(*ins, ht, reverse=reverse)
        y_ref[...] = _merge_heads(y)
        ht_ref[...] = hn

    io = pl.BlockSpec((CHUNK, hb * RN), cmap)
    return pl.pallas_call(
        body, name=name, grid=(RH // hb, NCH),
        in_specs=[io] * 6,
        out_specs=[io, pl.BlockSpec((1, hb, RN, RN), lambda g, c: (cmap(g, c)[0], g, 0, 0))],
        out_shape=[jax.ShapeDtypeStruct((T, RW), F32), jax.ShapeDtypeStruct((NCH, RH, RN, RN), F32)],
        scratch_shapes=[pltpu.VMEM((hb, RN, RN), F32)],
        compiler_params=pltpu.CompilerParams(dimension_semantics=("parallel", "arbitrary"), vmem_limit_bytes=VMEM_BIG),
    )(r, lw, k, v, a, b)


def scan_bwd(name, r, lw, k, v, a, b, h0, dy, reverse):
    hb = _HB_B
    cmap = _chunk_map(reverse, True)

    def body(r_ref, lw_ref, k_ref, v_ref, a_ref, b_ref, h0_ref, dy_ref, *rest):
        d_refs, dht_ref = rest[:6], rest[6]

        @pl.when(pl.program_id(1) == 0)
        def _():
            dht_ref[...] = jnp.zeros_like(dht_ref)

        ins = [_split_heads(x[...]) for x in (r_ref, lw_ref, k_ref, v_ref, a_ref, b_ref)]
        _, vjp = jax.vjp(functools.partial(_chunk, reverse=reverse), *ins, h0_ref[0])
        grads = vjp((_split_heads(dy_ref[...]), dht_ref[...]))
        for d_ref, gval in zip(d_refs, grads[:6]):
            d_ref[...] = _merge_heads(gval).astype(d_ref.dtype)
        dht_ref[...] = grads[6]

    io = pl.BlockSpec((CHUNK, hb * RN), cmap)
    return pl.pallas_call(
        body, name=name, grid=(RH // hb, NCH),
        in_specs=[io] * 6 + [pl.BlockSpec((1, hb, RN, RN), lambda g, c: (cmap(g, c)[0], g, 0, 0)), io],
        out_specs=[io] * 6,
        out_shape=[jax.ShapeDtypeStruct((T, RW), F32 if i == 1 else BF16) for i in range(6)],
        scratch_shapes=[pltpu.VMEM((hb, RN, RN), F32)],
        compiler_params=pltpu.CompilerParams(dimension_semantics=("parallel", "arbitrary"), vmem_limit_bytes=VMEM_BIG),
    )(r, lw, k, v, a, b, h0, dy)


def loss_stage(out, x2, tgt, g_post):
    tr = 256

    def body(o_ref, x_ref, t_ref, g_ref, do_ref, dy_ref, dg_ref, loss_ref):
        @pl.when(pl.program_id(0) == 0)
        def _():
            dg_ref[...] = jnp.zeros_like(dg_ref)
            loss_ref[...] = jnp.zeros_like(loss_ref)

        nrm, vjp = jax.vjp(_rms, o_ref[...], g_ref[...])
        e = x_ref[...] + nrm - t_ref[...]
        s = jnp.sum(jnp.sum(e * e, axis=1, keepdims=True), axis=0, keepdims=True)
        loss_ref[...] += jnp.broadcast_to(s * (0.5 / D), loss_ref.shape)
        dy = e * (1.0 / D)
        do, dg = vjp(dy)
        do_ref[...] = do.astype(do_ref.dtype)
        dy_ref[...] = dy
        dg_ref[...] += dg

    row = pl.BlockSpec((tr, D), lambda i: (i, 0))
    return pl.pallas_call(
        body, name="loss_stage", grid=(T // tr,),
        in_specs=[row, row, row, pl.BlockSpec((1, D), lambda i: (0, 0))],
        out_specs=[row, row, pl.BlockSpec((1, D), lambda i: (0, 0)), pl.BlockSpec((8, LANE), lambda i: (0, 0))],
        out_shape=[jax.ShapeDtypeStruct((T, D), BF16), jax.ShapeDtypeStruct((T, D), F32),
                   jax.ShapeDtypeStruct((1, D), F32), jax.ShapeDtypeStruct((8, LANE), F32)],
        compiler_params=pltpu.CompilerParams(dimension_semantics=("arbitrary",), vmem_limit_bytes=VMEM_BIG),
    )(out, x2, tgt, g_post)


_EW_BLOCK_BYTES = 1 << 20


def _row_tile(rows, cols):
    best = None
    for tr in range(16, rows + 1, 16):
        if rows % tr == 0 and tr * cols * 4 <= _EW_BLOCK_BYTES:
            best = tr
    return best or rows


def _axis_tile(shape, axis, words):
    rows, cols = shape
    n, other, unit = (rows, cols, 16) if axis == 0 else (cols, rows, LANE)
    best = unit if n % unit == 0 else n
    for t in range(unit, n + 1, unit):
        if n % t == 0 and t * other * words * 4 <= _EW_BLOCK_BYTES:
            best = t
    blk = (best, cols) if axis == 0 else (rows, best)
    at = (lambda s: (s, 0)) if axis == 0 else (lambda s: (0, s))
    return blk, n // best, at


def _adamw_update(g, w_ref, m_ref, v_ref, g_ref, d_ref, nm_ref, nv_ref):
    mm = ADAM_B1 * m_ref[...] + (1.0 - ADAM_B1) * g
    vv = ADAM_B2 * v_ref[...] + (1.0 - ADAM_B2) * (g * g)
    m_hat = mm / (1.0 - ADAM_B1 ** ADAM_STEP)
    v_hat = vv / (1.0 - ADAM_B2 ** ADAM_STEP)
    g_ref[...] = g
    d_ref[...] = -ADAM_LR * (m_hat / (jnp.sqrt(v_hat) + ADAM_EPS) + ADAM_WD * w_ref[...])
    nm_ref[...] = mm
    nv_ref[...] = vv


def adamw(name, w, m, v, parts):
    rows, cols = w.shape
    br = _row_tile(rows, cols)
    npart = len(parts)

    def body(w_ref, m_ref, v_ref, *rest):
        g = rest[0][...].astype(F32)
        for p in rest[1:npart]:
            g = g + p[...].astype(F32)
        _adamw_update(g, w_ref, m_ref, v_ref, *rest[npart:])

    blk = pl.BlockSpec((br, cols), lambda i: (i, 0))
    return pl.pallas_call(
        body, name=name, grid=(rows // br,),
        in_specs=[blk] * (3 + npart), out_specs=[blk] * 4,
        out_shape=[jax.ShapeDtypeStruct((rows, cols), F32)] * 4,
        compiler_params=pltpu.CompilerParams(dimension_semantics=("parallel",), vmem_limit_bytes=VMEM_BIG),
    )(w, m, v, *parts)


def adamw_halves(name, place, w, m, v, mine, theirs, axis):
    half_shape = mine.shape
    blk_shape, nb, at = _axis_tile(half_shape, axis, 1)

    def body(p_ref, w_ref, m_ref, v_ref, a_ref, b_ref, *outs):
        own = (pl.program_id(0) // nb) == p_ref[0]
        _adamw_update(jnp.where(own, a_ref[...], b_ref[...]), w_ref, m_ref, v_ref, *outs)

    blk = pl.BlockSpec(blk_shape, lambda i, p: at(i))
    half = pl.BlockSpec(blk_shape, lambda i, p: at(i % nb))
    return pl.pallas_call(
        body, name=name,
        grid_spec=pltpu.PrefetchScalarGridSpec(num_scalar_prefetch=1, grid=(2 * nb,),
                                               in_specs=[blk] * 3 + [half] * 2, out_specs=[blk] * 4),
        out_shape=[jax.ShapeDtypeStruct(w.shape, F32)] * 4,
        compiler_params=pltpu.CompilerParams(dimension_semantics=("arbitrary",), vmem_limit_bytes=VMEM_BIG),
    )(place, w, m, v, mine, theirs)


def pair_sum(name, place, send, other, axis):
    blk_shape, nb, at = _axis_tile(other.shape[1:], axis, 4)

    def body(p_ref, a_ref, b_ref, o_ref):
        o_ref[...] = (a_ref[...].astype(F32) + b_ref[...].astype(F32)).astype(o_ref.dtype)

    blk = pl.BlockSpec((4,) + blk_shape, lambda i, p: (0,) + at(i))
    mine = pl.BlockSpec((4,) + blk_shape, lambda i, p: (0,) + at(p[0] * nb + i))
    return pl.pallas_call(
        body, name=name,
        grid_spec=pltpu.PrefetchScalarGridSpec(num_scalar_prefetch=1, grid=(nb,), in_specs=[mine, blk], out_specs=blk),
        out_shape=jax.ShapeDtypeStruct(other.shape, BF16),
        compiler_params=pltpu.CompilerParams(dimension_semantics=("arbitrary",), vmem_limit_bytes=VMEM_BIG),
    )(place, send, other)


def sum4(name, place, recv, own, axis):
    blk_shape, nb, at = _axis_tile(recv.shape[1:], axis, 4)

    def body(p_ref, r_ref, s_ref, o_ref):
        me = p_ref[0]
        t = [jnp.where(me == j, s_ref[j], r_ref[j]).astype(F32) for j in range(4)]
        o_ref[...] = ((t[0] + t[1]) + t[2]) + t[3]

    blk = pl.BlockSpec((4,) + blk_shape, lambda i, p: (0,) + at(i))
    return pl.pallas_call(
        body, name=name,
        grid_spec=pltpu.PrefetchScalarGridSpec(num_scalar_prefetch=1, grid=(nb,), in_specs=[blk, blk],
                                               out_specs=pl.BlockSpec(blk_shape, lambda i, p: at(i))),
        out_shape=jax.ShapeDtypeStruct(recv.shape[1:], F32),
        compiler_params=pltpu.CompilerParams(dimension_semantics=("arbitrary",), vmem_limit_bytes=VMEM_BIG),
    )(place, recv, own)


_ANY = pl.BlockSpec(memory_space=pl.ANY)


def _place():
    x, y, c = lax.axis_index("x"), lax.axis_index("y"), lax.axis_index("c")
    return x, y, c, 2 * x + y


def _chip_peers(x, y):
    out = []
    for k in (1, 2, 3):
        px = 1 - x if k & 2 else x
        py = 1 - y if k & 1 else y
        out.append((k, px, py, 2 * px + py))
    return out


def _half(c, shape, axis):
    n = shape[axis] // 2
    sl = pl.ds(pl.multiple_of(c * n, 16 if axis == 0 else LANE), n)
    return (sl,) if axis == 0 else (pl.ds(0, shape[0]), sl)


def gather_weights(srcs, axes):
    side = gather_side(srcs, axes)
    n = len(srcs)

    def body(*refs):
        ins, outs, sems = refs[:n], refs[n:2 * n], refs[2 * n:]
        side.start(ins, outs, sems)
        side.finish(ins, outs, sems)

    return pl.pallas_call(
        body, name="gather_weights", in_specs=[_ANY] * n, out_specs=[_ANY] * n,
        out_shape=side.outs, scratch_shapes=side.sems,
    )(*srcs)


def gather_side(srcs, axes):
    n = len(srcs)

    def copies(src, dst, sems, sends_only=False):
        ssem, rsem, fssem, frsem = sems
        x, y, c, me = _place()
        sib = (x, y, 1 - c)
        out = []
        for i in range(n):
            mine, other = _half(c, srcs[i].shape, axes[i]), _half(1 - c, srcs[i].shape, axes[i])
            for k, px, py, peer in _chip_peers(x, y):
                sems_k = dict(send_sem=ssem.at[i, k - 1], recv_sem=rsem.at[i, k - 1], device_id=(px, py, c),
                              device_id_type=MESH_IDS)
                fsems = dict(send_sem=fssem.at[i, k - 1], recv_sem=frsem.at[i, k - 1], device_id=sib,
                             device_id_type=MESH_IDS)
                got = dst[i].at[(peer,) + mine]
                snd = pltpu.make_async_remote_copy(src_ref=src[i].at[mine], dst_ref=dst[i].at[(me,) + mine], **sems_k)
                if sends_only:
                    out.append(snd)
                    continue
                out.append((
                    snd,
                    pltpu.make_async_remote_copy(src_ref=src[i].at[mine], dst_ref=got, **sems_k),
                    pltpu.make_async_remote_copy(src_ref=got, dst_ref=got, **fsems),
                    pltpu.make_async_remote_copy(src_ref=got, dst_ref=dst[i].at[(peer,) + other], **fsems)))
        return out

    def start(src, dst, sems):
        for snd in copies(src, dst, sems, sends_only=True):
            snd.start()

    def finish(src, dst, sems):
        cps = copies(src, dst, sems)
        for _, rcv, fwd, _ in cps:
            rcv.wait_recv()
            fwd.start()
        for snd, _, fwd, frcv in cps:
            frcv.wait_recv()
            snd.wait_send()
            fwd.wait_send()

    return Side(list(srcs), [jax.ShapeDtypeStruct((4,) + s.shape, s.dtype) for s in srcs],
                [pltpu.SemaphoreType.DMA((n, 3))] * 4, start, finish)


def pair_exchange(name, srcs, axes):
    n = len(srcs)

    def half_shape(s, axis):
        return (4, s.shape[1] // 2, s.shape[2]) if axis == 0 else (4, s.shape[1], s.shape[2] // 2)

    def body(*refs):
        src, other = refs[:n], refs[n:2 * n]
        ssem, rsem = refs[2 * n:]
        x, y, c, _ = _place()
        cps = []
        for i in range(n):
            idx = (pl.ds(0, 4),) + _half(1 - c, srcs[i].shape[1:], axes[i])
            cps.append(pltpu.make_async_remote_copy(
                src_ref=src[i].at[idx], dst_ref=other[i], send_sem=ssem.at[i], recv_sem=rsem.at[i],
                device_id=(x, y, 1 - c), device_id_type=MESH_IDS))
            cps[-1].start()
        for cp in cps:
            cp.wait()

    return pl.pallas_call(
        body, name=name, in_specs=[_ANY] * n, out_specs=[_ANY] * n,
        out_shape=[jax.ShapeDtypeStruct(half_shape(s, a), s.dtype) for s, a in zip(srcs, axes)],
        scratch_shapes=[pltpu.SemaphoreType.DMA((n,))] * 2,
    )(*srcs)


def scatter_grads(srcs):
    side = scatter_side(srcs)
    n = len(srcs)

    def body(*refs):
        ins, outs, sems = refs[:n], refs[n:2 * n], refs[2 * n:]
        side.start(ins, outs, sems)
        side.finish(ins, outs, sems)

    return pl.pallas_call(
        body, name="scatter_grads", in_specs=[_ANY] * n, out_specs=[_ANY] * n,
        out_shape=side.outs, scratch_shapes=side.sems,
    )(*srcs)


def scatter_side(srcs):
    n = len(srcs)

    def copies(src, dst, sems, sends_only=False):
        ssem, rsem = sems
        x, y, c, me = _place()
        out = []
        for i in range(n):
            for k, px, py, peer in _chip_peers(x, y):
                sems_k = dict(send_sem=ssem.at[i, k - 1], recv_sem=rsem.at[i, k - 1], device_id=(px, py, c),
                              device_id_type=MESH_IDS)
                snd = pltpu.make_async_remote_copy(src_ref=src[i].at[peer], dst_ref=dst[i].at[me], **sems_k)
                if sends_only:
                    out.append(snd)
                    continue
                out.append((snd, pltpu.make_async_remote_copy(src_ref=src[i].at[peer], dst_ref=dst[i].at[peer],
                                                              **sems_k)))
        return out

    def start(src, dst, sems):
        for snd in copies(src, dst, sems, sends_only=True):
            snd.start()

    def finish(src, dst, sems):
        for snd, rcv in copies(src, dst, sems):
            rcv.wait_recv()
            snd.wait_send()

    return Side(list(srcs), [jax.ShapeDtypeStruct(s.shape, s.dtype) for s in srcs],
                [pltpu.SemaphoreType.DMA((n, 3))] * 2, start, finish)


_HBM = pl.BlockSpec(memory_space=pltpu.HBM)
_SEM = pl.BlockSpec(memory_space=pltpu.SEMAPHORE)
_DATAFLOW = pltpu.SideEffectType.DATAFLOW_SIDE_EFFECTING


def scatter_start(name, srcs):
    n = len(srcs)
    side = scatter_side(srcs)
    ns = 3 * n

    def body(*refs):
        src, land = refs[:n], refs[n:2 * n]
        sems = refs[2 * n:2 * n + 2 * ns]
        side.start(src, land, (_SemGrid(sems[:ns]), _SemGrid(sems[ns:])))
        refs[-1][...] = jnp.zeros_like(refs[-1])

    hbm = [pltpu.HBM(s.shape, s.dtype) for s in srcs]
    res = pl.pallas_call(
        body, name=name,
        out_shape=[pltpu.SemaphoreType.DMA(())] * (2 * ns) + hbm + hbm + [jax.ShapeDtypeStruct((8, LANE), F32)],
        in_specs=[_HBM] * (2 * n),
        out_specs=[_SEM] * (2 * ns) + [_HBM] * (2 * n) + [pl.BlockSpec(memory_space=pltpu.VMEM)],
        input_output_aliases={i: 2 * ns + i for i in range(2 * n)},
        compiler_params=pltpu.CompilerParams(has_side_effects=_DATAFLOW),
    )(*[pltpu.with_memory_space_constraint(s, pltpu.HBM) for s in srcs],
      *[pltpu.with_memory_space_constraint(lax.empty(s.shape, s.dtype), pltpu.HBM) for s in srcs])
    return res[:2 * ns], res[2 * ns:2 * ns + n], res[2 * ns + n:2 * ns + 2 * n], res[-1]


def scatter_wait(name, sems, srcs, lands, after):
    n = len(srcs)
    side = scatter_side(srcs)
    ns = 3 * n

    def body(*refs):
        src, land = refs[:n], refs[n:2 * n]
        s = refs[2 * n:2 * n + 2 * ns]
        side.finish(src, land, (_SemGrid(s[:ns]), _SemGrid(s[ns:])))

    hbm = [pltpu.HBM(s.shape, s.dtype) for s in srcs]
    res = pl.pallas_call(
        body, name=name, out_shape=hbm + hbm,
        in_specs=[_HBM] * (2 * n) + [_SEM] * (2 * ns) + [_ANY], out_specs=[_HBM] * (2 * n),
        input_output_aliases={i: i for i in range(2 * n)},
        compiler_params=pltpu.CompilerParams(has_side_effects=_DATAFLOW),
    )(*srcs, *lands, *sems, after)
    return res[:n], res[n:]


class _SemGrid:
    def __init__(self, sems):
        self.sems = sems

    @property
    def at(self):
        return self

    def __getitem__(self, ik):
        return self.sems[3 * ik[0] + ik[1]]


def swap_halves(name, srcs):
    n = len(srcs)

    def body(*refs):
        src, dst = refs[:n], refs[n:2 * n]
        ssem, rsem = refs[2 * n:]
        x, y, c, _ = _place()
        cps = []
        for i in range(n):
            cps.append(pltpu.make_async_remote_copy(src_ref=src[i], dst_ref=dst[i], send_sem=ssem.at[i],
                                                    recv_sem=rsem.at[i], device_id=(x, y, 1 - c),
                                                    device_id_type=MESH_IDS))
            cps[-1].start()
        for cp in cps:
            cp.wait()

    return pl.pallas_call(
        body, name=name, in_specs=[_ANY] * n, out_specs=[_ANY] * n,
        out_shape=[jax.ShapeDtypeStruct(s.shape, s.dtype) for s in srcs],
        scratch_shapes=[pltpu.SemaphoreType.DMA((n,))] * 2,
    )(*srcs)


def _ag8_copies(src, dst, sems, sends_only=False):
    x, y, c = lax.axis_index("x"), lax.axis_index("y"), lax.axis_index("c")
    me = 4 * x + 2 * y + c
    out = []
    for k in range(1, 8):
        px = 1 - x if k & 4 else x
        py = 1 - y if k & 2 else y
        pc = 1 - c if k & 1 else c
        peer = 4 * px + 2 * py + pc
        out.append(tuple(pltpu.make_async_remote_copy(
            src_ref=src, dst_ref=dst.at[slot], send_sem=sems[k - 1], recv_sem=sems[7 + k - 1],
            device_id=(px, py, pc), device_id_type=MESH_IDS) for slot in ((me,) if sends_only else (me, peer))))
    return out


def allgather8_start(name, src):
    def body(src_ref, land_ref, *rest):
        for (snd,) in _ag8_copies(src_ref, land_ref, rest[:14], sends_only=True):
            snd.start()

    land = jax.ShapeDtypeStruct((8,) + src.shape, src.dtype)
    res = pl.pallas_call(
        body, name=name,
        out_shape=[pltpu.SemaphoreType.DMA(())] * 14 + [pltpu.HBM(src.shape, src.dtype), pltpu.HBM(land.shape, land.dtype)],
        in_specs=[_HBM, _HBM], out_specs=[_SEM] * 14 + [_HBM, _HBM],
        input_output_aliases={0: 14, 1: 15},
        compiler_params=pltpu.CompilerParams(has_side_effects=_DATAFLOW),
    )(pltpu.with_memory_space_constraint(src, pltpu.HBM),
      pltpu.with_memory_space_constraint(lax.empty(land.shape, land.dtype), pltpu.HBM))
    return res[:14], res[14], res[15]


def allgather8_wait(name, sems, src, land, after):
    def body(src_ref, land_ref, *rest):
        for snd, rcv in _ag8_copies(src_ref, land_ref, rest[:14]):
            rcv.wait_recv()
            snd.wait_send()

    return pl.pallas_call(
        body, name=name, out_shape=[pltpu.HBM(src.shape, src.dtype), pltpu.HBM(land.shape, land.dtype)],
        in_specs=[_HBM, _HBM] + [_SEM] * 14 + [_ANY], out_specs=[_HBM, _HBM],
        input_output_aliases={0: 0, 1: 1},
        compiler_params=pltpu.CompilerParams(has_side_effects=_DATAFLOW),
    )(src, land, *sems, after)


WEIGHTS = ['g_pre', 'w_in', 'mla_q_norm', 'mla_wq_b', 'mla_kv_norm', 'mla_wkv_b', 'rwkv_mu', 'rwkv_w0_f', 'rwkv_w2_f',
           'rwkv_w0_b', 'rwkv_w2_b', 'rwkv_a0_f', 'rwkv_a2_f', 'rwkv_a0_b', 'rwkv_a2_b', 'rwkv_k_k', 'rwkv_k_a',
           'rwkv_r_k', 'rwkv_gn_g', 'rwkv_gn_b', 'w_br_mla', 'w_br_rwkv', 'w_out', 'g_post']
BIG_SHAPES = {'w_in': (D_IN // 4, D), 'mla_wq_b': (Q_RANK, 384), 'mla_wkv_b': (KV_RANK, 512),
              'rwkv_w2_f': (LORA, 256), 'rwkv_w2_b': (LORA, 256), 'rwkv_a2_f': (LORA, 256), 'rwkv_a2_b': (LORA, 256),
              'w_br_mla': (RW, 512), 'w_br_rwkv': (RW, 512), 'w_out': (512, D)}
BIG = list(BIG_SHAPES)
SMALL = [n for n in WEIGHTS if n not in BIG_SHAPES]
SMALL_SHAPES = {'g_pre': (D,), 'mla_q_norm': (Q_RANK,), 'mla_kv_norm': (KV_RANK,), 'rwkv_mu': (3456,),
                'rwkv_w0_f': (RW,), 'rwkv_w0_b': (RW,), 'rwkv_a0_f': (RW,), 'rwkv_a0_b': (RW,), 'rwkv_k_k': (RW,),
                'rwkv_k_a': (RW,), 'rwkv_r_k': (RH, RN), 'rwkv_gn_g': (RW,), 'rwkv_gn_b': (RW,), 'g_post': (D,)}
SMALL_LEN = sum(int(np.prod(s)) for s in SMALL_SHAPES.values())
SMALL_ROWS = 144


UNITS = [('w_in',), ('mla_wq_b',), ('mla_wkv_b',), ('rwkv_w2_f', 'rwkv_w2_b', 'rwkv_a2_f', 'rwkv_a2_b'),
         ('w_br_mla', 'w_br_rwkv'), ('w_out',)]
UNIT_AXIS = [1, 0, 0, 0, 0, 0]
ROW_SHARDED = ('w_in', 'w_out')


def _unit_cat(parts):
    return parts[0] if len(parts) == 1 else jnp.concatenate(parts, axis=0)


def _unit_split(arr, names, axis):
    out, o = {}, 0
    for n in names:
        rows = BIG_SHAPES[n][0]
        out[n] = lax.slice_in_dim(arr, o, o + rows, axis=axis)
        o += rows
    return out


def _gathered(units, ag, own, me):
    out = {}
    for names, arr, mine in zip(units, ag, own):
        slots = [jnp.where(me == j, mine, arr[j]) for j in range(4)]
        for n in names:
            parts = [_unit_split(s, names, 0)[n] for s in slots]
            out[n] = jnp.concatenate(parts, axis=0 if n in ROW_SHARDED else 1)
    return out


def _shards(n, g):
    r, w = BIG_SHAPES[n]
    if n in ROW_SHARDED:
        return [g[j * r:(j + 1) * r] for j in range(4)]
    return [g[:, j * w:(j + 1) * w] for j in range(4)]


def _pack_small(d, extra=None):
    flat = jnp.concatenate([d[n].reshape(-1) for n in SMALL] + ([extra.reshape(-1)] if extra is not None else []))
    return jnp.pad(flat, (0, SMALL_ROWS * LANE - flat.shape[0])).reshape(SMALL_ROWS, LANE)


def _unpack_small(packed):
    flat, out, o = packed.reshape(-1), {}, 0
    for n in SMALL:
        sz = int(np.prod(SMALL_SHAPES[n]))
        out[n] = flat[o:o + sz].reshape(SMALL_SHAPES[n])
        o += sz
    return out


def _perm_w_in(gathered, own, me):
    per = D_IN // 4

    def rows(a, b):
        out = []
        while a < b:
            j, lo = divmod(a, per)
            hi = min(b - j * per, per)
            out.append(jnp.where(me == j, own[lo:hi], gathered[j, lo:hi]))
            a = j * per + hi
        return out

    z = lambda n: [jnp.zeros((n, own.shape[1]), own.dtype)]
    lora = []
    for i in range(4):
        lora += rows(4160 + LORA * i, 4160 + LORA * (i + 1)) + z(LANE - LORA)
    return jnp.concatenate(rows(0, 1024) + rows(1088, 4160) + rows(4544, D_IN) + lora + rows(1024, 1088)
                           + z(256 - ROPE), axis=0)


def _unperm_w_in(g):
    lora = [g[OFF_LORA + LANE * i:OFF_LORA + LANE * i + LORA] for i in range(4)]
    return jnp.concatenate([g[0:1024], g[OFF_KR:OFF_KR + ROPE], g[1024:4096]] + lora + [g[4096:OFF_LORA]], axis=0)


def _perm_wq(w):
    w3 = w.reshape(Q_RANK, HEADS, NOPE + ROPE)
    rope = jnp.pad(w3[:, :, NOPE:], ((0, 0), (0, 0), (0, LANE - ROPE)))
    return jnp.concatenate([w3[:, :, :NOPE].reshape(Q_RANK, -1), rope.reshape(Q_RANK, -1)], axis=1)


def _unperm_wq(g):
    return jnp.concatenate([g[:, :1024].reshape(Q_RANK, HEADS, NOPE),
                            g[:, 1024:].reshape(Q_RANK, HEADS, LANE)[:, :, :ROPE]], axis=2).reshape(Q_RANK, -1)


def _perm_wkv(w):
    w3 = w.reshape(KV_RANK, HEADS, NOPE + VDIM)
    return jnp.concatenate([w3[:, :, :NOPE].reshape(KV_RANK, -1), w3[:, :, NOPE:].reshape(KV_RANK, -1)], axis=1)


def _unperm_wkv(g):
    return jnp.concatenate([g[:, :1024].reshape(KV_RANK, HEADS, NOPE), g[:, 1024:].reshape(KV_RANK, HEADS, VDIM)],
                           axis=2).reshape(KV_RANK, -1)


def _pad_rows(w):
    return jnp.pad(w, ((0, LANE - LORA), (0, 0)))


def _perm_mu(mu):
    parts = [mu[:3072]]
    for i in range(4):
        parts += [mu[3072 + LORA * i:3072 + LORA * (i + 1)], jnp.zeros((LANE - LORA,), mu.dtype)]
    return jnp.concatenate(parts).reshape(1, NLERP)


def _unperm_mu(g):
    g = g.reshape(-1)
    return jnp.concatenate([g[:3072]] + [g[3072 + LANE * i:3072 + LANE * i + LORA] for i in range(4)])


def _constants():
    g2 = np.kron(np.eye(2, dtype=np.float32), np.ones((RN, RN), np.float32))
    pos = jnp.arange(T, dtype=F32)
    inv_freq = jnp.power(ROPE_THETA, -jnp.arange(0, ROPE, 2, dtype=F32) / ROPE)
    ang = pos[:, None] * inv_freq[None, :]
    cos, sin, zero = jnp.cos(ang), jnp.sin(ang), jnp.zeros((T, LANE - ROPE), F32)
    cq = jnp.tile(jnp.concatenate([cos, cos, zero], axis=1), (1, HEADS))
    sq = jnp.tile(jnp.concatenate([-sin, sin, zero], axis=1), (1, HEADS))
    return jnp.asarray(g2, BF16), cq, sq


def _step(x, tgt, w, m, v):
    x2, tgt2 = x.reshape(T, D), tgt.reshape(T, D)
    g2, cq, sq = _constants()
    row = lambda n: w[n].reshape(1, -1)
    w, m, v = ({**t, 'w_in': t['w_in'].T} for t in (w, m, v))

    core, chip = lax.axis_index("c"), 2 * lax.axis_index("x") + lax.axis_index("y")
    core1, chip1 = core.astype(jnp.int32).reshape(1), chip.astype(jnp.int32).reshape(1)
    own_bf = [_unit_cat([w[n].astype(BF16) for n in u]) for u in UNITS]
    wp = _perm_w_in(gather_weights(own_bf[:1], UNIT_AXIS[:1])[0], own_bf[0], chip)
    full = {}
    mu_p = _perm_mu(w['rwkv_mu'])

    st_pre = Stage("pre", f_pre, [(D, BF16), (D, None)], 256, [0], [0], [F32])
    st_mla = Stage("mla", f_mla, [(1024, BF16), (1024, BF16), (1024, BF16), (LANE, BF16), (1024, BF16)], 256,
                   [0, 1, 2], [0, 1, 2, 3], [BF16] * 3)
    st_rpre = Stage("rwkv_pre", f_rwkv_pre, [(RW, F32)] * 9, 256, [0], list(range(10)), [F32])
    st_rpost = Stage("rwkv_post", f_rwkv_post, [(RW, BF16)], 256, [0, 2, 3, 4, 5, 6], [0, 1, 2],
                     [F32, F32, F32, F32, F32, BF16])
    st_gate = Stage("gate", f_gate, [(RW, BF16)], 256, [0, 1], [], [F32, BF16])
    st_merge = Stage("merge", f_merge, [(D, BF16)], 256, [0, 1, 2, 3], [], [BF16] * 4)

    pre_rows, pre_par = [(x2, D, 0)], [row('g_pre')]
    (h,) = st_pre.fwd(pre_rows, pre_par)
    proj, rest = matmul("mm_in", h, wp, "nt", side=gather_side(own_bf[1:4], UNIT_AXIS[1:4]))
    full.update(_gathered(UNITS[1:4], rest, own_bf[1:4], chip))
    wq, wkv = _perm_wq(full['mla_wq_b']), _perm_wkv(full['mla_wkv_b'])
    lora_w = [_pad_rows(full[n]).astype(F32) for n in ('rwkv_w2_f', 'rwkv_w2_b', 'rwkv_a2_f', 'rwkv_a2_b')]

    mla_rows = [(proj, 512, OFF_QA // 512), (proj, 512, OFF_KVA // 512), (proj, 256, OFF_KR // 256),
                (cq, 1024, 0), (sq, 1024, 0), (cq, LANE, 0), (sq, LANE, 0)]
    mla_par = [row('mla_q_norm'), row('mla_kv_norm'), wq, wkv]
    att = st_mla.fwd(mla_rows, mla_par)
    y_mla, rest = attn_fwd(*att, gather_side(own_bf[4:], UNIT_AXIS[4:]))
    full.update(_gathered(UNITS[4:], rest, own_bf[4:], chip))

    lerp = shift_fwd(proj, mu_p)
    rpre_rows = [(lerp, NLERP, 0)]
    rpre_par = [row('rwkv_w0_f'), row('rwkv_w0_b'), row('rwkv_a0_f'), row('rwkv_a0_b'), row('rwkv_k_k'),
                row('rwkv_k_a')] + lora_w + [g2]
    r_, v_, lwf, lwb, kf, kb, an, bf_, bb_ = st_rpre.fwd(rpre_rows, rpre_par)
    fin = [r_, lwf, kf, v_, an, bf_]
    bin_ = [r_, lwb, kb, v_, an, bb_]
    yf, h0f = scan_fwd("scan_f", *fin, reverse=False)
    yb, h0b = scan_fwd("scan_b", *bin_, reverse=True)
    rpost_rows = [(yf, RW, 0), (yb, RW, 0), (r_, RW, 0), (kf, RW, 0), (kb, RW, 0),
                  (v_, RW, 0), (proj, RW, OFF_ZR // RW)]
    rpost_par = [row('rwkv_gn_g'), row('rwkv_gn_b'), row('rwkv_r_k'), g2]
    (gr,) = st_rpost.fwd(rpost_rows, rpost_par)
    gate_rows = [(y_mla, RW, 0), (proj, RW, OFF_ZM // RW)]
    (gm,) = st_gate.fwd(gate_rows, [])
    um = matmul("mm_br_mla", gm, full['w_br_mla'], "nn")
    ur = matmul("mm_br_rwkv", gr, full['w_br_rwkv'], "nn")
    merge_rows = [(um, D, 0), (ur, D, 0), (proj, D, OFF_GM // D), (proj, D, OFF_GR // D)]
    (merged,) = st_merge.fwd(merge_rows, [])
    out = matmul("mm_out", merged, full['w_out'], "nn")
    d_out, dy, dg_post, loss_blk = loss_stage(out, x2, tgt2, row('g_post'))

    gw = {'g_post': dg_post}
    d_merged = matmul("mm_out_dx", d_out, full['w_out'], "nt")
    gw['w_out'] = matmul("mm_out_dw", merged, d_out, "tn")
    (d_um, d_ur, d_gm, d_gr), _ = st_merge.bwd(merge_rows, [], [[(d_merged, D, 0)]])
    d_gmla = matmul("mm_br_mla_dx", d_um, full['w_br_mla'], "nt")
    gw['w_br_mla'] = matmul("mm_br_mla_dw", gm, d_um, "tn")
    d_grw = matmul("mm_br_rwkv_dx", d_ur, full['w_br_rwkv'], "nt")
    gw['w_br_rwkv'] = matmul("mm_br_rwkv_dw", gr, d_ur, "tn")
    (d_ymla, d_zm), _ = st_gate.bwd(gate_rows, [], [[(d_gmla, RW, 0)]])
    (d_y, d_r3, d_kf2, d_kb2, d_v3, d_zr), (gw['rwkv_gn_g'], gw['rwkv_gn_b'], d_rk) = st_rpost.bwd(
        rpost_rows, rpost_par, [[(d_grw, RW, 0)]])
    gw['rwkv_r_k'] = d_rk
    sf = scan_bwd("scan_f_bwd", *fin, h0f, d_y, reverse=False)
    sb = scan_bwd("scan_b_bwd", *bin_, h0b, d_y, reverse=True)
    c = lambda *ts: [(t, RW, 0) for t in ts]
    rpre_cts = [c(sf[0], sb[0], d_r3), c(sf[3], sb[3], d_v3), c(sf[1]), c(sb[1]), c(sf[2], d_kf2), c(sb[2], d_kb2),
                c(sf[4], sb[4]), c(sf[5]), c(sb[5])]
    (d_rin,), rpre_g = st_rpre.bwd(rpre_rows, rpre_par, rpre_cts)
    for n, gval in zip(('rwkv_w0_f', 'rwkv_w0_b', 'rwkv_a0_f', 'rwkv_a0_b', 'rwkv_k_k', 'rwkv_k_a'), rpre_g[:6]):
        gw[n] = gval
    for n, gval in zip(('rwkv_w2_f', 'rwkv_w2_b', 'rwkv_a2_f', 'rwkv_a2_b'), rpre_g[6:]):
        gw[n] = gval[:LORA]
    d_lerp, d_mu = shift_bwd(proj, mu_p, d_rin)
    gw['rwkv_mu'] = _unperm_mu(d_mu)

    mla_cts = [[(t, t.shape[1], 0)] for t in attn_bwd(*att, d_ymla)]
    (d_qa, d_kva, d_kr), (gw['mla_q_norm'], gw['mla_kv_norm'], d_wq, d_wkv) = st_mla.bwd(mla_rows, mla_par, mla_cts)
    gw['mla_wq_b'], gw['mla_wkv_b'] = _unperm_wq(d_wq), _unperm_wkv(d_wkv)

    dproj = jnp.concatenate([d_qa, d_kva, d_lerp[:, :3072], d_zm, d_zr, d_gm, d_gr, d_lerp[:, 3072:], d_kr], axis=1)

    def pair_sums(name, ids):
        send = [jnp.stack([_unit_cat([_shards(n, gw[n])[j].astype(BF16) for n in UNITS[i]]) for j in range(4)])
                for i in ids]
        axes = [UNIT_AXIS[i] for i in ids]
        other = pair_exchange(name, send, axes)
        return [pair_sum(f"pair_sum_{i}", core1, s, o, ax) for i, s, o, ax in zip(ids, send, other, axes)]

    late, early = [0], list(range(1, len(UNITS)))
    pairs_e = pair_sums("pair_exchange_rest", early)
    gw_in, recv_e = matmul("mm_in_dw", dproj, h, "tn", BF16, side=scatter_side(pairs_e))
    gw['w_in'] = _unperm_w_in(gw_in)
    pairs_l = pair_sums("pair_exchange_w_in", late)
    sems, src_fly, land_fly, token = scatter_start("scatter_w_in_start", pairs_l)
    dh = matmul("mm_in_dx", dproj, wp, "nn", after=(token,))
    (grad_x,), (gw['g_pre'],) = st_pre.bwd(pre_rows, pre_par, [[(dh, D, 0)], [(dy, D, 0)]])

    big = [dict() for _ in range(4)]

    def update(name, ids, recv, pairs):
        mine = [sum4(f"sum4_{i}", chip1, r, p, UNIT_AXIS[i]) for i, r, p in zip(ids, recv, pairs)]
        theirs = swap_halves(name, mine)
        for i, mi, th in zip(ids, mine, theirs):
            res = adamw_halves(f"adamw_{i}", core1, *[_unit_cat([t[n] for n in UNITS[i]]) for t in (w, m, v)], mi, th,
                               UNIT_AXIS[i])
            for q in range(4):
                big[q].update(_unit_split(res[q], UNITS[i], 0))
        return res

    small_fly = allgather8_start("gather_small_start", _pack_small(gw, loss_blk[0, :1]))
    last = update("swap_halves_rest", early, recv_e, pairs_e)
    own_small, landed = allgather8_wait("gather_small_wait", *small_fly, last[0])
    dev = 2 * chip + core
    parts = [jnp.where(dev == i, own_small, landed[i]) for i in range(8)]
    small = adamw("adamw_small", _pack_small(w), _pack_small(m), _pack_small(v), parts)
    pairs_l, recv_l = scatter_wait("scatter_w_in_wait", sems, src_fly, land_fly, small[0] + last[0][:1, :1])
    update("swap_halves_w_in", late, recv_l, pairs_l)

    outs = []
    for b_d, s_arr in zip(big, small):
        d = {**b_d, **_unpack_small(s_arr)}
        d['w_in'] = d['w_in'].T
        outs.append([d[n] for n in WEIGHTS])
    loss = small[0][SMALL_LEN // LANE, 0]
    return (loss, grad_x.reshape(1, T, D), *outs[0], *outs[1], *outs[2], *outs[3])


def kernel(x, g_pre, w_in, mla_q_norm, mla_wq_b, mla_kv_norm, mla_wkv_b, rwkv_mu, rwkv_w0_f, rwkv_w2_f, rwkv_w0_b, rwkv_w2_b, rwkv_a0_f, rwkv_a2_f, rwkv_a0_b, rwkv_a2_b, rwkv_k_k, rwkv_k_a, rwkv_r_k, rwkv_gn_g, rwkv_gn_b, w_br_mla, w_br_rwkv, w_out, g_post, loss_target, m_g_pre, m_w_in, m_mla_q_norm, m_mla_wq_b, m_mla_kv_norm, m_mla_wkv_b, m_rwkv_mu, m_rwkv_w0_f, m_rwkv_w2_f, m_rwkv_w0_b, m_rwkv_w2_b, m_rwkv_a0_f, m_rwkv_a2_f, m_rwkv_a0_b, m_rwkv_a2_b, m_rwkv_k_k, m_rwkv_k_a, m_rwkv_r_k, m_rwkv_gn_g, m_rwkv_gn_b, m_w_br_mla, m_w_br_rwkv, m_w_out, m_g_post, v_g_pre, v_w_in, v_mla_q_norm, v_mla_wq_b, v_mla_kv_norm, v_mla_wkv_b, v_rwkv_mu, v_rwkv_w0_f, v_rwkv_w2_f, v_rwkv_w0_b, v_rwkv_w2_b, v_rwkv_a0_f, v_rwkv_a2_f, v_rwkv_a0_b, v_rwkv_a2_b, v_rwkv_k_k, v_rwkv_k_a, v_rwkv_r_k, v_rwkv_gn_g, v_rwkv_gn_b, v_w_br_mla, v_w_br_rwkv, v_w_out, v_g_post):
    given = dict(locals())
    w = {n: given[n] for n in WEIGHTS}
    m = {n: given['m_' + n] for n in WEIGHTS}
    v = {n: given['v_' + n] for n in WEIGHTS}
    return _step(x, loss_target, w, m, v)
```

```python
import functools
import math

import numpy as np
import jax
import jax.numpy as jnp
from jax import lax
from jax.experimental import pallas as pl
from jax.experimental.pallas import tpu as pltpu

F32, BF16 = jnp.float32, jnp.bfloat16
MESH_IDS = pl.DeviceIdType.MESH

D = 2048
T = 2048
HEADS = 8
Q_RANK = 512
KV_RANK = 512
NOPE = 128
ROPE = 64
VDIM = 128
RW = 1024
RH = 16
RN = 64
LORA = 96
D_IN = 10688
NORM_EPS = 1e-6
GN_EPS = 64e-5
ROPE_THETA = 10000.0
ADAM_LR, ADAM_B1, ADAM_B2, ADAM_EPS, ADAM_WD, ADAM_STEP = 0.001, 0.9, 0.999, 1e-08, 0.01, 10

LANE = 128
VMEM_BIG = 56 * 2**20

NP = 11008
OFF_QA, OFF_KVA, OFF_RKV, OFF_ZM, OFF_ZR, OFF_GM, OFF_GR, OFF_LORA, OFF_KR = 0, 512, 1024, 4096, 5120, 6144, 8192, 10240, 10752
NLERP = 3584

CHUNK = 64
NCH = T // CHUNK


def _dg(a, b, ca, cb, batch=False, prec=None):
    bd = ((0,), (0,)) if batch else ((), ())
    return lax.dot_general(a, b, (((ca,), (cb,)), bd), precision=prec, preferred_element_type=F32)


@jax.custom_vjp
def bdot(a, b):
    return _dg(a.astype(BF16), b.astype(BF16), 1, 0)


def _bdot_fwd(a, b):
    return bdot(a, b), (a, b)


def _bdot_bwd(res, g):
    a, b = res
    gb = g.astype(BF16)
    da = _dg(gb, b.astype(BF16), 1, 1)
    db = _dg(a.astype(BF16), gb, 0, 0)
    return da.astype(a.dtype), db.astype(b.dtype)


bdot.defvjp(_bdot_fwd, _bdot_bwd)


def _split(x):
    hi = x.astype(BF16)
    lo = (x - hi.astype(F32)).astype(BF16)
    return hi, lo


@jax.custom_vjp
def gsum(x, g2):
    hi, lo = _split(x)
    return _dg(hi, g2, 1, 0) + _dg(lo, g2, 1, 0)


def _gsum_fwd(x, g2):
    return gsum(x, g2), g2


def _gsum_bwd(g2, g):
    hi, lo = _split(g)
    return _dg(hi, g2, 1, 1) + _dg(lo, g2, 1, 1), jnp.zeros_like(g2)


gsum.defvjp(_gsum_fwd, _gsum_bwd)


def headsum(x, g2):
    return jnp.concatenate([gsum(x[:, i * LANE:(i + 1) * LANE], g2) for i in range(x.shape[1] // LANE)], axis=1)


def _terms(x, n):
    out = []
    for i in range(n):
        t = x.astype(BF16)
        out.append(t)
        if i < n - 1:
            x = x - t.astype(F32)
    return out


def _bmm(a, b, ca, cb, na, nb):
    acc = None
    for i, ai in enumerate(_terms(a, na)):
        for j, bj in enumerate(_terms(b, nb)):
            if i + j < max(na, nb):
                p = _dg(ai, bj, ca, cb, True)
                acc = p if acc is None else acc + p
    return acc


_NN, _NT, _TN = (2, 1), (2, 2), (1, 1)


def _make_dots(nf, nb_nn, nb_nt, nb_tn):
    @jax.custom_vjp
    def nn(a, b):
        return _bmm(a, b, *_NN, nf, nf)

    @jax.custom_vjp
    def nt(a, b):
        return _bmm(a, b, *_NT, nf, nf)

    @jax.custom_vjp
    def tn(a, b):
        return _bmm(a, b, *_TN, nf, nf)

    nn.defvjp(lambda a, b: (nn(a, b), (a, b)),
              lambda r, g: (_bmm(g, r[1], *_NT, nb_nn, nb_nn), _bmm(r[0], g, *_TN, nb_nn, nb_nn)))
    nt.defvjp(lambda a, b: (nt(a, b), (a, b)),
              lambda r, g: (_bmm(g, r[1], *_NN, 1, nb_nt), _bmm(g, r[0], *_TN, 1, nb_nt)))
    tn.defvjp(lambda a, b: (tn(a, b), (a, b)),
              lambda r, g: (_bmm(r[1], g, *_NT, nb_tn, nb_tn), _bmm(r[0], g, *_NN, nb_tn, nb_tn)))
    return nn, nt, tn


_SCAN_NF, _SCAN_NB = 1, 1
nn, nt, tn = _make_dots(_SCAN_NF, 1, 2, 1)


@jax.custom_vjp
def cumdot(ones, x):
    return _bmm(ones, x, *_NN, 1, 3)


cumdot.defvjp(lambda o, x: (cumdot(o, x), o), lambda o, g: (jnp.zeros_like(o), _bmm(o, g, *_TN, 1, 3)))


def _solve_powers(l):
    pw = [l]
    for _ in range(int(math.log2(l.shape[-1])) - 1):
        pw.append(_bmm(pw[-1], pw[-1], *_NN, _SCAN_NF, _SCAN_NF))
    return pw


@jax.custom_vjp
def tri_solve(l, rhs):
    x = rhs
    for p in _solve_powers(l):
        x = x + _bmm(p, x, *_NN, _SCAN_NF, _SCAN_NF)
    return x


def _tri_solve_fwd(l, rhs):
    pw = _solve_powers(l)
    x = rhs
    for p in pw:
        x = x + _bmm(p, x, *_NN, _SCAN_NF, _SCAN_NF)
    return x, (pw, x)


def _tri_solve_bwd(res, g):
    pw, x = res
    y = g
    for p in pw:
        y = y + _bmm(p, y, *_TN, _SCAN_NB, _SCAN_NB)
    return _bmm(y, x, *_NT, _SCAN_NB, _SCAN_NB), y


tri_solve.defvjp(_tri_solve_fwd, _tri_solve_bwd)


def _rms(x, g):
    return x * lax.rsqrt(jnp.mean(x * x, axis=-1, keepdims=True) + NORM_EPS) * g


def _softplus(x):
    pos = x > 0
    return jnp.where(pos, x, 0.0) + jnp.log(1.0 + jnp.exp(-jnp.where(pos, x, -x)))


def _silu(z):
    return z * jax.nn.sigmoid(z)


def _tile(n, cands):
    for c in cands:
        if n % c == 0:
            return c
    raise ValueError(n)


_MM_VMEM_BYTES = 32 * 2**20


def _mm_tiles(m, n, k):
    best = None
    for tm in (2048, 1024, 512, 256):
        for tn_ in (2048, 1024, 512, 256):
            for d in range(k // LANE, 0, -1):
                tk = LANE * d
                if m % tm or n % tn_ or k % tk:
                    continue
                nk = k // tk
                vmem = 4 * tk * (tm + tn_) + 8 * tm * tn_ + (4 * tm * tn_ if nk > 1 else 0)
                if vmem > _MM_VMEM_BYTES:
                    continue
                a_reads = n // tn_ if nk > 1 else 1
                b_reads = 1 if (nk == 1 and n == tn_) else m // tm
                acc_rmw = nk * m * n if nk > 1 else 0
                cost = (a_reads * m * k + b_reads * k * n + acc_rmw, -tm * tn_ * tk)
                if best is None or cost < best[0]:
                    best = (cost, (tm, tn_, tk))
    return best[1]


class Side:
    def __init__(self, ins, outs, sems, start, finish, mid=None):
        self.ins, self.outs, self.sems, self.start, self.finish, self.mid = ins, outs, sems, start, finish, mid

    def at_step(self, step, steps, *refs):
        @pl.when(step == 0)
        def _():
            self.start(*refs)

        if self.mid is not None and steps > 2:
            @pl.when(step == steps // 2)
            def _():
                self.mid(*refs)

    def at_end(self, step, steps, *refs):
        @pl.when(step == steps - 1)
        def _():
            if self.mid is not None and steps <= 2:
                self.mid(*refs)
            self.finish(*refs)

    def run(self, *refs):
        self.start(*refs)
        if self.mid is not None:
            self.mid(*refs)
        self.finish(*refs)


def matmul(name, a, b, mode, out_dtype=F32, side=None, after=()):
    if mode == "nn":
        (m, k), n = a.shape, b.shape[1]
    elif mode == "nt":
        (m, k), n = a.shape, b.shape[0]
    else:
        (k, m), n = a.shape, b.shape[1]
    tm, tn_, tk = _mm_tiles(m, n, k)
    nk = k // tk
    if mode == "nn":
        a_spec = pl.BlockSpec((tm, tk), lambda i, j, kk: (i, kk))
        b_spec = pl.BlockSpec((tk, tn_), lambda i, j, kk: (kk, j))
        ca, cb = 1, 0
    elif mode == "nt":
        a_spec = pl.BlockSpec((tm, tk), lambda i, j, kk: (i, kk))
        b_spec = pl.BlockSpec((tn_, tk), lambda i, j, kk: (j, kk))
        ca, cb = 1, 1
    else:
        a_spec = pl.BlockSpec((tk, tm), lambda i, j, kk: (kk, i))
        b_spec = pl.BlockSpec((tk, tn_), lambda i, j, kk: (kk, j))
        ca, cb = 0, 0

    grid = (m // tm, n // tn_, nk)
    n_in = len(side.ins) if side else 0
    n_out = len(side.outs) if side else 0
    n_dep = len(after)

    def body(a_ref, b_ref, *rest):
        rest = rest[n_dep:]
        s_ins, o_ref, s_outs = rest[:n_in], rest[n_in], rest[n_in + 1:n_in + 1 + n_out]
        scratch = rest[n_in + 1 + n_out:]
        acc, s_sems = (scratch[:1], scratch[1:]) if nk > 1 else ((), scratch)
        steps = grid[0] * grid[1] * grid[2]
        if side:
            step = (pl.program_id(0) * grid[1] + pl.program_id(1)) * grid[2] + pl.program_id(2)
            side.at_step(step, steps, s_ins, s_outs, s_sems)

        part = _dg(a_ref[...].astype(BF16), b_ref[...].astype(BF16), ca, cb)
        if nk == 1:
            o_ref[...] = part.astype(o_ref.dtype)
        else:
            acc_ref, kk = acc[0], pl.program_id(2)

            @pl.when(kk == 0)
            def _():
                acc_ref[...] = part

            @pl.when(kk > 0)
            def _():
                acc_ref[...] += part

            @pl.when(kk == nk - 1)
            def _():
                o_ref[...] = acc_ref[...].astype(o_ref.dtype)

        if side:
            side.at_end(step, steps, s_ins, s_outs, s_sems)

    res = pl.pallas_call(
        body, name=name, grid=grid,
        in_specs=[a_spec, b_spec] + [_ANY] * (n_dep + n_in),
        out_specs=[pl.BlockSpec((tm, tn_), lambda i, j, kk: (i, j))] + [_ANY] * n_out,
        out_shape=[jax.ShapeDtypeStruct((m, n), out_dtype)] + (list(side.outs) if side else []),
        scratch_shapes=([pltpu.VMEM((tm, tn_), F32)] if nk > 1 else []) + (list(side.sems) if side else []),
        compiler_params=pltpu.CompilerParams(
            dimension_semantics=("arbitrary",) * 3 if side else ("parallel", "parallel", "arbitrary"),
            vmem_limit_bytes=VMEM_BIG),
    )(a, b, *after, *(side.ins if side else []))
    return (res[0], res[1:]) if side else res[0]


def _rspec(tr, width, blk):
    return pl.BlockSpec((tr, width), lambda i: (i, blk))


def _full_spec(arr):
    return pl.BlockSpec(arr.shape, lambda i: (0,) * arr.ndim)


class Stage:
    def __init__(self, name, f, outs, tr, diff_rows, diff_params, drow_dtypes):
        self.name, self.f, self.outs, self.tr = name, f, outs, tr
        self.diff_rows, self.diff_params, self.drow_dtypes = diff_rows, diff_params, drow_dtypes

    def fwd(self, rows, params):
        f, nr, npar = self.f, len(rows), len(params)
        stored = [(w, dt) for (w, dt) in self.outs if dt is not None]
        keep = [i for i, (w, dt) in enumerate(self.outs) if dt is not None]

        def body(*refs):
            vals = f(*[r[...].astype(F32) for r in refs[:nr]], *[p[...] for p in refs[nr:nr + npar]])
            for o_ref, i in zip(refs[nr + npar:], keep):
                o_ref[...] = vals[i].astype(o_ref.dtype)

        return pl.pallas_call(
            body, name=self.name + "_fwd", grid=(T // self.tr,),
            in_specs=[_rspec(self.tr, w, b) for (_, w, b) in rows] + [_full_spec(p) for p in params],
            out_specs=[_rspec(self.tr, w, 0) for (w, _) in stored],
            out_shape=[jax.ShapeDtypeStruct((T, w), dt) for (w, dt) in stored],
            compiler_params=pltpu.CompilerParams(dimension_semantics=("arbitrary",), vmem_limit_bytes=VMEM_BIG),
        )(*[r[0] for r in rows], *params)

    def bwd(self, rows, params, cts):
        f, nr, npar = self.f, len(rows), len(params)
        dr_idx, dp_idx = self.diff_rows, self.diff_params
        flat_cts = [c for lst in cts for c in lst]
        nct = len(flat_cts)

        def body(*refs):
            row_refs, par_refs = refs[:nr], refs[nr:nr + npar]
            ct_refs = refs[nr + npar:nr + npar + nct]
            drow_refs = refs[nr + npar + nct:nr + npar + nct + len(dr_idx)]
            dpar_refs = refs[nr + npar + nct + len(dr_idx):]
            row_vals = [r[...].astype(F32) for r in row_refs]
            par_vals = [p[...] for p in par_refs]

            def g(*dv):
                rv, pv = list(row_vals), list(par_vals)
                for j, i in enumerate(dr_idx):
                    rv[i] = dv[j]
                for j, i in enumerate(dp_idx):
                    pv[i] = dv[len(dr_idx) + j]
                return f(*rv, *pv)

            _, vjp = jax.vjp(g, *[row_vals[i] for i in dr_idx], *[par_vals[i] for i in dp_idx])
            ct_vals, pos = [], 0
            for lst in cts:
                acc = ct_refs[pos][...].astype(F32)
                for q in range(1, len(lst)):
                    acc = acc + ct_refs[pos + q][...].astype(F32)
                pos += len(lst)
                ct_vals.append(acc)
            grads = vjp(tuple(ct_vals))
            for j, r in enumerate(drow_refs):
                r[...] = grads[j].astype(r.dtype)

            @pl.when(pl.program_id(0) == 0)
            def _():
                for r in dpar_refs:
                    r[...] = jnp.zeros_like(r)

            for j, r in enumerate(dpar_refs):
                r[...] += grads[len(dr_idx) + j].astype(F32)

        drow_shapes = [jax.ShapeDtypeStruct((T, rows[i][1]), dt) for i, dt in zip(dr_idx, self.drow_dtypes)]
        dpar_shapes = [jax.ShapeDtypeStruct(params[i].shape, F32) for i in dp_idx]
        res = pl.pallas_call(
            body, name=self.name + "_bwd", grid=(T // self.tr,),
            in_specs=[_rspec(self.tr, w, b) for (_, w, b) in rows] + [_full_spec(p) for p in params]
            + [_rspec(self.tr, w, b) for (_, w, b) in flat_cts],
            out_specs=[_rspec(self.tr, rows[i][1], 0) for i in dr_idx] + [_full_spec(params[i]) for i in dp_idx],
            out_shape=drow_shapes + dpar_shapes,
            compiler_params=pltpu.CompilerParams(dimension_semantics=("arbitrary",), vmem_limit_bytes=VMEM_BIG),
        )(*[r[0] for r in rows], *params, *[c[0] for c in flat_cts])
        return res[:len(dr_idx)], res[len(dr_idx):]


def f_pre(x, g):
    return _rms(x, g), x


@jax.custom_vjp
def swap32(t):
    width = t.shape[1]
    lane = lax.broadcasted_iota(jnp.int32, t.shape, 1) % LANE
    return jnp.where(lane < 32, pltpu.roll(t, width - 32, 1), jnp.where(lane < 64, pltpu.roll(t, 32, 1), 0.0))


swap32.defvjp(lambda t: (swap32(t), None), lambda _, g: (swap32(g),))


def f_mla(q_a, kv_a, kr, cq, sq, ck, sk, gq, gkv, wq, wkv):
    q = bdot(_rms(q_a, gq), wq)
    kv = bdot(_rms(kv_a, gkv), wkv)
    t, k = q[:, 1024:], kr[:, :LANE]
    return (q[:, :1024], t * cq + swap32(t) * sq, kv[:, :1024], k * ck + swap32(k) * sk, kv[:, 1024:])


def f_rwkv_pre(lerp, w0f, w0b, a0f, a0b, kkw, kaw, w2f, w2b, a2f, a2b, g2):
    r, k, v = lerp[:, :RW], lerp[:, RW:2 * RW], lerp[:, 2 * RW:3 * RW]
    wdf, wdb, adf, adb = (lerp[:, 3 * RW + i * LANE:3 * RW + (i + 1) * LANE] for i in range(4))

    def logdecay(w0, wd, w2):
        z = w0 + bdot(jnp.tanh(wd), w2)
        return -jnp.exp(-_softplus(-z) - 0.5)

    a_f = jax.nn.sigmoid(a0f + bdot(adf, a2f))
    a_b = jax.nn.sigmoid(a0b + bdot(adb, a2b))
    kk = k * kkw
    kk = kk / jnp.maximum(jnp.sqrt(headsum(kk * kk, g2)), 1e-12)
    return (r, v, logdecay(w0f, wdf, w2f), logdecay(w0b, wdb, w2b),
            k * (1.0 + (a_f - 1.0) * kaw), k * (1.0 + (a_b - 1.0) * kaw), -kk, kk * a_f, kk * a_b)


def f_rwkv_post(yf, yb, r, kf, kb, v, z, gng, gnb, rk, g2):
    y = yf + yb
    mu = headsum(y, g2) * (1.0 / RN)
    d = y - mu
    var = headsum(d * d, g2) * (1.0 / RN)
    yn = d * lax.rsqrt(var + GN_EPS) * gng + gnb
    bonus = headsum(r * (kf + kb) * rk, g2) * v
    return ((yn + bonus) * _silu(z),)


def f_gate(y, z):
    return (y * _silu(z),)


def f_merge(um, ur, gm, gr):
    return (jax.nn.sigmoid(gm) * um + jax.nn.sigmoid(gr) * ur,)


_SHIFT_W = 256


def _lerp_colblock(j):
    return jnp.where(j < 3072 // _SHIFT_W, OFF_RKV // _SHIFT_W + j, OFF_LORA // _SHIFT_W + j - 3072 // _SHIFT_W)


def _nbr_mean(x):
    row = lax.broadcasted_iota(jnp.int32, x.shape, 0)
    up = jnp.where(row == 0, 0.0, pltpu.roll(x, 1, 0))
    dn = jnp.where(row == T - 1, 0.0, pltpu.roll(x, T - 1, 0))
    return 0.5 * (up + dn)


def shift_fwd(proj, mu):
    def body(x_ref, mu_ref, o_ref):
        x = x_ref[...]
        o_ref[...] = x + mu_ref[...] * (_nbr_mean(x) - x)

    return pl.pallas_call(
        body, name="shift_fwd", grid=(NLERP // _SHIFT_W,),
        in_specs=[pl.BlockSpec((T, _SHIFT_W), lambda j: (0, _lerp_colblock(j))),
                  pl.BlockSpec((1, _SHIFT_W), lambda j: (0, j))],
        out_specs=pl.BlockSpec((T, _SHIFT_W), lambda j: (0, j)),
        out_shape=jax.ShapeDtypeStruct((T, NLERP), F32),
        compiler_params=pltpu.CompilerParams(dimension_semantics=("parallel",), vmem_limit_bytes=VMEM_BIG),
    )(proj, mu)


def shift_bwd(proj, mu, g):
    def body(x_ref, mu_ref, g_ref, dx_ref, dmu_ref):
        x, gv = x_ref[...], g_ref[...]
        dmu_ref[...] = jnp.sum(gv * (_nbr_mean(x) - x), axis=0, keepdims=True)
        gm = gv * mu_ref[...]
        dx_ref[...] = (gv - gm + _nbr_mean(gm)).astype(dx_ref.dtype)

    col = pl.BlockSpec((T, _SHIFT_W), lambda j: (0, j))
    vec = pl.BlockSpec((1, _SHIFT_W), lambda j: (0, j))
    return pl.pallas_call(
        body, name="shift_bwd", grid=(NLERP // _SHIFT_W,),
        in_specs=[pl.BlockSpec((T, _SHIFT_W), lambda j: (0, _lerp_colblock(j))), vec, col],
        out_specs=[col, vec],
        out_shape=[jax.ShapeDtypeStruct((T, NLERP), BF16), jax.ShapeDtypeStruct((1, NLERP), F32)],
        compiler_params=pltpu.CompilerParams(dimension_semantics=("parallel",), vmem_limit_bytes=VMEM_BIG),
    )(proj, mu, g)


_TQ_F, _TQ_B = 256, 512
_ATT_SCALE = (NOPE + ROPE) ** -0.5


def _probs(q, k):
    s = _dg(q, k, 1, 1) * _ATT_SCALE
    e = jnp.exp(s - jnp.max(s, axis=-1, keepdims=True))
    return e * (1.0 / jnp.sum(e, axis=-1, keepdims=True))


def _q_blk(tq):
    return pl.BlockSpec((tq, LANE), lambda h, i: (i, h))


_K_BLK = pl.BlockSpec((T, LANE), lambda h, i: (0, h))
_KR_BLK = pl.BlockSpec((T, LANE), lambda h, i: (0, 0))


def _load_qk(qn_ref, qr_ref, kn_ref, kr_ref, kcat_ref):
    @pl.when(pl.program_id(1) == 0)
    def _():
        kcat_ref[:, :LANE] = kn_ref[...]
        kcat_ref[:, LANE:] = kr_ref[...]

    return jnp.concatenate([qn_ref[...], qr_ref[...]], axis=1), kcat_ref[...]


def attn_fwd(qn, qr, kn, kr, v, side):
    n_in, n_out, steps = len(side.ins), len(side.outs), HEADS * (T // _TQ_F)

    def body(qn_ref, qr_ref, kn_ref, kr_ref, v_ref, *rest):
        s_ins, o_ref, s_outs = rest[:n_in], rest[n_in], rest[n_in + 1:n_in + 1 + n_out]
        kcat_ref, s_sems = rest[n_in + 1 + n_out], rest[n_in + 2 + n_out:]
        step = pl.program_id(0) * (T // _TQ_F) + pl.program_id(1)
        side.at_step(step, steps, s_ins, s_outs, s_sems)
        q, k = _load_qk(qn_ref, qr_ref, kn_ref, kr_ref, kcat_ref)
        o_ref[...] = _dg(_probs(q, k).astype(BF16), v_ref[...], 1, 0)
        side.at_end(step, steps, s_ins, s_outs, s_sems)

    res = pl.pallas_call(
        body, name="attn_fwd", grid=(HEADS, T // _TQ_F),
        in_specs=[_q_blk(_TQ_F), _q_blk(_TQ_F), _K_BLK, _KR_BLK, _K_BLK] + [_ANY] * n_in,
        out_specs=[_q_blk(_TQ_F)] + [_ANY] * n_out,
        out_shape=[jax.ShapeDtypeStruct((T, HEADS * VDIM), F32)] + list(side.outs),
        scratch_shapes=[pltpu.VMEM((T, 2 * LANE), BF16)] + list(side.sems),
        compiler_params=pltpu.CompilerParams(dimension_semantics=("arbitrary", "arbitrary"), vmem_limit_bytes=VMEM_BIG),
    )(qn, qr, kn, kr, v, *side.ins)
    return res[0], res[1:]


def attn_bwd(qn, qr, kn, kr, v, do):
    def body(qn_ref, qr_ref, kn_ref, kr_ref, v_ref, do_ref, dqn_ref, dqr_ref, dkn_ref, dkr_ref, dv_ref, kcat_ref):
        h, i = pl.program_id(0), pl.program_id(1)

        @pl.when(i == 0)
        def _():
            dkn_ref[...] = jnp.zeros_like(dkn_ref)
            dv_ref[...] = jnp.zeros_like(dv_ref)

        @pl.when((i == 0) & (h == 0))
        def _():
            dkr_ref[...] = jnp.zeros_like(dkr_ref)

        q, k = _load_qk(qn_ref, qr_ref, kn_ref, kr_ref, kcat_ref)
        dob = do_ref[...].astype(BF16)
        p = _probs(q, k)
        dv_ref[...] += _dg(p.astype(BF16), dob, 0, 0)
        dp = _dg(dob, v_ref[...], 1, 1)
        ds = (p * (dp - jnp.sum(dp * p, axis=-1, keepdims=True)) * _ATT_SCALE).astype(BF16)
        dq = _dg(ds, k, 1, 0)
        dqn_ref[...] = dq[:, :LANE]
        dqr_ref[...] = dq[:, LANE:]
        dk = _dg(ds, q, 0, 0)
        dkn_ref[...] += dk[:, :LANE]
        dkr_ref[...] += dk[:, LANE:]

    wide = jax.ShapeDtypeStruct((T, HEADS * LANE), F32)
    return pl.pallas_call(
        body, name="attn_bwd", grid=(HEADS, T // _TQ_B),
        in_specs=[_q_blk(_TQ_B), _q_blk(_TQ_B), _K_BLK, _KR_BLK, _K_BLK, _q_blk(_TQ_B)],
        out_specs=[_q_blk(_TQ_B), _q_blk(_TQ_B), _K_BLK, _KR_BLK, _K_BLK],
        out_shape=[wide, wide, wide, jax.ShapeDtypeStruct((T, LANE), F32), wide],
        scratch_shapes=[pltpu.VMEM((T, 2 * LANE), BF16)],
        compiler_params=pltpu.CompilerParams(dimension_semantics=("arbitrary", "arbitrary"), vmem_limit_bytes=VMEM_BIG),
    )(qn, qr, kn, kr, v, do)


def _chunk(r, lw, k, v, a, b, ht, *, reverse):
    hb, c, _ = r.shape
    ti = lax.broadcasted_iota(jnp.int32, (c, c), 0)
    si = lax.broadcasted_iota(jnp.int32, (c, c), 1)
    incl = (si >= ti) if reverse else (si <= ti)
    strict = (si > ti) if reverse else (si < ti)
    ones = jnp.broadcast_to(incl.astype(F32)[None], (hb, c, c))
    cum = cumdot(ones, lw)
    cum_ex = cum - lw
    tot = jnp.sum(lw, axis=1, keepdims=True)
    mid = 0.5 * tot
    rt, at = r * jnp.exp(cum - mid), a * jnp.exp(cum_ex - mid)
    einv = jnp.exp(mid - cum)
    kt, bt = k * einv, b * einv
    m_ab = jnp.where(strict, nt(at, bt), 0.0)
    m_ak = jnp.where(strict, nt(at, kt), 0.0)
    m_rb = jnp.where(incl, nt(rt, bt), 0.0)
    m_rk = jnp.where(incl, nt(rt, kt), 0.0)
    u = tri_solve(m_ab, nt(a * jnp.exp(cum_ex), ht) + nn(m_ak, v))
    y = nt(r * jnp.exp(cum), ht) + nn(m_rb, u) + nn(m_rk, v)
    eend = jnp.exp(tot - cum)
    ht_new = ht * jnp.exp(tot) + tn(u, b * eend) + tn(v, k * eend)
    return y, ht_new


_HB_F, _HB_B = 16, 16


def _split_heads(x):
    return jnp.stack([x[:, i * RN:(i + 1) * RN] for i in range(x.shape[1] // RN)])


def _merge_heads(y):
    return jnp.concatenate([y[i] for i in range(y.shape[0])], axis=1)


def _chunk_map(reverse, backward):
    flip = reverse != backward
    return (lambda g, c: (NCH - 1 - c, g)) if flip else (lambda g, c: (c, g))


def scan_fwd(name, r, lw, k, v, a, b, reverse):
    hb = _HB_F
    cmap = _chunk_map(reverse, False)

    def body(r_ref, lw_ref, k_ref, v_ref, a_ref, b_ref, y_ref, h0_ref, ht_ref):
        @pl.when(pl.program_id(1) == 0)
        def _():
            ht_ref[...] = jnp.zeros_like(ht_ref)

        ht = ht_ref[...]
        h0_ref[0] = ht
        ins = [_split_heads(x[...]) for x in (r_ref, lw_ref, k_ref, v_ref, a_ref, b_ref)]
        y, hn = _chunk(*ins, ht, reverse=reverse)
        y_ref[...] = _merge_heads(y)
        ht_ref[...] = hn

    io = pl.BlockSpec((CHUNK, hb * RN), cmap)
    return pl.pallas_call(
        body, name=name, grid=(RH // hb, NCH),
        in_specs=[io] * 6,
        out_specs=[io, pl.BlockSpec((1, hb, RN, RN), lambda g, c: (cmap(g, c)[0], g, 0, 0))],
        out_shape=[jax.ShapeDtypeStruct((T, RW), F32), jax.ShapeDtypeStruct((NCH, RH, RN, RN), F32)],
        scratch_shapes=[pltpu.VMEM((hb, RN, RN), F32)],
        compiler_params=pltpu.CompilerParams(dimension_semantics=("parallel", "arbitrary"), vmem_limit_bytes=VMEM_BIG),
    )(r, lw, k, v, a, b)


def scan_bwd(name, r, lw, k, v, a, b, h0, dy, reverse):
    hb = _HB_B
    cmap = _chunk_map(reverse, True)

    def body(r_ref, lw_ref, k_ref, v_ref, a_ref, b_ref, h0_ref, dy_ref, *rest):
        d_refs, dht_ref = rest[:6], rest[6]

        @pl.when(pl.program_id(1) == 0)
        def _():
            dht_ref[...] = jnp.zeros_like(dht_ref)

        ins = [_split_heads(x[...]) for x in (r_ref, lw_ref, k_ref, v_ref, a_ref, b_ref)]
        _, vjp = jax.vjp(functools.partial(_chunk, reverse=reverse), *ins, h0_ref[0])
        grads = vjp((_split_heads(dy_ref[...]), dht_ref[...]))
        for d_ref, gval in zip(d_refs, grads[:6]):
            d_ref[...] = _merge_heads(gval).astype(d_ref.dtype)
        dht_ref[...] = grads[6]

    io = pl.BlockSpec((CHUNK, hb * RN), cmap)
    return pl.pallas_call(
        body, name=name, grid=(RH // hb, NCH),
        in_specs=[io] * 6 + [pl.BlockSpec((1, hb, RN, RN), lambda g, c: (cmap(g, c)[0], g, 0, 0)), io],
        out_specs=[io] * 6,
        out_shape=[jax.ShapeDtypeStruct((T, RW), F32 if i == 1 else BF16) for i in range(6)],
        scratch_shapes=[pltpu.VMEM((hb, RN, RN), F32)],
        compiler_params=pltpu.CompilerParams(dimension_semantics=("parallel", "arbitrary"), vmem_limit_bytes=VMEM_BIG),
    )(r, lw, k, v, a, b, h0, dy)


def loss_stage(out, x2, tgt, g_post):
    tr = 256

    def body(o_ref, x_ref, t_ref, g_ref, do_ref, dy_ref, dg_ref, loss_ref):
        @pl.when(pl.program_id(0) == 0)
        def _():
            dg_ref[...] = jnp.zeros_like(dg_ref)
            loss_ref[...] = jnp.zeros_like(loss_ref)

        nrm, vjp = jax.vjp(_rms, o_ref[...], g_ref[...])
        e = x_ref[...] + nrm - t_ref[...]
        s = jnp.sum(jnp.sum(e * e, axis=1, keepdims=True), axis=0, keepdims=True)
        loss_ref[...] += jnp.broadcast_to(s * (0.5 / D), loss_ref.shape)
        dy = e * (1.0 / D)
        do, dg = vjp(dy)
        do_ref[...] = do.astype(do_ref.dtype)
        dy_ref[...] = dy
        dg_ref[...] += dg

    row = pl.BlockSpec((tr, D), lambda i: (i, 0))
    return pl.pallas_call(
        body, name="loss_stage", grid=(T // tr,),
        in_specs=[row, row, row, pl.BlockSpec((1, D), lambda i: (0, 0))],
        out_specs=[row, row, pl.BlockSpec((1, D), lambda i: (0, 0)), pl.BlockSpec((8, LANE), lambda i: (0, 0))],
        out_shape=[jax.ShapeDtypeStruct((T, D), BF16), jax.ShapeDtypeStruct((T, D), F32),
                   jax.ShapeDtypeStruct((1, D), F32), jax.ShapeDtypeStruct((8, LANE), F32)],
        compiler_params=pltpu.CompilerParams(dimension_semantics=("arbitrary",), vmem_limit_bytes=VMEM_BIG),
    )(out, x2, tgt, g_post)


_EW_BLOCK_BYTES = 1 << 20


def _row_tile(rows, cols):
    best = None
    for tr in range(16, rows + 1, 16):
        if rows % tr == 0 and tr * cols * 4 <= _EW_BLOCK_BYTES:
            best = tr
    return best or rows


def _axis_tile(shape, axis, words):
    rows, cols = shape
    n, other, unit = (rows, cols, 16) if axis == 0 else (cols, rows, LANE)
    best = unit if n % unit == 0 else n
    for t in range(unit, n + 1, unit):
        if n % t == 0 and t * other * words * 4 <= _EW_BLOCK_BYTES:
            best = t
    blk = (best, cols) if axis == 0 else (rows, best)
    at = (lambda s: (s, 0)) if axis == 0 else (lambda s: (0, s))
    return blk, n // best, at


def _adamw_update(g, w_ref, m_ref, v_ref, g_ref, d_ref, nm_ref, nv_ref):
    mm = ADAM_B1 * m_ref[...] + (1.0 - ADAM_B1) * g
    vv = ADAM_B2 * v_ref[...] + (1.0 - ADAM_B2) * (g * g)
    m_hat = mm / (1.0 - ADAM_B1 ** ADAM_STEP)
    v_hat = vv / (1.0 - ADAM_B2 ** ADAM_STEP)
    g_ref[...] = g
    d_ref[...] = -ADAM_LR * (m_hat / (jnp.sqrt(v_hat) + ADAM_EPS) + ADAM_WD * w_ref[...])
    nm_ref[...] = mm
    nv_ref[...] = vv


def adamw(name, w, m, v, parts):
    rows, cols = w.shape
    br = _row_tile(rows, cols)
    npart = len(parts)

    def body(w_ref, m_ref, v_ref, *rest):
        g = rest[0][...].astype(F32)
        for p in rest[1:npart]:
            g = g + p[...].astype(F32)
        _adamw_update(g, w_ref, m_ref, v_ref, *rest[npart:])

    blk = pl.BlockSpec((br, cols), lambda i: (i, 0))
    return pl.pallas_call(
        body, name=name, grid=(rows // br,),
        in_specs=[blk] * (3 + npart), out_specs=[blk] * 4,
        out_shape=[jax.ShapeDtypeStruct((rows, cols), F32)] * 4,
        compiler_params=pltpu.CompilerParams(dimension_semantics=("parallel",), vmem_limit_bytes=VMEM_BIG),
    )(w, m, v, *parts)


def adamw_halves(name, place, w, m, v, mine, theirs, axis):
    half_shape = mine.shape
    blk_shape, nb, at = _axis_tile(half_shape, axis, 1)

    def body(p_ref, w_ref, m_ref, v_ref, a_ref, b_ref, *outs):
        own = (pl.program_id(0) // nb) == p_ref[0]
        _adamw_update(jnp.where(own, a_ref[...], b_ref[...]), w_ref, m_ref, v_ref, *outs)

    blk = pl.BlockSpec(blk_shape, lambda i, p: at(i))
    half = pl.BlockSpec(blk_shape, lambda i, p: at(i % nb))
    return pl.pallas_call(
        body, name=name,
        grid_spec=pltpu.PrefetchScalarGridSpec(num_scalar_prefetch=1, grid=(2 * nb,),
                                               in_specs=[blk] * 3 + [half] * 2, out_specs=[blk] * 4),
        out_shape=[jax.ShapeDtypeStruct(w.shape, F32)] * 4,
        compiler_params=pltpu.CompilerParams(dimension_semantics=("arbitrary",), vmem_limit_bytes=VMEM_BIG),
    )(place, w, m, v, mine, theirs)


def pair_sum(name, place, send, other, axis):
    blk_shape, nb, at = _axis_tile(other.shape[1:], axis, 4)

    def body(p_ref, a_ref, b_ref, o_ref):
        o_ref[...] = (a_ref[...].astype(F32) + b_ref[...].astype(F32)).astype(o_ref.dtype)

    blk = pl.BlockSpec((4,) + blk_shape, lambda i, p: (0,) + at(i))
    mine = pl.BlockSpec((4,) + blk_shape, lambda i, p: (0,) + at(p[0] * nb + i))
    return pl.pallas_call(
        body, name=name,
        grid_spec=pltpu.PrefetchScalarGridSpec(num_scalar_prefetch=1, grid=(nb,), in_specs=[mine, blk], out_specs=blk),
        out_shape=jax.ShapeDtypeStruct(other.shape, BF16),
        compiler_params=pltpu.CompilerParams(dimension_semantics=("arbitrary",), vmem_limit_bytes=VMEM_BIG),
    )(place, send, other)


def sum4(name, place, recv, own, axis):
    blk_shape, nb, at = _axis_tile(recv.shape[1:], axis, 4)

    def body(p_ref, r_ref, s_ref, o_ref):
        me = p_ref[0]
        t = [jnp.where(me == j, s_ref[j], r_ref[j]).astype(F32) for j in range(4)]
        o_ref[...] = ((t[0] + t[1]) + t[2]) + t[3]

    blk = pl.BlockSpec((4,) + blk_shape, lambda i, p: (0,) + at(i))
    return pl.pallas_call(
        body, name=name,
        grid_spec=pltpu.PrefetchScalarGridSpec(num_scalar_prefetch=1, grid=(nb,), in_specs=[blk, blk],
                                               out_specs=pl.BlockSpec(blk_shape, lambda i, p: at(i))),
        out_shape=jax.ShapeDtypeStruct(recv.shape[1:], F32),
        compiler_params=pltpu.CompilerParams(dimension_semantics=("arbitrary",), vmem_limit_bytes=VMEM_BIG),
    )(place, recv, own)


_ANY = pl.BlockSpec(memory_space=pl.ANY)


def _place():
    x, y, c = lax.axis_index("x"), lax.axis_index("y"), lax.axis_index("c")
    return x, y, c, 2 * x + y


def _chip_peers(x, y):
    out = []
    for k in (1, 2, 3):
        px = 1 - x if k & 2 else x
        py = 1 - y if k & 1 else y
        out.append((k, px, py, 2 * px + py))
    return out


def _half(c, shape, axis):
    n = shape[axis] // 2
    sl = pl.ds(pl.multiple_of(c * n, 16 if axis == 0 else LANE), n)
    return (sl,) if axis == 0 else (pl.ds(0, shape[0]), sl)


def gather_weights(srcs, axes):
    side = gather_side(srcs, axes)
    n = len(srcs)

    def body(*refs):
        ins, outs, sems = refs[:n], refs[n:2 * n], refs[2 * n:]
        side.run(ins, outs, sems)

    return pl.pallas_call(
        body, name="gather_weights", in_specs=[_ANY] * n, out_specs=[_ANY] * n,
        out_shape=side.outs, scratch_shapes=side.sems,
    )(*srcs)


def gather_side(srcs, axes):
    n = len(srcs)

    def copies(src, dst, sems, want):
        ssem, rsem, fssem, frsem = sems
        x, y, c, me = _place()
        sib = (x, y, 1 - c)
        out = []
        for i in range(n):
            mine, other = _half(c, srcs[i].shape, axes[i]), _half(1 - c, srcs[i].shape, axes[i])
            for k, px, py, peer in _chip_peers(x, y):
                sems_k = dict(send_sem=ssem.at[i, k - 1], recv_sem=rsem.at[i, k - 1], device_id=(px, py, c),
                              device_id_type=MESH_IDS)
                fsems = dict(send_sem=fssem.at[i, k - 1], recv_sem=frsem.at[i, k - 1], device_id=sib,
                             device_id_type=MESH_IDS)
                got = dst[i].at[(peer,) + mine]
                mk = pltpu.make_async_remote_copy
                made = dict(
                    snd=lambda: mk(src_ref=src[i].at[mine], dst_ref=dst[i].at[(me,) + mine], **sems_k),
                    rcv=lambda: mk(src_ref=src[i].at[mine], dst_ref=got, **sems_k),
                    fwd=lambda: mk(src_ref=got, dst_ref=got, **fsems),
                    frcv=lambda: mk(src_ref=got, dst_ref=dst[i].at[(peer,) + other], **fsems))
                out.append([made[w]() for w in want])
        return out

    def start(src, dst, sems):
        for (snd,) in copies(src, dst, sems, ("snd",)):
            snd.start()

    def mid(src, dst, sems):
        for rcv, fwd in copies(src, dst, sems, ("rcv", "fwd")):
            rcv.wait_recv()
            fwd.start()

    def finish(src, dst, sems):
        for snd, fwd, frcv in copies(src, dst, sems, ("snd", "fwd", "frcv")):
            frcv.wait_recv()
            snd.wait_send()
            fwd.wait_send()

    return Side(list(srcs), [jax.ShapeDtypeStruct((4,) + s.shape, s.dtype) for s in srcs],
                [pltpu.SemaphoreType.DMA((n, 3))] * 4, start, finish, mid)


def pair_exchange(name, srcs, axes):
    n = len(srcs)

    def half_shape(s, axis):
        return (4, s.shape[1] // 2, s.shape[2]) if axis == 0 else (4, s.shape[1], s.shape[2] // 2)

    def body(*refs):
        src, other = refs[:n], refs[n:2 * n]
        ssem, rsem = refs[2 * n:]
        x, y, c, _ = _place()
        cps = []
        for i in range(n):
            idx = (pl.ds(0, 4),) + _half(1 - c, srcs[i].shape[1:], axes[i])
            cps.append(pltpu.make_async_remote_copy(
                src_ref=src[i].at[idx], dst_ref=other[i], send_sem=ssem.at[i], recv_sem=rsem.at[i],
                device_id=(x, y, 1 - c), device_id_type=MESH_IDS))
            cps[-1].start()
        for cp in cps:
            cp.wait()

    return pl.pallas_call(
        body, name=name, in_specs=[_ANY] * n, out_specs=[_ANY] * n,
        out_shape=[jax.ShapeDtypeStruct(half_shape(s, a), s.dtype) for s, a in zip(srcs, axes)],
        scratch_shapes=[pltpu.SemaphoreType.DMA((n,))] * 2,
    )(*srcs)


def scatter_grads(srcs):
    side = scatter_side(srcs)
    n = len(srcs)

    def body(*refs):
        ins, outs, sems = refs[:n], refs[n:2 * n], refs[2 * n:]
        side.run(ins, outs, sems)

    return pl.pallas_call(
        body, name="scatter_grads", in_specs=[_ANY] * n, out_specs=[_ANY] * n,
        out_shape=side.outs, scratch_shapes=side.sems,
    )(*srcs)


def scatter_side(srcs):
    n = len(srcs)

    def copies(src, dst, sems, sends_only=False):
        ssem, rsem = sems
        x, y, c, me = _place()
        out = []
        for i in range(n):
            for k, px, py, peer in _chip_peers(x, y):
                sems_k = dict(send_sem=ssem.at[i, k - 1], recv_sem=rsem.at[i, k - 1], device_id=(px, py, c),
                              device_id_type=MESH_IDS)
                snd = pltpu.make_async_remote_copy(src_ref=src[i].at[peer], dst_ref=dst[i].at[me], **sems_k)
                if sends_only:
                    out.append(snd)
                    continue
                out.append((snd, pltpu.make_async_remote_copy(src_ref=src[i].at[peer], dst_ref=dst[i].at[peer],
                                                              **sems_k)))
        return out

    def start(src, dst, sems):
        for snd in copies(src, dst, sems, sends_only=True):
            snd.start()

    def finish(src, dst, sems):
        for snd, rcv in copies(src, dst, sems):
            rcv.wait_recv()
            snd.wait_send()

    return Side(list(srcs), [jax.ShapeDtypeStruct(s.shape, s.dtype) for s in srcs],
                [pltpu.SemaphoreType.DMA((n, 3))] * 2, start, finish)


_HBM = pl.BlockSpec(memory_space=pltpu.HBM)
_SEM = pl.BlockSpec(memory_space=pltpu.SEMAPHORE)
_DATAFLOW = pltpu.SideEffectType.DATAFLOW_SIDE_EFFECTING


def scatter_start(name, srcs):
    n = len(srcs)
    side = scatter_side(srcs)
    ns = 3 * n

    def body(*refs):
        src, land = refs[:n], refs[n:2 * n]
        sems = refs[2 * n:2 * n + 2 * ns]
        side.start(src, land, (_SemGrid(sems[:ns]), _SemGrid(sems[ns:])))
        refs[-1][...] = jnp.zeros_like(refs[-1])

    hbm = [pltpu.HBM(s.shape, s.dtype) for s in srcs]
    res = pl.pallas_call(
        body, name=name,
        out_shape=[pltpu.SemaphoreType.DMA(())] * (2 * ns) + hbm + hbm + [jax.ShapeDtypeStruct((8, LANE), F32)],
        in_specs=[_HBM] * (2 * n),
        out_specs=[_SEM] * (2 * ns) + [_HBM] * (2 * n) + [pl.BlockSpec(memory_space=pltpu.VMEM)],
        input_output_aliases={i: 2 * ns + i for i in range(2 * n)},
        compiler_params=pltpu.CompilerParams(has_side_effects=_DATAFLOW),
    )(*[pltpu.with_memory_space_constraint(s, pltpu.HBM) for s in srcs],
      *[pltpu.with_memory_space_constraint(lax.empty(s.shape, s.dtype), pltpu.HBM) for s in srcs])
    return res[:2 * ns], res[2 * ns:2 * ns + n], res[2 * ns + n:2 * ns + 2 * n], res[-1]


def scatter_wait(name, sems, srcs, lands, after):
    n = len(srcs)
    side = scatter_side(srcs)
    ns = 3 * n

    def body(*refs):
        src, land = refs[:n], refs[n:2 * n]
        s = refs[2 * n:2 * n + 2 * ns]
        side.finish(src, land, (_SemGrid(s[:ns]), _SemGrid(s[ns:])))

    hbm = [pltpu.HBM(s.shape, s.dtype) for s in srcs]
    res = pl.pallas_call(
        body, name=name, out_shape=hbm + hbm,
        in_specs=[_HBM] * (2 * n) + [_SEM] * (2 * ns) + [_ANY], out_specs=[_HBM] * (2 * n),
        input_output_aliases={i: i for i in range(2 * n)},
        compiler_params=pltpu.CompilerParams(has_side_effects=_DATAFLOW),
    )(*srcs, *lands, *sems, after)
    return res[:n], res[n:]


class _SemGrid:
    def __init__(self, sems):
        self.sems = sems

    @property
    def at(self):
        return self

    def __getitem__(self, ik):
        return self.sems[3 * ik[0] + ik[1]]


def swap_halves(name, srcs):
    n = len(srcs)

    def body(*refs):
        src, dst = refs[:n], refs[n:2 * n]
        ssem, rsem = refs[2 * n:]
        x, y, c, _ = _place()
        cps = []
        for i in range(n):
            cps.append(pltpu.make_async_remote_copy(src_ref=src[i], dst_ref=dst[i], send_sem=ssem.at[i],
                                                    recv_sem=rsem.at[i], device_id=(x, y, 1 - c),
                                                    device_id_type=MESH_IDS))
            cps[-1].start()
        for cp in cps:
            cp.wait()

    return pl.pallas_call(
        body, name=name, in_specs=[_ANY] * n, out_specs=[_ANY] * n,
        out_shape=[jax.ShapeDtypeStruct(s.shape, s.dtype) for s in srcs],
        scratch_shapes=[pltpu.SemaphoreType.DMA((n,))] * 2,
    )(*srcs)


def _ag8_copies(src, dst, sems, sends_only=False):
    x, y, c = lax.axis_index("x"), lax.axis_index("y"), lax.axis_index("c")
    me = 4 * x + 2 * y + c
    out = []
    for k in range(1, 8):
        px = 1 - x if k & 4 else x
        py = 1 - y if k & 2 else y
        pc = 1 - c if k & 1 else c
        peer = 4 * px + 2 * py + pc
        out.append(tuple(pltpu.make_async_remote_copy(
            src_ref=src, dst_ref=dst.at[slot], send_sem=sems[k - 1], recv_sem=sems[7 + k - 1],
            device_id=(px, py, pc), device_id_type=MESH_IDS) for slot in ((me,) if sends_only else (me, peer))))
    return out


def allgather8_start(name, src):
    def body(src_ref, land_ref, *rest):
        for (snd,) in _ag8_copies(src_ref, land_ref, rest[:14], sends_only=True):
            snd.start()

    land = jax.ShapeDtypeStruct((8,) + src.shape, src.dtype)
    res = pl.pallas_call(
        body, name=name,
        out_shape=[pltpu.SemaphoreType.DMA(())] * 14 + [pltpu.HBM(src.shape, src.dtype), pltpu.HBM(land.shape, land.dtype)],
        in_specs=[_HBM, _HBM], out_specs=[_SEM] * 14 + [_HBM, _HBM],
        input_output_aliases={0: 14, 1: 15},
        compiler_params=pltpu.CompilerParams(has_side_effects=_DATAFLOW),
    )(pltpu.with_memory_space_constraint(src, pltpu.HBM),
      pltpu.with_memory_space_constraint(lax.empty(land.shape, land.dtype), pltpu.HBM))
    return res[:14], res[14], res[15]


def allgather8_wait(name, sems, src, land, after):
    def body(src_ref, land_ref, *rest):
        for snd, rcv in _ag8_copies(src_ref, land_ref, rest[:14]):
            rcv.wait_recv()
            snd.wait_send()

    return pl.pallas_call(
        body, name=name, out_shape=[pltpu.HBM(src.shape, src.dtype), pltpu.HBM(land.shape, land.dtype)],
        in_specs=[_HBM, _HBM] + [_SEM] * 14 + [_ANY], out_specs=[_HBM, _HBM],
        input_output_aliases={0: 0, 1: 1},
        compiler_params=pltpu.CompilerParams(has_side_effects=_DATAFLOW),
    )(src, land, *sems, after)


WEIGHTS = ['g_pre', 'w_in', 'mla_q_norm', 'mla_wq_b', 'mla_kv_norm', 'mla_wkv_b', 'rwkv_mu', 'rwkv_w0_f', 'rwkv_w2_f',
           'rwkv_w0_b', 'rwkv_w2_b', 'rwkv_a0_f', 'rwkv_a2_f', 'rwkv_a0_b', 'rwkv_a2_b', 'rwkv_k_k', 'rwkv_k_a',
           'rwkv_r_k', 'rwkv_gn_g', 'rwkv_gn_b', 'w_br_mla', 'w_br_rwkv', 'w_out', 'g_post']
BIG_SHAPES = {'w_in': (D_IN // 4, D), 'mla_wq_b': (Q_RANK, 384), 'mla_wkv_b': (KV_RANK, 512),
              'rwkv_w2_f': (LORA, 256), 'rwkv_w2_b': (LORA, 256), 'rwkv_a2_f': (LORA, 256), 'rwkv_a2_b': (LORA, 256),
              'w_br_mla': (RW, 512), 'w_br_rwkv': (RW, 512), 'w_out': (512, D)}
BIG = list(BIG_SHAPES)
SMALL = [n for n in WEIGHTS if n not in BIG_SHAPES]
SMALL_SHAPES = {'g_pre': (D,), 'mla_q_norm': (Q_RANK,), 'mla_kv_norm': (KV_RANK,), 'rwkv_mu': (3456,),
                'rwkv_w0_f': (RW,), 'rwkv_w0_b': (RW,), 'rwkv_a0_f': (RW,), 'rwkv_a0_b': (RW,), 'rwkv_k_k': (RW,),
                'rwkv_k_a': (RW,), 'rwkv_r_k': (RH, RN), 'rwkv_gn_g': (RW,), 'rwkv_gn_b': (RW,), 'g_post': (D,)}
SMALL_LEN = sum(int(np.prod(s)) for s in SMALL_SHAPES.values())
SMALL_ROWS = 144


UNITS = [('w_in',), ('mla_wq_b',), ('mla_wkv_b',), ('rwkv_w2_f', 'rwkv_w2_b', 'rwkv_a2_f', 'rwkv_a2_b'),
         ('w_br_mla', 'w_br_rwkv'), ('w_out',)]
UNIT_AXIS = [1, 0, 0, 0, 0, 0]
ROW_SHARDED = ('w_in', 'w_out')


def _unit_cat(parts):
    return parts[0] if len(parts) == 1 else jnp.concatenate(parts, axis=0)


def _unit_split(arr, names, axis):
    out, o = {}, 0
    for n in names:
        rows = BIG_SHAPES[n][0]
        out[n] = lax.slice_in_dim(arr, o, o + rows, axis=axis)
        o += rows
    return out


def _gathered(units, ag, own, me):
    out = {}
    for names, arr, mine in zip(units, ag, own):
        slots = [jnp.where(me == j, mine, arr[j]) for j in range(4)]
        for n in names:
            parts = [_unit_split(s, names, 0)[n] for s in slots]
            out[n] = jnp.concatenate(parts, axis=0 if n in ROW_SHARDED else 1)
    return out


def _shards(n, g):
    r, w = BIG_SHAPES[n]
    if n in ROW_SHARDED:
        return [g[j * r:(j + 1) * r] for j in range(4)]
    return [g[:, j * w:(j + 1) * w] for j in range(4)]


def _pack_small(d, extra=None):
    flat = jnp.concatenate([d[n].reshape(-1) for n in SMALL] + ([extra.reshape(-1)] if extra is not None else []))
    return jnp.pad(flat, (0, SMALL_ROWS * LANE - flat.shape[0])).reshape(SMALL_ROWS, LANE)


def _unpack_small(packed):
    flat, out, o = packed.reshape(-1), {}, 0
    for n in SMALL:
        sz = int(np.prod(SMALL_SHAPES[n]))
        out[n] = flat[o:o + sz].reshape(SMALL_SHAPES[n])
        o += sz
    return out


def _perm_w_in(gathered, own, me):
    per = D_IN // 4

    def rows(a, b):
        out = []
        while a < b:
            j, lo = divmod(a, per)
            hi = min(b - j * per, per)
            out.append(jnp.where(me == j, own[lo:hi], gathered[j, lo:hi]))
            a = j * per + hi
        return out

    z = lambda n: [jnp.zeros((n, own.shape[1]), own.dtype)]
    lora = []
    for i in range(4):
        lora += rows(4160 + LORA * i, 4160 + LORA * (i + 1)) + z(LANE - LORA)
    return jnp.concatenate(rows(0, 1024) + rows(1088, 4160) + rows(4544, D_IN) + lora + rows(1024, 1088)
                           + z(256 - ROPE), axis=0)


def _unperm_w_in(g):
    lora = [g[OFF_LORA + LANE * i:OFF_LORA + LANE * i + LORA] for i in range(4)]
    return jnp.concatenate([g[0:1024], g[OFF_KR:OFF_KR + ROPE], g[1024:4096]] + lora + [g[4096:OFF_LORA]], axis=0)


def _perm_wq(w):
    w3 = w.reshape(Q_RANK, HEADS, NOPE + ROPE)
    rope = jnp.pad(w3[:, :, NOPE:], ((0, 0), (0, 0), (0, LANE - ROPE)))
    return jnp.concatenate([w3[:, :, :NOPE].reshape(Q_RANK, -1), rope.reshape(Q_RANK, -1)], axis=1)


def _unperm_wq(g):
    return jnp.concatenate([g[:, :1024].reshape(Q_RANK, HEADS, NOPE),
                            g[:, 1024:].reshape(Q_RANK, HEADS, LANE)[:, :, :ROPE]], axis=2).reshape(Q_RANK, -1)


def _perm_wkv(w):
    w3 = w.reshape(KV_RANK, HEADS, NOPE + VDIM)
    return jnp.concatenate([w3[:, :, :NOPE].reshape(KV_RANK, -1), w3[:, :, NOPE:].reshape(KV_RANK, -1)], axis=1)


def _unperm_wkv(g):
    return jnp.concatenate([g[:, :1024].reshape(KV_RANK, HEADS, NOPE), g[:, 1024:].reshape(KV_RANK, HEADS, VDIM)],
                           axis=2).reshape(KV_RANK, -1)


def _pad_rows(w):
    return jnp.pad(w, ((0, LANE - LORA), (0, 0)))


def _perm_mu(mu):
    parts = [mu[:3072]]
    for i in range(4):
        parts += [mu[3072 + LORA * i:3072 + LORA * (i + 1)], jnp.zeros((LANE - LORA,), mu.dtype)]
    return jnp.concatenate(parts).reshape(1, NLERP)


def _unperm_mu(g):
    g = g.reshape(-1)
    return jnp.concatenate([g[:3072]] + [g[3072 + LANE * i:3072 + LANE * i + LORA] for i in range(4)])


def _constants():
    g2 = np.kron(np.eye(2, dtype=np.float32), np.ones((RN, RN), np.float32))
    pos = jnp.arange(T, dtype=F32)
    inv_freq = jnp.power(ROPE_THETA, -jnp.arange(0, ROPE, 2, dtype=F32) / ROPE)
    ang = pos[:, None] * inv_freq[None, :]
    cos, sin, zero = jnp.cos(ang), jnp.sin(ang), jnp.zeros((T, LANE - ROPE), F32)
    cq = jnp.tile(jnp.concatenate([cos, cos, zero], axis=1), (1, HEADS))
    sq = jnp.tile(jnp.concatenate([-sin, sin, zero], axis=1), (1, HEADS))
    return jnp.asarray(g2, BF16), cq, sq


def _step(x, tgt, w, m, v):
    x2, tgt2 = x.reshape(T, D), tgt.reshape(T, D)
    g2, cq, sq = _constants()
    row = lambda n: w[n].reshape(1, -1)
    w, m, v = ({**t, 'w_in': t['w_in'].T} for t in (w, m, v))

    core, chip = lax.axis_index("c"), 2 * lax.axis_index("x") + lax.axis_index("y")
    core1, chip1 = core.astype(jnp.int32).reshape(1), chip.astype(jnp.int32).reshape(1)
    own_bf = [_unit_cat([w[n].astype(BF16) for n in u]) for u in UNITS]
    wp = _perm_w_in(gather_weights(own_bf[:1], UNIT_AXIS[:1])[0], own_bf[0], chip)
    full = {}
    mu_p = _perm_mu(w['rwkv_mu'])

    st_pre = Stage("pre", f_pre, [(D, BF16), (D, None)], 256, [0], [0], [F32])
    st_mla = Stage("mla", f_mla, [(1024, BF16), (1024, BF16), (1024, BF16), (LANE, BF16), (1024, BF16)], 256,
                   [0, 1, 2], [0, 1, 2, 3], [BF16] * 3)
    st_rpre = Stage("rwkv_pre", f_rwkv_pre, [(RW, F32)] * 9, 256, [0], list(range(10)), [F32])
    st_rpost = Stage("rwkv_post", f_rwkv_post, [(RW, BF16)], 256, [0, 2, 3, 4, 5, 6], [0, 1, 2],
                     [F32, F32, F32, F32, F32, BF16])
    st_gate = Stage("gate", f_gate, [(RW, BF16)], 256, [0, 1], [], [F32, BF16])
    st_merge = Stage("merge", f_merge, [(D, BF16)], 256, [0, 1, 2, 3], [], [BF16] * 4)

    pre_rows, pre_par = [(x2, D, 0)], [row('g_pre')]
    (h,) = st_pre.fwd(pre_rows, pre_par)
    proj, rest = matmul("mm_in", h, wp, "nt", side=gather_side(own_bf[1:4], UNIT_AXIS[1:4]))
    full.update(_gathered(UNITS[1:4], rest, own_bf[1:4], chip))
    wq, wkv = _perm_wq(full['mla_wq_b']), _perm_wkv(full['mla_wkv_b'])
    lora_w = [_pad_rows(full[n]).astype(F32) for n in ('rwkv_w2_f', 'rwkv_w2_b', 'rwkv_a2_f', 'rwkv_a2_b')]

    mla_rows = [(proj, 512, OFF_QA // 512), (proj, 512, OFF_KVA // 512), (proj, 256, OFF_KR // 256),
                (cq, 1024, 0), (sq, 1024, 0), (cq, LANE, 0), (sq, LANE, 0)]
    mla_par = [row('mla_q_norm'), row('mla_kv_norm'), wq, wkv]
    att = st_mla.fwd(mla_rows, mla_par)
    y_mla, rest = attn_fwd(*att, gather_side(own_bf[4:], UNIT_AXIS[4:]))
    full.update(_gathered(UNITS[4:], rest, own_bf[4:], chip))

    lerp = shift_fwd(proj, mu_p)
    rpre_rows = [(lerp, NLERP, 0)]
    rpre_par = [row('rwkv_w0_f'), row('rwkv_w0_b'), row('rwkv_a0_f'), row('rwkv_a0_b'), row('rwkv_k_k'),
                row('rwkv_k_a')] + lora_w + [g2]
    r_, v_, lwf, lwb, kf, kb, an, bf_, bb_ = st_rpre.fwd(rpre_rows, rpre_par)
    fin = [r_, lwf, kf, v_, an, bf_]
    bin_ = [r_, lwb, kb, v_, an, bb_]
    yf, h0f = scan_fwd("scan_f", *fin, reverse=False)
    yb, h0b = scan_fwd("scan_b", *bin_, reverse=True)
    rpost_rows = [(yf, RW, 0), (yb, RW, 0), (r_, RW, 0), (kf, RW, 0), (kb, RW, 0),
                  (v_, RW, 0), (proj, RW, OFF_ZR // RW)]
    rpost_par = [row('rwkv_gn_g'), row('rwkv_gn_b'), row('rwkv_r_k'), g2]
    (gr,) = st_rpost.fwd(rpost_rows, rpost_par)
    gate_rows = [(y_mla, RW, 0), (proj, RW, OFF_ZM // RW)]
    (gm,) = st_gate.fwd(gate_rows, [])
    um = matmul("mm_br_mla", gm, full['w_br_mla'], "nn")
    ur = matmul("mm_br_rwkv", gr, full['w_br_rwkv'], "nn")
    merge_rows = [(um, D, 0), (ur, D, 0), (proj, D, OFF_GM // D), (proj, D, OFF_GR // D)]
    (merged,) = st_merge.fwd(merge_rows, [])
    out = matmul("mm_out", merged, full['w_out'], "nn")
    d_out, dy, dg_post, loss_blk = loss_stage(out, x2, tgt2, row('g_post'))

    gw = {'g_post': dg_post}
    d_merged = matmul("mm_out_dx", d_out, full['w_out'], "nt")
    gw['w_out'] = matmul("mm_out_dw", merged, d_out, "tn")
    (d_um, d_ur, d_gm, d_gr), _ = st_merge.bwd(merge_rows, [], [[(d_merged, D, 0)]])
    d_gmla = matmul("mm_br_mla_dx", d_um, full['w_br_mla'], "nt")
    gw['w_br_mla'] = matmul("mm_br_mla_dw", gm, d_um, "tn")
    d_grw = matmul("mm_br_rwkv_dx", d_ur, full['w_br_rwkv'], "nt")
    gw['w_br_rwkv'] = matmul("mm_br_rwkv_dw", gr, d_ur, "tn")
    (d_ymla, d_zm), _ = st_gate.bwd(gate_rows, [], [[(d_gmla, RW, 0)]])
    (d_y, d_r3, d_kf2, d_kb2, d_v3, d_zr), (gw['rwkv_gn_g'], gw['rwkv_gn_b'], d_rk) = st_rpost.bwd(
        rpost_rows, rpost_par, [[(d_grw, RW, 0)]])
    gw['rwkv_r_k'] = d_rk
    sf = scan_bwd("scan_f_bwd", *fin, h0f, d_y, reverse=False)
    sb = scan_bwd("scan_b_bwd", *bin_, h0b, d_y, reverse=True)
    c = lambda *ts: [(t, RW, 0) for t in ts]
    rpre_cts = [c(sf[0], sb[0], d_r3), c(sf[3], sb[3], d_v3), c(sf[1]), c(sb[1]), c(sf[2], d_kf2), c(sb[2], d_kb2),
                c(sf[4], sb[4]), c(sf[5]), c(sb[5])]
    (d_rin,), rpre_g = st_rpre.bwd(rpre_rows, rpre_par, rpre_cts)
    for n, gval in zip(('rwkv_w0_f', 'rwkv_w0_b', 'rwkv_a0_f', 'rwkv_a0_b', 'rwkv_k_k', 'rwkv_k_a'), rpre_g[:6]):
        gw[n] = gval
    for n, gval in zip(('rwkv_w2_f', 'rwkv_w2_b', 'rwkv_a2_f', 'rwkv_a2_b'), rpre_g[6:]):
        gw[n] = gval[:LORA]
    d_lerp, d_mu = shift_bwd(proj, mu_p, d_rin)
    gw['rwkv_mu'] = _unperm_mu(d_mu)

    mla_cts = [[(t, t.shape[1], 0)] for t in attn_bwd(*att, d_ymla)]
    (d_qa, d_kva, d_kr), (gw['mla_q_norm'], gw['mla_kv_norm'], d_wq, d_wkv) = st_mla.bwd(mla_rows, mla_par, mla_cts)
    gw['mla_wq_b'], gw['mla_wkv_b'] = _unperm_wq(d_wq), _unperm_wkv(d_wkv)

    dproj = jnp.concatenate([d_qa, d_kva, d_lerp[:, :3072], d_zm, d_zr, d_gm, d_gr, d_lerp[:, 3072:], d_kr], axis=1)

    def pair_sums(name, ids):
        send = [jnp.stack([_unit_cat([_shards(n, gw[n])[j].astype(BF16) for n in UNITS[i]]) for j in range(4)])
                for i in ids]
        axes = [UNIT_AXIS[i] for i in ids]
        other = pair_exchange(name, send, axes)
        return [pair_sum(f"pair_sum_{i}", core1, s, o, ax) for i, s, o, ax in zip(ids, send, other, axes)]

    late, early = [0], list(range(1, len(UNITS)))
    pairs_e = pair_sums("pair_exchange_rest", early)
    gw_in, recv_e = matmul("mm_in_dw", dproj, h, "tn", BF16, side=scatter_side(pairs_e))
    gw['w_in'] = _unperm_w_in(gw_in)
    pairs_l = pair_sums("pair_exchange_w_in", late)
    sems, src_fly, land_fly, token = scatter_start("scatter_w_in_start", pairs_l)
    dh = matmul("mm_in_dx", dproj, wp, "nn", after=(token,))
    (grad_x,), (gw['g_pre'],) = st_pre.bwd(pre_rows, pre_par, [[(dh, D, 0)], [(dy, D, 0)]])

    big = [dict() for _ in range(4)]

    def update(name, ids, recv, pairs):
        mine = [sum4(f"sum4_{i}", chip1, r, p, UNIT_AXIS[i]) for i, r, p in zip(ids, recv, pairs)]
        theirs = swap_halves(name, mine)
        for i, mi, th in zip(ids, mine, theirs):
            res = adamw_halves(f"adamw_{i}", core1, *[_unit_cat([t[n] for n in UNITS[i]]) for t in (w, m, v)], mi, th,
                               UNIT_AXIS[i])
            for q in range(4):
                big[q].update(_unit_split(res[q], UNITS[i], 0))
        return res

    small_fly = allgather8_start("gather_small_start", _pack_small(gw, loss_blk[0, :1]))
    last = update("swap_halves_rest", early, recv_e, pairs_e)
    own_small, landed = allgather8_wait("gather_small_wait", *small_fly, last[0])
    dev = 2 * chip + core
    parts = [jnp.where(dev == i, own_small, landed[i]) for i in range(8)]
    small = adamw("adamw_small", _pack_small(w), _pack_small(m), _pack_small(v), parts)
    pairs_l, recv_l = scatter_wait("scatter_w_in_wait", sems, src_fly, land_fly, small[0] + last[0][:1, :1])
    update("swap_halves_w_in", late, recv_l, pairs_l)

    outs = []
    for b_d, s_arr in zip(big, small):
        d = {**b_d, **_unpack_small(s_arr)}
        d['w_in'] = d['w_in'].T
        outs.append([d[n] for n in WEIGHTS])
    loss = small[0][SMALL_LEN // LANE, 0]
    return (loss, grad_x.reshape(1, T, D), *outs[0], *outs[1], *outs[2], *outs[3])


def kernel(x, g_pre, w_in, mla_q_norm, mla_wq_b, mla_kv_norm, mla_wkv_b, rwkv_mu, rwkv_w0_f, rwkv_w2_f, rwkv_w0_b, rwkv_w2_b, rwkv_a0_f, rwkv_a2_f, rwkv_a0_b, rwkv_a2_b, rwkv_k_k, rwkv_k_a, rwkv_r_k, rwkv_gn_g, rwkv_gn_b, w_br_mla, w_br_rwkv, w_out, g_post, loss_target, m_g_pre, m_w_in, m_mla_q_norm, m_mla_wq_b, m_mla_kv_norm, m_mla_wkv_b, m_rwkv_mu, m_rwkv_w0_f, m_rwkv_w2_f, m_rwkv_w0_b, m_rwkv_w2_b, m_rwkv_a0_f, m_rwkv_a2_f, m_rwkv_a0_b, m_rwkv_a2_b, m_rwkv_k_k, m_rwkv_k_a, m_rwkv_r_k, m_rwkv_gn_g, m_rwkv_gn_b, m_w_br_mla, m_w_br_rwkv, m_w_out, m_g_post, v_g_pre, v_w_in, v_mla_q_norm, v_mla_wq_b, v_mla_kv_norm, v_mla_wkv_b, v_rwkv_mu, v_rwkv_w0_f, v_rwkv_w2_f, v_rwkv_w0_b, v_rwkv_w2_b, v_rwkv_a0_f, v_rwkv_a2_f, v_rwkv_a0_b, v_rwkv_a2_b, v_rwkv_k_k, v_rwkv_k_a, v_rwkv_r_k, v_rwkv_gn_g, v_rwkv_gn_b, v_w_br_mla, v_w_br_rwkv, v_w_out, v_g_post):
    given = dict(locals())
    w = {n: given[n] for n in WEIGHTS}
    m = {n: given['m_' + n] for n in WEIGHTS}
    v = {n: given['v_' + n] for n in WEIGHTS}
    return _step(x, loss_target, w, m, v)
```

```python
import functools
import math

import numpy as np
import jax
import jax.numpy as jnp
from jax import lax
from jax.experimental import pallas as pl
from jax.experimental.pallas import tpu as pltpu

F32, BF16 = jnp.float32, jnp.bfloat16
MESH_IDS = pl.DeviceIdType.MESH

D = 2048
T = 2048
HEADS = 8
Q_RANK = 512
KV_RANK = 512
NOPE = 128
ROPE = 64
VDIM = 128
RW = 1024
RH = 16
RN = 64
LORA = 96
D_IN = 10688
NORM_EPS = 1e-6
GN_EPS = 64e-5
ROPE_THETA = 10000.0
ADAM_LR, ADAM_B1, ADAM_B2, ADAM_EPS, ADAM_WD, ADAM_STEP = 0.001, 0.9, 0.999, 1e-08, 0.01, 10

LANE = 128
VMEM_BIG = 56 * 2**20

NP = 11008
OFF_QA, OFF_KVA, OFF_RKV, OFF_ZM, OFF_ZR, OFF_GM, OFF_GR, OFF_LORA, OFF_KR = 0, 512, 1024, 4096, 5120, 6144, 8192, 10240, 10752
NLERP = 3584

CHUNK = 64
NCH = T // CHUNK


def _dg(a, b, ca, cb, batch=False, prec=None):
    bd = ((0,), (0,)) if batch else ((), ())
    return lax.dot_general(a, b, (((ca,), (cb,)), bd), precision=prec, preferred_element_type=F32)


@jax.custom_vjp
def bdot(a, b):
    return _dg(a.astype(BF16), b.astype(BF16), 1, 0)


def _bdot_fwd(a, b):
    return bdot(a, b), (a, b)


def _bdot_bwd(res, g):
    a, b = res
    gb = g.astype(BF16)
    da = _dg(gb, b.astype(BF16), 1, 1)
    db = _dg(a.astype(BF16), gb, 0, 0)
    return da.astype(a.dtype), db.astype(b.dtype)


bdot.defvjp(_bdot_fwd, _bdot_bwd)


def _split(x):
    hi = x.astype(BF16)
    lo = (x - hi.astype(F32)).astype(BF16)
    return hi, lo


@jax.custom_vjp
def gsum(x, g2):
    hi, lo = _split(x)
    return _dg(hi, g2, 1, 0) + _dg(lo, g2, 1, 0)


def _gsum_fwd(x, g2):
    return gsum(x, g2), g2


def _gsum_bwd(g2, g):
    hi, lo = _split(g)
    return _dg(hi, g2, 1, 1) + _dg(lo, g2, 1, 1), jnp.zeros_like(g2)


gsum.defvjp(_gsum_fwd, _gsum_bwd)


def headsum(x, g2):
    return jnp.concatenate([gsum(x[:, i * LANE:(i + 1) * LANE], g2) for i in range(x.shape[1] // LANE)], axis=1)


def _terms(x, n):
    out = []
    for i in range(n):
        t = x.astype(BF16)
        out.append(t)
        if i < n - 1:
            x = x - t.astype(F32)
    return out


def _bmm(a, b, ca, cb, na, nb):
    acc = None
    for i, ai in enumerate(_terms(a, na)):
        for j, bj in enumerate(_terms(b, nb)):
            if i + j < max(na, nb):
                p = _dg(ai, bj, ca, cb, True)
                acc = p if acc is None else acc + p
    return acc


_NN, _NT, _TN = (2, 1), (2, 2), (1, 1)


def _make_dots(nf, nb_nn, nb_nt, nb_tn):
    @jax.custom_vjp
    def nn(a, b):
        return _bmm(a, b, *_NN, nf, nf)

    @jax.custom_vjp
    def nt(a, b):
        return _bmm(a, b, *_NT, nf, nf)

    @jax.custom_vjp
    def tn(a, b):
        return _bmm(a, b, *_TN, nf, nf)

    nn.defvjp(lambda a, b: (nn(a, b), (a, b)),
              lambda r, g: (_bmm(g, r[1], *_NT, nb_nn, nb_nn), _bmm(r[0], g, *_TN, nb_nn, nb_nn)))
    nt.defvjp(lambda a, b: (nt(a, b), (a, b)),
              lambda r, g: (_bmm(g, r[1], *_NN, 1, nb_nt), _bmm(g, r[0], *_TN, 1, nb_nt)))
    tn.defvjp(lambda a, b: (tn(a, b), (a, b)),
              lambda r, g: (_bmm(r[1], g, *_NT, nb_tn, nb_tn), _bmm(r[0], g, *_NN, nb_tn, nb_tn)))
    return nn, nt, tn


_SCAN_NF, _SCAN_NB = 1, 1
nn, nt, tn = _make_dots(_SCAN_NF, 1, 2, 1)


@jax.custom_vjp
def cumdot(ones, x):
    return _bmm(ones, x, *_NN, 1, 3)


cumdot.defvjp(lambda o, x: (cumdot(o, x), o), lambda o, g: (jnp.zeros_like(o), _bmm(o, g, *_TN, 1, 3)))


def _solve_powers(l):
    pw = [l]
    for _ in range(int(math.log2(l.shape[-1])) - 1):
        pw.append(_bmm(pw[-1], pw[-1], *_NN, _SCAN_NF, _SCAN_NF))
    return pw


@jax.custom_vjp
def tri_solve(l, rhs):
    x = rhs
    for p in _solve_powers(l):
        x = x + _bmm(p, x, *_NN, _SCAN_NF, _SCAN_NF)
    return x


def _tri_solve_fwd(l, rhs):
    pw = _solve_powers(l)
    x = rhs
    for p in pw:
        x = x + _bmm(p, x, *_NN, _SCAN_NF, _SCAN_NF)
    return x, (pw, x)


def _tri_solve_bwd(res, g):
    pw, x = res
    y = g
    for p in pw:
        y = y + _bmm(p, y, *_TN, _SCAN_NB, _SCAN_NB)
    return _bmm(y, x, *_NT, _SCAN_NB, _SCAN_NB), y


tri_solve.defvjp(_tri_solve_fwd, _tri_solve_bwd)


def _rms(x, g):
    return x * lax.rsqrt(jnp.mean(x * x, axis=-1, keepdims=True) + NORM_EPS) * g


def _softplus(x):
    pos = x > 0
    return jnp.where(pos, x, 0.0) + jnp.log(1.0 + jnp.exp(-jnp.where(pos, x, -x)))


def _silu(z):
    return z * jax.nn.sigmoid(z)


_MM_VMEM_BYTES = 32 * 2**20


def _mm_tiles(m, n, k):
    best = None
    for tm in (2048, 1024, 512, 256):
        for tn_ in (2048, 1024, 512, 256):
            for d in range(k // LANE, 0, -1):
                tk = LANE * d
                if m % tm or n % tn_ or k % tk:
                    continue
                nk = k // tk
                vmem = 4 * tk * (tm + tn_) + 8 * tm * tn_ + (4 * tm * tn_ if nk > 1 else 0)
                if vmem > _MM_VMEM_BYTES:
                    continue
                a_reads = n // tn_ if nk > 1 else 1
                b_reads = 1 if (nk == 1 and n == tn_) else m // tm
                acc_rmw = nk * m * n if nk > 1 else 0
                cost = (a_reads * m * k + b_reads * k * n + acc_rmw, -tm * tn_ * tk)
                if best is None or cost < best[0]:
                    best = (cost, (tm, tn_, tk))
    return best[1]


class Side:
    def __init__(self, ins, outs, sems, start, finish):
        self.ins, self.outs, self.sems, self.start, self.finish = ins, outs, sems, start, finish

    def at_step(self, step, steps, *refs):
        @pl.when(step == 0)
        def _():
            self.start(*refs)

    def at_end(self, step, steps, *refs):
        @pl.when(step == steps - 1)
        def _():
            self.finish(*refs)

    def run(self, *refs):
        self.start(*refs)
        self.finish(*refs)


def matmul(name, a, b, mode, out_dtype=F32, side=None, after=()):
    if mode == "nn":
        (m, k), n = a.shape, b.shape[1]
    elif mode == "nt":
        (m, k), n = a.shape, b.shape[0]
    else:
        (k, m), n = a.shape, b.shape[1]
    tm, tn_, tk = _mm_tiles(m, n, k)
    nk = k // tk
    if mode == "nn":
        a_spec = pl.BlockSpec((tm, tk), lambda i, j, kk: (i, kk))
        b_spec = pl.BlockSpec((tk, tn_), lambda i, j, kk: (kk, j))
        ca, cb = 1, 0
    elif mode == "nt":
        a_spec = pl.BlockSpec((tm, tk), lambda i, j, kk: (i, kk))
        b_spec = pl.BlockSpec((tn_, tk), lambda i, j, kk: (j, kk))
        ca, cb = 1, 1
    else:
        a_spec = pl.BlockSpec((tk, tm), lambda i, j, kk: (kk, i))
        b_spec = pl.BlockSpec((tk, tn_), lambda i, j, kk: (kk, j))
        ca, cb = 0, 0

    grid = (m // tm, n // tn_, nk)
    n_in = len(side.ins) if side else 0
    n_out = len(side.outs) if side else 0
    n_dep = len(after)

    def body(a_ref, b_ref, *rest):
        rest = rest[n_dep:]
        s_ins, o_ref, s_outs = rest[:n_in], rest[n_in], rest[n_in + 1:n_in + 1 + n_out]
        scratch = rest[n_in + 1 + n_out:]
        acc, s_sems = (scratch[:1], scratch[1:]) if nk > 1 else ((), scratch)
        steps = grid[0] * grid[1] * grid[2]
        if side:
            step = (pl.program_id(0) * grid[1] + pl.program_id(1)) * grid[2] + pl.program_id(2)
            side.at_step(step, steps, s_ins, s_outs, s_sems)

        part = _dg(a_ref[...].astype(BF16), b_ref[...].astype(BF16), ca, cb)
        if nk == 1:
            o_ref[...] = part.astype(o_ref.dtype)
        else:
            acc_ref, kk = acc[0], pl.program_id(2)

            @pl.when(kk == 0)
            def _():
                acc_ref[...] = part

            @pl.when(kk > 0)
            def _():
                acc_ref[...] += part

            @pl.when(kk == nk - 1)
            def _():
                o_ref[...] = acc_ref[...].astype(o_ref.dtype)

        if side:
            side.at_end(step, steps, s_ins, s_outs, s_sems)

    res = pl.pallas_call(
        body, name=name, grid=grid,
        in_specs=[a_spec, b_spec] + [_ANY] * (n_dep + n_in),
        out_specs=[pl.BlockSpec((tm, tn_), lambda i, j, kk: (i, j))] + [_ANY] * n_out,
        out_shape=[jax.ShapeDtypeStruct((m, n), out_dtype)] + (list(side.outs) if side else []),
        scratch_shapes=([pltpu.VMEM((tm, tn_), F32)] if nk > 1 else []) + (list(side.sems) if side else []),
        compiler_params=pltpu.CompilerParams(
            dimension_semantics=("arbitrary",) * 3 if side else ("parallel", "parallel", "arbitrary"),
            vmem_limit_bytes=VMEM_BIG),
    )(a, b, *after, *(side.ins if side else []))
    return (res[0], res[1:]) if side else res[0]


def _rspec(tr, width, blk):
    return pl.BlockSpec((tr, width), lambda i: (i, blk))


def _full_spec(arr):
    return pl.BlockSpec(arr.shape, lambda i: (0,) * arr.ndim)


class Stage:
    def __init__(self, name, f, outs, tr, diff_rows, diff_params, drow_dtypes):
        self.name, self.f, self.outs, self.tr = name, f, outs, tr
        self.diff_rows, self.diff_params, self.drow_dtypes = diff_rows, diff_params, drow_dtypes

    def fwd(self, rows, params):
        f, nr, npar = self.f, len(rows), len(params)
        stored = [(w, dt) for (w, dt) in self.outs if dt is not None]
        keep = [i for i, (w, dt) in enumerate(self.outs) if dt is not None]

        def body(*refs):
            vals = f(*[r[...].astype(F32) for r in refs[:nr]], *[p[...] for p in refs[nr:nr + npar]])
            for o_ref, i in zip(refs[nr + npar:], keep):
                o_ref[...] = vals[i].astype(o_ref.dtype)

        return pl.pallas_call(
            body, name=self.name + "_fwd", grid=(T // self.tr,),
            in_specs=[_rspec(self.tr, w, b) for (_, w, b) in rows] + [_full_spec(p) for p in params],
            out_specs=[_rspec(self.tr, w, 0) for (w, _) in stored],
            out_shape=[jax.ShapeDtypeStruct((T, w), dt) for (w, dt) in stored],
            compiler_params=pltpu.CompilerParams(dimension_semantics=("arbitrary",), vmem_limit_bytes=VMEM_BIG),
        )(*[r[0] for r in rows], *params)

    def bwd(self, rows, params, cts):
        f, nr, npar = self.f, len(rows), len(params)
        dr_idx, dp_idx = self.diff_rows, self.diff_params
        flat_cts = [c for lst in cts for c in lst]
        nct = len(flat_cts)

        def body(*refs):
            row_refs, par_refs = refs[:nr], refs[nr:nr + npar]
            ct_refs = refs[nr + npar:nr + npar + nct]
            drow_refs = refs[nr + npar + nct:nr + npar + nct + len(dr_idx)]
            dpar_refs = refs[nr + npar + nct + len(dr_idx):]
            row_vals = [r[...].astype(F32) for r in row_refs]
            par_vals = [p[...] for p in par_refs]

            def g(*dv):
                rv, pv = list(row_vals), list(par_vals)
                for j, i in enumerate(dr_idx):
                    rv[i] = dv[j]
                for j, i in enumerate(dp_idx):
                    pv[i] = dv[len(dr_idx) + j]
                return f(*rv, *pv)

            _, vjp = jax.vjp(g, *[row_vals[i] for i in dr_idx], *[par_vals[i] for i in dp_idx])
            ct_vals, pos = [], 0
            for lst in cts:
                acc = ct_refs[pos][...].astype(F32)
                for q in range(1, len(lst)):
                    acc = acc + ct_refs[pos + q][...].astype(F32)
                pos += len(lst)
                ct_vals.append(acc)
            grads = vjp(tuple(ct_vals))
            for j, r in enumerate(drow_refs):
                r[...] = grads[j].astype(r.dtype)

            @pl.when(pl.program_id(0) == 0)
            def _():
                for r in dpar_refs:
                    r[...] = jnp.zeros_like(r)

            for j, r in enumerate(dpar_refs):
                r[...] += grads[len(dr_idx) + j].astype(F32)

        drow_shapes = [jax.ShapeDtypeStruct((T, rows[i][1]), dt) for i, dt in zip(dr_idx, self.drow_dtypes)]
        dpar_shapes = [jax.ShapeDtypeStruct(params[i].shape, F32) for i in dp_idx]
        res = pl.pallas_call(
            body, name=self.name + "_bwd", grid=(T // self.tr,),
            in_specs=[_rspec(self.tr, w, b) for (_, w, b) in rows] + [_full_spec(p) for p in params]
            + [_rspec(self.tr, w, b) for (_, w, b) in flat_cts],
            out_specs=[_rspec(self.tr, rows[i][1], 0) for i in dr_idx] + [_full_spec(params[i]) for i in dp_idx],
            out_shape=drow_shapes + dpar_shapes,
            compiler_params=pltpu.CompilerParams(dimension_semantics=("arbitrary",), vmem_limit_bytes=VMEM_BIG),
        )(*[r[0] for r in rows], *params, *[c[0] for c in flat_cts])
        return res[:len(dr_idx)], res[len(dr_idx):]


def f_pre(x, g):
    return _rms(x, g), x


@jax.custom_vjp
def swap32(t):
    width = t.shape[1]
    lane = lax.broadcasted_iota(jnp.int32, t.shape, 1) % LANE
    return jnp.where(lane < 32, pltpu.roll(t, width - 32, 1), jnp.where(lane < 64, pltpu.roll(t, 32, 1), 0.0))


swap32.defvjp(lambda t: (swap32(t), None), lambda _, g: (swap32(g),))


def f_mla(q_a, kv_a, kr, cq, sq, ck, sk, gq, gkv, wq, wkv):
    q = bdot(_rms(q_a, gq), wq)
    kv = bdot(_rms(kv_a, gkv), wkv)
    t, k = q[:, 1024:], kr[:, :LANE]
    return (q[:, :1024], t * cq + swap32(t) * sq, kv[:, :1024], k * ck + swap32(k) * sk, kv[:, 1024:])


def f_rwkv_pre(lerp, w0f, w0b, a0f, a0b, kkw, kaw, w2f, w2b, a2f, a2b, g2):
    r, k, v = lerp[:, :RW], lerp[:, RW:2 * RW], lerp[:, 2 * RW:3 * RW]
    wdf, wdb, adf, adb = (lerp[:, 3 * RW + i * LANE:3 * RW + (i + 1) * LANE] for i in range(4))

    def logdecay(w0, wd, w2):
        z = w0 + bdot(jnp.tanh(wd), w2)
        return -jnp.exp(-_softplus(-z) - 0.5)

    a_f = jax.nn.sigmoid(a0f + bdot(adf, a2f))
    a_b = jax.nn.sigmoid(a0b + bdot(adb, a2b))
    kk = k * kkw
    kk = kk / jnp.maximum(jnp.sqrt(headsum(kk * kk, g2)), 1e-12)
    return (r, v, logdecay(w0f, wdf, w2f), logdecay(w0b, wdb, w2b),
            k * (1.0 + (a_f - 1.0) * kaw), k * (1.0 + (a_b - 1.0) * kaw), -kk, kk * a_f, kk * a_b)


def f_rwkv_post(yf, yb, r, kf, kb, v, z, gng, gnb, rk, g2):
    y = yf + yb
    mu = headsum(y, g2) * (1.0 / RN)
    d = y - mu
    var = headsum(d * d, g2) * (1.0 / RN)
    yn = d * lax.rsqrt(var + GN_EPS) * gng + gnb
    bonus = headsum(r * (kf + kb) * rk, g2) * v
    return ((yn + bonus) * _silu(z),)


def f_gate(y, z):
    return (y * _silu(z),)


def f_merge(um, ur, gm, gr):
    return (jax.nn.sigmoid(gm) * um + jax.nn.sigmoid(gr) * ur,)


_SHIFT_W = 256


def _lerp_colblock(j):
    return jnp.where(j < 3072 // _SHIFT_W, OFF_RKV // _SHIFT_W + j, OFF_LORA // _SHIFT_W + j - 3072 // _SHIFT_W)


def _nbr_mean(x):
    row = lax.broadcasted_iota(jnp.int32, x.shape, 0)
    up = jnp.where(row == 0, 0.0, pltpu.roll(x, 1, 0))
    dn = jnp.where(row == T - 1, 0.0, pltpu.roll(x, T - 1, 0))
    return 0.5 * (up + dn)


def shift_fwd(proj, mu):
    def body(x_ref, mu_ref, o_ref):
        x = x_ref[...]
        o_ref[...] = x + mu_ref[...] * (_nbr_mean(x) - x)

    return pl.pallas_call(
        body, name="shift_fwd", grid=(NLERP // _SHIFT_W,),
        in_specs=[pl.BlockSpec((T, _SHIFT_W), lambda j: (0, _lerp_colblock(j))),
                  pl.BlockSpec((1, _SHIFT_W), lambda j: (0, j))],
        out_specs=pl.BlockSpec((T, _SHIFT_W), lambda j: (0, j)),
        out_shape=jax.ShapeDtypeStruct((T, NLERP), F32),
        compiler_params=pltpu.CompilerParams(dimension_semantics=("parallel",), vmem_limit_bytes=VMEM_BIG),
    )(proj, mu)


def shift_bwd(proj, mu, g):
    def body(x_ref, mu_ref, g_ref, dx_ref, dmu_ref):
        x, gv = x_ref[...], g_ref[...]
        dmu_ref[...] = jnp.sum(gv * (_nbr_mean(x) - x), axis=0, keepdims=True)
        gm = gv * mu_ref[...]
        dx_ref[...] = (gv - gm + _nbr_mean(gm)).astype(dx_ref.dtype)

    col = pl.BlockSpec((T, _SHIFT_W), lambda j: (0, j))
    vec = pl.BlockSpec((1, _SHIFT_W), lambda j: (0, j))
    return pl.pallas_call(
        body, name="shift_bwd", grid=(NLERP // _SHIFT_W,),
        in_specs=[pl.BlockSpec((T, _SHIFT_W), lambda j: (0, _lerp_colblock(j))), vec, col],
        out_specs=[col, vec],
        out_shape=[jax.ShapeDtypeStruct((T, NLERP), BF16), jax.ShapeDtypeStruct((1, NLERP), F32)],
        compiler_params=pltpu.CompilerParams(dimension_semantics=("parallel",), vmem_limit_bytes=VMEM_BIG),
    )(proj, mu, g)


_TQ_F, _TQ_B = 256, 512
_ATT_SCALE = (NOPE + ROPE) ** -0.5


def _probs(q, k):
    s = _dg(q, k, 1, 1) * _ATT_SCALE
    e = jnp.exp(s - jnp.max(s, axis=-1, keepdims=True))
    return e * (1.0 / jnp.sum(e, axis=-1, keepdims=True))


def _q_blk(tq):
    return pl.BlockSpec((tq, LANE), lambda h, i: (i, h))


_K_BLK = pl.BlockSpec((T, LANE), lambda h, i: (0, h))
_KR_BLK = pl.BlockSpec((T, LANE), lambda h, i: (0, 0))


def _load_qk(qn_ref, qr_ref, kn_ref, kr_ref, kcat_ref):
    @pl.when(pl.program_id(1) == 0)
    def _():
        kcat_ref[:, :LANE] = kn_ref[...]
        kcat_ref[:, LANE:] = kr_ref[...]

    return jnp.concatenate([qn_ref[...], qr_ref[...]], axis=1), kcat_ref[...]


def attn_fwd(qn, qr, kn, kr, v):
    def body(qn_ref, qr_ref, kn_ref, kr_ref, v_ref, o_ref, kcat_ref):
        q, k = _load_qk(qn_ref, qr_ref, kn_ref, kr_ref, kcat_ref)
        o_ref[...] = _dg(_probs(q, k).astype(BF16), v_ref[...], 1, 0)

    return pl.pallas_call(
        body, name="attn_fwd", grid=(HEADS, T // _TQ_F),
        in_specs=[_q_blk(_TQ_F), _q_blk(_TQ_F), _K_BLK, _KR_BLK, _K_BLK], out_specs=_q_blk(_TQ_F),
        out_shape=jax.ShapeDtypeStruct((T, HEADS * VDIM), F32),
        scratch_shapes=[pltpu.VMEM((T, 2 * LANE), BF16)],
        compiler_params=pltpu.CompilerParams(dimension_semantics=("arbitrary", "arbitrary"), vmem_limit_bytes=VMEM_BIG),
    )(qn, qr, kn, kr, v)


def attn_bwd(qn, qr, kn, kr, v, do):
    def body(qn_ref, qr_ref, kn_ref, kr_ref, v_ref, do_ref, dqn_ref, dqr_ref, dkn_ref, dkr_ref, dv_ref, kcat_ref):
        h, i = pl.program_id(0), pl.program_id(1)

        @pl.when(i == 0)
        def _():
            dkn_ref[...] = jnp.zeros_like(dkn_ref)
            dv_ref[...] = jnp.zeros_like(dv_ref)

        @pl.when((i == 0) & (h == 0))
        def _():
            dkr_ref[...] = jnp.zeros_like(dkr_ref)

        q, k = _load_qk(qn_ref, qr_ref, kn_ref, kr_ref, kcat_ref)
        dob = do_ref[...].astype(BF16)
        p = _probs(q, k)
        dv_ref[...] += _dg(p.astype(BF16), dob, 0, 0)
        dp = _dg(dob, v_ref[...], 1, 1)
        ds = (p * (dp - jnp.sum(dp * p, axis=-1, keepdims=True)) * _ATT_SCALE).astype(BF16)
        dq = _dg(ds, k, 1, 0)
        dqn_ref[...] = dq[:, :LANE]
        dqr_ref[...] = dq[:, LANE:]
        dk = _dg(ds, q, 0, 0)
        dkn_ref[...] += dk[:, :LANE]
        dkr_ref[...] += dk[:, LANE:]

    wide = jax.ShapeDtypeStruct((T, HEADS * LANE), F32)
    return pl.pallas_call(
        body, name="attn_bwd", grid=(HEADS, T // _TQ_B),
        in_specs=[_q_blk(_TQ_B), _q_blk(_TQ_B), _K_BLK, _KR_BLK, _K_BLK, _q_blk(_TQ_B)],
        out_specs=[_q_blk(_TQ_B), _q_blk(_TQ_B), _K_BLK, _KR_BLK, _K_BLK],
        out_shape=[wide, wide, wide, jax.ShapeDtypeStruct((T, LANE), F32), wide],
        scratch_shapes=[pltpu.VMEM((T, 2 * LANE), BF16)],
        compiler_params=pltpu.CompilerParams(dimension_semantics=("arbitrary", "arbitrary"), vmem_limit_bytes=VMEM_BIG),
    )(qn, qr, kn, kr, v, do)


def _chunk(r, lw, k, v, a, b, ht, *, reverse):
    hb, c, _ = r.shape
    ti = lax.broadcasted_iota(jnp.int32, (c, c), 0)
    si = lax.broadcasted_iota(jnp.int32, (c, c), 1)
    incl = (si >= ti) if reverse else (si <= ti)
    strict = (si > ti) if reverse else (si < ti)
    ones = jnp.broadcast_to(incl.astype(F32)[None], (hb, c, c))
    cum = cumdot(ones, lw)
    cum_ex = cum - lw
    tot = jnp.sum(lw, axis=1, keepdims=True)
    mid = 0.5 * tot
    rt, at = r * jnp.exp(cum - mid), a * jnp.exp(cum_ex - mid)
    einv = jnp.exp(mid - cum)
    bk = jnp.concatenate([b * einv, k * einv], axis=1)
    m_a, m_r = nt(at, bk), nt(rt, bk)
    m_ab = jnp.where(strict, m_a[:, :, :c], 0.0)
    m_ak = jnp.where(strict, m_a[:, :, c:], 0.0)
    t2 = lax.broadcasted_iota(jnp.int32, (c, 2 * c), 0)
    s2 = lax.broadcasted_iota(jnp.int32, (c, 2 * c), 1)
    s2 = jnp.where(s2 >= c, s2 - c, s2)
    m_r = jnp.where((s2 >= t2) if reverse else (s2 <= t2), m_r, 0.0)
    u = tri_solve(m_ab, nt(a * jnp.exp(cum_ex), ht) + nn(m_ak, v))
    uv = jnp.concatenate([u, v], axis=1)
    y = nt(r * jnp.exp(cum), ht) + nn(m_r, uv)
    eend = jnp.exp(tot - cum)
    ht_new = ht * jnp.exp(tot) + tn(uv, jnp.concatenate([b * eend, k * eend], axis=1))
    return y, ht_new


_HB_F, _HB_B = 16, 16


def _split_heads(x):
    return jnp.stack([x[:, i * RN:(i + 1) * RN] for i in range(x.shape[1] // RN)])


def _merge_heads(y):
    return jnp.concatenate([y[i] for i in range(y.shape[0])], axis=1)


def _chunk_map(reverse, backward):
    flip = reverse != backward
    return (lambda g, c: (NCH - 1 - c, g)) if flip else (lambda g, c: (c, g))


def scan_fwd(name, r, lw, k, v, a, b, reverse):
    hb = _HB_F
    cmap = _chunk_map(reverse, False)

    def body(r_ref, lw_ref, k_ref, v_ref, a_ref, b_ref, y_ref, h0_ref, ht_ref):
        @pl.when(pl.program_id(1) == 0)
        def _():
            ht_ref[...] = jnp.zeros_like(ht_ref)

        ht = ht_ref[...]
        h0_ref[0] = ht
        ins = [_split_heads(x[...]) for x in (r_ref, lw_ref, k_ref, v_ref, a_ref, b_ref)]
        y, hn = _chunk(*ins, ht, reverse=reverse)
        y_ref[...] = _merge_heads(y)
        ht_ref[...] = hn

    io = pl.BlockSpec((CHUNK, hb * RN), cmap)
    return pl.pallas_call(
        body, name=name, grid=(RH // hb, NCH),
        in_specs=[io] * 6,
        out_specs=[io, pl.BlockSpec((1, hb, RN, RN), lambda g, c: (cmap(g, c)[0], g, 0, 0))],
        out_shape=[jax.ShapeDtypeStruct((T, RW), F32), jax.ShapeDtypeStruct((NCH, RH, RN, RN), F32)],
        scratch_shapes=[pltpu.VMEM((hb, RN, RN), F32)],
        compiler_params=pltpu.CompilerParams(dimension_semantics=("parallel", "arbitrary"), vmem_limit_bytes=VMEM_BIG),
    )(r, lw, k, v, a, b)


def scan_bwd(name, r, lw, k, v, a, b, h0, dy, reverse):
    hb = _HB_B
    cmap = _chunk_map(reverse, True)

    def body(r_ref, lw_ref, k_ref, v_ref, a_ref, b_ref, h0_ref, dy_ref, *rest):
        d_refs, dht_ref = rest[:6], rest[6]

        @pl.when(pl.program_id(1) == 0)
        def _():
            dht_ref[...] = jnp.zeros_like(dht_ref)

        ins = [_split_heads(x[...]) for x in (r_ref, lw_ref, k_ref, v_ref, a_ref, b_ref)]
        _, vjp = jax.vjp(functools.partial(_chunk, reverse=reverse), *ins, h0_ref[0])
        grads = vjp((_split_heads(dy_ref[...]), dht_ref[...]))
        for d_ref, gval in zip(d_refs, grads[:6]):
            d_ref[...] = _merge_heads(gval).astype(d_ref.dtype)
        dht_ref[...] = grads[6]

    io = pl.BlockSpec((CHUNK, hb * RN), cmap)
    return pl.pallas_call(
        body, name=name, grid=(RH // hb, NCH),
        in_specs=[io] * 6 + [pl.BlockSpec((1, hb, RN, RN), lambda g, c: (cmap(g, c)[0], g, 0, 0)), io],
        out_specs=[io] * 6,
        out_shape=[jax.ShapeDtypeStruct((T, RW), F32 if i == 1 else BF16) for i in range(6)],
        scratch_shapes=[pltpu.VMEM((hb, RN, RN), F32)],
        compiler_params=pltpu.CompilerParams(dimension_semantics=("parallel", "arbitrary"), vmem_limit_bytes=VMEM_BIG),
    )(r, lw, k, v, a, b, h0, dy)


def loss_stage(out, x2, tgt, g_post):
    tr = 256

    def body(o_ref, x_ref, t_ref, g_ref, do_ref, dy_ref, dg_ref, loss_ref):
        @pl.when(pl.program_id(0) == 0)
        def _():
            dg_ref[...] = jnp.zeros_like(dg_ref)
            loss_ref[...] = jnp.zeros_like(loss_ref)

        nrm, vjp = jax.vjp(_rms, o_ref[...], g_ref[...])
        e = x_ref[...] + nrm - t_ref[...]
        s = jnp.sum(jnp.sum(e * e, axis=1, keepdims=True), axis=0, keepdims=True)
        loss_ref[...] += jnp.broadcast_to(s * (0.5 / D), loss_ref.shape)
        dy = e * (1.0 / D)
        do, dg = vjp(dy)
        do_ref[...] = do.astype(do_ref.dtype)
        dy_ref[...] = dy
        dg_ref[...] += dg

    row = pl.BlockSpec((tr, D), lambda i: (i, 0))
    return pl.pallas_call(
        body, name="loss_stage", grid=(T // tr,),
        in_specs=[row, row, row, pl.BlockSpec((1, D), lambda i: (0, 0))],
        out_specs=[row, row, pl.BlockSpec((1, D), lambda i: (0, 0)), pl.BlockSpec((8, LANE), lambda i: (0, 0))],
        out_shape=[jax.ShapeDtypeStruct((T, D), BF16), jax.ShapeDtypeStruct((T, D), F32),
                   jax.ShapeDtypeStruct((1, D), F32), jax.ShapeDtypeStruct((8, LANE), F32)],
        compiler_params=pltpu.CompilerParams(dimension_semantics=("arbitrary",), vmem_limit_bytes=VMEM_BIG),
    )(out, x2, tgt, g_post)


_EW_BLOCK_BYTES = 1 << 20


def _row_tile(rows, cols):
    best = None
    for tr in range(16, rows + 1, 16):
        if rows % tr == 0 and tr * cols * 4 <= _EW_BLOCK_BYTES:
            best = tr
    return best or rows


def _axis_tile(shape, axis, words):
    rows, cols = shape
    n, other, unit = (rows, cols, 16) if axis == 0 else (cols, rows, LANE)
    best = unit if n % unit == 0 else n
    for t in range(unit, n + 1, unit):
        if n % t == 0 and t * other * words * 4 <= _EW_BLOCK_BYTES:
            best = t
    blk = (best, cols) if axis == 0 else (rows, best)
    at = (lambda s: (s, 0)) if axis == 0 else (lambda s: (0, s))
    return blk, n // best, at


def _adamw_update(g, w_ref, m_ref, v_ref, g_ref, d_ref, nm_ref, nv_ref):
    mm = ADAM_B1 * m_ref[...] + (1.0 - ADAM_B1) * g
    vv = ADAM_B2 * v_ref[...] + (1.0 - ADAM_B2) * (g * g)
    m_hat = mm / (1.0 - ADAM_B1 ** ADAM_STEP)
    v_hat = vv / (1.0 - ADAM_B2 ** ADAM_STEP)
    g_ref[...] = g
    d_ref[...] = -ADAM_LR * (m_hat / (jnp.sqrt(v_hat) + ADAM_EPS) + ADAM_WD * w_ref[...])
    nm_ref[...] = mm
    nv_ref[...] = vv


def adamw(name, w, m, v, parts):
    rows, cols = w.shape
    br = _row_tile(rows, cols)
    npart = len(parts)

    def body(w_ref, m_ref, v_ref, *rest):
        g = rest[0][...].astype(F32)
        for p in rest[1:npart]:
            g = g + p[...].astype(F32)
        _adamw_update(g, w_ref, m_ref, v_ref, *rest[npart:])

    blk = pl.BlockSpec((br, cols), lambda i: (i, 0))
    return pl.pallas_call(
        body, name=name, grid=(rows // br,),
        in_specs=[blk] * (3 + npart), out_specs=[blk] * 4,
        out_shape=[jax.ShapeDtypeStruct((rows, cols), F32)] * 4,
        compiler_params=pltpu.CompilerParams(dimension_semantics=("parallel",), vmem_limit_bytes=VMEM_BIG),
    )(w, m, v, *parts)


def adamw_halves(name, place, w, m, v, mine, theirs, axis):
    half_shape = mine.shape
    blk_shape, nb, at = _axis_tile(half_shape, axis, 1)

    def body(p_ref, w_ref, m_ref, v_ref, a_ref, b_ref, *outs):
        own = (pl.program_id(0) // nb) == p_ref[0]
        _adamw_update(jnp.where(own, a_ref[...], b_ref[...]), w_ref, m_ref, v_ref, *outs)

    blk = pl.BlockSpec(blk_shape, lambda i, p: at(i))
    half = pl.BlockSpec(blk_shape, lambda i, p: at(i % nb))
    return pl.pallas_call(
        body, name=name,
        grid_spec=pltpu.PrefetchScalarGridSpec(num_scalar_prefetch=1, grid=(2 * nb,),
                                               in_specs=[blk] * 3 + [half] * 2, out_specs=[blk] * 4),
        out_shape=[jax.ShapeDtypeStruct(w.shape, F32)] * 4,
        compiler_params=pltpu.CompilerParams(dimension_semantics=("arbitrary",), vmem_limit_bytes=VMEM_BIG),
    )(place, w, m, v, mine, theirs)


def pair_sum(name, place, send, other, axis):
    blk_shape, nb, at = _axis_tile(other.shape[1:], axis, 4)

    def body(p_ref, a_ref, b_ref, o_ref):
        o_ref[...] = (a_ref[...].astype(F32) + b_ref[...].astype(F32)).astype(o_ref.dtype)

    blk = pl.BlockSpec((4,) + blk_shape, lambda i, p: (0,) + at(i))
    mine = pl.BlockSpec((4,) + blk_shape, lambda i, p: (0,) + at(p[0] * nb + i))
    return pl.pallas_call(
        body, name=name,
        grid_spec=pltpu.PrefetchScalarGridSpec(num_scalar_prefetch=1, grid=(nb,), in_specs=[mine, blk], out_specs=blk),
        out_shape=jax.ShapeDtypeStruct(other.shape, BF16),
        compiler_params=pltpu.CompilerParams(dimension_semantics=("arbitrary",), vmem_limit_bytes=VMEM_BIG),
    )(place, send, other)


def sum4(name, place, recv, own, axis):
    blk_shape, nb, at = _axis_tile(recv.shape[1:], axis, 4)

    def body(p_ref, r_ref, s_ref, o_ref):
        me = p_ref[0]
        t = [jnp.where(me == j, s_ref[j], r_ref[j]).astype(F32) for j in range(4)]
        o_ref[...] = ((t[0] + t[1]) + t[2]) + t[3]

    blk = pl.BlockSpec((4,) + blk_shape, lambda i, p: (0,) + at(i))
    return pl.pallas_call(
        body, name=name,
        grid_spec=pltpu.PrefetchScalarGridSpec(num_scalar_prefetch=1, grid=(nb,), in_specs=[blk, blk],
                                               out_specs=pl.BlockSpec(blk_shape, lambda i, p: at(i))),
        out_shape=jax.ShapeDtypeStruct(recv.shape[1:], F32),
        compiler_params=pltpu.CompilerParams(dimension_semantics=("arbitrary",), vmem_limit_bytes=VMEM_BIG),
    )(place, recv, own)


_ANY = pl.BlockSpec(memory_space=pl.ANY)


def _place():
    x, y, c = lax.axis_index("x"), lax.axis_index("y"), lax.axis_index("c")
    return x, y, c, 2 * x + y


def _chip_peers(x, y):
    out = []
    for k in (1, 2, 3):
        px = 1 - x if k & 2 else x
        py = 1 - y if k & 1 else y
        out.append((k, px, py, 2 * px + py))
    return out


def _half(c, shape, axis):
    n = shape[axis] // 2
    sl = pl.ds(pl.multiple_of(c * n, 16 if axis == 0 else LANE), n)
    return (sl,) if axis == 0 else (pl.ds(0, shape[0]), sl)


def gather_weights(srcs, axes):
    side = gather_side(srcs, axes)
    n = len(srcs)

    def body(*refs):
        ins, outs, sems = refs[:n], refs[n:2 * n], refs[2 * n:]
        side.run(ins, outs, sems)

    return pl.pallas_call(
        body, name="gather_weights", in_specs=[_ANY] * n, out_specs=[_ANY] * n,
        out_shape=side.outs, scratch_shapes=side.sems,
    )(*srcs)


def gather_side(srcs, axes):
    n = len(srcs)

    def copies(src, dst, sems, want):
        ssem, rsem, fssem, frsem = sems
        x, y, c, me = _place()
        sib = (x, y, 1 - c)
        out = []
        for i in range(n):
            mine, other = _half(c, srcs[i].shape, axes[i]), _half(1 - c, srcs[i].shape, axes[i])
            for k, px, py, peer in _chip_peers(x, y):
                sems_k = dict(send_sem=ssem.at[i, k - 1], recv_sem=rsem.at[i, k - 1], device_id=(px, py, c),
                              device_id_type=MESH_IDS)
                fsems = dict(send_sem=fssem.at[i, k - 1], recv_sem=frsem.at[i, k - 1], device_id=sib,
                             device_id_type=MESH_IDS)
                got = dst[i].at[(peer,) + mine]
                mk = pltpu.make_async_remote_copy
                made = dict(
                    snd=lambda: mk(src_ref=src[i].at[mine], dst_ref=dst[i].at[(me,) + mine], **sems_k),
                    rcv=lambda: mk(src_ref=src[i].at[mine], dst_ref=got, **sems_k),
                    fwd=lambda: mk(src_ref=got, dst_ref=got, **fsems),
                    frcv=lambda: mk(src_ref=got, dst_ref=dst[i].at[(peer,) + other], **fsems))
                out.append([made[w]() for w in want])
        return out

    def start(src, dst, sems):
        for (snd,) in copies(src, dst, sems, ("snd",)):
            snd.start()

    def finish(src, dst, sems):
        cps = copies(src, dst, sems, ("snd", "rcv", "fwd", "frcv"))
        for _, rcv, fwd, _ in cps:
            rcv.wait_recv()
            fwd.start()
        for snd, _, fwd, frcv in cps:
            frcv.wait_recv()
            snd.wait_send()
            fwd.wait_send()

    return Side(list(srcs), [jax.ShapeDtypeStruct((4,) + s.shape, s.dtype) for s in srcs],
                [pltpu.SemaphoreType.DMA((n, 3))] * 4, start, finish)


def pair_exchange(name, srcs, axes):
    n = len(srcs)

    def half_shape(s, axis):
        return (4, s.shape[1] // 2, s.shape[2]) if axis == 0 else (4, s.shape[1], s.shape[2] // 2)

    def body(*refs):
        src, other = refs[:n], refs[n:2 * n]
        ssem, rsem = refs[2 * n:]
        x, y, c, _ = _place()
        cps = []
        for i in range(n):
            idx = (pl.ds(0, 4),) + _half(1 - c, srcs[i].shape[1:], axes[i])
            cps.append(pltpu.make_async_remote_copy(
                src_ref=src[i].at[idx], dst_ref=other[i], send_sem=ssem.at[i], recv_sem=rsem.at[i],
                device_id=(x, y, 1 - c), device_id_type=MESH_IDS))
            cps[-1].start()
        for cp in cps:
            cp.wait()

    return pl.pallas_call(
        body, name=name, in_specs=[_ANY] * n, out_specs=[_ANY] * n,
        out_shape=[jax.ShapeDtypeStruct(half_shape(s, a), s.dtype) for s, a in zip(srcs, axes)],
        scratch_shapes=[pltpu.SemaphoreType.DMA((n,))] * 2,
    )(*srcs)


def scatter_side(srcs):
    n = len(srcs)

    def copies(src, dst, sems, sends_only=False):
        ssem, rsem = sems
        x, y, c, me = _place()
        out = []
        for i in range(n):
            for k, px, py, peer in _chip_peers(x, y):
                sems_k = dict(send_sem=ssem.at[i, k - 1], recv_sem=rsem.at[i, k - 1], device_id=(px, py, c),
                              device_id_type=MESH_IDS)
                snd = pltpu.make_async_remote_copy(src_ref=src[i].at[peer], dst_ref=dst[i].at[me], **sems_k)
                if sends_only:
                    out.append(snd)
                    continue
                out.append((snd, pltpu.make_async_remote_copy(src_ref=src[i].at[peer], dst_ref=dst[i].at[peer],
                                                              **sems_k)))
        return out

    def start(src, dst, sems):
        for snd in copies(src, dst, sems, sends_only=True):
            snd.start()

    def finish(src, dst, sems):
        for snd, rcv in copies(src, dst, sems):
            rcv.wait_recv()
            snd.wait_send()

    return Side(list(srcs), [jax.ShapeDtypeStruct(s.shape, s.dtype) for s in srcs],
                [pltpu.SemaphoreType.DMA((n, 3))] * 2, start, finish)


_HBM = pl.BlockSpec(memory_space=pltpu.HBM)
_SEM = pl.BlockSpec(memory_space=pltpu.SEMAPHORE)
_DATAFLOW = pltpu.SideEffectType.DATAFLOW_SIDE_EFFECTING


def scatter_start(name, srcs):
    n = len(srcs)
    side = scatter_side(srcs)
    ns = 3 * n

    def body(*refs):
        src, land = refs[:n], refs[n:2 * n]
        sems = refs[2 * n:2 * n + 2 * ns]
        side.start(src, land, (_SemGrid(sems[:ns]), _SemGrid(sems[ns:])))
        refs[-1][...] = jnp.zeros_like(refs[-1])

    hbm = [pltpu.HBM(s.shape, s.dtype) for s in srcs]
    res = pl.pallas_call(
        body, name=name,
        out_shape=[pltpu.SemaphoreType.DMA(())] * (2 * ns) + hbm + hbm + [jax.ShapeDtypeStruct((8, LANE), F32)],
        in_specs=[_HBM] * (2 * n),
        out_specs=[_SEM] * (2 * ns) + [_HBM] * (2 * n) + [pl.BlockSpec(memory_space=pltpu.VMEM)],
        input_output_aliases={i: 2 * ns + i for i in range(2 * n)},
        compiler_params=pltpu.CompilerParams(has_side_effects=_DATAFLOW),
    )(*[pltpu.with_memory_space_constraint(s, pltpu.HBM) for s in srcs],
      *[pltpu.with_memory_space_constraint(lax.empty(s.shape, s.dtype), pltpu.HBM) for s in srcs])
    return res[:2 * ns], res[2 * ns:2 * ns + n], res[2 * ns + n:2 * ns + 2 * n], res[-1]


def scatter_wait(name, sems, srcs, lands, after):
    n = len(srcs)
    side = scatter_side(srcs)
    ns = 3 * n

    def body(*refs):
        src, land = refs[:n], refs[n:2 * n]
        s = refs[2 * n:2 * n + 2 * ns]
        side.finish(src, land, (_SemGrid(s[:ns]), _SemGrid(s[ns:])))

    hbm = [pltpu.HBM(s.shape, s.dtype) for s in srcs]
    res = pl.pallas_call(
        body, name=name, out_shape=hbm + hbm,
        in_specs=[_HBM] * (2 * n) + [_SEM] * (2 * ns) + [_ANY], out_specs=[_HBM] * (2 * n),
        input_output_aliases={i: i for i in range(2 * n)},
        compiler_params=pltpu.CompilerParams(has_side_effects=_DATAFLOW),
    )(*srcs, *lands, *sems, after)
    return res[:n], res[n:]


class _SemGrid:
    def __init__(self, sems):
        self.sems = sems

    @property
    def at(self):
        return self

    def __getitem__(self, ik):
        return self.sems[3 * ik[0] + ik[1]]


def swap_halves(name, srcs):
    n = len(srcs)

    def body(*refs):
        src, dst = refs[:n], refs[n:2 * n]
        ssem, rsem = refs[2 * n:]
        x, y, c, _ = _place()
        cps = []
        for i in range(n):
            cps.append(pltpu.make_async_remote_copy(src_ref=src[i], dst_ref=dst[i], send_sem=ssem.at[i],
                                                    recv_sem=rsem.at[i], device_id=(x, y, 1 - c),
                                                    device_id_type=MESH_IDS))
            cps[-1].start()
        for cp in cps:
            cp.wait()

    return pl.pallas_call(
        body, name=name, in_specs=[_ANY] * n, out_specs=[_ANY] * n,
        out_shape=[jax.ShapeDtypeStruct(s.shape, s.dtype) for s in srcs],
        scratch_shapes=[pltpu.SemaphoreType.DMA((n,))] * 2,
    )(*srcs)


def _ag8_copies(src, dst, sems, sends_only=False):
    x, y, c = lax.axis_index("x"), lax.axis_index("y"), lax.axis_index("c")
    me = 4 * x + 2 * y + c
    out = []
    for k in range(1, 8):
        px = 1 - x if k & 4 else x
        py = 1 - y if k & 2 else y
        pc = 1 - c if k & 1 else c
        peer = 4 * px + 2 * py + pc
        out.append(tuple(pltpu.make_async_remote_copy(
            src_ref=src, dst_ref=dst.at[slot], send_sem=sems[k - 1], recv_sem=sems[7 + k - 1],
            device_id=(px, py, pc), device_id_type=MESH_IDS) for slot in ((me,) if sends_only else (me, peer))))
    return out


def allgather8_start(name, src):
    def body(src_ref, land_ref, *rest):
        for (snd,) in _ag8_copies(src_ref, land_ref, rest[:14], sends_only=True):
            snd.start()

    land = jax.ShapeDtypeStruct((8,) + src.shape, src.dtype)
    res = pl.pallas_call(
        body, name=name,
        out_shape=[pltpu.SemaphoreType.DMA(())] * 14 + [pltpu.HBM(src.shape, src.dtype), pltpu.HBM(land.shape, land.dtype)],
        in_specs=[_HBM, _HBM], out_specs=[_SEM] * 14 + [_HBM, _HBM],
        input_output_aliases={0: 14, 1: 15},
        compiler_params=pltpu.CompilerParams(has_side_effects=_DATAFLOW),
    )(pltpu.with_memory_space_constraint(src, pltpu.HBM),
      pltpu.with_memory_space_constraint(lax.empty(land.shape, land.dtype), pltpu.HBM))
    return res[:14], res[14], res[15]


def allgather8_wait(name, sems, src, land, after):
    def body(src_ref, land_ref, *rest):
        for snd, rcv in _ag8_copies(src_ref, land_ref, rest[:14]):
            rcv.wait_recv()
            snd.wait_send()

    return pl.pallas_call(
        body, name=name, out_shape=[pltpu.HBM(src.shape, src.dtype), pltpu.HBM(land.shape, land.dtype)],
        in_specs=[_HBM, _HBM] + [_SEM] * 14 + [_ANY], out_specs=[_HBM, _HBM],
        input_output_aliases={0: 0, 1: 1},
        compiler_params=pltpu.CompilerParams(has_side_effects=_DATAFLOW),
    )(src, land, *sems, after)


WEIGHTS = ['g_pre', 'w_in', 'mla_q_norm', 'mla_wq_b', 'mla_kv_norm', 'mla_wkv_b', 'rwkv_mu', 'rwkv_w0_f', 'rwkv_w2_f',
           'rwkv_w0_b', 'rwkv_w2_b', 'rwkv_a0_f', 'rwkv_a2_f', 'rwkv_a0_b', 'rwkv_a2_b', 'rwkv_k_k', 'rwkv_k_a',
           'rwkv_r_k', 'rwkv_gn_g', 'rwkv_gn_b', 'w_br_mla', 'w_br_rwkv', 'w_out', 'g_post']
BIG_SHAPES = {'w_in': (D_IN // 4, D), 'mla_wq_b': (Q_RANK, 384), 'mla_wkv_b': (KV_RANK, 512),
              'rwkv_w2_f': (LORA, 256), 'rwkv_w2_b': (LORA, 256), 'rwkv_a2_f': (LORA, 256), 'rwkv_a2_b': (LORA, 256),
              'w_br_mla': (RW, 512), 'w_br_rwkv': (RW, 512), 'w_out': (512, D)}
BIG = list(BIG_SHAPES)
SMALL = [n for n in WEIGHTS if n not in BIG_SHAPES]
SMALL_SHAPES = {'g_pre': (D,), 'mla_q_norm': (Q_RANK,), 'mla_kv_norm': (KV_RANK,), 'rwkv_mu': (3456,),
                'rwkv_w0_f': (RW,), 'rwkv_w0_b': (RW,), 'rwkv_a0_f': (RW,), 'rwkv_a0_b': (RW,), 'rwkv_k_k': (RW,),
                'rwkv_k_a': (RW,), 'rwkv_r_k': (RH, RN), 'rwkv_gn_g': (RW,), 'rwkv_gn_b': (RW,), 'g_post': (D,)}
SMALL_LEN = sum(int(np.prod(s)) for s in SMALL_SHAPES.values())
SMALL_ROWS = 144


UNITS = [('w_in',), ('mla_wq_b',), ('mla_wkv_b',), ('rwkv_w2_f', 'rwkv_w2_b', 'rwkv_a2_f', 'rwkv_a2_b'),
         ('w_br_mla', 'w_br_rwkv'), ('w_out',)]
UNIT_AXIS = [1, 0, 0, 0, 0, 0]
ROW_SHARDED = ('w_in', 'w_out')


def _unit_cat(parts):
    return parts[0] if len(parts) == 1 else jnp.concatenate(parts, axis=0)


def _unit_split(arr, names, axis):
    out, o = {}, 0
    for n in names:
        rows = BIG_SHAPES[n][0]
        out[n] = lax.slice_in_dim(arr, o, o + rows, axis=axis)
        o += rows
    return out


def _gathered(units, ag, own, me):
    out = {}
    for names, arr, mine in zip(units, ag, own):
        slots = [jnp.where(me == j, mine, arr[j]) for j in range(4)]
        for n in names:
            parts = [_unit_split(s, names, 0)[n] for s in slots]
            out[n] = jnp.concatenate(parts, axis=0 if n in ROW_SHARDED else 1)
    return out


def _shards(n, g):
    r, w = BIG_SHAPES[n]
    if n in ROW_SHARDED:
        return [g[j * r:(j + 1) * r] for j in range(4)]
    return [g[:, j * w:(j + 1) * w] for j in range(4)]


def _pack_small(d, extra=None):
    flat = jnp.concatenate([d[n].reshape(-1) for n in SMALL] + ([extra.reshape(-1)] if extra is not None else []))
    return jnp.pad(flat, (0, SMALL_ROWS * LANE - flat.shape[0])).reshape(SMALL_ROWS, LANE)


def _unpack_small(packed):
    flat, out, o = packed.reshape(-1), {}, 0
    for n in SMALL:
        sz = int(np.prod(SMALL_SHAPES[n]))
        out[n] = flat[o:o + sz].reshape(SMALL_SHAPES[n])
        o += sz
    return out


def _perm_w_in(gathered, own, me):
    per = D_IN // 4

    def rows(a, b):
        out = []
        while a < b:
            j, lo = divmod(a, per)
            hi = min(b - j * per, per)
            out.append(jnp.where(me == j, own[lo:hi], gathered[j, lo:hi]))
            a = j * per + hi
        return out

    z = lambda n: [jnp.zeros((n, own.shape[1]), own.dtype)]
    lora = []
    for i in range(4):
        lora += rows(4160 + LORA * i, 4160 + LORA * (i + 1)) + z(LANE - LORA)
    return jnp.concatenate(rows(0, 1024) + rows(1088, 4160) + rows(4544, D_IN) + lora + rows(1024, 1088)
                           + z(256 - ROPE), axis=0)


def _unperm_w_in(g):
    lora = [g[OFF_LORA + LANE * i:OFF_LORA + LANE * i + LORA] for i in range(4)]
    return jnp.concatenate([g[0:1024], g[OFF_KR:OFF_KR + ROPE], g[1024:4096]] + lora + [g[4096:OFF_LORA]], axis=0)


def _perm_wq(w):
    w3 = w.reshape(Q_RANK, HEADS, NOPE + ROPE)
    rope = jnp.pad(w3[:, :, NOPE:], ((0, 0), (0, 0), (0, LANE - ROPE)))
    return jnp.concatenate([w3[:, :, :NOPE].reshape(Q_RANK, -1), rope.reshape(Q_RANK, -1)], axis=1)


def _unperm_wq(g):
    return jnp.concatenate([g[:, :1024].reshape(Q_RANK, HEADS, NOPE),
                            g[:, 1024:].reshape(Q_RANK, HEADS, LANE)[:, :, :ROPE]], axis=2).reshape(Q_RANK, -1)


def _perm_wkv(w):
    w3 = w.reshape(KV_RANK, HEADS, NOPE + VDIM)
    return jnp.concatenate([w3[:, :, :NOPE].reshape(KV_RANK, -1), w3[:, :, NOPE:].reshape(KV_RANK, -1)], axis=1)


def _unperm_wkv(g):
    return jnp.concatenate([g[:, :1024].reshape(KV_RANK, HEADS, NOPE), g[:, 1024:].reshape(KV_RANK, HEADS, VDIM)],
                           axis=2).reshape(KV_RANK, -1)


def _pad_rows(w):
    return jnp.pad(w, ((0, LANE - LORA), (0, 0)))


def _perm_mu(mu):
    parts = [mu[:3072]]
    for i in range(4):
        parts += [mu[3072 + LORA * i:3072 + LORA * (i + 1)], jnp.zeros((LANE - LORA,), mu.dtype)]
    return jnp.concatenate(parts).reshape(1, NLERP)


def _unperm_mu(g):
    g = g.reshape(-1)
    return jnp.concatenate([g[:3072]] + [g[3072 + LANE * i:3072 + LANE * i + LORA] for i in range(4)])


def _constants():
    g2 = np.kron(np.eye(2, dtype=np.float32), np.ones((RN, RN), np.float32))
    pos = jnp.arange(T, dtype=F32)
    inv_freq = jnp.power(ROPE_THETA, -jnp.arange(0, ROPE, 2, dtype=F32) / ROPE)
    ang = pos[:, None] * inv_freq[None, :]
    cos, sin, zero = jnp.cos(ang), jnp.sin(ang), jnp.zeros((T, LANE - ROPE), F32)
    cq = jnp.tile(jnp.concatenate([cos, cos, zero], axis=1), (1, HEADS))
    sq = jnp.tile(jnp.concatenate([-sin, sin, zero], axis=1), (1, HEADS))
    return jnp.asarray(g2, BF16), cq, sq


def _step(x, tgt, w, m, v):
    x2, tgt2 = x.reshape(T, D), tgt.reshape(T, D)
    g2, cq, sq = _constants()
    row = lambda n: w[n].reshape(1, -1)
    w, m, v = ({**t, 'w_in': t['w_in'].T} for t in (w, m, v))

    core, chip = lax.axis_index("c"), 2 * lax.axis_index("x") + lax.axis_index("y")
    core1, chip1 = core.astype(jnp.int32).reshape(1), chip.astype(jnp.int32).reshape(1)
    own_bf = [_unit_cat([w[n].astype(BF16) for n in u]) for u in UNITS]
    wp = _perm_w_in(gather_weights(own_bf[:1], UNIT_AXIS[:1])[0], own_bf[0], chip)
    full = {}
    mu_p = _perm_mu(w['rwkv_mu'])

    st_pre = Stage("pre", f_pre, [(D, BF16), (D, None)], 256, [0], [0], [F32])
    st_mla = Stage("mla", f_mla, [(1024, BF16), (1024, BF16), (1024, BF16), (LANE, BF16), (1024, BF16)], 256,
                   [0, 1, 2], [0, 1, 2, 3], [BF16] * 3)
    st_rpre = Stage("rwkv_pre", f_rwkv_pre, [(RW, F32)] * 9, 256, [0], list(range(10)), [F32])
    st_rpost = Stage("rwkv_post", f_rwkv_post, [(RW, BF16)], 256, [0, 2, 3, 4, 5, 6], [0, 1, 2],
                     [F32, F32, F32, F32, F32, BF16])
    st_gate = Stage("gate", f_gate, [(RW, BF16)], 256, [0, 1], [], [F32, BF16])
    st_merge = Stage("merge", f_merge, [(D, BF16)], 256, [0, 1, 2, 3], [], [BF16] * 4)

    pre_rows, pre_par = [(x2, D, 0)], [row('g_pre')]
    (h,) = st_pre.fwd(pre_rows, pre_par)
    proj, rest = matmul("mm_in", h, wp, "nt", side=gather_side(own_bf[1:], UNIT_AXIS[1:]))
    full.update(_gathered(UNITS[1:], rest, own_bf[1:], chip))
    wq, wkv = _perm_wq(full['mla_wq_b']), _perm_wkv(full['mla_wkv_b'])
    lora_w = [_pad_rows(full[n]).astype(F32) for n in ('rwkv_w2_f', 'rwkv_w2_b', 'rwkv_a2_f', 'rwkv_a2_b')]

    mla_rows = [(proj, 512, OFF_QA // 512), (proj, 512, OFF_KVA // 512), (proj, 256, OFF_KR // 256),
                (cq, 1024, 0), (sq, 1024, 0), (cq, LANE, 0), (sq, LANE, 0)]
    mla_par = [row('mla_q_norm'), row('mla_kv_norm'), wq, wkv]
    att = st_mla.fwd(mla_rows, mla_par)
    y_mla = attn_fwd(*att)

    lerp = shift_fwd(proj, mu_p)
    rpre_rows = [(lerp, NLERP, 0)]
    rpre_par = [row('rwkv_w0_f'), row('rwkv_w0_b'), row('rwkv_a0_f'), row('rwkv_a0_b'), row('rwkv_k_k'),
                row('rwkv_k_a')] + lora_w + [g2]
    r_, v_, lwf, lwb, kf, kb, an, bf_, bb_ = st_rpre.fwd(rpre_rows, rpre_par)
    fin = [r_, lwf, kf, v_, an, bf_]
    bin_ = [r_, lwb, kb, v_, an, bb_]
    yf, h0f = scan_fwd("scan_f", *fin, reverse=False)
    yb, h0b = scan_fwd("scan_b", *bin_, reverse=True)
    rpost_rows = [(yf, RW, 0), (yb, RW, 0), (r_, RW, 0), (kf, RW, 0), (kb, RW, 0),
                  (v_, RW, 0), (proj, RW, OFF_ZR // RW)]
    rpost_par = [row('rwkv_gn_g'), row('rwkv_gn_b'), row('rwkv_r_k'), g2]
    (gr,) = st_rpost.fwd(rpost_rows, rpost_par)
    gate_rows = [(y_mla, RW, 0), (proj, RW, OFF_ZM // RW)]
    (gm,) = st_gate.fwd(gate_rows, [])
    um = matmul("mm_br_mla", gm, full['w_br_mla'], "nn")
    ur = matmul("mm_br_rwkv", gr, full['w_br_rwkv'], "nn")
    merge_rows = [(um, D, 0), (ur, D, 0), (proj, D, OFF_GM // D), (proj, D, OFF_GR // D)]
    (merged,) = st_merge.fwd(merge_rows, [])
    out = matmul("mm_out", merged, full['w_out'], "nn")
    d_out, dy, dg_post, loss_blk = loss_stage(out, x2, tgt2, row('g_post'))

    gw = {'g_post': dg_post}
    d_merged = matmul("mm_out_dx", d_out, full['w_out'], "nt")
    gw['w_out'] = matmul("mm_out_dw", merged, d_out, "tn")
    (d_um, d_ur, d_gm, d_gr), _ = st_merge.bwd(merge_rows, [], [[(d_merged, D, 0)]])
    d_gmla = matmul("mm_br_mla_dx", d_um, full['w_br_mla'], "nt")
    gw['w_br_mla'] = matmul("mm_br_mla_dw", gm, d_um, "tn")
    d_grw = matmul("mm_br_rwkv_dx", d_ur, full['w_br_rwkv'], "nt")
    gw['w_br_rwkv'] = matmul("mm_br_rwkv_dw", gr, d_ur, "tn")
    (d_ymla, d_zm), _ = st_gate.bwd(gate_rows, [], [[(d_gmla, RW, 0)]])
    (d_y, d_r3, d_kf2, d_kb2, d_v3, d_zr), (gw['rwkv_gn_g'], gw['rwkv_gn_b'], d_rk) = st_rpost.bwd(
        rpost_rows, rpost_par, [[(d_grw, RW, 0)]])
    gw['rwkv_r_k'] = d_rk
    sf = scan_bwd("scan_f_bwd", *fin, h0f, d_y, reverse=False)
    sb = scan_bwd("scan_b_bwd", *bin_, h0b, d_y, reverse=True)
    c = lambda *ts: [(t, RW, 0) for t in ts]
    rpre_cts = [c(sf[0], sb[0], d_r3), c(sf[3], sb[3], d_v3), c(sf[1]), c(sb[1]), c(sf[2], d_kf2), c(sb[2], d_kb2),
                c(sf[4], sb[4]), c(sf[5]), c(sb[5])]
    (d_rin,), rpre_g = st_rpre.bwd(rpre_rows, rpre_par, rpre_cts)
    for n, gval in zip(('rwkv_w0_f', 'rwkv_w0_b', 'rwkv_a0_f', 'rwkv_a0_b', 'rwkv_k_k', 'rwkv_k_a'), rpre_g[:6]):
        gw[n] = gval
    for n, gval in zip(('rwkv_w2_f', 'rwkv_w2_b', 'rwkv_a2_f', 'rwkv_a2_b'), rpre_g[6:]):
        gw[n] = gval[:LORA]
    d_lerp, d_mu = shift_bwd(proj, mu_p, d_rin)
    gw['rwkv_mu'] = _unperm_mu(d_mu)

    mla_cts = [[(t, t.shape[1], 0)] for t in attn_bwd(*att, d_ymla)]
    (d_qa, d_kva, d_kr), (gw['mla_q_norm'], gw['mla_kv_norm'], d_wq, d_wkv) = st_mla.bwd(mla_rows, mla_par, mla_cts)
    gw['mla_wq_b'], gw['mla_wkv_b'] = _unperm_wq(d_wq), _unperm_wkv(d_wkv)

    dproj = jnp.concatenate([d_qa, d_kva, d_lerp[:, :3072], d_zm, d_zr, d_gm, d_gr, d_lerp[:, 3072:], d_kr], axis=1)

    def pair_sums(name, ids):
        send = [jnp.stack([_unit_cat([_shards(n, gw[n])[j].astype(BF16) for n in UNITS[i]]) for j in range(4)])
                for i in ids]
        axes = [UNIT_AXIS[i] for i in ids]
        other = pair_exchange(name, send, axes)
        return [pair_sum(f"pair_sum_{i}", core1, s, o, ax) for i, s, o, ax in zip(ids, send, other, axes)]

    late, early = [0], list(range(1, len(UNITS)))
    pairs_e = pair_sums("pair_exchange_rest", early)
    gw_in, recv_e = matmul("mm_in_dw", dproj, h, "tn", BF16, side=scatter_side(pairs_e))
    gw['w_in'] = _unperm_w_in(gw_in)
    pairs_l = pair_sums("pair_exchange_w_in", late)
    sems, src_fly, land_fly, token = scatter_start("scatter_w_in_start", pairs_l)
    dh = matmul("mm_in_dx", dproj, wp, "nn", after=(token,))
    (grad_x,), (gw['g_pre'],) = st_pre.bwd(pre_rows, pre_par, [[(dh, D, 0)], [(dy, D, 0)]])

    big = [dict() for _ in range(4)]

    def update(name, ids, recv, pairs):
        mine = [sum4(f"sum4_{i}", chip1, r, p, UNIT_AXIS[i]) for i, r, p in zip(ids, recv, pairs)]
        theirs = swap_halves(name, mine)
        for i, mi, th in zip(ids, mine, theirs):
            res = adamw_halves(f"adamw_{i}", core1, *[_unit_cat([t[n] for n in UNITS[i]]) for t in (w, m, v)], mi, th,
                               UNIT_AXIS[i])
            for q in range(4):
                big[q].update(_unit_split(res[q], UNITS[i], 0))
        return res

    small_fly = allgather8_start("gather_small_start", _pack_small(gw, loss_blk[0, :1]))
    last = update("swap_halves_rest", early, recv_e, pairs_e)
    own_small, landed = allgather8_wait("gather_small_wait", *small_fly, last[0])
    dev = 2 * chip + core
    parts = [jnp.where(dev == i, own_small, landed[i]) for i in range(8)]
    small = adamw("adamw_small", _pack_small(w), _pack_small(m), _pack_small(v), parts)
    pairs_l, recv_l = scatter_wait("scatter_w_in_wait", sems, src_fly, land_fly, small[0] + last[0][:1, :1])
    update("swap_halves_w_in", late, recv_l, pairs_l)

    outs = []
    for b_d, s_arr in zip(big, small):
        d = {**b_d, **_unpack_small(s_arr)}
        d['w_in'] = d['w_in'].T
        outs.append([d[n] for n in WEIGHTS])
    loss = small[0][SMALL_LEN // LANE, 0]
    return (loss, grad_x.reshape(1, T, D), *outs[0], *outs[1], *outs[2], *outs[3])


def kernel(x, g_pre, w_in, mla_q_norm, mla_wq_b, mla_kv_norm, mla_wkv_b, rwkv_mu, rwkv_w0_f, rwkv_w2_f, rwkv_w0_b, rwkv_w2_b, rwkv_a0_f, rwkv_a2_f, rwkv_a0_b, rwkv_a2_b, rwkv_k_k, rwkv_k_a, rwkv_r_k, rwkv_gn_g, rwkv_gn_b, w_br_mla, w_br_rwkv, w_out, g_post, loss_target, m_g_pre, m_w_in, m_mla_q_norm, m_mla_wq_b, m_mla_kv_norm, m_mla_wkv_b, m_rwkv_mu, m_rwkv_w0_f, m_rwkv_w2_f, m_rwkv_w0_b, m_rwkv_w2_b, m_rwkv_a0_f, m_rwkv_a2_f, m_rwkv_a0_b, m_rwkv_a2_b, m_rwkv_k_k, m_rwkv_k_a, m_rwkv_r_k, m_rwkv_gn_g, m_rwkv_gn_b, m_w_br_mla, m_w_br_rwkv, m_w_out, m_g_post, v_g_pre, v_w_in, v_mla_q_norm, v_mla_wq_b, v_mla_kv_norm, v_mla_wkv_b, v_rwkv_mu, v_rwkv_w0_f, v_rwkv_w2_f, v_rwkv_w0_b, v_rwkv_w2_b, v_rwkv_a0_f, v_rwkv_a2_f, v_rwkv_a0_b, v_rwkv_a2_b, v_rwkv_k_k, v_rwkv_k_a, v_rwkv_r_k, v_rwkv_gn_g, v_rwkv_gn_b, v_w_br_mla, v_w_br_rwkv, v_w_out, v_g_post):
    given = dict(locals())
    w = {n: given[n] for n in WEIGHTS}
    m = {n: given['m_' + n] for n in WEIGHTS}
    v = {n: given['v_' + n] for n in WEIGHTS}
    return _step(x, loss_target, w, m, v)
```

```python
import functools
import math

import numpy as np
import jax
import jax.numpy as jnp
from jax import lax
from jax.experimental import pallas as pl
from jax.experimental.pallas import tpu as pltpu

F32, BF16 = jnp.float32, jnp.bfloat16
MESH_IDS = pl.DeviceIdType.MESH

D = 2048
T = 2048
HEADS = 8
Q_RANK = 512
KV_RANK = 512
NOPE = 128
ROPE = 64
VDIM = 128
RW = 1024
RH = 16
RN = 64
LORA = 96
D_IN = 10688
NORM_EPS = 1e-6
GN_EPS = 64e-5
ROPE_THETA = 10000.0
ADAM_LR, ADAM_B1, ADAM_B2, ADAM_EPS, ADAM_WD, ADAM_STEP = 0.001, 0.9, 0.999, 1e-08, 0.01, 10

LANE = 128
VMEM_BIG = 56 * 2**20

NP = 11008
OFF_QA, OFF_KVA, OFF_RKV, OFF_ZM, OFF_ZR, OFF_GM, OFF_GR, OFF_LORA, OFF_KR = 0, 512, 1024, 4096, 5120, 6144, 8192, 10240, 10752
NLERP = 3584

CHUNK = 64
NCH = T // CHUNK


def _dg(a, b, ca, cb, batch=False, prec=None):
    bd = ((0,), (0,)) if batch else ((), ())
    return lax.dot_general(a, b, (((ca,), (cb,)), bd), precision=prec, preferred_element_type=F32)


@jax.custom_vjp
def bdot(a, b):
    return _dg(a.astype(BF16), b.astype(BF16), 1, 0)


def _bdot_fwd(a, b):
    return bdot(a, b), (a, b)


def _bdot_bwd(res, g):
    a, b = res
    gb = g.astype(BF16)
    da = _dg(gb, b.astype(BF16), 1, 1)
    db = _dg(a.astype(BF16), gb, 0, 0)
    return da.astype(a.dtype), db.astype(b.dtype)


bdot.defvjp(_bdot_fwd, _bdot_bwd)


def _split(x):
    hi = x.astype(BF16)
    lo = (x - hi.astype(F32)).astype(BF16)
    return hi, lo


@jax.custom_vjp
def gsum(x, g2):
    hi, lo = _split(x)
    return _dg(hi, g2, 1, 0) + _dg(lo, g2, 1, 0)


def _gsum_fwd(x, g2):
    return gsum(x, g2), g2


def _gsum_bwd(g2, g):
    hi, lo = _split(g)
    return _dg(hi, g2, 1, 1) + _dg(lo, g2, 1, 1), jnp.zeros_like(g2)


gsum.defvjp(_gsum_fwd, _gsum_bwd)


def headsum(x, g2):
    return jnp.concatenate([gsum(x[:, i * LANE:(i + 1) * LANE], g2) for i in range(x.shape[1] // LANE)], axis=1)


def _terms(x, n):
    out = []
    for i in range(n):
        t = x.astype(BF16)
        out.append(t)
        if i < n - 1:
            x = x - t.astype(F32)
    return out


def _bmm(a, b, ca, cb, na, nb):
    acc = None
    for i, ai in enumerate(_terms(a, na)):
        for j, bj in enumerate(_terms(b, nb)):
            if i + j < max(na, nb):
                p = _dg(ai, bj, ca, cb, True)
                acc = p if acc is None else acc + p
    return acc


_NN, _NT, _TN = (2, 1), (2, 2), (1, 1)


def _make_dots(nf, nb_nn, nb_nt, nb_tn):
    @jax.custom_vjp
    def nn(a, b):
        return _bmm(a, b, *_NN, nf, nf)

    @jax.custom_vjp
    def nt(a, b):
        return _bmm(a, b, *_NT, nf, nf)

    @jax.custom_vjp
    def tn(a, b):
        return _bmm(a, b, *_TN, nf, nf)

    nn.defvjp(lambda a, b: (nn(a, b), (a, b)),
              lambda r, g: (_bmm(g, r[1], *_NT, nb_nn, nb_nn), _bmm(r[0], g, *_TN, nb_nn, nb_nn)))
    nt.defvjp(lambda a, b: (nt(a, b), (a, b)),
              lambda r, g: (_bmm(g, r[1], *_NN, 1, nb_nt), _bmm(g, r[0], *_TN, 1, nb_nt)))
    tn.defvjp(lambda a, b: (tn(a, b), (a, b)),
              lambda r, g: (_bmm(r[1], g, *_NT, nb_tn, nb_tn), _bmm(r[0], g, *_NN, nb_tn, nb_tn)))
    return nn, nt, tn


_SCAN_NF, _SCAN_NB = 1, 1
nn, nt, tn = _make_dots(_SCAN_NF, 1, 2, 1)


@jax.custom_vjp
def cumdot(ones, x):
    return _bmm(ones, x, *_NN, 1, 3)


cumdot.defvjp(lambda o, x: (cumdot(o, x), o), lambda o, g: (jnp.zeros_like(o), _bmm(o, g, *_TN, 1, 3)))


def _solve_powers(l):
    pw = [l]
    for _ in range(int(math.log2(l.shape[-1])) - 1):
        pw.append(_bmm(pw[-1], pw[-1], *_NN, _SCAN_NF, _SCAN_NF))
    return pw


@jax.custom_vjp
def tri_solve(l, rhs):
    x = rhs
    for p in _solve_powers(l):
        x = x + _bmm(p, x, *_NN, _SCAN_NF, _SCAN_NF)
    return x


def _tri_solve_fwd(l, rhs):
    pw = _solve_powers(l)
    x = rhs
    for p in pw:
        x = x + _bmm(p, x, *_NN, _SCAN_NF, _SCAN_NF)
    return x, (pw, x)


def _tri_solve_bwd(res, g):
    pw, x = res
    y = g
    for p in pw:
        y = y + _bmm(p, y, *_TN, _SCAN_NB, _SCAN_NB)
    return _bmm(y, x, *_NT, _SCAN_NB, _SCAN_NB), y


tri_solve.defvjp(_tri_solve_fwd, _tri_solve_bwd)


@jax.custom_vjp
def known_solve(l, rhs, x):
    return x


known_solve.defvjp(lambda l, rhs, x: (x, (_solve_powers(l), x)),
                   lambda res, g: _tri_solve_bwd(res, g) + (jnp.zeros_like(g),))


def _rms(x, g):
    return x * lax.rsqrt(jnp.mean(x * x, axis=-1, keepdims=True) + NORM_EPS) * g


def _softplus(x):
    pos = x > 0
    return jnp.where(pos, x, 0.0) + jnp.log(1.0 + jnp.exp(-jnp.where(pos, x, -x)))


def _silu(z):
    return z * jax.nn.sigmoid(z)


_MM_VMEM_BYTES = 32 * 2**20


def _mm_tiles(m, n, k):
    best = None
    for tm in (2048, 1024, 512, 256):
        for tn_ in (2048, 1024, 512, 256):
            for d in range(k // LANE, 0, -1):
                tk = LANE * d
                if m % tm or n % tn_ or k % tk:
                    continue
                nk = k // tk
                vmem = 4 * tk * (tm + tn_) + 8 * tm * tn_ + (4 * tm * tn_ if nk > 1 else 0)
                if vmem > _MM_VMEM_BYTES:
                    continue
                a_reads = n // tn_ if nk > 1 else 1
                b_reads = 1 if (nk == 1 and n == tn_) else m // tm
                acc_rmw = nk * m * n if nk > 1 else 0
                cost = (a_reads * m * k + b_reads * k * n + acc_rmw, -tm * tn_ * tk)
                if best is None or cost < best[0]:
                    best = (cost, (tm, tn_, tk))
    return best[1]


class Side:
    def __init__(self, ins, outs, sems, start, finish):
        self.ins, self.outs, self.sems, self.start, self.finish = ins, outs, sems, start, finish

    def at_step(self, step, steps, *refs):
        @pl.when(step == 0)
        def _():
            self.start(*refs)

    def at_end(self, step, steps, *refs):
        @pl.when(step == steps - 1)
        def _():
            self.finish(*refs)

    def run(self, *refs):
        self.start(*refs)
        self.finish(*refs)


def matmul(name, a, b, mode, out_dtype=F32, side=None, after=()):
    if mode == "nn":
        (m, k), n = a.shape, b.shape[1]
    elif mode == "nt":
        (m, k), n = a.shape, b.shape[0]
    else:
        (k, m), n = a.shape, b.shape[1]
    tm, tn_, tk = _mm_tiles(m, n, k)
    nk = k // tk
    if mode == "nn":
        a_spec = pl.BlockSpec((tm, tk), lambda i, j, kk: (i, kk))
        b_spec = pl.BlockSpec((tk, tn_), lambda i, j, kk: (kk, j))
        ca, cb = 1, 0
    elif mode == "nt":
        a_spec = pl.BlockSpec((tm, tk), lambda i, j, kk: (i, kk))
        b_spec = pl.BlockSpec((tn_, tk), lambda i, j, kk: (j, kk))
        ca, cb = 1, 1
    else:
        a_spec = pl.BlockSpec((tk, tm), lambda i, j, kk: (kk, i))
        b_spec = pl.BlockSpec((tk, tn_), lambda i, j, kk: (kk, j))
        ca, cb = 0, 0

    grid = (m // tm, n // tn_, nk)
    n_in = len(side.ins) if side else 0
    n_out = len(side.outs) if side else 0
    n_dep = len(after)

    def body(a_ref, b_ref, *rest):
        rest = rest[n_dep:]
        s_ins, o_ref, s_outs = rest[:n_in], rest[n_in], rest[n_in + 1:n_in + 1 + n_out]
        scratch = rest[n_in + 1 + n_out:]
        acc, s_sems = (scratch[:1], scratch[1:]) if nk > 1 else ((), scratch)
        steps = grid[0] * grid[1] * grid[2]
        if side:
            step = (pl.program_id(0) * grid[1] + pl.program_id(1)) * grid[2] + pl.program_id(2)
            side.at_step(step, steps, s_ins, s_outs, s_sems)

        part = _dg(a_ref[...].astype(BF16), b_ref[...].astype(BF16), ca, cb)
        if nk == 1:
            o_ref[...] = part.astype(o_ref.dtype)
        else:
            acc_ref, kk = acc[0], pl.program_id(2)

            @pl.when(kk == 0)
            def _():
                acc_ref[...] = part

            @pl.when(kk > 0)
            def _():
                acc_ref[...] += part

            @pl.when(kk == nk - 1)
            def _():
                o_ref[...] = acc_ref[...].astype(o_ref.dtype)

        if side:
            side.at_end(step, steps, s_ins, s_outs, s_sems)

    res = pl.pallas_call(
        body, name=name, grid=grid,
        in_specs=[a_spec, b_spec] + [_ANY] * (n_dep + n_in),
        out_specs=[pl.BlockSpec((tm, tn_), lambda i, j, kk: (i, j))] + [_ANY] * n_out,
        out_shape=[jax.ShapeDtypeStruct((m, n), out_dtype)] + (list(side.outs) if side else []),
        scratch_shapes=([pltpu.VMEM((tm, tn_), F32)] if nk > 1 else []) + (list(side.sems) if side else []),
        compiler_params=pltpu.CompilerParams(
            dimension_semantics=("arbitrary",) * 3 if side else ("parallel", "parallel", "arbitrary"),
            vmem_limit_bytes=VMEM_BIG),
    )(a, b, *after, *(side.ins if side else []))
    return (res[0], res[1:]) if side else res[0]


def _rspec(tr, width, blk):
    return pl.BlockSpec((tr, width), lambda i: (i, blk))


def _full_spec(arr):
    return pl.BlockSpec(arr.shape, lambda i: (0,) * arr.ndim)


class Stage:
    def __init__(self, name, f, outs, tr, diff_rows, diff_params, drow_dtypes):
        self.name, self.f, self.outs, self.tr = name, f, outs, tr
        self.diff_rows, self.diff_params, self.drow_dtypes = diff_rows, diff_params, drow_dtypes

    def fwd(self, rows, params):
        f, nr, npar = self.f, len(rows), len(params)
        stored = [(w, dt) for (w, dt) in self.outs if dt is not None]
        keep = [i for i, (w, dt) in enumerate(self.outs) if dt is not None]

        def body(*refs):
            vals = f(*[r[...].astype(F32) for r in refs[:nr]], *[p[...] for p in refs[nr:nr + npar]])
            for o_ref, i in zip(refs[nr + npar:], keep):
                o_ref[...] = vals[i].astype(o_ref.dtype)

        return pl.pallas_call(
            body, name=self.name + "_fwd", grid=(T // self.tr,),
            in_specs=[_rspec(self.tr, w, b) for (_, w, b) in rows] + [_full_spec(p) for p in params],
            out_specs=[_rspec(self.tr, w, 0) for (w, _) in stored],
            out_shape=[jax.ShapeDtypeStruct((T, w), dt) for (w, dt) in stored],
            compiler_params=pltpu.CompilerParams(dimension_semantics=("arbitrary",), vmem_limit_bytes=VMEM_BIG),
        )(*[r[0] for r in rows], *params)

    def bwd(self, rows, params, cts):
        f, nr, npar = self.f, len(rows), len(params)
        dr_idx, dp_idx = self.diff_rows, self.diff_params
        flat_cts = [c for lst in cts for c in lst]
        nct = len(flat_cts)

        def body(*refs):
            row_refs, par_refs = refs[:nr], refs[nr:nr + npar]
            ct_refs = refs[nr + npar:nr + npar + nct]
            drow_refs = refs[nr + npar + nct:nr + npar + nct + len(dr_idx)]
            dpar_refs = refs[nr + npar + nct + len(dr_idx):]
            row_vals = [r[...].astype(F32) for r in row_refs]
            par_vals = [p[...] for p in par_refs]

            def g(*dv):
                rv, pv = list(row_vals), list(par_vals)
                for j, i in enumerate(dr_idx):
                    rv[i] = dv[j]
                for j, i in enumerate(dp_idx):
                    pv[i] = dv[len(dr_idx) + j]
                return f(*rv, *pv)

            _, vjp = jax.vjp(g, *[row_vals[i] for i in dr_idx], *[par_vals[i] for i in dp_idx])
            ct_vals, pos = [], 0
            for lst in cts:
                acc = ct_refs[pos][...].astype(F32)
                for q in range(1, len(lst)):
                    acc = acc + ct_refs[pos + q][...].astype(F32)
                pos += len(lst)
                ct_vals.append(acc)
            grads = vjp(tuple(ct_vals))
            for j, r in enumerate(drow_refs):
                r[...] = grads[j].astype(r.dtype)

            @pl.when(pl.program_id(0) == 0)
            def _():
                for r in dpar_refs:
                    r[...] = jnp.zeros_like(r)

            for j, r in enumerate(dpar_refs):
                r[...] += grads[len(dr_idx) + j].astype(F32)

        drow_shapes = [jax.ShapeDtypeStruct((T, rows[i][1]), dt) for i, dt in zip(dr_idx, self.drow_dtypes)]
        dpar_shapes = [jax.ShapeDtypeStruct(params[i].shape, F32) for i in dp_idx]
        res = pl.pallas_call(
            body, name=self.name + "_bwd", grid=(T // self.tr,),
            in_specs=[_rspec(self.tr, w, b) for (_, w, b) in rows] + [_full_spec(p) for p in params]
            + [_rspec(self.tr, w, b) for (_, w, b) in flat_cts],
            out_specs=[_rspec(self.tr, rows[i][1], 0) for i in dr_idx] + [_full_spec(params[i]) for i in dp_idx],
            out_shape=drow_shapes + dpar_shapes,
            compiler_params=pltpu.CompilerParams(dimension_semantics=("arbitrary",), vmem_limit_bytes=VMEM_BIG),
        )(*[r[0] for r in rows], *params, *[c[0] for c in flat_cts])
        return res[:len(dr_idx)], res[len(dr_idx):]


def f_pre(x, g):
    return _rms(x, g), x


@jax.custom_vjp
def swap32(t):
    width = t.shape[1]
    lane = lax.broadcasted_iota(jnp.int32, t.shape, 1) % LANE
    return jnp.where(lane < 32, pltpu.roll(t, width - 32, 1), jnp.where(lane < 64, pltpu.roll(t, 32, 1), 0.0))


swap32.defvjp(lambda t: (swap32(t), None), lambda _, g: (swap32(g),))


def f_mla(q_a, kv_a, kr, cq, sq, ck, sk, gq, gkv, wq, wkv):
    q = bdot(_rms(q_a, gq), wq)
    kv = bdot(_rms(kv_a, gkv), wkv)
    t, k = q[:, 1024:], kr[:, :LANE]
    return (q[:, :1024], t * cq + swap32(t) * sq, kv[:, :1024], k * ck + swap32(k) * sk, kv[:, 1024:])


def f_rwkv_pre(lerp, w0f, w0b, a0f, a0b, kkw, kaw, w2f, w2b, a2f, a2b, g2):
    r, k, v = lerp[:, :RW], lerp[:, RW:2 * RW], lerp[:, 2 * RW:3 * RW]
    wdf, wdb, adf, adb = (lerp[:, 3 * RW + i * LANE:3 * RW + (i + 1) * LANE] for i in range(4))

    def logdecay(w0, wd, w2):
        z = w0 + bdot(jnp.tanh(wd), w2)
        return -jnp.exp(-_softplus(-z) - 0.5)

    a_f = jax.nn.sigmoid(a0f + bdot(adf, a2f))
    a_b = jax.nn.sigmoid(a0b + bdot(adb, a2b))
    kk = k * kkw
    kk = kk / jnp.maximum(jnp.sqrt(headsum(kk * kk, g2)), 1e-12)
    return (r, v, logdecay(w0f, wdf, w2f), logdecay(w0b, wdb, w2b),
            k * (1.0 + (a_f - 1.0) * kaw), k * (1.0 + (a_b - 1.0) * kaw), -kk, kk * a_f, kk * a_b)


def f_rwkv_post(yf, yb, r, kf, kb, v, z, gng, gnb, rk, g2):
    y = yf + yb
    mu = headsum(y, g2) * (1.0 / RN)
    d = y - mu
    var = headsum(d * d, g2) * (1.0 / RN)
    yn = d * lax.rsqrt(var + GN_EPS) * gng + gnb
    bonus = headsum(r * (kf + kb) * rk, g2) * v
    return ((yn + bonus) * _silu(z),)


def f_gate(y, z):
    return (y * _silu(z),)


def f_merge(um, ur, gm, gr):
    return (jax.nn.sigmoid(gm) * um + jax.nn.sigmoid(gr) * ur,)


_SHIFT_W = 256


def _lerp_colblock(j):
    return jnp.where(j < 3072 // _SHIFT_W, OFF_RKV // _SHIFT_W + j, OFF_LORA // _SHIFT_W + j - 3072 // _SHIFT_W)


def _nbr_mean(x):
    row = lax.broadcasted_iota(jnp.int32, x.shape, 0)
    up = jnp.where(row == 0, 0.0, pltpu.roll(x, 1, 0))
    dn = jnp.where(row == T - 1, 0.0, pltpu.roll(x, T - 1, 0))
    return 0.5 * (up + dn)


def shift_fwd(proj, mu):
    def body(x_ref, mu_ref, o_ref):
        x = x_ref[...]
        o_ref[...] = x + mu_ref[...] * (_nbr_mean(x) - x)

    return pl.pallas_call(
        body, name="shift_fwd", grid=(NLERP // _SHIFT_W,),
        in_specs=[pl.BlockSpec((T, _SHIFT_W), lambda j: (0, _lerp_colblock(j))),
                  pl.BlockSpec((1, _SHIFT_W), lambda j: (0, j))],
        out_specs=pl.BlockSpec((T, _SHIFT_W), lambda j: (0, j)),
        out_shape=jax.ShapeDtypeStruct((T, NLERP), F32),
        compiler_params=pltpu.CompilerParams(dimension_semantics=("parallel",), vmem_limit_bytes=VMEM_BIG),
    )(proj, mu)


def shift_bwd(proj, mu, g):
    def body(x_ref, mu_ref, g_ref, dx_ref, dmu_ref):
        x, gv = x_ref[...], g_ref[...]
        dmu_ref[...] = jnp.sum(gv * (_nbr_mean(x) - x), axis=0, keepdims=True)
        gm = gv * mu_ref[...]
        dx_ref[...] = (gv - gm + _nbr_mean(gm)).astype(dx_ref.dtype)

    col = pl.BlockSpec((T, _SHIFT_W), lambda j: (0, j))
    vec = pl.BlockSpec((1, _SHIFT_W), lambda j: (0, j))
    return pl.pallas_call(
        body, name="shift_bwd", grid=(NLERP // _SHIFT_W,),
        in_specs=[pl.BlockSpec((T, _SHIFT_W), lambda j: (0, _lerp_colblock(j))), vec, col],
        out_specs=[col, vec],
        out_shape=[jax.ShapeDtypeStruct((T, NLERP), BF16), jax.ShapeDtypeStruct((1, NLERP), F32)],
        compiler_params=pltpu.CompilerParams(dimension_semantics=("parallel",), vmem_limit_bytes=VMEM_BIG),
    )(proj, mu, g)


_TQ_F, _TQ_B = 256, 512
_ATT_SCALE = (NOPE + ROPE) ** -0.5


def _probs(q, k):
    s = _dg(q, k, 1, 1) * _ATT_SCALE
    e = jnp.exp(s - jnp.max(s, axis=-1, keepdims=True))
    return e * (1.0 / jnp.sum(e, axis=-1, keepdims=True))


def _q_blk(tq):
    return pl.BlockSpec((tq, LANE), lambda h, i: (i, h))


_K_BLK = pl.BlockSpec((T, LANE), lambda h, i: (0, h))
_KR_BLK = pl.BlockSpec((T, LANE), lambda h, i: (0, 0))


def _load_qk(qn_ref, qr_ref, kn_ref, kr_ref, kcat_ref):
    @pl.when(pl.program_id(1) == 0)
    def _():
        kcat_ref[:, :LANE] = kn_ref[...]
        kcat_ref[:, LANE:] = kr_ref[...]

    return jnp.concatenate([qn_ref[...], qr_ref[...]], axis=1), kcat_ref[...]


def attn_fwd(qn, qr, kn, kr, v):
    def body(qn_ref, qr_ref, kn_ref, kr_ref, v_ref, o_ref, kcat_ref):
        q, k = _load_qk(qn_ref, qr_ref, kn_ref, kr_ref, kcat_ref)
        o_ref[...] = _dg(_probs(q, k).astype(BF16), v_ref[...], 1, 0)

    return pl.pallas_call(
        body, name="attn_fwd", grid=(HEADS, T // _TQ_F),
        in_specs=[_q_blk(_TQ_F), _q_blk(_TQ_F), _K_BLK, _KR_BLK, _K_BLK], out_specs=_q_blk(_TQ_F),
        out_shape=jax.ShapeDtypeStruct((T, HEADS * VDIM), F32),
        scratch_shapes=[pltpu.VMEM((T, 2 * LANE), BF16)],
        compiler_params=pltpu.CompilerParams(dimension_semantics=("arbitrary", "arbitrary"), vmem_limit_bytes=VMEM_BIG),
    )(qn, qr, kn, kr, v)


def attn_bwd(qn, qr, kn, kr, v, do):
    def body(qn_ref, qr_ref, kn_ref, kr_ref, v_ref, do_ref, dqn_ref, dqr_ref, dkn_ref, dkr_ref, dv_ref, kcat_ref):
        h, i = pl.program_id(0), pl.program_id(1)

        @pl.when(i == 0)
        def _():
            dkn_ref[...] = jnp.zeros_like(dkn_ref)
            dv_ref[...] = jnp.zeros_like(dv_ref)

        @pl.when((i == 0) & (h == 0))
        def _():
            dkr_ref[...] = jnp.zeros_like(dkr_ref)

        q, k = _load_qk(qn_ref, qr_ref, kn_ref, kr_ref, kcat_ref)
        dob = do_ref[...].astype(BF16)
        p = _probs(q, k)
        dv_ref[...] += _dg(p.astype(BF16), dob, 0, 0)
        dp = _dg(dob, v_ref[...], 1, 1)
        ds = (p * (dp - jnp.sum(dp * p, axis=-1, keepdims=True)) * _ATT_SCALE).astype(BF16)
        dq = _dg(ds, k, 1, 0)
        dqn_ref[...] = dq[:, :LANE]
        dqr_ref[...] = dq[:, LANE:]
        dk = _dg(ds, q, 0, 0)
        dkn_ref[...] += dk[:, :LANE]
        dkr_ref[...] += dk[:, LANE:]

    wide = jax.ShapeDtypeStruct((T, HEADS * LANE), F32)
    return pl.pallas_call(
        body, name="attn_bwd", grid=(HEADS, T // _TQ_B),
        in_specs=[_q_blk(_TQ_B), _q_blk(_TQ_B), _K_BLK, _KR_BLK, _K_BLK, _q_blk(_TQ_B)],
        out_specs=[_q_blk(_TQ_B), _q_blk(_TQ_B), _K_BLK, _KR_BLK, _K_BLK],
        out_shape=[wide, wide, wide, jax.ShapeDtypeStruct((T, LANE), F32), wide],
        scratch_shapes=[pltpu.VMEM((T, 2 * LANE), BF16)],
        compiler_params=pltpu.CompilerParams(dimension_semantics=("arbitrary", "arbitrary"), vmem_limit_bytes=VMEM_BIG),
    )(qn, qr, kn, kr, v, do)


def _chunk(r, lw, k, v, a, b, ht, u_kept=None, *, reverse):
    hb, c, _ = r.shape
    ti = lax.broadcasted_iota(jnp.int32, (c, c), 0)
    si = lax.broadcasted_iota(jnp.int32, (c, c), 1)
    incl = (si >= ti) if reverse else (si <= ti)
    strict = (si > ti) if reverse else (si < ti)
    ones = jnp.broadcast_to(incl.astype(F32)[None], (hb, c, c))
    cum = cumdot(ones, lw)
    cum_ex = cum - lw
    tot = jnp.sum(lw, axis=1, keepdims=True)
    mid = 0.5 * tot
    rt, at = r * jnp.exp(cum - mid), a * jnp.exp(cum_ex - mid)
    einv = jnp.exp(mid - cum)
    bk = jnp.concatenate([b * einv, k * einv], axis=1)
    m_a, m_r = nt(at, bk), nt(rt, bk)
    m_ab = jnp.where(strict, m_a[:, :, :c], 0.0)
    m_ak = jnp.where(strict, m_a[:, :, c:], 0.0)
    t2 = lax.broadcasted_iota(jnp.int32, (c, 2 * c), 0)
    s2 = lax.broadcasted_iota(jnp.int32, (c, 2 * c), 1)
    s2 = jnp.where(s2 >= c, s2 - c, s2)
    m_r = jnp.where((s2 >= t2) if reverse else (s2 <= t2), m_r, 0.0)
    rhs = nt(a * jnp.exp(cum_ex), ht) + nn(m_ak, v)
    u = tri_solve(m_ab, rhs) if u_kept is None else known_solve(m_ab, rhs, u_kept)
    uv = jnp.concatenate([u, v], axis=1)
    y = nt(r * jnp.exp(cum), ht) + nn(m_r, uv)
    eend = jnp.exp(tot - cum)
    ht_new = ht * jnp.exp(tot) + tn(uv, jnp.concatenate([b * eend, k * eend], axis=1))
    return y, ht_new, u


_HB_F, _HB_B = 16, 16


def _split_heads(x):
    return jnp.stack([x[:, i * RN:(i + 1) * RN] for i in range(x.shape[1] // RN)])


def _merge_heads(y):
    return jnp.concatenate([y[i] for i in range(y.shape[0])], axis=1)


def _chunk_map(reverse, backward):
    flip = reverse != backward
    return (lambda g, c: (NCH - 1 - c, g)) if flip else (lambda g, c: (c, g))


def scan_fwd(name, r, lw, k, v, a, b, reverse):
    hb = _HB_F
    cmap = _chunk_map(reverse, False)

    def body(r_ref, lw_ref, k_ref, v_ref, a_ref, b_ref, y_ref, h0_ref, u_ref, ht_ref):
        @pl.when(pl.program_id(1) == 0)
        def _():
            ht_ref[...] = jnp.zeros_like(ht_ref)

        ht = ht_ref[...]
        h0_ref[0] = ht
        ins = [_split_heads(x[...]) for x in (r_ref, lw_ref, k_ref, v_ref, a_ref, b_ref)]
        y, hn, u = _chunk(*ins, ht, reverse=reverse)
        y_ref[...] = _merge_heads(y)
        u_ref[...] = _merge_heads(u)
        ht_ref[...] = hn

    io = pl.BlockSpec((CHUNK, hb * RN), cmap)
    return pl.pallas_call(
        body, name=name, grid=(RH // hb, NCH),
        in_specs=[io] * 6,
        out_specs=[io, pl.BlockSpec((1, hb, RN, RN), lambda g, c: (cmap(g, c)[0], g, 0, 0)), io],
        out_shape=[jax.ShapeDtypeStruct((T, RW), F32), jax.ShapeDtypeStruct((NCH, RH, RN, RN), F32),
                   jax.ShapeDtypeStruct((T, RW), F32)],
        scratch_shapes=[pltpu.VMEM((hb, RN, RN), F32)],
        compiler_params=pltpu.CompilerParams(dimension_semantics=("parallel", "arbitrary"), vmem_limit_bytes=VMEM_BIG),
    )(r, lw, k, v, a, b)


def scan_bwd(name, r, lw, k, v, a, b, h0, u, dy, reverse):
    hb = _HB_B
    cmap = _chunk_map(reverse, True)

    def body(r_ref, lw_ref, k_ref, v_ref, a_ref, b_ref, h0_ref, u_ref, dy_ref, *rest):
        d_refs, dht_ref = rest[:6], rest[6]

        @pl.when(pl.program_id(1) == 0)
        def _():
            dht_ref[...] = jnp.zeros_like(dht_ref)

        ins = [_split_heads(x[...]) for x in (r_ref, lw_ref, k_ref, v_ref, a_ref, b_ref)]
        _, vjp = jax.vjp(functools.partial(_chunk, reverse=reverse), *ins, h0_ref[0], _split_heads(u_ref[...]))
        dy = _split_heads(dy_ref[...])
        grads = vjp((dy, dht_ref[...], jnp.zeros_like(dy)))
        for d_ref, gval in zip(d_refs, grads[:6]):
            d_ref[...] = _merge_heads(gval).astype(d_ref.dtype)
        dht_ref[...] = grads[6]

    io = pl.BlockSpec((CHUNK, hb * RN), cmap)
    return pl.pallas_call(
        body, name=name, grid=(RH // hb, NCH),
        in_specs=[io] * 6 + [pl.BlockSpec((1, hb, RN, RN), lambda g, c: (cmap(g, c)[0], g, 0, 0)), io, io],
        out_specs=[io] * 6,
        out_shape=[jax.ShapeDtypeStruct((T, RW), F32 if i == 1 else BF16) for i in range(6)],
        scratch_shapes=[pltpu.VMEM((hb, RN, RN), F32)],
        compiler_params=pltpu.CompilerParams(dimension_semantics=("parallel", "arbitrary"), vmem_limit_bytes=VMEM_BIG),
    )(r, lw, k, v, a, b, h0, u, dy)


def loss_stage(out, x2, tgt, g_post):
    tr = 256

    def body(o_ref, x_ref, t_ref, g_ref, do_ref, dy_ref, dg_ref, loss_ref):
        @pl.when(pl.program_id(0) == 0)
        def _():
            dg_ref[...] = jnp.zeros_like(dg_ref)
            loss_ref[...] = jnp.zeros_like(loss_ref)

        nrm, vjp = jax.vjp(_rms, o_ref[...], g_ref[...])
        e = x_ref[...] + nrm - t_ref[...]
        s = jnp.sum(jnp.sum(e * e, axis=1, keepdims=True), axis=0, keepdims=True)
        loss_ref[...] += jnp.broadcast_to(s * (0.5 / D), loss_ref.shape)
        dy = e * (1.0 / D)
        do, dg = vjp(dy)
        do_ref[...] = do.astype(do_ref.dtype)
        dy_ref[...] = dy
        dg_ref[...] += dg

    row = pl.BlockSpec((tr, D), lambda i: (i, 0))
    return pl.pallas_call(
        body, name="loss_stage", grid=(T // tr,),
        in_specs=[row, row, row, pl.BlockSpec((1, D), lambda i: (0, 0))],
        out_specs=[row, row, pl.BlockSpec((1, D), lambda i: (0, 0)), pl.BlockSpec((8, LANE), lambda i: (0, 0))],
        out_shape=[jax.ShapeDtypeStruct((T, D), BF16), jax.ShapeDtypeStruct((T, D), F32),
                   jax.ShapeDtypeStruct((1, D), F32), jax.ShapeDtypeStruct((8, LANE), F32)],
        compiler_params=pltpu.CompilerParams(dimension_semantics=("arbitrary",), vmem_limit_bytes=VMEM_BIG),
    )(out, x2, tgt, g_post)


_EW_BLOCK_BYTES = 1 << 20


def _row_tile(rows, cols):
    best = None
    for tr in range(16, rows + 1, 16):
        if rows % tr == 0 and tr * cols * 4 <= _EW_BLOCK_BYTES:
            best = tr
    return best or rows


def _axis_tile(shape, axis, words):
    rows, cols = shape
    n, other, unit = (rows, cols, 16) if axis == 0 else (cols, rows, LANE)
    best = unit if n % unit == 0 else n
    for t in range(unit, n + 1, unit):
        if n % t == 0 and t * other * words * 4 <= _EW_BLOCK_BYTES:
            best = t
    blk = (best, cols) if axis == 0 else (rows, best)
    at = (lambda s: (s, 0)) if axis == 0 else (lambda s: (0, s))
    return blk, n // best, at


def _adamw_update(g, w_ref, m_ref, v_ref, g_ref, d_ref, nm_ref, nv_ref):
    mm = ADAM_B1 * m_ref[...] + (1.0 - ADAM_B1) * g
    vv = ADAM_B2 * v_ref[...] + (1.0 - ADAM_B2) * (g * g)
    m_hat = mm / (1.0 - ADAM_B1 ** ADAM_STEP)
    v_hat = vv / (1.0 - ADAM_B2 ** ADAM_STEP)
    g_ref[...] = g
    d_ref[...] = -ADAM_LR * (m_hat / (jnp.sqrt(v_hat) + ADAM_EPS) + ADAM_WD * w_ref[...])
    nm_ref[...] = mm
    nv_ref[...] = vv


def adamw(name, w, m, v, parts):
    rows, cols = w.shape
    br = _row_tile(rows, cols)
    npart = len(parts)

    def body(w_ref, m_ref, v_ref, *rest):
        g = rest[0][...].astype(F32)
        for p in rest[1:npart]:
            g = g + p[...].astype(F32)
        _adamw_update(g, w_ref, m_ref, v_ref, *rest[npart:])

    blk = pl.BlockSpec((br, cols), lambda i: (i, 0))
    return pl.pallas_call(
        body, name=name, grid=(rows // br,),
        in_specs=[blk] * (3 + npart), out_specs=[blk] * 4,
        out_shape=[jax.ShapeDtypeStruct((rows, cols), F32)] * 4,
        compiler_params=pltpu.CompilerParams(dimension_semantics=("parallel",), vmem_limit_bytes=VMEM_BIG),
    )(w, m, v, *parts)


def adamw_halves(name, place, w, m, v, mine, theirs, axis):
    half_shape = mine.shape
    blk_shape, nb, at = _axis_tile(half_shape, axis, 1)

    def body(p_ref, w_ref, m_ref, v_ref, a_ref, b_ref, *outs):
        own = (pl.program_id(0) // nb) == p_ref[0]
        _adamw_update(jnp.where(own, a_ref[...], b_ref[...]), w_ref, m_ref, v_ref, *outs)

    blk = pl.BlockSpec(blk_shape, lambda i, p: at(i))
    half = pl.BlockSpec(blk_shape, lambda i, p: at(i % nb))
    return pl.pallas_call(
        body, name=name,
        grid_spec=pltpu.PrefetchScalarGridSpec(num_scalar_prefetch=1, grid=(2 * nb,),
                                               in_specs=[blk] * 3 + [half] * 2, out_specs=[blk] * 4),
        out_shape=[jax.ShapeDtypeStruct(w.shape, F32)] * 4,
        compiler_params=pltpu.CompilerParams(dimension_semantics=("arbitrary",), vmem_limit_bytes=VMEM_BIG),
    )(place, w, m, v, mine, theirs)


def pair_sum(name, place, send, other, axis):
    blk_shape, nb, at = _axis_tile(other.shape[1:], axis, 4)

    def body(p_ref, a_ref, b_ref, o_ref):
        o_ref[...] = (a_ref[...].astype(F32) + b_ref[...].astype(F32)).astype(o_ref.dtype)

    blk = pl.BlockSpec((4,) + blk_shape, lambda i, p: (0,) + at(i))
    mine = pl.BlockSpec((4,) + blk_shape, lambda i, p: (0,) + at(p[0] * nb + i))
    return pl.pallas_call(
        body, name=name,
        grid_spec=pltpu.PrefetchScalarGridSpec(num_scalar_prefetch=1, grid=(nb,), in_specs=[mine, blk], out_specs=blk),
        out_shape=jax.ShapeDtypeStruct(other.shape, BF16),
        compiler_params=pltpu.CompilerParams(dimension_semantics=("arbitrary",), vmem_limit_bytes=VMEM_BIG),
    )(place, send, other)


def sum4(name, place, recv, own, axis):
    blk_shape, nb, at = _axis_tile(recv.shape[1:], axis, 4)

    def body(p_ref, r_ref, s_ref, o_ref):
        me = p_ref[0]
        t = [jnp.where(me == j, s_ref[j], r_ref[j]).astype(F32) for j in range(4)]
        o_ref[...] = ((t[0] + t[1]) + t[2]) + t[3]

    blk = pl.BlockSpec((4,) + blk_shape, lambda i, p: (0,) + at(i))
    return pl.pallas_call(
        body, name=name,
        grid_spec=pltpu.PrefetchScalarGridSpec(num_scalar_prefetch=1, grid=(nb,), in_specs=[blk, blk],
                                               out_specs=pl.BlockSpec(blk_shape, lambda i, p: at(i))),
        out_shape=jax.ShapeDtypeStruct(recv.shape[1:], F32),
        compiler_params=pltpu.CompilerParams(dimension_semantics=("arbitrary",), vmem_limit_bytes=VMEM_BIG),
    )(place, recv, own)


_ANY = pl.BlockSpec(memory_space=pl.ANY)


def _place():
    x, y, c = lax.axis_index("x"), lax.axis_index("y"), lax.axis_index("c")
    return x, y, c, 2 * x + y


def _chip_peers(x, y):
    out = []
    for k in (1, 2, 3):
        px = 1 - x if k & 2 else x
        py = 1 - y if k & 1 else y
        out.append((k, px, py, 2 * px + py))
    return out


def _half(c, shape, axis):
    n = shape[axis] // 2
    sl = pl.ds(pl.multiple_of(c * n, 16 if axis == 0 else LANE), n)
    return (sl,) if axis == 0 else (pl.ds(0, shape[0]), sl)


def gather_weights(srcs, axes):
    side = gather_side(srcs, axes)
    n = len(srcs)

    def body(*refs):
        ins, outs, sems = refs[:n], refs[n:2 * n], refs[2 * n:]
        side.run(ins, outs, sems)

    return pl.pallas_call(
        body, name="gather_weights", in_specs=[_ANY] * n, out_specs=[_ANY] * n,
        out_shape=side.outs, scratch_shapes=side.sems,
    )(*srcs)


def gather_side(srcs, axes):
    n = len(srcs)

    def copies(src, dst, sems, want):
        ssem, rsem, fssem, frsem = sems
        x, y, c, me = _place()
        sib = (x, y, 1 - c)
        out = []
        for i in range(n):
            mine, other = _half(c, srcs[i].shape, axes[i]), _half(1 - c, srcs[i].shape, axes[i])
            for k, px, py, peer in _chip_peers(x, y):
                sems_k = dict(send_sem=ssem.at[i, k - 1], recv_sem=rsem.at[i, k - 1], device_id=(px, py, c),
                              device_id_type=MESH_IDS)
                fsems = dict(send_sem=fssem.at[i, k - 1], recv_sem=frsem.at[i, k - 1], device_id=sib,
                             device_id_type=MESH_IDS)
                got = dst[i].at[(peer,) + mine]
                mk = pltpu.make_async_remote_copy
                made = dict(
                    snd=lambda: mk(src_ref=src[i].at[mine], dst_ref=dst[i].at[(me,) + mine], **sems_k),
                    rcv=lambda: mk(src_ref=src[i].at[mine], dst_ref=got, **sems_k),
                    fwd=lambda: mk(src_ref=got, dst_ref=got, **fsems),
                    frcv=lambda: mk(src_ref=got, dst_ref=dst[i].at[(peer,) + other], **fsems))
                out.append([made[w]() for w in want])
        return out

    def start(src, dst, sems):
        for (snd,) in copies(src, dst, sems, ("snd",)):
            snd.start()

    def finish(src, dst, sems):
        cps = copies(src, dst, sems, ("snd", "rcv", "fwd", "frcv"))
        for _, rcv, fwd, _ in cps:
            rcv.wait_recv()
            fwd.start()
        for snd, _, fwd, frcv in cps:
            frcv.wait_recv()
            snd.wait_send()
            fwd.wait_send()

    return Side(list(srcs), [jax.ShapeDtypeStruct((4,) + s.shape, s.dtype) for s in srcs],
                [pltpu.SemaphoreType.DMA((n, 3))] * 4, start, finish)


def pair_exchange(name, srcs, axes):
    n = len(srcs)

    def half_shape(s, axis):
        return (4, s.shape[1] // 2, s.shape[2]) if axis == 0 else (4, s.shape[1], s.shape[2] // 2)

    def body(*refs):
        src, other = refs[:n], refs[n:2 * n]
        ssem, rsem = refs[2 * n:]
        x, y, c, _ = _place()
        cps = []
        for i in range(n):
            idx = (pl.ds(0, 4),) + _half(1 - c, srcs[i].shape[1:], axes[i])
            cps.append(pltpu.make_async_remote_copy(
                src_ref=src[i].at[idx], dst_ref=other[i], send_sem=ssem.at[i], recv_sem=rsem.at[i],
                device_id=(x, y, 1 - c), device_id_type=MESH_IDS))
            cps[-1].start()
        for cp in cps:
            cp.wait()

    return pl.pallas_call(
        body, name=name, in_specs=[_ANY] * n, out_specs=[_ANY] * n,
        out_shape=[jax.ShapeDtypeStruct(half_shape(s, a), s.dtype) for s, a in zip(srcs, axes)],
        scratch_shapes=[pltpu.SemaphoreType.DMA((n,))] * 2,
    )(*srcs)


def scatter_side(srcs):
    n = len(srcs)

    def copies(src, dst, sems, sends_only=False):
        ssem, rsem = sems
        x, y, c, me = _place()
        out = []
        for i in range(n):
            for k, px, py, peer in _chip_peers(x, y):
                sems_k = dict(send_sem=ssem.at[i, k - 1], recv_sem=rsem.at[i, k - 1], device_id=(px, py, c),
                              device_id_type=MESH_IDS)
                snd = pltpu.make_async_remote_copy(src_ref=src[i].at[peer], dst_ref=dst[i].at[me], **sems_k)
                if sends_only:
                    out.append(snd)
                    continue
                out.append((snd, pltpu.make_async_remote_copy(src_ref=src[i].at[peer], dst_ref=dst[i].at[peer],
                                                              **sems_k)))
        return out

    def start(src, dst, sems):
        for snd in copies(src, dst, sems, sends_only=True):
            snd.start()

    def finish(src, dst, sems):
        for snd, rcv in copies(src, dst, sems):
            rcv.wait_recv()
            snd.wait_send()

    return Side(list(srcs), [jax.ShapeDtypeStruct(s.shape, s.dtype) for s in srcs],
                [pltpu.SemaphoreType.DMA((n, 3))] * 2, start, finish)


_HBM = pl.BlockSpec(memory_space=pltpu.HBM)
_SEM = pl.BlockSpec(memory_space=pltpu.SEMAPHORE)
_DATAFLOW = pltpu.SideEffectType.DATAFLOW_SIDE_EFFECTING


def scatter_start(name, srcs):
    n = len(srcs)
    side = scatter_side(srcs)
    ns = 3 * n

    def body(*refs):
        src, land = refs[:n], refs[n:2 * n]
        sems = refs[2 * n:2 * n + 2 * ns]
        side.start(src, land, (_SemGrid(sems[:ns]), _SemGrid(sems[ns:])))
        refs[-1][...] = jnp.zeros_like(refs[-1])

    hbm = [pltpu.HBM(s.shape, s.dtype) for s in srcs]
    res = pl.pallas_call(
        body, name=name,
        out_shape=[pltpu.SemaphoreType.DMA(())] * (2 * ns) + hbm + hbm + [jax.ShapeDtypeStruct((8, LANE), F32)],
        in_specs=[_HBM] * (2 * n),
        out_specs=[_SEM] * (2 * ns) + [_HBM] * (2 * n) + [pl.BlockSpec(memory_space=pltpu.VMEM)],
        input_output_aliases={i: 2 * ns + i for i in range(2 * n)},
        compiler_params=pltpu.CompilerParams(has_side_effects=_DATAFLOW),
    )(*[pltpu.with_memory_space_constraint(s, pltpu.HBM) for s in srcs],
      *[pltpu.with_memory_space_constraint(lax.empty(s.shape, s.dtype), pltpu.HBM) for s in srcs])
    return res[:2 * ns], res[2 * ns:2 * ns + n], res[2 * ns + n:2 * ns + 2 * n], res[-1]


def scatter_wait(name, sems, srcs, lands, after):
    n = len(srcs)
    side = scatter_side(srcs)
    ns = 3 * n

    def body(*refs):
        src, land = refs[:n], refs[n:2 * n]
        s = refs[2 * n:2 * n + 2 * ns]
        side.finish(src, land, (_SemGrid(s[:ns]), _SemGrid(s[ns:])))

    hbm = [pltpu.HBM(s.shape, s.dtype) for s in srcs]
    res = pl.pallas_call(
        body, name=name, out_shape=hbm + hbm,
        in_specs=[_HBM] * (2 * n) + [_SEM] * (2 * ns) + [_ANY], out_specs=[_HBM] * (2 * n),
        input_output_aliases={i: i for i in range(2 * n)},
        compiler_params=pltpu.CompilerParams(has_side_effects=_DATAFLOW),
    )(*srcs, *lands, *sems, after)
    return res[:n], res[n:]


class _SemGrid:
    def __init__(self, sems):
        self.sems = sems

    @property
    def at(self):
        return self

    def __getitem__(self, ik):
        return self.sems[3 * ik[0] + ik[1]]


def swap_halves(name, srcs):
    n = len(srcs)

    def body(*refs):
        src, dst = refs[:n], refs[n:2 * n]
        ssem, rsem = refs[2 * n:]
        x, y, c, _ = _place()
        cps = []
        for i in range(n):
            cps.append(pltpu.make_async_remote_copy(src_ref=src[i], dst_ref=dst[i], send_sem=ssem.at[i],
                                                    recv_sem=rsem.at[i], device_id=(x, y, 1 - c),
                                                    device_id_type=MESH_IDS))
            cps[-1].start()
        for cp in cps:
            cp.wait()

    return pl.pallas_call(
        body, name=name, in_specs=[_ANY] * n, out_specs=[_ANY] * n,
        out_shape=[jax.ShapeDtypeStruct(s.shape, s.dtype) for s in srcs],
        scratch_shapes=[pltpu.SemaphoreType.DMA((n,))] * 2,
    )(*srcs)


def _ag8_copies(src, dst, sems, sends_only=False):
    x, y, c = lax.axis_index("x"), lax.axis_index("y"), lax.axis_index("c")
    me = 4 * x + 2 * y + c
    out = []
    for k in range(1, 8):
        px = 1 - x if k & 4 else x
        py = 1 - y if k & 2 else y
        pc = 1 - c if k & 1 else c
        peer = 4 * px + 2 * py + pc
        out.append(tuple(pltpu.make_async_remote_copy(
            src_ref=src, dst_ref=dst.at[slot], send_sem=sems[k - 1], recv_sem=sems[7 + k - 1],
            device_id=(px, py, pc), device_id_type=MESH_IDS) for slot in ((me,) if sends_only else (me, peer))))
    return out


def allgather8_start(name, src):
    def body(src_ref, land_ref, *rest):
        for (snd,) in _ag8_copies(src_ref, land_ref, rest[:14], sends_only=True):
            snd.start()

    land = jax.ShapeDtypeStruct((8,) + src.shape, src.dtype)
    res = pl.pallas_call(
        body, name=name,
        out_shape=[pltpu.SemaphoreType.DMA(())] * 14 + [pltpu.HBM(src.shape, src.dtype), pltpu.HBM(land.shape, land.dtype)],
        in_specs=[_HBM, _HBM], out_specs=[_SEM] * 14 + [_HBM, _HBM],
        input_output_aliases={0: 14, 1: 15},
        compiler_params=pltpu.CompilerParams(has_side_effects=_DATAFLOW),
    )(pltpu.with_memory_space_constraint(src, pltpu.HBM),
      pltpu.with_memory_space_constraint(lax.empty(land.shape, land.dtype), pltpu.HBM))
    return res[:14], res[14], res[15]


def allgather8_wait(name, sems, src, land, after):
    def body(src_ref, land_ref, *rest):
        for snd, rcv in _ag8_copies(src_ref, land_ref, rest[:14]):
            rcv.wait_recv()
            snd.wait_send()

    return pl.pallas_call(
        body, name=name, out_shape=[pltpu.HBM(src.shape, src.dtype), pltpu.HBM(land.shape, land.dtype)],
        in_specs=[_HBM, _HBM] + [_SEM] * 14 + [_ANY], out_specs=[_HBM, _HBM],
        input_output_aliases={0: 0, 1: 1},
        compiler_params=pltpu.CompilerParams(has_side_effects=_DATAFLOW),
    )(src, land, *sems, after)


WEIGHTS = ['g_pre', 'w_in', 'mla_q_norm', 'mla_wq_b', 'mla_kv_norm', 'mla_wkv_b', 'rwkv_mu', 'rwkv_w0_f', 'rwkv_w2_f',
           'rwkv_w0_b', 'rwkv_w2_b', 'rwkv_a0_f', 'rwkv_a2_f', 'rwkv_a0_b', 'rwkv_a2_b', 'rwkv_k_k', 'rwkv_k_a',
           'rwkv_r_k', 'rwkv_gn_g', 'rwkv_gn_b', 'w_br_mla', 'w_br_rwkv', 'w_out', 'g_post']
BIG_SHAPES = {'w_in': (D_IN // 4, D), 'mla_wq_b': (Q_RANK, 384), 'mla_wkv_b': (KV_RANK, 512),
              'rwkv_w2_f': (LORA, 256), 'rwkv_w2_b': (LORA, 256), 'rwkv_a2_f': (LORA, 256), 'rwkv_a2_b': (LORA, 256),
              'w_br_mla': (RW, 512), 'w_br_rwkv': (RW, 512), 'w_out': (512, D)}
BIG = list(BIG_SHAPES)
SMALL = [n for n in WEIGHTS if n not in BIG_SHAPES]
SMALL_SHAPES = {'g_pre': (D,), 'mla_q_norm': (Q_RANK,), 'mla_kv_norm': (KV_RANK,), 'rwkv_mu': (3456,),
                'rwkv_w0_f': (RW,), 'rwkv_w0_b': (RW,), 'rwkv_a0_f': (RW,), 'rwkv_a0_b': (RW,), 'rwkv_k_k': (RW,),
                'rwkv_k_a': (RW,), 'rwkv_r_k': (RH, RN), 'rwkv_gn_g': (RW,), 'rwkv_gn_b': (RW,), 'g_post': (D,)}
SMALL_LEN = sum(int(np.prod(s)) for s in SMALL_SHAPES.values())
SMALL_ROWS = 144


UNITS = [('w_in',), ('mla_wq_b',), ('mla_wkv_b',), ('rwkv_w2_f', 'rwkv_w2_b', 'rwkv_a2_f', 'rwkv_a2_b'),
         ('w_br_mla', 'w_br_rwkv'), ('w_out',)]
UNIT_AXIS = [1, 0, 0, 0, 0, 0]
ROW_SHARDED = ('w_in', 'w_out')


def _unit_cat(parts):
    return parts[0] if len(parts) == 1 else jnp.concatenate(parts, axis=0)


def _unit_split(arr, names, axis):
    out, o = {}, 0
    for n in names:
        rows = BIG_SHAPES[n][0]
        out[n] = lax.slice_in_dim(arr, o, o + rows, axis=axis)
        o += rows
    return out


def _gathered(units, ag, own, me):
    out = {}
    for names, arr, mine in zip(units, ag, own):
        slots = [jnp.where(me == j, mine, arr[j]) for j in range(4)]
        for n in names:
            parts = [_unit_split(s, names, 0)[n] for s in slots]
            out[n] = jnp.concatenate(parts, axis=0 if n in ROW_SHARDED else 1)
    return out


def _shards(n, g):
    r, w = BIG_SHAPES[n]
    if n in ROW_SHARDED:
        return [g[j * r:(j + 1) * r] for j in range(4)]
    return [g[:, j * w:(j + 1) * w] for j in range(4)]


def _pack_small(d, extra=None):
    flat = jnp.concatenate([d[n].reshape(-1) for n in SMALL] + ([extra.reshape(-1)] if extra is not None else []))
    return jnp.pad(flat, (0, SMALL_ROWS * LANE - flat.shape[0])).reshape(SMALL_ROWS, LANE)


def _unpack_small(packed):
    flat, out, o = packed.reshape(-1), {}, 0
    for n in SMALL:
        sz = int(np.prod(SMALL_SHAPES[n]))
        out[n] = flat[o:o + sz].reshape(SMALL_SHAPES[n])
        o += sz
    return out


def _perm_w_in(gathered, own, me):
    per = D_IN // 4

    def rows(a, b):
        out = []
        while a < b:
            j, lo = divmod(a, per)
            hi = min(b - j * per, per)
            out.append(jnp.where(me == j, own[lo:hi], gathered[j, lo:hi]))
            a = j * per + hi
        return out

    z = lambda n: [jnp.zeros((n, own.shape[1]), own.dtype)]
    lora = []
    for i in range(4):
        lora += rows(4160 + LORA * i, 4160 + LORA * (i + 1)) + z(LANE - LORA)
    return jnp.concatenate(rows(0, 1024) + rows(1088, 4160) + rows(4544, D_IN) + lora + rows(1024, 1088)
                           + z(256 - ROPE), axis=0)


def _unperm_w_in(g):
    lora = [g[OFF_LORA + LANE * i:OFF_LORA + LANE * i + LORA] for i in range(4)]
    return jnp.concatenate([g[0:1024], g[OFF_KR:OFF_KR + ROPE], g[1024:4096]] + lora + [g[4096:OFF_LORA]], axis=0)


def _perm_wq(w):
    w3 = w.reshape(Q_RANK, HEADS, NOPE + ROPE)
    rope = jnp.pad(w3[:, :, NOPE:], ((0, 0), (0, 0), (0, LANE - ROPE)))
    return jnp.concatenate([w3[:, :, :NOPE].reshape(Q_RANK, -1), rope.reshape(Q_RANK, -1)], axis=1)


def _unperm_wq(g):
    return jnp.concatenate([g[:, :1024].reshape(Q_RANK, HEADS, NOPE),
                            g[:, 1024:].reshape(Q_RANK, HEADS, LANE)[:, :, :ROPE]], axis=2).reshape(Q_RANK, -1)


def _perm_wkv(w):
    w3 = w.reshape(KV_RANK, HEADS, NOPE + VDIM)
    return jnp.concatenate([w3[:, :, :NOPE].reshape(KV_RANK, -1), w3[:, :, NOPE:].reshape(KV_RANK, -1)], axis=1)


def _unperm_wkv(g):
    return jnp.concatenate([g[:, :1024].reshape(KV_RANK, HEADS, NOPE), g[:, 1024:].reshape(KV_RANK, HEADS, VDIM)],
                           axis=2).reshape(KV_RANK, -1)


def _pad_rows(w):
    return jnp.pad(w, ((0, LANE - LORA), (0, 0)))


def _perm_mu(mu):
    parts = [mu[:3072]]
    for i in range(4):
        parts += [mu[3072 + LORA * i:3072 + LORA * (i + 1)], jnp.zeros((LANE - LORA,), mu.dtype)]
    return jnp.concatenate(parts).reshape(1, NLERP)


def _unperm_mu(g):
    g = g.reshape(-1)
    return jnp.concatenate([g[:3072]] + [g[3072 + LANE * i:3072 + LANE * i + LORA] for i in range(4)])


def _constants():
    g2 = np.kron(np.eye(2, dtype=np.float32), np.ones((RN, RN), np.float32))
    pos = jnp.arange(T, dtype=F32)
    inv_freq = jnp.power(ROPE_THETA, -jnp.arange(0, ROPE, 2, dtype=F32) / ROPE)
    ang = pos[:, None] * inv_freq[None, :]
    cos, sin, zero = jnp.cos(ang), jnp.sin(ang), jnp.zeros((T, LANE - ROPE), F32)
    cq = jnp.tile(jnp.concatenate([cos, cos, zero], axis=1), (1, HEADS))
    sq = jnp.tile(jnp.concatenate([-sin, sin, zero], axis=1), (1, HEADS))
    return jnp.asarray(g2, BF16), cq, sq


def _step(x, tgt, w, m, v):
    x2, tgt2 = x.reshape(T, D), tgt.reshape(T, D)
    g2, cq, sq = _constants()
    row = lambda n: w[n].reshape(1, -1)
    w, m, v = ({**t, 'w_in': t['w_in'].T} for t in (w, m, v))

    core, chip = lax.axis_index("c"), 2 * lax.axis_index("x") + lax.axis_index("y")
    core1, chip1 = core.astype(jnp.int32).reshape(1), chip.astype(jnp.int32).reshape(1)
    own_bf = [_unit_cat([w[n].astype(BF16) for n in u]) for u in UNITS]
    wp = _perm_w_in(gather_weights(own_bf[:1], UNIT_AXIS[:1])[0], own_bf[0], chip)
    full = {}
    mu_p = _perm_mu(w['rwkv_mu'])

    st_pre = Stage("pre", f_pre, [(D, BF16), (D, None)], 256, [0], [0], [F32])
    st_mla = Stage("mla", f_mla, [(1024, BF16), (1024, BF16), (1024, BF16), (LANE, BF16), (1024, BF16)], 256,
                   [0, 1, 2], [0, 1, 2, 3], [BF16] * 3)
    st_rpre = Stage("rwkv_pre", f_rwkv_pre, [(RW, F32)] * 9, 256, [0], list(range(10)), [F32])
    st_rpost = Stage("rwkv_post", f_rwkv_post, [(RW, BF16)], 256, [0, 2, 3, 4, 5, 6], [0, 1, 2],
                     [F32, F32, F32, F32, F32, BF16])
    st_gate = Stage("gate", f_gate, [(RW, BF16)], 256, [0, 1], [], [F32, BF16])
    st_merge = Stage("merge", f_merge, [(D, BF16)], 256, [0, 1, 2, 3], [], [BF16] * 4)

    pre_rows, pre_par = [(x2, D, 0)], [row('g_pre')]
    (h,) = st_pre.fwd(pre_rows, pre_par)
    proj, rest = matmul("mm_in", h, wp, "nt", side=gather_side(own_bf[1:], UNIT_AXIS[1:]))
    full.update(_gathered(UNITS[1:], rest, own_bf[1:], chip))
    wq, wkv = _perm_wq(full['mla_wq_b']), _perm_wkv(full['mla_wkv_b'])
    lora_w = [_pad_rows(full[n]).astype(F32) for n in ('rwkv_w2_f', 'rwkv_w2_b', 'rwkv_a2_f', 'rwkv_a2_b')]

    mla_rows = [(proj, 512, OFF_QA // 512), (proj, 512, OFF_KVA // 512), (proj, 256, OFF_KR // 256),
                (cq, 1024, 0), (sq, 1024, 0), (cq, LANE, 0), (sq, LANE, 0)]
    mla_par = [row('mla_q_norm'), row('mla_kv_norm'), wq, wkv]
    att = st_mla.fwd(mla_rows, mla_par)
    y_mla = attn_fwd(*att)

    lerp = shift_fwd(proj, mu_p)
    rpre_rows = [(lerp, NLERP, 0)]
    rpre_par = [row('rwkv_w0_f'), row('rwkv_w0_b'), row('rwkv_a0_f'), row('rwkv_a0_b'), row('rwkv_k_k'),
                row('rwkv_k_a')] + lora_w + [g2]
    r_, v_, lwf, lwb, kf, kb, an, bf_, bb_ = st_rpre.fwd(rpre_rows, rpre_par)
    fin = [r_, lwf, kf, v_, an, bf_]
    bin_ = [r_, lwb, kb, v_, an, bb_]
    yf, h0f, uf = scan_fwd("scan_f", *fin, reverse=False)
    yb, h0b, ub = scan_fwd("scan_b", *bin_, reverse=True)
    rpost_rows = [(yf, RW, 0), (yb, RW, 0), (r_, RW, 0), (kf, RW, 0), (kb, RW, 0),
                  (v_, RW, 0), (proj, RW, OFF_ZR // RW)]
    rpost_par = [row('rwkv_gn_g'), row('rwkv_gn_b'), row('rwkv_r_k'), g2]
    (gr,) = st_rpost.fwd(rpost_rows, rpost_par)
    gate_rows = [(y_mla, RW, 0), (proj, RW, OFF_ZM // RW)]
    (gm,) = st_gate.fwd(gate_rows, [])
    um = matmul("mm_br_mla", gm, full['w_br_mla'], "nn")
    ur = matmul("mm_br_rwkv", gr, full['w_br_rwkv'], "nn")
    merge_rows = [(um, D, 0), (ur, D, 0), (proj, D, OFF_GM // D), (proj, D, OFF_GR // D)]
    (merged,) = st_merge.fwd(merge_rows, [])
    out = matmul("mm_out", merged, full['w_out'], "nn")
    d_out, dy, dg_post, loss_blk = loss_stage(out, x2, tgt2, row('g_post'))

    gw = {'g_post': dg_post}
    d_merged = matmul("mm_out_dx", d_out, full['w_out'], "nt")
    gw['w_out'] = matmul("mm_out_dw", merged, d_out, "tn")
    (d_um, d_ur, d_gm, d_gr), _ = st_merge.bwd(merge_rows, [], [[(d_merged, D, 0)]])
    d_gmla = matmul("mm_br_mla_dx", d_um, full['w_br_mla'], "nt")
    gw['w_br_mla'] = matmul("mm_br_mla_dw", gm, d_um, "tn")
    d_grw = matmul("mm_br_rwkv_dx", d_ur, full['w_br_rwkv'], "nt")
    gw['w_br_rwkv'] = matmul("mm_br_rwkv_dw", gr, d_ur, "tn")
    (d_ymla, d_zm), _ = st_gate.bwd(gate_rows, [], [[(d_gmla, RW, 0)]])
    (d_y, d_r3, d_kf2, d_kb2, d_v3, d_zr), (gw['rwkv_gn_g'], gw['rwkv_gn_b'], d_rk) = st_rpost.bwd(
        rpost_rows, rpost_par, [[(d_grw, RW, 0)]])
    gw['rwkv_r_k'] = d_rk
    sf = scan_bwd("scan_f_bwd", *fin, h0f, uf, d_y, reverse=False)
    sb = scan_bwd("scan_b_bwd", *bin_, h0b, ub, d_y, reverse=True)
    c = lambda *ts: [(t, RW, 0) for t in ts]
    rpre_cts = [c(sf[0], sb[0], d_r3), c(sf[3], sb[3], d_v3), c(sf[1]), c(sb[1]), c(sf[2], d_kf2), c(sb[2], d_kb2),
                c(sf[4], sb[4]), c(sf[5]), c(sb[5])]
    (d_rin,), rpre_g = st_rpre.bwd(rpre_rows, rpre_par, rpre_cts)
    for n, gval in zip(('rwkv_w0_f', 'rwkv_w0_b', 'rwkv_a0_f', 'rwkv_a0_b', 'rwkv_k_k', 'rwkv_k_a'), rpre_g[:6]):
        gw[n] = gval
    for n, gval in zip(('rwkv_w2_f', 'rwkv_w2_b', 'rwkv_a2_f', 'rwkv_a2_b'), rpre_g[6:]):
        gw[n] = gval[:LORA]
    d_lerp, d_mu = shift_bwd(proj, mu_p, d_rin)
    gw['rwkv_mu'] = _unperm_mu(d_mu)

    mla_cts = [[(t, t.shape[1], 0)] for t in attn_bwd(*att, d_ymla)]
    (d_qa, d_kva, d_kr), (gw['mla_q_norm'], gw['mla_kv_norm'], d_wq, d_wkv) = st_mla.bwd(mla_rows, mla_par, mla_cts)
    gw['mla_wq_b'], gw['mla_wkv_b'] = _unperm_wq(d_wq), _unperm_wkv(d_wkv)

    dproj = jnp.concatenate([d_qa, d_kva, d_lerp[:, :3072], d_zm, d_zr, d_gm, d_gr, d_lerp[:, 3072:], d_kr], axis=1)

    def pair_sums(name, ids):
        send = [jnp.stack([_unit_cat([_shards(n, gw[n])[j].astype(BF16) for n in UNITS[i]]) for j in range(4)])
                for i in ids]
        axes = [UNIT_AXIS[i] for i in ids]
        other = pair_exchange(name, send, axes)
        return [pair_sum(f"pair_sum_{i}", core1, s, o, ax) for i, s, o, ax in zip(ids, send, other, axes)]

    late, early = [0], list(range(1, len(UNITS)))
    pairs_e = pair_sums("pair_exchange_rest", early)
    gw_in, recv_e = matmul("mm_in_dw", dproj, h, "tn", BF16, side=scatter_side(pairs_e))
    gw['w_in'] = _unperm_w_in(gw_in)
    pairs_l = pair_sums("pair_exchange_w_in", late)
    sems, src_fly, land_fly, token = scatter_start("scatter_w_in_start", pairs_l)
    dh = matmul("mm_in_dx", dproj, wp, "nn", after=(token,))
    (grad_x,), (gw['g_pre'],) = st_pre.bwd(pre_rows, pre_par, [[(dh, D, 0)], [(dy, D, 0)]])

    big = [dict() for _ in range(4)]

    def update(name, ids, recv, pairs):
        mine = [sum4(f"sum4_{i}", chip1, r, p, UNIT_AXIS[i]) for i, r, p in zip(ids, recv, pairs)]
        theirs = swap_halves(name, mine)
        for i, mi, th in zip(ids, mine, theirs):
            res = adamw_halves(f"adamw_{i}", core1, *[_unit_cat([t[n] for n in UNITS[i]]) for t in (w, m, v)], mi, th,
                               UNIT_AXIS[i])
            for q in range(4):
                big[q].update(_unit_split(res[q], UNITS[i], 0))
        return res

    small_fly = allgather8_start("gather_small_start", _pack_small(gw, loss_blk[0, :1]))
    last = update("swap_halves_rest", early, recv_e, pairs_e)
    own_small, landed = allgather8_wait("gather_small_wait", *small_fly, last[0])
    dev = 2 * chip + core
    parts = [jnp.where(dev == i, own_small, landed[i]) for i in range(8)]
    small = adamw("adamw_small", _pack_small(w), _pack_small(m), _pack_small(v), parts)
    pairs_l, recv_l = scatter_wait("scatter_w_in_wait", sems, src_fly, land_fly, small[0] + last[0][:1, :1])
    update("swap_halves_w_in", late, recv_l, pairs_l)

    outs = []
    for b_d, s_arr in zip(big, small):
        d = {**b_d, **_unpack_small(s_arr)}
        d['w_in'] = d['w_in'].T
        outs.append([d[n] for n in WEIGHTS])
    loss = small[0][SMALL_LEN // LANE, 0]
    return (loss, grad_x.reshape(1, T, D), *outs[0], *outs[1], *outs[2], *outs[3])


def kernel(x, g_pre, w_in, mla_q_norm, mla_wq_b, mla_kv_norm, mla_wkv_b, rwkv_mu, rwkv_w0_f, rwkv_w2_f, rwkv_w0_b, rwkv_w2_b, rwkv_a0_f, rwkv_a2_f, rwkv_a0_b, rwkv_a2_b, rwkv_k_k, rwkv_k_a, rwkv_r_k, rwkv_gn_g, rwkv_gn_b, w_br_mla, w_br_rwkv, w_out, g_post, loss_target, m_g_pre, m_w_in, m_mla_q_norm, m_mla_wq_b, m_mla_kv_norm, m_mla_wkv_b, m_rwkv_mu, m_rwkv_w0_f, m_rwkv_w2_f, m_rwkv_w0_b, m_rwkv_w2_b, m_rwkv_a0_f, m_rwkv_a2_f, m_rwkv_a0_b, m_rwkv_a2_b, m_rwkv_k_k, m_rwkv_k_a, m_rwkv_r_k, m_rwkv_gn_g, m_rwkv_gn_b, m_w_br_mla, m_w_br_rwkv, m_w_out, m_g_post, v_g_pre, v_w_in, v_mla_q_norm, v_mla_wq_b, v_mla_kv_norm, v_mla_wkv_b, v_rwkv_mu, v_rwkv_w0_f, v_rwkv_w2_f, v_rwkv_w0_b, v_rwkv_w2_b, v_rwkv_a0_f, v_rwkv_a2_f, v_rwkv_a0_b, v_rwkv_a2_b, v_rwkv_k_k, v_rwkv_k_a, v_rwkv_r_k, v_rwkv_gn_g, v_rwkv_gn_b, v_w_br_mla, v_w_br_rwkv, v_w_out, v_g_post):
    given = dict(locals())
    w = {n: given[n] for n in WEIGHTS}
    m = {n: given['m_' + n] for n in WEIGHTS}
    v = {n: given['v_' + n] for n in WEIGHTS}
    return _step(x, loss_target, w, m, v)
```

```python
import functools
import math

import numpy as np
import jax
import jax.numpy as jnp
from jax import lax
from jax.experimental import pallas as pl
from jax.experimental.pallas import tpu as pltpu

F32, BF16 = jnp.float32, jnp.bfloat16
MESH_IDS = pl.DeviceIdType.MESH

D = 2048
T = 2048
HEADS = 8
Q_RANK = 512
KV_RANK = 512
NOPE = 128
ROPE = 64
VDIM = 128
RW = 1024
RH = 16
RN = 64
LORA = 96
D_IN = 10688
NORM_EPS = 1e-6
GN_EPS = 64e-5
ROPE_THETA = 10000.0
ADAM_LR, ADAM_B1, ADAM_B2, ADAM_EPS, ADAM_WD, ADAM_STEP = 0.001, 0.9, 0.999, 1e-08, 0.01, 10

LANE = 128
VMEM_BIG = 56 * 2**20

NP = 11008
OFF_QA, OFF_KVA, OFF_RKV, OFF_ZM, OFF_ZR, OFF_GM, OFF_GR, OFF_LORA, OFF_KR = 0, 512, 1024, 4096, 5120, 6144, 8192, 10240, 10752
NLERP = 3584

CHUNK = 64
NCH = T // CHUNK


def _dg(a, b, ca, cb, batch=False, prec=None):
    bd = ((0,), (0,)) if batch else ((), ())
    return lax.dot_general(a, b, (((ca,), (cb,)), bd), precision=prec, preferred_element_type=F32)


@jax.custom_vjp
def bdot(a, b):
    return _dg(a.astype(BF16), b.astype(BF16), 1, 0)


def _bdot_fwd(a, b):
    return bdot(a, b), (a, b)


def _bdot_bwd(res, g):
    a, b = res
    gb = g.astype(BF16)
    da = _dg(gb, b.astype(BF16), 1, 1)
    db = _dg(a.astype(BF16), gb, 0, 0)
    return da.astype(a.dtype), db.astype(b.dtype)


bdot.defvjp(_bdot_fwd, _bdot_bwd)


def _split(x):
    hi = x.astype(BF16)
    lo = (x - hi.astype(F32)).astype(BF16)
    return hi, lo


@jax.custom_vjp
def gsum(x, g2):
    hi, lo = _split(x)
    return _dg(hi, g2, 1, 0) + _dg(lo, g2, 1, 0)


def _gsum_fwd(x, g2):
    return gsum(x, g2), g2


def _gsum_bwd(g2, g):
    hi, lo = _split(g)
    return _dg(hi, g2, 1, 1) + _dg(lo, g2, 1, 1), jnp.zeros_like(g2)


gsum.defvjp(_gsum_fwd, _gsum_bwd)


def headsum(x, g2):
    return jnp.concatenate([gsum(x[:, i * LANE:(i + 1) * LANE], g2) for i in range(x.shape[1] // LANE)], axis=1)


def _terms(x, n):
    out = []
    for i in range(n):
        t = x.astype(BF16)
        out.append(t)
        if i < n - 1:
            x = x - t.astype(F32)
    return out


def _bmm(a, b, ca, cb, na, nb):
    acc = None
    for i, ai in enumerate(_terms(a, na)):
        for j, bj in enumerate(_terms(b, nb)):
            if i + j < max(na, nb):
                p = _dg(ai, bj, ca, cb, True)
                acc = p if acc is None else acc + p
    return acc


_NN, _NT, _TN = (2, 1), (2, 2), (1, 1)


def _make_dots(nf, nb_nn, nb_nt, nb_tn):
    @jax.custom_vjp
    def nn(a, b):
        return _bmm(a, b, *_NN, nf, nf)

    @jax.custom_vjp
    def nt(a, b):
        return _bmm(a, b, *_NT, nf, nf)

    @jax.custom_vjp
    def tn(a, b):
        return _bmm(a, b, *_TN, nf, nf)

    nn.defvjp(lambda a, b: (nn(a, b), (a, b)),
              lambda r, g: (_bmm(g, r[1], *_NT, nb_nn, nb_nn), _bmm(r[0], g, *_TN, nb_nn, nb_nn)))
    nt.defvjp(lambda a, b: (nt(a, b), (a, b)),
              lambda r, g: (_bmm(g, r[1], *_NN, 1, nb_nt), _bmm(g, r[0], *_TN, 1, nb_nt)))
    tn.defvjp(lambda a, b: (tn(a, b), (a, b)),
              lambda r, g: (_bmm(r[1], g, *_NT, nb_tn, nb_tn), _bmm(r[0], g, *_NN, nb_tn, nb_tn)))
    return nn, nt, tn


_SCAN_NF, _SCAN_NB = 1, 1
nn, nt, tn = _make_dots(_SCAN_NF, 1, 2, 1)
_, nt_state, _ = _make_dots(_SCAN_NF, 1, 1, 1)


@jax.custom_vjp
def cumdot(ones, x):
    return _bmm(ones, x, *_NN, 1, 2)


cumdot.defvjp(lambda o, x: (cumdot(o, x), o), lambda o, g: (jnp.zeros_like(o), _bmm(o, g, *_TN, 1, 2)))


def _solve_powers(l):
    pw = [l]
    for _ in range(int(math.log2(l.shape[-1])) - 1):
        pw.append(_bmm(pw[-1], pw[-1], *_NN, _SCAN_NF, _SCAN_NF))
    return pw


@jax.custom_vjp
def tri_solve(l, rhs):
    x = rhs
    for p in _solve_powers(l):
        x = x + _bmm(p, x, *_NN, _SCAN_NF, _SCAN_NF)
    return x


def _tri_solve_fwd(l, rhs):
    pw = _solve_powers(l)
    x = rhs
    for p in pw:
        x = x + _bmm(p, x, *_NN, _SCAN_NF, _SCAN_NF)
    return x, (pw, x)


def _tri_solve_bwd(res, g):
    pw, x = res
    y = g
    for p in pw:
        y = y + _bmm(p, y, *_TN, _SCAN_NB, _SCAN_NB)
    return _bmm(y, x, *_NT, _SCAN_NB, _SCAN_NB), y


tri_solve.defvjp(_tri_solve_fwd, _tri_solve_bwd)


@jax.custom_vjp
def known_solve(l, rhs, x):
    return x


known_solve.defvjp(lambda l, rhs, x: (x, (_solve_powers(l), x)),
                   lambda res, g: _tri_solve_bwd(res, g) + (jnp.zeros_like(g),))


def _rms(x, g):
    return x * lax.rsqrt(jnp.mean(x * x, axis=-1, keepdims=True) + NORM_EPS) * g


def _softplus(x):
    pos = x > 0
    return jnp.where(pos, x, 0.0) + jnp.log(1.0 + jnp.exp(-jnp.where(pos, x, -x)))


def _silu(z):
    return z * jax.nn.sigmoid(z)


_MM_VMEM_BYTES = 32 * 2**20


def _mm_tiles(m, n, k):
    best = None
    for tm in (2048, 1024, 512, 256):
        for tn_ in (2048, 1024, 512, 256):
            for d in range(k // LANE, 0, -1):
                tk = LANE * d
                if m % tm or n % tn_ or k % tk:
                    continue
                nk = k // tk
                vmem = 4 * tk * (tm + tn_) + 8 * tm * tn_ + (4 * tm * tn_ if nk > 1 else 0)
                if vmem > _MM_VMEM_BYTES:
                    continue
                a_reads = n // tn_ if nk > 1 else 1
                b_reads = 1 if (nk == 1 and n == tn_) else m // tm
                acc_rmw = nk * m * n if nk > 1 else 0
                cost = (a_reads * m * k + b_reads * k * n + acc_rmw, -tm * tn_ * tk)
                if best is None or cost < best[0]:
                    best = (cost, (tm, tn_, tk))
    return best[1]


class Side:
    def __init__(self, ins, outs, sems, start, finish):
        self.ins, self.outs, self.sems, self.start, self.finish = ins, outs, sems, start, finish

    def at_step(self, step, steps, *refs):
        @pl.when(step == 0)
        def _():
            self.start(*refs)

    def at_end(self, step, steps, *refs):
        @pl.when(step == steps - 1)
        def _():
            self.finish(*refs)

    def run(self, *refs):
        self.start(*refs)
        self.finish(*refs)


def matmul(name, a, b, mode, out_dtype=F32, side=None, after=()):
    if mode == "nn":
        (m, k), n = a.shape, b.shape[1]
    elif mode == "nt":
        (m, k), n = a.shape, b.shape[0]
    else:
        (k, m), n = a.shape, b.shape[1]
    tm, tn_, tk = _mm_tiles(m, n, k)
    nk = k // tk
    if mode == "nn":
        a_spec = pl.BlockSpec((tm, tk), lambda i, j, kk: (i, kk))
        b_spec = pl.BlockSpec((tk, tn_), lambda i, j, kk: (kk, j))
        ca, cb = 1, 0
    elif mode == "nt":
        a_spec = pl.BlockSpec((tm, tk), lambda i, j, kk: (i, kk))
        b_spec = pl.BlockSpec((tn_, tk), lambda i, j, kk: (j, kk))
        ca, cb = 1, 1
    else:
        a_spec = pl.BlockSpec((tk, tm), lambda i, j, kk: (kk, i))
        b_spec = pl.BlockSpec((tk, tn_), lambda i, j, kk: (kk, j))
        ca, cb = 0, 0

    grid = (m // tm, n // tn_, nk)
    n_in = len(side.ins) if side else 0
    n_out = len(side.outs) if side else 0
    n_dep = len(after)

    def body(a_ref, b_ref, *rest):
        rest = rest[n_dep:]
        s_ins, o_ref, s_outs = rest[:n_in], rest[n_in], rest[n_in + 1:n_in + 1 + n_out]
        scratch = rest[n_in + 1 + n_out:]
        acc, s_sems = (scratch[:1], scratch[1:]) if nk > 1 else ((), scratch)
        steps = grid[0] * grid[1] * grid[2]
        if side:
            step = (pl.program_id(0) * grid[1] + pl.program_id(1)) * grid[2] + pl.program_id(2)
            side.at_step(step, steps, s_ins, s_outs, s_sems)

        part = _dg(a_ref[...].astype(BF16), b_ref[...].astype(BF16), ca, cb)
        if nk == 1:
            o_ref[...] = part.astype(o_ref.dtype)
        else:
            acc_ref, kk = acc[0], pl.program_id(2)

            @pl.when(kk == 0)
            def _():
                acc_ref[...] = part

            @pl.when(kk > 0)
            def _():
                acc_ref[...] += part

            @pl.when(kk == nk - 1)
            def _():
                o_ref[...] = acc_ref[...].astype(o_ref.dtype)

        if side:
            side.at_end(step, steps, s_ins, s_outs, s_sems)

    res = pl.pallas_call(
        body, name=name, grid=grid,
        in_specs=[a_spec, b_spec] + [_ANY] * (n_dep + n_in),
        out_specs=[pl.BlockSpec((tm, tn_), lambda i, j, kk: (i, j))] + [_ANY] * n_out,
        out_shape=[jax.ShapeDtypeStruct((m, n), out_dtype)] + (list(side.outs) if side else []),
        scratch_shapes=([pltpu.VMEM((tm, tn_), F32)] if nk > 1 else []) + (list(side.sems) if side else []),
        compiler_params=pltpu.CompilerParams(
            dimension_semantics=("arbitrary",) * 3 if side else ("parallel", "parallel", "arbitrary"),
            vmem_limit_bytes=VMEM_BIG),
    )(a, b, *after, *(side.ins if side else []))
    return (res[0], res[1:]) if side else res[0]


def _rspec(tr, width, blk):
    return pl.BlockSpec((tr, width), lambda i: (i, blk))


def _full_spec(arr):
    return pl.BlockSpec(arr.shape, lambda i: (0,) * arr.ndim)


class Stage:
    def __init__(self, name, f, outs, tr, diff_rows, diff_params, drow_dtypes):
        self.name, self.f, self.outs, self.tr = name, f, outs, tr
        self.diff_rows, self.diff_params, self.drow_dtypes = diff_rows, diff_params, drow_dtypes

    def fwd(self, rows, params):
        f, nr, npar = self.f, len(rows), len(params)
        stored = [(w, dt) for (w, dt) in self.outs if dt is not None]
        keep = [i for i, (w, dt) in enumerate(self.outs) if dt is not None]

        def body(*refs):
            vals = f(*[r[...].astype(F32) for r in refs[:nr]], *[p[...] for p in refs[nr:nr + npar]])
            for o_ref, i in zip(refs[nr + npar:], keep):
                o_ref[...] = vals[i].astype(o_ref.dtype)

        return pl.pallas_call(
            body, name=self.name + "_fwd", grid=(T // self.tr,),
            in_specs=[_rspec(self.tr, w, b) for (_, w, b) in rows] + [_full_spec(p) for p in params],
            out_specs=[_rspec(self.tr, w, 0) for (w, _) in stored],
            out_shape=[jax.ShapeDtypeStruct((T, w), dt) for (w, dt) in stored],
            compiler_params=pltpu.CompilerParams(dimension_semantics=("arbitrary",), vmem_limit_bytes=VMEM_BIG),
        )(*[r[0] for r in rows], *params)

    def bwd(self, rows, params, cts):
        f, nr, npar = self.f, len(rows), len(params)
        dr_idx, dp_idx = self.diff_rows, self.diff_params
        flat_cts = [c for lst in cts for c in lst]
        nct = len(flat_cts)

        def body(*refs):
            row_refs, par_refs = refs[:nr], refs[nr:nr + npar]
            ct_refs = refs[nr + npar:nr + npar + nct]
            drow_refs = refs[nr + npar + nct:nr + npar + nct + len(dr_idx)]
            dpar_refs = refs[nr + npar + nct + len(dr_idx):]
            row_vals = [r[...].astype(F32) for r in row_refs]
            par_vals = [p[...] for p in par_refs]

            def g(*dv):
                rv, pv = list(row_vals), list(par_vals)
                for j, i in enumerate(dr_idx):
                    rv[i] = dv[j]
                for j, i in enumerate(dp_idx):
                    pv[i] = dv[len(dr_idx) + j]
                return f(*rv, *pv)

            _, vjp = jax.vjp(g, *[row_vals[i] for i in dr_idx], *[par_vals[i] for i in dp_idx])
            ct_vals, pos = [], 0
            for lst in cts:
                acc = ct_refs[pos][...].astype(F32)
                for q in range(1, len(lst)):
                    acc = acc + ct_refs[pos + q][...].astype(F32)
                pos += len(lst)
                ct_vals.append(acc)
            grads = vjp(tuple(ct_vals))
            for j, r in enumerate(drow_refs):
                r[...] = grads[j].astype(r.dtype)

            @pl.when(pl.program_id(0) == 0)
            def _():
                for r in dpar_refs:
                    r[...] = jnp.zeros_like(r)

            for j, r in enumerate(dpar_refs):
                r[...] += grads[len(dr_idx) + j].astype(F32)

        drow_shapes = [jax.ShapeDtypeStruct((T, rows[i][1]), dt) for i, dt in zip(dr_idx, self.drow_dtypes)]
        dpar_shapes = [jax.ShapeDtypeStruct(params[i].shape, F32) for i in dp_idx]
        res = pl.pallas_call(
            body, name=self.name + "_bwd", grid=(T // self.tr,),
            in_specs=[_rspec(self.tr, w, b) for (_, w, b) in rows] + [_full_spec(p) for p in params]
            + [_rspec(self.tr, w, b) for (_, w, b) in flat_cts],
            out_specs=[_rspec(self.tr, rows[i][1], 0) for i in dr_idx] + [_full_spec(params[i]) for i in dp_idx],
            out_shape=drow_shapes + dpar_shapes,
            compiler_params=pltpu.CompilerParams(dimension_semantics=("arbitrary",), vmem_limit_bytes=VMEM_BIG),
        )(*[r[0] for r in rows], *params, *[c[0] for c in flat_cts])
        return res[:len(dr_idx)], res[len(dr_idx):]


def f_pre(x, g):
    return _rms(x, g), x


@jax.custom_vjp
def swap32(t):
    width = t.shape[1]
    lane = lax.broadcasted_iota(jnp.int32, t.shape, 1) % LANE
    return jnp.where(lane < 32, pltpu.roll(t, width - 32, 1), jnp.where(lane < 64, pltpu.roll(t, 32, 1), 0.0))


swap32.defvjp(lambda t: (swap32(t), None), lambda _, g: (swap32(g),))


def f_mla(q_a, kv_a, kr, cq, sq, ck, sk, gq, gkv, wq, wkv):
    q = bdot(_rms(q_a, gq), wq)
    kv = bdot(_rms(kv_a, gkv), wkv)
    t, k = q[:, 1024:], kr[:, :LANE]
    return (q[:, :1024], t * cq + swap32(t) * sq, kv[:, :1024], k * ck + swap32(k) * sk, kv[:, 1024:])


def f_rwkv_pre(lerp, w0f, w0b, a0f, a0b, kkw, kaw, w2f, w2b, a2f, a2b, g2):
    r, k, v = lerp[:, :RW], lerp[:, RW:2 * RW], lerp[:, 2 * RW:3 * RW]
    wdf, wdb, adf, adb = (lerp[:, 3 * RW + i * LANE:3 * RW + (i + 1) * LANE] for i in range(4))

    def logdecay(w0, wd, w2):
        z = w0 + bdot(jnp.tanh(wd), w2)
        return -jnp.exp(-_softplus(-z) - 0.5)

    a_f = jax.nn.sigmoid(a0f + bdot(adf, a2f))
    a_b = jax.nn.sigmoid(a0b + bdot(adb, a2b))
    kk = k * kkw
    kk = kk / jnp.maximum(jnp.sqrt(headsum(kk * kk, g2)), 1e-12)
    return (r, v, logdecay(w0f, wdf, w2f), logdecay(w0b, wdb, w2b),
            k * (1.0 + (a_f - 1.0) * kaw), k * (1.0 + (a_b - 1.0) * kaw), -kk, kk * a_f, kk * a_b)


def f_rwkv_post(yf, yb, r, kf, kb, v, z, gng, gnb, rk, g2):
    y = yf + yb
    mu = headsum(y, g2) * (1.0 / RN)
    d = y - mu
    var = headsum(d * d, g2) * (1.0 / RN)
    yn = d * lax.rsqrt(var + GN_EPS) * gng + gnb
    bonus = headsum(r * (kf + kb) * rk, g2) * v
    return ((yn + bonus) * _silu(z),)


def f_gate(y, z):
    return (y * _silu(z),)


def f_merge(um, ur, gm, gr):
    return (jax.nn.sigmoid(gm) * um + jax.nn.sigmoid(gr) * ur,)


_SHIFT_W = 256


def _lerp_colblock(j):
    return jnp.where(j < 3072 // _SHIFT_W, OFF_RKV // _SHIFT_W + j, OFF_LORA // _SHIFT_W + j - 3072 // _SHIFT_W)


def _nbr_mean(x):
    row = lax.broadcasted_iota(jnp.int32, x.shape, 0)
    up = jnp.where(row == 0, 0.0, pltpu.roll(x, 1, 0))
    dn = jnp.where(row == T - 1, 0.0, pltpu.roll(x, T - 1, 0))
    return 0.5 * (up + dn)


def shift_fwd(proj, mu):
    def body(x_ref, mu_ref, o_ref):
        x = x_ref[...]
        o_ref[...] = x + mu_ref[...] * (_nbr_mean(x) - x)

    return pl.pallas_call(
        body, name="shift_fwd", grid=(NLERP // _SHIFT_W,),
        in_specs=[pl.BlockSpec((T, _SHIFT_W), lambda j: (0, _lerp_colblock(j))),
                  pl.BlockSpec((1, _SHIFT_W), lambda j: (0, j))],
        out_specs=pl.BlockSpec((T, _SHIFT_W), lambda j: (0, j)),
        out_shape=jax.ShapeDtypeStruct((T, NLERP), F32),
        compiler_params=pltpu.CompilerParams(dimension_semantics=("parallel",), vmem_limit_bytes=VMEM_BIG),
    )(proj, mu)


def shift_bwd(proj, mu, g):
    def body(x_ref, mu_ref, g_ref, dx_ref, dmu_ref):
        x, gv = x_ref[...], g_ref[...]
        dmu_ref[...] = jnp.sum(gv * (_nbr_mean(x) - x), axis=0, keepdims=True)
        gm = gv * mu_ref[...]
        dx_ref[...] = (gv - gm + _nbr_mean(gm)).astype(dx_ref.dtype)

    col = pl.BlockSpec((T, _SHIFT_W), lambda j: (0, j))
    vec = pl.BlockSpec((1, _SHIFT_W), lambda j: (0, j))
    return pl.pallas_call(
        body, name="shift_bwd", grid=(NLERP // _SHIFT_W,),
        in_specs=[pl.BlockSpec((T, _SHIFT_W), lambda j: (0, _lerp_colblock(j))), vec, col],
        out_specs=[col, vec],
        out_shape=[jax.ShapeDtypeStruct((T, NLERP), BF16), jax.ShapeDtypeStruct((1, NLERP), F32)],
        compiler_params=pltpu.CompilerParams(dimension_semantics=("parallel",), vmem_limit_bytes=VMEM_BIG),
    )(proj, mu, g)


_TQ_F, _TQ_B = 256, 512
_ATT_SCALE = (NOPE + ROPE) ** -0.5


def _probs(q, k):
    s = _dg(q, k, 1, 1) * _ATT_SCALE
    e = jnp.exp(s - jnp.max(s, axis=-1, keepdims=True))
    return e * (1.0 / jnp.sum(e, axis=-1, keepdims=True))


def _q_blk(tq):
    return pl.BlockSpec((tq, LANE), lambda h, i: (i, h))


_K_BLK = pl.BlockSpec((T, LANE), lambda h, i: (0, h))
_KR_BLK = pl.BlockSpec((T, LANE), lambda h, i: (0, 0))


def _load_qk(qn_ref, qr_ref, kn_ref, kr_ref, kcat_ref):
    @pl.when(pl.program_id(1) == 0)
    def _():
        kcat_ref[:, :LANE] = kn_ref[...]
        kcat_ref[:, LANE:] = kr_ref[...]

    return jnp.concatenate([qn_ref[...], qr_ref[...]], axis=1), kcat_ref[...]


def attn_fwd(qn, qr, kn, kr, v):
    def body(qn_ref, qr_ref, kn_ref, kr_ref, v_ref, o_ref, kcat_ref):
        q, k = _load_qk(qn_ref, qr_ref, kn_ref, kr_ref, kcat_ref)
        o_ref[...] = _dg(_probs(q, k).astype(BF16), v_ref[...], 1, 0)

    return pl.pallas_call(
        body, name="attn_fwd", grid=(HEADS, T // _TQ_F),
        in_specs=[_q_blk(_TQ_F), _q_blk(_TQ_F), _K_BLK, _KR_BLK, _K_BLK], out_specs=_q_blk(_TQ_F),
        out_shape=jax.ShapeDtypeStruct((T, HEADS * VDIM), F32),
        scratch_shapes=[pltpu.VMEM((T, 2 * LANE), BF16)],
        compiler_params=pltpu.CompilerParams(dimension_semantics=("arbitrary", "arbitrary"), vmem_limit_bytes=VMEM_BIG),
    )(qn, qr, kn, kr, v)


def attn_bwd(qn, qr, kn, kr, v, do):
    def body(qn_ref, qr_ref, kn_ref, kr_ref, v_ref, do_ref, dqn_ref, dqr_ref, dkn_ref, dkr_ref, dv_ref, kcat_ref):
        h, i = pl.program_id(0), pl.program_id(1)

        @pl.when(i == 0)
        def _():
            dkn_ref[...] = jnp.zeros_like(dkn_ref)
            dv_ref[...] = jnp.zeros_like(dv_ref)

        @pl.when((i == 0) & (h == 0))
        def _():
            dkr_ref[...] = jnp.zeros_like(dkr_ref)

        q, k = _load_qk(qn_ref, qr_ref, kn_ref, kr_ref, kcat_ref)
        dob = do_ref[...].astype(BF16)
        p = _probs(q, k)
        dv_ref[...] += _dg(p.astype(BF16), dob, 0, 0)
        dp = _dg(dob, v_ref[...], 1, 1)
        ds = (p * (dp - jnp.sum(dp * p, axis=-1, keepdims=True)) * _ATT_SCALE).astype(BF16)
        dq = _dg(ds, k, 1, 0)
        dqn_ref[...] = dq[:, :LANE]
        dqr_ref[...] = dq[:, LANE:]
        dk = _dg(ds, q, 0, 0)
        dkn_ref[...] += dk[:, :LANE]
        dkr_ref[...] += dk[:, LANE:]

    wide = jax.ShapeDtypeStruct((T, HEADS * LANE), F32)
    return pl.pallas_call(
        body, name="attn_bwd", grid=(HEADS, T // _TQ_B),
        in_specs=[_q_blk(_TQ_B), _q_blk(_TQ_B), _K_BLK, _KR_BLK, _K_BLK, _q_blk(_TQ_B)],
        out_specs=[_q_blk(_TQ_B), _q_blk(_TQ_B), _K_BLK, _KR_BLK, _K_BLK],
        out_shape=[wide, wide, wide, jax.ShapeDtypeStruct((T, LANE), F32), wide],
        scratch_shapes=[pltpu.VMEM((T, 2 * LANE), BF16)],
        compiler_params=pltpu.CompilerParams(dimension_semantics=("arbitrary", "arbitrary"), vmem_limit_bytes=VMEM_BIG),
    )(qn, qr, kn, kr, v, do)


def _chunk(r, lw, k, v, a, b, ht, u_kept=None, *, reverse):
    hb, c, _ = r.shape
    ti = lax.broadcasted_iota(jnp.int32, (c, c), 0)
    si = lax.broadcasted_iota(jnp.int32, (c, c), 1)
    incl = (si >= ti) if reverse else (si <= ti)
    strict = (si > ti) if reverse else (si < ti)
    ones = jnp.broadcast_to(incl.astype(F32)[None], (hb, c, c))
    cum = cumdot(ones, lw)
    cum_ex = cum - lw
    tot = jnp.sum(lw, axis=1, keepdims=True)
    mid = 0.5 * tot
    rt, at = r * jnp.exp(cum - mid), a * jnp.exp(cum_ex - mid)
    einv = jnp.exp(mid - cum)
    bk = jnp.concatenate([b * einv, k * einv], axis=1)
    m_a, m_r = nt(at, bk), nt(rt, bk)
    m_ab = jnp.where(strict, m_a[:, :, :c], 0.0)
    m_ak = jnp.where(strict, m_a[:, :, c:], 0.0)
    t2 = lax.broadcasted_iota(jnp.int32, (c, 2 * c), 0)
    s2 = lax.broadcasted_iota(jnp.int32, (c, 2 * c), 1)
    s2 = jnp.where(s2 >= c, s2 - c, s2)
    m_r = jnp.where((s2 >= t2) if reverse else (s2 <= t2), m_r, 0.0)
    rhs = nt_state(a * jnp.exp(cum_ex), ht) + nn(m_ak, v)
    u = tri_solve(m_ab, rhs) if u_kept is None else known_solve(m_ab, rhs, u_kept)
    uv = jnp.concatenate([u, v], axis=1)
    y = nt_state(r * jnp.exp(cum), ht) + nn(m_r, uv)
    eend = jnp.exp(tot - cum)
    ht_new = ht * jnp.exp(tot) + tn(uv, jnp.concatenate([b * eend, k * eend], axis=1))
    return y, ht_new, u


_HB_F, _HB_B = 16, 16


def _split_heads(x):
    return jnp.stack([x[:, i * RN:(i + 1) * RN] for i in range(x.shape[1] // RN)])


def _merge_heads(y):
    return jnp.concatenate([y[i] for i in range(y.shape[0])], axis=1)


def _chunk_map(reverse, backward):
    flip = reverse != backward
    return (lambda g, c: (NCH - 1 - c, g)) if flip else (lambda g, c: (c, g))


def scan_fwd(name, r, lw, k, v, a, b, reverse):
    hb = _HB_F
    cmap = _chunk_map(reverse, False)

    def body(r_ref, lw_ref, k_ref, v_ref, a_ref, b_ref, y_ref, h0_ref, u_ref, ht_ref):
        @pl.when(pl.program_id(1) == 0)
        def _():
            ht_ref[...] = jnp.zeros_like(ht_ref)

        ht = ht_ref[...]
        h0_ref[0] = ht
        ins = [_split_heads(x[...]) for x in (r_ref, lw_ref, k_ref, v_ref, a_ref, b_ref)]
        y, hn, u = _chunk(*ins, ht, reverse=reverse)
        y_ref[...] = _merge_heads(y)
        u_ref[...] = _merge_heads(u)
        ht_ref[...] = hn

    io = pl.BlockSpec((CHUNK, hb * RN), cmap)
    return pl.pallas_call(
        body, name=name, grid=(RH // hb, NCH),
        in_specs=[io] * 6,
        out_specs=[io, pl.BlockSpec((1, hb, RN, RN), lambda g, c: (cmap(g, c)[0], g, 0, 0)), io],
        out_shape=[jax.ShapeDtypeStruct((T, RW), F32), jax.ShapeDtypeStruct((NCH, RH, RN, RN), F32),
                   jax.ShapeDtypeStruct((T, RW), F32)],
        scratch_shapes=[pltpu.VMEM((hb, RN, RN), F32)],
        compiler_params=pltpu.CompilerParams(dimension_semantics=("parallel", "arbitrary"), vmem_limit_bytes=VMEM_BIG),
    )(r, lw, k, v, a, b)


def scan_bwd(name, r, lw, k, v, a, b, h0, u, dy, reverse):
    hb = _HB_B
    cmap = _chunk_map(reverse, True)

    def body(r_ref, lw_ref, k_ref, v_ref, a_ref, b_ref, h0_ref, u_ref, dy_ref, *rest):
        d_refs, dht_ref = rest[:6], rest[6]

        @pl.when(pl.program_id(1) == 0)
        def _():
            dht_ref[...] = jnp.zeros_like(dht_ref)

        ins = [_split_heads(x[...]) for x in (r_ref, lw_ref, k_ref, v_ref, a_ref, b_ref)]
        _, vjp = jax.vjp(functools.partial(_chunk, reverse=reverse), *ins, h0_ref[0], _split_heads(u_ref[...]))
        dy = _split_heads(dy_ref[...])
        grads = vjp((dy, dht_ref[...], jnp.zeros_like(dy)))
        for d_ref, gval in zip(d_refs, grads[:6]):
            d_ref[...] = _merge_heads(gval).astype(d_ref.dtype)
        dht_ref[...] = grads[6]

    io = pl.BlockSpec((CHUNK, hb * RN), cmap)
    return pl.pallas_call(
        body, name=name, grid=(RH // hb, NCH),
        in_specs=[io] * 6 + [pl.BlockSpec((1, hb, RN, RN), lambda g, c: (cmap(g, c)[0], g, 0, 0)), io, io],
        out_specs=[io] * 6,
        out_shape=[jax.ShapeDtypeStruct((T, RW), F32 if i == 1 else BF16) for i in range(6)],
        scratch_shapes=[pltpu.VMEM((hb, RN, RN), F32)],
        compiler_params=pltpu.CompilerParams(dimension_semantics=("parallel", "arbitrary"), vmem_limit_bytes=VMEM_BIG),
    )(r, lw, k, v, a, b, h0, u, dy)


def loss_stage(out, x2, tgt, g_post):
    tr = 256

    def body(o_ref, x_ref, t_ref, g_ref, do_ref, dy_ref, dg_ref, loss_ref):
        @pl.when(pl.program_id(0) == 0)
        def _():
            dg_ref[...] = jnp.zeros_like(dg_ref)
            loss_ref[...] = jnp.zeros_like(loss_ref)

        nrm, vjp = jax.vjp(_rms, o_ref[...], g_ref[...])
        e = x_ref[...] + nrm - t_ref[...]
        s = jnp.sum(jnp.sum(e * e, axis=1, keepdims=True), axis=0, keepdims=True)
        loss_ref[...] += jnp.broadcast_to(s * (0.5 / D), loss_ref.shape)
        dy = e * (1.0 / D)
        do, dg = vjp(dy)
        do_ref[...] = do.astype(do_ref.dtype)
        dy_ref[...] = dy
        dg_ref[...] += dg

    row = pl.BlockSpec((tr, D), lambda i: (i, 0))
    return pl.pallas_call(
        body, name="loss_stage", grid=(T // tr,),
        in_specs=[row, row, row, pl.BlockSpec((1, D), lambda i: (0, 0))],
        out_specs=[row, row, pl.BlockSpec((1, D), lambda i: (0, 0)), pl.BlockSpec((8, LANE), lambda i: (0, 0))],
        out_shape=[jax.ShapeDtypeStruct((T, D), BF16), jax.ShapeDtypeStruct((T, D), F32),
                   jax.ShapeDtypeStruct((1, D), F32), jax.ShapeDtypeStruct((8, LANE), F32)],
        compiler_params=pltpu.CompilerParams(dimension_semantics=("arbitrary",), vmem_limit_bytes=VMEM_BIG),
    )(out, x2, tgt, g_post)


_EW_BLOCK_BYTES = 1 << 20


def _row_tile(rows, cols):
    best = None
    for tr in range(16, rows + 1, 16):
        if rows % tr == 0 and tr * cols * 4 <= _EW_BLOCK_BYTES:
            best = tr
    return best or rows


def _axis_tile(shape, axis, words):
    rows, cols = shape
    n, other, unit = (rows, cols, 16) if axis == 0 else (cols, rows, LANE)
    best = unit if n % unit == 0 else n
    for t in range(unit, n + 1, unit):
        if n % t == 0 and t * other * words * 4 <= _EW_BLOCK_BYTES:
            best = t
    blk = (best, cols) if axis == 0 else (rows, best)
    at = (lambda s: (s, 0)) if axis == 0 else (lambda s: (0, s))
    return blk, n // best, at


def _adamw_update(g, w_ref, m_ref, v_ref, g_ref, d_ref, nm_ref, nv_ref):
    mm = ADAM_B1 * m_ref[...] + (1.0 - ADAM_B1) * g
    vv = ADAM_B2 * v_ref[...] + (1.0 - ADAM_B2) * (g * g)
    m_hat = mm / (1.0 - ADAM_B1 ** ADAM_STEP)
    v_hat = vv / (1.0 - ADAM_B2 ** ADAM_STEP)
    g_ref[...] = g
    d_ref[...] = -ADAM_LR * (m_hat / (jnp.sqrt(v_hat) + ADAM_EPS) + ADAM_WD * w_ref[...])
    nm_ref[...] = mm
    nv_ref[...] = vv


def adamw(name, w, m, v, parts):
    rows, cols = w.shape
    br = _row_tile(rows, cols)
    npart = len(parts)

    def body(w_ref, m_ref, v_ref, *rest):
        g = rest[0][...].astype(F32)
        for p in rest[1:npart]:
            g = g + p[...].astype(F32)
        _adamw_update(g, w_ref, m_ref, v_ref, *rest[npart:])

    blk = pl.BlockSpec((br, cols), lambda i: (i, 0))
    return pl.pallas_call(
        body, name=name, grid=(rows // br,),
        in_specs=[blk] * (3 + npart), out_specs=[blk] * 4,
        out_shape=[jax.ShapeDtypeStruct((rows, cols), F32)] * 4,
        compiler_params=pltpu.CompilerParams(dimension_semantics=("parallel",), vmem_limit_bytes=VMEM_BIG),
    )(w, m, v, *parts)


def adamw_halves(name, place, w, m, v, mine, theirs, axis):
    half_shape = mine.shape
    blk_shape, nb, at = _axis_tile(half_shape, axis, 1)

    def body(p_ref, w_ref, m_ref, v_ref, a_ref, b_ref, *outs):
        own = (pl.program_id(0) // nb) == p_ref[0]
        _adamw_update(jnp.where(own, a_ref[...], b_ref[...]), w_ref, m_ref, v_ref, *outs)

    blk = pl.BlockSpec(blk_shape, lambda i, p: at(i))
    half = pl.BlockSpec(blk_shape, lambda i, p: at(i % nb))
    return pl.pallas_call(
        body, name=name,
        grid_spec=pltpu.PrefetchScalarGridSpec(num_scalar_prefetch=1, grid=(2 * nb,),
                                               in_specs=[blk] * 3 + [half] * 2, out_specs=[blk] * 4),
        out_shape=[jax.ShapeDtypeStruct(w.shape, F32)] * 4,
        compiler_params=pltpu.CompilerParams(dimension_semantics=("arbitrary",), vmem_limit_bytes=VMEM_BIG),
    )(place, w, m, v, mine, theirs)


def pair_sum(name, place, send, other, axis):
    blk_shape, nb, at = _axis_tile(other.shape[1:], axis, 4)

    def body(p_ref, a_ref, b_ref, o_ref):
        o_ref[...] = (a_ref[...].astype(F32) + b_ref[...].astype(F32)).astype(o_ref.dtype)

    blk = pl.BlockSpec((4,) + blk_shape, lambda i, p: (0,) + at(i))
    mine = pl.BlockSpec((4,) + blk_shape, lambda i, p: (0,) + at(p[0] * nb + i))
    return pl.pallas_call(
        body, name=name,
        grid_spec=pltpu.PrefetchScalarGridSpec(num_scalar_prefetch=1, grid=(nb,), in_specs=[mine, blk], out_specs=blk),
        out_shape=jax.ShapeDtypeStruct(other.shape, BF16),
        compiler_params=pltpu.CompilerParams(dimension_semantics=("arbitrary",), vmem_limit_bytes=VMEM_BIG),
    )(place, send, other)


def sum4(name, place, recv, own, axis):
    blk_shape, nb, at = _axis_tile(recv.shape[1:], axis, 4)

    def body(p_ref, r_ref, s_ref, o_ref):
        me = p_ref[0]
        t = [jnp.where(me == j, s_ref[j], r_ref[j]).astype(F32) for j in range(4)]
        o_ref[...] = ((t[0] + t[1]) + t[2]) + t[3]

    blk = pl.BlockSpec((4,) + blk_shape, lambda i, p: (0,) + at(i))
    return pl.pallas_call(
        body, name=name,
        grid_spec=pltpu.PrefetchScalarGridSpec(num_scalar_prefetch=1, grid=(nb,), in_specs=[blk, blk],
                                               out_specs=pl.BlockSpec(blk_shape, lambda i, p: at(i))),
        out_shape=jax.ShapeDtypeStruct(recv.shape[1:], F32),
        compiler_params=pltpu.CompilerParams(dimension_semantics=("arbitrary",), vmem_limit_bytes=VMEM_BIG),
    )(place, recv, own)


_ANY = pl.BlockSpec(memory_space=pl.ANY)


def _place():
    x, y, c = lax.axis_index("x"), lax.axis_index("y"), lax.axis_index("c")
    return x, y, c, 2 * x + y


def _chip_peers(x, y):
    out = []
    for k in (1, 2, 3):
        px = 1 - x if k & 2 else x
        py = 1 - y if k & 1 else y
        out.append((k, px, py, 2 * px + py))
    return out


def _half(c, shape, axis):
    n = shape[axis] // 2
    sl = pl.ds(pl.multiple_of(c * n, 16 if axis == 0 else LANE), n)
    return (sl,) if axis == 0 else (pl.ds(0, shape[0]), sl)


def gather_weights(srcs, axes):
    side = gather_side(srcs, axes)
    n = len(srcs)

    def body(*refs):
        ins, outs, sems = refs[:n], refs[n:2 * n], refs[2 * n:]
        side.run(ins, outs, sems)

    return pl.pallas_call(
        body, name="gather_weights", in_specs=[_ANY] * n, out_specs=[_ANY] * n,
        out_shape=side.outs, scratch_shapes=side.sems,
    )(*srcs)


def gather_side(srcs, axes):
    n = len(srcs)

    def copies(src, dst, sems, want):
        ssem, rsem, fssem, frsem = sems
        x, y, c, me = _place()
        sib = (x, y, 1 - c)
        out = []
        for i in range(n):
            mine, other = _half(c, srcs[i].shape, axes[i]), _half(1 - c, srcs[i].shape, axes[i])
            for k, px, py, peer in _chip_peers(x, y):
                sems_k = dict(send_sem=ssem.at[i, k - 1], recv_sem=rsem.at[i, k - 1], device_id=(px, py, c),
                              device_id_type=MESH_IDS)
                fsems = dict(send_sem=fssem.at[i, k - 1], recv_sem=frsem.at[i, k - 1], device_id=sib,
                             device_id_type=MESH_IDS)
                got = dst[i].at[(peer,) + mine]
                mk = pltpu.make_async_remote_copy
                made = dict(
                    snd=lambda: mk(src_ref=src[i].at[mine], dst_ref=dst[i].at[(me,) + mine], **sems_k),
                    rcv=lambda: mk(src_ref=src[i].at[mine], dst_ref=got, **sems_k),
                    fwd=lambda: mk(src_ref=got, dst_ref=got, **fsems),
                    frcv=lambda: mk(src_ref=got, dst_ref=dst[i].at[(peer,) + other], **fsems))
                out.append([made[w]() for w in want])
        return out

    def start(src, dst, sems):
        for (snd,) in copies(src, dst, sems, ("snd",)):
            snd.start()

    def finish(src, dst, sems):
        cps = copies(src, dst, sems, ("snd", "rcv", "fwd", "frcv"))
        for _, rcv, fwd, _ in cps:
            rcv.wait_recv()
            fwd.start()
        for snd, _, fwd, frcv in cps:
            frcv.wait_recv()
            snd.wait_send()
            fwd.wait_send()

    return Side(list(srcs), [jax.ShapeDtypeStruct((4,) + s.shape, s.dtype) for s in srcs],
                [pltpu.SemaphoreType.DMA((n, 3))] * 4, start, finish)


def pair_exchange(name, srcs, axes):
    n = len(srcs)

    def half_shape(s, axis):
        return (4, s.shape[1] // 2, s.shape[2]) if axis == 0 else (4, s.shape[1], s.shape[2] // 2)

    def body(*refs):
        src, other = refs[:n], refs[n:2 * n]
        ssem, rsem = refs[2 * n:]
        x, y, c, _ = _place()
        cps = []
        for i in range(n):
            idx = (pl.ds(0, 4),) + _half(1 - c, srcs[i].shape[1:], axes[i])
            cps.append(pltpu.make_async_remote_copy(
                src_ref=src[i].at[idx], dst_ref=other[i], send_sem=ssem.at[i], recv_sem=rsem.at[i],
                device_id=(x, y, 1 - c), device_id_type=MESH_IDS))
            cps[-1].start()
        for cp in cps:
            cp.wait()

    return pl.pallas_call(
        body, name=name, in_specs=[_ANY] * n, out_specs=[_ANY] * n,
        out_shape=[jax.ShapeDtypeStruct(half_shape(s, a), s.dtype) for s, a in zip(srcs, axes)],
        scratch_shapes=[pltpu.SemaphoreType.DMA((n,))] * 2,
    )(*srcs)


def scatter_side(srcs):
    n = len(srcs)

    def copies(src, dst, sems, sends_only=False):
        ssem, rsem = sems
        x, y, c, me = _place()
        out = []
        for i in range(n):
            for k, px, py, peer in _chip_peers(x, y):
                sems_k = dict(send_sem=ssem.at[i, k - 1], recv_sem=rsem.at[i, k - 1], device_id=(px, py, c),
                              device_id_type=MESH_IDS)
                snd = pltpu.make_async_remote_copy(src_ref=src[i].at[peer], dst_ref=dst[i].at[me], **sems_k)
                if sends_only:
                    out.append(snd)
                    continue
                out.append((snd, pltpu.make_async_remote_copy(src_ref=src[i].at[peer], dst_ref=dst[i].at[peer],
                                                              **sems_k)))
        return out

    def start(src, dst, sems):
        for snd in copies(src, dst, sems, sends_only=True):
            snd.start()

    def finish(src, dst, sems):
        for snd, rcv in copies(src, dst, sems):
            rcv.wait_recv()
            snd.wait_send()

    return Side(list(srcs), [jax.ShapeDtypeStruct(s.shape, s.dtype) for s in srcs],
                [pltpu.SemaphoreType.DMA((n, 3))] * 2, start, finish)


_HBM = pl.BlockSpec(memory_space=pltpu.HBM)
_SEM = pl.BlockSpec(memory_space=pltpu.SEMAPHORE)
_DATAFLOW = pltpu.SideEffectType.DATAFLOW_SIDE_EFFECTING


def scatter_start(name, srcs):
    n = len(srcs)
    side = scatter_side(srcs)
    ns = 3 * n

    def body(*refs):
        src, land = refs[:n], refs[n:2 * n]
        sems = refs[2 * n:2 * n + 2 * ns]
        side.start(src, land, (_SemGrid(sems[:ns]), _SemGrid(sems[ns:])))
        refs[-1][...] = jnp.zeros_like(refs[-1])

    hbm = [pltpu.HBM(s.shape, s.dtype) for s in srcs]
    res = pl.pallas_call(
        body, name=name,
        out_shape=[pltpu.SemaphoreType.DMA(())] * (2 * ns) + hbm + hbm + [jax.ShapeDtypeStruct((8, LANE), F32)],
        in_specs=[_HBM] * (2 * n),
        out_specs=[_SEM] * (2 * ns) + [_HBM] * (2 * n) + [pl.BlockSpec(memory_space=pltpu.VMEM)],
        input_output_aliases={i: 2 * ns + i for i in range(2 * n)},
        compiler_params=pltpu.CompilerParams(has_side_effects=_DATAFLOW),
    )(*[pltpu.with_memory_space_constraint(s, pltpu.HBM) for s in srcs],
      *[pltpu.with_memory_space_constraint(lax.empty(s.shape, s.dtype), pltpu.HBM) for s in srcs])
    return res[:2 * ns], res[2 * ns:2 * ns + n], res[2 * ns + n:2 * ns + 2 * n], res[-1]


def scatter_wait(name, sems, srcs, lands, after):
    n = len(srcs)
    side = scatter_side(srcs)
    ns = 3 * n

    def body(*refs):
        src, land = refs[:n], refs[n:2 * n]
        s = refs[2 * n:2 * n + 2 * ns]
        side.finish(src, land, (_SemGrid(s[:ns]), _SemGrid(s[ns:])))

    hbm = [pltpu.HBM(s.shape, s.dtype) for s in srcs]
    res = pl.pallas_call(
        body, name=name, out_shape=hbm + hbm,
        in_specs=[_HBM] * (2 * n) + [_SEM] * (2 * ns) + [_ANY], out_specs=[_HBM] * (2 * n),
        input_output_aliases={i: i for i in range(2 * n)},
        compiler_params=pltpu.CompilerParams(has_side_effects=_DATAFLOW),
    )(*srcs, *lands, *sems, after)
    return res[:n], res[n:]


class _SemGrid:
    def __init__(self, sems):
        self.sems = sems

    @property
    def at(self):
        return self

    def __getitem__(self, ik):
        return self.sems[3 * ik[0] + ik[1]]


def swap_halves(name, srcs):
    n = len(srcs)

    def body(*refs):
        src, dst = refs[:n], refs[n:2 * n]
        ssem, rsem = refs[2 * n:]
        x, y, c, _ = _place()
        cps = []
        for i in range(n):
            cps.append(pltpu.make_async_remote_copy(src_ref=src[i], dst_ref=dst[i], send_sem=ssem.at[i],
                                                    recv_sem=rsem.at[i], device_id=(x, y, 1 - c),
                                                    device_id_type=MESH_IDS))
            cps[-1].start()
        for cp in cps:
            cp.wait()

    return pl.pallas_call(
        body, name=name, in_specs=[_ANY] * n, out_specs=[_ANY] * n,
        out_shape=[jax.ShapeDtypeStruct(s.shape, s.dtype) for s in srcs],
        scratch_shapes=[pltpu.SemaphoreType.DMA((n,))] * 2,
    )(*srcs)


def _ag8_copies(src, dst, sems, sends_only=False):
    x, y, c = lax.axis_index("x"), lax.axis_index("y"), lax.axis_index("c")
    me = 4 * x + 2 * y + c
    out = []
    for k in range(1, 8):
        px = 1 - x if k & 4 else x
        py = 1 - y if k & 2 else y
        pc = 1 - c if k & 1 else c
        peer = 4 * px + 2 * py + pc
        out.append(tuple(pltpu.make_async_remote_copy(
            src_ref=src, dst_ref=dst.at[slot], send_sem=sems[k - 1], recv_sem=sems[7 + k - 1],
            device_id=(px, py, pc), device_id_type=MESH_IDS) for slot in ((me,) if sends_only else (me, peer))))
    return out


def allgather8_start(name, src):
    def body(src_ref, land_ref, *rest):
        for (snd,) in _ag8_copies(src_ref, land_ref, rest[:14], sends_only=True):
            snd.start()

    land = jax.ShapeDtypeStruct((8,) + src.shape, src.dtype)
    res = pl.pallas_call(
        body, name=name,
        out_shape=[pltpu.SemaphoreType.DMA(())] * 14 + [pltpu.HBM(src.shape, src.dtype), pltpu.HBM(land.shape, land.dtype)],
        in_specs=[_HBM, _HBM], out_specs=[_SEM] * 14 + [_HBM, _HBM],
        input_output_aliases={0: 14, 1: 15},
        compiler_params=pltpu.CompilerParams(has_side_effects=_DATAFLOW),
    )(pltpu.with_memory_space_constraint(src, pltpu.HBM),
      pltpu.with_memory_space_constraint(lax.empty(land.shape, land.dtype), pltpu.HBM))
    return res[:14], res[14], res[15]


def allgather8_wait(name, sems, src, land, after):
    def body(src_ref, land_ref, *rest):
        for snd, rcv in _ag8_copies(src_ref, land_ref, rest[:14]):
            rcv.wait_recv()
            snd.wait_send()

    return pl.pallas_call(
        body, name=name, out_shape=[pltpu.HBM(src.shape, src.dtype), pltpu.HBM(land.shape, land.dtype)],
        in_specs=[_HBM, _HBM] + [_SEM] * 14 + [_ANY], out_specs=[_HBM, _HBM],
        input_output_aliases={0: 0, 1: 1},
        compiler_params=pltpu.CompilerParams(has_side_effects=_DATAFLOW),
    )(src, land, *sems, after)


WEIGHTS = ['g_pre', 'w_in', 'mla_q_norm', 'mla_wq_b', 'mla_kv_norm', 'mla_wkv_b', 'rwkv_mu', 'rwkv_w0_f', 'rwkv_w2_f',
           'rwkv_w0_b', 'rwkv_w2_b', 'rwkv_a0_f', 'rwkv_a2_f', 'rwkv_a0_b', 'rwkv_a2_b', 'rwkv_k_k', 'rwkv_k_a',
           'rwkv_r_k', 'rwkv_gn_g', 'rwkv_gn_b', 'w_br_mla', 'w_br_rwkv', 'w_out', 'g_post']
BIG_SHAPES = {'w_in': (D_IN // 4, D), 'mla_wq_b': (Q_RANK, 384), 'mla_wkv_b': (KV_RANK, 512),
              'rwkv_w2_f': (LORA, 256), 'rwkv_w2_b': (LORA, 256), 'rwkv_a2_f': (LORA, 256), 'rwkv_a2_b': (LORA, 256),
              'w_br_mla': (RW, 512), 'w_br_rwkv': (RW, 512), 'w_out': (512, D)}
BIG = list(BIG_SHAPES)
SMALL = [n for n in WEIGHTS if n not in BIG_SHAPES]
SMALL_SHAPES = {'g_pre': (D,), 'mla_q_norm': (Q_RANK,), 'mla_kv_norm': (KV_RANK,), 'rwkv_mu': (3456,),
                'rwkv_w0_f': (RW,), 'rwkv_w0_b': (RW,), 'rwkv_a0_f': (RW,), 'rwkv_a0_b': (RW,), 'rwkv_k_k': (RW,),
                'rwkv_k_a': (RW,), 'rwkv_r_k': (RH, RN), 'rwkv_gn_g': (RW,), 'rwkv_gn_b': (RW,), 'g_post': (D,)}
SMALL_LEN = sum(int(np.prod(s)) for s in SMALL_SHAPES.values())
SMALL_ROWS = 144


UNITS = [('w_in',), ('mla_wq_b',), ('mla_wkv_b',), ('rwkv_w2_f', 'rwkv_w2_b', 'rwkv_a2_f', 'rwkv_a2_b'),
         ('w_br_mla', 'w_br_rwkv'), ('w_out',)]
UNIT_AXIS = [1, 0, 0, 0, 0, 0]
ROW_SHARDED = ('w_in', 'w_out')


def _unit_cat(parts):
    return parts[0] if len(parts) == 1 else jnp.concatenate(parts, axis=0)


def _unit_split(arr, names, axis):
    out, o = {}, 0
    for n in names:
        rows = BIG_SHAPES[n][0]
        out[n] = lax.slice_in_dim(arr, o, o + rows, axis=axis)
        o += rows
    return out


def _gathered(units, ag, own, me):
    out = {}
    for names, arr, mine in zip(units, ag, own):
        slots = [jnp.where(me == j, mine, arr[j]) for j in range(4)]
        for n in names:
            parts = [_unit_split(s, names, 0)[n] for s in slots]
            out[n] = jnp.concatenate(parts, axis=0 if n in ROW_SHARDED else 1)
    return out


def _shards(n, g):
    r, w = BIG_SHAPES[n]
    if n in ROW_SHARDED:
        return [g[j * r:(j + 1) * r] for j in range(4)]
    return [g[:, j * w:(j + 1) * w] for j in range(4)]


def _pack_small(d, extra=None):
    flat = jnp.concatenate([d[n].reshape(-1) for n in SMALL] + ([extra.reshape(-1)] if extra is not None else []))
    return jnp.pad(flat, (0, SMALL_ROWS * LANE - flat.shape[0])).reshape(SMALL_ROWS, LANE)


def _unpack_small(packed):
    flat, out, o = packed.reshape(-1), {}, 0
    for n in SMALL:
        sz = int(np.prod(SMALL_SHAPES[n]))
        out[n] = flat[o:o + sz].reshape(SMALL_SHAPES[n])
        o += sz
    return out


def _perm_w_in(gathered, own, me):
    per = D_IN // 4

    def rows(a, b):
        out = []
        while a < b:
            j, lo = divmod(a, per)
            hi = min(b - j * per, per)
            out.append(jnp.where(me == j, own[lo:hi], gathered[j, lo:hi]))
            a = j * per + hi
        return out

    z = lambda n: [jnp.zeros((n, own.shape[1]), own.dtype)]
    lora = []
    for i in range(4):
        lora += rows(4160 + LORA * i, 4160 + LORA * (i + 1)) + z(LANE - LORA)
    return jnp.concatenate(rows(0, 1024) + rows(1088, 4160) + rows(4544, D_IN) + lora + rows(1024, 1088)
                           + z(256 - ROPE), axis=0)


def _unperm_w_in(g):
    lora = [g[OFF_LORA + LANE * i:OFF_LORA + LANE * i + LORA] for i in range(4)]
    return jnp.concatenate([g[0:1024], g[OFF_KR:OFF_KR + ROPE], g[1024:4096]] + lora + [g[4096:OFF_LORA]], axis=0)


def _perm_wq(w):
    w3 = w.reshape(Q_RANK, HEADS, NOPE + ROPE)
    rope = jnp.pad(w3[:, :, NOPE:], ((0, 0), (0, 0), (0, LANE - ROPE)))
    return jnp.concatenate([w3[:, :, :NOPE].reshape(Q_RANK, -1), rope.reshape(Q_RANK, -1)], axis=1)


def _unperm_wq(g):
    return jnp.concatenate([g[:, :1024].reshape(Q_RANK, HEADS, NOPE),
                            g[:, 1024:].reshape(Q_RANK, HEADS, LANE)[:, :, :ROPE]], axis=2).reshape(Q_RANK, -1)


def _perm_wkv(w):
    w3 = w.reshape(KV_RANK, HEADS, NOPE + VDIM)
    return jnp.concatenate([w3[:, :, :NOPE].reshape(KV_RANK, -1), w3[:, :, NOPE:].reshape(KV_RANK, -1)], axis=1)


def _unperm_wkv(g):
    return jnp.concatenate([g[:, :1024].reshape(KV_RANK, HEADS, NOPE), g[:, 1024:].reshape(KV_RANK, HEADS, VDIM)],
                           axis=2).reshape(KV_RANK, -1)


def _pad_rows(w):
    return jnp.pad(w, ((0, LANE - LORA), (0, 0)))


def _perm_mu(mu):
    parts = [mu[:3072]]
    for i in range(4):
        parts += [mu[3072 + LORA * i:3072 + LORA * (i + 1)], jnp.zeros((LANE - LORA,), mu.dtype)]
    return jnp.concatenate(parts).reshape(1, NLERP)


def _unperm_mu(g):
    g = g.reshape(-1)
    return jnp.concatenate([g[:3072]] + [g[3072 + LANE * i:3072 + LANE * i + LORA] for i in range(4)])


def _constants():
    g2 = np.kron(np.eye(2, dtype=np.float32), np.ones((RN, RN), np.float32))
    pos = jnp.arange(T, dtype=F32)
    inv_freq = jnp.power(ROPE_THETA, -jnp.arange(0, ROPE, 2, dtype=F32) / ROPE)
    ang = pos[:, None] * inv_freq[None, :]
    cos, sin, zero = jnp.cos(ang), jnp.sin(ang), jnp.zeros((T, LANE - ROPE), F32)
    cq = jnp.tile(jnp.concatenate([cos, cos, zero], axis=1), (1, HEADS))
    sq = jnp.tile(jnp.concatenate([-sin, sin, zero], axis=1), (1, HEADS))
    return jnp.asarray(g2, BF16), cq, sq


def _step(x, tgt, w, m, v):
    x2, tgt2 = x.reshape(T, D), tgt.reshape(T, D)
    g2, cq, sq = _constants()
    row = lambda n: w[n].reshape(1, -1)
    w, m, v = ({**t, 'w_in': t['w_in'].T} for t in (w, m, v))

    core, chip = lax.axis_index("c"), 2 * lax.axis_index("x") + lax.axis_index("y")
    core1, chip1 = core.astype(jnp.int32).reshape(1), chip.astype(jnp.int32).reshape(1)
    own_bf = [_unit_cat([w[n].astype(BF16) for n in u]) for u in UNITS]
    wp = _perm_w_in(gather_weights(own_bf[:1], UNIT_AXIS[:1])[0], own_bf[0], chip)
    full = {}
    mu_p = _perm_mu(w['rwkv_mu'])

    st_pre = Stage("pre", f_pre, [(D, BF16), (D, None)], 256, [0], [0], [F32])
    st_mla = Stage("mla", f_mla, [(1024, BF16), (1024, BF16), (1024, BF16), (LANE, BF16), (1024, BF16)], 256,
                   [0, 1, 2], [0, 1, 2, 3], [BF16] * 3)
    st_rpre = Stage("rwkv_pre", f_rwkv_pre, [(RW, F32)] * 9, 256, [0], list(range(10)), [F32])
    st_rpost = Stage("rwkv_post", f_rwkv_post, [(RW, BF16)], 256, [0, 2, 3, 4, 5, 6], [0, 1, 2],
                     [F32, F32, F32, F32, F32, BF16])
    st_gate = Stage("gate", f_gate, [(RW, BF16)], 256, [0, 1], [], [F32, BF16])
    st_merge = Stage("merge", f_merge, [(D, BF16)], 256, [0, 1, 2, 3], [], [BF16] * 4)

    pre_rows, pre_par = [(x2, D, 0)], [row('g_pre')]
    (h,) = st_pre.fwd(pre_rows, pre_par)
    proj, rest = matmul("mm_in", h, wp, "nt", side=gather_side(own_bf[1:], UNIT_AXIS[1:]))
    full.update(_gathered(UNITS[1:], rest, own_bf[1:], chip))
    wq, wkv = _perm_wq(full['mla_wq_b']), _perm_wkv(full['mla_wkv_b'])
    lora_w = [_pad_rows(full[n]).astype(F32) for n in ('rwkv_w2_f', 'rwkv_w2_b', 'rwkv_a2_f', 'rwkv_a2_b')]

    mla_rows = [(proj, 512, OFF_QA // 512), (proj, 512, OFF_KVA // 512), (proj, 256, OFF_KR // 256),
                (cq, 1024, 0), (sq, 1024, 0), (cq, LANE, 0), (sq, LANE, 0)]
    mla_par = [row('mla_q_norm'), row('mla_kv_norm'), wq, wkv]
    att = st_mla.fwd(mla_rows, mla_par)
    y_mla = attn_fwd(*att)

    lerp = shift_fwd(proj, mu_p)
    rpre_rows = [(lerp, NLERP, 0)]
    rpre_par = [row('rwkv_w0_f'), row('rwkv_w0_b'), row('rwkv_a0_f'), row('rwkv_a0_b'), row('rwkv_k_k'),
                row('rwkv_k_a')] + lora_w + [g2]
    r_, v_, lwf, lwb, kf, kb, an, bf_, bb_ = st_rpre.fwd(rpre_rows, rpre_par)
    fin = [r_, lwf, kf, v_, an, bf_]
    bin_ = [r_, lwb, kb, v_, an, bb_]
    yf, h0f, uf = scan_fwd("scan_f", *fin, reverse=False)
    yb, h0b, ub = scan_fwd("scan_b", *bin_, reverse=True)
    rpost_rows = [(yf, RW, 0), (yb, RW, 0), (r_, RW, 0), (kf, RW, 0), (kb, RW, 0),
                  (v_, RW, 0), (proj, RW, OFF_ZR // RW)]
    rpost_par = [row('rwkv_gn_g'), row('rwkv_gn_b'), row('rwkv_r_k'), g2]
    (gr,) = st_rpost.fwd(rpost_rows, rpost_par)
    gate_rows = [(y_mla, RW, 0), (proj, RW, OFF_ZM // RW)]
    (gm,) = st_gate.fwd(gate_rows, [])
    um = matmul("mm_br_mla", gm, full['w_br_mla'], "nn")
    ur = matmul("mm_br_rwkv", gr, full['w_br_rwkv'], "nn")
    merge_rows = [(um, D, 0), (ur, D, 0), (proj, D, OFF_GM // D), (proj, D, OFF_GR // D)]
    (merged,) = st_merge.fwd(merge_rows, [])
    out = matmul("mm_out", merged, full['w_out'], "nn")
    d_out, dy, dg_post, loss_blk = loss_stage(out, x2, tgt2, row('g_post'))

    gw = {'g_post': dg_post}
    d_merged = matmul("mm_out_dx", d_out, full['w_out'], "nt")
    gw['w_out'] = matmul("mm_out_dw", merged, d_out, "tn")
    (d_um, d_ur, d_gm, d_gr), _ = st_merge.bwd(merge_rows, [], [[(d_merged, D, 0)]])
    d_gmla = matmul("mm_br_mla_dx", d_um, full['w_br_mla'], "nt")
    gw['w_br_mla'] = matmul("mm_br_mla_dw", gm, d_um, "tn")
    d_grw = matmul("mm_br_rwkv_dx", d_ur, full['w_br_rwkv'], "nt")
    gw['w_br_rwkv'] = matmul("mm_br_rwkv_dw", gr, d_ur, "tn")
    (d_ymla, d_zm), _ = st_gate.bwd(gate_rows, [], [[(d_gmla, RW, 0)]])
    (d_y, d_r3, d_kf2, d_kb2, d_v3, d_zr), (gw['rwkv_gn_g'], gw['rwkv_gn_b'], d_rk) = st_rpost.bwd(
        rpost_rows, rpost_par, [[(d_grw, RW, 0)]])
    gw['rwkv_r_k'] = d_rk
    sf = scan_bwd("scan_f_bwd", *fin, h0f, uf, d_y, reverse=False)
    sb = scan_bwd("scan_b_bwd", *bin_, h0b, ub, d_y, reverse=True)
    c = lambda *ts: [(t, RW, 0) for t in ts]
    rpre_cts = [c(sf[0], sb[0], d_r3), c(sf[3], sb[3], d_v3), c(sf[1]), c(sb[1]), c(sf[2], d_kf2), c(sb[2], d_kb2),
                c(sf[4], sb[4]), c(sf[5]), c(sb[5])]
    (d_rin,), rpre_g = st_rpre.bwd(rpre_rows, rpre_par, rpre_cts)
    for n, gval in zip(('rwkv_w0_f', 'rwkv_w0_b', 'rwkv_a0_f', 'rwkv_a0_b', 'rwkv_k_k', 'rwkv_k_a'), rpre_g[:6]):
        gw[n] = gval
    for n, gval in zip(('rwkv_w2_f', 'rwkv_w2_b', 'rwkv_a2_f', 'rwkv_a2_b'), rpre_g[6:]):
        gw[n] = gval[:LORA]
    d_lerp, d_mu = shift_bwd(proj, mu_p, d_rin)
    gw['rwkv_mu'] = _unperm_mu(d_mu)

    mla_cts = [[(t, t.shape[1], 0)] for t in attn_bwd(*att, d_ymla)]
    (d_qa, d_kva, d_kr), (gw['mla_q_norm'], gw['mla_kv_norm'], d_wq, d_wkv) = st_mla.bwd(mla_rows, mla_par, mla_cts)
    gw['mla_wq_b'], gw['mla_wkv_b'] = _unperm_wq(d_wq), _unperm_wkv(d_wkv)

    dproj = jnp.concatenate([d_qa, d_kva, d_lerp[:, :3072], d_zm, d_zr, d_gm, d_gr, d_lerp[:, 3072:], d_kr], axis=1)

    def pair_sums(name, ids):
        send = [jnp.stack([_unit_cat([_shards(n, gw[n])[j].astype(BF16) for n in UNITS[i]]) for j in range(4)])
                for i in ids]
        axes = [UNIT_AXIS[i] for i in ids]
        other = pair_exchange(name, send, axes)
        return [pair_sum(f"pair_sum_{i}", core1, s, o, ax) for i, s, o, ax in zip(ids, send, other, axes)]

    late, early = [0], list(range(1, len(UNITS)))
    pairs_e = pair_sums("pair_exchange_rest", early)
    gw_in, recv_e = matmul("mm_in_dw", dproj, h, "tn", BF16, side=scatter_side(pairs_e))
    gw['w_in'] = _unperm_w_in(gw_in)
    pairs_l = pair_sums("pair_exchange_w_in", late)
    sems, src_fly, land_fly, token = scatter_start("scatter_w_in_start", pairs_l)
    dh = matmul("mm_in_dx", dproj, wp, "nn", after=(token,))
    (grad_x,), (gw['g_pre'],) = st_pre.bwd(pre_rows, pre_par, [[(dh, D, 0)], [(dy, D, 0)]])

    big = [dict() for _ in range(4)]

    def update(name, ids, recv, pairs):
        mine = [sum4(f"sum4_{i}", chip1, r, p, UNIT_AXIS[i]) for i, r, p in zip(ids, recv, pairs)]
        theirs = swap_halves(name, mine)
        for i, mi, th in zip(ids, mine, theirs):
            res = adamw_halves(f"adamw_{i}", core1, *[_unit_cat([t[n] for n in UNITS[i]]) for t in (w, m, v)], mi, th,
                               UNIT_AXIS[i])
            for q in range(4):
                big[q].update(_unit_split(res[q], UNITS[i], 0))
        return res

    small_fly = allgather8_start("gather_small_start", _pack_small(gw, loss_blk[0, :1]))
    last = update("swap_halves_rest", early, recv_e, pairs_e)
    own_small, landed = allgather8_wait("gather_small_wait", *small_fly, last[0])
    dev = 2 * chip + core
    parts = [jnp.where(dev == i, own_small, landed[i]) for i in range(8)]
    small = adamw("adamw_small", _pack_small(w), _pack_small(m), _pack_small(v), parts)
    pairs_l, recv_l = scatter_wait("scatter_w_in_wait", sems, src_fly, land_fly, small[0] + last[0][:1, :1])
    update("swap_halves_w_in", late, recv_l, pairs_l)

    outs = []
    for b_d, s_arr in zip(big, small):
        d = {**b_d, **_unpack_small(s_arr)}
        d['w_in'] = d['w_in'].T
        outs.append([d[n] for n in WEIGHTS])
    loss = small[0][SMALL_LEN // LANE, 0]
    return (loss, grad_x.reshape(1, T, D), *outs[0], *outs[1], *outs[2], *outs[3])


def kernel(x, g_pre, w_in, mla_q_norm, mla_wq_b, mla_kv_norm, mla_wkv_b, rwkv_mu, rwkv_w0_f, rwkv_w2_f, rwkv_w0_b, rwkv_w2_b, rwkv_a0_f, rwkv_a2_f, rwkv_a0_b, rwkv_a2_b, rwkv_k_k, rwkv_k_a, rwkv_r_k, rwkv_gn_g, rwkv_gn_b, w_br_mla, w_br_rwkv, w_out, g_post, loss_target, m_g_pre, m_w_in, m_mla_q_norm, m_mla_wq_b, m_mla_kv_norm, m_mla_wkv_b, m_rwkv_mu, m_rwkv_w0_f, m_rwkv_w2_f, m_rwkv_w0_b, m_rwkv_w2_b, m_rwkv_a0_f, m_rwkv_a2_f, m_rwkv_a0_b, m_rwkv_a2_b, m_rwkv_k_k, m_rwkv_k_a, m_rwkv_r_k, m_rwkv_gn_g, m_rwkv_gn_b, m_w_br_mla, m_w_br_rwkv, m_w_out, m_g_post, v_g_pre, v_w_in, v_mla_q_norm, v_mla_wq_b, v_mla_kv_norm, v_mla_wkv_b, v_rwkv_mu, v_rwkv_w0_f, v_rwkv_w2_f, v_rwkv_w0_b, v_rwkv_w2_b, v_rwkv_a0_f, v_rwkv_a2_f, v_rwkv_a0_b, v_rwkv_a2_b, v_rwkv_k_k, v_rwkv_k_a, v_rwkv_r_k, v_rwkv_gn_g, v_rwkv_gn_b, v_w_br_mla, v_w_br_rwkv, v_w_out, v_g_post):
    given = dict(locals())
    w = {n: given[n] for n in WEIGHTS}
    m = {n: given['m_' + n] for n in WEIGHTS}
    v = {n: given['v_' + n] for n in WEIGHTS}
    return _step(x, loss_target, w, m, v)
```

```python
import functools
import math

import numpy as np
import jax
import jax.numpy as jnp
from jax import lax
from jax.experimental import pallas as pl
from jax.experimental.pallas import tpu as pltpu

F32, BF16 = jnp.float32, jnp.bfloat16
MESH_IDS = pl.DeviceIdType.MESH

D = 2048
T = 2048
HEADS = 8
Q_RANK = 512
KV_RANK = 512
NOPE = 128
ROPE = 64
VDIM = 128
RW = 1024
RH = 16
RN = 64
LORA = 96
D_IN = 10688
NORM_EPS = 1e-6
GN_EPS = 64e-5
ROPE_THETA = 10000.0
ADAM_LR, ADAM_B1, ADAM_B2, ADAM_EPS, ADAM_WD, ADAM_STEP = 0.001, 0.9, 0.999, 1e-08, 0.01, 10

LANE = 128
VMEM_BIG = 56 * 2**20

NP = 11008
OFF_QA, OFF_KVA, OFF_RKV, OFF_ZM, OFF_ZR, OFF_GM, OFF_GR, OFF_LORA, OFF_KR = 0, 512, 1024, 4096, 5120, 6144, 8192, 10240, 10752
NLERP = 3584

CHUNK = 64
NCH = T // CHUNK


def _dg(a, b, ca, cb, batch=False, prec=None):
    bd = ((0,), (0,)) if batch else ((), ())
    return lax.dot_general(a, b, (((ca,), (cb,)), bd), precision=prec, preferred_element_type=F32)


@jax.custom_vjp
def bdot(a, b):
    return _dg(a.astype(BF16), b.astype(BF16), 1, 0)


def _bdot_fwd(a, b):
    return bdot(a, b), (a, b)


def _bdot_bwd(res, g):
    a, b = res
    gb = g.astype(BF16)
    da = _dg(gb, b.astype(BF16), 1, 1)
    db = _dg(a.astype(BF16), gb, 0, 0)
    return da.astype(a.dtype), db.astype(b.dtype)


bdot.defvjp(_bdot_fwd, _bdot_bwd)


def _split(x):
    hi = x.astype(BF16)
    lo = (x - hi.astype(F32)).astype(BF16)
    return hi, lo


@jax.custom_vjp
def gsum(x, g2):
    hi, lo = _split(x)
    return _dg(hi, g2, 1, 0) + _dg(lo, g2, 1, 0)


def _gsum_fwd(x, g2):
    return gsum(x, g2), g2


def _gsum_bwd(g2, g):
    hi, lo = _split(g)
    return _dg(hi, g2, 1, 1) + _dg(lo, g2, 1, 1), jnp.zeros_like(g2)


gsum.defvjp(_gsum_fwd, _gsum_bwd)


def headsum(x, g2):
    return jnp.concatenate([gsum(x[:, i * LANE:(i + 1) * LANE], g2) for i in range(x.shape[1] // LANE)], axis=1)


def _terms(x, n):
    out = []
    for i in range(n):
        t = x.astype(BF16)
        out.append(t)
        if i < n - 1:
            x = x - t.astype(F32)
    return out


def _bmm(a, b, ca, cb, na, nb):
    acc = None
    for i, ai in enumerate(_terms(a, na)):
        for j, bj in enumerate(_terms(b, nb)):
            if i + j < max(na, nb):
                p = _dg(ai, bj, ca, cb, True)
                acc = p if acc is None else acc + p
    return acc


_NN, _NT, _TN = (2, 1), (2, 2), (1, 1)


def _make_dots(nf, nb_nn, nb_nt, nb_tn):
    @jax.custom_vjp
    def nn(a, b):
        return _bmm(a, b, *_NN, nf, nf)

    @jax.custom_vjp
    def nt(a, b):
        return _bmm(a, b, *_NT, nf, nf)

    @jax.custom_vjp
    def tn(a, b):
        return _bmm(a, b, *_TN, nf, nf)

    nn.defvjp(lambda a, b: (nn(a, b), (a, b)),
              lambda r, g: (_bmm(g, r[1], *_NT, nb_nn, nb_nn), _bmm(r[0], g, *_TN, nb_nn, nb_nn)))
    nt.defvjp(lambda a, b: (nt(a, b), (a, b)),
              lambda r, g: (_bmm(g, r[1], *_NN, 1, nb_nt), _bmm(g, r[0], *_TN, 1, nb_nt)))
    tn.defvjp(lambda a, b: (tn(a, b), (a, b)),
              lambda r, g: (_bmm(r[1], g, *_NT, nb_tn, nb_tn), _bmm(r[0], g, *_NN, nb_tn, nb_tn)))
    return nn, nt, tn


_SCAN_NF, _SCAN_NB = 1, 1
nn, nt, tn = _make_dots(_SCAN_NF, 1, 2, 1)
_, nt_state, _ = _make_dots(_SCAN_NF, 1, 1, 1)


@jax.custom_vjp
def cumdot(ones, x):
    return _bmm(ones, x, *_NN, 1, 2)


cumdot.defvjp(lambda o, x: (cumdot(o, x), o), lambda o, g: (jnp.zeros_like(o), _bmm(o, g, *_TN, 1, 2)))


def _solve_powers(l):
    pw = [l]
    for _ in range(int(math.log2(l.shape[-1])) - 1):
        pw.append(_bmm(pw[-1], pw[-1], *_NN, _SCAN_NF, _SCAN_NF))
    return pw


@jax.custom_vjp
def tri_solve(l, rhs):
    x = rhs
    for p in _solve_powers(l):
        x = x + _bmm(p, x, *_NN, _SCAN_NF, _SCAN_NF)
    return x


def _tri_solve_fwd(l, rhs):
    pw = _solve_powers(l)
    x = rhs
    for p in pw:
        x = x + _bmm(p, x, *_NN, _SCAN_NF, _SCAN_NF)
    return x, (pw, x)


def _tri_solve_bwd(res, g):
    pw, x = res
    y = g
    for p in pw:
        y = y + _bmm(p, y, *_TN, _SCAN_NB, _SCAN_NB)
    return _bmm(y, x, *_NT, _SCAN_NB, _SCAN_NB), y


tri_solve.defvjp(_tri_solve_fwd, _tri_solve_bwd)


@jax.custom_vjp
def known_solve(l, rhs, x):
    return x


known_solve.defvjp(lambda l, rhs, x: (x, (_solve_powers(l), x)),
                   lambda res, g: _tri_solve_bwd(res, g) + (jnp.zeros_like(g),))


def _rms(x, g):
    return x * lax.rsqrt(jnp.mean(x * x, axis=-1, keepdims=True) + NORM_EPS) * g


def _softplus(x):
    pos = x > 0
    return jnp.where(pos, x, 0.0) + jnp.log(1.0 + jnp.exp(-jnp.where(pos, x, -x)))


def _silu(z):
    return z * jax.nn.sigmoid(z)


_MM_VMEM_BYTES = 32 * 2**20


def _mm_tiles(m, n, k):
    best = None
    for tm in (2048, 1024, 512, 256):
        for tn_ in (2048, 1024, 512, 256):
            for d in range(k // LANE, 0, -1):
                tk = LANE * d
                if m % tm or n % tn_ or k % tk:
                    continue
                nk = k // tk
                vmem = 4 * tk * (tm + tn_) + 8 * tm * tn_ + (4 * tm * tn_ if nk > 1 else 0)
                if vmem > _MM_VMEM_BYTES:
                    continue
                a_reads = n // tn_ if nk > 1 else 1
                b_reads = 1 if (nk == 1 and n == tn_) else m // tm
                acc_rmw = nk * m * n if nk > 1 else 0
                cost = (a_reads * m * k + b_reads * k * n + acc_rmw, -tm * tn_ * tk)
                if best is None or cost < best[0]:
                    best = (cost, (tm, tn_, tk))
    return best[1]


class Side:
    def __init__(self, ins, outs, sems, start, finish):
        self.ins, self.outs, self.sems, self.start, self.finish = ins, outs, sems, start, finish

    def at_step(self, step, steps, *refs):
        @pl.when(step == 0)
        def _():
            self.start(*refs)

    def at_end(self, step, steps, *refs):
        @pl.when(step == steps - 1)
        def _():
            self.finish(*refs)

    def run(self, *refs):
        self.start(*refs)
        self.finish(*refs)


def matmul(name, a, b, mode, out_dtype=F32, side=None, after=()):
    if mode == "nn":
        (m, k), n = a.shape, b.shape[1]
    elif mode == "nt":
        (m, k), n = a.shape, b.shape[0]
    else:
        (k, m), n = a.shape, b.shape[1]
    tm, tn_, tk = _mm_tiles(m, n, k)
    nk = k // tk
    if mode == "nn":
        a_spec = pl.BlockSpec((tm, tk), lambda i, j, kk: (i, kk))
        b_spec = pl.BlockSpec((tk, tn_), lambda i, j, kk: (kk, j))
        ca, cb = 1, 0
    elif mode == "nt":
        a_spec = pl.BlockSpec((tm, tk), lambda i, j, kk: (i, kk))
        b_spec = pl.BlockSpec((tn_, tk), lambda i, j, kk: (j, kk))
        ca, cb = 1, 1
    else:
        a_spec = pl.BlockSpec((tk, tm), lambda i, j, kk: (kk, i))
        b_spec = pl.BlockSpec((tk, tn_), lambda i, j, kk: (kk, j))
        ca, cb = 0, 0

    grid = (m // tm, n // tn_, nk)
    n_in = len(side.ins) if side else 0
    n_out = len(side.outs) if side else 0
    n_dep = len(after)

    def body(a_ref, b_ref, *rest):
        rest = rest[n_dep:]
        s_ins, o_ref, s_outs = rest[:n_in], rest[n_in], rest[n_in + 1:n_in + 1 + n_out]
        scratch = rest[n_in + 1 + n_out:]
        acc, s_sems = (scratch[:1], scratch[1:]) if nk > 1 else ((), scratch)
        steps = grid[0] * grid[1] * grid[2]
        if side:
            step = (pl.program_id(0) * grid[1] + pl.program_id(1)) * grid[2] + pl.program_id(2)
            side.at_step(step, steps, s_ins, s_outs, s_sems)

        part = _dg(a_ref[...].astype(BF16), b_ref[...].astype(BF16), ca, cb)
        if nk == 1:
            o_ref[...] = part.astype(o_ref.dtype)
        else:
            acc_ref, kk = acc[0], pl.program_id(2)

            @pl.when(kk == 0)
            def _():
                acc_ref[...] = part

            @pl.when(kk > 0)
            def _():
                acc_ref[...] += part

            @pl.when(kk == nk - 1)
            def _():
                o_ref[...] = acc_ref[...].astype(o_ref.dtype)

        if side:
            side.at_end(step, steps, s_ins, s_outs, s_sems)

    res = pl.pallas_call(
        body, name=name, grid=grid,
        in_specs=[a_spec, b_spec] + [_ANY] * (n_dep + n_in),
        out_specs=[pl.BlockSpec((tm, tn_), lambda i, j, kk: (i, j))] + [_ANY] * n_out,
        out_shape=[jax.ShapeDtypeStruct((m, n), out_dtype)] + (list(side.outs) if side else []),
        scratch_shapes=([pltpu.VMEM((tm, tn_), F32)] if nk > 1 else []) + (list(side.sems) if side else []),
        compiler_params=pltpu.CompilerParams(
            dimension_semantics=("arbitrary",) * 3 if side else ("parallel", "parallel", "arbitrary"),
            vmem_limit_bytes=VMEM_BIG),
    )(a, b, *after, *(side.ins if side else []))
    return (res[0], res[1:]) if side else res[0]


def _rspec(tr, width, blk):
    return pl.BlockSpec((tr, width), lambda i: (i, blk))


def _full_spec(arr):
    return pl.BlockSpec(arr.shape, lambda i: (0,) * arr.ndim)


class Stage:
    def __init__(self, name, f, outs, tr, diff_rows, diff_params, drow_dtypes):
        self.name, self.f, self.outs, self.tr = name, f, outs, tr
        self.diff_rows, self.diff_params, self.drow_dtypes = diff_rows, diff_params, drow_dtypes

    def fwd(self, rows, params):
        f, nr, npar = self.f, len(rows), len(params)
        stored = [(w, dt) for (w, dt) in self.outs if dt is not None]
        keep = [i for i, (w, dt) in enumerate(self.outs) if dt is not None]

        def body(*refs):
            vals = f(*[r[...].astype(F32) for r in refs[:nr]], *[p[...] for p in refs[nr:nr + npar]])
            for o_ref, i in zip(refs[nr + npar:], keep):
                o_ref[...] = vals[i].astype(o_ref.dtype)

        return pl.pallas_call(
            body, name=self.name + "_fwd", grid=(T // self.tr,),
            in_specs=[_rspec(self.tr, w, b) for (_, w, b) in rows] + [_full_spec(p) for p in params],
            out_specs=[_rspec(self.tr, w, 0) for (w, _) in stored],
            out_shape=[jax.ShapeDtypeStruct((T, w), dt) for (w, dt) in stored],
            compiler_params=pltpu.CompilerParams(dimension_semantics=("arbitrary",), vmem_limit_bytes=VMEM_BIG),
        )(*[r[0] for r in rows], *params)

    def bwd(self, rows, params, cts):
        f, nr, npar = self.f, len(rows), len(params)
        dr_idx, dp_idx = self.diff_rows, self.diff_params
        flat_cts = [c for lst in cts for c in lst]
        nct = len(flat_cts)

        def body(*refs):
            row_refs, par_refs = refs[:nr], refs[nr:nr + npar]
            ct_refs = refs[nr + npar:nr + npar + nct]
            drow_refs = refs[nr + npar + nct:nr + npar + nct + len(dr_idx)]
            dpar_refs = refs[nr + npar + nct + len(dr_idx):]
            row_vals = [r[...].astype(F32) for r in row_refs]
            par_vals = [p[...] for p in par_refs]

            def g(*dv):
                rv, pv = list(row_vals), list(par_vals)
                for j, i in enumerate(dr_idx):
                    rv[i] = dv[j]
                for j, i in enumerate(dp_idx):
                    pv[i] = dv[len(dr_idx) + j]
                return f(*rv, *pv)

            _, vjp = jax.vjp(g, *[row_vals[i] for i in dr_idx], *[par_vals[i] for i in dp_idx])
            ct_vals, pos = [], 0
            for lst in cts:
                acc = ct_refs[pos][...].astype(F32)
                for q in range(1, len(lst)):
                    acc = acc + ct_refs[pos + q][...].astype(F32)
                pos += len(lst)
                ct_vals.append(acc)
            grads = vjp(tuple(ct_vals))
            for j, r in enumerate(drow_refs):
                r[...] = grads[j].astype(r.dtype)

            @pl.when(pl.program_id(0) == 0)
            def _():
                for r in dpar_refs:
                    r[...] = jnp.zeros_like(r)

            for j, r in enumerate(dpar_refs):
                r[...] += grads[len(dr_idx) + j].astype(F32)

        drow_shapes = [jax.ShapeDtypeStruct((T, rows[i][1]), dt) for i, dt in zip(dr_idx, self.drow_dtypes)]
        dpar_shapes = [jax.ShapeDtypeStruct(params[i].shape, F32) for i in dp_idx]
        res = pl.pallas_call(
            body, name=self.name + "_bwd", grid=(T // self.tr,),
            in_specs=[_rspec(self.tr, w, b) for (_, w, b) in rows] + [_full_spec(p) for p in params]
            + [_rspec(self.tr, w, b) for (_, w, b) in flat_cts],
            out_specs=[_rspec(self.tr, rows[i][1], 0) for i in dr_idx] + [_full_spec(params[i]) for i in dp_idx],
            out_shape=drow_shapes + dpar_shapes,
            compiler_params=pltpu.CompilerParams(dimension_semantics=("arbitrary",), vmem_limit_bytes=VMEM_BIG),
        )(*[r[0] for r in rows], *params, *[c[0] for c in flat_cts])
        return res[:len(dr_idx)], res[len(dr_idx):]


def f_pre(x, g):
    return _rms(x, g), x


@jax.custom_vjp
def swap32(t):
    width = t.shape[1]
    lane = lax.broadcasted_iota(jnp.int32, t.shape, 1) % LANE
    return jnp.where(lane < 32, pltpu.roll(t, width - 32, 1), jnp.where(lane < 64, pltpu.roll(t, 32, 1), 0.0))


swap32.defvjp(lambda t: (swap32(t), None), lambda _, g: (swap32(g),))


def f_mla(q_a, kv_a, kr, cq, sq, ck, sk, gq, gkv, wq, wkv):
    q = bdot(_rms(q_a, gq), wq)
    kv = bdot(_rms(kv_a, gkv), wkv)
    t, k = q[:, 1024:], kr[:, :LANE]
    return (q[:, :1024], t * cq + swap32(t) * sq, kv[:, :1024], k * ck + swap32(k) * sk, kv[:, 1024:])


def f_rwkv_pre(lerp, w0f, w0b, a0f, a0b, kkw, kaw, w2f, w2b, a2f, a2b, g2):
    r, k, v = lerp[:, :RW], lerp[:, RW:2 * RW], lerp[:, 2 * RW:3 * RW]
    wdf, wdb, adf, adb = (lerp[:, 3 * RW + i * LANE:3 * RW + (i + 1) * LANE] for i in range(4))

    def logdecay(w0, wd, w2):
        z = w0 + bdot(jnp.tanh(wd), w2)
        return -jnp.exp(-_softplus(-z) - 0.5)

    a_f = jax.nn.sigmoid(a0f + bdot(adf, a2f))
    a_b = jax.nn.sigmoid(a0b + bdot(adb, a2b))
    kk = k * kkw
    kk = kk / jnp.maximum(jnp.sqrt(headsum(kk * kk, g2)), 1e-12)
    return (r, v, logdecay(w0f, wdf, w2f), logdecay(w0b, wdb, w2b),
            k * (1.0 + (a_f - 1.0) * kaw), k * (1.0 + (a_b - 1.0) * kaw), -kk, kk * a_f, kk * a_b)


def f_rwkv_post(yf, yb, r, kf, kb, v, z, gng, gnb, rk, g2):
    y = yf + yb
    mu = headsum(y, g2) * (1.0 / RN)
    d = y - mu
    var = headsum(d * d, g2) * (1.0 / RN)
    yn = d * lax.rsqrt(var + GN_EPS) * gng + gnb
    bonus = headsum(r * (kf + kb) * rk, g2) * v
    return ((yn + bonus) * _silu(z),)


def f_gate(y, z):
    return (y * _silu(z),)


def f_merge(um, ur, gm, gr):
    return (jax.nn.sigmoid(gm) * um + jax.nn.sigmoid(gr) * ur,)


_SHIFT_W = 256


def _lerp_colblock(j):
    return jnp.where(j < 3072 // _SHIFT_W, OFF_RKV // _SHIFT_W + j, OFF_LORA // _SHIFT_W + j - 3072 // _SHIFT_W)


def _nbr_mean(x):
    row = lax.broadcasted_iota(jnp.int32, x.shape, 0)
    up = jnp.where(row == 0, 0.0, pltpu.roll(x, 1, 0))
    dn = jnp.where(row == T - 1, 0.0, pltpu.roll(x, T - 1, 0))
    return 0.5 * (up + dn)


def shift_fwd(proj, mu):
    def body(x_ref, mu_ref, o_ref):
        x = x_ref[...]
        o_ref[...] = x + mu_ref[...] * (_nbr_mean(x) - x)

    return pl.pallas_call(
        body, name="shift_fwd", grid=(NLERP // _SHIFT_W,),
        in_specs=[pl.BlockSpec((T, _SHIFT_W), lambda j: (0, _lerp_colblock(j))),
                  pl.BlockSpec((1, _SHIFT_W), lambda j: (0, j))],
        out_specs=pl.BlockSpec((T, _SHIFT_W), lambda j: (0, j)),
        out_shape=jax.ShapeDtypeStruct((T, NLERP), F32),
        compiler_params=pltpu.CompilerParams(dimension_semantics=("parallel",), vmem_limit_bytes=VMEM_BIG),
    )(proj, mu)


def shift_bwd(proj, mu, g):
    def body(x_ref, mu_ref, g_ref, dx_ref, dmu_ref):
        x, gv = x_ref[...], g_ref[...]
        dmu_ref[...] = jnp.sum(gv * (_nbr_mean(x) - x), axis=0, keepdims=True)
        gm = gv * mu_ref[...]
        dx_ref[...] = (gv - gm + _nbr_mean(gm)).astype(dx_ref.dtype)

    col = pl.BlockSpec((T, _SHIFT_W), lambda j: (0, j))
    vec = pl.BlockSpec((1, _SHIFT_W), lambda j: (0, j))
    return pl.pallas_call(
        body, name="shift_bwd", grid=(NLERP // _SHIFT_W,),
        in_specs=[pl.BlockSpec((T, _SHIFT_W), lambda j: (0, _lerp_colblock(j))), vec, col],
        out_specs=[col, vec],
        out_shape=[jax.ShapeDtypeStruct((T, NLERP), BF16), jax.ShapeDtypeStruct((1, NLERP), F32)],
        compiler_params=pltpu.CompilerParams(dimension_semantics=("parallel",), vmem_limit_bytes=VMEM_BIG),
    )(proj, mu, g)


_TQ_F, _TQ_B = 256, 512
_ATT_SCALE = (NOPE + ROPE) ** -0.5


def _probs(q, k, lse=None):
    s = _dg(q, k, 1, 1) * _ATT_SCALE
    if lse is not None:
        return jnp.exp(s - lse), lse
    m = jnp.max(s, axis=-1, keepdims=True)
    e = jnp.exp(s - m)
    l = jnp.sum(e, axis=-1, keepdims=True)
    return e * (1.0 / l), m + jnp.log(l)


def _q_blk(tq):
    return pl.BlockSpec((tq, LANE), lambda h, i: (i, h))


_K_BLK = pl.BlockSpec((T, LANE), lambda h, i: (0, h))
_KR_BLK = pl.BlockSpec((T, LANE), lambda h, i: (0, 0))


def _load_qk(qn_ref, qr_ref, kn_ref, kr_ref, kcat_ref):
    @pl.when(pl.program_id(1) == 0)
    def _():
        kcat_ref[:, :LANE] = kn_ref[...]
        kcat_ref[:, LANE:] = kr_ref[...]

    return jnp.concatenate([qn_ref[...], qr_ref[...]], axis=1), kcat_ref[...]


def attn_fwd(qn, qr, kn, kr, v):
    def body(qn_ref, qr_ref, kn_ref, kr_ref, v_ref, o_ref, lse_ref, kcat_ref):
        q, k = _load_qk(qn_ref, qr_ref, kn_ref, kr_ref, kcat_ref)
        p, lse = _probs(q, k)
        o_ref[...] = _dg(p.astype(BF16), v_ref[...], 1, 0)
        lse_ref[...] = jnp.broadcast_to(lse, lse_ref.shape)

    return pl.pallas_call(
        body, name="attn_fwd", grid=(HEADS, T // _TQ_F),
        in_specs=[_q_blk(_TQ_F), _q_blk(_TQ_F), _K_BLK, _KR_BLK, _K_BLK], out_specs=[_q_blk(_TQ_F)] * 2,
        out_shape=[jax.ShapeDtypeStruct((T, HEADS * VDIM), F32)] * 2,
        scratch_shapes=[pltpu.VMEM((T, 2 * LANE), BF16)],
        compiler_params=pltpu.CompilerParams(dimension_semantics=("arbitrary", "arbitrary"), vmem_limit_bytes=VMEM_BIG),
    )(qn, qr, kn, kr, v)


def attn_bwd(qn, qr, kn, kr, v, lse, do):
    def body(qn_ref, qr_ref, kn_ref, kr_ref, v_ref, lse_ref, do_ref, dqn_ref, dqr_ref, dkn_ref, dkr_ref, dv_ref,
             kcat_ref):
        h, i = pl.program_id(0), pl.program_id(1)

        @pl.when(i == 0)
        def _():
            dkn_ref[...] = jnp.zeros_like(dkn_ref)
            dv_ref[...] = jnp.zeros_like(dv_ref)

        @pl.when((i == 0) & (h == 0))
        def _():
            dkr_ref[...] = jnp.zeros_like(dkr_ref)

        q, k = _load_qk(qn_ref, qr_ref, kn_ref, kr_ref, kcat_ref)
        dob = do_ref[...].astype(BF16)
        p, _ = _probs(q, k, lse_ref[:, :1])
        dv_ref[...] += _dg(p.astype(BF16), dob, 0, 0)
        dp = _dg(dob, v_ref[...], 1, 1)
        ds = (p * (dp - jnp.sum(dp * p, axis=-1, keepdims=True)) * _ATT_SCALE).astype(BF16)
        dq = _dg(ds, k, 1, 0)
        dqn_ref[...] = dq[:, :LANE]
        dqr_ref[...] = dq[:, LANE:]
        dk = _dg(ds, q, 0, 0)
        dkn_ref[...] += dk[:, :LANE]
        dkr_ref[...] += dk[:, LANE:]

    wide = jax.ShapeDtypeStruct((T, HEADS * LANE), F32)
    return pl.pallas_call(
        body, name="attn_bwd", grid=(HEADS, T // _TQ_B),
        in_specs=[_q_blk(_TQ_B), _q_blk(_TQ_B), _K_BLK, _KR_BLK, _K_BLK, _q_blk(_TQ_B), _q_blk(_TQ_B)],
        out_specs=[_q_blk(_TQ_B), _q_blk(_TQ_B), _K_BLK, _KR_BLK, _K_BLK],
        out_shape=[wide, wide, wide, jax.ShapeDtypeStruct((T, LANE), F32), wide],
        scratch_shapes=[pltpu.VMEM((T, 2 * LANE), BF16)],
        compiler_params=pltpu.CompilerParams(dimension_semantics=("arbitrary", "arbitrary"), vmem_limit_bytes=VMEM_BIG),
    )(qn, qr, kn, kr, v, lse, do)


def _chunk(r, lw, k, v, a, b, ht, u_kept=None, *, reverse):
    hb, c, _ = r.shape
    ti = lax.broadcasted_iota(jnp.int32, (c, c), 0)
    si = lax.broadcasted_iota(jnp.int32, (c, c), 1)
    incl = (si >= ti) if reverse else (si <= ti)
    strict = (si > ti) if reverse else (si < ti)
    ones = jnp.broadcast_to(incl.astype(F32)[None], (hb, c, c))
    cum = cumdot(ones, lw)
    cum_ex = cum - lw
    tot = jnp.sum(lw, axis=1, keepdims=True)
    mid = 0.5 * tot
    rt, at = r * jnp.exp(cum - mid), a * jnp.exp(cum_ex - mid)
    einv = jnp.exp(mid - cum)
    bk = jnp.concatenate([b * einv, k * einv], axis=1)
    m_a, m_r = nt(at, bk), nt(rt, bk)
    m_ab = jnp.where(strict, m_a[:, :, :c], 0.0)
    m_ak = jnp.where(strict, m_a[:, :, c:], 0.0)
    t2 = lax.broadcasted_iota(jnp.int32, (c, 2 * c), 0)
    s2 = lax.broadcasted_iota(jnp.int32, (c, 2 * c), 1)
    s2 = jnp.where(s2 >= c, s2 - c, s2)
    m_r = jnp.where((s2 >= t2) if reverse else (s2 <= t2), m_r, 0.0)
    rhs = nt_state(a * jnp.exp(cum_ex), ht) + nn(m_ak, v)
    u = tri_solve(m_ab, rhs) if u_kept is None else known_solve(m_ab, rhs, u_kept)
    uv = jnp.concatenate([u, v], axis=1)
    y = nt_state(r * jnp.exp(cum), ht) + nn(m_r, uv)
    eend = jnp.exp(tot - cum)
    ht_new = ht * jnp.exp(tot) + tn(uv, jnp.concatenate([b * eend, k * eend], axis=1))
    return y, ht_new, u


_HB_F, _HB_B = 16, 16


def _split_heads(x):
    return jnp.stack([x[:, i * RN:(i + 1) * RN] for i in range(x.shape[1] // RN)])


def _merge_heads(y):
    return jnp.concatenate([y[i] for i in range(y.shape[0])], axis=1)


def _chunk_map(reverse, backward):
    flip = reverse != backward
    return (lambda g, c: (NCH - 1 - c, g)) if flip else (lambda g, c: (c, g))


def scan_fwd(name, r, lw, k, v, a, b, reverse):
    hb = _HB_F
    cmap = _chunk_map(reverse, False)

    def body(r_ref, lw_ref, k_ref, v_ref, a_ref, b_ref, y_ref, h0_ref, u_ref, ht_ref):
        @pl.when(pl.program_id(1) == 0)
        def _():
            ht_ref[...] = jnp.zeros_like(ht_ref)

        ht = ht_ref[...]
        h0_ref[0] = ht
        ins = [_split_heads(x[...]) for x in (r_ref, lw_ref, k_ref, v_ref, a_ref, b_ref)]
        y, hn, u = _chunk(*ins, ht, reverse=reverse)
        y_ref[...] = _merge_heads(y)
        u_ref[...] = _merge_heads(u)
        ht_ref[...] = hn

    io = pl.BlockSpec((CHUNK, hb * RN), cmap)
    return pl.pallas_call(
        body, name=name, grid=(RH // hb, NCH),
        in_specs=[io] * 6,
        out_specs=[io, pl.BlockSpec((1, hb, RN, RN), lambda g, c: (cmap(g, c)[0], g, 0, 0)), io],
        out_shape=[jax.ShapeDtypeStruct((T, RW), F32), jax.ShapeDtypeStruct((NCH, RH, RN, RN), F32),
                   jax.ShapeDtypeStruct((T, RW), F32)],
        scratch_shapes=[pltpu.VMEM((hb, RN, RN), F32)],
        compiler_params=pltpu.CompilerParams(dimension_semantics=("parallel", "arbitrary"), vmem_limit_bytes=VMEM_BIG),
    )(r, lw, k, v, a, b)


def scan_bwd(name, r, lw, k, v, a, b, h0, u, dy, reverse):
    hb = _HB_B
    cmap = _chunk_map(reverse, True)

    def body(r_ref, lw_ref, k_ref, v_ref, a_ref, b_ref, h0_ref, u_ref, dy_ref, *rest):
        d_refs, dht_ref = rest[:6], rest[6]

        @pl.when(pl.program_id(1) == 0)
        def _():
            dht_ref[...] = jnp.zeros_like(dht_ref)

        ins = [_split_heads(x[...]) for x in (r_ref, lw_ref, k_ref, v_ref, a_ref, b_ref)]
        _, vjp = jax.vjp(functools.partial(_chunk, reverse=reverse), *ins, h0_ref[0], _split_heads(u_ref[...]))
        dy = _split_heads(dy_ref[...])
        grads = vjp((dy, dht_ref[...], jnp.zeros_like(dy)))
        for d_ref, gval in zip(d_refs, grads[:6]):
            d_ref[...] = _merge_heads(gval).astype(d_ref.dtype)
        dht_ref[...] = grads[6]

    io = pl.BlockSpec((CHUNK, hb * RN), cmap)
    return pl.pallas_call(
        body, name=name, grid=(RH // hb, NCH),
        in_specs=[io] * 6 + [pl.BlockSpec((1, hb, RN, RN), lambda g, c: (cmap(g, c)[0], g, 0, 0)), io, io],
        out_specs=[io] * 6,
        out_shape=[jax.ShapeDtypeStruct((T, RW), F32 if i == 1 else BF16) for i in range(6)],
        scratch_shapes=[pltpu.VMEM((hb, RN, RN), F32)],
        compiler_params=pltpu.CompilerParams(dimension_semantics=("parallel", "arbitrary"), vmem_limit_bytes=VMEM_BIG),
    )(r, lw, k, v, a, b, h0, u, dy)


def loss_stage(out, x2, tgt, g_post):
    tr = 256

    def body(o_ref, x_ref, t_ref, g_ref, do_ref, dy_ref, dg_ref, loss_ref):
        @pl.when(pl.program_id(0) == 0)
        def _():
            dg_ref[...] = jnp.zeros_like(dg_ref)
            loss_ref[...] = jnp.zeros_like(loss_ref)

        nrm, vjp = jax.vjp(_rms, o_ref[...], g_ref[...])
        e = x_ref[...] + nrm - t_ref[...]
        s = jnp.sum(jnp.sum(e * e, axis=1, keepdims=True), axis=0, keepdims=True)
        loss_ref[...] += jnp.broadcast_to(s * (0.5 / D), loss_ref.shape)
        dy = e * (1.0 / D)
        do, dg = vjp(dy)
        do_ref[...] = do.astype(do_ref.dtype)
        dy_ref[...] = dy
        dg_ref[...] += dg

    row = pl.BlockSpec((tr, D), lambda i: (i, 0))
    return pl.pallas_call(
        body, name="loss_stage", grid=(T // tr,),
        in_specs=[row, row, row, pl.BlockSpec((1, D), lambda i: (0, 0))],
        out_specs=[row, row, pl.BlockSpec((1, D), lambda i: (0, 0)), pl.BlockSpec((8, LANE), lambda i: (0, 0))],
        out_shape=[jax.ShapeDtypeStruct((T, D), BF16), jax.ShapeDtypeStruct((T, D), F32),
                   jax.ShapeDtypeStruct((1, D), F32), jax.ShapeDtypeStruct((8, LANE), F32)],
        compiler_params=pltpu.CompilerParams(dimension_semantics=("arbitrary",), vmem_limit_bytes=VMEM_BIG),
    )(out, x2, tgt, g_post)


_EW_BLOCK_BYTES = 1 << 20


def _row_tile(rows, cols):
    best = None
    for tr in range(16, rows + 1, 16):
        if rows % tr == 0 and tr * cols * 4 <= _EW_BLOCK_BYTES:
            best = tr
    return best or rows


def _axis_tile(shape, axis, words):
    rows, cols = shape
    n, other, unit = (rows, cols, 16) if axis == 0 else (cols, rows, LANE)
    best = unit if n % unit == 0 else n
    for t in range(unit, n + 1, unit):
        if n % t == 0 and t * other * words * 4 <= _EW_BLOCK_BYTES:
            best = t
    blk = (best, cols) if axis == 0 else (rows, best)
    at = (lambda s: (s, 0)) if axis == 0 else (lambda s: (0, s))
    return blk, n // best, at


def _adamw_update(g, w_ref, m_ref, v_ref, g_ref, d_ref, nm_ref, nv_ref):
    mm = ADAM_B1 * m_ref[...] + (1.0 - ADAM_B1) * g
    vv = ADAM_B2 * v_ref[...] + (1.0 - ADAM_B2) * (g * g)
    m_hat = mm / (1.0 - ADAM_B1 ** ADAM_STEP)
    v_hat = vv / (1.0 - ADAM_B2 ** ADAM_STEP)
    g_ref[...] = g
    d_ref[...] = -ADAM_LR * (m_hat / (jnp.sqrt(v_hat) + ADAM_EPS) + ADAM_WD * w_ref[...])
    nm_ref[...] = mm
    nv_ref[...] = vv


def adamw(name, w, m, v, parts):
    rows, cols = w.shape
    br = _row_tile(rows, cols)
    npart = len(parts)

    def body(w_ref, m_ref, v_ref, *rest):
        g = rest[0][...].astype(F32)
        for p in rest[1:npart]:
            g = g + p[...].astype(F32)
        _adamw_update(g, w_ref, m_ref, v_ref, *rest[npart:])

    blk = pl.BlockSpec((br, cols), lambda i: (i, 0))
    return pl.pallas_call(
        body, name=name, grid=(rows // br,),
        in_specs=[blk] * (3 + npart), out_specs=[blk] * 4,
        out_shape=[jax.ShapeDtypeStruct((rows, cols), F32)] * 4,
        compiler_params=pltpu.CompilerParams(dimension_semantics=("parallel",), vmem_limit_bytes=VMEM_BIG),
    )(w, m, v, *parts)


def adamw_halves(name, place, w, m, v, mine, theirs, axis):
    half_shape = mine.shape
    blk_shape, nb, at = _axis_tile(half_shape, axis, 1)

    def body(p_ref, w_ref, m_ref, v_ref, a_ref, b_ref, *outs):
        own = (pl.program_id(0) // nb) == p_ref[0]
        _adamw_update(jnp.where(own, a_ref[...], b_ref[...]), w_ref, m_ref, v_ref, *outs)

    blk = pl.BlockSpec(blk_shape, lambda i, p: at(i))
    half = pl.BlockSpec(blk_shape, lambda i, p: at(i % nb))
    return pl.pallas_call(
        body, name=name,
        grid_spec=pltpu.PrefetchScalarGridSpec(num_scalar_prefetch=1, grid=(2 * nb,),
                                               in_specs=[blk] * 3 + [half] * 2, out_specs=[blk] * 4),
        out_shape=[jax.ShapeDtypeStruct(w.shape, F32)] * 4,
        compiler_params=pltpu.CompilerParams(dimension_semantics=("arbitrary",), vmem_limit_bytes=VMEM_BIG),
    )(place, w, m, v, mine, theirs)


def pair_sum(name, place, send, other, axis):
    blk_shape, nb, at = _axis_tile(other.shape[1:], axis, 4)

    def body(p_ref, a_ref, b_ref, o_ref):
        o_ref[...] = (a_ref[...].astype(F32) + b_ref[...].astype(F32)).astype(o_ref.dtype)

    blk = pl.BlockSpec((4,) + blk_shape, lambda i, p: (0,) + at(i))
    mine = pl.BlockSpec((4,) + blk_shape, lambda i, p: (0,) + at(p[0] * nb + i))
    return pl.pallas_call(
        body, name=name,
        grid_spec=pltpu.PrefetchScalarGridSpec(num_scalar_prefetch=1, grid=(nb,), in_specs=[mine, blk], out_specs=blk),
        out_shape=jax.ShapeDtypeStruct(other.shape, BF16),
        compiler_params=pltpu.CompilerParams(dimension_semantics=("arbitrary",), vmem_limit_bytes=VMEM_BIG),
    )(place, send, other)


def sum4(name, place, recv, own, axis):
    blk_shape, nb, at = _axis_tile(recv.shape[1:], axis, 4)

    def body(p_ref, r_ref, s_ref, o_ref):
        me = p_ref[0]
        t = [jnp.where(me == j, s_ref[j], r_ref[j]).astype(F32) for j in range(4)]
        o_ref[...] = ((t[0] + t[1]) + t[2]) + t[3]

    blk = pl.BlockSpec((4,) + blk_shape, lambda i, p: (0,) + at(i))
    return pl.pallas_call(
        body, name=name,
        grid_spec=pltpu.PrefetchScalarGridSpec(num_scalar_prefetch=1, grid=(nb,), in_specs=[blk, blk],
                                               out_specs=pl.BlockSpec(blk_shape, lambda i, p: at(i))),
        out_shape=jax.ShapeDtypeStruct(recv.shape[1:], F32),
        compiler_params=pltpu.CompilerParams(dimension_semantics=("arbitrary",), vmem_limit_bytes=VMEM_BIG),
    )(place, recv, own)


_ANY = pl.BlockSpec(memory_space=pl.ANY)


def _place():
    x, y, c = lax.axis_index("x"), lax.axis_index("y"), lax.axis_index("c")
    return x, y, c, 2 * x + y


def _chip_peers(x, y):
    out = []
    for k in (1, 2, 3):
        px = 1 - x if k & 2 else x
        py = 1 - y if k & 1 else y
        out.append((k, px, py, 2 * px + py))
    return out


def _half(c, shape, axis):
    n = shape[axis] // 2
    sl = pl.ds(pl.multiple_of(c * n, 16 if axis == 0 else LANE), n)
    return (sl,) if axis == 0 else (pl.ds(0, shape[0]), sl)


def gather_weights(srcs, axes):
    side = gather_side(srcs, axes)
    n = len(srcs)

    def body(*refs):
        ins, outs, sems = refs[:n], refs[n:2 * n], refs[2 * n:]
        side.run(ins, outs, sems)

    return pl.pallas_call(
        body, name="gather_weights", in_specs=[_ANY] * n, out_specs=[_ANY] * n,
        out_shape=side.outs, scratch_shapes=side.sems,
    )(*srcs)


def gather_side(srcs, axes):
    n = len(srcs)

    def copies(src, dst, sems, want):
        ssem, rsem, fssem, frsem = sems
        x, y, c, me = _place()
        sib = (x, y, 1 - c)
        out = []
        for i in range(n):
            mine, other = _half(c, srcs[i].shape, axes[i]), _half(1 - c, srcs[i].shape, axes[i])
            for k, px, py, peer in _chip_peers(x, y):
                sems_k = dict(send_sem=ssem.at[i, k - 1], recv_sem=rsem.at[i, k - 1], device_id=(px, py, c),
                              device_id_type=MESH_IDS)
                fsems = dict(send_sem=fssem.at[i, k - 1], recv_sem=frsem.at[i, k - 1], device_id=sib,
                             device_id_type=MESH_IDS)
                got = dst[i].at[(peer,) + mine]
                mk = pltpu.make_async_remote_copy
                made = dict(
                    snd=lambda: mk(src_ref=src[i].at[mine], dst_ref=dst[i].at[(me,) + mine], **sems_k),
                    rcv=lambda: mk(src_ref=src[i].at[mine], dst_ref=got, **sems_k),
                    fwd=lambda: mk(src_ref=got, dst_ref=got, **fsems),
                    frcv=lambda: mk(src_ref=got, dst_ref=dst[i].at[(peer,) + other], **fsems))
                out.append([made[w]() for w in want])
        return out

    def start(src, dst, sems):
        for (snd,) in copies(src, dst, sems, ("snd",)):
            snd.start()

    def finish(src, dst, sems):
        cps = copies(src, dst, sems, ("snd", "rcv", "fwd", "frcv"))
        for _, rcv, fwd, _ in cps:
            rcv.wait_recv()
            fwd.start()
        for snd, _, fwd, frcv in cps:
            frcv.wait_recv()
            snd.wait_send()
            fwd.wait_send()

    return Side(list(srcs), [jax.ShapeDtypeStruct((4,) + s.shape, s.dtype) for s in srcs],
                [pltpu.SemaphoreType.DMA((n, 3))] * 4, start, finish)


def pair_exchange(name, srcs, axes):
    n = len(srcs)

    def half_shape(s, axis):
        return (4, s.shape[1] // 2, s.shape[2]) if axis == 0 else (4, s.shape[1], s.shape[2] // 2)

    def body(*refs):
        src, other = refs[:n], refs[n:2 * n]
        ssem, rsem = refs[2 * n:]
        x, y, c, _ = _place()
        cps = []
        for i in range(n):
            idx = (pl.ds(0, 4),) + _half(1 - c, srcs[i].shape[1:], axes[i])
            cps.append(pltpu.make_async_remote_copy(
                src_ref=src[i].at[idx], dst_ref=other[i], send_sem=ssem.at[i], recv_sem=rsem.at[i],
                device_id=(x, y, 1 - c), device_id_type=MESH_IDS))
            cps[-1].start()
        for cp in cps:
            cp.wait()

    return pl.pallas_call(
        body, name=name, in_specs=[_ANY] * n, out_specs=[_ANY] * n,
        out_shape=[jax.ShapeDtypeStruct(half_shape(s, a), s.dtype) for s, a in zip(srcs, axes)],
        scratch_shapes=[pltpu.SemaphoreType.DMA((n,))] * 2,
    )(*srcs)


def scatter_side(srcs):
    n = len(srcs)

    def copies(src, dst, sems, sends_only=False):
        ssem, rsem = sems
        x, y, c, me = _place()
        out = []
        for i in range(n):
            for k, px, py, peer in _chip_peers(x, y):
                sems_k = dict(send_sem=ssem.at[i, k - 1], recv_sem=rsem.at[i, k - 1], device_id=(px, py, c),
                              device_id_type=MESH_IDS)
                snd = pltpu.make_async_remote_copy(src_ref=src[i].at[peer], dst_ref=dst[i].at[me], **sems_k)
                if sends_only:
                    out.append(snd)
                    continue
                out.append((snd, pltpu.make_async_remote_copy(src_ref=src[i].at[peer], dst_ref=dst[i].at[peer],
                                                              **sems_k)))
        return out

    def start(src, dst, sems):
        for snd in copies(src, dst, sems, sends_only=True):
            snd.start()

    def finish(src, dst, sems):
        for snd, rcv in copies(src, dst, sems):
            rcv.wait_recv()
            snd.wait_send()

    return Side(list(srcs), [jax.ShapeDtypeStruct(s.shape, s.dtype) for s in srcs],
                [pltpu.SemaphoreType.DMA((n, 3))] * 2, start, finish)


_HBM = pl.BlockSpec(memory_space=pltpu.HBM)
_SEM = pl.BlockSpec(memory_space=pltpu.SEMAPHORE)
_DATAFLOW = pltpu.SideEffectType.DATAFLOW_SIDE_EFFECTING


def scatter_start(name, srcs):
    n = len(srcs)
    side = scatter_side(srcs)
    ns = 3 * n

    def body(*refs):
        src, land = refs[:n], refs[n:2 * n]
        sems = refs[2 * n:2 * n + 2 * ns]
        side.start(src, land, (_SemGrid(sems[:ns]), _SemGrid(sems[ns:])))
        refs[-1][...] = jnp.zeros_like(refs[-1])

    hbm = [pltpu.HBM(s.shape, s.dtype) for s in srcs]
    res = pl.pallas_call(
        body, name=name,
        out_shape=[pltpu.SemaphoreType.DMA(())] * (2 * ns) + hbm + hbm + [jax.ShapeDtypeStruct((8, LANE), F32)],
        in_specs=[_HBM] * (2 * n),
        out_specs=[_SEM] * (2 * ns) + [_HBM] * (2 * n) + [pl.BlockSpec(memory_space=pltpu.VMEM)],
        input_output_aliases={i: 2 * ns + i for i in range(2 * n)},
        compiler_params=pltpu.CompilerParams(has_side_effects=_DATAFLOW),
    )(*[pltpu.with_memory_space_constraint(s, pltpu.HBM) for s in srcs],
      *[pltpu.with_memory_space_constraint(lax.empty(s.shape, s.dtype), pltpu.HBM) for s in srcs])
    return res[:2 * ns], res[2 * ns:2 * ns + n], res[2 * ns + n:2 * ns + 2 * n], res[-1]


def scatter_wait(name, sems, srcs, lands, after):
    n = len(srcs)
    side = scatter_side(srcs)
    ns = 3 * n

    def body(*refs):
        src, land = refs[:n], refs[n:2 * n]
        s = refs[2 * n:2 * n + 2 * ns]
        side.finish(src, land, (_SemGrid(s[:ns]), _SemGrid(s[ns:])))

    hbm = [pltpu.HBM(s.shape, s.dtype) for s in srcs]
    res = pl.pallas_call(
        body, name=name, out_shape=hbm + hbm,
        in_specs=[_HBM] * (2 * n) + [_SEM] * (2 * ns) + [_ANY], out_specs=[_HBM] * (2 * n),
        input_output_aliases={i: i for i in range(2 * n)},
        compiler_params=pltpu.CompilerParams(has_side_effects=_DATAFLOW),
    )(*srcs, *lands, *sems, after)
    return res[:n], res[n:]


class _SemGrid:
    def __init__(self, sems):
        self.sems = sems

    @property
    def at(self):
        return self

    def __getitem__(self, ik):
        return self.sems[3 * ik[0] + ik[1]]


def swap_halves(name, srcs):
    n = len(srcs)

    def body(*refs):
        src, dst = refs[:n], refs[n:2 * n]
        ssem, rsem = refs[2 * n:]
        x, y, c, _ = _place()
        cps = []
        for i in range(n):
            cps.append(pltpu.make_async_remote_copy(src_ref=src[i], dst_ref=dst[i], send_sem=ssem.at[i],
                                                    recv_sem=rsem.at[i], device_id=(x, y, 1 - c),
                                                    device_id_type=MESH_IDS))
            cps[-1].start()
        for cp in cps:
            cp.wait()

    return pl.pallas_call(
        body, name=name, in_specs=[_ANY] * n, out_specs=[_ANY] * n,
        out_shape=[jax.ShapeDtypeStruct(s.shape, s.dtype) for s in srcs],
        scratch_shapes=[pltpu.SemaphoreType.DMA((n,))] * 2,
    )(*srcs)


def _ag8_copies(src, dst, sems, sends_only=False):
    x, y, c = lax.axis_index("x"), lax.axis_index("y"), lax.axis_index("c")
    me = 4 * x + 2 * y + c
    out = []
    for k in range(1, 8):
        px = 1 - x if k & 4 else x
        py = 1 - y if k & 2 else y
        pc = 1 - c if k & 1 else c
        peer = 4 * px + 2 * py + pc
        out.append(tuple(pltpu.make_async_remote_copy(
            src_ref=src, dst_ref=dst.at[slot], send_sem=sems[k - 1], recv_sem=sems[7 + k - 1],
            device_id=(px, py, pc), device_id_type=MESH_IDS) for slot in ((me,) if sends_only else (me, peer))))
    return out


def allgather8_start(name, src):
    def body(src_ref, land_ref, *rest):
        for (snd,) in _ag8_copies(src_ref, land_ref, rest[:14], sends_only=True):
            snd.start()

    land = jax.ShapeDtypeStruct((8,) + src.shape, src.dtype)
    res = pl.pallas_call(
        body, name=name,
        out_shape=[pltpu.SemaphoreType.DMA(())] * 14 + [pltpu.HBM(src.shape, src.dtype), pltpu.HBM(land.shape, land.dtype)],
        in_specs=[_HBM, _HBM], out_specs=[_SEM] * 14 + [_HBM, _HBM],
        input_output_aliases={0: 14, 1: 15},
        compiler_params=pltpu.CompilerParams(has_side_effects=_DATAFLOW),
    )(pltpu.with_memory_space_constraint(src, pltpu.HBM),
      pltpu.with_memory_space_constraint(lax.empty(land.shape, land.dtype), pltpu.HBM))
    return res[:14], res[14], res[15]


def allgather8_wait(name, sems, src, land, after):
    def body(src_ref, land_ref, *rest):
        for snd, rcv in _ag8_copies(src_ref, land_ref, rest[:14]):
            rcv.wait_recv()
            snd.wait_send()

    return pl.pallas_call(
        body, name=name, out_shape=[pltpu.HBM(src.shape, src.dtype), pltpu.HBM(land.shape, land.dtype)],
        in_specs=[_HBM, _HBM] + [_SEM] * 14 + [_ANY], out_specs=[_HBM, _HBM],
        input_output_aliases={0: 0, 1: 1},
        compiler_params=pltpu.CompilerParams(has_side_effects=_DATAFLOW),
    )(src, land, *sems, after)


WEIGHTS = ['g_pre', 'w_in', 'mla_q_norm', 'mla_wq_b', 'mla_kv_norm', 'mla_wkv_b', 'rwkv_mu', 'rwkv_w0_f', 'rwkv_w2_f',
           'rwkv_w0_b', 'rwkv_w2_b', 'rwkv_a0_f', 'rwkv_a2_f', 'rwkv_a0_b', 'rwkv_a2_b', 'rwkv_k_k', 'rwkv_k_a',
           'rwkv_r_k', 'rwkv_gn_g', 'rwkv_gn_b', 'w_br_mla', 'w_br_rwkv', 'w_out', 'g_post']
BIG_SHAPES = {'w_in': (D_IN // 4, D), 'mla_wq_b': (Q_RANK, 384), 'mla_wkv_b': (KV_RANK, 512),
              'rwkv_w2_f': (LORA, 256), 'rwkv_w2_b': (LORA, 256), 'rwkv_a2_f': (LORA, 256), 'rwkv_a2_b': (LORA, 256),
              'w_br_mla': (RW, 512), 'w_br_rwkv': (RW, 512), 'w_out': (512, D)}
BIG = list(BIG_SHAPES)
SMALL = [n for n in WEIGHTS if n not in BIG_SHAPES]
SMALL_SHAPES = {'g_pre': (D,), 'mla_q_norm': (Q_RANK,), 'mla_kv_norm': (KV_RANK,), 'rwkv_mu': (3456,),
                'rwkv_w0_f': (RW,), 'rwkv_w0_b': (RW,), 'rwkv_a0_f': (RW,), 'rwkv_a0_b': (RW,), 'rwkv_k_k': (RW,),
                'rwkv_k_a': (RW,), 'rwkv_r_k': (RH, RN), 'rwkv_gn_g': (RW,), 'rwkv_gn_b': (RW,), 'g_post': (D,)}
SMALL_LEN = sum(int(np.prod(s)) for s in SMALL_SHAPES.values())
SMALL_ROWS = 144


UNITS = [('w_in',), ('mla_wq_b',), ('mla_wkv_b',), ('rwkv_w2_f', 'rwkv_w2_b', 'rwkv_a2_f', 'rwkv_a2_b'),
         ('w_br_mla', 'w_br_rwkv'), ('w_out',)]
UNIT_AXIS = [1, 0, 0, 0, 0, 0]
ROW_SHARDED = ('w_in', 'w_out')


def _unit_cat(parts):
    return parts[0] if len(parts) == 1 else jnp.concatenate(parts, axis=0)


def _unit_split(arr, names, axis):
    out, o = {}, 0
    for n in names:
        rows = BIG_SHAPES[n][0]
        out[n] = lax.slice_in_dim(arr, o, o + rows, axis=axis)
        o += rows
    return out


def _gathered(units, ag, own, me):
    out = {}
    for names, arr, mine in zip(units, ag, own):
        slots = [jnp.where(me == j, mine, arr[j]) for j in range(4)]
        for n in names:
            parts = [_unit_split(s, names, 0)[n] for s in slots]
            out[n] = jnp.concatenate(parts, axis=0 if n in ROW_SHARDED else 1)
    return out


def _shards(n, g):
    r, w = BIG_SHAPES[n]
    if n in ROW_SHARDED:
        return [g[j * r:(j + 1) * r] for j in range(4)]
    return [g[:, j * w:(j + 1) * w] for j in range(4)]


def _pack_small(d, extra=None):
    flat = jnp.concatenate([d[n].reshape(-1) for n in SMALL] + ([extra.reshape(-1)] if extra is not None else []))
    return jnp.pad(flat, (0, SMALL_ROWS * LANE - flat.shape[0])).reshape(SMALL_ROWS, LANE)


def _unpack_small(packed):
    flat, out, o = packed.reshape(-1), {}, 0
    for n in SMALL:
        sz = int(np.prod(SMALL_SHAPES[n]))
        out[n] = flat[o:o + sz].reshape(SMALL_SHAPES[n])
        o += sz
    return out


def _perm_w_in(gathered, own, me):
    per = D_IN // 4

    def rows(a, b):
        out = []
        while a < b:
            j, lo = divmod(a, per)
            hi = min(b - j * per, per)
            out.append(jnp.where(me == j, own[lo:hi], gathered[j, lo:hi]))
            a = j * per + hi
        return out

    z = lambda n: [jnp.zeros((n, own.shape[1]), own.dtype)]
    lora = []
    for i in range(4):
        lora += rows(4160 + LORA * i, 4160 + LORA * (i + 1)) + z(LANE - LORA)
    return jnp.concatenate(rows(0, 1024) + rows(1088, 4160) + rows(4544, D_IN) + lora + rows(1024, 1088)
                           + z(256 - ROPE), axis=0)


def _unperm_w_in(g):
    lora = [g[OFF_LORA + LANE * i:OFF_LORA + LANE * i + LORA] for i in range(4)]
    return jnp.concatenate([g[0:1024], g[OFF_KR:OFF_KR + ROPE], g[1024:4096]] + lora + [g[4096:OFF_LORA]], axis=0)


def _perm_wq(w):
    w3 = w.reshape(Q_RANK, HEADS, NOPE + ROPE)
    rope = jnp.pad(w3[:, :, NOPE:], ((0, 0), (0, 0), (0, LANE - ROPE)))
    return jnp.concatenate([w3[:, :, :NOPE].reshape(Q_RANK, -1), rope.reshape(Q_RANK, -1)], axis=1)


def _unperm_wq(g):
    return jnp.concatenate([g[:, :1024].reshape(Q_RANK, HEADS, NOPE),
                            g[:, 1024:].reshape(Q_RANK, HEADS, LANE)[:, :, :ROPE]], axis=2).reshape(Q_RANK, -1)


def _perm_wkv(w):
    w3 = w.reshape(KV_RANK, HEADS, NOPE + VDIM)
    return jnp.concatenate([w3[:, :, :NOPE].reshape(KV_RANK, -1), w3[:, :, NOPE:].reshape(KV_RANK, -1)], axis=1)


def _unperm_wkv(g):
    return jnp.concatenate([g[:, :1024].reshape(KV_RANK, HEADS, NOPE), g[:, 1024:].reshape(KV_RANK, HEADS, VDIM)],
                           axis=2).reshape(KV_RANK, -1)


def _pad_rows(w):
    return jnp.pad(w, ((0, LANE - LORA), (0, 0)))


def _perm_mu(mu):
    parts = [mu[:3072]]
    for i in range(4):
        parts += [mu[3072 + LORA * i:3072 + LORA * (i + 1)], jnp.zeros((LANE - LORA,), mu.dtype)]
    return jnp.concatenate(parts).reshape(1, NLERP)


def _unperm_mu(g):
    g = g.reshape(-1)
    return jnp.concatenate([g[:3072]] + [g[3072 + LANE * i:3072 + LANE * i + LORA] for i in range(4)])


def _constants():
    g2 = np.kron(np.eye(2, dtype=np.float32), np.ones((RN, RN), np.float32))
    pos = jnp.arange(T, dtype=F32)
    inv_freq = jnp.power(ROPE_THETA, -jnp.arange(0, ROPE, 2, dtype=F32) / ROPE)
    ang = pos[:, None] * inv_freq[None, :]
    cos, sin, zero = jnp.cos(ang), jnp.sin(ang), jnp.zeros((T, LANE - ROPE), F32)
    cq = jnp.tile(jnp.concatenate([cos, cos, zero], axis=1), (1, HEADS))
    sq = jnp.tile(jnp.concatenate([-sin, sin, zero], axis=1), (1, HEADS))
    return jnp.asarray(g2, BF16), cq, sq


def _step(x, tgt, w, m, v):
    x2, tgt2 = x.reshape(T, D), tgt.reshape(T, D)
    g2, cq, sq = _constants()
    row = lambda n: w[n].reshape(1, -1)
    w, m, v = ({**t, 'w_in': t['w_in'].T} for t in (w, m, v))

    core, chip = lax.axis_index("c"), 2 * lax.axis_index("x") + lax.axis_index("y")
    core1, chip1 = core.astype(jnp.int32).reshape(1), chip.astype(jnp.int32).reshape(1)
    own_bf = [_unit_cat([w[n].astype(BF16) for n in u]) for u in UNITS]
    wp = _perm_w_in(gather_weights(own_bf[:1], UNIT_AXIS[:1])[0], own_bf[0], chip)
    full = {}
    mu_p = _perm_mu(w['rwkv_mu'])

    st_pre = Stage("pre", f_pre, [(D, BF16), (D, None)], 256, [0], [0], [F32])
    st_mla = Stage("mla", f_mla, [(1024, BF16), (1024, BF16), (1024, BF16), (LANE, BF16), (1024, BF16)], 256,
                   [0, 1, 2], [0, 1, 2, 3], [BF16] * 3)
    st_rpre = Stage("rwkv_pre", f_rwkv_pre, [(RW, F32)] * 9, 256, [0], list(range(10)), [F32])
    st_rpost = Stage("rwkv_post", f_rwkv_post, [(RW, BF16)], 256, [0, 2, 3, 4, 5, 6], [0, 1, 2],
                     [F32, F32, F32, F32, F32, BF16])
    st_gate = Stage("gate", f_gate, [(RW, BF16)], 256, [0, 1], [], [F32, BF16])
    st_merge = Stage("merge", f_merge, [(D, BF16)], 256, [0, 1, 2, 3], [], [BF16] * 4)

    pre_rows, pre_par = [(x2, D, 0)], [row('g_pre')]
    (h,) = st_pre.fwd(pre_rows, pre_par)
    proj, rest = matmul("mm_in", h, wp, "nt", side=gather_side(own_bf[1:], UNIT_AXIS[1:]))
    full.update(_gathered(UNITS[1:], rest, own_bf[1:], chip))
    wq, wkv = _perm_wq(full['mla_wq_b']), _perm_wkv(full['mla_wkv_b'])
    lora_w = [_pad_rows(full[n]).astype(F32) for n in ('rwkv_w2_f', 'rwkv_w2_b', 'rwkv_a2_f', 'rwkv_a2_b')]

    mla_rows = [(proj, 512, OFF_QA // 512), (proj, 512, OFF_KVA // 512), (proj, 256, OFF_KR // 256),
                (cq, 1024, 0), (sq, 1024, 0), (cq, LANE, 0), (sq, LANE, 0)]
    mla_par = [row('mla_q_norm'), row('mla_kv_norm'), wq, wkv]
    att = st_mla.fwd(mla_rows, mla_par)
    y_mla, lse = attn_fwd(*att)

    lerp = shift_fwd(proj, mu_p)
    rpre_rows = [(lerp, NLERP, 0)]
    rpre_par = [row('rwkv_w0_f'), row('rwkv_w0_b'), row('rwkv_a0_f'), row('rwkv_a0_b'), row('rwkv_k_k'),
                row('rwkv_k_a')] + lora_w + [g2]
    r_, v_, lwf, lwb, kf, kb, an, bf_, bb_ = st_rpre.fwd(rpre_rows, rpre_par)
    fin = [r_, lwf, kf, v_, an, bf_]
    bin_ = [r_, lwb, kb, v_, an, bb_]
    yf, h0f, uf = scan_fwd("scan_f", *fin, reverse=False)
    yb, h0b, ub = scan_fwd("scan_b", *bin_, reverse=True)
    rpost_rows = [(yf, RW, 0), (yb, RW, 0), (r_, RW, 0), (kf, RW, 0), (kb, RW, 0),
                  (v_, RW, 0), (proj, RW, OFF_ZR // RW)]
    rpost_par = [row('rwkv_gn_g'), row('rwkv_gn_b'), row('rwkv_r_k'), g2]
    (gr,) = st_rpost.fwd(rpost_rows, rpost_par)
    gate_rows = [(y_mla, RW, 0), (proj, RW, OFF_ZM // RW)]
    (gm,) = st_gate.fwd(gate_rows, [])
    um = matmul("mm_br_mla", gm, full['w_br_mla'], "nn")
    ur = matmul("mm_br_rwkv", gr, full['w_br_rwkv'], "nn")
    merge_rows = [(um, D, 0), (ur, D, 0), (proj, D, OFF_GM // D), (proj, D, OFF_GR // D)]
    (merged,) = st_merge.fwd(merge_rows, [])
    out = matmul("mm_out", merged, full['w_out'], "nn")
    d_out, dy, dg_post, loss_blk = loss_stage(out, x2, tgt2, row('g_post'))

    gw = {'g_post': dg_post}
    d_merged = matmul("mm_out_dx", d_out, full['w_out'], "nt")
    gw['w_out'] = matmul("mm_out_dw", merged, d_out, "tn")
    (d_um, d_ur, d_gm, d_gr), _ = st_merge.bwd(merge_rows, [], [[(d_merged, D, 0)]])
    d_gmla = matmul("mm_br_mla_dx", d_um, full['w_br_mla'], "nt")
    gw['w_br_mla'] = matmul("mm_br_mla_dw", gm, d_um, "tn")
    d_grw = matmul("mm_br_rwkv_dx", d_ur, full['w_br_rwkv'], "nt")
    gw['w_br_rwkv'] = matmul("mm_br_rwkv_dw", gr, d_ur, "tn")
    (d_ymla, d_zm), _ = st_gate.bwd(gate_rows, [], [[(d_gmla, RW, 0)]])
    (d_y, d_r3, d_kf2, d_kb2, d_v3, d_zr), (gw['rwkv_gn_g'], gw['rwkv_gn_b'], d_rk) = st_rpost.bwd(
        rpost_rows, rpost_par, [[(d_grw, RW, 0)]])
    gw['rwkv_r_k'] = d_rk
    sf = scan_bwd("scan_f_bwd", *fin, h0f, uf, d_y, reverse=False)
    sb = scan_bwd("scan_b_bwd", *bin_, h0b, ub, d_y, reverse=True)
    c = lambda *ts: [(t, RW, 0) for t in ts]
    rpre_cts = [c(sf[0], sb[0], d_r3), c(sf[3], sb[3], d_v3), c(sf[1]), c(sb[1]), c(sf[2], d_kf2), c(sb[2], d_kb2),
                c(sf[4], sb[4]), c(sf[5]), c(sb[5])]
    (d_rin,), rpre_g = st_rpre.bwd(rpre_rows, rpre_par, rpre_cts)
    for n, gval in zip(('rwkv_w0_f', 'rwkv_w0_b', 'rwkv_a0_f', 'rwkv_a0_b', 'rwkv_k_k', 'rwkv_k_a'), rpre_g[:6]):
        gw[n] = gval
    for n, gval in zip(('rwkv_w2_f', 'rwkv_w2_b', 'rwkv_a2_f', 'rwkv_a2_b'), rpre_g[6:]):
        gw[n] = gval[:LORA]
    d_lerp, d_mu = shift_bwd(proj, mu_p, d_rin)
    gw['rwkv_mu'] = _unperm_mu(d_mu)

    mla_cts = [[(t, t.shape[1], 0)] for t in attn_bwd(*att, lse, d_ymla)]
    (d_qa, d_kva, d_kr), (gw['mla_q_norm'], gw['mla_kv_norm'], d_wq, d_wkv) = st_mla.bwd(mla_rows, mla_par, mla_cts)
    gw['mla_wq_b'], gw['mla_wkv_b'] = _unperm_wq(d_wq), _unperm_wkv(d_wkv)

    dproj = jnp.concatenate([d_qa, d_kva, d_lerp[:, :3072], d_zm, d_zr, d_gm, d_gr, d_lerp[:, 3072:], d_kr], axis=1)

    def pair_sums(name, ids):
        send = [jnp.stack([_unit_cat([_shards(n, gw[n])[j].astype(BF16) for n in UNITS[i]]) for j in range(4)])
                for i in ids]
        axes = [UNIT_AXIS[i] for i in ids]
        other = pair_exchange(name, send, axes)
        return [pair_sum(f"pair_sum_{i}", core1, s, o, ax) for i, s, o, ax in zip(ids, send, other, axes)]

    late, early = [0], list(range(1, len(UNITS)))
    pairs_e = pair_sums("pair_exchange_rest", early)
    gw_in, recv_e = matmul("mm_in_dw", dproj, h, "tn", BF16, side=scatter_side(pairs_e))
    gw['w_in'] = _unperm_w_in(gw_in)
    pairs_l = pair_sums("pair_exchange_w_in", late)
    sems, src_fly, land_fly, token = scatter_start("scatter_w_in_start", pairs_l)
    dh = matmul("mm_in_dx", dproj, wp, "nn", after=(token,))
    (grad_x,), (gw['g_pre'],) = st_pre.bwd(pre_rows, pre_par, [[(dh, D, 0)], [(dy, D, 0)]])

    big = [dict() for _ in range(4)]

    def update(name, ids, recv, pairs):
        mine = [sum4(f"sum4_{i}", chip1, r, p, UNIT_AXIS[i]) for i, r, p in zip(ids, recv, pairs)]
        theirs = swap_halves(name, mine)
        for i, mi, th in zip(ids, mine, theirs):
            res = adamw_halves(f"adamw_{i}", core1, *[_unit_cat([t[n] for n in UNITS[i]]) for t in (w, m, v)], mi, th,
                               UNIT_AXIS[i])
            for q in range(4):
                big[q].update(_unit_split(res[q], UNITS[i], 0))
        return res

    small_fly = allgather8_start("gather_small_start", _pack_small(gw, loss_blk[0, :1]))
    last = update("swap_halves_rest", early, recv_e, pairs_e)
    own_small, landed = allgather8_wait("gather_small_wait", *small_fly, last[0])
    dev = 2 * chip + core
    parts = [jnp.where(dev == i, own_small, landed[i]) for i in range(8)]
    small = adamw("adamw_small", _pack_small(w), _pack_small(m), _pack_small(v), parts)
    pairs_l, recv_l = scatter_wait("scatter_w_in_wait", sems, src_fly, land_fly, small[0] + last[0][:1, :1])
    update("swap_halves_w_in", late, recv_l, pairs_l)

    outs = []
    for b_d, s_arr in zip(big, small):
        d = {**b_d, **_unpack_small(s_arr)}
        d['w_in'] = d['w_in'].T
        outs.append([d[n] for n in WEIGHTS])
    loss = small[0][SMALL_LEN // LANE, 0]
    return (loss, grad_x.reshape(1, T, D), *outs[0], *outs[1], *outs[2], *outs[3])


def kernel(x, g_pre, w_in, mla_q_norm, mla_wq_b, mla_kv_norm, mla_wkv_b, rwkv_mu, rwkv_w0_f, rwkv_w2_f, rwkv_w0_b, rwkv_w2_b, rwkv_a0_f, rwkv_a2_f, rwkv_a0_b, rwkv_a2_b, rwkv_k_k, rwkv_k_a, rwkv_r_k, rwkv_gn_g, rwkv_gn_b, w_br_mla, w_br_rwkv, w_out, g_post, loss_target, m_g_pre, m_w_in, m_mla_q_norm, m_mla_wq_b, m_mla_kv_norm, m_mla_wkv_b, m_rwkv_mu, m_rwkv_w0_f, m_rwkv_w2_f, m_rwkv_w0_b, m_rwkv_w2_b, m_rwkv_a0_f, m_rwkv_a2_f, m_rwkv_a0_b, m_rwkv_a2_b, m_rwkv_k_k, m_rwkv_k_a, m_rwkv_r_k, m_rwkv_gn_g, m_rwkv_gn_b, m_w_br_mla, m_w_br_rwkv, m_w_out, m_g_post, v_g_pre, v_w_in, v_mla_q_norm, v_mla_wq_b, v_mla_kv_norm, v_mla_wkv_b, v_rwkv_mu, v_rwkv_w0_f, v_rwkv_w2_f, v_rwkv_w0_b, v_rwkv_w2_b, v_rwkv_a0_f, v_rwkv_a2_f, v_rwkv_a0_b, v_rwkv_a2_b, v_rwkv_k_k, v_rwkv_k_a, v_rwkv_r_k, v_rwkv_gn_g, v_rwkv_gn_b, v_w_br_mla, v_w_br_rwkv, v_w_out, v_g_post):
    given = dict(locals())
    w = {n: given[n] for n in WEIGHTS}
    m = {n: given['m_' + n] for n in WEIGHTS}
    v = {n: given['v_' + n] for n in WEIGHTS}
    return _step(x, loss_target, w, m, v)
```

```python
import functools
import math

import numpy as np
import jax
import jax.numpy as jnp
from jax import lax
from jax.experimental import pallas as pl
from jax.experimental.pallas import tpu as pltpu

F32, BF16 = jnp.float32, jnp.bfloat16
MESH_IDS = pl.DeviceIdType.MESH

D = 2048
T = 2048
HEADS = 8
Q_RANK = 512
KV_RANK = 512
NOPE = 128
ROPE = 64
VDIM = 128
RW = 1024
RH = 16
RN = 64
LORA = 96
D_IN = 10688
NORM_EPS = 1e-6
GN_EPS = 64e-5
ROPE_THETA = 10000.0
ADAM_LR, ADAM_B1, ADAM_B2, ADAM_EPS, ADAM_WD, ADAM_STEP = 0.001, 0.9, 0.999, 1e-08, 0.01, 10

LANE = 128
VMEM_BIG = 56 * 2**20

NP = 11008
OFF_QA, OFF_KVA, OFF_RKV, OFF_ZM, OFF_ZR, OFF_GM, OFF_GR, OFF_LORA, OFF_KR = 0, 512, 1024, 4096, 5120, 6144, 8192, 10240, 10752
NLERP = 3584

CHUNK = 64
NCH = T // CHUNK


def _dg(a, b, ca, cb, batch=False, prec=None):
    bd = ((0,), (0,)) if batch else ((), ())
    return lax.dot_general(a, b, (((ca,), (cb,)), bd), precision=prec, preferred_element_type=F32)


@jax.custom_vjp
def bdot(a, b):
    return _dg(a.astype(BF16), b.astype(BF16), 1, 0)


def _bdot_fwd(a, b):
    return bdot(a, b), (a, b)


def _bdot_bwd(res, g):
    a, b = res
    gb = g.astype(BF16)
    da = _dg(gb, b.astype(BF16), 1, 1)
    db = _dg(a.astype(BF16), gb, 0, 0)
    return da.astype(a.dtype), db.astype(b.dtype)


bdot.defvjp(_bdot_fwd, _bdot_bwd)


def _split(x):
    hi = x.astype(BF16)
    lo = (x - hi.astype(F32)).astype(BF16)
    return hi, lo


@jax.custom_vjp
def gsum(x, g2):
    hi, lo = _split(x)
    return _dg(hi, g2, 1, 0) + _dg(lo, g2, 1, 0)


def _gsum_fwd(x, g2):
    return gsum(x, g2), g2


def _gsum_bwd(g2, g):
    hi, lo = _split(g)
    return _dg(hi, g2, 1, 1) + _dg(lo, g2, 1, 1), jnp.zeros_like(g2)


gsum.defvjp(_gsum_fwd, _gsum_bwd)


def headsum(x, g2):
    return jnp.concatenate([gsum(x[:, i * LANE:(i + 1) * LANE], g2) for i in range(x.shape[1] // LANE)], axis=1)


def _terms(x, n):
    out = []
    for i in range(n):
        t = x.astype(BF16)
        out.append(t)
        if i < n - 1:
            x = x - t.astype(F32)
    return out


def _bmm(a, b, ca, cb, na, nb):
    acc = None
    for i, ai in enumerate(_terms(a, na)):
        for j, bj in enumerate(_terms(b, nb)):
            if i + j < max(na, nb):
                p = _dg(ai, bj, ca, cb, True)
                acc = p if acc is None else acc + p
    return acc


_NN, _NT, _TN = (2, 1), (2, 2), (1, 1)


def _make_dots(nf, nb_nn, nb_nt, nb_tn):
    @jax.custom_vjp
    def nn(a, b):
        return _bmm(a, b, *_NN, nf, nf)

    @jax.custom_vjp
    def nt(a, b):
        return _bmm(a, b, *_NT, nf, nf)

    @jax.custom_vjp
    def tn(a, b):
        return _bmm(a, b, *_TN, nf, nf)

    nn.defvjp(lambda a, b: (nn(a, b), (a, b)),
              lambda r, g: (_bmm(g, r[1], *_NT, nb_nn, nb_nn), _bmm(r[0], g, *_TN, nb_nn, nb_nn)))
    nt.defvjp(lambda a, b: (nt(a, b), (a, b)),
              lambda r, g: (_bmm(g, r[1], *_NN, 1, nb_nt), _bmm(g, r[0], *_TN, 1, nb_nt)))
    tn.defvjp(lambda a, b: (tn(a, b), (a, b)),
              lambda r, g: (_bmm(r[1], g, *_NT, nb_tn, nb_tn), _bmm(r[0], g, *_NN, nb_tn, nb_tn)))
    return nn, nt, tn


_SCAN_NF, _SCAN_NB = 1, 1
nn, nt, tn = _make_dots(_SCAN_NF, 1, 2, 1)
_, nt_state, _ = _make_dots(_SCAN_NF, 1, 1, 1)


@jax.custom_vjp
def cumdot(ones, x):
    return _bmm(ones, x, *_NN, 1, 2)


cumdot.defvjp(lambda o, x: (cumdot(o, x), o), lambda o, g: (jnp.zeros_like(o), _bmm(o, g, *_TN, 1, 2)))


def _solve_powers(l):
    pw = [l]
    for _ in range(int(math.log2(l.shape[-1])) - 1):
        pw.append(_bmm(pw[-1], pw[-1], *_NN, _SCAN_NF, _SCAN_NF))
    return pw


@jax.custom_vjp
def tri_solve(l, rhs):
    x = rhs
    for p in _solve_powers(l):
        x = x + _bmm(p, x, *_NN, _SCAN_NF, _SCAN_NF)
    return x


def _tri_solve_fwd(l, rhs):
    pw = _solve_powers(l)
    x = rhs
    for p in pw:
        x = x + _bmm(p, x, *_NN, _SCAN_NF, _SCAN_NF)
    return x, (pw, x)


def _tri_solve_bwd(res, g):
    pw, x = res
    y = g
    for p in pw:
        y = y + _bmm(p, y, *_TN, _SCAN_NB, _SCAN_NB)
    return _bmm(y, x, *_NT, _SCAN_NB, _SCAN_NB), y


tri_solve.defvjp(_tri_solve_fwd, _tri_solve_bwd)


@jax.custom_vjp
def known_solve(l, rhs, x):
    return x


known_solve.defvjp(lambda l, rhs, x: (x, (_solve_powers(l), x)),
                   lambda res, g: _tri_solve_bwd(res, g) + (jnp.zeros_like(g),))


def _rms(x, g):
    return x * lax.rsqrt(jnp.mean(x * x, axis=-1, keepdims=True) + NORM_EPS) * g


def _softplus(x):
    pos = x > 0
    return jnp.where(pos, x, 0.0) + jnp.log(1.0 + jnp.exp(-jnp.where(pos, x, -x)))


def _silu(z):
    return z * jax.nn.sigmoid(z)


_MM_VMEM_BYTES = 32 * 2**20


def _mm_tiles(m, n, k):
    best = None
    for tm in (2048, 1024, 512, 256):
        for tn_ in (2048, 1024, 512, 256):
            for d in range(k // LANE, 0, -1):
                tk = LANE * d
                if m % tm or n % tn_ or k % tk:
                    continue
                nk = k // tk
                vmem = 4 * tk * (tm + tn_) + 8 * tm * tn_ + (4 * tm * tn_ if nk > 1 else 0)
                if vmem > _MM_VMEM_BYTES:
                    continue
                a_reads = n // tn_ if nk > 1 else 1
                b_reads = 1 if (nk == 1 and n == tn_) else m // tm
                acc_rmw = nk * m * n if nk > 1 else 0
                cost = (a_reads * m * k + b_reads * k * n + acc_rmw, -tm * tn_ * tk)
                if best is None or cost < best[0]:
                    best = (cost, (tm, tn_, tk))
    return best[1]


class Side:
    def __init__(self, ins, outs, sems, start, finish):
        self.ins, self.outs, self.sems, self.start, self.finish = ins, outs, sems, start, finish

    def at_step(self, step, steps, *refs):
        @pl.when(step == 0)
        def _():
            self.start(*refs)

    def at_end(self, step, steps, *refs):
        @pl.when(step == steps - 1)
        def _():
            self.finish(*refs)

    def run(self, *refs):
        self.start(*refs)
        self.finish(*refs)


def matmul(name, a, b, mode, out_dtype=F32, side=None, after=()):
    if mode == "nn":
        (m, k), n = a.shape, b.shape[1]
    elif mode == "nt":
        (m, k), n = a.shape, b.shape[0]
    else:
        (k, m), n = a.shape, b.shape[1]
    tm, tn_, tk = _mm_tiles(m, n, k)
    nk = k // tk
    if mode == "nn":
        a_spec = pl.BlockSpec((tm, tk), lambda i, j, kk: (i, kk))
        b_spec = pl.BlockSpec((tk, tn_), lambda i, j, kk: (kk, j))
        ca, cb = 1, 0
    elif mode == "nt":
        a_spec = pl.BlockSpec((tm, tk), lambda i, j, kk: (i, kk))
        b_spec = pl.BlockSpec((tn_, tk), lambda i, j, kk: (j, kk))
        ca, cb = 1, 1
    else:
        a_spec = pl.BlockSpec((tk, tm), lambda i, j, kk: (kk, i))
        b_spec = pl.BlockSpec((tk, tn_), lambda i, j, kk: (kk, j))
        ca, cb = 0, 0

    grid = (m // tm, n // tn_, nk)
    n_in = len(side.ins) if side else 0
    n_out = len(side.outs) if side else 0
    n_dep = len(after)

    def body(a_ref, b_ref, *rest):
        rest = rest[n_dep:]
        s_ins, o_ref, s_outs = rest[:n_in], rest[n_in], rest[n_in + 1:n_in + 1 + n_out]
        scratch = rest[n_in + 1 + n_out:]
        acc, s_sems = (scratch[:1], scratch[1:]) if nk > 1 else ((), scratch)
        steps = grid[0] * grid[1] * grid[2]
        if side:
            step = (pl.program_id(0) * grid[1] + pl.program_id(1)) * grid[2] + pl.program_id(2)
            side.at_step(step, steps, s_ins, s_outs, s_sems)

        part = _dg(a_ref[...].astype(BF16), b_ref[...].astype(BF16), ca, cb)
        if nk == 1:
            o_ref[...] = part.astype(o_ref.dtype)
        else:
            acc_ref, kk = acc[0], pl.program_id(2)

            @pl.when(kk == 0)
            def _():
                acc_ref[...] = part

            @pl.when(kk > 0)
            def _():
                acc_ref[...] += part

            @pl.when(kk == nk - 1)
            def _():
                o_ref[...] = acc_ref[...].astype(o_ref.dtype)

        if side:
            side.at_end(step, steps, s_ins, s_outs, s_sems)

    res = pl.pallas_call(
        body, name=name, grid=grid,
        in_specs=[a_spec, b_spec] + [_ANY] * (n_dep + n_in),
        out_specs=[pl.BlockSpec((tm, tn_), lambda i, j, kk: (i, j))] + [_ANY] * n_out,
        out_shape=[jax.ShapeDtypeStruct((m, n), out_dtype)] + (list(side.outs) if side else []),
        scratch_shapes=([pltpu.VMEM((tm, tn_), F32)] if nk > 1 else []) + (list(side.sems) if side else []),
        compiler_params=pltpu.CompilerParams(
            dimension_semantics=("arbitrary",) * 3 if side else ("parallel", "parallel", "arbitrary"),
            vmem_limit_bytes=VMEM_BIG),
    )(a, b, *after, *(side.ins if side else []))
    return (res[0], res[1:]) if side else res[0]


def _rspec(tr, width, blk):
    return pl.BlockSpec((tr, width), lambda i: (i, blk))


def _full_spec(arr):
    return pl.BlockSpec(arr.shape, lambda i: (0,) * arr.ndim)


class Stage:
    def __init__(self, name, f, outs, tr, diff_rows, diff_params, drow_dtypes):
        self.name, self.f, self.outs, self.tr = name, f, outs, tr
        self.diff_rows, self.diff_params, self.drow_dtypes = diff_rows, diff_params, drow_dtypes

    def fwd(self, rows, params):
        f, nr, npar = self.f, len(rows), len(params)
        stored = [(w, dt) for (w, dt) in self.outs if dt is not None]
        keep = [i for i, (w, dt) in enumerate(self.outs) if dt is not None]

        def body(*refs):
            vals = f(*[r[...].astype(F32) for r in refs[:nr]], *[p[...] for p in refs[nr:nr + npar]])
            for o_ref, i in zip(refs[nr + npar:], keep):
                o_ref[...] = vals[i].astype(o_ref.dtype)

        return pl.pallas_call(
            body, name=self.name + "_fwd", grid=(T // self.tr,),
            in_specs=[_rspec(self.tr, w, b) for (_, w, b) in rows] + [_full_spec(p) for p in params],
            out_specs=[_rspec(self.tr, w, 0) for (w, _) in stored],
            out_shape=[jax.ShapeDtypeStruct((T, w), dt) for (w, dt) in stored],
            compiler_params=pltpu.CompilerParams(dimension_semantics=("arbitrary",), vmem_limit_bytes=VMEM_BIG),
        )(*[r[0] for r in rows], *params)

    def bwd(self, rows, params, cts):
        f, nr, npar = self.f, len(rows), len(params)
        dr_idx, dp_idx = self.diff_rows, self.diff_params
        flat_cts = [c for lst in cts for c in lst]
        nct = len(flat_cts)

        def body(*refs):
            row_refs, par_refs = refs[:nr], refs[nr:nr + npar]
            ct_refs = refs[nr + npar:nr + npar + nct]
            drow_refs = refs[nr + npar + nct:nr + npar + nct + len(dr_idx)]
            dpar_refs = refs[nr + npar + nct + len(dr_idx):]
            row_vals = [r[...].astype(F32) for r in row_refs]
            par_vals = [p[...] for p in par_refs]

            def g(*dv):
                rv, pv = list(row_vals), list(par_vals)
                for j, i in enumerate(dr_idx):
                    rv[i] = dv[j]
                for j, i in enumerate(dp_idx):
                    pv[i] = dv[len(dr_idx) + j]
                return f(*rv, *pv)

            _, vjp = jax.vjp(g, *[row_vals[i] for i in dr_idx], *[par_vals[i] for i in dp_idx])
            ct_vals, pos = [], 0
            for lst in cts:
                acc = ct_refs[pos][...].astype(F32)
                for q in range(1, len(lst)):
                    acc = acc + ct_refs[pos + q][...].astype(F32)
                pos += len(lst)
                ct_vals.append(acc)
            grads = vjp(tuple(ct_vals))
            for j, r in enumerate(drow_refs):
                r[...] = grads[j].astype(r.dtype)

            @pl.when(pl.program_id(0) == 0)
            def _():
                for r in dpar_refs:
                    r[...] = jnp.zeros_like(r)

            for j, r in enumerate(dpar_refs):
                r[...] += grads[len(dr_idx) + j].astype(F32)

        drow_shapes = [jax.ShapeDtypeStruct((T, rows[i][1]), dt) for i, dt in zip(dr_idx, self.drow_dtypes)]
        dpar_shapes = [jax.ShapeDtypeStruct(params[i].shape, F32) for i in dp_idx]
        res = pl.pallas_call(
            body, name=self.name + "_bwd", grid=(T // self.tr,),
            in_specs=[_rspec(self.tr, w, b) for (_, w, b) in rows] + [_full_spec(p) for p in params]
            + [_rspec(self.tr, w, b) for (_, w, b) in flat_cts],
            out_specs=[_rspec(self.tr, rows[i][1], 0) for i in dr_idx] + [_full_spec(params[i]) for i in dp_idx],
            out_shape=drow_shapes + dpar_shapes,
            compiler_params=pltpu.CompilerParams(dimension_semantics=("arbitrary",), vmem_limit_bytes=VMEM_BIG),
        )(*[r[0] for r in rows], *params, *[c[0] for c in flat_cts])
        return res[:len(dr_idx)], res[len(dr_idx):]


def f_pre(x, g):
    return _rms(x, g), x


@jax.custom_vjp
def swap32(t):
    width = t.shape[1]
    lane = lax.broadcasted_iota(jnp.int32, t.shape, 1) % LANE
    return jnp.where(lane < 32, pltpu.roll(t, width - 32, 1), jnp.where(lane < 64, pltpu.roll(t, 32, 1), 0.0))


swap32.defvjp(lambda t: (swap32(t), None), lambda _, g: (swap32(g),))


def f_mla(q_a, kv_a, kr, cq, sq, ck, sk, gq, gkv, wq, wkv):
    q = bdot(_rms(q_a, gq), wq)
    kv = bdot(_rms(kv_a, gkv), wkv)
    t, k = q[:, 1024:], kr[:, :LANE]
    return (q[:, :1024], t * cq + swap32(t) * sq, kv[:, :1024], k * ck + swap32(k) * sk, kv[:, 1024:])


def f_rwkv_pre(lerp, w0f, w0b, a0f, a0b, kkw, kaw, w2f, w2b, a2f, a2b, g2):
    r, k, v = lerp[:, :RW], lerp[:, RW:2 * RW], lerp[:, 2 * RW:3 * RW]
    wdf, wdb, adf, adb = (lerp[:, 3 * RW + i * LANE:3 * RW + (i + 1) * LANE] for i in range(4))

    def logdecay(w0, wd, w2):
        z = w0 + bdot(jnp.tanh(wd), w2)
        return -jnp.exp(-_softplus(-z) - 0.5)

    a_f = jax.nn.sigmoid(a0f + bdot(adf, a2f))
    a_b = jax.nn.sigmoid(a0b + bdot(adb, a2b))
    kk = k * kkw
    kk = kk / jnp.maximum(jnp.sqrt(headsum(kk * kk, g2)), 1e-12)
    return (r, v, logdecay(w0f, wdf, w2f), logdecay(w0b, wdb, w2b),
            k * (1.0 + (a_f - 1.0) * kaw), k * (1.0 + (a_b - 1.0) * kaw), -kk, kk * a_f, kk * a_b)


def f_rwkv_post(yf, yb, r, kf, kb, v, z, gng, gnb, rk, g2):
    y = yf + yb
    mu = headsum(y, g2) * (1.0 / RN)
    d = y - mu
    var = headsum(d * d, g2) * (1.0 / RN)
    yn = d * lax.rsqrt(var + GN_EPS) * gng + gnb
    bonus = headsum(r * (kf + kb) * rk, g2) * v
    return ((yn + bonus) * _silu(z),)


def f_gate(y, z):
    return (y * _silu(z),)


def f_merge(um, ur, gm, gr):
    return (jax.nn.sigmoid(gm) * um + jax.nn.sigmoid(gr) * ur,)


_SHIFT_W = 256


def _lerp_colblock(j):
    return jnp.where(j < 3072 // _SHIFT_W, OFF_RKV // _SHIFT_W + j, OFF_LORA // _SHIFT_W + j - 3072 // _SHIFT_W)


def _nbr_mean(x):
    row = lax.broadcasted_iota(jnp.int32, x.shape, 0)
    up = jnp.where(row == 0, 0.0, pltpu.roll(x, 1, 0))
    dn = jnp.where(row == T - 1, 0.0, pltpu.roll(x, T - 1, 0))
    return 0.5 * (up + dn)


def shift_fwd(proj, mu):
    def body(x_ref, mu_ref, o_ref):
        x = x_ref[...]
        o_ref[...] = x + mu_ref[...] * (_nbr_mean(x) - x)

    return pl.pallas_call(
        body, name="shift_fwd", grid=(NLERP // _SHIFT_W,),
        in_specs=[pl.BlockSpec((T, _SHIFT_W), lambda j: (0, _lerp_colblock(j))),
                  pl.BlockSpec((1, _SHIFT_W), lambda j: (0, j))],
        out_specs=pl.BlockSpec((T, _SHIFT_W), lambda j: (0, j)),
        out_shape=jax.ShapeDtypeStruct((T, NLERP), F32),
        compiler_params=pltpu.CompilerParams(dimension_semantics=("parallel",), vmem_limit_bytes=VMEM_BIG),
    )(proj, mu)


def shift_bwd(proj, mu, g):
    def body(x_ref, mu_ref, g_ref, dx_ref, dmu_ref):
        x, gv = x_ref[...], g_ref[...]
        dmu_ref[...] = jnp.sum(gv * (_nbr_mean(x) - x), axis=0, keepdims=True)
        gm = gv * mu_ref[...]
        dx_ref[...] = (gv - gm + _nbr_mean(gm)).astype(dx_ref.dtype)

    col = pl.BlockSpec((T, _SHIFT_W), lambda j: (0, j))
    vec = pl.BlockSpec((1, _SHIFT_W), lambda j: (0, j))
    return pl.pallas_call(
        body, name="shift_bwd", grid=(NLERP // _SHIFT_W,),
        in_specs=[pl.BlockSpec((T, _SHIFT_W), lambda j: (0, _lerp_colblock(j))), vec, col],
        out_specs=[col, vec],
        out_shape=[jax.ShapeDtypeStruct((T, NLERP), BF16), jax.ShapeDtypeStruct((1, NLERP), F32)],
        compiler_params=pltpu.CompilerParams(dimension_semantics=("parallel",), vmem_limit_bytes=VMEM_BIG),
    )(proj, mu, g)


_TQ_F, _TQ_B = 256, 512
_ATT_SCALE = (NOPE + ROPE) ** -0.5


def _probs(q, k, lse=None):
    s = _dg(q, k, 1, 1) * _ATT_SCALE
    if lse is not None:
        return jnp.exp(s - lse), lse
    m = jnp.max(s, axis=-1, keepdims=True)
    e = jnp.exp(s - m)
    l = jnp.sum(e, axis=-1, keepdims=True)
    return e * (1.0 / l), m + jnp.log(l)


def _q_blk(tq):
    return pl.BlockSpec((tq, LANE), lambda h, i: (i, h))


_K_BLK = pl.BlockSpec((T, LANE), lambda h, i: (0, h))
_KR_BLK = pl.BlockSpec((T, LANE), lambda h, i: (0, 0))


def _load_qk(qn_ref, qr_ref, kn_ref, kr_ref, kcat_ref):
    @pl.when(pl.program_id(1) == 0)
    def _():
        kcat_ref[:, :LANE] = kn_ref[...]
        kcat_ref[:, LANE:] = kr_ref[...]

    return jnp.concatenate([qn_ref[...], qr_ref[...]], axis=1), kcat_ref[...]


def attn_fwd(qn, qr, kn, kr, v):
    def body(qn_ref, qr_ref, kn_ref, kr_ref, v_ref, o_ref, lse_ref, kcat_ref):
        q, k = _load_qk(qn_ref, qr_ref, kn_ref, kr_ref, kcat_ref)
        p, lse = _probs(q, k)
        o_ref[...] = _dg(p.astype(BF16), v_ref[...], 1, 0)
        lse_ref[...] = jnp.broadcast_to(lse, lse_ref.shape)

    return pl.pallas_call(
        body, name="attn_fwd", grid=(HEADS, T // _TQ_F),
        in_specs=[_q_blk(_TQ_F), _q_blk(_TQ_F), _K_BLK, _KR_BLK, _K_BLK], out_specs=[_q_blk(_TQ_F)] * 2,
        out_shape=[jax.ShapeDtypeStruct((T, HEADS * VDIM), F32)] * 2,
        scratch_shapes=[pltpu.VMEM((T, 2 * LANE), BF16)],
        compiler_params=pltpu.CompilerParams(dimension_semantics=("arbitrary", "arbitrary"), vmem_limit_bytes=VMEM_BIG),
    )(qn, qr, kn, kr, v)


def attn_bwd(qn, qr, kn, kr, v, lse, o, do):
    def body(qn_ref, qr_ref, kn_ref, kr_ref, v_ref, lse_ref, o_ref, do_ref, dqn_ref, dqr_ref, dkn_ref, dkr_ref,
             dv_ref, kcat_ref):
        h, i = pl.program_id(0), pl.program_id(1)

        @pl.when(i == 0)
        def _():
            dkn_ref[...] = jnp.zeros_like(dkn_ref)
            dv_ref[...] = jnp.zeros_like(dv_ref)

        @pl.when((i == 0) & (h == 0))
        def _():
            dkr_ref[...] = jnp.zeros_like(dkr_ref)

        q, k = _load_qk(qn_ref, qr_ref, kn_ref, kr_ref, kcat_ref)
        do_f = do_ref[...]
        dob = do_f.astype(BF16)
        p, _ = _probs(q, k, lse_ref[:, :1])
        dv_ref[...] += _dg(p.astype(BF16), dob, 0, 0)
        dp = _dg(dob, v_ref[...], 1, 1)
        delta = jnp.sum(do_f * o_ref[...], axis=-1, keepdims=True)
        ds = (p * (dp - delta) * _ATT_SCALE).astype(BF16)
        dq = _dg(ds, k, 1, 0)
        dqn_ref[...] = dq[:, :LANE]
        dqr_ref[...] = dq[:, LANE:]
        dk = _dg(ds, q, 0, 0)
        dkn_ref[...] += dk[:, :LANE]
        dkr_ref[...] += dk[:, LANE:]

    wide = jax.ShapeDtypeStruct((T, HEADS * LANE), F32)
    return pl.pallas_call(
        body, name="attn_bwd", grid=(HEADS, T // _TQ_B),
        in_specs=[_q_blk(_TQ_B), _q_blk(_TQ_B), _K_BLK, _KR_BLK, _K_BLK] + [_q_blk(_TQ_B)] * 3,
        out_specs=[_q_blk(_TQ_B), _q_blk(_TQ_B), _K_BLK, _KR_BLK, _K_BLK],
        out_shape=[wide, wide, wide, jax.ShapeDtypeStruct((T, LANE), F32), wide],
        scratch_shapes=[pltpu.VMEM((T, 2 * LANE), BF16)],
        compiler_params=pltpu.CompilerParams(dimension_semantics=("arbitrary", "arbitrary"), vmem_limit_bytes=VMEM_BIG),
    )(qn, qr, kn, kr, v, lse, o, do)


def _chunk(r, lw, k, v, a, b, ht, u_kept=None, *, reverse):
    hb, c, _ = r.shape
    ti = lax.broadcasted_iota(jnp.int32, (c, c), 0)
    si = lax.broadcasted_iota(jnp.int32, (c, c), 1)
    incl = (si >= ti) if reverse else (si <= ti)
    strict = (si > ti) if reverse else (si < ti)
    ones = jnp.broadcast_to(incl.astype(F32)[None], (hb, c, c))
    cum = cumdot(ones, lw)
    cum_ex = cum - lw
    tot = jnp.sum(lw, axis=1, keepdims=True)
    mid = 0.5 * tot
    rt, at = r * jnp.exp(cum - mid), a * jnp.exp(cum_ex - mid)
    einv = jnp.exp(mid - cum)
    bk = jnp.concatenate([b * einv, k * einv], axis=1)
    m_a, m_r = nt(at, bk), nt(rt, bk)
    m_ab = jnp.where(strict, m_a[:, :, :c], 0.0)
    m_ak = jnp.where(strict, m_a[:, :, c:], 0.0)
    t2 = lax.broadcasted_iota(jnp.int32, (c, 2 * c), 0)
    s2 = lax.broadcasted_iota(jnp.int32, (c, 2 * c), 1)
    s2 = jnp.where(s2 >= c, s2 - c, s2)
    m_r = jnp.where((s2 >= t2) if reverse else (s2 <= t2), m_r, 0.0)
    rhs = nt_state(a * jnp.exp(cum_ex), ht) + nn(m_ak, v)
    u = tri_solve(m_ab, rhs) if u_kept is None else known_solve(m_ab, rhs, u_kept)
    uv = jnp.concatenate([u, v], axis=1)
    y = nt_state(r * jnp.exp(cum), ht) + nn(m_r, uv)
    eend = jnp.exp(tot - cum)
    ht_new = ht * jnp.exp(tot) + tn(uv, jnp.concatenate([b * eend, k * eend], axis=1))
    return y, ht_new, u


_HB_F, _HB_B = 16, 16


def _split_heads(x):
    return jnp.stack([x[:, i * RN:(i + 1) * RN] for i in range(x.shape[1] // RN)])


def _merge_heads(y):
    return jnp.concatenate([y[i] for i in range(y.shape[0])], axis=1)


def _chunk_map(reverse, backward):
    flip = reverse != backward
    return (lambda g, c: (NCH - 1 - c, g)) if flip else (lambda g, c: (c, g))


def scan_fwd(name, r, lw, k, v, a, b, reverse):
    hb = _HB_F
    cmap = _chunk_map(reverse, False)

    def body(r_ref, lw_ref, k_ref, v_ref, a_ref, b_ref, y_ref, h0_ref, u_ref, ht_ref):
        @pl.when(pl.program_id(1) == 0)
        def _():
            ht_ref[...] = jnp.zeros_like(ht_ref)

        ht = ht_ref[...]
        h0_ref[0] = ht
        ins = [_split_heads(x[...]) for x in (r_ref, lw_ref, k_ref, v_ref, a_ref, b_ref)]
        y, hn, u = _chunk(*ins, ht, reverse=reverse)
        y_ref[...] = _merge_heads(y)
        u_ref[...] = _merge_heads(u)
        ht_ref[...] = hn

    io = pl.BlockSpec((CHUNK, hb * RN), cmap)
    return pl.pallas_call(
        body, name=name, grid=(RH // hb, NCH),
        in_specs=[io] * 6,
        out_specs=[io, pl.BlockSpec((1, hb, RN, RN), lambda g, c: (cmap(g, c)[0], g, 0, 0)), io],
        out_shape=[jax.ShapeDtypeStruct((T, RW), F32), jax.ShapeDtypeStruct((NCH, RH, RN, RN), F32),
                   jax.ShapeDtypeStruct((T, RW), F32)],
        scratch_shapes=[pltpu.VMEM((hb, RN, RN), F32)],
        compiler_params=pltpu.CompilerParams(dimension_semantics=("parallel", "arbitrary"), vmem_limit_bytes=VMEM_BIG),
    )(r, lw, k, v, a, b)


def scan_bwd(name, r, lw, k, v, a, b, h0, u, dy, reverse):
    hb = _HB_B
    cmap = _chunk_map(reverse, True)

    def body(r_ref, lw_ref, k_ref, v_ref, a_ref, b_ref, h0_ref, u_ref, dy_ref, *rest):
        d_refs, dht_ref = rest[:6], rest[6]

        @pl.when(pl.program_id(1) == 0)
        def _():
            dht_ref[...] = jnp.zeros_like(dht_ref)

        ins = [_split_heads(x[...]) for x in (r_ref, lw_ref, k_ref, v_ref, a_ref, b_ref)]
        _, vjp = jax.vjp(functools.partial(_chunk, reverse=reverse), *ins, h0_ref[0], _split_heads(u_ref[...]))
        dy = _split_heads(dy_ref[...])
        grads = vjp((dy, dht_ref[...], jnp.zeros_like(dy)))
        for d_ref, gval in zip(d_refs, grads[:6]):
            d_ref[...] = _merge_heads(gval).astype(d_ref.dtype)
        dht_ref[...] = grads[6]

    io = pl.BlockSpec((CHUNK, hb * RN), cmap)
    return pl.pallas_call(
        body, name=name, grid=(RH // hb, NCH),
        in_specs=[io] * 6 + [pl.BlockSpec((1, hb, RN, RN), lambda g, c: (cmap(g, c)[0], g, 0, 0)), io, io],
        out_specs=[io] * 6,
        out_shape=[jax.ShapeDtypeStruct((T, RW), F32 if i == 1 else BF16) for i in range(6)],
        scratch_shapes=[pltpu.VMEM((hb, RN, RN), F32)],
        compiler_params=pltpu.CompilerParams(dimension_semantics=("parallel", "arbitrary"), vmem_limit_bytes=VMEM_BIG),
    )(r, lw, k, v, a, b, h0, u, dy)


def loss_stage(out, x2, tgt, g_post):
    tr = 256

    def body(o_ref, x_ref, t_ref, g_ref, do_ref, dy_ref, dg_ref, loss_ref):
        @pl.when(pl.program_id(0) == 0)
        def _():
            dg_ref[...] = jnp.zeros_like(dg_ref)
            loss_ref[...] = jnp.zeros_like(loss_ref)

        nrm, vjp = jax.vjp(_rms, o_ref[...], g_ref[...])
        e = x_ref[...] + nrm - t_ref[...]
        s = jnp.sum(jnp.sum(e * e, axis=1, keepdims=True), axis=0, keepdims=True)
        loss_ref[...] += jnp.broadcast_to(s * (0.5 / D), loss_ref.shape)
        dy = e * (1.0 / D)
        do, dg = vjp(dy)
        do_ref[...] = do.astype(do_ref.dtype)
        dy_ref[...] = dy
        dg_ref[...] += dg

    row = pl.BlockSpec((tr, D), lambda i: (i, 0))
    return pl.pallas_call(
        body, name="loss_stage", grid=(T // tr,),
        in_specs=[row, row, row, pl.BlockSpec((1, D), lambda i: (0, 0))],
        out_specs=[row, row, pl.BlockSpec((1, D), lambda i: (0, 0)), pl.BlockSpec((8, LANE), lambda i: (0, 0))],
        out_shape=[jax.ShapeDtypeStruct((T, D), BF16), jax.ShapeDtypeStruct((T, D), F32),
                   jax.ShapeDtypeStruct((1, D), F32), jax.ShapeDtypeStruct((8, LANE), F32)],
        compiler_params=pltpu.CompilerParams(dimension_semantics=("arbitrary",), vmem_limit_bytes=VMEM_BIG),
    )(out, x2, tgt, g_post)


_EW_BLOCK_BYTES = 1 << 20


def _row_tile(rows, cols):
    best = None
    for tr in range(16, rows + 1, 16):
        if rows % tr == 0 and tr * cols * 4 <= _EW_BLOCK_BYTES:
            best = tr
    return best or rows


def _axis_tile(shape, axis, words):
    rows, cols = shape
    n, other, unit = (rows, cols, 16) if axis == 0 else (cols, rows, LANE)
    best = unit if n % unit == 0 else n
    for t in range(unit, n + 1, unit):
        if n % t == 0 and t * other * words * 4 <= _EW_BLOCK_BYTES:
            best = t
    blk = (best, cols) if axis == 0 else (rows, best)
    at = (lambda s: (s, 0)) if axis == 0 else (lambda s: (0, s))
    return blk, n // best, at


def _adamw_update(g, w_ref, m_ref, v_ref, g_ref, d_ref, nm_ref, nv_ref):
    mm = ADAM_B1 * m_ref[...] + (1.0 - ADAM_B1) * g
    vv = ADAM_B2 * v_ref[...] + (1.0 - ADAM_B2) * (g * g)
    m_hat = mm / (1.0 - ADAM_B1 ** ADAM_STEP)
    v_hat = vv / (1.0 - ADAM_B2 ** ADAM_STEP)
    g_ref[...] = g
    d_ref[...] = -ADAM_LR * (m_hat / (jnp.sqrt(v_hat) + ADAM_EPS) + ADAM_WD * w_ref[...])
    nm_ref[...] = mm
    nv_ref[...] = vv


def adamw(name, w, m, v, parts):
    rows, cols = w.shape
    br = _row_tile(rows, cols)
    npart = len(parts)

    def body(w_ref, m_ref, v_ref, *rest):
        g = rest[0][...].astype(F32)
        for p in rest[1:npart]:
            g = g + p[...].astype(F32)
        _adamw_update(g, w_ref, m_ref, v_ref, *rest[npart:])

    blk = pl.BlockSpec((br, cols), lambda i: (i, 0))
    return pl.pallas_call(
        body, name=name, grid=(rows // br,),
        in_specs=[blk] * (3 + npart), out_specs=[blk] * 4,
        out_shape=[jax.ShapeDtypeStruct((rows, cols), F32)] * 4,
        compiler_params=pltpu.CompilerParams(dimension_semantics=("parallel",), vmem_limit_bytes=VMEM_BIG),
    )(w, m, v, *parts)


def adamw_halves(name, place, w, m, v, mine, theirs, axis):
    half_shape = mine.shape
    blk_shape, nb, at = _axis_tile(half_shape, axis, 1)

    def body(p_ref, w_ref, m_ref, v_ref, a_ref, b_ref, *outs):
        own = (pl.program_id(0) // nb) == p_ref[0]
        _adamw_update(jnp.where(own, a_ref[...], b_ref[...]), w_ref, m_ref, v_ref, *outs)

    blk = pl.BlockSpec(blk_shape, lambda i, p: at(i))
    half = pl.BlockSpec(blk_shape, lambda i, p: at(i % nb))
    return pl.pallas_call(
        body, name=name,
        grid_spec=pltpu.PrefetchScalarGridSpec(num_scalar_prefetch=1, grid=(2 * nb,),
                                               in_specs=[blk] * 3 + [half] * 2, out_specs=[blk] * 4),
        out_shape=[jax.ShapeDtypeStruct(w.shape, F32)] * 4,
        compiler_params=pltpu.CompilerParams(dimension_semantics=("arbitrary",), vmem_limit_bytes=VMEM_BIG),
    )(place, w, m, v, mine, theirs)


def pair_sum(name, place, send, other, axis):
    blk_shape, nb, at = _axis_tile(other.shape[1:], axis, 4)

    def body(p_ref, a_ref, b_ref, o_ref):
        o_ref[...] = (a_ref[...].astype(F32) + b_ref[...].astype(F32)).astype(o_ref.dtype)

    blk = pl.BlockSpec((4,) + blk_shape, lambda i, p: (0,) + at(i))
    mine = pl.BlockSpec((4,) + blk_shape, lambda i, p: (0,) + at(p[0] * nb + i))
    return pl.pallas_call(
        body, name=name,
        grid_spec=pltpu.PrefetchScalarGridSpec(num_scalar_prefetch=1, grid=(nb,), in_specs=[mine, blk], out_specs=blk),
        out_shape=jax.ShapeDtypeStruct(other.shape, BF16),
        compiler_params=pltpu.CompilerParams(dimension_semantics=("arbitrary",), vmem_limit_bytes=VMEM_BIG),
    )(place, send, other)


def sum4(name, place, recv, own, axis):
    blk_shape, nb, at = _axis_tile(recv.shape[1:], axis, 4)

    def body(p_ref, r_ref, s_ref, o_ref):
        me = p_ref[0]
        t = [jnp.where(me == j, s_ref[j], r_ref[j]).astype(F32) for j in range(4)]
        o_ref[...] = ((t[0] + t[1]) + t[2]) + t[3]

    blk = pl.BlockSpec((4,) + blk_shape, lambda i, p: (0,) + at(i))
    return pl.pallas_call(
        body, name=name,
        grid_spec=pltpu.PrefetchScalarGridSpec(num_scalar_prefetch=1, grid=(nb,), in_specs=[blk, blk],
                                               out_specs=pl.BlockSpec(blk_shape, lambda i, p: at(i))),
        out_shape=jax.ShapeDtypeStruct(recv.shape[1:], F32),
        compiler_params=pltpu.CompilerParams(dimension_semantics=("arbitrary",), vmem_limit_bytes=VMEM_BIG),
    )(place, recv, own)


_ANY = pl.BlockSpec(memory_space=pl.ANY)


def _place():
    x, y, c = lax.axis_index("x"), lax.axis_index("y"), lax.axis_index("c")
    return x, y, c, 2 * x + y


def _chip_peers(x, y):
    out = []
    for k in (1, 2, 3):
        px = 1 - x if k & 2 else x
        py = 1 - y if k & 1 else y
        out.append((k, px, py, 2 * px + py))
    return out


def _half(c, shape, axis):
    n = shape[axis] // 2
    sl = pl.ds(pl.multiple_of(c * n, 16 if axis == 0 else LANE), n)
    return (sl,) if axis == 0 else (pl.ds(0, shape[0]), sl)


def gather_weights(srcs, axes):
    side = gather_side(srcs, axes)
    n = len(srcs)

    def body(*refs):
        ins, outs, sems = refs[:n], refs[n:2 * n], refs[2 * n:]
        side.run(ins, outs, sems)

    return pl.pallas_call(
        body, name="gather_weights", in_specs=[_ANY] * n, out_specs=[_ANY] * n,
        out_shape=side.outs, scratch_shapes=side.sems,
    )(*srcs)


def gather_side(srcs, axes):
    n = len(srcs)

    def copies(src, dst, sems, want):
        ssem, rsem, fssem, frsem = sems
        x, y, c, me = _place()
        sib = (x, y, 1 - c)
        out = []
        for i in range(n):
            mine, other = _half(c, srcs[i].shape, axes[i]), _half(1 - c, srcs[i].shape, axes[i])
            for k, px, py, peer in _chip_peers(x, y):
                sems_k = dict(send_sem=ssem.at[i, k - 1], recv_sem=rsem.at[i, k - 1], device_id=(px, py, c),
                              device_id_type=MESH_IDS)
                fsems = dict(send_sem=fssem.at[i, k - 1], recv_sem=frsem.at[i, k - 1], device_id=sib,
                             device_id_type=MESH_IDS)
                got = dst[i].at[(peer,) + mine]
                mk = pltpu.make_async_remote_copy
                made = dict(
                    snd=lambda: mk(src_ref=src[i].at[mine], dst_ref=dst[i].at[(me,) + mine], **sems_k),
                    rcv=lambda: mk(src_ref=src[i].at[mine], dst_ref=got, **sems_k),
                    fwd=lambda: mk(src_ref=got, dst_ref=got, **fsems),
                    frcv=lambda: mk(src_ref=got, dst_ref=dst[i].at[(peer,) + other], **fsems))
                out.append([made[w]() for w in want])
        return out

    def start(src, dst, sems):
        for (snd,) in copies(src, dst, sems, ("snd",)):
            snd.start()

    def finish(src, dst, sems):
        cps = copies(src, dst, sems, ("snd", "rcv", "fwd", "frcv"))
        for _, rcv, fwd, _ in cps:
            rcv.wait_recv()
            fwd.start()
        for snd, _, fwd, frcv in cps:
            frcv.wait_recv()
            snd.wait_send()
            fwd.wait_send()

    return Side(list(srcs), [jax.ShapeDtypeStruct((4,) + s.shape, s.dtype) for s in srcs],
                [pltpu.SemaphoreType.DMA((n, 3))] * 4, start, finish)


def pair_exchange(name, srcs, axes):
    n = len(srcs)

    def half_shape(s, axis):
        return (4, s.shape[1] // 2, s.shape[2]) if axis == 0 else (4, s.shape[1], s.shape[2] // 2)

    def body(*refs):
        src, other = refs[:n], refs[n:2 * n]
        ssem, rsem = refs[2 * n:]
        x, y, c, _ = _place()
        cps = []
        for i in range(n):
            idx = (pl.ds(0, 4),) + _half(1 - c, srcs[i].shape[1:], axes[i])
            cps.append(pltpu.make_async_remote_copy(
                src_ref=src[i].at[idx], dst_ref=other[i], send_sem=ssem.at[i], recv_sem=rsem.at[i],
                device_id=(x, y, 1 - c), device_id_type=MESH_IDS))
            cps[-1].start()
        for cp in cps:
            cp.wait()

    return pl.pallas_call(
        body, name=name, in_specs=[_ANY] * n, out_specs=[_ANY] * n,
        out_shape=[jax.ShapeDtypeStruct(half_shape(s, a), s.dtype) for s, a in zip(srcs, axes)],
        scratch_shapes=[pltpu.SemaphoreType.DMA((n,))] * 2,
    )(*srcs)


def scatter_side(srcs):
    n = len(srcs)

    def copies(src, dst, sems, sends_only=False):
        ssem, rsem = sems
        x, y, c, me = _place()
        out = []
        for i in range(n):
            for k, px, py, peer in _chip_peers(x, y):
                sems_k = dict(send_sem=ssem.at[i, k - 1], recv_sem=rsem.at[i, k - 1], device_id=(px, py, c),
                              device_id_type=MESH_IDS)
                snd = pltpu.make_async_remote_copy(src_ref=src[i].at[peer], dst_ref=dst[i].at[me], **sems_k)
                if sends_only:
                    out.append(snd)
                    continue
                out.append((snd, pltpu.make_async_remote_copy(src_ref=src[i].at[peer], dst_ref=dst[i].at[peer],
                                                              **sems_k)))
        return out

    def start(src, dst, sems):
        for snd in copies(src, dst, sems, sends_only=True):
            snd.start()

    def finish(src, dst, sems):
        for snd, rcv in copies(src, dst, sems):
            rcv.wait_recv()
            snd.wait_send()

    return Side(list(srcs), [jax.ShapeDtypeStruct(s.shape, s.dtype) for s in srcs],
                [pltpu.SemaphoreType.DMA((n, 3))] * 2, start, finish)


_HBM = pl.BlockSpec(memory_space=pltpu.HBM)
_SEM = pl.BlockSpec(memory_space=pltpu.SEMAPHORE)
_DATAFLOW = pltpu.SideEffectType.DATAFLOW_SIDE_EFFECTING


def scatter_start(name, srcs):
    n = len(srcs)
    side = scatter_side(srcs)
    ns = 3 * n

    def body(*refs):
        src, land = refs[:n], refs[n:2 * n]
        sems = refs[2 * n:2 * n + 2 * ns]
        side.start(src, land, (_SemGrid(sems[:ns]), _SemGrid(sems[ns:])))
        refs[-1][...] = jnp.zeros_like(refs[-1])

    hbm = [pltpu.HBM(s.shape, s.dtype) for s in srcs]
    res = pl.pallas_call(
        body, name=name,
        out_shape=[pltpu.SemaphoreType.DMA(())] * (2 * ns) + hbm + hbm + [jax.ShapeDtypeStruct((8, LANE), F32)],
        in_specs=[_HBM] * (2 * n),
        out_specs=[_SEM] * (2 * ns) + [_HBM] * (2 * n) + [pl.BlockSpec(memory_space=pltpu.VMEM)],
        input_output_aliases={i: 2 * ns + i for i in range(2 * n)},
        compiler_params=pltpu.CompilerParams(has_side_effects=_DATAFLOW),
    )(*[pltpu.with_memory_space_constraint(s, pltpu.HBM) for s in srcs],
      *[pltpu.with_memory_space_constraint(lax.empty(s.shape, s.dtype), pltpu.HBM) for s in srcs])
    return res[:2 * ns], res[2 * ns:2 * ns + n], res[2 * ns + n:2 * ns + 2 * n], res[-1]


def scatter_wait(name, sems, srcs, lands, after):
    n = len(srcs)
    side = scatter_side(srcs)
    ns = 3 * n

    def body(*refs):
        src, land = refs[:n], refs[n:2 * n]
        s = refs[2 * n:2 * n + 2 * ns]
        side.finish(src, land, (_SemGrid(s[:ns]), _SemGrid(s[ns:])))

    hbm = [pltpu.HBM(s.shape, s.dtype) for s in srcs]
    res = pl.pallas_call(
        body, name=name, out_shape=hbm + hbm,
        in_specs=[_HBM] * (2 * n) + [_SEM] * (2 * ns) + [_ANY], out_specs=[_HBM] * (2 * n),
        input_output_aliases={i: i for i in range(2 * n)},
        compiler_params=pltpu.CompilerParams(has_side_effects=_DATAFLOW),
    )(*srcs, *lands, *sems, after)
    return res[:n], res[n:]


class _SemGrid:
    def __init__(self, sems):
        self.sems = sems

    @property
    def at(self):
        return self

    def __getitem__(self, ik):
        return self.sems[3 * ik[0] + ik[1]]


def swap_halves(name, srcs):
    n = len(srcs)

    def body(*refs):
        src, dst = refs[:n], refs[n:2 * n]
        ssem, rsem = refs[2 * n:]
        x, y, c, _ = _place()
        cps = []
        for i in range(n):
            cps.append(pltpu.make_async_remote_copy(src_ref=src[i], dst_ref=dst[i], send_sem=ssem.at[i],
                                                    recv_sem=rsem.at[i], device_id=(x, y, 1 - c),
                                                    device_id_type=MESH_IDS))
            cps[-1].start()
        for cp in cps:
            cp.wait()

    return pl.pallas_call(
        body, name=name, in_specs=[_ANY] * n, out_specs=[_ANY] * n,
        out_shape=[jax.ShapeDtypeStruct(s.shape, s.dtype) for s in srcs],
        scratch_shapes=[pltpu.SemaphoreType.DMA((n,))] * 2,
    )(*srcs)


def _ag8_copies(src, dst, sems, sends_only=False):
    x, y, c = lax.axis_index("x"), lax.axis_index("y"), lax.axis_index("c")
    me = 4 * x + 2 * y + c
    out = []
    for k in range(1, 8):
        px = 1 - x if k & 4 else x
        py = 1 - y if k & 2 else y
        pc = 1 - c if k & 1 else c
        peer = 4 * px + 2 * py + pc
        out.append(tuple(pltpu.make_async_remote_copy(
            src_ref=src, dst_ref=dst.at[slot], send_sem=sems[k - 1], recv_sem=sems[7 + k - 1],
            device_id=(px, py, pc), device_id_type=MESH_IDS) for slot in ((me,) if sends_only else (me, peer))))
    return out


def allgather8_start(name, src):
    def body(src_ref, land_ref, *rest):
        for (snd,) in _ag8_copies(src_ref, land_ref, rest[:14], sends_only=True):
            snd.start()

    land = jax.ShapeDtypeStruct((8,) + src.shape, src.dtype)
    res = pl.pallas_call(
        body, name=name,
        out_shape=[pltpu.SemaphoreType.DMA(())] * 14 + [pltpu.HBM(src.shape, src.dtype), pltpu.HBM(land.shape, land.dtype)],
        in_specs=[_HBM, _HBM], out_specs=[_SEM] * 14 + [_HBM, _HBM],
        input_output_aliases={0: 14, 1: 15},
        compiler_params=pltpu.CompilerParams(has_side_effects=_DATAFLOW),
    )(pltpu.with_memory_space_constraint(src, pltpu.HBM),
      pltpu.with_memory_space_constraint(lax.empty(land.shape, land.dtype), pltpu.HBM))
    return res[:14], res[14], res[15]


def allgather8_wait(name, sems, src, land, after):
    def body(src_ref, land_ref, *rest):
        for snd, rcv in _ag8_copies(src_ref, land_ref, rest[:14]):
            rcv.wait_recv()
            snd.wait_send()

    return pl.pallas_call(
        body, name=name, out_shape=[pltpu.HBM(src.shape, src.dtype), pltpu.HBM(land.shape, land.dtype)],
        in_specs=[_HBM, _HBM] + [_SEM] * 14 + [_ANY], out_specs=[_HBM, _HBM],
        input_output_aliases={0: 0, 1: 1},
        compiler_params=pltpu.CompilerParams(has_side_effects=_DATAFLOW),
    )(src, land, *sems, after)


WEIGHTS = ['g_pre', 'w_in', 'mla_q_norm', 'mla_wq_b', 'mla_kv_norm', 'mla_wkv_b', 'rwkv_mu', 'rwkv_w0_f', 'rwkv_w2_f',
           'rwkv_w0_b', 'rwkv_w2_b', 'rwkv_a0_f', 'rwkv_a2_f', 'rwkv_a0_b', 'rwkv_a2_b', 'rwkv_k_k', 'rwkv_k_a',
           'rwkv_r_k', 'rwkv_gn_g', 'rwkv_gn_b', 'w_br_mla', 'w_br_rwkv', 'w_out', 'g_post']
BIG_SHAPES = {'w_in': (D_IN // 4, D), 'mla_wq_b': (Q_RANK, 384), 'mla_wkv_b': (KV_RANK, 512),
              'rwkv_w2_f': (LORA, 256), 'rwkv_w2_b': (LORA, 256), 'rwkv_a2_f': (LORA, 256), 'rwkv_a2_b': (LORA, 256),
              'w_br_mla': (RW, 512), 'w_br_rwkv': (RW, 512), 'w_out': (512, D)}
BIG = list(BIG_SHAPES)
SMALL = [n for n in WEIGHTS if n not in BIG_SHAPES]
SMALL_SHAPES = {'g_pre': (D,), 'mla_q_norm': (Q_RANK,), 'mla_kv_norm': (KV_RANK,), 'rwkv_mu': (3456,),
                'rwkv_w0_f': (RW,), 'rwkv_w0_b': (RW,), 'rwkv_a0_f': (RW,), 'rwkv_a0_b': (RW,), 'rwkv_k_k': (RW,),
                'rwkv_k_a': (RW,), 'rwkv_r_k': (RH, RN), 'rwkv_gn_g': (RW,), 'rwkv_gn_b': (RW,), 'g_post': (D,)}
SMALL_LEN = sum(int(np.prod(s)) for s in SMALL_SHAPES.values())
SMALL_ROWS = 144


UNITS = [('w_in',), ('mla_wq_b',), ('mla_wkv_b',), ('rwkv_w2_f', 'rwkv_w2_b', 'rwkv_a2_f', 'rwkv_a2_b'),
         ('w_br_mla', 'w_br_rwkv'), ('w_out',)]
UNIT_AXIS = [1, 0, 0, 0, 0, 0]
ROW_SHARDED = ('w_in', 'w_out')


def _unit_cat(parts):
    return parts[0] if len(parts) == 1 else jnp.concatenate(parts, axis=0)


def _unit_split(arr, names, axis):
    out, o = {}, 0
    for n in names:
        rows = BIG_SHAPES[n][0]
        out[n] = lax.slice_in_dim(arr, o, o + rows, axis=axis)
        o += rows
    return out


def _gathered(units, ag, own, me):
    out = {}
    for names, arr, mine in zip(units, ag, own):
        slots = [jnp.where(me == j, mine, arr[j]) for j in range(4)]
        for n in names:
            parts = [_unit_split(s, names, 0)[n] for s in slots]
            out[n] = jnp.concatenate(parts, axis=0 if n in ROW_SHARDED else 1)
    return out


def _shards(n, g):
    r, w = BIG_SHAPES[n]
    if n in ROW_SHARDED:
        return [g[j * r:(j + 1) * r] for j in range(4)]
    return [g[:, j * w:(j + 1) * w] for j in range(4)]


def _pack_small(d, extra=None):
    flat = jnp.concatenate([d[n].reshape(-1) for n in SMALL] + ([extra.reshape(-1)] if extra is not None else []))
    return jnp.pad(flat, (0, SMALL_ROWS * LANE - flat.shape[0])).reshape(SMALL_ROWS, LANE)


def _unpack_small(packed):
    flat, out, o = packed.reshape(-1), {}, 0
    for n in SMALL:
        sz = int(np.prod(SMALL_SHAPES[n]))
        out[n] = flat[o:o + sz].reshape(SMALL_SHAPES[n])
        o += sz
    return out


def _perm_w_in(gathered, own, me):
    per = D_IN // 4

    def rows(a, b):
        out = []
        while a < b:
            j, lo = divmod(a, per)
            hi = min(b - j * per, per)
            out.append(jnp.where(me == j, own[lo:hi], gathered[j, lo:hi]))
            a = j * per + hi
        return out

    z = lambda n: [jnp.zeros((n, own.shape[1]), own.dtype)]
    lora = []
    for i in range(4):
        lora += rows(4160 + LORA * i, 4160 + LORA * (i + 1)) + z(LANE - LORA)
    return jnp.concatenate(rows(0, 1024) + rows(1088, 4160) + rows(4544, D_IN) + lora + rows(1024, 1088)
                           + z(256 - ROPE), axis=0)


def _unperm_w_in(g):
    lora = [g[OFF_LORA + LANE * i:OFF_LORA + LANE * i + LORA] for i in range(4)]
    return jnp.concatenate([g[0:1024], g[OFF_KR:OFF_KR + ROPE], g[1024:4096]] + lora + [g[4096:OFF_LORA]], axis=0)


def _perm_wq(w):
    w3 = w.reshape(Q_RANK, HEADS, NOPE + ROPE)
    rope = jnp.pad(w3[:, :, NOPE:], ((0, 0), (0, 0), (0, LANE - ROPE)))
    return jnp.concatenate([w3[:, :, :NOPE].reshape(Q_RANK, -1), rope.reshape(Q_RANK, -1)], axis=1)


def _unperm_wq(g):
    return jnp.concatenate([g[:, :1024].reshape(Q_RANK, HEADS, NOPE),
                            g[:, 1024:].reshape(Q_RANK, HEADS, LANE)[:, :, :ROPE]], axis=2).reshape(Q_RANK, -1)


def _perm_wkv(w):
    w3 = w.reshape(KV_RANK, HEADS, NOPE + VDIM)
    return jnp.concatenate([w3[:, :, :NOPE].reshape(KV_RANK, -1), w3[:, :, NOPE:].reshape(KV_RANK, -1)], axis=1)


def _unperm_wkv(g):
    return jnp.concatenate([g[:, :1024].reshape(KV_RANK, HEADS, NOPE), g[:, 1024:].reshape(KV_RANK, HEADS, VDIM)],
                           axis=2).reshape(KV_RANK, -1)


def _pad_rows(w):
    return jnp.pad(w, ((0, LANE - LORA), (0, 0)))


def _perm_mu(mu):
    parts = [mu[:3072]]
    for i in range(4):
        parts += [mu[3072 + LORA * i:3072 + LORA * (i + 1)], jnp.zeros((LANE - LORA,), mu.dtype)]
    return jnp.concatenate(parts).reshape(1, NLERP)


def _unperm_mu(g):
    g = g.reshape(-1)
    return jnp.concatenate([g[:3072]] + [g[3072 + LANE * i:3072 + LANE * i + LORA] for i in range(4)])


def _constants():
    g2 = np.kron(np.eye(2, dtype=np.float32), np.ones((RN, RN), np.float32))
    pos = jnp.arange(T, dtype=F32)
    inv_freq = jnp.power(ROPE_THETA, -jnp.arange(0, ROPE, 2, dtype=F32) / ROPE)
    ang = pos[:, None] * inv_freq[None, :]
    cos, sin, zero = jnp.cos(ang), jnp.sin(ang), jnp.zeros((T, LANE - ROPE), F32)
    cq = jnp.tile(jnp.concatenate([cos, cos, zero], axis=1), (1, HEADS))
    sq = jnp.tile(jnp.concatenate([-sin, sin, zero], axis=1), (1, HEADS))
    return jnp.asarray(g2, BF16), cq, sq


def _step(x, tgt, w, m, v):
    x2, tgt2 = x.reshape(T, D), tgt.reshape(T, D)
    g2, cq, sq = _constants()
    row = lambda n: w[n].reshape(1, -1)
    w, m, v = ({**t, 'w_in': t['w_in'].T} for t in (w, m, v))

    core, chip = lax.axis_index("c"), 2 * lax.axis_index("x") + lax.axis_index("y")
    core1, chip1 = core.astype(jnp.int32).reshape(1), chip.astype(jnp.int32).reshape(1)
    own_bf = [_unit_cat([w[n].astype(BF16) for n in u]) for u in UNITS]
    wp = _perm_w_in(gather_weights(own_bf[:1], UNIT_AXIS[:1])[0], own_bf[0], chip)
    full = {}
    mu_p = _perm_mu(w['rwkv_mu'])

    st_pre = Stage("pre", f_pre, [(D, BF16), (D, None)], 256, [0], [0], [F32])
    st_mla = Stage("mla", f_mla, [(1024, BF16), (1024, BF16), (1024, BF16), (LANE, BF16), (1024, BF16)], 256,
                   [0, 1, 2], [0, 1, 2, 3], [BF16] * 3)
    st_rpre = Stage("rwkv_pre", f_rwkv_pre, [(RW, F32)] * 9, 256, [0], list(range(10)), [F32])
    st_rpost = Stage("rwkv_post", f_rwkv_post, [(RW, BF16)], 256, [0, 2, 3, 4, 5, 6], [0, 1, 2],
                     [F32, F32, F32, F32, F32, BF16])
    st_gate = Stage("gate", f_gate, [(RW, BF16)], 256, [0, 1], [], [F32, BF16])
    st_merge = Stage("merge", f_merge, [(D, BF16)], 256, [0, 1, 2, 3], [], [BF16] * 4)

    pre_rows, pre_par = [(x2, D, 0)], [row('g_pre')]
    (h,) = st_pre.fwd(pre_rows, pre_par)
    proj, rest = matmul("mm_in", h, wp, "nt", side=gather_side(own_bf[1:], UNIT_AXIS[1:]))
    full.update(_gathered(UNITS[1:], rest, own_bf[1:], chip))
    wq, wkv = _perm_wq(full['mla_wq_b']), _perm_wkv(full['mla_wkv_b'])
    lora_w = [_pad_rows(full[n]).astype(F32) for n in ('rwkv_w2_f', 'rwkv_w2_b', 'rwkv_a2_f', 'rwkv_a2_b')]

    mla_rows = [(proj, 512, OFF_QA // 512), (proj, 512, OFF_KVA // 512), (proj, 256, OFF_KR // 256),
                (cq, 1024, 0), (sq, 1024, 0), (cq, LANE, 0), (sq, LANE, 0)]
    mla_par = [row('mla_q_norm'), row('mla_kv_norm'), wq, wkv]
    att = st_mla.fwd(mla_rows, mla_par)
    y_mla, lse = attn_fwd(*att)

    lerp = shift_fwd(proj, mu_p)
    rpre_rows = [(lerp, NLERP, 0)]
    rpre_par = [row('rwkv_w0_f'), row('rwkv_w0_b'), row('rwkv_a0_f'), row('rwkv_a0_b'), row('rwkv_k_k'),
                row('rwkv_k_a')] + lora_w + [g2]
    r_, v_, lwf, lwb, kf, kb, an, bf_, bb_ = st_rpre.fwd(rpre_rows, rpre_par)
    fin = [r_, lwf, kf, v_, an, bf_]
    bin_ = [r_, lwb, kb, v_, an, bb_]
    yf, h0f, uf = scan_fwd("scan_f", *fin, reverse=False)
    yb, h0b, ub = scan_fwd("scan_b", *bin_, reverse=True)
    rpost_rows = [(yf, RW, 0), (yb, RW, 0), (r_, RW, 0), (kf, RW, 0), (kb, RW, 0),
                  (v_, RW, 0), (proj, RW, OFF_ZR // RW)]
    rpost_par = [row('rwkv_gn_g'), row('rwkv_gn_b'), row('rwkv_r_k'), g2]
    (gr,) = st_rpost.fwd(rpost_rows, rpost_par)
    gate_rows = [(y_mla, RW, 0), (proj, RW, OFF_ZM // RW)]
    (gm,) = st_gate.fwd(gate_rows, [])
    um = matmul("mm_br_mla", gm, full['w_br_mla'], "nn")
    ur = matmul("mm_br_rwkv", gr, full['w_br_rwkv'], "nn")
    merge_rows = [(um, D, 0), (ur, D, 0), (proj, D, OFF_GM // D), (proj, D, OFF_GR // D)]
    (merged,) = st_merge.fwd(merge_rows, [])
    out = matmul("mm_out", merged, full['w_out'], "nn")
    d_out, dy, dg_post, loss_blk = loss_stage(out, x2, tgt2, row('g_post'))

    gw = {'g_post': dg_post}
    d_merged = matmul("mm_out_dx", d_out, full['w_out'], "nt")
    gw['w_out'] = matmul("mm_out_dw", merged, d_out, "tn")
    (d_um, d_ur, d_gm, d_gr), _ = st_merge.bwd(merge_rows, [], [[(d_merged, D, 0)]])
    d_gmla = matmul("mm_br_mla_dx", d_um, full['w_br_mla'], "nt")
    gw['w_br_mla'] = matmul("mm_br_mla_dw", gm, d_um, "tn")
    d_grw = matmul("mm_br_rwkv_dx", d_ur, full['w_br_rwkv'], "nt")
    gw['w_br_rwkv'] = matmul("mm_br_rwkv_dw", gr, d_ur, "tn")
    (d_ymla, d_zm), _ = st_gate.bwd(gate_rows, [], [[(d_gmla, RW, 0)]])
    (d_y, d_r3, d_kf2, d_kb2, d_v3, d_zr), (gw['rwkv_gn_g'], gw['rwkv_gn_b'], d_rk) = st_rpost.bwd(
        rpost_rows, rpost_par, [[(d_grw, RW, 0)]])
    gw['rwkv_r_k'] = d_rk
    sf = scan_bwd("scan_f_bwd", *fin, h0f, uf, d_y, reverse=False)
    sb = scan_bwd("scan_b_bwd", *bin_, h0b, ub, d_y, reverse=True)
    c = lambda *ts: [(t, RW, 0) for t in ts]
    rpre_cts = [c(sf[0], sb[0], d_r3), c(sf[3], sb[3], d_v3), c(sf[1]), c(sb[1]), c(sf[2], d_kf2), c(sb[2], d_kb2),
                c(sf[4], sb[4]), c(sf[5]), c(sb[5])]
    (d_rin,), rpre_g = st_rpre.bwd(rpre_rows, rpre_par, rpre_cts)
    for n, gval in zip(('rwkv_w0_f', 'rwkv_w0_b', 'rwkv_a0_f', 'rwkv_a0_b', 'rwkv_k_k', 'rwkv_k_a'), rpre_g[:6]):
        gw[n] = gval
    for n, gval in zip(('rwkv_w2_f', 'rwkv_w2_b', 'rwkv_a2_f', 'rwkv_a2_b'), rpre_g[6:]):
        gw[n] = gval[:LORA]
    d_lerp, d_mu = shift_bwd(proj, mu_p, d_rin)
    gw['rwkv_mu'] = _unperm_mu(d_mu)

    mla_cts = [[(t, t.shape[1], 0)] for t in attn_bwd(*att, lse, y_mla, d_ymla)]
    (d_qa, d_kva, d_kr), (gw['mla_q_norm'], gw['mla_kv_norm'], d_wq, d_wkv) = st_mla.bwd(mla_rows, mla_par, mla_cts)
    gw['mla_wq_b'], gw['mla_wkv_b'] = _unperm_wq(d_wq), _unperm_wkv(d_wkv)

    dproj = jnp.concatenate([d_qa, d_kva, d_lerp[:, :3072], d_zm, d_zr, d_gm, d_gr, d_lerp[:, 3072:], d_kr], axis=1)

    def pair_sums(name, ids):
        send = [jnp.stack([_unit_cat([_shards(n, gw[n])[j].astype(BF16) for n in UNITS[i]]) for j in range(4)])
                for i in ids]
        axes = [UNIT_AXIS[i] for i in ids]
        other = pair_exchange(name, send, axes)
        return [pair_sum(f"pair_sum_{i}", core1, s, o, ax) for i, s, o, ax in zip(ids, send, other, axes)]

    late, early = [0], list(range(1, len(UNITS)))
    pairs_e = pair_sums("pair_exchange_rest", early)
    gw_in, recv_e = matmul("mm_in_dw", dproj, h, "tn", BF16, side=scatter_side(pairs_e))
    gw['w_in'] = _unperm_w_in(gw_in)
    pairs_l = pair_sums("pair_exchange_w_in", late)
    sems, src_fly, land_fly, token = scatter_start("scatter_w_in_start", pairs_l)
    dh = matmul("mm_in_dx", dproj, wp, "nn", after=(token,))
    (grad_x,), (gw['g_pre'],) = st_pre.bwd(pre_rows, pre_par, [[(dh, D, 0)], [(dy, D, 0)]])

    big = [dict() for _ in range(4)]

    def update(name, ids, recv, pairs):
        mine = [sum4(f"sum4_{i}", chip1, r, p, UNIT_AXIS[i]) for i, r, p in zip(ids, recv, pairs)]
        theirs = swap_halves(name, mine)
        for i, mi, th in zip(ids, mine, theirs):
            res = adamw_halves(f"adamw_{i}", core1, *[_unit_cat([t[n] for n in UNITS[i]]) for t in (w, m, v)], mi, th,
                               UNIT_AXIS[i])
            for q in range(4):
                big[q].update(_unit_split(res[q], UNITS[i], 0))
        return res

    small_fly = allgather8_start("gather_small_start", _pack_small(gw, loss_blk[0, :1]))
    last = update("swap_halves_rest", early, recv_e, pairs_e)
    own_small, landed = allgather8_wait("gather_small_wait", *small_fly, last[0])
    dev = 2 * chip + core
    parts = [jnp.where(dev == i, own_small, landed[i]) for i in range(8)]
    small = adamw("adamw_small", _pack_small(w), _pack_small(m), _pack_small(v), parts)
    pairs_l, recv_l = scatter_wait("scatter_w_in_wait", sems, src_fly, land_fly, small[0] + last[0][:1, :1])
    update("swap_halves_w_in", late, recv_l, pairs_l)

    outs = []
    for b_d, s_arr in zip(big, small):
        d = {**b_d, **_unpack_small(s_arr)}
        d['w_in'] = d['w_in'].T
        outs.append([d[n] for n in WEIGHTS])
    loss = small[0][SMALL_LEN // LANE, 0]
    return (loss, grad_x.reshape(1, T, D), *outs[0], *outs[1], *outs[2], *outs[3])


def kernel(x, g_pre, w_in, mla_q_norm, mla_wq_b, mla_kv_norm, mla_wkv_b, rwkv_mu, rwkv_w0_f, rwkv_w2_f, rwkv_w0_b, rwkv_w2_b, rwkv_a0_f, rwkv_a2_f, rwkv_a0_b, rwkv_a2_b, rwkv_k_k, rwkv_k_a, rwkv_r_k, rwkv_gn_g, rwkv_gn_b, w_br_mla, w_br_rwkv, w_out, g_post, loss_target, m_g_pre, m_w_in, m_mla_q_norm, m_mla_wq_b, m_mla_kv_norm, m_mla_wkv_b, m_rwkv_mu, m_rwkv_w0_f, m_rwkv_w2_f, m_rwkv_w0_b, m_rwkv_w2_b, m_rwkv_a0_f, m_rwkv_a2_f, m_rwkv_a0_b, m_rwkv_a2_b, m_rwkv_k_k, m_rwkv_k_a, m_rwkv_r_k, m_rwkv_gn_g, m_rwkv_gn_b, m_w_br_mla, m_w_br_rwkv, m_w_out, m_g_post, v_g_pre, v_w_in, v_mla_q_norm, v_mla_wq_b, v_mla_kv_norm, v_mla_wkv_b, v_rwkv_mu, v_rwkv_w0_f, v_rwkv_w2_f, v_rwkv_w0_b, v_rwkv_w2_b, v_rwkv_a0_f, v_rwkv_a2_f, v_rwkv_a0_b, v_rwkv_a2_b, v_rwkv_k_k, v_rwkv_k_a, v_rwkv_r_k, v_rwkv_gn_g, v_rwkv_gn_b, v_w_br_mla, v_w_br_rwkv, v_w_out, v_g_post):
    given = dict(locals())
    w = {n: given[n] for n in WEIGHTS}
    m = {n: given['m_' + n] for n in WEIGHTS}
    v = {n: given['v_' + n] for n in WEIGHTS}
    return _step(x, loss_target, w, m, v)
```

```python
import functools
import math

import numpy as np
import jax
import jax.numpy as jnp
from jax import lax
from jax.experimental import pallas as pl
from jax.experimental.pallas import tpu as pltpu

F32, BF16 = jnp.float32, jnp.bfloat16
MESH_IDS = pl.DeviceIdType.MESH

D = 2048
T = 2048
HEADS = 8
Q_RANK = 512
KV_RANK = 512
NOPE = 128
ROPE = 64
VDIM = 128
RW = 1024
RH = 16
RN = 64
LORA = 96
D_IN = 10688
NORM_EPS = 1e-6
GN_EPS = 64e-5
ROPE_THETA = 10000.0
ADAM_LR, ADAM_B1, ADAM_B2, ADAM_EPS, ADAM_WD, ADAM_STEP = 0.001, 0.9, 0.999, 1e-08, 0.01, 10

LANE = 128
VMEM_BIG = 56 * 2**20

NP = 11008
OFF_QA, OFF_KVA, OFF_RKV, OFF_ZM, OFF_ZR, OFF_GM, OFF_GR, OFF_LORA, OFF_KR = 0, 512, 1024, 4096, 5120, 6144, 8192, 10240, 10752
NLERP = 3584

CHUNK = 64
NCH = T // CHUNK


def _dg(a, b, ca, cb, batch=False, prec=None):
    bd = ((0,), (0,)) if batch else ((), ())
    return lax.dot_general(a, b, (((ca,), (cb,)), bd), precision=prec, preferred_element_type=F32)


@jax.custom_vjp
def bdot(a, b):
    return _dg(a.astype(BF16), b.astype(BF16), 1, 0)


def _bdot_fwd(a, b):
    return bdot(a, b), (a, b)


def _bdot_bwd(res, g):
    a, b = res
    gb = g.astype(BF16)
    da = _dg(gb, b.astype(BF16), 1, 1)
    db = _dg(a.astype(BF16), gb, 0, 0)
    return da.astype(a.dtype), db.astype(b.dtype)


bdot.defvjp(_bdot_fwd, _bdot_bwd)


def _split(x):
    hi = x.astype(BF16)
    lo = (x - hi.astype(F32)).astype(BF16)
    return hi, lo


@jax.custom_vjp
def gsum(x, g2):
    hi, lo = _split(x)
    return _dg(hi, g2, 1, 0) + _dg(lo, g2, 1, 0)


def _gsum_fwd(x, g2):
    return gsum(x, g2), g2


def _gsum_bwd(g2, g):
    hi, lo = _split(g)
    return _dg(hi, g2, 1, 1) + _dg(lo, g2, 1, 1), jnp.zeros_like(g2)


gsum.defvjp(_gsum_fwd, _gsum_bwd)


def headsum(x, g2):
    return jnp.concatenate([gsum(x[:, i * LANE:(i + 1) * LANE], g2) for i in range(x.shape[1] // LANE)], axis=1)


def _terms(x, n):
    out = []
    for i in range(n):
        t = x.astype(BF16)
        out.append(t)
        if i < n - 1:
            x = x - t.astype(F32)
    return out


def _bmm(a, b, ca, cb, na, nb):
    acc = None
    for i, ai in enumerate(_terms(a, na)):
        for j, bj in enumerate(_terms(b, nb)):
            if i + j < max(na, nb):
                p = _dg(ai, bj, ca, cb, True)
                acc = p if acc is None else acc + p
    return acc


_NN, _NT, _TN = (2, 1), (2, 2), (1, 1)


def _make_dots(nf, nb_nn, nb_nt, nb_tn):
    @jax.custom_vjp
    def nn(a, b):
        return _bmm(a, b, *_NN, nf, nf)

    @jax.custom_vjp
    def nt(a, b):
        return _bmm(a, b, *_NT, nf, nf)

    @jax.custom_vjp
    def tn(a, b):
        return _bmm(a, b, *_TN, nf, nf)

    nn.defvjp(lambda a, b: (nn(a, b), (a, b)),
              lambda r, g: (_bmm(g, r[1], *_NT, nb_nn, nb_nn), _bmm(r[0], g, *_TN, nb_nn, nb_nn)))
    nt.defvjp(lambda a, b: (nt(a, b), (a, b)),
              lambda r, g: (_bmm(g, r[1], *_NN, 1, nb_nt), _bmm(g, r[0], *_TN, 1, nb_nt)))
    tn.defvjp(lambda a, b: (tn(a, b), (a, b)),
              lambda r, g: (_bmm(r[1], g, *_NT, nb_tn, nb_tn), _bmm(r[0], g, *_NN, nb_tn, nb_tn)))
    return nn, nt, tn


_SCAN_NF, _SCAN_NB = 1, 1
nn, nt, tn = _make_dots(_SCAN_NF, 1, 2, 1)
_, nt_state, _ = _make_dots(_SCAN_NF, 1, 1, 1)


@jax.custom_vjp
def cumdot(ones, x):
    return _bmm(ones, x, *_NN, 1, 2)


cumdot.defvjp(lambda o, x: (cumdot(o, x), o), lambda o, g: (jnp.zeros_like(o), _bmm(o, g, *_TN, 1, 2)))


def _solve_powers(l):
    pw = [l]
    for _ in range(int(math.log2(l.shape[-1])) - 1):
        pw.append(_bmm(pw[-1], pw[-1], *_NN, _SCAN_NF, _SCAN_NF))
    return pw


@jax.custom_vjp
def tri_solve(l, rhs):
    x = rhs
    for p in _solve_powers(l):
        x = x + _bmm(p, x, *_NN, _SCAN_NF, _SCAN_NF)
    return x


def _tri_solve_fwd(l, rhs):
    pw = _solve_powers(l)
    x = rhs
    for p in pw:
        x = x + _bmm(p, x, *_NN, _SCAN_NF, _SCAN_NF)
    return x, (pw, x)


def _tri_solve_bwd(res, g):
    pw, x = res
    y = g
    for p in pw:
        y = y + _bmm(p, y, *_TN, _SCAN_NB, _SCAN_NB)
    return _bmm(y, x, *_NT, _SCAN_NB, _SCAN_NB), y


tri_solve.defvjp(_tri_solve_fwd, _tri_solve_bwd)


@jax.custom_vjp
def known_solve(l, rhs, x):
    return x


known_solve.defvjp(lambda l, rhs, x: (x, (_solve_powers(l), x)),
                   lambda res, g: _tri_solve_bwd(res, g) + (jnp.zeros_like(g),))


def _rms(x, g):
    return x * lax.rsqrt(jnp.mean(x * x, axis=-1, keepdims=True) + NORM_EPS) * g


def _softplus(x):
    pos = x > 0
    return jnp.where(pos, x, 0.0) + jnp.log(1.0 + jnp.exp(-jnp.where(pos, x, -x)))


def _silu(z):
    return z * jax.nn.sigmoid(z)


_MM_VMEM_BYTES = 32 * 2**20


def _mm_tiles(m, n, k):
    best = None
    for tm in (2048, 1024, 512, 256):
        for tn_ in (2048, 1024, 512, 256):
            for d in range(k // LANE, 0, -1):
                tk = LANE * d
                if m % tm or n % tn_ or k % tk:
                    continue
                nk = k // tk
                vmem = 4 * tk * (tm + tn_) + 8 * tm * tn_ + (4 * tm * tn_ if nk > 1 else 0)
                if vmem > _MM_VMEM_BYTES:
                    continue
                a_reads = n // tn_ if nk > 1 else 1
                b_reads = 1 if (nk == 1 and n == tn_) else m // tm
                acc_rmw = nk * m * n if nk > 1 else 0
                cost = (a_reads * m * k + b_reads * k * n + acc_rmw, -tm * tn_ * tk)
                if best is None or cost < best[0]:
                    best = (cost, (tm, tn_, tk))
    return best[1]


class Side:
    def __init__(self, ins, outs, sems, start, finish):
        self.ins, self.outs, self.sems, self.start, self.finish = ins, outs, sems, start, finish

    def at_step(self, step, steps, *refs):
        @pl.when(step == 0)
        def _():
            self.start(*refs)

    def at_end(self, step, steps, *refs):
        @pl.when(step == steps - 1)
        def _():
            self.finish(*refs)

    def run(self, *refs):
        self.start(*refs)
        self.finish(*refs)


def matmul(name, a, b, mode, out_dtype=F32, side=None, after=()):
    if mode == "nn":
        (m, k), n = a.shape, b.shape[1]
    elif mode == "nt":
        (m, k), n = a.shape, b.shape[0]
    else:
        (k, m), n = a.shape, b.shape[1]
    tm, tn_, tk = _mm_tiles(m, n, k)
    nk = k // tk
    if mode == "nn":
        a_spec = pl.BlockSpec((tm, tk), lambda i, j, kk: (i, kk))
        b_spec = pl.BlockSpec((tk, tn_), lambda i, j, kk: (kk, j))
        ca, cb = 1, 0
    elif mode == "nt":
        a_spec = pl.BlockSpec((tm, tk), lambda i, j, kk: (i, kk))
        b_spec = pl.BlockSpec((tn_, tk), lambda i, j, kk: (j, kk))
        ca, cb = 1, 1
    else:
        a_spec = pl.BlockSpec((tk, tm), lambda i, j, kk: (kk, i))
        b_spec = pl.BlockSpec((tk, tn_), lambda i, j, kk: (kk, j))
        ca, cb = 0, 0

    grid = (m // tm, n // tn_, nk)
    n_in = len(side.ins) if side else 0
    n_out = len(side.outs) if side else 0
    n_dep = len(after)

    def body(a_ref, b_ref, *rest):
        rest = rest[n_dep:]
        s_ins, o_ref, s_outs = rest[:n_in], rest[n_in], rest[n_in + 1:n_in + 1 + n_out]
        scratch = rest[n_in + 1 + n_out:]
        acc, s_sems = (scratch[:1], scratch[1:]) if nk > 1 else ((), scratch)
        steps = grid[0] * grid[1] * grid[2]
        if side:
            step = (pl.program_id(0) * grid[1] + pl.program_id(1)) * grid[2] + pl.program_id(2)
            side.at_step(step, steps, s_ins, s_outs, s_sems)

        part = _dg(a_ref[...].astype(BF16), b_ref[...].astype(BF16), ca, cb)
        if nk == 1:
            o_ref[...] = part.astype(o_ref.dtype)
        else:
            acc_ref, kk = acc[0], pl.program_id(2)

            @pl.when(kk == 0)
            def _():
                acc_ref[...] = part

            @pl.when(kk > 0)
            def _():
                acc_ref[...] += part

            @pl.when(kk == nk - 1)
            def _():
                o_ref[...] = acc_ref[...].astype(o_ref.dtype)

        if side:
            side.at_end(step, steps, s_ins, s_outs, s_sems)

    res = pl.pallas_call(
        body, name=name, grid=grid,
        in_specs=[a_spec, b_spec] + [_ANY] * (n_dep + n_in),
        out_specs=[pl.BlockSpec((tm, tn_), lambda i, j, kk: (i, j))] + [_ANY] * n_out,
        out_shape=[jax.ShapeDtypeStruct((m, n), out_dtype)] + (list(side.outs) if side else []),
        scratch_shapes=([pltpu.VMEM((tm, tn_), F32)] if nk > 1 else []) + (list(side.sems) if side else []),
        compiler_params=pltpu.CompilerParams(
            dimension_semantics=("arbitrary",) * 3 if side else ("parallel", "parallel", "arbitrary"),
            vmem_limit_bytes=VMEM_BIG),
    )(a, b, *after, *(side.ins if side else []))
    return (res[0], res[1:]) if side else res[0]


def _rspec(tr, width, blk):
    return pl.BlockSpec((tr, width), lambda i: (i, blk))


def _full_spec(arr):
    return pl.BlockSpec(arr.shape, lambda i: (0,) * arr.ndim)


class Stage:
    def __init__(self, name, f, outs, tr, diff_rows, diff_params, drow_dtypes):
        self.name, self.f, self.outs, self.tr = name, f, outs, tr
        self.diff_rows, self.diff_params, self.drow_dtypes = diff_rows, diff_params, drow_dtypes

    def fwd(self, rows, params):
        f, nr, npar = self.f, len(rows), len(params)
        stored = [(w, dt) for (w, dt) in self.outs if dt is not None]
        keep = [i for i, (w, dt) in enumerate(self.outs) if dt is not None]

        def body(*refs):
            vals = f(*[r[...].astype(F32) for r in refs[:nr]], *[p[...] for p in refs[nr:nr + npar]])
            for o_ref, i in zip(refs[nr + npar:], keep):
                o_ref[...] = vals[i].astype(o_ref.dtype)

        return pl.pallas_call(
            body, name=self.name + "_fwd", grid=(T // self.tr,),
            in_specs=[_rspec(self.tr, w, b) for (_, w, b) in rows] + [_full_spec(p) for p in params],
            out_specs=[_rspec(self.tr, w, 0) for (w, _) in stored],
            out_shape=[jax.ShapeDtypeStruct((T, w), dt) for (w, dt) in stored],
            compiler_params=pltpu.CompilerParams(dimension_semantics=("arbitrary",), vmem_limit_bytes=VMEM_BIG),
        )(*[r[0] for r in rows], *params)

    def bwd(self, rows, params, cts):
        f, nr, npar = self.f, len(rows), len(params)
        dr_idx, dp_idx = self.diff_rows, self.diff_params
        flat_cts = [c for lst in cts for c in lst]
        nct = len(flat_cts)

        def body(*refs):
            row_refs, par_refs = refs[:nr], refs[nr:nr + npar]
            ct_refs = refs[nr + npar:nr + npar + nct]
            drow_refs = refs[nr + npar + nct:nr + npar + nct + len(dr_idx)]
            dpar_refs = refs[nr + npar + nct + len(dr_idx):]
            row_vals = [r[...].astype(F32) for r in row_refs]
            par_vals = [p[...] for p in par_refs]

            def g(*dv):
                rv, pv = list(row_vals), list(par_vals)
                for j, i in enumerate(dr_idx):
                    rv[i] = dv[j]
                for j, i in enumerate(dp_idx):
                    pv[i] = dv[len(dr_idx) + j]
                return f(*rv, *pv)

            _, vjp = jax.vjp(g, *[row_vals[i] for i in dr_idx], *[par_vals[i] for i in dp_idx])
            ct_vals, pos = [], 0
            for lst in cts:
                acc = ct_refs[pos][...].astype(F32)
                for q in range(1, len(lst)):
                    acc = acc + ct_refs[pos + q][...].astype(F32)
                pos += len(lst)
                ct_vals.append(acc)
            grads = vjp(tuple(ct_vals))
            for j, r in enumerate(drow_refs):
                r[...] = grads[j].astype(r.dtype)

            @pl.when(pl.program_id(0) == 0)
            def _():
                for r in dpar_refs:
                    r[...] = jnp.zeros_like(r)

            for j, r in enumerate(dpar_refs):
                r[...] += grads[len(dr_idx) + j].astype(F32)

        drow_shapes = [jax.ShapeDtypeStruct((T, rows[i][1]), dt) for i, dt in zip(dr_idx, self.drow_dtypes)]
        dpar_shapes = [jax.ShapeDtypeStruct(params[i].shape, F32) for i in dp_idx]
        res = pl.pallas_call(
            body, name=self.name + "_bwd", grid=(T // self.tr,),
            in_specs=[_rspec(self.tr, w, b) for (_, w, b) in rows] + [_full_spec(p) for p in params]
            + [_rspec(self.tr, w, b) for (_, w, b) in flat_cts],
            out_specs=[_rspec(self.tr, rows[i][1], 0) for i in dr_idx] + [_full_spec(params[i]) for i in dp_idx],
            out_shape=drow_shapes + dpar_shapes,
            compiler_params=pltpu.CompilerParams(dimension_semantics=("arbitrary",), vmem_limit_bytes=VMEM_BIG),
        )(*[r[0] for r in rows], *params, *[c[0] for c in flat_cts])
        return res[:len(dr_idx)], res[len(dr_idx):]


def f_pre(x, g):
    return _rms(x, g), x


@jax.custom_vjp
def swap32(t):
    width = t.shape[1]
    lane = lax.broadcasted_iota(jnp.int32, t.shape, 1) % LANE
    return jnp.where(lane < 32, pltpu.roll(t, width - 32, 1), jnp.where(lane < 64, pltpu.roll(t, 32, 1), 0.0))


swap32.defvjp(lambda t: (swap32(t), None), lambda _, g: (swap32(g),))


def f_mla(q_a, kv_a, kr, cq, sq, ck, sk, gq, gkv, wq, wkv):
    q = bdot(_rms(q_a, gq), wq)
    kv = bdot(_rms(kv_a, gkv), wkv)
    t, k = q[:, 1024:], kr[:, :LANE]
    return (q[:, :1024], t * cq + swap32(t) * sq, kv[:, :1024], k * ck + swap32(k) * sk, kv[:, 1024:])


def f_rwkv_pre(lerp, w0f, w0b, a0f, a0b, kkw, kaw, w2f, w2b, a2f, a2b, g2):
    r, k, v = lerp[:, :RW], lerp[:, RW:2 * RW], lerp[:, 2 * RW:3 * RW]
    wdf, wdb, adf, adb = (lerp[:, 3 * RW + i * LANE:3 * RW + (i + 1) * LANE] for i in range(4))

    def logdecay(w0, wd, w2):
        z = w0 + bdot(jnp.tanh(wd), w2)
        return -jnp.exp(-_softplus(-z) - 0.5)

    a_f = jax.nn.sigmoid(a0f + bdot(adf, a2f))
    a_b = jax.nn.sigmoid(a0b + bdot(adb, a2b))
    kk = k * kkw
    kk = kk / jnp.maximum(jnp.sqrt(headsum(kk * kk, g2)), 1e-12)
    return (r, v, logdecay(w0f, wdf, w2f), logdecay(w0b, wdb, w2b),
            k * (1.0 + (a_f - 1.0) * kaw), k * (1.0 + (a_b - 1.0) * kaw), -kk, kk * a_f, kk * a_b)


def f_rwkv_post(yf, yb, r, kf, kb, v, z, gng, gnb, rk, g2):
    y = yf + yb
    mu = headsum(y, g2) * (1.0 / RN)
    d = y - mu
    var = headsum(d * d, g2) * (1.0 / RN)
    yn = d * lax.rsqrt(var + GN_EPS) * gng + gnb
    bonus = headsum(r * (kf + kb) * rk, g2) * v
    return ((yn + bonus) * _silu(z),)


def f_gate(y, z):
    return (y * _silu(z),)


def f_merge(um, ur, gm, gr):
    return (jax.nn.sigmoid(gm) * um + jax.nn.sigmoid(gr) * ur,)


_SHIFT_W = 256


def _lerp_colblock(j):
    return jnp.where(j < 3072 // _SHIFT_W, OFF_RKV // _SHIFT_W + j, OFF_LORA // _SHIFT_W + j - 3072 // _SHIFT_W)


def _nbr_mean(x):
    row = lax.broadcasted_iota(jnp.int32, x.shape, 0)
    up = jnp.where(row == 0, 0.0, pltpu.roll(x, 1, 0))
    dn = jnp.where(row == T - 1, 0.0, pltpu.roll(x, T - 1, 0))
    return 0.5 * (up + dn)


def shift_fwd(proj, mu):
    def body(x_ref, mu_ref, o_ref):
        x = x_ref[...]
        o_ref[...] = x + mu_ref[...] * (_nbr_mean(x) - x)

    return pl.pallas_call(
        body, name="shift_fwd", grid=(NLERP // _SHIFT_W,),
        in_specs=[pl.BlockSpec((T, _SHIFT_W), lambda j: (0, _lerp_colblock(j))),
                  pl.BlockSpec((1, _SHIFT_W), lambda j: (0, j))],
        out_specs=pl.BlockSpec((T, _SHIFT_W), lambda j: (0, j)),
        out_shape=jax.ShapeDtypeStruct((T, NLERP), F32),
        compiler_params=pltpu.CompilerParams(dimension_semantics=("parallel",), vmem_limit_bytes=VMEM_BIG),
    )(proj, mu)


def shift_bwd(proj, mu, g):
    def body(x_ref, mu_ref, g_ref, dx_ref, dmu_ref):
        x, gv = x_ref[...], g_ref[...]
        dmu_ref[...] = jnp.sum(gv * (_nbr_mean(x) - x), axis=0, keepdims=True)
        gm = gv * mu_ref[...]
        dx_ref[...] = (gv - gm + _nbr_mean(gm)).astype(dx_ref.dtype)

    col = pl.BlockSpec((T, _SHIFT_W), lambda j: (0, j))
    vec = pl.BlockSpec((1, _SHIFT_W), lambda j: (0, j))
    return pl.pallas_call(
        body, name="shift_bwd", grid=(NLERP // _SHIFT_W,),
        in_specs=[pl.BlockSpec((T, _SHIFT_W), lambda j: (0, _lerp_colblock(j))), vec, col],
        out_specs=[col, vec],
        out_shape=[jax.ShapeDtypeStruct((T, NLERP), BF16), jax.ShapeDtypeStruct((1, NLERP), F32)],
        compiler_params=pltpu.CompilerParams(dimension_semantics=("parallel",), vmem_limit_bytes=VMEM_BIG),
    )(proj, mu, g)


_TQ_F, _TQ_B = 256, 512
_ATT_SCALE = (NOPE + ROPE) ** -0.5


def _probs(q, k, lse=None):
    s = _dg(q, k, 1, 1) * _ATT_SCALE
    if lse is not None:
        return jnp.exp(s - lse), lse
    m = jnp.max(s, axis=-1, keepdims=True)
    e = jnp.exp(s - m)
    l = jnp.sum(e, axis=-1, keepdims=True)
    return e * (1.0 / l), m + jnp.log(l)


def _q_blk(tq):
    return pl.BlockSpec((tq, LANE), lambda h, i: (i, h))


_K_BLK = pl.BlockSpec((T, LANE), lambda h, i: (0, h))
_KR_BLK = pl.BlockSpec((T, LANE), lambda h, i: (0, 0))


def _load_qk(qn_ref, qr_ref, kn_ref, kr_ref, kcat_ref):
    @pl.when(pl.program_id(1) == 0)
    def _():
        kcat_ref[:, :LANE] = kn_ref[...]
        kcat_ref[:, LANE:] = kr_ref[...]

    return jnp.concatenate([qn_ref[...], qr_ref[...]], axis=1), kcat_ref[...]


def attn_fwd(qn, qr, kn, kr, v, proj):
    def body(qn_ref, qr_ref, kn_ref, kr_ref, v_ref, z_ref, o_ref, lse_ref, g_ref, kcat_ref):
        q, k = _load_qk(qn_ref, qr_ref, kn_ref, kr_ref, kcat_ref)
        p, lse = _probs(q, k)
        o = _dg(p.astype(BF16), v_ref[...], 1, 0)
        o_ref[...] = o
        lse_ref[...] = jnp.broadcast_to(lse, lse_ref.shape)
        g_ref[...] = f_gate(o, z_ref[...])[0].astype(g_ref.dtype)

    wide = jax.ShapeDtypeStruct((T, HEADS * VDIM), F32)
    z_blk = pl.BlockSpec((_TQ_F, LANE), lambda h, i: (i, OFF_ZM // LANE + h))
    return pl.pallas_call(
        body, name="attn_fwd", grid=(HEADS, T // _TQ_F),
        in_specs=[_q_blk(_TQ_F), _q_blk(_TQ_F), _K_BLK, _KR_BLK, _K_BLK, z_blk], out_specs=[_q_blk(_TQ_F)] * 3,
        out_shape=[wide, wide, jax.ShapeDtypeStruct((T, HEADS * VDIM), BF16)],
        scratch_shapes=[pltpu.VMEM((T, 2 * LANE), BF16)],
        compiler_params=pltpu.CompilerParams(dimension_semantics=("arbitrary", "arbitrary"), vmem_limit_bytes=VMEM_BIG),
    )(qn, qr, kn, kr, v, proj)


def attn_bwd(qn, qr, kn, kr, v, lse, o, dg, proj):
    def body(qn_ref, qr_ref, kn_ref, kr_ref, v_ref, lse_ref, o_ref, dg_ref, z_ref, dqn_ref, dqr_ref, dkn_ref,
             dkr_ref, dv_ref, dz_ref, kcat_ref):
        h, i = pl.program_id(0), pl.program_id(1)

        @pl.when(i == 0)
        def _():
            dkn_ref[...] = jnp.zeros_like(dkn_ref)
            dv_ref[...] = jnp.zeros_like(dv_ref)

        @pl.when((i == 0) & (h == 0))
        def _():
            dkr_ref[...] = jnp.zeros_like(dkr_ref)

        q, k = _load_qk(qn_ref, qr_ref, kn_ref, kr_ref, kcat_ref)
        o = o_ref[...]
        _, gate_vjp = jax.vjp(f_gate, o, z_ref[...].astype(F32))
        do_f, dz = gate_vjp((dg_ref[...].astype(F32),))
        dz_ref[...] = dz.astype(dz_ref.dtype)
        dob = do_f.astype(BF16)
        p, _ = _probs(q, k, lse_ref[:, :1])
        dv_ref[...] += _dg(p.astype(BF16), dob, 0, 0)
        dp = _dg(dob, v_ref[...], 1, 1)
        delta = jnp.sum(do_f * o, axis=-1, keepdims=True)
        ds = (p * (dp - delta) * _ATT_SCALE).astype(BF16)
        dq = _dg(ds, k, 1, 0)
        dqn_ref[...] = dq[:, :LANE]
        dqr_ref[...] = dq[:, LANE:]
        dk = _dg(ds, q, 0, 0)
        dkn_ref[...] += dk[:, :LANE]
        dkr_ref[...] += dk[:, LANE:]

    wide = jax.ShapeDtypeStruct((T, HEADS * LANE), F32)
    return pl.pallas_call(
        body, name="attn_bwd", grid=(HEADS, T // _TQ_B),
        in_specs=[_q_blk(_TQ_B), _q_blk(_TQ_B), _K_BLK, _KR_BLK, _K_BLK] + [_q_blk(_TQ_B)] * 3
        + [pl.BlockSpec((_TQ_B, LANE), lambda h, i: (i, OFF_ZM // LANE + h))],
        out_specs=[_q_blk(_TQ_B), _q_blk(_TQ_B), _K_BLK, _KR_BLK, _K_BLK, _q_blk(_TQ_B)],
        out_shape=[wide, wide, wide, jax.ShapeDtypeStruct((T, LANE), F32), wide,
                   jax.ShapeDtypeStruct((T, HEADS * LANE), BF16)],
        scratch_shapes=[pltpu.VMEM((T, 2 * LANE), BF16)],
        compiler_params=pltpu.CompilerParams(dimension_semantics=("arbitrary", "arbitrary"), vmem_limit_bytes=VMEM_BIG),
    )(qn, qr, kn, kr, v, lse, o, dg, proj)


def _chunk(r, lw, k, v, a, b, ht, u_kept=None, *, reverse):
    hb, c, _ = r.shape
    ti = lax.broadcasted_iota(jnp.int32, (c, c), 0)
    si = lax.broadcasted_iota(jnp.int32, (c, c), 1)
    incl = (si >= ti) if reverse else (si <= ti)
    strict = (si > ti) if reverse else (si < ti)
    ones = jnp.broadcast_to(incl.astype(F32)[None], (hb, c, c))
    cum = cumdot(ones, lw)
    cum_ex = cum - lw
    tot = jnp.sum(lw, axis=1, keepdims=True)
    mid = 0.5 * tot
    rt, at = r * jnp.exp(cum - mid), a * jnp.exp(cum_ex - mid)
    einv = jnp.exp(mid - cum)
    bk = jnp.concatenate([b * einv, k * einv], axis=1)
    m_a, m_r = nt(at, bk), nt(rt, bk)
    m_ab = jnp.where(strict, m_a[:, :, :c], 0.0)
    m_ak = jnp.where(strict, m_a[:, :, c:], 0.0)
    t2 = lax.broadcasted_iota(jnp.int32, (c, 2 * c), 0)
    s2 = lax.broadcasted_iota(jnp.int32, (c, 2 * c), 1)
    s2 = jnp.where(s2 >= c, s2 - c, s2)
    m_r = jnp.where((s2 >= t2) if reverse else (s2 <= t2), m_r, 0.0)
    rhs = nt_state(a * jnp.exp(cum_ex), ht) + nn(m_ak, v)
    u = tri_solve(m_ab, rhs) if u_kept is None else known_solve(m_ab, rhs, u_kept)
    uv = jnp.concatenate([u, v], axis=1)
    y = nt_state(r * jnp.exp(cum), ht) + nn(m_r, uv)
    eend = jnp.exp(tot - cum)
    ht_new = ht * jnp.exp(tot) + tn(uv, jnp.concatenate([b * eend, k * eend], axis=1))
    return y, ht_new, u


_HB_F, _HB_B = 16, 16


def _split_heads(x):
    return jnp.stack([x[:, i * RN:(i + 1) * RN] for i in range(x.shape[1] // RN)])


def _merge_heads(y):
    return jnp.concatenate([y[i] for i in range(y.shape[0])], axis=1)


def _chunk_map(reverse, backward):
    flip = reverse != backward
    return (lambda g, c: (NCH - 1 - c, g)) if flip else (lambda g, c: (c, g))


def scan_fwd(name, r, lw, k, v, a, b, reverse):
    hb = _HB_F
    cmap = _chunk_map(reverse, False)

    def body(r_ref, lw_ref, k_ref, v_ref, a_ref, b_ref, y_ref, h0_ref, u_ref, ht_ref):
        @pl.when(pl.program_id(1) == 0)
        def _():
            ht_ref[...] = jnp.zeros_like(ht_ref)

        ht = ht_ref[...]
        h0_ref[0] = ht
        ins = [_split_heads(x[...]) for x in (r_ref, lw_ref, k_ref, v_ref, a_ref, b_ref)]
        y, hn, u = _chunk(*ins, ht, reverse=reverse)
        y_ref[...] = _merge_heads(y)
        u_ref[...] = _merge_heads(u)
        ht_ref[...] = hn

    io = pl.BlockSpec((CHUNK, hb * RN), cmap)
    return pl.pallas_call(
        body, name=name, grid=(RH // hb, NCH),
        in_specs=[io] * 6,
        out_specs=[io, pl.BlockSpec((1, hb, RN, RN), lambda g, c: (cmap(g, c)[0], g, 0, 0)), io],
        out_shape=[jax.ShapeDtypeStruct((T, RW), F32), jax.ShapeDtypeStruct((NCH, RH, RN, RN), F32),
                   jax.ShapeDtypeStruct((T, RW), F32)],
        scratch_shapes=[pltpu.VMEM((hb, RN, RN), F32)],
        compiler_params=pltpu.CompilerParams(dimension_semantics=("parallel", "arbitrary"), vmem_limit_bytes=VMEM_BIG),
    )(r, lw, k, v, a, b)


def scan_bwd(name, r, lw, k, v, a, b, h0, u, dy, reverse):
    hb = _HB_B
    cmap = _chunk_map(reverse, True)

    def body(r_ref, lw_ref, k_ref, v_ref, a_ref, b_ref, h0_ref, u_ref, dy_ref, *rest):
        d_refs, dht_ref = rest[:6], rest[6]

        @pl.when(pl.program_id(1) == 0)
        def _():
            dht_ref[...] = jnp.zeros_like(dht_ref)

        ins = [_split_heads(x[...]) for x in (r_ref, lw_ref, k_ref, v_ref, a_ref, b_ref)]
        _, vjp = jax.vjp(functools.partial(_chunk, reverse=reverse), *ins, h0_ref[0], _split_heads(u_ref[...]))
        dy = _split_heads(dy_ref[...])
        grads = vjp((dy, dht_ref[...], jnp.zeros_like(dy)))
        for d_ref, gval in zip(d_refs, grads[:6]):
            d_ref[...] = _merge_heads(gval).astype(d_ref.dtype)
        dht_ref[...] = grads[6]

    io = pl.BlockSpec((CHUNK, hb * RN), cmap)
    return pl.pallas_call(
        body, name=name, grid=(RH // hb, NCH),
        in_specs=[io] * 6 + [pl.BlockSpec((1, hb, RN, RN), lambda g, c: (cmap(g, c)[0], g, 0, 0)), io, io],
        out_specs=[io] * 6,
        out_shape=[jax.ShapeDtypeStruct((T, RW), F32 if i == 1 else BF16) for i in range(6)],
        scratch_shapes=[pltpu.VMEM((hb, RN, RN), F32)],
        compiler_params=pltpu.CompilerParams(dimension_semantics=("parallel", "arbitrary"), vmem_limit_bytes=VMEM_BIG),
    )(r, lw, k, v, a, b, h0, u, dy)


def loss_stage(out, x2, tgt, g_post):
    tr = 256

    def body(o_ref, x_ref, t_ref, g_ref, do_ref, dy_ref, dg_ref, loss_ref):
        @pl.when(pl.program_id(0) == 0)
        def _():
            dg_ref[...] = jnp.zeros_like(dg_ref)
            loss_ref[...] = jnp.zeros_like(loss_ref)

        nrm, vjp = jax.vjp(_rms, o_ref[...], g_ref[...])
        e = x_ref[...] + nrm - t_ref[...]
        s = jnp.sum(jnp.sum(e * e, axis=1, keepdims=True), axis=0, keepdims=True)
        loss_ref[...] += jnp.broadcast_to(s * (0.5 / D), loss_ref.shape)
        dy = e * (1.0 / D)
        do, dg = vjp(dy)
        do_ref[...] = do.astype(do_ref.dtype)
        dy_ref[...] = dy
        dg_ref[...] += dg

    row = pl.BlockSpec((tr, D), lambda i: (i, 0))
    return pl.pallas_call(
        body, name="loss_stage", grid=(T // tr,),
        in_specs=[row, row, row, pl.BlockSpec((1, D), lambda i: (0, 0))],
        out_specs=[row, row, pl.BlockSpec((1, D), lambda i: (0, 0)), pl.BlockSpec((8, LANE), lambda i: (0, 0))],
        out_shape=[jax.ShapeDtypeStruct((T, D), BF16), jax.ShapeDtypeStruct((T, D), F32),
                   jax.ShapeDtypeStruct((1, D), F32), jax.ShapeDtypeStruct((8, LANE), F32)],
        compiler_params=pltpu.CompilerParams(dimension_semantics=("arbitrary",), vmem_limit_bytes=VMEM_BIG),
    )(out, x2, tgt, g_post)


_EW_BLOCK_BYTES = 1 << 20


def _row_tile(rows, cols):
    best = None
    for tr in range(16, rows + 1, 16):
        if rows % tr == 0 and tr * cols * 4 <= _EW_BLOCK_BYTES:
            best = tr
    return best or rows


def _axis_tile(shape, axis, words):
    rows, cols = shape
    n, other, unit = (rows, cols, 16) if axis == 0 else (cols, rows, LANE)
    best = unit if n % unit == 0 else n
    for t in range(unit, n + 1, unit):
        if n % t == 0 and t * other * words * 4 <= _EW_BLOCK_BYTES:
            best = t
    blk = (best, cols) if axis == 0 else (rows, best)
    at = (lambda s: (s, 0)) if axis == 0 else (lambda s: (0, s))
    return blk, n // best, at


def _adamw_update(g, w_ref, m_ref, v_ref, g_ref, d_ref, nm_ref, nv_ref):
    mm = ADAM_B1 * m_ref[...] + (1.0 - ADAM_B1) * g
    vv = ADAM_B2 * v_ref[...] + (1.0 - ADAM_B2) * (g * g)
    m_hat = mm / (1.0 - ADAM_B1 ** ADAM_STEP)
    v_hat = vv / (1.0 - ADAM_B2 ** ADAM_STEP)
    g_ref[...] = g
    d_ref[...] = -ADAM_LR * (m_hat / (jnp.sqrt(v_hat) + ADAM_EPS) + ADAM_WD * w_ref[...])
    nm_ref[...] = mm
    nv_ref[...] = vv


def adamw(name, w, m, v, parts):
    rows, cols = w.shape
    br = _row_tile(rows, cols)
    npart = len(parts)

    def body(w_ref, m_ref, v_ref, *rest):
        g = rest[0][...].astype(F32)
        for p in rest[1:npart]:
            g = g + p[...].astype(F32)
        _adamw_update(g, w_ref, m_ref, v_ref, *rest[npart:])

    blk = pl.BlockSpec((br, cols), lambda i: (i, 0))
    return pl.pallas_call(
        body, name=name, grid=(rows // br,),
        in_specs=[blk] * (3 + npart), out_specs=[blk] * 4,
        out_shape=[jax.ShapeDtypeStruct((rows, cols), F32)] * 4,
        compiler_params=pltpu.CompilerParams(dimension_semantics=("parallel",), vmem_limit_bytes=VMEM_BIG),
    )(w, m, v, *parts)


def adamw_halves(name, place, w, m, v, mine, theirs, axis):
    half_shape = mine.shape
    blk_shape, nb, at = _axis_tile(half_shape, axis, 1)

    def body(p_ref, w_ref, m_ref, v_ref, a_ref, b_ref, *outs):
        own = (pl.program_id(0) // nb) == p_ref[0]
        _adamw_update(jnp.where(own, a_ref[...], b_ref[...]), w_ref, m_ref, v_ref, *outs)

    blk = pl.BlockSpec(blk_shape, lambda i, p: at(i))
    half = pl.BlockSpec(blk_shape, lambda i, p: at(i % nb))
    return pl.pallas_call(
        body, name=name,
        grid_spec=pltpu.PrefetchScalarGridSpec(num_scalar_prefetch=1, grid=(2 * nb,),
                                               in_specs=[blk] * 3 + [half] * 2, out_specs=[blk] * 4),
        out_shape=[jax.ShapeDtypeStruct(w.shape, F32)] * 4,
        compiler_params=pltpu.CompilerParams(dimension_semantics=("arbitrary",), vmem_limit_bytes=VMEM_BIG),
    )(place, w, m, v, mine, theirs)


def pair_sum(name, place, send, other, axis):
    blk_shape, nb, at = _axis_tile(other.shape[1:], axis, 4)

    def body(p_ref, a_ref, b_ref, o_ref):
        o_ref[...] = (a_ref[...].astype(F32) + b_ref[...].astype(F32)).astype(o_ref.dtype)

    blk = pl.BlockSpec((4,) + blk_shape, lambda i, p: (0,) + at(i))
    mine = pl.BlockSpec((4,) + blk_shape, lambda i, p: (0,) + at(p[0] * nb + i))
    return pl.pallas_call(
        body, name=name,
        grid_spec=pltpu.PrefetchScalarGridSpec(num_scalar_prefetch=1, grid=(nb,), in_specs=[mine, blk], out_specs=blk),
        out_shape=jax.ShapeDtypeStruct(other.shape, BF16),
        compiler_params=pltpu.CompilerParams(dimension_semantics=("arbitrary",), vmem_limit_bytes=VMEM_BIG),
    )(place, send, other)


def sum4(name, place, recv, own, axis):
    blk_shape, nb, at = _axis_tile(recv.shape[1:], axis, 4)

    def body(p_ref, r_ref, s_ref, o_ref):
        me = p_ref[0]
        t = [jnp.where(me == j, s_ref[j], r_ref[j]).astype(F32) for j in range(4)]
        o_ref[...] = ((t[0] + t[1]) + t[2]) + t[3]

    blk = pl.BlockSpec((4,) + blk_shape, lambda i, p: (0,) + at(i))
    return pl.pallas_call(
        body, name=name,
        grid_spec=pltpu.PrefetchScalarGridSpec(num_scalar_prefetch=1, grid=(nb,), in_specs=[blk, blk],
                                               out_specs=pl.BlockSpec(blk_shape, lambda i, p: at(i))),
        out_shape=jax.ShapeDtypeStruct(recv.shape[1:], F32),
        compiler_params=pltpu.CompilerParams(dimension_semantics=("arbitrary",), vmem_limit_bytes=VMEM_BIG),
    )(place, recv, own)


_ANY = pl.BlockSpec(memory_space=pl.ANY)


def _place():
    x, y, c = lax.axis_index("x"), lax.axis_index("y"), lax.axis_index("c")
    return x, y, c, 2 * x + y


def _chip_peers(x, y):
    out = []
    for k in (1, 2, 3):
        px = 1 - x if k & 2 else x
        py = 1 - y if k & 1 else y
        out.append((k, px, py, 2 * px + py))
    return out


def _half(c, shape, axis):
    n = shape[axis] // 2
    sl = pl.ds(pl.multiple_of(c * n, 16 if axis == 0 else LANE), n)
    return (sl,) if axis == 0 else (pl.ds(0, shape[0]), sl)


def gather_weights(srcs, axes):
    side = gather_side(srcs, axes)
    n = len(srcs)

    def body(*refs):
        ins, outs, sems = refs[:n], refs[n:2 * n], refs[2 * n:]
        side.run(ins, outs, sems)

    return pl.pallas_call(
        body, name="gather_weights", in_specs=[_ANY] * n, out_specs=[_ANY] * n,
        out_shape=side.outs, scratch_shapes=side.sems,
    )(*srcs)


def gather_side(srcs, axes):
    n = len(srcs)

    def copies(src, dst, sems, want):
        ssem, rsem, fssem, frsem = sems
        x, y, c, me = _place()
        sib = (x, y, 1 - c)
        out = []
        for i in range(n):
            mine, other = _half(c, srcs[i].shape, axes[i]), _half(1 - c, srcs[i].shape, axes[i])
            for k, px, py, peer in _chip_peers(x, y):
                sems_k = dict(send_sem=ssem.at[i, k - 1], recv_sem=rsem.at[i, k - 1], device_id=(px, py, c),
                              device_id_type=MESH_IDS)
                fsems = dict(send_sem=fssem.at[i, k - 1], recv_sem=frsem.at[i, k - 1], device_id=sib,
                             device_id_type=MESH_IDS)
                got = dst[i].at[(peer,) + mine]
                mk = pltpu.make_async_remote_copy
                made = dict(
                    snd=lambda: mk(src_ref=src[i].at[mine], dst_ref=dst[i].at[(me,) + mine], **sems_k),
                    rcv=lambda: mk(src_ref=src[i].at[mine], dst_ref=got, **sems_k),
                    fwd=lambda: mk(src_ref=got, dst_ref=got, **fsems),
                    frcv=lambda: mk(src_ref=got, dst_ref=dst[i].at[(peer,) + other], **fsems))
                out.append([made[w]() for w in want])
        return out

    def start(src, dst, sems):
        for (snd,) in copies(src, dst, sems, ("snd",)):
            snd.start()

    def finish(src, dst, sems):
        cps = copies(src, dst, sems, ("snd", "rcv", "fwd", "frcv"))
        for _, rcv, fwd, _ in cps:
            rcv.wait_recv()
            fwd.start()
        for snd, _, fwd, frcv in cps:
            frcv.wait_recv()
            snd.wait_send()
            fwd.wait_send()

    return Side(list(srcs), [jax.ShapeDtypeStruct((4,) + s.shape, s.dtype) for s in srcs],
                [pltpu.SemaphoreType.DMA((n, 3))] * 4, start, finish)


def pair_exchange(name, srcs, axes):
    n = len(srcs)

    def half_shape(s, axis):
        return (4, s.shape[1] // 2, s.shape[2]) if axis == 0 else (4, s.shape[1], s.shape[2] // 2)

    def body(*refs):
        src, other = refs[:n], refs[n:2 * n]
        ssem, rsem = refs[2 * n:]
        x, y, c, _ = _place()
        cps = []
        for i in range(n):
            idx = (pl.ds(0, 4),) + _half(1 - c, srcs[i].shape[1:], axes[i])
            cps.append(pltpu.make_async_remote_copy(
                src_ref=src[i].at[idx], dst_ref=other[i], send_sem=ssem.at[i], recv_sem=rsem.at[i],
                device_id=(x, y, 1 - c), device_id_type=MESH_IDS))
            cps[-1].start()
        for cp in cps:
            cp.wait()

    return pl.pallas_call(
        body, name=name, in_specs=[_ANY] * n, out_specs=[_ANY] * n,
        out_shape=[jax.ShapeDtypeStruct(half_shape(s, a), s.dtype) for s, a in zip(srcs, axes)],
        scratch_shapes=[pltpu.SemaphoreType.DMA((n,))] * 2,
    )(*srcs)


def scatter_side(srcs):
    n = len(srcs)

    def copies(src, dst, sems, sends_only=False):
        ssem, rsem = sems
        x, y, c, me = _place()
        out = []
        for i in range(n):
            for k, px, py, peer in _chip_peers(x, y):
                sems_k = dict(send_sem=ssem.at[i, k - 1], recv_sem=rsem.at[i, k - 1], device_id=(px, py, c),
                              device_id_type=MESH_IDS)
                snd = pltpu.make_async_remote_copy(src_ref=src[i].at[peer], dst_ref=dst[i].at[me], **sems_k)
                if sends_only:
                    out.append(snd)
                    continue
                out.append((snd, pltpu.make_async_remote_copy(src_ref=src[i].at[peer], dst_ref=dst[i].at[peer],
                                                              **sems_k)))
        return out

    def start(src, dst, sems):
        for snd in copies(src, dst, sems, sends_only=True):
            snd.start()

    def finish(src, dst, sems):
        for snd, rcv in copies(src, dst, sems):
            rcv.wait_recv()
            snd.wait_send()

    return Side(list(srcs), [jax.ShapeDtypeStruct(s.shape, s.dtype) for s in srcs],
                [pltpu.SemaphoreType.DMA((n, 3))] * 2, start, finish)


_HBM = pl.BlockSpec(memory_space=pltpu.HBM)
_SEM = pl.BlockSpec(memory_space=pltpu.SEMAPHORE)
_DATAFLOW = pltpu.SideEffectType.DATAFLOW_SIDE_EFFECTING


def scatter_start(name, srcs):
    n = len(srcs)
    side = scatter_side(srcs)
    ns = 3 * n

    def body(*refs):
        src, land = refs[:n], refs[n:2 * n]
        sems = refs[2 * n:2 * n + 2 * ns]
        side.start(src, land, (_SemGrid(sems[:ns]), _SemGrid(sems[ns:])))
        refs[-1][...] = jnp.zeros_like(refs[-1])

    hbm = [pltpu.HBM(s.shape, s.dtype) for s in srcs]
    res = pl.pallas_call(
        body, name=name,
        out_shape=[pltpu.SemaphoreType.DMA(())] * (2 * ns) + hbm + hbm + [jax.ShapeDtypeStruct((8, LANE), F32)],
        in_specs=[_HBM] * (2 * n),
        out_specs=[_SEM] * (2 * ns) + [_HBM] * (2 * n) + [pl.BlockSpec(memory_space=pltpu.VMEM)],
        input_output_aliases={i: 2 * ns + i for i in range(2 * n)},
        compiler_params=pltpu.CompilerParams(has_side_effects=_DATAFLOW),
    )(*[pltpu.with_memory_space_constraint(s, pltpu.HBM) for s in srcs],
      *[pltpu.with_memory_space_constraint(lax.empty(s.shape, s.dtype), pltpu.HBM) for s in srcs])
    return res[:2 * ns], res[2 * ns:2 * ns + n], res[2 * ns + n:2 * ns + 2 * n], res[-1]


def scatter_wait(name, sems, srcs, lands, after):
    n = len(srcs)
    side = scatter_side(srcs)
    ns = 3 * n

    def body(*refs):
        src, land = refs[:n], refs[n:2 * n]
        s = refs[2 * n:2 * n + 2 * ns]
        side.finish(src, land, (_SemGrid(s[:ns]), _SemGrid(s[ns:])))

    hbm = [pltpu.HBM(s.shape, s.dtype) for s in srcs]
    res = pl.pallas_call(
        body, name=name, out_shape=hbm + hbm,
        in_specs=[_HBM] * (2 * n) + [_SEM] * (2 * ns) + [_ANY], out_specs=[_HBM] * (2 * n),
        input_output_aliases={i: i for i in range(2 * n)},
        compiler_params=pltpu.CompilerParams(has_side_effects=_DATAFLOW),
    )(*srcs, *lands, *sems, after)
    return res[:n], res[n:]


class _SemGrid:
    def __init__(self, sems):
        self.sems = sems

    @property
    def at(self):
        return self

    def __getitem__(self, ik):
        return self.sems[3 * ik[0] + ik[1]]


def swap_halves(name, srcs):
    n = len(srcs)

    def body(*refs):
        src, dst = refs[:n], refs[n:2 * n]
        ssem, rsem = refs[2 * n:]
        x, y, c, _ = _place()
        cps = []
        for i in range(n):
            cps.append(pltpu.make_async_remote_copy(src_ref=src[i], dst_ref=dst[i], send_sem=ssem.at[i],
                                                    recv_sem=rsem.at[i], device_id=(x, y, 1 - c),
                                                    device_id_type=MESH_IDS))
            cps[-1].start()
        for cp in cps:
            cp.wait()

    return pl.pallas_call(
        body, name=name, in_specs=[_ANY] * n, out_specs=[_ANY] * n,
        out_shape=[jax.ShapeDtypeStruct(s.shape, s.dtype) for s in srcs],
        scratch_shapes=[pltpu.SemaphoreType.DMA((n,))] * 2,
    )(*srcs)


def _ag8_copies(src, dst, sems, sends_only=False):
    x, y, c = lax.axis_index("x"), lax.axis_index("y"), lax.axis_index("c")
    me = 4 * x + 2 * y + c
    out = []
    for k in range(1, 8):
        px = 1 - x if k & 4 else x
        py = 1 - y if k & 2 else y
        pc = 1 - c if k & 1 else c
        peer = 4 * px + 2 * py + pc
        out.append(tuple(pltpu.make_async_remote_copy(
            src_ref=src, dst_ref=dst.at[slot], send_sem=sems[k - 1], recv_sem=sems[7 + k - 1],
            device_id=(px, py, pc), device_id_type=MESH_IDS) for slot in ((me,) if sends_only else (me, peer))))
    return out


def allgather8_start(name, src):
    def body(src_ref, land_ref, *rest):
        for (snd,) in _ag8_copies(src_ref, land_ref, rest[:14], sends_only=True):
            snd.start()

    land = jax.ShapeDtypeStruct((8,) + src.shape, src.dtype)
    res = pl.pallas_call(
        body, name=name,
        out_shape=[pltpu.SemaphoreType.DMA(())] * 14 + [pltpu.HBM(src.shape, src.dtype), pltpu.HBM(land.shape, land.dtype)],
        in_specs=[_HBM, _HBM], out_specs=[_SEM] * 14 + [_HBM, _HBM],
        input_output_aliases={0: 14, 1: 15},
        compiler_params=pltpu.CompilerParams(has_side_effects=_DATAFLOW),
    )(pltpu.with_memory_space_constraint(src, pltpu.HBM),
      pltpu.with_memory_space_constraint(lax.empty(land.shape, land.dtype), pltpu.HBM))
    return res[:14], res[14], res[15]


def allgather8_wait(name, sems, src, land, after):
    def body(src_ref, land_ref, *rest):
        for snd, rcv in _ag8_copies(src_ref, land_ref, rest[:14]):
            rcv.wait_recv()
            snd.wait_send()

    return pl.pallas_call(
        body, name=name, out_shape=[pltpu.HBM(src.shape, src.dtype), pltpu.HBM(land.shape, land.dtype)],
        in_specs=[_HBM, _HBM] + [_SEM] * 14 + [_ANY], out_specs=[_HBM, _HBM],
        input_output_aliases={0: 0, 1: 1},
        compiler_params=pltpu.CompilerParams(has_side_effects=_DATAFLOW),
    )(src, land, *sems, after)


WEIGHTS = ['g_pre', 'w_in', 'mla_q_norm', 'mla_wq_b', 'mla_kv_norm', 'mla_wkv_b', 'rwkv_mu', 'rwkv_w0_f', 'rwkv_w2_f',
           'rwkv_w0_b', 'rwkv_w2_b', 'rwkv_a0_f', 'rwkv_a2_f', 'rwkv_a0_b', 'rwkv_a2_b', 'rwkv_k_k', 'rwkv_k_a',
           'rwkv_r_k', 'rwkv_gn_g', 'rwkv_gn_b', 'w_br_mla', 'w_br_rwkv', 'w_out', 'g_post']
BIG_SHAPES = {'w_in': (D_IN // 4, D), 'mla_wq_b': (Q_RANK, 384), 'mla_wkv_b': (KV_RANK, 512),
              'rwkv_w2_f': (LORA, 256), 'rwkv_w2_b': (LORA, 256), 'rwkv_a2_f': (LORA, 256), 'rwkv_a2_b': (LORA, 256),
              'w_br_mla': (RW, 512), 'w_br_rwkv': (RW, 512), 'w_out': (512, D)}
BIG = list(BIG_SHAPES)
SMALL = [n for n in WEIGHTS if n not in BIG_SHAPES]
SMALL_SHAPES = {'g_pre': (D,), 'mla_q_norm': (Q_RANK,), 'mla_kv_norm': (KV_RANK,), 'rwkv_mu': (3456,),
                'rwkv_w0_f': (RW,), 'rwkv_w0_b': (RW,), 'rwkv_a0_f': (RW,), 'rwkv_a0_b': (RW,), 'rwkv_k_k': (RW,),
                'rwkv_k_a': (RW,), 'rwkv_r_k': (RH, RN), 'rwkv_gn_g': (RW,), 'rwkv_gn_b': (RW,), 'g_post': (D,)}
SMALL_LEN = sum(int(np.prod(s)) for s in SMALL_SHAPES.values())
SMALL_ROWS = 144


UNITS = [('w_in',), ('mla_wq_b',), ('mla_wkv_b',), ('rwkv_w2_f', 'rwkv_w2_b', 'rwkv_a2_f', 'rwkv_a2_b'),
         ('w_br_mla', 'w_br_rwkv'), ('w_out',)]
UNIT_AXIS = [1, 0, 0, 0, 0, 0]
ROW_SHARDED = ('w_in', 'w_out')


def _unit_cat(parts):
    return parts[0] if len(parts) == 1 else jnp.concatenate(parts, axis=0)


def _unit_split(arr, names, axis):
    out, o = {}, 0
    for n in names:
        rows = BIG_SHAPES[n][0]
        out[n] = lax.slice_in_dim(arr, o, o + rows, axis=axis)
        o += rows
    return out


def _gathered(units, ag, own, me):
    out = {}
    for names, arr, mine in zip(units, ag, own):
        slots = [jnp.where(me == j, mine, arr[j]) for j in range(4)]
        for n in names:
            parts = [_unit_split(s, names, 0)[n] for s in slots]
            out[n] = jnp.concatenate(parts, axis=0 if n in ROW_SHARDED else 1)
    return out


def _shards(n, g):
    r, w = BIG_SHAPES[n]
    if n in ROW_SHARDED:
        return [g[j * r:(j + 1) * r] for j in range(4)]
    return [g[:, j * w:(j + 1) * w] for j in range(4)]


def _pack_small(d, extra=None):
    flat = jnp.concatenate([d[n].reshape(-1) for n in SMALL] + ([extra.reshape(-1)] if extra is not None else []))
    return jnp.pad(flat, (0, SMALL_ROWS * LANE - flat.shape[0])).reshape(SMALL_ROWS, LANE)


def _unpack_small(packed):
    flat, out, o = packed.reshape(-1), {}, 0
    for n in SMALL:
        sz = int(np.prod(SMALL_SHAPES[n]))
        out[n] = flat[o:o + sz].reshape(SMALL_SHAPES[n])
        o += sz
    return out


def _perm_w_in(gathered, own, me):
    per = D_IN // 4

    def rows(a, b):
        out = []
        while a < b:
            j, lo = divmod(a, per)
            hi = min(b - j * per, per)
            out.append(jnp.where(me == j, own[lo:hi], gathered[j, lo:hi]))
            a = j * per + hi
        return out

    z = lambda n: [jnp.zeros((n, own.shape[1]), own.dtype)]
    lora = []
    for i in range(4):
        lora += rows(4160 + LORA * i, 4160 + LORA * (i + 1)) + z(LANE - LORA)
    return jnp.concatenate(rows(0, 1024) + rows(1088, 4160) + rows(4544, D_IN) + lora + rows(1024, 1088)
                           + z(256 - ROPE), axis=0)


def _unperm_w_in(g):
    lora = [g[OFF_LORA + LANE * i:OFF_LORA + LANE * i + LORA] for i in range(4)]
    return jnp.concatenate([g[0:1024], g[OFF_KR:OFF_KR + ROPE], g[1024:4096]] + lora + [g[4096:OFF_LORA]], axis=0)


def _perm_wq(w):
    w3 = w.reshape(Q_RANK, HEADS, NOPE + ROPE)
    rope = jnp.pad(w3[:, :, NOPE:], ((0, 0), (0, 0), (0, LANE - ROPE)))
    return jnp.concatenate([w3[:, :, :NOPE].reshape(Q_RANK, -1), rope.reshape(Q_RANK, -1)], axis=1)


def _unperm_wq(g):
    return jnp.concatenate([g[:, :1024].reshape(Q_RANK, HEADS, NOPE),
                            g[:, 1024:].reshape(Q_RANK, HEADS, LANE)[:, :, :ROPE]], axis=2).reshape(Q_RANK, -1)


def _perm_wkv(w):
    w3 = w.reshape(KV_RANK, HEADS, NOPE + VDIM)
    return jnp.concatenate([w3[:, :, :NOPE].reshape(KV_RANK, -1), w3[:, :, NOPE:].reshape(KV_RANK, -1)], axis=1)


def _unperm_wkv(g):
    return jnp.concatenate([g[:, :1024].reshape(KV_RANK, HEADS, NOPE), g[:, 1024:].reshape(KV_RANK, HEADS, VDIM)],
                           axis=2).reshape(KV_RANK, -1)


def _pad_rows(w):
    return jnp.pad(w, ((0, LANE - LORA), (0, 0)))


def _perm_mu(mu):
    parts = [mu[:3072]]
    for i in range(4):
        parts += [mu[3072 + LORA * i:3072 + LORA * (i + 1)], jnp.zeros((LANE - LORA,), mu.dtype)]
    return jnp.concatenate(parts).reshape(1, NLERP)


def _unperm_mu(g):
    g = g.reshape(-1)
    return jnp.concatenate([g[:3072]] + [g[3072 + LANE * i:3072 + LANE * i + LORA] for i in range(4)])


def _constants():
    g2 = np.kron(np.eye(2, dtype=np.float32), np.ones((RN, RN), np.float32))
    pos = jnp.arange(T, dtype=F32)
    inv_freq = jnp.power(ROPE_THETA, -jnp.arange(0, ROPE, 2, dtype=F32) / ROPE)
    ang = pos[:, None] * inv_freq[None, :]
    cos, sin, zero = jnp.cos(ang), jnp.sin(ang), jnp.zeros((T, LANE - ROPE), F32)
    cq = jnp.tile(jnp.concatenate([cos, cos, zero], axis=1), (1, HEADS))
    sq = jnp.tile(jnp.concatenate([-sin, sin, zero], axis=1), (1, HEADS))
    return jnp.asarray(g2, BF16), cq, sq


def _step(x, tgt, w, m, v):
    x2, tgt2 = x.reshape(T, D), tgt.reshape(T, D)
    g2, cq, sq = _constants()
    row = lambda n: w[n].reshape(1, -1)
    w, m, v = ({**t, 'w_in': t['w_in'].T} for t in (w, m, v))

    core, chip = lax.axis_index("c"), 2 * lax.axis_index("x") + lax.axis_index("y")
    core1, chip1 = core.astype(jnp.int32).reshape(1), chip.astype(jnp.int32).reshape(1)
    own_bf = [_unit_cat([w[n].astype(BF16) for n in u]) for u in UNITS]
    wp = _perm_w_in(gather_weights(own_bf[:1], UNIT_AXIS[:1])[0], own_bf[0], chip)
    full = {}
    mu_p = _perm_mu(w['rwkv_mu'])

    st_pre = Stage("pre", f_pre, [(D, BF16), (D, None)], 256, [0], [0], [F32])
    st_mla = Stage("mla", f_mla, [(1024, BF16), (1024, BF16), (1024, BF16), (LANE, BF16), (1024, BF16)], 256,
                   [0, 1, 2], [0, 1, 2, 3], [BF16] * 3)
    st_rpre = Stage("rwkv_pre", f_rwkv_pre, [(RW, F32)] * 9, 256, [0], list(range(10)), [F32])
    st_rpost = Stage("rwkv_post", f_rwkv_post, [(RW, BF16)], 256, [0, 2, 3, 4, 5, 6], [0, 1, 2],
                     [F32, F32, F32, F32, F32, BF16])
    st_merge = Stage("merge", f_merge, [(D, BF16)], 256, [0, 1, 2, 3], [], [BF16] * 4)

    pre_rows, pre_par = [(x2, D, 0)], [row('g_pre')]
    (h,) = st_pre.fwd(pre_rows, pre_par)
    proj, rest = matmul("mm_in", h, wp, "nt", side=gather_side(own_bf[1:], UNIT_AXIS[1:]))
    full.update(_gathered(UNITS[1:], rest, own_bf[1:], chip))
    wq, wkv = _perm_wq(full['mla_wq_b']), _perm_wkv(full['mla_wkv_b'])
    lora_w = [_pad_rows(full[n]).astype(F32) for n in ('rwkv_w2_f', 'rwkv_w2_b', 'rwkv_a2_f', 'rwkv_a2_b')]

    mla_rows = [(proj, 512, OFF_QA // 512), (proj, 512, OFF_KVA // 512), (proj, 256, OFF_KR // 256),
                (cq, 1024, 0), (sq, 1024, 0), (cq, LANE, 0), (sq, LANE, 0)]
    mla_par = [row('mla_q_norm'), row('mla_kv_norm'), wq, wkv]
    att = st_mla.fwd(mla_rows, mla_par)
    y_mla, lse, gm = attn_fwd(*att, proj)

    lerp = shift_fwd(proj, mu_p)
    rpre_rows = [(lerp, NLERP, 0)]
    rpre_par = [row('rwkv_w0_f'), row('rwkv_w0_b'), row('rwkv_a0_f'), row('rwkv_a0_b'), row('rwkv_k_k'),
                row('rwkv_k_a')] + lora_w + [g2]
    r_, v_, lwf, lwb, kf, kb, an, bf_, bb_ = st_rpre.fwd(rpre_rows, rpre_par)
    fin = [r_, lwf, kf, v_, an, bf_]
    bin_ = [r_, lwb, kb, v_, an, bb_]
    yf, h0f, uf = scan_fwd("scan_f", *fin, reverse=False)
    yb, h0b, ub = scan_fwd("scan_b", *bin_, reverse=True)
    rpost_rows = [(yf, RW, 0), (yb, RW, 0), (r_, RW, 0), (kf, RW, 0), (kb, RW, 0),
                  (v_, RW, 0), (proj, RW, OFF_ZR // RW)]
    rpost_par = [row('rwkv_gn_g'), row('rwkv_gn_b'), row('rwkv_r_k'), g2]
    (gr,) = st_rpost.fwd(rpost_rows, rpost_par)
    um = matmul("mm_br_mla", gm, full['w_br_mla'], "nn")
    ur = matmul("mm_br_rwkv", gr, full['w_br_rwkv'], "nn")
    merge_rows = [(um, D, 0), (ur, D, 0), (proj, D, OFF_GM // D), (proj, D, OFF_GR // D)]
    (merged,) = st_merge.fwd(merge_rows, [])
    out = matmul("mm_out", merged, full['w_out'], "nn")
    d_out, dy, dg_post, loss_blk = loss_stage(out, x2, tgt2, row('g_post'))

    gw = {'g_post': dg_post}
    d_merged = matmul("mm_out_dx", d_out, full['w_out'], "nt")
    gw['w_out'] = matmul("mm_out_dw", merged, d_out, "tn")
    (d_um, d_ur, d_gm, d_gr), _ = st_merge.bwd(merge_rows, [], [[(d_merged, D, 0)]])
    d_gmla = matmul("mm_br_mla_dx", d_um, full['w_br_mla'], "nt")
    gw['w_br_mla'] = matmul("mm_br_mla_dw", gm, d_um, "tn")
    d_grw = matmul("mm_br_rwkv_dx", d_ur, full['w_br_rwkv'], "nt")
    gw['w_br_rwkv'] = matmul("mm_br_rwkv_dw", gr, d_ur, "tn")
    (d_y, d_r3, d_kf2, d_kb2, d_v3, d_zr), (gw['rwkv_gn_g'], gw['rwkv_gn_b'], d_rk) = st_rpost.bwd(
        rpost_rows, rpost_par, [[(d_grw, RW, 0)]])
    gw['rwkv_r_k'] = d_rk
    sf = scan_bwd("scan_f_bwd", *fin, h0f, uf, d_y, reverse=False)
    sb = scan_bwd("scan_b_bwd", *bin_, h0b, ub, d_y, reverse=True)
    c = lambda *ts: [(t, RW, 0) for t in ts]
    rpre_cts = [c(sf[0], sb[0], d_r3), c(sf[3], sb[3], d_v3), c(sf[1]), c(sb[1]), c(sf[2], d_kf2), c(sb[2], d_kb2),
                c(sf[4], sb[4]), c(sf[5]), c(sb[5])]
    (d_rin,), rpre_g = st_rpre.bwd(rpre_rows, rpre_par, rpre_cts)
    for n, gval in zip(('rwkv_w0_f', 'rwkv_w0_b', 'rwkv_a0_f', 'rwkv_a0_b', 'rwkv_k_k', 'rwkv_k_a'), rpre_g[:6]):
        gw[n] = gval
    for n, gval in zip(('rwkv_w2_f', 'rwkv_w2_b', 'rwkv_a2_f', 'rwkv_a2_b'), rpre_g[6:]):
        gw[n] = gval[:LORA]
    d_lerp, d_mu = shift_bwd(proj, mu_p, d_rin)
    gw['rwkv_mu'] = _unperm_mu(d_mu)

    *d_att, d_zm = attn_bwd(*att, lse, y_mla, d_gmla, proj)
    mla_cts = [[(t, t.shape[1], 0)] for t in d_att]
    (d_qa, d_kva, d_kr), (gw['mla_q_norm'], gw['mla_kv_norm'], d_wq, d_wkv) = st_mla.bwd(mla_rows, mla_par, mla_cts)
    gw['mla_wq_b'], gw['mla_wkv_b'] = _unperm_wq(d_wq), _unperm_wkv(d_wkv)

    dproj = jnp.concatenate([d_qa, d_kva, d_lerp[:, :3072], d_zm, d_zr, d_gm, d_gr, d_lerp[:, 3072:], d_kr], axis=1)

    def pair_sums(name, ids):
        send = [jnp.stack([_unit_cat([_shards(n, gw[n])[j].astype(BF16) for n in UNITS[i]]) for j in range(4)])
                for i in ids]
        axes = [UNIT_AXIS[i] for i in ids]
        other = pair_exchange(name, send, axes)
        return [pair_sum(f"pair_sum_{i}", core1, s, o, ax) for i, s, o, ax in zip(ids, send, other, axes)]

    late, early = [0], list(range(1, len(UNITS)))
    pairs_e = pair_sums("pair_exchange_rest", early)
    gw_in, recv_e = matmul("mm_in_dw", dproj, h, "tn", BF16, side=scatter_side(pairs_e))
    gw['w_in'] = _unperm_w_in(gw_in)
    pairs_l = pair_sums("pair_exchange_w_in", late)
    sems, src_fly, land_fly, token = scatter_start("scatter_w_in_start", pairs_l)
    dh = matmul("mm_in_dx", dproj, wp, "nn", after=(token,))
    (grad_x,), (gw['g_pre'],) = st_pre.bwd(pre_rows, pre_par, [[(dh, D, 0)], [(dy, D, 0)]])

    big = [dict() for _ in range(4)]

    def update(name, ids, recv, pairs):
        mine = [sum4(f"sum4_{i}", chip1, r, p, UNIT_AXIS[i]) for i, r, p in zip(ids, recv, pairs)]
        theirs = swap_halves(name, mine)
        for i, mi, th in zip(ids, mine, theirs):
            res = adamw_halves(f"adamw_{i}", core1, *[_unit_cat([t[n] for n in UNITS[i]]) for t in (w, m, v)], mi, th,
                               UNIT_AXIS[i])
            for q in range(4):
                big[q].update(_unit_split(res[q], UNITS[i], 0))
        return res

    small_fly = allgather8_start("gather_small_start", _pack_small(gw, loss_blk[0, :1]))
    last = update("swap_halves_rest", early, recv_e, pairs_e)
    own_small, landed = allgather8_wait("gather_small_wait", *small_fly, last[0])
    dev = 2 * chip + core
    parts = [jnp.where(dev == i, own_small, landed[i]) for i in range(8)]
    small = adamw("adamw_small", _pack_small(w), _pack_small(m), _pack_small(v), parts)
    pairs_l, recv_l = scatter_wait("scatter_w_in_wait", sems, src_fly, land_fly, small[0] + last[0][:1, :1])
    update("swap_halves_w_in", late, recv_l, pairs_l)

    outs = []
    for b_d, s_arr in zip(big, small):
        d = {**b_d, **_unpack_small(s_arr)}
        d['w_in'] = d['w_in'].T
        outs.append([d[n] for n in WEIGHTS])
    loss = small[0][SMALL_LEN // LANE, 0]
    return (loss, grad_x.reshape(1, T, D), *outs[0], *outs[1], *outs[2], *outs[3])


def kernel(x, g_pre, w_in, mla_q_norm, mla_wq_b, mla_kv_norm, mla_wkv_b, rwkv_mu, rwkv_w0_f, rwkv_w2_f, rwkv_w0_b, rwkv_w2_b, rwkv_a0_f, rwkv_a2_f, rwkv_a0_b, rwkv_a2_b, rwkv_k_k, rwkv_k_a, rwkv_r_k, rwkv_gn_g, rwkv_gn_b, w_br_mla, w_br_rwkv, w_out, g_post, loss_target, m_g_pre, m_w_in, m_mla_q_norm, m_mla_wq_b, m_mla_kv_norm, m_mla_wkv_b, m_rwkv_mu, m_rwkv_w0_f, m_rwkv_w2_f, m_rwkv_w0_b, m_rwkv_w2_b, m_rwkv_a0_f, m_rwkv_a2_f, m_rwkv_a0_b, m_rwkv_a2_b, m_rwkv_k_k, m_rwkv_k_a, m_rwkv_r_k, m_rwkv_gn_g, m_rwkv_gn_b, m_w_br_mla, m_w_br_rwkv, m_w_out, m_g_post, v_g_pre, v_w_in, v_mla_q_norm, v_mla_wq_b, v_mla_kv_norm, v_mla_wkv_b, v_rwkv_mu, v_rwkv_w0_f, v_rwkv_w2_f, v_rwkv_w0_b, v_rwkv_w2_b, v_rwkv_a0_f, v_rwkv_a2_f, v_rwkv_a0_b, v_rwkv_a2_b, v_rwkv_k_k, v_rwkv_k_a, v_rwkv_r_k, v_rwkv_gn_g, v_rwkv_gn_b, v_w_br_mla, v_w_br_rwkv, v_w_out, v_g_post):
    given = dict(locals())
    w = {n: given[n] for n in WEIGHTS}
    m = {n: given['m_' + n] for n in WEIGHTS}
    v = {n: given['v_' + n] for n in WEIGHTS}
    return _step(x, loss_target, w, m, v)
```

```python
import functools
import math

import numpy as np
import jax
import jax.numpy as jnp
from jax import lax
from jax.experimental import pallas as pl
from jax.experimental.pallas import tpu as pltpu

F32, BF16 = jnp.float32, jnp.bfloat16
MESH_IDS = pl.DeviceIdType.MESH

D = 2048
T = 2048
HEADS = 8
Q_RANK = 512
KV_RANK = 512
NOPE = 128
ROPE = 64
VDIM = 128
RW = 1024
RH = 16
RN = 64
LORA = 96
D_IN = 10688
NORM_EPS = 1e-6
GN_EPS = 64e-5
ROPE_THETA = 10000.0
ADAM_LR, ADAM_B1, ADAM_B2, ADAM_EPS, ADAM_WD, ADAM_STEP = 0.001, 0.9, 0.999, 1e-08, 0.01, 10

LANE = 128
VMEM_BIG = 56 * 2**20

NP = 11008
OFF_QA, OFF_KVA, OFF_RKV, OFF_ZM, OFF_ZR, OFF_GM, OFF_GR, OFF_LORA, OFF_KR = 0, 512, 1024, 4096, 5120, 6144, 8192, 10240, 10752
NLERP = 3584

CHUNK = 64
NCH = T // CHUNK


def _dg(a, b, ca, cb, batch=False, prec=None):
    bd = ((0,), (0,)) if batch else ((), ())
    return lax.dot_general(a, b, (((ca,), (cb,)), bd), precision=prec, preferred_element_type=F32)


@jax.custom_vjp
def bdot(a, b):
    return _dg(a.astype(BF16), b.astype(BF16), 1, 0)


def _bdot_fwd(a, b):
    return bdot(a, b), (a, b)


def _bdot_bwd(res, g):
    a, b = res
    gb = g.astype(BF16)
    da = _dg(gb, b.astype(BF16), 1, 1)
    db = _dg(a.astype(BF16), gb, 0, 0)
    return da.astype(a.dtype), db.astype(b.dtype)


bdot.defvjp(_bdot_fwd, _bdot_bwd)


def _split(x):
    hi = x.astype(BF16)
    lo = (x - hi.astype(F32)).astype(BF16)
    return hi, lo


@jax.custom_vjp
def gsum(x, g2):
    hi, lo = _split(x)
    return _dg(hi, g2, 1, 0) + _dg(lo, g2, 1, 0)


def _gsum_fwd(x, g2):
    return gsum(x, g2), g2


def _gsum_bwd(g2, g):
    hi, lo = _split(g)
    return _dg(hi, g2, 1, 1) + _dg(lo, g2, 1, 1), jnp.zeros_like(g2)


gsum.defvjp(_gsum_fwd, _gsum_bwd)


def headsum(x, g2):
    return jnp.concatenate([gsum(x[:, i * LANE:(i + 1) * LANE], g2) for i in range(x.shape[1] // LANE)], axis=1)


def _terms(x, n):
    out = []
    for i in range(n):
        t = x.astype(BF16)
        out.append(t)
        if i < n - 1:
            x = x - t.astype(F32)
    return out


def _bmm(a, b, ca, cb, na, nb):
    acc = None
    for i, ai in enumerate(_terms(a, na)):
        for j, bj in enumerate(_terms(b, nb)):
            if i + j < max(na, nb):
                p = _dg(ai, bj, ca, cb, True)
                acc = p if acc is None else acc + p
    return acc


_NN, _NT, _TN = (2, 1), (2, 2), (1, 1)


def _make_dots(nf, nb_nn, nb_nt, nb_tn):
    @jax.custom_vjp
    def nn(a, b):
        return _bmm(a, b, *_NN, nf, nf)

    @jax.custom_vjp
    def nt(a, b):
        return _bmm(a, b, *_NT, nf, nf)

    @jax.custom_vjp
    def tn(a, b):
        return _bmm(a, b, *_TN, nf, nf)

    nn.defvjp(lambda a, b: (nn(a, b), (a, b)),
              lambda r, g: (_bmm(g, r[1], *_NT, nb_nn, nb_nn), _bmm(r[0], g, *_TN, nb_nn, nb_nn)))
    nt.defvjp(lambda a, b: (nt(a, b), (a, b)),
              lambda r, g: (_bmm(g, r[1], *_NN, 1, nb_nt), _bmm(g, r[0], *_TN, 1, nb_nt)))
    tn.defvjp(lambda a, b: (tn(a, b), (a, b)),
              lambda r, g: (_bmm(r[1], g, *_NT, nb_tn, nb_tn), _bmm(r[0], g, *_NN, nb_tn, nb_tn)))
    return nn, nt, tn


_SCAN_NF, _SCAN_NB = 1, 1
nn, nt, tn = _make_dots(_SCAN_NF, 1, 2, 1)
_, nt_state, _ = _make_dots(_SCAN_NF, 1, 1, 1)


@jax.custom_vjp
def cumdot(ones, x):
    return _bmm(ones, x, *_NN, 1, 2)


cumdot.defvjp(lambda o, x: (cumdot(o, x), o), lambda o, g: (jnp.zeros_like(o), _bmm(o, g, *_TN, 1, 2)))


def _solve_powers(l):
    pw = [l]
    for _ in range(int(math.log2(l.shape[-1])) - 1):
        pw.append(_bmm(pw[-1], pw[-1], *_NN, _SCAN_NF, _SCAN_NF))
    return pw


@jax.custom_vjp
def tri_solve(l, rhs):
    x = rhs
    for p in _solve_powers(l):
        x = x + _bmm(p, x, *_NN, _SCAN_NF, _SCAN_NF)
    return x


def _tri_solve_fwd(l, rhs):
    pw = _solve_powers(l)
    x = rhs
    for p in pw:
        x = x + _bmm(p, x, *_NN, _SCAN_NF, _SCAN_NF)
    return x, (pw, x)


def _tri_solve_bwd(res, g):
    pw, x = res
    y = g
    for p in pw:
        y = y + _bmm(p, y, *_TN, _SCAN_NB, _SCAN_NB)
    return _bmm(y, x, *_NT, _SCAN_NB, _SCAN_NB), y


tri_solve.defvjp(_tri_solve_fwd, _tri_solve_bwd)


@jax.custom_vjp
def known_solve(l, rhs, x):
    return x


known_solve.defvjp(lambda l, rhs, x: (x, (_solve_powers(l), x)),
                   lambda res, g: _tri_solve_bwd(res, g) + (jnp.zeros_like(g),))


def _rms(x, g):
    return x * lax.rsqrt(jnp.mean(x * x, axis=-1, keepdims=True) + NORM_EPS) * g


def _softplus(x):
    pos = x > 0
    return jnp.where(pos, x, 0.0) + jnp.log(1.0 + jnp.exp(-jnp.where(pos, x, -x)))


def _silu(z):
    return z * jax.nn.sigmoid(z)


_MM_VMEM_BYTES = 32 * 2**20


def _mm_tiles(m, n, k):
    best = None
    for tm in (2048, 1024, 512, 256):
        for tn_ in (2048, 1024, 512, 256):
            for d in range(k // LANE, 0, -1):
                tk = LANE * d
                if m % tm or n % tn_ or k % tk:
                    continue
                nk = k // tk
                vmem = 4 * tk * (tm + tn_) + 8 * tm * tn_ + (4 * tm * tn_ if nk > 1 else 0)
                if vmem > _MM_VMEM_BYTES:
                    continue
                a_reads = n // tn_ if nk > 1 else 1
                b_reads = 1 if (nk == 1 and n == tn_) else m // tm
                acc_rmw = nk * m * n if nk > 1 else 0
                cost = (a_reads * m * k + b_reads * k * n + acc_rmw, -tm * tn_ * tk)
                if best is None or cost < best[0]:
                    best = (cost, (tm, tn_, tk))
    return best[1]


class Side:
    def __init__(self, ins, outs, sems, start, finish):
        self.ins, self.outs, self.sems, self.start, self.finish = ins, outs, sems, start, finish

    def at_step(self, step, steps, *refs):
        @pl.when(step == 0)
        def _():
            self.start(*refs)

    def at_end(self, step, steps, *refs):
        @pl.when(step == steps - 1)
        def _():
            self.finish(*refs)

    def run(self, *refs):
        self.start(*refs)
        self.finish(*refs)


def matmul(name, a, b, mode, out_dtype=F32, side=None, after=()):
    if mode == "nn":
        (m, k), n = a.shape, b.shape[1]
    elif mode == "nt":
        (m, k), n = a.shape, b.shape[0]
    else:
        (k, m), n = a.shape, b.shape[1]
    tm, tn_, tk = _mm_tiles(m, n, k)
    nk = k // tk
    if mode == "nn":
        a_spec = pl.BlockSpec((tm, tk), lambda i, j, kk: (i, kk))
        b_spec = pl.BlockSpec((tk, tn_), lambda i, j, kk: (kk, j))
        ca, cb = 1, 0
    elif mode == "nt":
        a_spec = pl.BlockSpec((tm, tk), lambda i, j, kk: (i, kk))
        b_spec = pl.BlockSpec((tn_, tk), lambda i, j, kk: (j, kk))
        ca, cb = 1, 1
    else:
        a_spec = pl.BlockSpec((tk, tm), lambda i, j, kk: (kk, i))
        b_spec = pl.BlockSpec((tk, tn_), lambda i, j, kk: (kk, j))
        ca, cb = 0, 0

    grid = (m // tm, n // tn_, nk)
    n_in = len(side.ins) if side else 0
    n_out = len(side.outs) if side else 0
    n_dep = len(after)

    def body(a_ref, b_ref, *rest):
        rest = rest[n_dep:]
        s_ins, o_ref, s_outs = rest[:n_in], rest[n_in], rest[n_in + 1:n_in + 1 + n_out]
        scratch = rest[n_in + 1 + n_out:]
        acc, s_sems = (scratch[:1], scratch[1:]) if nk > 1 else ((), scratch)
        steps = grid[0] * grid[1] * grid[2]
        if side:
            step = (pl.program_id(0) * grid[1] + pl.program_id(1)) * grid[2] + pl.program_id(2)
            side.at_step(step, steps, s_ins, s_outs, s_sems)

        part = _dg(a_ref[...].astype(BF16), b_ref[...].astype(BF16), ca, cb)
        if nk == 1:
            o_ref[...] = part.astype(o_ref.dtype)
        else:
            acc_ref, kk = acc[0], pl.program_id(2)

            @pl.when(kk == 0)
            def _():
                acc_ref[...] = part

            @pl.when(kk > 0)
            def _():
                acc_ref[...] += part

            @pl.when(kk == nk - 1)
            def _():
                o_ref[...] = acc_ref[...].astype(o_ref.dtype)

        if side:
            side.at_end(step, steps, s_ins, s_outs, s_sems)

    res = pl.pallas_call(
        body, name=name, grid=grid,
        in_specs=[a_spec, b_spec] + [_ANY] * (n_dep + n_in),
        out_specs=[pl.BlockSpec((tm, tn_), lambda i, j, kk: (i, j))] + [_ANY] * n_out,
        out_shape=[jax.ShapeDtypeStruct((m, n), out_dtype)] + (list(side.outs) if side else []),
        scratch_shapes=([pltpu.VMEM((tm, tn_), F32)] if nk > 1 else []) + (list(side.sems) if side else []),
        compiler_params=pltpu.CompilerParams(
            dimension_semantics=("arbitrary",) * 3 if side else ("parallel", "parallel", "arbitrary"),
            vmem_limit_bytes=VMEM_BIG),
    )(a, b, *after, *(side.ins if side else []))
    return (res[0], res[1:]) if side else res[0]


def _rspec(tr, width, blk):
    return pl.BlockSpec((tr, width), lambda i: (i, blk))


def _full_spec(arr):
    return pl.BlockSpec(arr.shape, lambda i: (0,) * arr.ndim)


class Stage:
    def __init__(self, name, f, outs, tr, diff_rows, diff_params, drow_dtypes):
        self.name, self.f, self.outs, self.tr = name, f, outs, tr
        self.diff_rows, self.diff_params, self.drow_dtypes = diff_rows, diff_params, drow_dtypes

    def fwd(self, rows, params):
        f, nr, npar = self.f, len(rows), len(params)
        stored = [(w, dt) for (w, dt) in self.outs if dt is not None]
        keep = [i for i, (w, dt) in enumerate(self.outs) if dt is not None]

        def body(*refs):
            vals = f(*[r[...].astype(F32) for r in refs[:nr]], *[p[...] for p in refs[nr:nr + npar]])
            for o_ref, i in zip(refs[nr + npar:], keep):
                o_ref[...] = vals[i].astype(o_ref.dtype)

        return pl.pallas_call(
            body, name=self.name + "_fwd", grid=(T // self.tr,),
            in_specs=[_rspec(self.tr, w, b) for (_, w, b) in rows] + [_full_spec(p) for p in params],
            out_specs=[_rspec(self.tr, w, 0) for (w, _) in stored],
            out_shape=[jax.ShapeDtypeStruct((T, w), dt) for (w, dt) in stored],
            compiler_params=pltpu.CompilerParams(dimension_semantics=("arbitrary",), vmem_limit_bytes=VMEM_BIG),
        )(*[r[0] for r in rows], *params)

    def bwd(self, rows, params, cts):
        f, nr, npar = self.f, len(rows), len(params)
        dr_idx, dp_idx = self.diff_rows, self.diff_params
        flat_cts = [c for lst in cts for c in lst]
        nct = len(flat_cts)

        def body(*refs):
            row_refs, par_refs = refs[:nr], refs[nr:nr + npar]
            ct_refs = refs[nr + npar:nr + npar + nct]
            drow_refs = refs[nr + npar + nct:nr + npar + nct + len(dr_idx)]
            dpar_refs = refs[nr + npar + nct + len(dr_idx):]
            row_vals = [r[...].astype(F32) for r in row_refs]
            par_vals = [p[...] for p in par_refs]

            def g(*dv):
                rv, pv = list(row_vals), list(par_vals)
                for j, i in enumerate(dr_idx):
                    rv[i] = dv[j]
                for j, i in enumerate(dp_idx):
                    pv[i] = dv[len(dr_idx) + j]
                return f(*rv, *pv)

            _, vjp = jax.vjp(g, *[row_vals[i] for i in dr_idx], *[par_vals[i] for i in dp_idx])
            ct_vals, pos = [], 0
            for lst in cts:
                acc = ct_refs[pos][...].astype(F32)
                for q in range(1, len(lst)):
                    acc = acc + ct_refs[pos + q][...].astype(F32)
                pos += len(lst)
                ct_vals.append(acc)
            grads = vjp(tuple(ct_vals))
            for j, r in enumerate(drow_refs):
                r[...] = grads[j].astype(r.dtype)

            @pl.when(pl.program_id(0) == 0)
            def _():
                for r in dpar_refs:
                    r[...] = jnp.zeros_like(r)

            for j, r in enumerate(dpar_refs):
                r[...] += grads[len(dr_idx) + j].astype(F32)

        drow_shapes = [jax.ShapeDtypeStruct((T, rows[i][1]), dt) for i, dt in zip(dr_idx, self.drow_dtypes)]
        dpar_shapes = [jax.ShapeDtypeStruct(params[i].shape, F32) for i in dp_idx]
        res = pl.pallas_call(
            body, name=self.name + "_bwd", grid=(T // self.tr,),
            in_specs=[_rspec(self.tr, w, b) for (_, w, b) in rows] + [_full_spec(p) for p in params]
            + [_rspec(self.tr, w, b) for (_, w, b) in flat_cts],
            out_specs=[_rspec(self.tr, rows[i][1], 0) for i in dr_idx] + [_full_spec(params[i]) for i in dp_idx],
            out_shape=drow_shapes + dpar_shapes,
            compiler_params=pltpu.CompilerParams(dimension_semantics=("arbitrary",), vmem_limit_bytes=VMEM_BIG),
        )(*[r[0] for r in rows], *params, *[c[0] for c in flat_cts])
        return res[:len(dr_idx)], res[len(dr_idx):]


def f_pre(x, g):
    return _rms(x, g), x


@jax.custom_vjp
def swap32(t):
    width = t.shape[1]
    lane = lax.broadcasted_iota(jnp.int32, t.shape, 1) % LANE
    return jnp.where(lane < 32, pltpu.roll(t, width - 32, 1), jnp.where(lane < 64, pltpu.roll(t, 32, 1), 0.0))


swap32.defvjp(lambda t: (swap32(t), None), lambda _, g: (swap32(g),))


def f_mla(q_a, kv_a, kr, cq, sq, ck, sk, gq, gkv, wq, wkv):
    q = bdot(_rms(q_a, gq), wq)
    kv = bdot(_rms(kv_a, gkv), wkv)
    t, k = q[:, 1024:], kr[:, :LANE]
    return (q[:, :1024], t * cq + swap32(t) * sq, kv[:, :1024], k * ck + swap32(k) * sk, kv[:, 1024:])


def f_rwkv_pre(lerp, w0f, w0b, a0f, a0b, kkw, kaw, w2f, w2b, a2f, a2b, g2):
    r, k, v = lerp[:, :RW], lerp[:, RW:2 * RW], lerp[:, 2 * RW:3 * RW]
    wdf, wdb, adf, adb = (lerp[:, 3 * RW + i * LANE:3 * RW + (i + 1) * LANE] for i in range(4))

    def logdecay(w0, wd, w2):
        z = w0 + bdot(jnp.tanh(wd), w2)
        return -jnp.exp(-_softplus(-z) - 0.5)

    a_f = jax.nn.sigmoid(a0f + bdot(adf, a2f))
    a_b = jax.nn.sigmoid(a0b + bdot(adb, a2b))
    kk = k * kkw
    kk = kk / jnp.maximum(jnp.sqrt(headsum(kk * kk, g2)), 1e-12)
    return (r, v, logdecay(w0f, wdf, w2f), logdecay(w0b, wdb, w2b),
            k * (1.0 + (a_f - 1.0) * kaw), k * (1.0 + (a_b - 1.0) * kaw), -kk, kk * a_f, kk * a_b)


def f_rwkv_post(yf, yb, r, kf, kb, v, z, gng, gnb, rk, g2):
    y = yf + yb
    mu = headsum(y, g2) * (1.0 / RN)
    d = y - mu
    var = headsum(d * d, g2) * (1.0 / RN)
    yn = d * lax.rsqrt(var + GN_EPS) * gng + gnb
    bonus = headsum(r * (kf + kb) * rk, g2) * v
    return ((yn + bonus) * _silu(z),)


def f_gate(y, z):
    return (y * _silu(z),)


def f_merge(um, ur, gm, gr):
    return (jax.nn.sigmoid(gm) * um + jax.nn.sigmoid(gr) * ur,)


_SHIFT_W = 256


def _lerp_colblock(j):
    return jnp.where(j < 3072 // _SHIFT_W, OFF_RKV // _SHIFT_W + j, OFF_LORA // _SHIFT_W + j - 3072 // _SHIFT_W)


def _nbr_mean(x):
    row = lax.broadcasted_iota(jnp.int32, x.shape, 0)
    up = jnp.where(row == 0, 0.0, pltpu.roll(x, 1, 0))
    dn = jnp.where(row == T - 1, 0.0, pltpu.roll(x, T - 1, 0))
    return 0.5 * (up + dn)


def shift_fwd(proj, mu):
    def body(x_ref, mu_ref, o_ref):
        x = x_ref[...]
        o_ref[...] = x + mu_ref[...] * (_nbr_mean(x) - x)

    return pl.pallas_call(
        body, name="shift_fwd", grid=(NLERP // _SHIFT_W,),
        in_specs=[pl.BlockSpec((T, _SHIFT_W), lambda j: (0, _lerp_colblock(j))),
                  pl.BlockSpec((1, _SHIFT_W), lambda j: (0, j))],
        out_specs=pl.BlockSpec((T, _SHIFT_W), lambda j: (0, j)),
        out_shape=jax.ShapeDtypeStruct((T, NLERP), F32),
        compiler_params=pltpu.CompilerParams(dimension_semantics=("parallel",), vmem_limit_bytes=VMEM_BIG),
    )(proj, mu)


def shift_bwd(proj, mu, g):
    def body(x_ref, mu_ref, g_ref, dx_ref, dmu_ref):
        x, gv = x_ref[...], g_ref[...]
        dmu_ref[...] = jnp.sum(gv * (_nbr_mean(x) - x), axis=0, keepdims=True)
        gm = gv * mu_ref[...]
        dx_ref[...] = (gv - gm + _nbr_mean(gm)).astype(dx_ref.dtype)

    col = pl.BlockSpec((T, _SHIFT_W), lambda j: (0, j))
    vec = pl.BlockSpec((1, _SHIFT_W), lambda j: (0, j))
    return pl.pallas_call(
        body, name="shift_bwd", grid=(NLERP // _SHIFT_W,),
        in_specs=[pl.BlockSpec((T, _SHIFT_W), lambda j: (0, _lerp_colblock(j))), vec, col],
        out_specs=[col, vec],
        out_shape=[jax.ShapeDtypeStruct((T, NLERP), BF16), jax.ShapeDtypeStruct((1, NLERP), F32)],
        compiler_params=pltpu.CompilerParams(dimension_semantics=("parallel",), vmem_limit_bytes=VMEM_BIG),
    )(proj, mu, g)


_TQ_F, _TQ_B = 256, 512
_ATT_SCALE = (NOPE + ROPE) ** -0.5


def _probs(q, k, lse=None):
    s = _dg(q, k, 1, 1) * _ATT_SCALE
    if lse is not None:
        return jnp.exp(s - lse), lse
    m = jnp.max(s, axis=-1, keepdims=True)
    e = jnp.exp(s - m)
    l = jnp.sum(e, axis=-1, keepdims=True)
    return e * (1.0 / l), m + jnp.log(l)


def _q_blk(tq):
    return pl.BlockSpec((tq, LANE), lambda h, i: (i, h))


_K_BLK = pl.BlockSpec((T, LANE), lambda h, i: (0, h))
_KR_BLK = pl.BlockSpec((T, LANE), lambda h, i: (0, 0))


def _load_qk(qn_ref, qr_ref, kn_ref, kr_ref, kcat_ref):
    @pl.when(pl.program_id(1) == 0)
    def _():
        kcat_ref[:, :LANE] = kn_ref[...]
        kcat_ref[:, LANE:] = kr_ref[...]

    return jnp.concatenate([qn_ref[...], qr_ref[...]], axis=1), kcat_ref[...]


def attn_fwd(qn, qr, kn, kr, v, proj):
    def body(qn_ref, qr_ref, kn_ref, kr_ref, v_ref, z_ref, o_ref, lse_ref, g_ref, kcat_ref):
        q, k = _load_qk(qn_ref, qr_ref, kn_ref, kr_ref, kcat_ref)
        p, lse = _probs(q, k)
        o = _dg(p.astype(BF16), v_ref[...], 1, 0)
        o_ref[...] = o
        lse_ref[...] = jnp.broadcast_to(lse, lse_ref.shape)
        g_ref[...] = f_gate(o, z_ref[...])[0].astype(g_ref.dtype)

    wide = jax.ShapeDtypeStruct((T, HEADS * VDIM), F32)
    z_blk = pl.BlockSpec((_TQ_F, LANE), lambda h, i: (i, OFF_ZM // LANE + h))
    return pl.pallas_call(
        body, name="attn_fwd", grid=(HEADS, T // _TQ_F),
        in_specs=[_q_blk(_TQ_F), _q_blk(_TQ_F), _K_BLK, _KR_BLK, _K_BLK, z_blk], out_specs=[_q_blk(_TQ_F)] * 3,
        out_shape=[wide, wide, jax.ShapeDtypeStruct((T, HEADS * VDIM), BF16)],
        scratch_shapes=[pltpu.VMEM((T, 2 * LANE), BF16)],
        compiler_params=pltpu.CompilerParams(dimension_semantics=("arbitrary", "arbitrary"), vmem_limit_bytes=VMEM_BIG),
    )(qn, qr, kn, kr, v, proj)


def attn_bwd(qn, qr, kn, kr, v, lse, o, dg, proj):
    def body(qn_ref, qr_ref, kn_ref, kr_ref, v_ref, lse_ref, o_ref, dg_ref, z_ref, dqn_ref, dqr_ref, dkn_ref,
             dkr_ref, dv_ref, dz_ref, kcat_ref):
        h, i = pl.program_id(0), pl.program_id(1)

        @pl.when(i == 0)
        def _():
            dkn_ref[...] = jnp.zeros_like(dkn_ref)
            dv_ref[...] = jnp.zeros_like(dv_ref)

        @pl.when((i == 0) & (h == 0))
        def _():
            dkr_ref[...] = jnp.zeros_like(dkr_ref)

        q, k = _load_qk(qn_ref, qr_ref, kn_ref, kr_ref, kcat_ref)
        o = o_ref[...]
        _, gate_vjp = jax.vjp(f_gate, o, z_ref[...].astype(F32))
        do_f, dz = gate_vjp((dg_ref[...].astype(F32),))
        dz_ref[...] = dz.astype(dz_ref.dtype)
        dob = do_f.astype(BF16)
        p, _ = _probs(q, k, lse_ref[:, :1])
        dv_ref[...] += _dg(p.astype(BF16), dob, 0, 0)
        dp = _dg(dob, v_ref[...], 1, 1)
        delta = jnp.sum(do_f * o, axis=-1, keepdims=True)
        ds = (p * (dp - delta) * _ATT_SCALE).astype(BF16)
        dq = _dg(ds, k, 1, 0)
        dqn_ref[...] = dq[:, :LANE]
        dqr_ref[...] = dq[:, LANE:]
        dk = _dg(ds, q, 0, 0)
        dkn_ref[...] += dk[:, :LANE]
        dkr_ref[...] += dk[:, LANE:]

    wide = jax.ShapeDtypeStruct((T, HEADS * LANE), F32)
    return pl.pallas_call(
        body, name="attn_bwd", grid=(HEADS, T // _TQ_B),
        in_specs=[_q_blk(_TQ_B), _q_blk(_TQ_B), _K_BLK, _KR_BLK, _K_BLK] + [_q_blk(_TQ_B)] * 3
        + [pl.BlockSpec((_TQ_B, LANE), lambda h, i: (i, OFF_ZM // LANE + h))],
        out_specs=[_q_blk(_TQ_B), _q_blk(_TQ_B), _K_BLK, _KR_BLK, _K_BLK, _q_blk(_TQ_B)],
        out_shape=[wide, wide, wide, jax.ShapeDtypeStruct((T, LANE), F32), wide,
                   jax.ShapeDtypeStruct((T, HEADS * LANE), BF16)],
        scratch_shapes=[pltpu.VMEM((T, 2 * LANE), BF16)],
        compiler_params=pltpu.CompilerParams(dimension_semantics=("arbitrary", "arbitrary"), vmem_limit_bytes=VMEM_BIG),
    )(qn, qr, kn, kr, v, lse, o, dg, proj)


def _chunk(r, lw, k, v, a, b, ht, u_kept=None, *, reverse):
    hb, c, _ = r.shape
    ti = lax.broadcasted_iota(jnp.int32, (c, c), 0)
    si = lax.broadcasted_iota(jnp.int32, (c, c), 1)
    incl = (si >= ti) if reverse else (si <= ti)
    strict = (si > ti) if reverse else (si < ti)
    ones = jnp.broadcast_to(incl.astype(F32)[None], (hb, c, c))
    cum = cumdot(ones, lw)
    cum_ex = cum - lw
    tot = jnp.sum(lw, axis=1, keepdims=True)
    mid = 0.5 * tot
    rt, at = r * jnp.exp(cum - mid), a * jnp.exp(cum_ex - mid)
    einv = jnp.exp(mid - cum)
    bk = jnp.concatenate([b * einv, k * einv], axis=1)
    m_a, m_r = nt(at, bk), nt(rt, bk)
    m_ab = jnp.where(strict, m_a[:, :, :c], 0.0)
    m_ak = jnp.where(strict, m_a[:, :, c:], 0.0)
    t2 = lax.broadcasted_iota(jnp.int32, (c, 2 * c), 0)
    s2 = lax.broadcasted_iota(jnp.int32, (c, 2 * c), 1)
    s2 = jnp.where(s2 >= c, s2 - c, s2)
    m_r = jnp.where((s2 >= t2) if reverse else (s2 <= t2), m_r, 0.0)
    rhs = nt_state(a * jnp.exp(cum_ex), ht) + nn(m_ak, v)
    u = tri_solve(m_ab, rhs) if u_kept is None else known_solve(m_ab, rhs, u_kept)
    uv = jnp.concatenate([u, v], axis=1)
    y = nt_state(r * jnp.exp(cum), ht) + nn(m_r, uv)
    eend = jnp.exp(tot - cum)
    ht_new = ht * jnp.exp(tot) + tn(uv, jnp.concatenate([b * eend, k * eend], axis=1))
    return y, ht_new, u


_HB_F, _HB_B = 16, 16


def _split_heads(x):
    return jnp.stack([x[:, i * RN:(i + 1) * RN] for i in range(x.shape[1] // RN)])


def _merge_heads(y):
    return jnp.concatenate([y[i] for i in range(y.shape[0])], axis=1)


def _chunk_map(reverse, backward):
    flip = reverse != backward
    return (lambda g, c: (NCH - 1 - c, g)) if flip else (lambda g, c: (c, g))


def scan_fwd(name, r, lw, k, v, a, b, reverse):
    hb = _HB_F
    cmap = _chunk_map(reverse, False)

    def body(r_ref, lw_ref, k_ref, v_ref, a_ref, b_ref, y_ref, h0_ref, u_ref, ht_ref):
        @pl.when(pl.program_id(1) == 0)
        def _():
            ht_ref[...] = jnp.zeros_like(ht_ref)

        ht = ht_ref[...]
        h0_ref[0] = ht
        ins = [_split_heads(x[...]) for x in (r_ref, lw_ref, k_ref, v_ref, a_ref, b_ref)]
        y, hn, u = _chunk(*ins, ht, reverse=reverse)
        y_ref[...] = _merge_heads(y)
        u_ref[...] = _merge_heads(u)
        ht_ref[...] = hn

    io = pl.BlockSpec((CHUNK, hb * RN), cmap)
    return pl.pallas_call(
        body, name=name, grid=(RH // hb, NCH),
        in_specs=[io] * 6,
        out_specs=[io, pl.BlockSpec((1, hb, RN, RN), lambda g, c: (cmap(g, c)[0], g, 0, 0)), io],
        out_shape=[jax.ShapeDtypeStruct((T, RW), F32), jax.ShapeDtypeStruct((NCH, RH, RN, RN), F32),
                   jax.ShapeDtypeStruct((T, RW), F32)],
        scratch_shapes=[pltpu.VMEM((hb, RN, RN), F32)],
        compiler_params=pltpu.CompilerParams(dimension_semantics=("parallel", "arbitrary"), vmem_limit_bytes=VMEM_BIG),
    )(r, lw, k, v, a, b)


def scan_bwd(name, r, lw, k, v, a, b, h0, u, dy, reverse):
    hb = _HB_B
    cmap = _chunk_map(reverse, True)

    def body(r_ref, lw_ref, k_ref, v_ref, a_ref, b_ref, h0_ref, u_ref, dy_ref, *rest):
        d_refs, dht_ref = rest[:6], rest[6]

        @pl.when(pl.program_id(1) == 0)
        def _():
            dht_ref[...] = jnp.zeros_like(dht_ref)

        ins = [_split_heads(x[...]) for x in (r_ref, lw_ref, k_ref, v_ref, a_ref, b_ref)]
        _, vjp = jax.vjp(functools.partial(_chunk, reverse=reverse), *ins, h0_ref[0], _split_heads(u_ref[...]))
        dy = _split_heads(dy_ref[...])
        grads = vjp((dy, dht_ref[...], jnp.zeros_like(dy)))
        for d_ref, gval in zip(d_refs, grads[:6]):
            d_ref[...] = _merge_heads(gval).astype(d_ref.dtype)
        dht_ref[...] = grads[6]

    io = pl.BlockSpec((CHUNK, hb * RN), cmap)
    return pl.pallas_call(
        body, name=name, grid=(RH // hb, NCH),
        in_specs=[io] * 6 + [pl.BlockSpec((1, hb, RN, RN), lambda g, c: (cmap(g, c)[0], g, 0, 0)), io, io],
        out_specs=[io] * 6,
        out_shape=[jax.ShapeDtypeStruct((T, RW), F32 if i == 1 else BF16) for i in range(6)],
        scratch_shapes=[pltpu.VMEM((hb, RN, RN), F32)],
        compiler_params=pltpu.CompilerParams(dimension_semantics=("parallel", "arbitrary"), vmem_limit_bytes=VMEM_BIG),
    )(r, lw, k, v, a, b, h0, u, dy)


def loss_stage(merged, w_out, x2, tgt, g_post):
    tr = 256

    def body(a_ref, w_ref, x_ref, t_ref, g_ref, do_ref, dy_ref, dg_ref, loss_ref):
        @pl.when(pl.program_id(0) == 0)
        def _():
            dg_ref[...] = jnp.zeros_like(dg_ref)
            loss_ref[...] = jnp.zeros_like(loss_ref)

        out = _dg(a_ref[...], w_ref[...], 1, 0)
        nrm, vjp = jax.vjp(_rms, out, g_ref[...])
        e = x_ref[...] + nrm - t_ref[...]
        s = jnp.sum(jnp.sum(e * e, axis=1, keepdims=True), axis=0, keepdims=True)
        loss_ref[...] += jnp.broadcast_to(s * (0.5 / D), loss_ref.shape)
        dy = e * (1.0 / D)
        do, dg = vjp(dy)
        do_ref[...] = do.astype(do_ref.dtype)
        dy_ref[...] = dy
        dg_ref[...] += dg

    row = pl.BlockSpec((tr, D), lambda i: (i, 0))
    return pl.pallas_call(
        body, name="loss_stage", grid=(T // tr,),
        in_specs=[row, pl.BlockSpec((D, D), lambda i: (0, 0)), row, row, pl.BlockSpec((1, D), lambda i: (0, 0))],
        out_specs=[row, row, pl.BlockSpec((1, D), lambda i: (0, 0)), pl.BlockSpec((8, LANE), lambda i: (0, 0))],
        out_shape=[jax.ShapeDtypeStruct((T, D), BF16), jax.ShapeDtypeStruct((T, D), F32),
                   jax.ShapeDtypeStruct((1, D), F32), jax.ShapeDtypeStruct((8, LANE), F32)],
        compiler_params=pltpu.CompilerParams(dimension_semantics=("arbitrary",), vmem_limit_bytes=VMEM_BIG),
    )(merged, w_out, x2, tgt, g_post)


_EW_BLOCK_BYTES = 1 << 20


def _row_tile(rows, cols):
    best = None
    for tr in range(16, rows + 1, 16):
        if rows % tr == 0 and tr * cols * 4 <= _EW_BLOCK_BYTES:
            best = tr
    return best or rows


def _axis_tile(shape, axis, words):
    rows, cols = shape
    n, other, unit = (rows, cols, 16) if axis == 0 else (cols, rows, LANE)
    best = unit if n % unit == 0 else n
    for t in range(unit, n + 1, unit):
        if n % t == 0 and t * other * words * 4 <= _EW_BLOCK_BYTES:
            best = t
    blk = (best, cols) if axis == 0 else (rows, best)
    at = (lambda s: (s, 0)) if axis == 0 else (lambda s: (0, s))
    return blk, n // best, at


def _adamw_update(g, w_ref, m_ref, v_ref, g_ref, d_ref, nm_ref, nv_ref):
    mm = ADAM_B1 * m_ref[...] + (1.0 - ADAM_B1) * g
    vv = ADAM_B2 * v_ref[...] + (1.0 - ADAM_B2) * (g * g)
    m_hat = mm / (1.0 - ADAM_B1 ** ADAM_STEP)
    v_hat = vv / (1.0 - ADAM_B2 ** ADAM_STEP)
    g_ref[...] = g
    d_ref[...] = -ADAM_LR * (m_hat / (jnp.sqrt(v_hat) + ADAM_EPS) + ADAM_WD * w_ref[...])
    nm_ref[...] = mm
    nv_ref[...] = vv


def adamw(name, w, m, v, parts):
    rows, cols = w.shape
    br = _row_tile(rows, cols)
    npart = len(parts)

    def body(w_ref, m_ref, v_ref, *rest):
        g = rest[0][...].astype(F32)
        for p in rest[1:npart]:
            g = g + p[...].astype(F32)
        _adamw_update(g, w_ref, m_ref, v_ref, *rest[npart:])

    blk = pl.BlockSpec((br, cols), lambda i: (i, 0))
    return pl.pallas_call(
        body, name=name, grid=(rows // br,),
        in_specs=[blk] * (3 + npart), out_specs=[blk] * 4,
        out_shape=[jax.ShapeDtypeStruct((rows, cols), F32)] * 4,
        compiler_params=pltpu.CompilerParams(dimension_semantics=("parallel",), vmem_limit_bytes=VMEM_BIG),
    )(w, m, v, *parts)


def adamw_halves(name, place, w, m, v, mine, theirs, axis):
    half_shape = mine.shape
    blk_shape, nb, at = _axis_tile(half_shape, axis, 1)

    def body(p_ref, w_ref, m_ref, v_ref, a_ref, b_ref, *outs):
        own = (pl.program_id(0) // nb) == p_ref[0]
        _adamw_update(jnp.where(own, a_ref[...], b_ref[...]), w_ref, m_ref, v_ref, *outs)

    blk = pl.BlockSpec(blk_shape, lambda i, p: at(i))
    half = pl.BlockSpec(blk_shape, lambda i, p: at(i % nb))
    return pl.pallas_call(
        body, name=name,
        grid_spec=pltpu.PrefetchScalarGridSpec(num_scalar_prefetch=1, grid=(2 * nb,),
                                               in_specs=[blk] * 3 + [half] * 2, out_specs=[blk] * 4),
        out_shape=[jax.ShapeDtypeStruct(w.shape, F32)] * 4,
        compiler_params=pltpu.CompilerParams(dimension_semantics=("arbitrary",), vmem_limit_bytes=VMEM_BIG),
    )(place, w, m, v, mine, theirs)


def pair_sum(name, place, send, other, axis):
    blk_shape, nb, at = _axis_tile(other.shape[1:], axis, 4)

    def body(p_ref, a_ref, b_ref, o_ref):
        o_ref[...] = (a_ref[...].astype(F32) + b_ref[...].astype(F32)).astype(o_ref.dtype)

    blk = pl.BlockSpec((4,) + blk_shape, lambda i, p: (0,) + at(i))
    mine = pl.BlockSpec((4,) + blk_shape, lambda i, p: (0,) + at(p[0] * nb + i))
    return pl.pallas_call(
        body, name=name,
        grid_spec=pltpu.PrefetchScalarGridSpec(num_scalar_prefetch=1, grid=(nb,), in_specs=[mine, blk], out_specs=blk),
        out_shape=jax.ShapeDtypeStruct(other.shape, BF16),
        compiler_params=pltpu.CompilerParams(dimension_semantics=("arbitrary",), vmem_limit_bytes=VMEM_BIG),
    )(place, send, other)


def sum4(name, place, recv, own, axis):
    blk_shape, nb, at = _axis_tile(recv.shape[1:], axis, 4)

    def body(p_ref, r_ref, s_ref, o_ref):
        me = p_ref[0]
        t = [jnp.where(me == j, s_ref[j], r_ref[j]).astype(F32) for j in range(4)]
        o_ref[...] = ((t[0] + t[1]) + t[2]) + t[3]

    blk = pl.BlockSpec((4,) + blk_shape, lambda i, p: (0,) + at(i))
    return pl.pallas_call(
        body, name=name,
        grid_spec=pltpu.PrefetchScalarGridSpec(num_scalar_prefetch=1, grid=(nb,), in_specs=[blk, blk],
                                               out_specs=pl.BlockSpec(blk_shape, lambda i, p: at(i))),
        out_shape=jax.ShapeDtypeStruct(recv.shape[1:], F32),
        compiler_params=pltpu.CompilerParams(dimension_semantics=("arbitrary",), vmem_limit_bytes=VMEM_BIG),
    )(place, recv, own)


_ANY = pl.BlockSpec(memory_space=pl.ANY)


def _place():
    x, y, c = lax.axis_index("x"), lax.axis_index("y"), lax.axis_index("c")
    return x, y, c, 2 * x + y


def _chip_peers(x, y):
    out = []
    for k in (1, 2, 3):
        px = 1 - x if k & 2 else x
        py = 1 - y if k & 1 else y
        out.append((k, px, py, 2 * px + py))
    return out


def _half(c, shape, axis):
    n = shape[axis] // 2
    sl = pl.ds(pl.multiple_of(c * n, 16 if axis == 0 else LANE), n)
    return (sl,) if axis == 0 else (pl.ds(0, shape[0]), sl)


def gather_weights(srcs, axes):
    side = gather_side(srcs, axes)
    n = len(srcs)

    def body(*refs):
        ins, outs, sems = refs[:n], refs[n:2 * n], refs[2 * n:]
        side.run(ins, outs, sems)

    return pl.pallas_call(
        body, name="gather_weights", in_specs=[_ANY] * n, out_specs=[_ANY] * n,
        out_shape=side.outs, scratch_shapes=side.sems,
    )(*srcs)


def gather_side(srcs, axes):
    n = len(srcs)

    def copies(src, dst, sems, want):
        ssem, rsem, fssem, frsem = sems
        x, y, c, me = _place()
        sib = (x, y, 1 - c)
        out = []
        for i in range(n):
            mine, other = _half(c, srcs[i].shape, axes[i]), _half(1 - c, srcs[i].shape, axes[i])
            for k, px, py, peer in _chip_peers(x, y):
                sems_k = dict(send_sem=ssem.at[i, k - 1], recv_sem=rsem.at[i, k - 1], device_id=(px, py, c),
                              device_id_type=MESH_IDS)
                fsems = dict(send_sem=fssem.at[i, k - 1], recv_sem=frsem.at[i, k - 1], device_id=sib,
                             device_id_type=MESH_IDS)
                got = dst[i].at[(peer,) + mine]
                mk = pltpu.make_async_remote_copy
                made = dict(
                    snd=lambda: mk(src_ref=src[i].at[mine], dst_ref=dst[i].at[(me,) + mine], **sems_k),
                    rcv=lambda: mk(src_ref=src[i].at[mine], dst_ref=got, **sems_k),
                    fwd=lambda: mk(src_ref=got, dst_ref=got, **fsems),
                    frcv=lambda: mk(src_ref=got, dst_ref=dst[i].at[(peer,) + other], **fsems))
                out.append([made[w]() for w in want])
        return out

    def start(src, dst, sems):
        for (snd,) in copies(src, dst, sems, ("snd",)):
            snd.start()

    def finish(src, dst, sems):
        cps = copies(src, dst, sems, ("snd", "rcv", "fwd", "frcv"))
        for _, rcv, fwd, _ in cps:
            rcv.wait_recv()
            fwd.start()
        for snd, _, fwd, frcv in cps:
            frcv.wait_recv()
            snd.wait_send()
            fwd.wait_send()

    return Side(list(srcs), [jax.ShapeDtypeStruct((4,) + s.shape, s.dtype) for s in srcs],
                [pltpu.SemaphoreType.DMA((n, 3))] * 4, start, finish)


def pair_exchange(name, srcs, axes):
    n = len(srcs)

    def half_shape(s, axis):
        return (4, s.shape[1] // 2, s.shape[2]) if axis == 0 else (4, s.shape[1], s.shape[2] // 2)

    def body(*refs):
        src, other = refs[:n], refs[n:2 * n]
        ssem, rsem = refs[2 * n:]
        x, y, c, _ = _place()
        cps = []
        for i in range(n):
            idx = (pl.ds(0, 4),) + _half(1 - c, srcs[i].shape[1:], axes[i])
            cps.append(pltpu.make_async_remote_copy(
                src_ref=src[i].at[idx], dst_ref=other[i], send_sem=ssem.at[i], recv_sem=rsem.at[i],
                device_id=(x, y, 1 - c), device_id_type=MESH_IDS))
            cps[-1].start()
        for cp in cps:
            cp.wait()

    return pl.pallas_call(
        body, name=name, in_specs=[_ANY] * n, out_specs=[_ANY] * n,
        out_shape=[jax.ShapeDtypeStruct(half_shape(s, a), s.dtype) for s, a in zip(srcs, axes)],
        scratch_shapes=[pltpu.SemaphoreType.DMA((n,))] * 2,
    )(*srcs)


def scatter_side(srcs):
    n = len(srcs)

    def copies(src, dst, sems, sends_only=False):
        ssem, rsem = sems
        x, y, c, me = _place()
        out = []
        for i in range(n):
            for k, px, py, peer in _chip_peers(x, y):
                sems_k = dict(send_sem=ssem.at[i, k - 1], recv_sem=rsem.at[i, k - 1], device_id=(px, py, c),
                              device_id_type=MESH_IDS)
                snd = pltpu.make_async_remote_copy(src_ref=src[i].at[peer], dst_ref=dst[i].at[me], **sems_k)
                if sends_only:
                    out.append(snd)
                    continue
                out.append((snd, pltpu.make_async_remote_copy(src_ref=src[i].at[peer], dst_ref=dst[i].at[peer],
                                                              **sems_k)))
        return out

    def start(src, dst, sems):
        for snd in copies(src, dst, sems, sends_only=True):
            snd.start()

    def finish(src, dst, sems):
        for snd, rcv in copies(src, dst, sems):
            rcv.wait_recv()
            snd.wait_send()

    return Side(list(srcs), [jax.ShapeDtypeStruct(s.shape, s.dtype) for s in srcs],
                [pltpu.SemaphoreType.DMA((n, 3))] * 2, start, finish)


_HBM = pl.BlockSpec(memory_space=pltpu.HBM)
_SEM = pl.BlockSpec(memory_space=pltpu.SEMAPHORE)
_DATAFLOW = pltpu.SideEffectType.DATAFLOW_SIDE_EFFECTING


def scatter_start(name, srcs):
    n = len(srcs)
    side = scatter_side(srcs)
    ns = 3 * n

    def body(*refs):
        src, land = refs[:n], refs[n:2 * n]
        sems = refs[2 * n:2 * n + 2 * ns]
        side.start(src, land, (_SemGrid(sems[:ns]), _SemGrid(sems[ns:])))
        refs[-1][...] = jnp.zeros_like(refs[-1])

    hbm = [pltpu.HBM(s.shape, s.dtype) for s in srcs]
    res = pl.pallas_call(
        body, name=name,
        out_shape=[pltpu.SemaphoreType.DMA(())] * (2 * ns) + hbm + hbm + [jax.ShapeDtypeStruct((8, LANE), F32)],
        in_specs=[_HBM] * (2 * n),
        out_specs=[_SEM] * (2 * ns) + [_HBM] * (2 * n) + [pl.BlockSpec(memory_space=pltpu.VMEM)],
        input_output_aliases={i: 2 * ns + i for i in range(2 * n)},
        compiler_params=pltpu.CompilerParams(has_side_effects=_DATAFLOW),
    )(*[pltpu.with_memory_space_constraint(s, pltpu.HBM) for s in srcs],
      *[pltpu.with_memory_space_constraint(lax.empty(s.shape, s.dtype), pltpu.HBM) for s in srcs])
    return res[:2 * ns], res[2 * ns:2 * ns + n], res[2 * ns + n:2 * ns + 2 * n], res[-1]


def scatter_wait(name, sems, srcs, lands, after):
    n = len(srcs)
    side = scatter_side(srcs)
    ns = 3 * n

    def body(*refs):
        src, land = refs[:n], refs[n:2 * n]
        s = refs[2 * n:2 * n + 2 * ns]
        side.finish(src, land, (_SemGrid(s[:ns]), _SemGrid(s[ns:])))

    hbm = [pltpu.HBM(s.shape, s.dtype) for s in srcs]
    res = pl.pallas_call(
        body, name=name, out_shape=hbm + hbm,
        in_specs=[_HBM] * (2 * n) + [_SEM] * (2 * ns) + [_ANY], out_specs=[_HBM] * (2 * n),
        input_output_aliases={i: i for i in range(2 * n)},
        compiler_params=pltpu.CompilerParams(has_side_effects=_DATAFLOW),
    )(*srcs, *lands, *sems, after)
    return res[:n], res[n:]


class _SemGrid:
    def __init__(self, sems):
        self.sems = sems

    @property
    def at(self):
        return self

    def __getitem__(self, ik):
        return self.sems[3 * ik[0] + ik[1]]


def swap_halves(name, srcs):
    n = len(srcs)

    def body(*refs):
        src, dst = refs[:n], refs[n:2 * n]
        ssem, rsem = refs[2 * n:]
        x, y, c, _ = _place()
        cps = []
        for i in range(n):
            cps.append(pltpu.make_async_remote_copy(src_ref=src[i], dst_ref=dst[i], send_sem=ssem.at[i],
                                                    recv_sem=rsem.at[i], device_id=(x, y, 1 - c),
                                                    device_id_type=MESH_IDS))
            cps[-1].start()
        for cp in cps:
            cp.wait()

    return pl.pallas_call(
        body, name=name, in_specs=[_ANY] * n, out_specs=[_ANY] * n,
        out_shape=[jax.ShapeDtypeStruct(s.shape, s.dtype) for s in srcs],
        scratch_shapes=[pltpu.SemaphoreType.DMA((n,))] * 2,
    )(*srcs)


def _ag8_copies(src, dst, sems, sends_only=False):
    x, y, c = lax.axis_index("x"), lax.axis_index("y"), lax.axis_index("c")
    me = 4 * x + 2 * y + c
    out = []
    for k in range(1, 8):
        px = 1 - x if k & 4 else x
        py = 1 - y if k & 2 else y
        pc = 1 - c if k & 1 else c
        peer = 4 * px + 2 * py + pc
        out.append(tuple(pltpu.make_async_remote_copy(
            src_ref=src, dst_ref=dst.at[slot], send_sem=sems[k - 1], recv_sem=sems[7 + k - 1],
            device_id=(px, py, pc), device_id_type=MESH_IDS) for slot in ((me,) if sends_only else (me, peer))))
    return out


def allgather8_start(name, src):
    def body(src_ref, land_ref, *rest):
        for (snd,) in _ag8_copies(src_ref, land_ref, rest[:14], sends_only=True):
            snd.start()

    land = jax.ShapeDtypeStruct((8,) + src.shape, src.dtype)
    res = pl.pallas_call(
        body, name=name,
        out_shape=[pltpu.SemaphoreType.DMA(())] * 14 + [pltpu.HBM(src.shape, src.dtype), pltpu.HBM(land.shape, land.dtype)],
        in_specs=[_HBM, _HBM], out_specs=[_SEM] * 14 + [_HBM, _HBM],
        input_output_aliases={0: 14, 1: 15},
        compiler_params=pltpu.CompilerParams(has_side_effects=_DATAFLOW),
    )(pltpu.with_memory_space_constraint(src, pltpu.HBM),
      pltpu.with_memory_space_constraint(lax.empty(land.shape, land.dtype), pltpu.HBM))
    return res[:14], res[14], res[15]


def allgather8_wait(name, sems, src, land, after):
    def body(src_ref, land_ref, *rest):
        for snd, rcv in _ag8_copies(src_ref, land_ref, rest[:14]):
            rcv.wait_recv()
            snd.wait_send()

    return pl.pallas_call(
        body, name=name, out_shape=[pltpu.HBM(src.shape, src.dtype), pltpu.HBM(land.shape, land.dtype)],
        in_specs=[_HBM, _HBM] + [_SEM] * 14 + [_ANY], out_specs=[_HBM, _HBM],
        input_output_aliases={0: 0, 1: 1},
        compiler_params=pltpu.CompilerParams(has_side_effects=_DATAFLOW),
    )(src, land, *sems, after)


WEIGHTS = ['g_pre', 'w_in', 'mla_q_norm', 'mla_wq_b', 'mla_kv_norm', 'mla_wkv_b', 'rwkv_mu', 'rwkv_w0_f', 'rwkv_w2_f',
           'rwkv_w0_b', 'rwkv_w2_b', 'rwkv_a0_f', 'rwkv_a2_f', 'rwkv_a0_b', 'rwkv_a2_b', 'rwkv_k_k', 'rwkv_k_a',
           'rwkv_r_k', 'rwkv_gn_g', 'rwkv_gn_b', 'w_br_mla', 'w_br_rwkv', 'w_out', 'g_post']
BIG_SHAPES = {'w_in': (D_IN // 4, D), 'mla_wq_b': (Q_RANK, 384), 'mla_wkv_b': (KV_RANK, 512),
              'rwkv_w2_f': (LORA, 256), 'rwkv_w2_b': (LORA, 256), 'rwkv_a2_f': (LORA, 256), 'rwkv_a2_b': (LORA, 256),
              'w_br_mla': (RW, 512), 'w_br_rwkv': (RW, 512), 'w_out': (512, D)}
BIG = list(BIG_SHAPES)
SMALL = [n for n in WEIGHTS if n not in BIG_SHAPES]
SMALL_SHAPES = {'g_pre': (D,), 'mla_q_norm': (Q_RANK,), 'mla_kv_norm': (KV_RANK,), 'rwkv_mu': (3456,),
                'rwkv_w0_f': (RW,), 'rwkv_w0_b': (RW,), 'rwkv_a0_f': (RW,), 'rwkv_a0_b': (RW,), 'rwkv_k_k': (RW,),
                'rwkv_k_a': (RW,), 'rwkv_r_k': (RH, RN), 'rwkv_gn_g': (RW,), 'rwkv_gn_b': (RW,), 'g_post': (D,)}
SMALL_LEN = sum(int(np.prod(s)) for s in SMALL_SHAPES.values())
SMALL_ROWS = 144


UNITS = [('w_in',), ('mla_wq_b',), ('mla_wkv_b',), ('rwkv_w2_f', 'rwkv_w2_b', 'rwkv_a2_f', 'rwkv_a2_b'),
         ('w_br_mla', 'w_br_rwkv'), ('w_out',)]
UNIT_AXIS = [1, 0, 0, 0, 0, 0]
ROW_SHARDED = ('w_in', 'w_out')


def _unit_cat(parts):
    return parts[0] if len(parts) == 1 else jnp.concatenate(parts, axis=0)


def _unit_split(arr, names, axis):
    out, o = {}, 0
    for n in names:
        rows = BIG_SHAPES[n][0]
        out[n] = lax.slice_in_dim(arr, o, o + rows, axis=axis)
        o += rows
    return out


def _gathered(units, ag, own, me):
    out = {}
    for names, arr, mine in zip(units, ag, own):
        slots = [jnp.where(me == j, mine, arr[j]) for j in range(4)]
        for n in names:
            parts = [_unit_split(s, names, 0)[n] for s in slots]
            out[n] = jnp.concatenate(parts, axis=0 if n in ROW_SHARDED else 1)
    return out


def _shards(n, g):
    r, w = BIG_SHAPES[n]
    if n in ROW_SHARDED:
        return [g[j * r:(j + 1) * r] for j in range(4)]
    return [g[:, j * w:(j + 1) * w] for j in range(4)]


def _pack_small(d, extra=None):
    flat = jnp.concatenate([d[n].reshape(-1) for n in SMALL] + ([extra.reshape(-1)] if extra is not None else []))
    return jnp.pad(flat, (0, SMALL_ROWS * LANE - flat.shape[0])).reshape(SMALL_ROWS, LANE)


def _unpack_small(packed):
    flat, out, o = packed.reshape(-1), {}, 0
    for n in SMALL:
        sz = int(np.prod(SMALL_SHAPES[n]))
        out[n] = flat[o:o + sz].reshape(SMALL_SHAPES[n])
        o += sz
    return out


def _perm_w_in(gathered, own, me):
    per = D_IN // 4

    def rows(a, b):
        out = []
        while a < b:
            j, lo = divmod(a, per)
            hi = min(b - j * per, per)
            out.append(jnp.where(me == j, own[lo:hi], gathered[j, lo:hi]))
            a = j * per + hi
        return out

    z = lambda n: [jnp.zeros((n, own.shape[1]), own.dtype)]
    lora = []
    for i in range(4):
        lora += rows(4160 + LORA * i, 4160 + LORA * (i + 1)) + z(LANE - LORA)
    return jnp.concatenate(rows(0, 1024) + rows(1088, 4160) + rows(4544, D_IN) + lora + rows(1024, 1088)
                           + z(256 - ROPE), axis=0)


def _unperm_w_in(g):
    lora = [g[OFF_LORA + LANE * i:OFF_LORA + LANE * i + LORA] for i in range(4)]
    return jnp.concatenate([g[0:1024], g[OFF_KR:OFF_KR + ROPE], g[1024:4096]] + lora + [g[4096:OFF_LORA]], axis=0)


def _perm_wq(w):
    w3 = w.reshape(Q_RANK, HEADS, NOPE + ROPE)
    rope = jnp.pad(w3[:, :, NOPE:], ((0, 0), (0, 0), (0, LANE - ROPE)))
    return jnp.concatenate([w3[:, :, :NOPE].reshape(Q_RANK, -1), rope.reshape(Q_RANK, -1)], axis=1)


def _unperm_wq(g):
    return jnp.concatenate([g[:, :1024].reshape(Q_RANK, HEADS, NOPE),
                            g[:, 1024:].reshape(Q_RANK, HEADS, LANE)[:, :, :ROPE]], axis=2).reshape(Q_RANK, -1)


def _perm_wkv(w):
    w3 = w.reshape(KV_RANK, HEADS, NOPE + VDIM)
    return jnp.concatenate([w3[:, :, :NOPE].reshape(KV_RANK, -1), w3[:, :, NOPE:].reshape(KV_RANK, -1)], axis=1)


def _unperm_wkv(g):
    return jnp.concatenate([g[:, :1024].reshape(KV_RANK, HEADS, NOPE), g[:, 1024:].reshape(KV_RANK, HEADS, VDIM)],
                           axis=2).reshape(KV_RANK, -1)


def _pad_rows(w):
    return jnp.pad(w, ((0, LANE - LORA), (0, 0)))


def _perm_mu(mu):
    parts = [mu[:3072]]
    for i in range(4):
        parts += [mu[3072 + LORA * i:3072 + LORA * (i + 1)], jnp.zeros((LANE - LORA,), mu.dtype)]
    return jnp.concatenate(parts).reshape(1, NLERP)


def _unperm_mu(g):
    g = g.reshape(-1)
    return jnp.concatenate([g[:3072]] + [g[3072 + LANE * i:3072 + LANE * i + LORA] for i in range(4)])


def _constants():
    g2 = np.kron(np.eye(2, dtype=np.float32), np.ones((RN, RN), np.float32))
    pos = jnp.arange(T, dtype=F32)
    inv_freq = jnp.power(ROPE_THETA, -jnp.arange(0, ROPE, 2, dtype=F32) / ROPE)
    ang = pos[:, None] * inv_freq[None, :]
    cos, sin, zero = jnp.cos(ang), jnp.sin(ang), jnp.zeros((T, LANE - ROPE), F32)
    cq = jnp.tile(jnp.concatenate([cos, cos, zero], axis=1), (1, HEADS))
    sq = jnp.tile(jnp.concatenate([-sin, sin, zero], axis=1), (1, HEADS))
    return jnp.asarray(g2, BF16), cq, sq


def _step(x, tgt, w, m, v):
    x2, tgt2 = x.reshape(T, D), tgt.reshape(T, D)
    g2, cq, sq = _constants()
    row = lambda n: w[n].reshape(1, -1)
    w, m, v = ({**t, 'w_in': t['w_in'].T} for t in (w, m, v))

    core, chip = lax.axis_index("c"), 2 * lax.axis_index("x") + lax.axis_index("y")
    core1, chip1 = core.astype(jnp.int32).reshape(1), chip.astype(jnp.int32).reshape(1)
    own_bf = [_unit_cat([w[n].astype(BF16) for n in u]) for u in UNITS]
    wp = _perm_w_in(gather_weights(own_bf[:1], UNIT_AXIS[:1])[0], own_bf[0], chip)
    full = {}
    mu_p = _perm_mu(w['rwkv_mu'])

    st_pre = Stage("pre", f_pre, [(D, BF16), (D, None)], 256, [0], [0], [F32])
    st_mla = Stage("mla", f_mla, [(1024, BF16), (1024, BF16), (1024, BF16), (LANE, BF16), (1024, BF16)], 256,
                   [0, 1, 2], [0, 1, 2, 3], [BF16] * 3)
    st_rpre = Stage("rwkv_pre", f_rwkv_pre, [(RW, F32)] * 9, 256, [0], list(range(10)), [F32])
    st_rpost = Stage("rwkv_post", f_rwkv_post, [(RW, BF16)], 256, [0, 2, 3, 4, 5, 6], [0, 1, 2],
                     [F32, F32, F32, F32, F32, BF16])
    st_merge = Stage("merge", f_merge, [(D, BF16)], 256, [0, 1, 2, 3], [], [BF16] * 4)

    pre_rows, pre_par = [(x2, D, 0)], [row('g_pre')]
    (h,) = st_pre.fwd(pre_rows, pre_par)
    proj, rest = matmul("mm_in", h, wp, "nt", side=gather_side(own_bf[1:], UNIT_AXIS[1:]))
    full.update(_gathered(UNITS[1:], rest, own_bf[1:], chip))
    wq, wkv = _perm_wq(full['mla_wq_b']), _perm_wkv(full['mla_wkv_b'])
    lora_w = [_pad_rows(full[n]).astype(F32) for n in ('rwkv_w2_f', 'rwkv_w2_b', 'rwkv_a2_f', 'rwkv_a2_b')]

    mla_rows = [(proj, 512, OFF_QA // 512), (proj, 512, OFF_KVA // 512), (proj, 256, OFF_KR // 256),
                (cq, 1024, 0), (sq, 1024, 0), (cq, LANE, 0), (sq, LANE, 0)]
    mla_par = [row('mla_q_norm'), row('mla_kv_norm'), wq, wkv]
    att = st_mla.fwd(mla_rows, mla_par)
    y_mla, lse, gm = attn_fwd(*att, proj)

    lerp = shift_fwd(proj, mu_p)
    rpre_rows = [(lerp, NLERP, 0)]
    rpre_par = [row('rwkv_w0_f'), row('rwkv_w0_b'), row('rwkv_a0_f'), row('rwkv_a0_b'), row('rwkv_k_k'),
                row('rwkv_k_a')] + lora_w + [g2]
    r_, v_, lwf, lwb, kf, kb, an, bf_, bb_ = st_rpre.fwd(rpre_rows, rpre_par)
    fin = [r_, lwf, kf, v_, an, bf_]
    bin_ = [r_, lwb, kb, v_, an, bb_]
    yf, h0f, uf = scan_fwd("scan_f", *fin, reverse=False)
    yb, h0b, ub = scan_fwd("scan_b", *bin_, reverse=True)
    rpost_rows = [(yf, RW, 0), (yb, RW, 0), (r_, RW, 0), (kf, RW, 0), (kb, RW, 0),
                  (v_, RW, 0), (proj, RW, OFF_ZR // RW)]
    rpost_par = [row('rwkv_gn_g'), row('rwkv_gn_b'), row('rwkv_r_k'), g2]
    (gr,) = st_rpost.fwd(rpost_rows, rpost_par)
    um = matmul("mm_br_mla", gm, full['w_br_mla'], "nn")
    ur = matmul("mm_br_rwkv", gr, full['w_br_rwkv'], "nn")
    merge_rows = [(um, D, 0), (ur, D, 0), (proj, D, OFF_GM // D), (proj, D, OFF_GR // D)]
    (merged,) = st_merge.fwd(merge_rows, [])
    d_out, dy, dg_post, loss_blk = loss_stage(merged, full['w_out'], x2, tgt2, row('g_post'))

    gw = {'g_post': dg_post}
    d_merged = matmul("mm_out_dx", d_out, full['w_out'], "nt")
    gw['w_out'] = matmul("mm_out_dw", merged, d_out, "tn")
    (d_um, d_ur, d_gm, d_gr), _ = st_merge.bwd(merge_rows, [], [[(d_merged, D, 0)]])
    d_gmla = matmul("mm_br_mla_dx", d_um, full['w_br_mla'], "nt")
    gw['w_br_mla'] = matmul("mm_br_mla_dw", gm, d_um, "tn")
    d_grw = matmul("mm_br_rwkv_dx", d_ur, full['w_br_rwkv'], "nt")
    gw['w_br_rwkv'] = matmul("mm_br_rwkv_dw", gr, d_ur, "tn")
    (d_y, d_r3, d_kf2, d_kb2, d_v3, d_zr), (gw['rwkv_gn_g'], gw['rwkv_gn_b'], d_rk) = st_rpost.bwd(
        rpost_rows, rpost_par, [[(d_grw, RW, 0)]])
    gw['rwkv_r_k'] = d_rk
    sf = scan_bwd("scan_f_bwd", *fin, h0f, uf, d_y, reverse=False)
    sb = scan_bwd("scan_b_bwd", *bin_, h0b, ub, d_y, reverse=True)
    c = lambda *ts: [(t, RW, 0) for t in ts]
    rpre_cts = [c(sf[0], sb[0], d_r3), c(sf[3], sb[3], d_v3), c(sf[1]), c(sb[1]), c(sf[2], d_kf2), c(sb[2], d_kb2),
                c(sf[4], sb[4]), c(sf[5]), c(sb[5])]
    (d_rin,), rpre_g = st_rpre.bwd(rpre_rows, rpre_par, rpre_cts)
    for n, gval in zip(('rwkv_w0_f', 'rwkv_w0_b', 'rwkv_a0_f', 'rwkv_a0_b', 'rwkv_k_k', 'rwkv_k_a'), rpre_g[:6]):
        gw[n] = gval
    for n, gval in zip(('rwkv_w2_f', 'rwkv_w2_b', 'rwkv_a2_f', 'rwkv_a2_b'), rpre_g[6:]):
        gw[n] = gval[:LORA]
    d_lerp, d_mu = shift_bwd(proj, mu_p, d_rin)
    gw['rwkv_mu'] = _unperm_mu(d_mu)

    *d_att, d_zm = attn_bwd(*att, lse, y_mla, d_gmla, proj)
    mla_cts = [[(t, t.shape[1], 0)] for t in d_att]
    (d_qa, d_kva, d_kr), (gw['mla_q_norm'], gw['mla_kv_norm'], d_wq, d_wkv) = st_mla.bwd(mla_rows, mla_par, mla_cts)
    gw['mla_wq_b'], gw['mla_wkv_b'] = _unperm_wq(d_wq), _unperm_wkv(d_wkv)

    dproj = jnp.concatenate([d_qa, d_kva, d_lerp[:, :3072], d_zm, d_zr, d_gm, d_gr, d_lerp[:, 3072:], d_kr], axis=1)

    def pair_sums(name, ids):
        send = [jnp.stack([_unit_cat([_shards(n, gw[n])[j].astype(BF16) for n in UNITS[i]]) for j in range(4)])
                for i in ids]
        axes = [UNIT_AXIS[i] for i in ids]
        other = pair_exchange(name, send, axes)
        return [pair_sum(f"pair_sum_{i}", core1, s, o, ax) for i, s, o, ax in zip(ids, send, other, axes)]

    late, early = [0], list(range(1, len(UNITS)))
    pairs_e = pair_sums("pair_exchange_rest", early)
    gw_in, recv_e = matmul("mm_in_dw", dproj, h, "tn", BF16, side=scatter_side(pairs_e))
    gw['w_in'] = _unperm_w_in(gw_in)
    pairs_l = pair_sums("pair_exchange_w_in", late)
    sems, src_fly, land_fly, token = scatter_start("scatter_w_in_start", pairs_l)
    dh = matmul("mm_in_dx", dproj, wp, "nn", after=(token,))
    (grad_x,), (gw['g_pre'],) = st_pre.bwd(pre_rows, pre_par, [[(dh, D, 0)], [(dy, D, 0)]])

    big = [dict() for _ in range(4)]

    def update(name, ids, recv, pairs):
        mine = [sum4(f"sum4_{i}", chip1, r, p, UNIT_AXIS[i]) for i, r, p in zip(ids, recv, pairs)]
        theirs = swap_halves(name, mine)
        for i, mi, th in zip(ids, mine, theirs):
            res = adamw_halves(f"adamw_{i}", core1, *[_unit_cat([t[n] for n in UNITS[i]]) for t in (w, m, v)], mi, th,
                               UNIT_AXIS[i])
            for q in range(4):
                big[q].update(_unit_split(res[q], UNITS[i], 0))
        return res

    small_fly = allgather8_start("gather_small_start", _pack_small(gw, loss_blk[0, :1]))
    last = update("swap_halves_rest", early, recv_e, pairs_e)
    own_small, landed = allgather8_wait("gather_small_wait", *small_fly, last[0])
    dev = 2 * chip + core
    parts = [jnp.where(dev == i, own_small, landed[i]) for i in range(8)]
    small = adamw("adamw_small", _pack_small(w), _pack_small(m), _pack_small(v), parts)
    pairs_l, recv_l = scatter_wait("scatter_w_in_wait", sems, src_fly, land_fly, small[0] + last[0][:1, :1])
    update("swap_halves_w_in", late, recv_l, pairs_l)

    outs = []
    for b_d, s_arr in zip(big, small):
        d = {**b_d, **_unpack_small(s_arr)}
        d['w_in'] = d['w_in'].T
        outs.append([d[n] for n in WEIGHTS])
    loss = small[0][SMALL_LEN // LANE, 0]
    return (loss, grad_x.reshape(1, T, D), *outs[0], *outs[1], *outs[2], *outs[3])


def kernel(x, g_pre, w_in, mla_q_norm, mla_wq_b, mla_kv_norm, mla_wkv_b, rwkv_mu, rwkv_w0_f, rwkv_w2_f, rwkv_w0_b, rwkv_w2_b, rwkv_a0_f, rwkv_a2_f, rwkv_a0_b, rwkv_a2_b, rwkv_k_k, rwkv_k_a, rwkv_r_k, rwkv_gn_g, rwkv_gn_b, w_br_mla, w_br_rwkv, w_out, g_post, loss_target, m_g_pre, m_w_in, m_mla_q_norm, m_mla_wq_b, m_mla_kv_norm, m_mla_wkv_b, m_rwkv_mu, m_rwkv_w0_f, m_rwkv_w2_f, m_rwkv_w0_b, m_rwkv_w2_b, m_rwkv_a0_f, m_rwkv_a2_f, m_rwkv_a0_b, m_rwkv_a2_b, m_rwkv_k_k, m_rwkv_k_a, m_rwkv_r_k, m_rwkv_gn_g, m_rwkv_gn_b, m_w_br_mla, m_w_br_rwkv, m_w_out, m_g_post, v_g_pre, v_w_in, v_mla_q_norm, v_mla_wq_b, v_mla_kv_norm, v_mla_wkv_b, v_rwkv_mu, v_rwkv_w0_f, v_rwkv_w2_f, v_rwkv_w0_b, v_rwkv_w2_b, v_rwkv_a0_f, v_rwkv_a2_f, v_rwkv_a0_b, v_rwkv_a2_b, v_rwkv_k_k, v_rwkv_k_a, v_rwkv_r_k, v_rwkv_gn_g, v_rwkv_gn_b, v_w_br_mla, v_w_br_rwkv, v_w_out, v_g_post):
    given = dict(locals())
    w = {n: given[n] for n in WEIGHTS}
    m = {n: given['m_' + n] for n in WEIGHTS}
    v = {n: given['v_' + n] for n in WEIGHTS}
    return _step(x, loss_target, w, m, v)
```
